```python
import math
import jax, jax.numpy as jnp
from jax import lax
import numpy as np

D_MODEL = 2048
BATCH = 8
SEQ = 4096
DEPTH = 4

N_BRANCH = 3
W_MIX = D_MODEL // 2
S5_GROUP = 16
S5_GROUPS = W_MIX // S5_GROUP
S5_STATE = 64
S5_DT_MIN = 1e-3
S5_DT_MAX = 1e-1
RG_BLOCKS = 16
RG_BLOCK = W_MIX // RG_BLOCKS
RG_CONV = 4
RG_C = 8.0
HG_HEADS = 8
HG_DK = W_MIX // HG_HEADS
HG_DV = W_MIX // HG_HEADS
HG_CHUNK = 64
EPS = 1e-6

N_IN = 8 * W_MIX + N_BRANCH * D_MODEL
SPLITS = tuple(W_MIX * k for k in range(1, 9))

kernel_name = "hybrid_s5_rglru_hgrn2_gated_merge"


def rms_norm(x, w):
    x32 = x.astype(jnp.float32)
    y = x32 * lax.rsqrt(jnp.mean(x32 * x32, axis=-1, keepdims=True) + EPS)
    return (y * w.astype(jnp.float32)).astype(x.dtype)


def s5_mixer(u, lam_re, lam_im, log_step, b_re, b_im, c_re, c_im, d, w_glu, b_glu):
    f32 = jnp.float32
    bsz, s, _ = u.shape
    u32 = u.astype(f32).reshape(bsz, s, S5_GROUPS, S5_GROUP)
    lam_re = lam_re.astype(f32)
    lam_im = lam_im.astype(f32)
    step = jnp.exp(log_step.astype(f32))[:, None]
    mag = jnp.exp(lam_re * step)
    ang = lam_im * step
    abar_re = mag * jnp.cos(ang)
    abar_im = mag * jnp.sin(ang)
    num_re = abar_re - 1.0
    num_im = abar_im
    den = lam_re * lam_re + lam_im * lam_im
    coef_re = (num_re * lam_re + num_im * lam_im) / den
    coef_im = (num_im * lam_re - num_re * lam_im) / den
    b_re = b_re.astype(f32)
    b_im = b_im.astype(f32)
    bbar_re = coef_re[..., None] * b_re - coef_im[..., None] * b_im
    bbar_im = coef_re[..., None] * b_im + coef_im[..., None] * b_re
    bu_re = jnp.einsum('bsgh,gph->bsgp', u32, bbar_re)
    bu_im = jnp.einsum('bsgh,gph->bsgp', u32, bbar_im)
    a_re = jnp.broadcast_to(abar_re, (1, s) + abar_re.shape)
    a_im = jnp.broadcast_to(abar_im, (1, s) + abar_im.shape)

    def combine(left, right):
        ar1, ai1, br1, bi1 = left
        ar2, ai2, br2, bi2 = right
        return (ar2 * ar1 - ai2 * ai1,
                ar2 * ai1 + ai2 * ar1,
                ar2 * br1 - ai2 * bi1 + br2,
                ar2 * bi1 + ai2 * br1 + bi2)

    _, _, x_re, x_im = lax.associative_scan(combine, (a_re, a_im, bu_re, bu_im), axis=1)
    y = (jnp.einsum('bsgp,ghp->bsgh', x_re, c_re.astype(f32))
         - jnp.einsum('bsgp,ghp->bsgh', x_im, c_im.astype(f32)))
    y = y.reshape(bsz, s, W_MIX) + d.astype(f32) * u32.reshape(bsz, s, W_MIX)
    y = jax.nn.gelu(y)
    y = y * jax.nn.sigmoid(y @ w_glu.astype(f32) + b_glu.astype(f32))
    return y.astype(u.dtype)


def rg_lru_mixer(x, conv_w, conv_b, w_a, b_a, w_x, b_x, lam):
    f32 = jnp.float32
    bsz, s, _ = x.shape
    x32 = x.astype(f32)
    xc = lax.conv_general_dilated(
        x32, conv_w.astype(f32).reshape(RG_CONV, 1, W_MIX),
        window_strides=(1,), padding=[(RG_CONV - 1, 0)],
        dimension_numbers=('NWC', 'WIO', 'NWC'),
        feature_group_count=W_MIX) + conv_b.astype(f32)
    xb = xc.reshape(bsz, s, RG_BLOCKS, RG_BLOCK)
    r = jax.nn.sigmoid(jnp.einsum('bsni,nij->bsnj', xb, w_a.astype(f32)).reshape(bsz, s, W_MIX)
                       + b_a.astype(f32))
    i = jax.nn.sigmoid(jnp.einsum('bsni,nij->bsnj', xb, w_x.astype(f32)).reshape(bsz, s, W_MIX)
                       + b_x.astype(f32))
    log_a = -RG_C * r * jax.nn.softplus(-lam.astype(f32))
    a = jnp.exp(log_a)
    mult = jnp.sqrt(-jnp.expm1(2.0 * log_a))
    is_first = (jnp.arange(s) == 0)[None, :, None]
    mult = jnp.where(is_first, jnp.ones_like(mult), mult)
    b_term = mult * (i * xc)

    def combine(left, right):
        a1, b1 = left
        a2, b2 = right
        return a2 * a1, a2 * b1 + b2

    _, h = lax.associative_scan(combine, (a, b_term), axis=1)
    return h.astype(x.dtype)


def hgrn2_mixer(q, f_logit, inp, lb, norm_w):
    f32 = jnp.float32
    bsz, s, _ = q.shape
    n_chunks = s // HG_CHUNK
    q32 = jax.nn.silu(q.astype(f32))
    f = lb + (1.0 - lb) * jax.nn.sigmoid(f_logit.astype(f32))
    k32 = 1.0 - f
    g32 = jnp.log(f)
    v32 = inp.astype(f32)

    def to_chunks(t, d):
        return t.reshape(bsz, n_chunks, HG_CHUNK, HG_HEADS, d).transpose(1, 0, 3, 2, 4)

    qc = to_chunks(q32, HG_DK)
    kc = to_chunks(k32, HG_DK)
    gc = to_chunks(g32, HG_DK)
    vc = to_chunks(v32, HG_DV)
    causal = jnp.tril(jnp.ones((HG_CHUNK, HG_CHUNK), dtype=bool))

    def chunk_step(state, xs):
        qx, kx, gx, vx = xs
        G = jnp.cumsum(gx, axis=2)
        inter = jnp.einsum('bhtk,bhkv->bhtv', qx * jnp.exp(G), state)
        diff = G[:, :, :, None, :] - G[:, :, None, :, :]
        decay = jnp.exp(jnp.where(causal[None, None, :, :, None], diff, -jnp.inf))
        attn = jnp.einsum('bhtk,bhsk,bhtsk->bhts', qx, kx, decay)
        intra = jnp.einsum('bhts,bhsv->bhtv', attn, vx)
        g_last = G[:, :, -1]
        k_dec = kx * jnp.exp(g_last[:, :, None, :] - G)
        new_state = (jnp.exp(g_last)[..., None] * state
                     + jnp.einsum('bhsk,bhsv->bhkv', k_dec, vx))
        return new_state, inter + intra

    state0 = jnp.zeros((bsz, HG_HEADS, HG_DK, HG_DV), f32)
    _, o = lax.scan(chunk_step, state0, (qc, kc, gc, vc))
    o = o.transpose(1, 0, 3, 2, 4).reshape(bsz, s, HG_HEADS, HG_DV)
    o = o * lax.rsqrt(jnp.mean(o * o, axis=-1, keepdims=True) + EPS)
    o = o * norm_w.astype(f32).reshape(HG_HEADS, HG_DV)
    return o.reshape(bsz, s, W_MIX).astype(q.dtype)


def _fwd_setup_inputs(seed: int = 0) -> dict:
    key = jax.random.key(seed)
    ks = jax.random.split(key, 26)
    f32 = jnp.float32
    nrm = lambda k, shape, sc: sc * jax.random.normal(k, shape, f32)
    x = jax.random.normal(ks[0], (BATCH, SEQ, D_MODEL), f32)
    norm_w = 1.0 + nrm(ks[1], (DEPTH, D_MODEL), 0.02)
    w_in = nrm(ks[2], (DEPTH, D_MODEL, N_IN), D_MODEL ** -0.5)
    s5_lambda_re = -0.5 + nrm(ks[3], (DEPTH, S5_GROUPS, S5_STATE), 0.01)
    s5_lambda_im = (math.pi * jnp.arange(S5_STATE, dtype=f32)[None, None, :]
                    + nrm(ks[4], (DEPTH, S5_GROUPS, S5_STATE), 0.01))
    s5_log_step = jax.random.uniform(ks[5], (DEPTH, S5_GROUPS), f32,
                                     math.log(S5_DT_MIN), math.log(S5_DT_MAX))
    s5_b_re = nrm(ks[6], (DEPTH, S5_GROUPS, S5_STATE, S5_GROUP), (2 * S5_GROUP) ** -0.5)
    s5_b_im = nrm(ks[7], (DEPTH, S5_GROUPS, S5_STATE, S5_GROUP), (2 * S5_GROUP) ** -0.5)
    s5_c_re = nrm(ks[8], (DEPTH, S5_GROUPS, S5_GROUP, S5_STATE), (2 * S5_STATE) ** -0.5)
    s5_c_im = nrm(ks[9], (DEPTH, S5_GROUPS, S5_GROUP, S5_STATE), (2 * S5_STATE) ** -0.5)
    s5_d = nrm(ks[10], (DEPTH, W_MIX), 1.0)
    s5_w_glu = nrm(ks[11], (DEPTH, W_MIX, W_MIX), W_MIX ** -0.5)
    s5_b_glu = nrm(ks[12], (DEPTH, W_MIX), 0.01)
    rg_conv_w = nrm(ks[13], (DEPTH, RG_CONV, W_MIX), RG_CONV ** -0.5)
    rg_conv_b = nrm(ks[14], (DEPTH, W_MIX), 0.01)
    rg_w_a = nrm(ks[15], (DEPTH, RG_BLOCKS, RG_BLOCK, RG_BLOCK), RG_BLOCK ** -0.5)
    rg_b_a = nrm(ks[16], (DEPTH, W_MIX), 0.01)
    rg_w_x = nrm(ks[17], (DEPTH, RG_BLOCKS, RG_BLOCK, RG_BLOCK), RG_BLOCK ** -0.5)
    rg_b_x = nrm(ks[18], (DEPTH, W_MIX), 0.01)
    a0 = jax.random.uniform(ks[19], (DEPTH, W_MIX), f32, 0.9, 0.999)
    p = a0 ** (1.0 / RG_C)
    rg_lambda = jnp.log(p) - jnp.log1p(-p)
    hg_lower_bounds = nrm(ks[20], (DEPTH, W_MIX), 0.1)
    hg_norm_w = 1.0 + nrm(ks[21], (DEPTH, W_MIX), 0.02)
    w_branch = nrm(ks[22], (DEPTH, N_BRANCH, W_MIX, D_MODEL), W_MIX ** -0.5)
    w_out = nrm(ks[23], (DEPTH, D_MODEL, D_MODEL), D_MODEL ** -0.5)
    final_norm_w = 1.0 + nrm(ks[24], (D_MODEL,), 0.02)
    return {"x": x, "norm_w": norm_w, "w_in": w_in,
            "s5_lambda_re": s5_lambda_re, "s5_lambda_im": s5_lambda_im,
            "s5_log_step": s5_log_step, "s5_b_re": s5_b_re, "s5_b_im": s5_b_im,
            "s5_c_re": s5_c_re, "s5_c_im": s5_c_im, "s5_d": s5_d,
            "s5_w_glu": s5_w_glu, "s5_b_glu": s5_b_glu,
            "rg_conv_w": rg_conv_w, "rg_conv_b": rg_conv_b,
            "rg_w_a": rg_w_a, "rg_b_a": rg_b_a, "rg_w_x": rg_w_x, "rg_b_x": rg_b_x,
            "rg_lambda": rg_lambda,
            "hg_lower_bounds": hg_lower_bounds, "hg_norm_w": hg_norm_w,
            "w_branch": w_branch, "w_out": w_out, "final_norm_w": final_norm_w}


def _fwd_reference(x, norm_w, w_in, s5_lambda_re, s5_lambda_im, s5_log_step, s5_b_re, s5_b_im,
              s5_c_re, s5_c_im, s5_d, s5_w_glu, s5_b_glu, rg_conv_w, rg_conv_b,
              rg_w_a, rg_b_a, rg_w_x, rg_b_x, rg_lambda, hg_lower_bounds, hg_norm_w,
              w_branch, w_out, final_norm_w):
    bsz, s, _ = x.shape
    lb_sm = jax.nn.softmax(hg_lower_bounds.astype(jnp.float32), axis=0)
    lbs = jnp.cumsum(lb_sm, axis=0) - lb_sm[0]
    for l in range(DEPTH):
        h = rms_norm(x, norm_w[l])
        z = h @ w_in[l]
        u_a, g_a, x_b, g_b, q_c, f_c, i_c, g_c, gate_logits = jnp.split(z, SPLITS, axis=-1)
        y_a = s5_mixer(u_a, s5_lambda_re[l], s5_lambda_im[l], s5_log_step[l],
                       s5_b_re[l], s5_b_im[l], s5_c_re[l], s5_c_im[l], s5_d[l],
                       s5_w_glu[l], s5_b_glu[l]) * jax.nn.silu(g_a)
        y_b = rg_lru_mixer(x_b, rg_conv_w[l], rg_conv_b[l], rg_w_a[l], rg_b_a[l],
                           rg_w_x[l], rg_b_x[l], rg_lambda[l]) * jax.nn.silu(g_b)
        y_c = hgrn2_mixer(q_c, f_c, i_c, lbs[l], hg_norm_w[l]) * jax.nn.silu(g_c)
        ys = jnp.stack([y_a, y_b, y_c], axis=2)
        branch = jnp.einsum('bsnw,nwd->bsnd', ys, w_branch[l])
        gates = jax.nn.sigmoid(gate_logits.reshape(bsz, s, N_BRANCH, D_MODEL))
        merged = jnp.sum(gates * branch, axis=2)
        x = x + merged @ w_out[l]
    return rms_norm(x, final_norm_w)


import jax as _jax
import jax.numpy as _jnp

TWIN_FORMAT = 'train_step'
FWD_PARAMS = ['x', 'norm_w', 'w_in', 's5_lambda_re', 's5_lambda_im', 's5_log_step', 's5_b_re', 's5_b_im', 's5_c_re', 's5_c_im', 's5_d', 's5_w_glu', 's5_b_glu', 'rg_conv_w', 'rg_conv_b', 'rg_w_a', 'rg_b_a', 'rg_w_x', 'rg_b_x', 'rg_lambda', 'hg_lower_bounds', 'hg_norm_w', 'w_branch', 'w_out', 'final_norm_w']
TWIN_WEIGHTS = ['norm_w', 'w_in', 's5_lambda_re', 's5_lambda_im', 's5_log_step', 's5_b_re', 's5_b_im', 's5_c_re', 's5_c_im', 's5_d', 's5_w_glu', 's5_b_glu', 'rg_conv_w', 'rg_conv_b', 'rg_w_a', 'rg_b_a', 'rg_w_x', 'rg_b_x', 'rg_lambda', 'hg_lower_bounds', 'hg_norm_w', 'w_branch', 'w_out', 'final_norm_w']
TWIN_DIFF_INPUT = 'x'
TWIN_INPUTS = ['x', 'norm_w', 'w_in', 's5_lambda_re', 's5_lambda_im', 's5_log_step', 's5_b_re', 's5_b_im', 's5_c_re', 's5_c_im', 's5_d', 's5_w_glu', 's5_b_glu', 'rg_conv_w', 'rg_conv_b', 'rg_w_a', 'rg_b_a', 'rg_w_x', 'rg_b_x', 'rg_lambda', 'hg_lower_bounds', 'hg_norm_w', 'w_branch', 'w_out', 'final_norm_w', 'loss_target', 'm_norm_w', 'm_w_in', 'm_s5_lambda_re', 'm_s5_lambda_im', 'm_s5_log_step', 'm_s5_b_re', 'm_s5_b_im', 'm_s5_c_re', 'm_s5_c_im', 'm_s5_d', 'm_s5_w_glu', 'm_s5_b_glu', 'm_rg_conv_w', 'm_rg_conv_b', 'm_rg_w_a', 'm_rg_b_a', 'm_rg_w_x', 'm_rg_b_x', 'm_rg_lambda', 'm_hg_lower_bounds', 'm_hg_norm_w', 'm_w_branch', 'm_w_out', 'm_final_norm_w', 'v_norm_w', 'v_w_in', 'v_s5_lambda_re', 'v_s5_lambda_im', 'v_s5_log_step', 'v_s5_b_re', 'v_s5_b_im', 'v_s5_c_re', 'v_s5_c_im', 'v_s5_d', 'v_s5_w_glu', 'v_s5_b_glu', 'v_rg_conv_w', 'v_rg_conv_b', 'v_rg_w_a', 'v_rg_b_a', 'v_rg_w_x', 'v_rg_b_x', 'v_rg_lambda', 'v_hg_lower_bounds', 'v_hg_norm_w', 'v_w_branch', 'v_w_out', 'v_final_norm_w']
TWIN_OUTPUTS = ['loss', 'grad_x', 'grad_norm_w', 'grad_w_in', 'grad_s5_lambda_re', 'grad_s5_lambda_im', 'grad_s5_log_step', 'grad_s5_b_re', 'grad_s5_b_im', 'grad_s5_c_re', 'grad_s5_c_im', 'grad_s5_d', 'grad_s5_w_glu', 'grad_s5_b_glu', 'grad_rg_conv_w', 'grad_rg_conv_b', 'grad_rg_w_a', 'grad_rg_b_a', 'grad_rg_w_x', 'grad_rg_b_x', 'grad_rg_lambda', 'grad_hg_lower_bounds', 'grad_hg_norm_w', 'grad_w_branch', 'grad_w_out', 'grad_final_norm_w', 'delta_norm_w', 'delta_w_in', 'delta_s5_lambda_re', 'delta_s5_lambda_im', 'delta_s5_log_step', 'delta_s5_b_re', 'delta_s5_b_im', 'delta_s5_c_re', 'delta_s5_c_im', 'delta_s5_d', 'delta_s5_w_glu', 'delta_s5_b_glu', 'delta_rg_conv_w', 'delta_rg_conv_b', 'delta_rg_w_a', 'delta_rg_b_a', 'delta_rg_w_x', 'delta_rg_b_x', 'delta_rg_lambda', 'delta_hg_lower_bounds', 'delta_hg_norm_w', 'delta_w_branch', 'delta_w_out', 'delta_final_norm_w', 'new_m_norm_w', 'new_m_w_in', 'new_m_s5_lambda_re', 'new_m_s5_lambda_im', 'new_m_s5_log_step', 'new_m_s5_b_re', 'new_m_s5_b_im', 'new_m_s5_c_re', 'new_m_s5_c_im', 'new_m_s5_d', 'new_m_s5_w_glu', 'new_m_s5_b_glu', 'new_m_rg_conv_w', 'new_m_rg_conv_b', 'new_m_rg_w_a', 'new_m_rg_b_a', 'new_m_rg_w_x', 'new_m_rg_b_x', 'new_m_rg_lambda', 'new_m_hg_lower_bounds', 'new_m_hg_norm_w', 'new_m_w_branch', 'new_m_w_out', 'new_m_final_norm_w', 'new_v_norm_w', 'new_v_w_in', 'new_v_s5_lambda_re', 'new_v_s5_lambda_im', 'new_v_s5_log_step', 'new_v_s5_b_re', 'new_v_s5_b_im', 'new_v_s5_c_re', 'new_v_s5_c_im', 'new_v_s5_d', 'new_v_s5_w_glu', 'new_v_s5_b_glu', 'new_v_rg_conv_w', 'new_v_rg_conv_b', 'new_v_rg_w_a', 'new_v_rg_b_a', 'new_v_rg_w_x', 'new_v_rg_b_x', 'new_v_rg_lambda', 'new_v_hg_lower_bounds', 'new_v_hg_norm_w', 'new_v_w_branch', 'new_v_w_out', 'new_v_final_norm_w']
TWIN_LEAF_KINDS = {'loss': 'loss', 'grad_x': 'grad_x', 'grad_norm_w': 'grad_w', 'grad_w_in': 'grad_w', 'grad_s5_lambda_re': 'grad_w', 'grad_s5_lambda_im': 'grad_w', 'grad_s5_log_step': 'grad_w', 'grad_s5_b_re': 'grad_w', 'grad_s5_b_im': 'grad_w', 'grad_s5_c_re': 'grad_w', 'grad_s5_c_im': 'grad_w', 'grad_s5_d': 'grad_w', 'grad_s5_w_glu': 'grad_w', 'grad_s5_b_glu': 'grad_w', 'grad_rg_conv_w': 'grad_w', 'grad_rg_conv_b': 'grad_w', 'grad_rg_w_a': 'grad_w', 'grad_rg_b_a': 'grad_w', 'grad_rg_w_x': 'grad_w', 'grad_rg_b_x': 'grad_w', 'grad_rg_lambda': 'grad_w', 'grad_hg_lower_bounds': 'grad_w', 'grad_hg_norm_w': 'grad_w', 'grad_w_branch': 'grad_w', 'grad_w_out': 'grad_w', 'grad_final_norm_w': 'grad_w', 'delta_norm_w': 'delta_w', 'delta_w_in': 'delta_w', 'delta_s5_lambda_re': 'delta_w', 'delta_s5_lambda_im': 'delta_w', 'delta_s5_log_step': 'delta_w', 'delta_s5_b_re': 'delta_w', 'delta_s5_b_im': 'delta_w', 'delta_s5_c_re': 'delta_w', 'delta_s5_c_im': 'delta_w', 'delta_s5_d': 'delta_w', 'delta_s5_w_glu': 'delta_w', 'delta_s5_b_glu': 'delta_w', 'delta_rg_conv_w': 'delta_w', 'delta_rg_conv_b': 'delta_w', 'delta_rg_w_a': 'delta_w', 'delta_rg_b_a': 'delta_w', 'delta_rg_w_x': 'delta_w', 'delta_rg_b_x': 'delta_w', 'delta_rg_lambda': 'delta_w', 'delta_hg_lower_bounds': 'delta_w', 'delta_hg_norm_w': 'delta_w', 'delta_w_branch': 'delta_w', 'delta_w_out': 'delta_w', 'delta_final_norm_w': 'delta_w', 'new_m_norm_w': 'new_m', 'new_m_w_in': 'new_m', 'new_m_s5_lambda_re': 'new_m', 'new_m_s5_lambda_im': 'new_m', 'new_m_s5_log_step': 'new_m', 'new_m_s5_b_re': 'new_m', 'new_m_s5_b_im': 'new_m', 'new_m_s5_c_re': 'new_m', 'new_m_s5_c_im': 'new_m', 'new_m_s5_d': 'new_m', 'new_m_s5_w_glu': 'new_m', 'new_m_s5_b_glu': 'new_m', 'new_m_rg_conv_w': 'new_m', 'new_m_rg_conv_b': 'new_m', 'new_m_rg_w_a': 'new_m', 'new_m_rg_b_a': 'new_m', 'new_m_rg_w_x': 'new_m', 'new_m_rg_b_x': 'new_m', 'new_m_rg_lambda': 'new_m', 'new_m_hg_lower_bounds': 'new_m', 'new_m_hg_norm_w': 'new_m', 'new_m_w_branch': 'new_m', 'new_m_w_out': 'new_m', 'new_m_final_norm_w': 'new_m', 'new_v_norm_w': 'new_v', 'new_v_w_in': 'new_v', 'new_v_s5_lambda_re': 'new_v', 'new_v_s5_lambda_im': 'new_v', 'new_v_s5_log_step': 'new_v', 'new_v_s5_b_re': 'new_v', 'new_v_s5_b_im': 'new_v', 'new_v_s5_c_re': 'new_v', 'new_v_s5_c_im': 'new_v', 'new_v_s5_d': 'new_v', 'new_v_s5_w_glu': 'new_v', 'new_v_s5_b_glu': 'new_v', 'new_v_rg_conv_w': 'new_v', 'new_v_rg_conv_b': 'new_v', 'new_v_rg_w_a': 'new_v', 'new_v_rg_b_a': 'new_v', 'new_v_rg_w_x': 'new_v', 'new_v_rg_b_x': 'new_v', 'new_v_rg_lambda': 'new_v', 'new_v_hg_lower_bounds': 'new_v', 'new_v_hg_norm_w': 'new_v', 'new_v_w_branch': 'new_v', 'new_v_w_out': 'new_v', 'new_v_final_norm_w': 'new_v'}


def _forward(args):
    return _fwd_reference(*[args[k] for k in FWD_PARAMS])


def _output_shape():
    def fwd():
        inp = _fwd_setup_inputs(0)
        return _fwd_reference(*[inp[k] for k in FWD_PARAMS])
    out = _jax.eval_shape(fwd)
    return out.shape, out.dtype

N_MICROBATCH = 1
ADAM_LR = 0.001
ADAM_B1 = 0.9
ADAM_B2 = 0.999
ADAM_EPS = 1e-08
ADAM_WD = 0.01
ADAM_STEP = 10
PER_EXAMPLE_BATCH_AXIS = {'x': 0, 'loss_target': 0}
SHARED_INPUTS = []
_WEIGHT_DTYPES = {'norm_w': _jnp.float32, 'w_in': _jnp.float32, 's5_lambda_re': _jnp.float32, 's5_lambda_im': _jnp.float32, 's5_log_step': _jnp.float32, 's5_b_re': _jnp.float32, 's5_b_im': _jnp.float32, 's5_c_re': _jnp.float32, 's5_c_im': _jnp.float32, 's5_d': _jnp.float32, 's5_w_glu': _jnp.float32, 's5_b_glu': _jnp.float32, 'rg_conv_w': _jnp.float32, 'rg_conv_b': _jnp.float32, 'rg_w_a': _jnp.float32, 'rg_b_a': _jnp.float32, 'rg_w_x': _jnp.float32, 'rg_b_x': _jnp.float32, 'rg_lambda': _jnp.float32, 'hg_lower_bounds': _jnp.float32, 'hg_norm_w': _jnp.float32, 'w_branch': _jnp.float32, 'w_out': _jnp.float32, 'final_norm_w': _jnp.float32}
MOMENT_SCALE = {'norm_w': 5.188455e-02, 'w_in': 2.002152e-02, 's5_lambda_re': 6.862911e-04, 's5_lambda_im': 7.193098e-04, 's5_log_step': 4.610816e-01, 's5_b_re': 4.578739e-04, 's5_b_im': 4.571106e-04, 's5_c_re': 9.099459e-04, 's5_c_im': 9.299700e-04, 's5_d': 1.474116e-02, 's5_w_glu': 3.954524e-03, 's5_b_glu': 6.107628e-03, 'rg_conv_w': 3.348403e-02, 'rg_conv_b': 3.202397e-01, 'rg_w_a': 1.205590e-02, 'rg_b_a': 8.931847e-03, 'rg_w_x': 2.178469e-02, 'rg_b_x': 1.182167e-02, 'rg_lambda': 1.633997e-02, 'hg_lower_bounds': 3.054531e-03, 'hg_norm_w': 3.974045e-02, 'w_branch': 2.154594e-02, 'w_out': 3.742686e-02, 'final_norm_w': 1.600680e+01}


def _to_microbatches(a, axis):
    t = _jnp.moveaxis(a, axis, 0)
    t = t.reshape((N_MICROBATCH, t.shape[0] // N_MICROBATCH) + t.shape[1:])
    return _jnp.moveaxis(t, 1, axis + 1)


def setup_inputs(seed: int = 0) -> dict:
    inp = _fwd_setup_inputs(seed)
    key = _jax.random.fold_in(_jax.random.key(seed), 7919)
    shape, _ = _output_shape()
    out = dict(inp)
    out["loss_target"] = _jax.random.normal(_jax.random.fold_in(key, 0), shape, _jnp.float32)
    for i, name in enumerate(TWIN_WEIGHTS):
        w = inp[name].astype(_jnp.float32)
        if MOMENT_SCALE is None:
            s = _jnp.sqrt(_jnp.mean(_jnp.square(w)) + 1e-30)
        else:
            s = MOMENT_SCALE[name]
        km, kv = _jax.random.split(_jax.random.fold_in(key, i + 1))
        out[name] = w
        out["m_" + name] = s * _jax.random.normal(km, w.shape, _jnp.float32)
        out["v_" + name] = (s * s) * _jax.random.uniform(kv, w.shape, _jnp.float32, 0.5, 1.5)
    if N_MICROBATCH > 1:
        for name, axis in PER_EXAMPLE_BATCH_AXIS.items():
            out[name] = _to_microbatches(out[name], axis)
    return {'x': out['x'], 'norm_w': out['norm_w'], 'w_in': out['w_in'], 's5_lambda_re': out['s5_lambda_re'], 's5_lambda_im': out['s5_lambda_im'], 's5_log_step': out['s5_log_step'], 's5_b_re': out['s5_b_re'], 's5_b_im': out['s5_b_im'], 's5_c_re': out['s5_c_re'], 's5_c_im': out['s5_c_im'], 's5_d': out['s5_d'], 's5_w_glu': out['s5_w_glu'], 's5_b_glu': out['s5_b_glu'], 'rg_conv_w': out['rg_conv_w'], 'rg_conv_b': out['rg_conv_b'], 'rg_w_a': out['rg_w_a'], 'rg_b_a': out['rg_b_a'], 'rg_w_x': out['rg_w_x'], 'rg_b_x': out['rg_b_x'], 'rg_lambda': out['rg_lambda'], 'hg_lower_bounds': out['hg_lower_bounds'], 'hg_norm_w': out['hg_norm_w'], 'w_branch': out['w_branch'], 'w_out': out['w_out'], 'final_norm_w': out['final_norm_w'], 'loss_target': out['loss_target'], 'm_norm_w': out['m_norm_w'], 'm_w_in': out['m_w_in'], 'm_s5_lambda_re': out['m_s5_lambda_re'], 'm_s5_lambda_im': out['m_s5_lambda_im'], 'm_s5_log_step': out['m_s5_log_step'], 'm_s5_b_re': out['m_s5_b_re'], 'm_s5_b_im': out['m_s5_b_im'], 'm_s5_c_re': out['m_s5_c_re'], 'm_s5_c_im': out['m_s5_c_im'], 'm_s5_d': out['m_s5_d'], 'm_s5_w_glu': out['m_s5_w_glu'], 'm_s5_b_glu': out['m_s5_b_glu'], 'm_rg_conv_w': out['m_rg_conv_w'], 'm_rg_conv_b': out['m_rg_conv_b'], 'm_rg_w_a': out['m_rg_w_a'], 'm_rg_b_a': out['m_rg_b_a'], 'm_rg_w_x': out['m_rg_w_x'], 'm_rg_b_x': out['m_rg_b_x'], 'm_rg_lambda': out['m_rg_lambda'], 'm_hg_lower_bounds': out['m_hg_lower_bounds'], 'm_hg_norm_w': out['m_hg_norm_w'], 'm_w_branch': out['m_w_branch'], 'm_w_out': out['m_w_out'], 'm_final_norm_w': out['m_final_norm_w'], 'v_norm_w': out['v_norm_w'], 'v_w_in': out['v_w_in'], 'v_s5_lambda_re': out['v_s5_lambda_re'], 'v_s5_lambda_im': out['v_s5_lambda_im'], 'v_s5_log_step': out['v_s5_log_step'], 'v_s5_b_re': out['v_s5_b_re'], 'v_s5_b_im': out['v_s5_b_im'], 'v_s5_c_re': out['v_s5_c_re'], 'v_s5_c_im': out['v_s5_c_im'], 'v_s5_d': out['v_s5_d'], 'v_s5_w_glu': out['v_s5_w_glu'], 'v_s5_b_glu': out['v_s5_b_glu'], 'v_rg_conv_w': out['v_rg_conv_w'], 'v_rg_conv_b': out['v_rg_conv_b'], 'v_rg_w_a': out['v_rg_w_a'], 'v_rg_b_a': out['v_rg_b_a'], 'v_rg_w_x': out['v_rg_w_x'], 'v_rg_b_x': out['v_rg_b_x'], 'v_rg_lambda': out['v_rg_lambda'], 'v_hg_lower_bounds': out['v_hg_lower_bounds'], 'v_hg_norm_w': out['v_hg_norm_w'], 'v_w_branch': out['v_w_branch'], 'v_w_out': out['v_w_out'], 'v_final_norm_w': out['v_final_norm_w']}


def _loss(weights, diff, rest, loss_target):
    with _jax.named_scope("forward"):
        args = {**rest, TWIN_DIFF_INPUT: diff, **{k: w.astype(_WEIGHT_DTYPES[k]) for k, w in weights.items()}}
        y = _forward(args)
    with _jax.named_scope("loss_head"):
        err = _jnp.square(y.astype(_jnp.float32) - loss_target)
        return 0.5 * _jnp.sum(_jnp.mean(err, axis=-1)) if err.ndim else 0.5 * err


def _adamw(w, g, m, v):
    m = ADAM_B1 * m + (1.0 - ADAM_B1) * g
    v = ADAM_B2 * v + (1.0 - ADAM_B2) * _jnp.square(g)
    m_hat = m / (1.0 - ADAM_B1 ** ADAM_STEP)
    v_hat = v / (1.0 - ADAM_B2 ** ADAM_STEP)
    delta = -ADAM_LR * (m_hat / (_jnp.sqrt(v_hat) + ADAM_EPS) + ADAM_WD * w)
    return delta, m, v


def reference(x, norm_w, w_in, s5_lambda_re, s5_lambda_im, s5_log_step, s5_b_re, s5_b_im, s5_c_re, s5_c_im, s5_d, s5_w_glu, s5_b_glu, rg_conv_w, rg_conv_b, rg_w_a, rg_b_a, rg_w_x, rg_b_x, rg_lambda, hg_lower_bounds, hg_norm_w, w_branch, w_out, final_norm_w, loss_target, m_norm_w, m_w_in, m_s5_lambda_re, m_s5_lambda_im, m_s5_log_step, m_s5_b_re, m_s5_b_im, m_s5_c_re, m_s5_c_im, m_s5_d, m_s5_w_glu, m_s5_b_glu, m_rg_conv_w, m_rg_conv_b, m_rg_w_a, m_rg_b_a, m_rg_w_x, m_rg_b_x, m_rg_lambda, m_hg_lower_bounds, m_hg_norm_w, m_w_branch, m_w_out, m_final_norm_w, v_norm_w, v_w_in, v_s5_lambda_re, v_s5_lambda_im, v_s5_log_step, v_s5_b_re, v_s5_b_im, v_s5_c_re, v_s5_c_im, v_s5_d, v_s5_w_glu, v_s5_b_glu, v_rg_conv_w, v_rg_conv_b, v_rg_w_a, v_rg_b_a, v_rg_w_x, v_rg_b_x, v_rg_lambda, v_hg_lower_bounds, v_hg_norm_w, v_w_branch, v_w_out, v_final_norm_w):
    given = dict(x=x, norm_w=norm_w, w_in=w_in, s5_lambda_re=s5_lambda_re, s5_lambda_im=s5_lambda_im, s5_log_step=s5_log_step, s5_b_re=s5_b_re, s5_b_im=s5_b_im, s5_c_re=s5_c_re, s5_c_im=s5_c_im, s5_d=s5_d, s5_w_glu=s5_w_glu, s5_b_glu=s5_b_glu, rg_conv_w=rg_conv_w, rg_conv_b=rg_conv_b, rg_w_a=rg_w_a, rg_b_a=rg_b_a, rg_w_x=rg_w_x, rg_b_x=rg_b_x, rg_lambda=rg_lambda, hg_lower_bounds=hg_lower_bounds, hg_norm_w=hg_norm_w, w_branch=w_branch, w_out=w_out, final_norm_w=final_norm_w, loss_target=loss_target, m_norm_w=m_norm_w, m_w_in=m_w_in, m_s5_lambda_re=m_s5_lambda_re, m_s5_lambda_im=m_s5_lambda_im, m_s5_log_step=m_s5_log_step, m_s5_b_re=m_s5_b_re, m_s5_b_im=m_s5_b_im, m_s5_c_re=m_s5_c_re, m_s5_c_im=m_s5_c_im, m_s5_d=m_s5_d, m_s5_w_glu=m_s5_w_glu, m_s5_b_glu=m_s5_b_glu, m_rg_conv_w=m_rg_conv_w, m_rg_conv_b=m_rg_conv_b, m_rg_w_a=m_rg_w_a, m_rg_b_a=m_rg_b_a, m_rg_w_x=m_rg_w_x, m_rg_b_x=m_rg_b_x, m_rg_lambda=m_rg_lambda, m_hg_lower_bounds=m_hg_lower_bounds, m_hg_norm_w=m_hg_norm_w, m_w_branch=m_w_branch, m_w_out=m_w_out, m_final_norm_w=m_final_norm_w, v_norm_w=v_norm_w, v_w_in=v_w_in, v_s5_lambda_re=v_s5_lambda_re, v_s5_lambda_im=v_s5_lambda_im, v_s5_log_step=v_s5_log_step, v_s5_b_re=v_s5_b_re, v_s5_b_im=v_s5_b_im, v_s5_c_re=v_s5_c_re, v_s5_c_im=v_s5_c_im, v_s5_d=v_s5_d, v_s5_w_glu=v_s5_w_glu, v_s5_b_glu=v_s5_b_glu, v_rg_conv_w=v_rg_conv_w, v_rg_conv_b=v_rg_conv_b, v_rg_w_a=v_rg_w_a, v_rg_b_a=v_rg_b_a, v_rg_w_x=v_rg_w_x, v_rg_b_x=v_rg_b_x, v_rg_lambda=v_rg_lambda, v_hg_lower_bounds=v_hg_lower_bounds, v_hg_norm_w=v_hg_norm_w, v_w_branch=v_w_branch, v_w_out=v_w_out, v_final_norm_w=v_final_norm_w)
    weights = {n: given[n] for n in TWIN_WEIGHTS}
    shared = {n: given[n] for n in SHARED_INPUTS}
    per_example = {n: given[n] for n in ['x']}
    grad_fn = _jax.value_and_grad(_loss, argnums=(0, 1))

    def one_microbatch(ex, loss_target):
        ex = dict(ex)
        diff = ex.pop(TWIN_DIFF_INPUT)
        return grad_fn(weights, diff, {**shared, **ex}, loss_target)

    if N_MICROBATCH == 1:
        loss, (grad_w, grad_x) = one_microbatch(per_example, given["loss_target"])
    else:
        def body(carry, xs):
            loss_sum, grad_sum = carry
            l_k, (gw_k, gx_k) = one_microbatch(xs[0], xs[1])
            with _jax.named_scope("update"):
                return (loss_sum + l_k, _jax.tree.map(_jnp.add, grad_sum, gw_k)), gx_k

        init = (_jnp.zeros((), _jnp.float32), _jax.tree.map(_jnp.zeros_like, weights))
        (loss, grad_w), grad_x = _jax.lax.scan(body, init, (per_example, given["loss_target"]))
    with _jax.named_scope("update"):
        delta_w, new_m, new_v = {}, {}, {}
        for n in TWIN_WEIGHTS:
            delta_w[n], new_m[n], new_v[n] = _adamw(weights[n], grad_w[n], given["m_" + n], given["v_" + n])
    return (loss, grad_x, *[grad_w[n] for n in TWIN_WEIGHTS], *[delta_w[n] for n in TWIN_WEIGHTS],
            *[new_m[n] for n in TWIN_WEIGHTS], *[new_v[n] for n in TWIN_WEIGHTS])
```

```python
import functools
import math

import jax
import jax.numpy as jnp
from jax import lax
from jax.experimental import pallas as pl
from jax.experimental.pallas import tpu as pltpu

f32 = jnp.float32
bf16 = jnp.bfloat16

D_MODEL = 2048
W_MIX = 1024
DEPTH = 4
N_BRANCH = 3
N_IN = 8 * W_MIX + N_BRANCH * D_MODEL
S5_GROUPS, S5_STATE, S5_GROUP = 64, 64, 16
RG_BLOCKS, RG_BLOCK, RG_CONV, RG_C = 16, 64, 4, 8.0
HG_HEADS, HG_DK = 8, 128
HG_SUB = 16
EPS = 1e-6
ADAM_LR, ADAM_B1, ADAM_B2, ADAM_EPS, ADAM_WD, ADAM_STEP = 0.001, 0.9, 0.999, 1e-08, 0.01, 10

N_DEV = 8
LANE = 128
NCH = W_MIX // LANE
VMEM_LIMIT = 56 * 1024 * 1024
MESH = pl.DeviceIdType.MESH
ANY = pl.BlockSpec(memory_space=pl.ANY)
HIGHEST = lax.Precision.HIGHEST

C_UA, C_GA, C_XB, C_GB, C_Q, C_F, C_I, C_GC, C_GATE = (W_MIX * k for k in range(9))


def _cparams(sem=None):
    return pltpu.CompilerParams(dimension_semantics=sem, vmem_limit_bytes=VMEM_LIMIT)


@jax.custom_vjp
def bdot(a, w):
    return jnp.dot(a.astype(bf16), w.astype(bf16), preferred_element_type=f32)


def _bdot_fwd(a, w):
    return bdot(a, w), (a, w)


def _bdot_bwd(res, g):
    a, w = res
    gb = g.astype(bf16)
    da = lax.dot_general(gb, w.astype(bf16), (((1,), (1,)), ((), ())), preferred_element_type=f32)
    dw = lax.dot_general(a.astype(bf16), gb, (((0,), (0,)), ((), ())), preferred_element_type=f32)
    return da, dw


bdot.defvjp(_bdot_fwd, _bdot_bwd)


@jax.custom_vjp
def cdot(c, ct, a):
    return jnp.dot(c, a, preferred_element_type=f32, precision=HIGHEST)


def _cdot_fwd(c, ct, a):
    return cdot(c, ct, a), (c, ct)


def _cdot_bwd(res, g):
    c, ct = res
    return jnp.zeros_like(c), jnp.zeros_like(ct), jnp.dot(ct, g, preferred_element_type=f32, precision=HIGHEST)


cdot.defvjp(_cdot_fwd, _cdot_bwd)


def mm(a, b, *, name, out_dtype=f32, add=None, tm=512, tn=1024, tk=4096):
    m, k = a.shape
    _, n = b.shape
    tm, tn, tk = min(tm, m), min(tn, n), min(tk, k)
    assert m % tm == 0 and n % tn == 0 and k % tk == 0
    nk = k // tk

    def body(*refs):
        if add is None:
            a_ref, b_ref, o_ref, acc_ref = refs
        else:
            a_ref, b_ref, r_ref, o_ref, acc_ref = refs
        kk = pl.program_id(2)
        part = jnp.dot(a_ref[...], b_ref[...], preferred_element_type=f32)

        @pl.when(kk == 0)
        def _():
            acc_ref[...] = part

        @pl.when(kk > 0)
        def _():
            acc_ref[...] = acc_ref[...] + part

        @pl.when(kk == nk - 1)
        def _():
            acc = acc_ref[...]
            if add is not None:
                acc = acc + r_ref[...]
            o_ref[...] = acc.astype(out_dtype)

    in_specs = [pl.BlockSpec((tm, tk), lambda i, j, q: (i, q)), pl.BlockSpec((tk, tn), lambda i, j, q: (q, j))]
    args = [a, b]
    if add is not None:
        in_specs.append(pl.BlockSpec((tm, tn), lambda i, j, q: (i, j)))
        args.append(add)
    return pl.pallas_call(
        body, name=name, grid=(m // tm, n // tn, nk), in_specs=in_specs,
        out_specs=pl.BlockSpec((tm, tn), lambda i, j, q: (i, j)),
        out_shape=jax.ShapeDtypeStruct((m, n), out_dtype),
        scratch_shapes=[pltpu.VMEM((tm, tn), f32)],
        compiler_params=_cparams(("parallel", "parallel", "arbitrary")),
    )(*args)


def _row_spec(tm, wc, col_off):
    base = col_off // wc
    assert col_off % wc == 0
    return pl.BlockSpec((tm, wc), lambda j, i: (i, base + j))


def _slab_spec(arr):
    r, c = arr.shape[1:]
    if arr.shape[0] == 1:
        return pl.BlockSpec((1, r, c), lambda j, i: (0, 0, 0))
    return pl.BlockSpec((1, r, c), lambda j, i: (j, 0, 0))


def rowwise(fn, rows, params, consts, outs, *, name, tm=256, ncol=1, rowid=False):
    t = rows[0][0].shape[0]
    tm = min(tm, t)
    nr, npar, nc, no = len(rows), len(params), len(consts), len(outs)

    def body(*refs):
        r = [refs[k][...].astype(f32) for k in range(nr)]
        p = [refs[nr + k][0] for k in range(npar + nc)]
        extra = ()
        if rowid:
            extra = (pl.program_id(1) * tm + lax.broadcasted_iota(jnp.int32, (tm, 1), 0),)
        res = fn(*extra, *r, *p)
        for k in range(no):
            refs[nr + npar + nc + k][...] = res[k].astype(outs[k][1])

    in_specs = [_row_spec(tm, w // ncol, off) for (_, off, w) in rows]
    in_specs += [_slab_spec(a) for a in list(params) + list(consts)]
    out_specs = [pl.BlockSpec((tm, w // ncol), lambda j, i: (i, j)) for (w, _) in outs]
    out_shape = [jax.ShapeDtypeStruct((t, w), dt) for (w, dt) in outs]
    return pl.pallas_call(
        body, name=name, grid=(ncol, t // tm), in_specs=in_specs, out_specs=out_specs, out_shape=out_shape,
        compiler_params=_cparams(("parallel", "parallel")),
    )(*[r[0] for r in rows], *params, *consts)


def rowwise_vjp(fn, rows, params, consts, cts, d_rows, *, name, tm=256, ncol=1, rowid=False, sum_primal=None):
    t = rows[0][0].shape[0]
    tm = min(tm, t)
    nr, npar, nc, nct, ndr = len(rows), len(params), len(consts), len(cts), len(d_rows)

    def body(*refs):
        i = pl.program_id(1)
        r = [refs[k][...].astype(f32) for k in range(nr)]
        p = [refs[nr + k][0] for k in range(npar)]
        c = [refs[nr + npar + k][0] for k in range(nc)]
        g = [refs[nr + npar + nc + k][...].astype(f32) for k in range(nct)]
        orefs = refs[nr + npar + nc + nct:]
        extra = ()
        if rowid:
            extra = (i * tm + lax.broadcasted_iota(jnp.int32, (tm, 1), 0),)
        res, vjp = jax.vjp(lambda *v: fn(*extra, *v, *c), *r, *p)
        grads = vjp(tuple(g))
        for k, (idx, dt) in enumerate(d_rows):
            orefs[k][...] = grads[idx].astype(dt)
        acc = [grads[nr + k] for k in range(npar)]
        if sum_primal is not None:
            acc.append(jnp.sum(res[sum_primal], axis=0, keepdims=True))

        @pl.when(i == 0)
        def _():
            for k, a in enumerate(acc):
                orefs[ndr + k][0] = a

        @pl.when(i > 0)
        def _():
            for k, a in enumerate(acc):
                orefs[ndr + k][0] = orefs[ndr + k][0] + a

    in_specs = [_row_spec(tm, w // ncol, off) for (_, off, w) in rows]
    in_specs += [_slab_spec(a) for a in list(params) + list(consts)]
    in_specs += [_row_spec(tm, w // ncol, off) for (_, off, w) in cts]
    out_specs = [pl.BlockSpec((tm, rows[idx][2] // ncol), lambda j, i: (i, j)) for (idx, _) in d_rows]
    out_shape = [jax.ShapeDtypeStruct((t, rows[idx][2]), dt) for (idx, dt) in d_rows]
    for a in params:
        out_specs.append(pl.BlockSpec((1,) + a.shape[1:], lambda j, i: (j, 0, 0)))
        out_shape.append(jax.ShapeDtypeStruct(a.shape, f32))
    if sum_primal is not None:
        w = cts[sum_primal][2]
        out_specs.append(pl.BlockSpec((1, 1, w // ncol), lambda j, i: (j, 0, 0)))
        out_shape.append(jax.ShapeDtypeStruct((ncol, 1, w // ncol), f32))
    return pl.pallas_call(
        body, name=name, grid=(ncol, t // tm), in_specs=in_specs, out_specs=out_specs, out_shape=out_shape,
        compiler_params=_cparams(("parallel", "arbitrary")),
    )(*[r[0] for r in rows], *params, *consts, *[c[0] for c in cts])


VM = pl.BlockSpec(memory_space=pltpu.VMEM)


def whole(fn, ins, outs, *, name):
    def body(*refs):
        res = fn(*[r[...] for r in refs[:len(ins)]])
        for k, o in enumerate(refs[len(ins):]):
            o[...] = res[k]
    return pl.pallas_call(body, name=name, in_specs=[VM] * len(ins), out_specs=[VM] * len(outs),
                          out_shape=[jax.ShapeDtypeStruct(s, f32) for s in outs],
                          compiler_params=_cparams())(*ins)


def whole_vjp(fn, ins, cts, *, name):
    n = len(ins)

    def body(*refs):
        _, vjp = jax.vjp(fn, *[r[...] for r in refs[:n]])
        grads = vjp(tuple(r[...] for r in refs[n:n + len(cts)]))
        for k, o in enumerate(refs[n + len(cts):]):
            o[...] = grads[k]
    return pl.pallas_call(body, name=name, in_specs=[VM] * (n + len(cts)), out_specs=[VM] * n,
                          out_shape=[jax.ShapeDtypeStruct(a.shape, f32) for a in ins],
                          compiler_params=_cparams())(*ins, *cts)


def ln_fn(x, w):
    return (x * lax.rsqrt(jnp.mean(x * x, axis=-1, keepdims=True) + EPS) * w,)


def ln_res_fn(x, w):
    return ln_fn(x, w)[0], x


def loss_fn(x, tgt, w):
    y = ln_fn(x, w)[0]
    return (0.5 * jnp.mean(jnp.square(y - tgt), axis=-1, keepdims=True),)


def s5_prep_fn(lam_re, lam_im, log_step, b_re, b_im):
    step = jnp.exp(log_step)
    mag = jnp.exp(lam_re * step)
    ang = lam_im * step
    abar_re = mag * jnp.cos(ang)
    abar_im = mag * jnp.sin(ang)
    num_re = abar_re - 1.0
    num_im = abar_im
    den = lam_re * lam_re + lam_im * lam_im
    coef_re = (num_re * lam_re + num_im * lam_im) / den
    coef_im = (num_im * lam_re - num_re * lam_im) / den
    bbar_re = coef_re * b_re - coef_im * b_im
    bbar_im = coef_re * b_im + coef_im * b_re
    return abar_re, abar_im, bbar_re, bbar_im


def lb_prep_fn(r0, r1, r2, r3):
    m = jnp.maximum(jnp.maximum(r0, r1), jnp.maximum(r2, r3))
    e0, e1, e2, e3 = jnp.exp(r0 - m), jnp.exp(r1 - m), jnp.exp(r2 - m), jnp.exp(r3 - m)
    s = e0 + e1 + e2 + e3
    p0, p1, p2, p3 = e0 / s, e1 / s, e2 / s, e3 / s
    c1 = p0 + p1
    c2 = c1 + p2
    c3 = c2 + p3
    return p0 - p0, c1 - p0, c2 - p0, c3 - p0


def s5_post_fn(yssm, u, ga, d, wglu, bglu):
    y = jax.nn.gelu(yssm + d * u)
    y = y * jax.nn.sigmoid(bdot(y, wglu) + bglu)
    return (y * jax.nn.silu(ga),)


def rg_gate_fn(tglob, xc, wa, ba, wx, bx, lam):
    r = jax.nn.sigmoid(bdot(xc, wa) + ba)
    i = jax.nn.sigmoid(bdot(xc, wx) + bx)
    log_a = -RG_C * r * jax.nn.softplus(-lam)
    a = jnp.exp(log_a)
    mult = jnp.sqrt(-jnp.tanh(log_a) * (a * a + 1.0))
    mult = jnp.where(tglob == 0, 1.0, mult)
    return a, mult * (i * xc)


def hg_pre_fn(q, fl, lb, tri, tri_t):
    f = lb + (1.0 - lb) * jax.nn.sigmoid(fl)
    return jax.nn.silu(q), 1.0 - f, cdot(tri, tri_t, jnp.log(f))


def branch_prep_fn(hb, gb, oc, gc, nw):
    yb = hb * jax.nn.silu(gb)
    on = oc * lax.rsqrt(jnp.mean(oc * oc, axis=-1, keepdims=True) + EPS) * nw
    return yb, on * jax.nn.silu(gc)


def merge_fn(b0, b1, b2, g0, g1, g2):
    return (jax.nn.sigmoid(g0) * b0 + jax.nn.sigmoid(g1) * b1 + jax.nn.sigmoid(g2) * b2,)


S5_TB = 512
S5_SC = 512


def _shift_down(v, k, row, fill=0.0):
    return jnp.where(row >= k, pltpu.roll(v, k, 0), fill)


def _shift_up(v, k, row, n, fill=0.0):
    return jnp.where(row < n - k, pltpu.roll(v, n - k, 0), fill)


def s5_scan_fwd(z, bd_re, bd_im, cd_re, cd_im, a_re, a_im):
    t = z.shape[0]
    tb = min(S5_TB, t)

    def body(u_ref, bre, bim, cre, cim, are, aim, y_ref, xre_ref, xim_ref, car_re, car_im):
        @pl.when(pl.program_id(1) == 0)
        def _():
            car_re[...] = jnp.zeros_like(car_re)
            car_im[...] = jnp.zeros_like(car_im)

        u = u_ref[...].astype(bf16)
        row = lax.broadcasted_iota(jnp.int32, (tb, 1), 0)
        ar, ai = are[0], aim[0]
        cr, ci = car_re[...], car_im[...]
        xr = jnp.dot(u, bre[0], preferred_element_type=f32) + jnp.where(row == 0, ar * cr - ai * ci, 0.0)
        xi = jnp.dot(u, bim[0], preferred_element_type=f32) + jnp.where(row == 0, ar * ci + ai * cr, 0.0)
        pr, pi_ = ar, ai
        k = 1
        while k < tb:
            sr, si = _shift_down(xr, k, row), _shift_down(xi, k, row)
            xr, xi = xr + pr * sr - pi_ * si, xi + pr * si + pi_ * sr
            pr, pi_ = pr * pr - pi_ * pi_, 2.0 * pr * pi_
            k *= 2
        car_re[...] = xr[tb - 1:tb, :]
        car_im[...] = xi[tb - 1:tb, :]
        xre_ref[...] = xr
        xim_ref[...] = xi
        y_ref[...] = (jnp.dot(xr.astype(bf16), cre[0], preferred_element_type=f32)
                      - jnp.dot(xi.astype(bf16), cim[0], preferred_element_type=f32))

    chunk = lambda r, c: pl.BlockSpec((1, r, c), lambda j, i: (j, 0, 0))
    return pl.pallas_call(
        body, name="s5_scan_fwd", grid=(NCH, t // tb),
        in_specs=[pl.BlockSpec((tb, LANE), lambda j, i: (i, C_UA // LANE + j)),
                  chunk(LANE, S5_SC), chunk(LANE, S5_SC), chunk(S5_SC, LANE), chunk(S5_SC, LANE),
                  chunk(1, S5_SC), chunk(1, S5_SC)],
        out_specs=[pl.BlockSpec((tb, LANE), lambda j, i: (i, j)),
                   pl.BlockSpec((tb, S5_SC), lambda j, i: (i, j)),
                   pl.BlockSpec((tb, S5_SC), lambda j, i: (i, j))],
        out_shape=[jax.ShapeDtypeStruct((t, W_MIX), f32),
                   jax.ShapeDtypeStruct((t, NCH * S5_SC), f32),
                   jax.ShapeDtypeStruct((t, NCH * S5_SC), f32)],
        scratch_shapes=[pltpu.VMEM((1, S5_SC), f32), pltpu.VMEM((1, S5_SC), f32)],
        compiler_params=_cparams(("parallel", "arbitrary")),
    )(z, bd_re.astype(bf16), bd_im.astype(bf16), cd_re.astype(bf16), cd_im.astype(bf16), a_re, a_im)


def s5_scan_bwd(dy, du1, z, xre, xim, bd_re, bd_im, cd_re, cd_im, a_re, a_im):
    t = z.shape[0]
    tb = min(S5_TB, t)
    nt = t // tb

    def body(dy_ref, du1_ref, u_ref, xre_ref, xim_ref, hre_ref, him_ref, bre, bim, cre, cim, are, aim,
             du_ref, dbre, dbim, dcre, dcim, dare, daim, car_re, car_im):
        step = pl.program_id(1)
        tt = nt - 1 - step

        @pl.when(step == 0)
        def _():
            car_re[...] = jnp.zeros_like(car_re)
            car_im[...] = jnp.zeros_like(car_im)

        nt_dims = (((1,), (1,)), ((), ()))
        tn_dims = (((0,), (0,)), ((), ()))
        dyb = dy_ref[...].astype(bf16)
        row = lax.broadcasted_iota(jnp.int32, (tb, 1), 0)
        xr, xi = xre_ref[...], xim_ref[...]
        ar, ai = are[0], aim[0]
        cr, ci = ar, -ai
        kr, ki = car_re[...], car_im[...]
        lr = lax.dot_general(dyb, cre[0], nt_dims, preferred_element_type=f32)
        li = -lax.dot_general(dyb, cim[0], nt_dims, preferred_element_type=f32)
        lr = lr + jnp.where(row == tb - 1, cr * kr - ci * ki, 0.0)
        li = li + jnp.where(row == tb - 1, cr * ki + ci * kr, 0.0)
        pr, pi_ = cr, ci
        k = 1
        while k < tb:
            sr, si = _shift_up(lr, k, row, tb), _shift_up(li, k, row, tb)
            lr, li = lr + pr * sr - pi_ * si, li + pr * si + pi_ * sr
            pr, pi_ = pr * pr - pi_ * pi_, 2.0 * pr * pi_
            k *= 2
        car_re[...] = lr[0:1, :]
        car_im[...] = li[0:1, :]
        lrb, lib = lr.astype(bf16), li.astype(bf16)
        ub = u_ref[...].astype(bf16)
        du = (lax.dot_general(lrb, bre[0], nt_dims, preferred_element_type=f32)
              + lax.dot_general(lib, bim[0], nt_dims, preferred_element_type=f32))
        du_ref[...] = (du + du1_ref[...].astype(f32)).astype(du_ref.dtype)
        live = (tt > 0).astype(f32)
        xpr = jnp.where(row == 0, hre_ref[7:8, :] * live, pltpu.roll(xr, 1, 0))
        xpi = jnp.where(row == 0, him_ref[7:8, :] * live, pltpu.roll(xi, 1, 0))
        acc = [
            lax.dot_general(ub, lrb, tn_dims, preferred_element_type=f32),
            lax.dot_general(ub, lib, tn_dims, preferred_element_type=f32),
            lax.dot_general(xr.astype(bf16), dyb, tn_dims, preferred_element_type=f32),
            -lax.dot_general(xi.astype(bf16), dyb, tn_dims, preferred_element_type=f32),
            jnp.sum(lr * xpr + li * xpi, axis=0, keepdims=True),
            jnp.sum(li * xpr - lr * xpi, axis=0, keepdims=True),
        ]
        outs = [dbre, dbim, dcre, dcim, dare, daim]

        @pl.when(step == 0)
        def _():
            for o, a in zip(outs, acc):
                o[0] = a

        @pl.when(step > 0)
        def _():
            for o, a in zip(outs, acc):
                o[0] = o[0] + a

    chunk = lambda r, c: pl.BlockSpec((1, r, c), lambda j, i: (j, 0, 0))
    rev = lambda w, base=0: pl.BlockSpec((tb, w), lambda j, i: (nt - 1 - i, base + j))
    halo = pl.BlockSpec((8, S5_SC), lambda j, i: (jnp.maximum((nt - 1 - i) * (tb // 8) - 1, 0), j))
    return pl.pallas_call(
        body, name="s5_scan_bwd", grid=(NCH, nt),
        in_specs=[rev(LANE), rev(LANE), rev(LANE, C_UA // LANE), rev(S5_SC), rev(S5_SC), halo, halo,
                  chunk(LANE, S5_SC), chunk(LANE, S5_SC), chunk(S5_SC, LANE), chunk(S5_SC, LANE),
                  chunk(1, S5_SC), chunk(1, S5_SC)],
        out_specs=[rev(LANE), chunk(LANE, S5_SC), chunk(LANE, S5_SC), chunk(S5_SC, LANE), chunk(S5_SC, LANE),
                   chunk(1, S5_SC), chunk(1, S5_SC)],
        out_shape=[jax.ShapeDtypeStruct((t, W_MIX), bf16),
                   jax.ShapeDtypeStruct((NCH, LANE, S5_SC), f32), jax.ShapeDtypeStruct((NCH, LANE, S5_SC), f32),
                   jax.ShapeDtypeStruct((NCH, S5_SC, LANE), f32), jax.ShapeDtypeStruct((NCH, S5_SC, LANE), f32),
                   jax.ShapeDtypeStruct((NCH, 1, S5_SC), f32), jax.ShapeDtypeStruct((NCH, 1, S5_SC), f32)],
        scratch_shapes=[pltpu.VMEM((1, S5_SC), f32), pltpu.VMEM((1, S5_SC), f32)],
        compiler_params=_cparams(("parallel", "arbitrary")),
    )(dy, du1, z, xre, xim, xre, xim, bd_re.astype(bf16), bd_im.astype(bf16), cd_re.astype(bf16),
      cd_im.astype(bf16), a_re, a_im)


RG_TB = 512


def rg_conv_fwd(z, cw, cb):
    t = z.shape[0]
    tb = min(RG_TB, t)

    def body(x_ref, h_ref, cw_ref, cb_ref, o_ref):
        live = (pl.program_id(1) > 0).astype(f32)
        ext = jnp.concatenate([h_ref[...] * live, x_ref[...]], axis=0)
        w = cw_ref[0]
        acc = cb_ref[0] + w[3:4, :] * ext[8:, :]
        for k in range(3):
            acc = acc + w[k:k + 1, :] * pltpu.roll(ext, 3 - k, 0)[8:, :]
        o_ref[...] = acc

    base = C_XB // LANE
    chunk = lambda r: pl.BlockSpec((1, r, LANE), lambda j, i: (j, 0, 0))
    return pl.pallas_call(
        body, name="rg_conv_fwd", grid=(NCH, t // tb),
        in_specs=[pl.BlockSpec((tb, LANE), lambda j, i: (i, base + j)),
                  pl.BlockSpec((8, LANE), lambda j, i: (jnp.maximum(i * (tb // 8) - 1, 0), base + j)),
                  chunk(RG_CONV), chunk(1)],
        out_specs=pl.BlockSpec((tb, LANE), lambda j, i: (i, j)),
        out_shape=jax.ShapeDtypeStruct((t, W_MIX), f32),
        compiler_params=_cparams(("parallel", "parallel")),
    )(z, z, cw, cb)


def rg_conv_bwd(dxc, z, cw):
    t = z.shape[0]
    tb = min(RG_TB, t)
    nt = t // tb

    def body(g_ref, gn_ref, x_ref, h_ref, cw_ref, dx_ref, dcw_ref, dcb_ref):
        i = pl.program_id(1)
        g = g_ref[...]
        gext = jnp.concatenate([g, gn_ref[...] * (i < nt - 1).astype(f32)], axis=0)
        xext = jnp.concatenate([h_ref[...] * (i > 0).astype(f32), x_ref[...]], axis=0)
        w = cw_ref[0]
        dx = w[3:4, :] * g
        rows = [None] * RG_CONV
        rows[3] = jnp.sum(g * xext[8:, :], axis=0, keepdims=True)
        for k in range(3):
            s = 3 - k
            dx = dx + w[k:k + 1, :] * pltpu.roll(gext, tb + 8 - s, 0)[:tb, :]
            rows[k] = jnp.sum(g * pltpu.roll(xext, s, 0)[8:, :], axis=0, keepdims=True)
        dx_ref[...] = dx.astype(dx_ref.dtype)
        dcw = jnp.concatenate(rows, axis=0)
        dcb = jnp.sum(g, axis=0, keepdims=True)

        @pl.when(i == 0)
        def _():
            dcw_ref[0] = dcw
            dcb_ref[0] = dcb

        @pl.when(i > 0)
        def _():
            dcw_ref[0] = dcw_ref[0] + dcw
            dcb_ref[0] = dcb_ref[0] + dcb

    base = C_XB // LANE
    chunk = lambda r: pl.BlockSpec((1, r, LANE), lambda j, i: (j, 0, 0))
    return pl.pallas_call(
        body, name="rg_conv_bwd", grid=(NCH, nt),
        in_specs=[pl.BlockSpec((tb, LANE), lambda j, i: (i, j)),
                  pl.BlockSpec((8, LANE), lambda j, i: (jnp.minimum((i + 1) * (tb // 8), t // 8 - 1), j)),
                  pl.BlockSpec((tb, LANE), lambda j, i: (i, base + j)),
                  pl.BlockSpec((8, LANE), lambda j, i: (jnp.maximum(i * (tb // 8) - 1, 0), base + j)),
                  chunk(RG_CONV)],
        out_specs=[pl.BlockSpec((tb, LANE), lambda j, i: (i, j)), chunk(RG_CONV), chunk(1)],
        out_shape=[jax.ShapeDtypeStruct((t, W_MIX), bf16), jax.ShapeDtypeStruct((NCH, RG_CONV, LANE), f32),
                   jax.ShapeDtypeStruct((NCH, 1, LANE), f32)],
        compiler_params=_cparams(("parallel", "arbitrary")),
    )(dxc, dxc, z, z, cw)


def rg_scan_fwd(a, b):
    t = a.shape[0]
    tb = min(RG_TB, t)

    def body(a_ref, b_ref, h_ref, car):
        @pl.when(pl.program_id(1) == 0)
        def _():
            car[...] = jnp.zeros_like(car)

        row = lax.broadcasted_iota(jnp.int32, (tb, 1), 0)
        aa, bb = a_ref[...], b_ref[...]
        k = 1
        while k < tb:
            bb = bb + aa * _shift_down(bb, k, row)
            aa = aa * _shift_down(aa, k, row, 1.0)
            k *= 2
        h = bb + aa * car[...]
        car[...] = h[tb - 1:tb, :]
        h_ref[...] = h

    spec = pl.BlockSpec((tb, LANE), lambda j, i: (i, j))
    return pl.pallas_call(
        body, name="rg_scan_fwd", grid=(NCH, t // tb), in_specs=[spec, spec], out_specs=spec,
        out_shape=jax.ShapeDtypeStruct((t, W_MIX), f32), scratch_shapes=[pltpu.VMEM((1, LANE), f32)],
        compiler_params=_cparams(("parallel", "arbitrary")),
    )(a, b)


def rg_scan_bwd(dh, a, h):
    t = a.shape[0]
    tb = min(RG_TB, t)
    nt = t // tb

    def body(g_ref, a_ref, an_ref, h_ref, hp_ref, da_ref, db_ref, car):
        step = pl.program_id(1)
        tt = nt - 1 - step

        @pl.when(step == 0)
        def _():
            car[...] = jnp.zeros_like(car)

        row = lax.broadcasted_iota(jnp.int32, (tb, 1), 0)
        an = an_ref[0:1, :] * (tt < nt - 1).astype(f32)
        aa = jnp.where(row == tb - 1, an, pltpu.roll(a_ref[...], tb - 1, 0))
        bb = g_ref[...]
        k = 1
        while k < tb:
            bb = bb + aa * _shift_up(bb, k, row, tb)
            aa = aa * _shift_up(aa, k, row, tb, 1.0)
            k *= 2
        lam = bb + aa * car[...]
        car[...] = lam[0:1, :]
        hp = jnp.where(row == 0, hp_ref[7:8, :] * (tt > 0).astype(f32), pltpu.roll(h_ref[...], 1, 0))
        da_ref[...] = lam * hp
        db_ref[...] = lam

    rev = pl.BlockSpec((tb, LANE), lambda j, i: (nt - 1 - i, j))
    nxt = pl.BlockSpec((8, LANE), lambda j, i: (jnp.minimum((nt - i) * (tb // 8), t // 8 - 1), j))
    prv = pl.BlockSpec((8, LANE), lambda j, i: (jnp.maximum((nt - 1 - i) * (tb // 8) - 1, 0), j))
    return pl.pallas_call(
        body, name="rg_scan_bwd", grid=(NCH, nt), in_specs=[rev, rev, nxt, rev, prv], out_specs=[rev, rev],
        out_shape=[jax.ShapeDtypeStruct((t, W_MIX), f32)] * 2, scratch_shapes=[pltpu.VMEM((1, LANE), f32)],
        compiler_params=_cparams(("parallel", "arbitrary")),
    )(dh, a, a, h, h)


HG_TB = 256


def _heads(v):
    return jnp.stack([v[:, LANE * h:LANE * (h + 1)] for h in range(HG_HEADS)])


def _unheads(v):
    return jnp.concatenate([v[h] for h in range(HG_HEADS)], axis=-1)


def _bmm(eq, a, b):
    return jnp.einsum(eq, a.astype(bf16), b.astype(bf16), preferred_element_type=f32)


def hg_chunk_fwd(qs, kk, gcum, z):
    t = qs.shape[0]
    tb = min(HG_TB, t)
    nc = tb // HG_SUB

    def body(q_ref, k_ref, g_ref, v_ref, o_ref, sall_ref, st_ref):
        @pl.when(pl.program_id(0) == 0)
        def _():
            st_ref[...] = jnp.zeros_like(st_ref)

        ri = lax.broadcasted_iota(jnp.int32, (1, HG_SUB, 1), 1)

        def chunk(c, carry):
            rows = pl.ds(pl.multiple_of(c * HG_SUB, HG_SUB), HG_SUB)
            q, k, g, v = _heads(q_ref[rows, :]), _heads(k_ref[rows, :]), _heads(g_ref[rows, :]), _heads(v_ref[rows, :])
            st = st_ref[...]
            sall_ref[c] = st
            o = _bmm('htk,hvk->htv', q * jnp.exp(g), st)
            for s in range(HG_SUB):
                p = jnp.where(ri >= s, jnp.exp(jnp.minimum(g - g[:, s:s + 1, :], 0.0)), 0.0)
                col = jnp.sum(q * k[:, s:s + 1, :] * p, axis=-1, keepdims=True)
                o = o + col * v[:, s:s + 1, :]
            gl = g[:, HG_SUB - 1:HG_SUB, :]
            st_ref[...] = st * jnp.exp(gl) + _bmm('htv,htk->hvk', v, k * jnp.exp(gl - g))
            o_ref[rows, :] = _unheads(o)
            return carry

        lax.fori_loop(0, nc, chunk, 0)

    spec = lambda base=0: pl.BlockSpec((tb, W_MIX), lambda i: (i, base))
    return pl.pallas_call(
        body, name="hg_chunk_fwd", grid=(t // tb,),
        in_specs=[spec(), spec(), spec(), spec(C_I // W_MIX)],
        out_specs=[spec(), pl.BlockSpec((nc, HG_HEADS, HG_DK, HG_DK), lambda i: (i, 0, 0, 0))],
        out_shape=[jax.ShapeDtypeStruct((t, W_MIX), f32),
                   jax.ShapeDtypeStruct((t // HG_SUB, HG_HEADS, HG_DK, HG_DK), f32)],
        scratch_shapes=[pltpu.VMEM((HG_HEADS, HG_DK, HG_DK), f32)],
        compiler_params=_cparams(("arbitrary",)),
    )(qs, kk, gcum, z)


def hg_chunk_bwd(do, qs, kk, gcum, z, sall):
    t = qs.shape[0]
    tb = min(HG_TB, t)
    nc = tb // HG_SUB
    nt = t // tb

    def body(do_ref, q_ref, k_ref, g_ref, v_ref, sall_ref, dq_ref, dk_ref, dg_ref, dv_ref, dst_ref):
        @pl.when(pl.program_id(0) == 0)
        def _():
            dst_ref[...] = jnp.zeros_like(dst_ref)

        ri = lax.broadcasted_iota(jnp.int32, (1, HG_SUB, 1), 1)

        def chunk(cc, carry):
            c = nc - 1 - cc
            rows = pl.ds(pl.multiple_of(c * HG_SUB, HG_SUB), HG_SUB)
            q, k, g, v = _heads(q_ref[rows, :]), _heads(k_ref[rows, :]), _heads(g_ref[rows, :]), _heads(v_ref[rows, :])
            d_o = _heads(do_ref[rows, :])
            st = sall_ref[c]
            dsn = dst_ref[...]
            eg = jnp.exp(g)
            qe = q * eg
            gl = g[:, HG_SUB - 1:HG_SUB, :]
            egl = jnp.exp(gl)
            dec = jnp.exp(gl - g)
            kd = k * dec
            dqe = _bmm('htv,hvk->htk', d_o, st)
            dst_ref[...] = _bmm('htv,htk->hvk', d_o, qe) + dsn * egl
            dgl_dec = jnp.sum(dsn * st, axis=1, keepdims=True) * egl
            dv = _bmm('htk,hvk->htv', kd, dsn)
            dkd = _bmm('htv,hvk->htk', v, dsn)
            a1 = jnp.zeros_like(q)
            a2 = jnp.zeros_like(q)
            for s in range(HG_SUB):
                p = jnp.where(ri >= s, jnp.exp(jnp.minimum(g - g[:, s:s + 1, :], 0.0)), 0.0)
                krow = k[:, s:s + 1, :]
                col = jnp.sum(q * krow * p, axis=-1, keepdims=True)
                dcol = jnp.sum(d_o * v[:, s:s + 1, :], axis=-1, keepdims=True)
                dv = jnp.where(ri == s, dv + jnp.sum(col * d_o, axis=1, keepdims=True), dv)
                t1 = dcol * p
                a1 = a1 + t1 * krow
                a2 = jnp.where(ri == s, jnp.sum(t1 * q, axis=1, keepdims=True), a2)
            dgl = jnp.sum(dkd * kd, axis=1, keepdims=True) + dgl_dec
            dg = dqe * qe + q * a1 - k * a2 - dkd * kd
            dg = jnp.where(ri == HG_SUB - 1, dg + dgl, dg)
            dq_ref[rows, :] = _unheads(dqe * eg + a1)
            dk_ref[rows, :] = _unheads(dkd * dec + a2)
            dg_ref[rows, :] = _unheads(dg)
            dv_ref[rows, :] = _unheads(dv).astype(dv_ref.dtype)
            return carry

        lax.fori_loop(0, nc, chunk, 0)

    spec = lambda base=0: pl.BlockSpec((tb, W_MIX), lambda i: (nt - 1 - i, base))
    return pl.pallas_call(
        body, name="hg_chunk_bwd", grid=(nt,),
        in_specs=[spec(), spec(), spec(), spec(), spec(C_I // W_MIX),
                  pl.BlockSpec((nc, HG_HEADS, HG_DK, HG_DK), lambda i: (nt - 1 - i, 0, 0, 0))],
        out_specs=[spec(), spec(), spec(), spec()],
        out_shape=[jax.ShapeDtypeStruct((t, W_MIX), f32)] * 3 + [jax.ShapeDtypeStruct((t, W_MIX), bf16)],
        scratch_shapes=[pltpu.VMEM((HG_HEADS, HG_DK, HG_DK), f32)],
        compiler_params=_cparams(("arbitrary",)),
    )(do, qs, kk, gcum, z, sall)


def adamw(w, m, v, slots, *, name, tr):
    nl, r, c = w.shape
    tr = min(tr, r)
    ns = slots[0].shape[0]
    c1 = 1.0 / (1.0 - ADAM_B1 ** ADAM_STEP)
    c2 = 1.0 / (1.0 - ADAM_B2 ** ADAM_STEP)

    def body(*refs):
        w_ref, m_ref, v_ref = refs[:3]
        s_refs = refs[3:3 + nl]
        g_ref, d_ref, mo_ref, vo_ref = refs[3 + nl:]
        for l in range(nl):
            g = s_refs[l][0].astype(f32)
            for s in range(1, ns):
                g = g + s_refs[l][s].astype(f32)
            mn = ADAM_B1 * m_ref[l] + (1.0 - ADAM_B1) * g
            vn = ADAM_B2 * v_ref[l] + (1.0 - ADAM_B2) * (g * g)
            g_ref[l] = g
            mo_ref[l] = mn
            vo_ref[l] = vn
            d_ref[l] = -ADAM_LR * ((mn * c1) / (jnp.sqrt(vn * c2) + ADAM_EPS) + ADAM_WD * w_ref[l])

    full = pl.BlockSpec((nl, tr, c), lambda i: (0, i, 0))
    slot = pl.BlockSpec((ns, tr, c), lambda i: (0, i, 0))
    return pl.pallas_call(
        body, name=name, grid=(r // tr,), in_specs=[full] * 3 + [slot] * nl, out_specs=[full] * 4,
        out_shape=[jax.ShapeDtypeStruct(w.shape, f32)] * 4, compiler_params=_cparams(("parallel",)),
    )(w, m, v, *slots)


def _slab(ref, axis, idx, n):
    return ref.at[tuple([slice(None)] * axis + [pl.ds(idx * n, n)])]


def all_gather(x, axis, *, name):
    n = x.shape[axis]
    out_shape = x.shape[:axis] + (N_DEV * n,) + x.shape[axis + 1:]

    def body(x_ref, out_ref, send_sems, recv_sems, local_sem):
        xx, yy, cc = lax.axis_index("x"), lax.axis_index("y"), lax.axis_index("c")
        me, sibling = (xx, yy, cc), (xx, yy, 1 - cc)
        chips = [(1 - xx, yy), (xx, 1 - yy), (1 - xx, 1 - yy)]

        def slab(px, py, pc):
            return _slab(out_ref, axis, 4 * px + 2 * py + pc, n)

        def copy(k, block, to, src=None):
            return pltpu.make_async_remote_copy(
                src_ref=slab(*block) if src is None else src, dst_ref=slab(*block),
                send_sem=send_sems.at[k], recv_sem=recv_sems.at[k], device_id=to, device_id_type=MESH)

        mine = pltpu.make_async_copy(x_ref, slab(*me), local_sem)
        mine.start()
        first = [copy(0, me, sibling, src=x_ref)]
        first += [copy(1 + j, me, (*chip, cc), src=x_ref) for j, chip in enumerate(chips)]
        for cp in first:
            cp.start()
        passed = [copy(4 + j, (*chip, cc), sibling) for j, chip in enumerate(chips)]
        for j, chip in enumerate(chips):
            copy(1 + j, (*chip, cc), me).wait_recv()
            passed[j].start()
        copy(0, sibling, me).wait_recv()
        for j, chip in enumerate(chips):
            copy(4 + j, (*chip, 1 - cc), me).wait_recv()
        for cp in first + passed:
            cp.wait_send()
        mine.wait()

    return pl.pallas_call(
        body, name=name, out_shape=jax.ShapeDtypeStruct(out_shape, x.dtype), in_specs=[ANY], out_specs=ANY,
        scratch_shapes=[pltpu.SemaphoreType.DMA((7,)), pltpu.SemaphoreType.DMA((7,)), pltpu.SemaphoreType.DMA],
    )(x)


def push_slabs(g, axis, *, name):
    n = g.shape[axis] // N_DEV
    slab_shape = g.shape[:axis] + (n,) + g.shape[axis + 1:]

    def body(g_ref, out_ref, send_sems, recv_sems, local_sem):
        xx, yy, cc = lax.axis_index("x"), lax.axis_index("y"), lax.axis_index("c")
        me = 4 * xx + 2 * yy + cc
        mine = pltpu.make_async_copy(_slab(g_ref, axis, me, n), out_ref.at[me], local_sem)
        mine.start()
        copies = []
        for k in range(1, N_DEV):
            px = 1 - xx if k & 4 else xx
            py = 1 - yy if k & 2 else yy
            pc = 1 - cc if k & 1 else cc
            peer = 4 * px + 2 * py + pc
            copies.append(pltpu.make_async_remote_copy(
                src_ref=_slab(g_ref, axis, peer, n), dst_ref=out_ref.at[me],
                send_sem=send_sems.at[k - 1], recv_sem=recv_sems.at[k - 1], device_id=(px, py, pc), device_id_type=MESH))
        for cp in copies:
            cp.start()
        for cp in copies:
            cp.wait()
        mine.wait()

    return pl.pallas_call(
        body, name=name, out_shape=jax.ShapeDtypeStruct((N_DEV,) + slab_shape, g.dtype), in_specs=[ANY], out_specs=ANY,
        scratch_shapes=[pltpu.SemaphoreType.DMA((7,)), pltpu.SemaphoreType.DMA((7,)), pltpu.SemaphoreType.DMA],
    )(g)


def _blockdiag(b, nb):
    j, _, r, c = b.shape
    eye = jnp.eye(nb, dtype=bool)[None, :, None, :, None]
    return jnp.where(eye, b[:, :, :, None, :], jnp.zeros((), b.dtype)).reshape(j, nb * r, nb * c)


def _diagblocks(d, nb):
    j, rr, cc = d.shape
    return jnp.einsum('jarac->jarc', d.reshape(j, nb, rr // nb, nb, cc // nb))


def _s5_b_dense(bbar):
    return _blockdiag(bbar.transpose(0, 2, 1).reshape(NCH, 8, S5_GROUP, S5_STATE), 8)


def _s5_b_undense(d):
    return _diagblocks(d, 8).reshape(S5_GROUPS, S5_GROUP, S5_STATE).transpose(0, 2, 1)


def _s5_c_dense(c):
    return _blockdiag(c.transpose(0, 2, 1).reshape(NCH, 8, S5_STATE, S5_GROUP), 8)


def _s5_c_undense(d):
    return _diagblocks(d, 8).reshape(S5_GROUPS, S5_STATE, S5_GROUP).transpose(0, 2, 1)


def _rg_dense(w):
    return _blockdiag(w.reshape(NCH, 2, RG_BLOCK, RG_BLOCK), 2)


def _rg_undense(d):
    return _diagblocks(d, 2).reshape(RG_BLOCKS, RG_BLOCK, RG_BLOCK)


def _chunks(v):
    return v.reshape(NCH, 1, LANE)


def _tri(tm):
    r = jnp.arange(tm)
    m = (r[:, None] >= r[None, :]) & (r[:, None] // HG_SUB == r[None, :] // HG_SUB)
    m = m.astype(f32)
    return m[None], m.T[None]


SMALL = ['norm_w', 's5_lambda_re', 's5_lambda_im', 's5_log_step', 's5_b_re', 's5_b_im', 's5_c_re', 's5_c_im',
         's5_d', 's5_b_glu', 'rg_conv_w', 'rg_conv_b', 'rg_w_a', 'rg_b_a', 'rg_w_x', 'rg_b_x', 'rg_lambda',
         'hg_lower_bounds', 'hg_norm_w', 'final_norm_w']
WEIGHTS = ['norm_w', 'w_in', 's5_lambda_re', 's5_lambda_im', 's5_log_step', 's5_b_re', 's5_b_im', 's5_c_re',
           's5_c_im', 's5_d', 's5_w_glu', 's5_b_glu', 'rg_conv_w', 'rg_conv_b', 'rg_w_a', 'rg_b_a', 'rg_w_x',
           'rg_b_x', 'rg_lambda', 'hg_lower_bounds', 'hg_norm_w', 'w_branch', 'w_out', 'final_norm_w']
PACK_ROWS = 512


def _pack(arrs):
    flat = jnp.concatenate([a.reshape(-1) for a in arrs])
    pad = (-flat.shape[0]) % (PACK_ROWS * LANE)
    return jnp.pad(flat, (0, pad)).reshape(1, -1, LANE)


def _unpack(buf, shapes):
    flat = buf.reshape(-1)
    out, off = [], 0
    for s in shapes:
        n = math.prod(s)
        out.append(flat[off:off + n].reshape(s))
        off += n
    return out


def _step(x, tgt, w, m, v):
    t = x.shape[0]
    tri, tri_t = _tri(min(256, t))
    me = 4 * lax.axis_index("x") + 2 * lax.axis_index("y") + lax.axis_index("c")

    win, wglu, wbr, wout = [], [], [], []
    for l in range(DEPTH):
        win.append(all_gather(w['w_in'][l].astype(bf16), 1, name="ag_w_in"))
        wglu.append(all_gather(w['s5_w_glu'][l].astype(bf16), 0, name="ag_w_glu"))
        wbr.append(all_gather(w['w_branch'][l].astype(bf16), 2, name="ag_w_branch"))
        wout.append(all_gather(w['w_out'][l].astype(bf16), 0, name="ag_w_out"))
    conv_w = all_gather(w['rg_conv_w'].reshape(DEPTH * RG_CONV, LANE), 1, name="ag_conv_w")
    conv_w = conv_w.reshape(DEPTH, RG_CONV, W_MIX)

    lb_rows = [w['hg_lower_bounds'][l][None] for l in range(DEPTH)]
    lbs = whole(lb_prep_fn, lb_rows, [(1, W_MIX)] * DEPTH, name="lb_prep")

    saved = []
    for l in range(DEPTH):
        s = {}
        nw = w['norm_w'][l].reshape(1, 1, D_MODEL)
        (h,) = rowwise(ln_fn, [(x, 0, D_MODEL)], [nw], [], [(D_MODEL, bf16)], name="ln_fwd")
        z = mm(h, win[l], name="mm_in")
        s5p = [w['s5_lambda_re'][l][..., None], w['s5_lambda_im'][l][..., None], w['s5_log_step'][l][:, None, None],
               w['s5_b_re'][l], w['s5_b_im'][l]]
        gp = (S5_GROUPS, S5_STATE)
        abar_re, abar_im, bbar_re, bbar_im = whole(
            s5_prep_fn, s5p, [gp + (1,), gp + (1,), gp + (S5_GROUP,), gp + (S5_GROUP,)], name="s5_prep")
        a_re, a_im = abar_re.reshape(NCH, 1, S5_SC), abar_im.reshape(NCH, 1, S5_SC)
        bd_re, bd_im = _s5_b_dense(bbar_re), _s5_b_dense(bbar_im)
        cd_re, cd_im = _s5_c_dense(w['s5_c_re'][l]), _s5_c_dense(w['s5_c_im'][l])
        yssm, xre, xim = s5_scan_fwd(z, bd_re, bd_im, cd_re, cd_im, a_re, a_im)
        s5post_p = [w['s5_d'][l].reshape(1, 1, W_MIX), wglu[l].astype(f32)[None], w['s5_b_glu'][l].reshape(1, 1, W_MIX)]
        s5post_rows = [(yssm, 0, W_MIX), (z, C_UA, W_MIX), (z, C_GA, W_MIX)]
        (ya,) = rowwise(s5_post_fn, s5post_rows, s5post_p, [], [(W_MIX, bf16)], name="s5_post_fwd")
        cw, cb = conv_w[l].reshape(RG_CONV, NCH, LANE).transpose(1, 0, 2), _chunks(w['rg_conv_b'][l])
        xc = rg_conv_fwd(z, cw, cb)
        rg_p = [_rg_dense(w['rg_w_a'][l]), _chunks(w['rg_b_a'][l]), _rg_dense(w['rg_w_x'][l]),
                _chunks(w['rg_b_x'][l]), _chunks(w['rg_lambda'][l])]
        ra, rb = rowwise(rg_gate_fn, [(xc, 0, W_MIX)], rg_p, [], [(W_MIX, f32)] * 2, name="rg_gate_fwd",
                         ncol=NCH, rowid=True)
        hb = rg_scan_fwd(ra, rb)
        hg_rows = [(z, C_Q, W_MIX), (z, C_F, W_MIX)]
        hg_p = [_chunks(lbs[l].reshape(W_MIX))]
        qs, kk, gcum = rowwise(hg_pre_fn, hg_rows, hg_p, [tri, tri_t], [(W_MIX, f32)] * 3, name="hg_pre_fwd", ncol=NCH)
        oc, sall = hg_chunk_fwd(qs, kk, gcum, z)
        bp_rows = [(hb, 0, W_MIX), (z, C_GB, W_MIX), (oc, 0, W_MIX), (z, C_GC, W_MIX)]
        bp_p = [_chunks(w['hg_norm_w'][l])]
        yb, yc = rowwise(branch_prep_fn, bp_rows, bp_p, [], [(W_MIX, bf16)] * 2, name="branch_prep_fwd", ncol=NCH)
        ys = [ya, yb, yc]
        br = [mm(ys[n], wbr[l][n], name="mm_branch") for n in range(N_BRANCH)]
        mg_rows = [(br[n], 0, D_MODEL) for n in range(N_BRANCH)] + [(z, C_GATE + n * D_MODEL, D_MODEL) for n in range(N_BRANCH)]
        (merged,) = rowwise(merge_fn, mg_rows, [], [], [(D_MODEL, bf16)], name="merge_fwd", ncol=2)
        x_new = mm(merged, wout[l], add=x, name="mm_out")
        s.update(x=x, h=h, z=z, s5p=s5p, s5=(bd_re, bd_im, cd_re, cd_im, a_re, a_im), xre=xre, xim=xim,
                 s5post_rows=s5post_rows, s5post_p=s5post_p, cw=cw, xc=xc, rg_p=rg_p, ra=ra, hb=hb,
                 hg_rows=hg_rows, hg_p=hg_p, qs=qs, kk=kk, gcum=gcum, sall=sall, bp_rows=bp_rows, bp_p=bp_p,
                 ys=ys, mg_rows=mg_rows, merged=merged, nw=nw)
        saved.append(s)
        x = x_new

    fnw = w['final_norm_w'].reshape(1, 1, D_MODEL)
    ones = jnp.ones((t, 1), f32)
    dx, d_fnw, loss_sum = rowwise_vjp(loss_fn, [(x, 0, D_MODEL), (tgt, 0, D_MODEL)], [fnw], [], [(ones, 0, 1)],
                                      [(0, f32)], name="loss_head", sum_primal=0)
    loss = lax.psum(loss_sum.reshape(()), ("x", "y", "c"))

    small_g = {k: [None] * DEPTH for k in SMALL if k != 'final_norm_w'}
    big_slots = {k: [None] * DEPTH for k in ('w_in', 's5_w_glu', 'w_branch', 'w_out')}
    d_lbs = [None] * DEPTH
    for l in reversed(range(DEPTH)):
        s = saved[l]
        z = s['z']
        dxb = dx.astype(bf16)
        d_merged = mm(dxb, wout[l].T, name="mm_out_dx", out_dtype=bf16)
        big_slots['w_out'][l] = push_slabs(mm(s['merged'].T, dxb, name="mm_out_dw"), 0, name="rs_w_out")
        mg = rowwise_vjp(merge_fn, s['mg_rows'], [], [], [(d_merged, 0, D_MODEL)],
                         [(n, bf16) for n in range(2 * N_BRANCH)], name="merge_bwd", ncol=2)
        d_br, d_gl = mg[:N_BRANCH], mg[N_BRANCH:]
        d_ys = [mm(d_br[n], wbr[l][n].T, name="mm_branch_dx") for n in range(N_BRANCH)]
        d_wbr = jnp.stack([mm(s['ys'][n].T, d_br[n], name="mm_branch_dw") for n in range(N_BRANCH)])
        big_slots['w_branch'][l] = push_slabs(d_wbr, 2, name="rs_w_branch")
        d_hb, d_gb, d_oc, d_gc, d_hnw = rowwise_vjp(
            branch_prep_fn, s['bp_rows'], s['bp_p'], [], [(d_ys[1], 0, W_MIX), (d_ys[2], 0, W_MIX)],
            [(0, f32), (1, bf16), (2, f32), (3, bf16)], name="branch_prep_bwd", ncol=NCH)
        small_g['hg_norm_w'][l] = d_hnw.reshape(W_MIX)
        d_qs, d_kk, d_gcum, d_i = hg_chunk_bwd(d_oc, s['qs'], s['kk'], s['gcum'], z, s['sall'])
        d_q, d_f, d_lb = rowwise_vjp(
            hg_pre_fn, s['hg_rows'], s['hg_p'], [tri, tri_t], [(d_qs, 0, W_MIX), (d_kk, 0, W_MIX), (d_gcum, 0, W_MIX)],
            [(0, bf16), (1, bf16)], name="hg_pre_bwd", ncol=NCH)
        d_lbs[l] = d_lb.reshape(1, W_MIX)
        d_ra, d_rb = rg_scan_bwd(d_hb, s['ra'], s['hb'])
        rg = rowwise_vjp(rg_gate_fn, [(s['xc'], 0, W_MIX)], s['rg_p'], [], [(d_ra, 0, W_MIX), (d_rb, 0, W_MIX)],
                         [(0, f32)], name="rg_gate_bwd", ncol=NCH, rowid=True)
        d_xc, d_wa, d_ba, d_wx, d_bx, d_lam = rg
        d_xb, d_cw, d_cb = rg_conv_bwd(d_xc, z, s['cw'])
        small_g['rg_w_a'][l], small_g['rg_w_x'][l] = _rg_undense(d_wa), _rg_undense(d_wx)
        small_g['rg_b_a'][l], small_g['rg_b_x'][l] = d_ba.reshape(W_MIX), d_bx.reshape(W_MIX)
        small_g['rg_lambda'][l] = d_lam.reshape(W_MIX)
        small_g['rg_conv_w'][l] = d_cw.transpose(1, 0, 2).reshape(RG_CONV, W_MIX)
        small_g['rg_conv_b'][l] = d_cb.reshape(W_MIX)
        d_yssm, d_u1, d_ga, d_d, d_wglu, d_bglu = rowwise_vjp(
            s5_post_fn, s['s5post_rows'], s['s5post_p'], [], [(d_ys[0], 0, W_MIX)],
            [(0, bf16), (1, bf16), (2, bf16)], name="s5_post_bwd")
        big_slots['s5_w_glu'][l] = push_slabs(d_wglu[0], 0, name="rs_w_glu")
        small_g['s5_d'][l], small_g['s5_b_glu'][l] = d_d.reshape(W_MIX), d_bglu.reshape(W_MIX)
        d_ua, d_bdre, d_bdim, d_cdre, d_cdim, d_are, d_aim = s5_scan_bwd(d_yssm, d_u1, z, s['xre'], s['xim'], *s['s5'])
        small_g['s5_c_re'][l], small_g['s5_c_im'][l] = _s5_c_undense(d_cdre), _s5_c_undense(d_cdim)
        gp = (S5_GROUPS, S5_STATE, 1)
        s5g = whole_vjp(s5_prep_fn, s['s5p'],
                        [d_are.reshape(gp), d_aim.reshape(gp), _s5_b_undense(d_bdre), _s5_b_undense(d_bdim)],
                        name="s5_prep_bwd")
        small_g['s5_lambda_re'][l] = s5g[0].reshape(S5_GROUPS, S5_STATE)
        small_g['s5_lambda_im'][l] = s5g[1].reshape(S5_GROUPS, S5_STATE)
        small_g['s5_log_step'][l] = s5g[2].reshape(S5_GROUPS)
        small_g['s5_b_re'][l], small_g['s5_b_im'][l] = s5g[3], s5g[4]
        dz = jnp.concatenate([d_ua, d_ga, d_xb, d_gb, d_q, d_f, d_i, d_gc] + list(d_gl), axis=1)
        d_h = mm(dz, win[l].T, name="mm_in_dx", tk=2048)
        big_slots['w_in'][l] = push_slabs(mm(s['h'].T, dz, name="mm_in_dw"), 1, name="rs_w_in")
        dx, d_nw = rowwise_vjp(ln_res_fn, [(s['x'], 0, D_MODEL)], [s['nw']], [], [(d_h, 0, D_MODEL), (dx, 0, D_MODEL)],
                               [(0, f32)], name="ln_bwd")
        small_g['norm_w'][l] = d_nw.reshape(D_MODEL)
    d_lb_raw = whole_vjp(lb_prep_fn, lb_rows, d_lbs, name="lb_prep_bwd")
    small_g['hg_lower_bounds'] = [r.reshape(W_MIX) for r in d_lb_raw]

    g_small = [jnp.stack(small_g[k]) for k in SMALL if k != 'final_norm_w'] + [d_fnw.reshape(D_MODEL)]
    shapes = [g.shape for g in g_small]
    g_all = all_gather(_pack(g_small)[0], 0, name="ag_small_grads")
    g_all = g_all.reshape(N_DEV, -1, LANE)

    def local(d, k):
        return jnp.zeros(shapes[SMALL.index(k)], f32) if k == 'rg_conv_w' else d[k]
    packed = [_pack([local(d, k) for k in SMALL]) for d in (w, m, v)]
    outs = adamw(*packed, [g_all], name="adamw_small", tr=512)
    res = {}
    for kind, buf in zip(('grad', 'delta', 'new_m', 'new_v'), outs):
        for k, a in zip(SMALL, _unpack(buf, shapes)):
            res[kind + '_' + k] = a
    g_cw = lax.dynamic_slice_in_dim(res['grad_rg_conv_w'], me * LANE, LANE, axis=2)
    cw3 = lambda a: a.reshape(1, DEPTH * RG_CONV, LANE)
    outs = adamw(cw3(w['rg_conv_w']), cw3(m['rg_conv_w']), cw3(v['rg_conv_w']), [cw3(g_cw)], name="adamw_conv_w", tr=16)
    for kind, buf in zip(('grad', 'delta', 'new_m', 'new_v'), outs):
        res[kind + '_rg_conv_w'] = buf.reshape(DEPTH, RG_CONV, LANE)

    for k, tr in (('w_in', 32), ('s5_w_glu', 32), ('w_branch', 128), ('w_out', 32)):
        shp = w[k].shape
        r3 = lambda a: a.reshape(DEPTH, -1, shp[-1])
        slots = [a.reshape(N_DEV, -1, shp[-1]) for a in big_slots[k]]
        outs = adamw(r3(w[k]), r3(m[k]), r3(v[k]), slots, name="adamw_" + k, tr=tr)
        for kind, buf in zip(('grad', 'delta', 'new_m', 'new_v'), outs):
            res[kind + '_' + k] = buf.reshape(shp)

    return (loss, dx[None]) + tuple(res[kind + '_' + k] for kind in ('grad', 'delta', 'new_m', 'new_v') for k in WEIGHTS)


def kernel(x, norm_w, w_in, s5_lambda_re, s5_lambda_im, s5_log_step, s5_b_re, s5_b_im, s5_c_re, s5_c_im, s5_d, s5_w_glu, s5_b_glu, rg_conv_w, rg_conv_b, rg_w_a, rg_b_a, rg_w_x, rg_b_x, rg_lambda, hg_lower_bounds, hg_norm_w, w_branch, w_out, final_norm_w, loss_target, m_norm_w, m_w_in, m_s5_lambda_re, m_s5_lambda_im, m_s5_log_step, m_s5_b_re, m_s5_b_im, m_s5_c_re, m_s5_c_im, m_s5_d, m_s5_w_glu, m_s5_b_glu, m_rg_conv_w, m_rg_conv_b, m_rg_w_a, m_rg_b_a, m_rg_w_x, m_rg_b_x, m_rg_lambda, m_hg_lower_bounds, m_hg_norm_w, m_w_branch, m_w_out, m_final_norm_w, v_norm_w, v_w_in, v_s5_lambda_re, v_s5_lambda_im, v_s5_log_step, v_s5_b_re, v_s5_b_im, v_s5_c_re, v_s5_c_im, v_s5_d, v_s5_w_glu, v_s5_b_glu, v_rg_conv_w, v_rg_conv_b, v_rg_w_a, v_rg_b_a, v_rg_w_x, v_rg_b_x, v_rg_lambda, v_hg_lower_bounds, v_hg_norm_w, v_w_branch, v_w_out, v_final_norm_w):
    w = dict(zip(WEIGHTS, (norm_w, w_in, s5_lambda_re, s5_lambda_im, s5_log_step, s5_b_re, s5_b_im, s5_c_re, s5_c_im, s5_d, s5_w_glu, s5_b_glu, rg_conv_w, rg_conv_b, rg_w_a, rg_b_a, rg_w_x, rg_b_x, rg_lambda, hg_lower_bounds, hg_norm_w, w_branch, w_out, final_norm_w)))
    m = dict(zip(WEIGHTS, (m_norm_w, m_w_in, m_s5_lambda_re, m_s5_lambda_im, m_s5_log_step, m_s5_b_re, m_s5_b_im, m_s5_c_re, m_s5_c_im, m_s5_d, m_s5_w_glu, m_s5_b_glu, m_rg_conv_w, m_rg_conv_b, m_rg_w_a, m_rg_b_a, m_rg_w_x, m_rg_b_x, m_rg_lambda, m_hg_lower_bounds, m_hg_norm_w, m_w_branch, m_w_out, m_final_norm_w)))
    v = dict(zip(WEIGHTS, (v_norm_w, v_w_in, v_s5_lambda_re, v_s5_lambda_im, v_s5_log_step, v_s5_b_re, v_s5_b_im, v_s5_c_re, v_s5_c_im, v_s5_d, v_s5_w_glu, v_s5_b_glu, v_rg_conv_w, v_rg_conv_b, v_rg_w_a, v_rg_b_a, v_rg_w_x, v_rg_b_x, v_rg_lambda, v_hg_lower_bounds, v_hg_norm_w, v_w_branch, v_w_out, v_final_norm_w)))
    return _step(x[0], loss_target[0], w, m, v)
```

```python
import functools
import math

import jax
import jax.numpy as jnp
from jax import lax
from jax.experimental import pallas as pl
from jax.experimental.pallas import tpu as pltpu

f32 = jnp.float32
bf16 = jnp.bfloat16

D_MODEL = 2048
W_MIX = 1024
DEPTH = 4
N_BRANCH = 3
N_IN = 8 * W_MIX + N_BRANCH * D_MODEL
S5_GROUPS, S5_STATE, S5_GROUP = 64, 64, 16
RG_BLOCKS, RG_BLOCK, RG_CONV, RG_C = 16, 64, 4, 8.0
HG_HEADS, HG_DK = 8, 128
HG_SUB = 16
EPS = 1e-6
ADAM_LR, ADAM_B1, ADAM_B2, ADAM_EPS, ADAM_WD, ADAM_STEP = 0.001, 0.9, 0.999, 1e-08, 0.01, 10

N_DEV = 8
LANE = 128
NCH = W_MIX // LANE
TM_CHUNK = 1024
VMEM_LIMIT = 56 * 1024 * 1024
MESH = pl.DeviceIdType.MESH
ANY = pl.BlockSpec(memory_space=pl.ANY)
HIGHEST = lax.Precision.HIGHEST

C_UA, C_GA, C_XB, C_GB, C_Q, C_F, C_I, C_GC, C_GATE = (W_MIX * k for k in range(9))


def _cparams(sem=None):
    return pltpu.CompilerParams(dimension_semantics=sem, vmem_limit_bytes=VMEM_LIMIT)


@jax.custom_vjp
def bdot(a, w):
    return jnp.dot(a.astype(bf16), w.astype(bf16), preferred_element_type=f32)


def _bdot_fwd(a, w):
    return bdot(a, w), (a, w)


def _bdot_bwd(res, g):
    a, w = res
    gb = g.astype(bf16)
    da = lax.dot_general(gb, w.astype(bf16), (((1,), (1,)), ((), ())), preferred_element_type=f32)
    dw = lax.dot_general(a.astype(bf16), gb, (((0,), (0,)), ((), ())), preferred_element_type=f32)
    return da, dw


bdot.defvjp(_bdot_fwd, _bdot_bwd)


def _blockmm(c, a):
    n = c.shape[0]
    return jnp.concatenate([jnp.dot(c, a[i:i + n], preferred_element_type=f32, precision=HIGHEST)
                            for i in range(0, a.shape[0], n)], axis=0)


@jax.custom_vjp
def cdot(c, ct, a):
    return _blockmm(c, a)


def _cdot_fwd(c, ct, a):
    return cdot(c, ct, a), (c, ct)


def _cdot_bwd(res, g):
    c, ct = res
    return jnp.zeros_like(c), jnp.zeros_like(ct), _blockmm(ct, g)


cdot.defvjp(_cdot_fwd, _cdot_bwd)


def mm(a, b, *, name, out_dtype=f32, add=None, tm=512, tn=1024, tk=4096):
    m, k = a.shape
    _, n = b.shape
    tm, tn, tk = min(tm, m), min(tn, n), min(tk, k)
    assert m % tm == 0 and n % tn == 0 and k % tk == 0
    nk = k // tk

    def body(*refs):
        if add is None:
            a_ref, b_ref, o_ref, acc_ref = refs
        else:
            a_ref, b_ref, r_ref, o_ref, acc_ref = refs
        kk = pl.program_id(2)
        part = jnp.dot(a_ref[...], b_ref[...], preferred_element_type=f32)

        @pl.when(kk == 0)
        def _():
            acc_ref[...] = part

        @pl.when(kk > 0)
        def _():
            acc_ref[...] = acc_ref[...] + part

        @pl.when(kk == nk - 1)
        def _():
            acc = acc_ref[...]
            if add is not None:
                acc = acc + r_ref[...]
            o_ref[...] = acc.astype(out_dtype)

    in_specs = [pl.BlockSpec((tm, tk), lambda i, j, q: (i, q)), pl.BlockSpec((tk, tn), lambda i, j, q: (q, j))]
    args = [a, b]
    if add is not None:
        in_specs.append(pl.BlockSpec((tm, tn), lambda i, j, q: (i, j)))
        args.append(add)
    return pl.pallas_call(
        body, name=name, grid=(m // tm, n // tn, nk), in_specs=in_specs,
        out_specs=pl.BlockSpec((tm, tn), lambda i, j, q: (i, j)),
        out_shape=jax.ShapeDtypeStruct((m, n), out_dtype),
        scratch_shapes=[pltpu.VMEM((tm, tn), f32)],
        compiler_params=_cparams(("parallel", "parallel", "arbitrary")),
    )(*args)


def _row_spec(tm, wc, col_off):
    base = col_off // wc
    assert col_off % wc == 0
    return pl.BlockSpec((tm, wc), lambda j, i: (i, base + j))


def _slab_spec(arr):
    r, c = arr.shape[1:]
    if arr.shape[0] == 1:
        return pl.BlockSpec((1, r, c), lambda j, i: (0, 0, 0))
    return pl.BlockSpec((1, r, c), lambda j, i: (j, 0, 0))


def rowwise(fn, rows, params, consts, outs, *, name, tm=256, ncol=1, rowid=False):
    t = rows[0][0].shape[0]
    tm = min(tm, t)
    nr, npar, nc, no = len(rows), len(params), len(consts), len(outs)

    def body(*refs):
        r = [refs[k][...].astype(f32) for k in range(nr)]
        p = [refs[nr + k][0] for k in range(npar + nc)]
        extra = ()
        if rowid:
            extra = (pl.program_id(1) * tm + lax.broadcasted_iota(jnp.int32, (tm, 1), 0),)
        res = fn(*extra, *r, *p)
        for k in range(no):
            refs[nr + npar + nc + k][...] = res[k].astype(outs[k][1])

    in_specs = [_row_spec(tm, w // ncol, off) for (_, off, w) in rows]
    in_specs += [_slab_spec(a) for a in list(params) + list(consts)]
    out_specs = [pl.BlockSpec((tm, w // ncol), lambda j, i: (i, j)) for (w, _) in outs]
    out_shape = [jax.ShapeDtypeStruct((t, w), dt) for (w, dt) in outs]
    return pl.pallas_call(
        body, name=name, grid=(ncol, t // tm), in_specs=in_specs, out_specs=out_specs, out_shape=out_shape,
        compiler_params=_cparams(("parallel", "parallel")),
    )(*[r[0] for r in rows], *params, *consts)


def rowwise_vjp(fn, rows, params, consts, cts, d_rows, *, name, tm=256, ncol=1, rowid=False, sum_primal=None):
    t = rows[0][0].shape[0]
    tm = min(tm, t)
    nr, npar, nc, nct, ndr = len(rows), len(params), len(consts), len(cts), len(d_rows)

    def body(*refs):
        i = pl.program_id(1)
        r = [refs[k][...].astype(f32) for k in range(nr)]
        p = [refs[nr + k][0] for k in range(npar)]
        c = [refs[nr + npar + k][0] for k in range(nc)]
        g = [refs[nr + npar + nc + k][...].astype(f32) for k in range(nct)]
        orefs = refs[nr + npar + nc + nct:]
        extra = ()
        if rowid:
            extra = (i * tm + lax.broadcasted_iota(jnp.int32, (tm, 1), 0),)
        res, vjp = jax.vjp(lambda *v: fn(*extra, *v, *c), *r, *p)
        grads = vjp(tuple(g))
        for k, (idx, dt) in enumerate(d_rows):
            orefs[k][...] = grads[idx].astype(dt)
        acc = [grads[nr + k] for k in range(npar)]
        if sum_primal is not None:
            acc.append(jnp.sum(res[sum_primal], axis=0, keepdims=True))

        @pl.when(i == 0)
        def _():
            for k, a in enumerate(acc):
                orefs[ndr + k][0] = a

        @pl.when(i > 0)
        def _():
            for k, a in enumerate(acc):
                orefs[ndr + k][0] = orefs[ndr + k][0] + a

    in_specs = [_row_spec(tm, w // ncol, off) for (_, off, w) in rows]
    in_specs += [_slab_spec(a) for a in list(params) + list(consts)]
    in_specs += [_row_spec(tm, w // ncol, off) for (_, off, w) in cts]
    out_specs = [pl.BlockSpec((tm, rows[idx][2] // ncol), lambda j, i: (i, j)) for (idx, _) in d_rows]
    out_shape = [jax.ShapeDtypeStruct((t, rows[idx][2]), dt) for (idx, dt) in d_rows]
    for a in params:
        out_specs.append(pl.BlockSpec((1,) + a.shape[1:], lambda j, i: (j, 0, 0)))
        out_shape.append(jax.ShapeDtypeStruct(a.shape, f32))
    if sum_primal is not None:
        w = cts[sum_primal][2]
        out_specs.append(pl.BlockSpec((1, 1, w // ncol), lambda j, i: (j, 0, 0)))
        out_shape.append(jax.ShapeDtypeStruct((ncol, 1, w // ncol), f32))
    return pl.pallas_call(
        body, name=name, grid=(ncol, t // tm), in_specs=in_specs, out_specs=out_specs, out_shape=out_shape,
        compiler_params=_cparams(("parallel", "arbitrary")),
    )(*[r[0] for r in rows], *params, *consts, *[c[0] for c in cts])


VM = pl.BlockSpec(memory_space=pltpu.VMEM)


def whole(fn, ins, outs, *, name):
    def body(*refs):
        res = fn(*[r[...] for r in refs[:len(ins)]])
        for k, o in enumerate(refs[len(ins):]):
            o[...] = res[k]
    return pl.pallas_call(body, name=name, in_specs=[VM] * len(ins), out_specs=[VM] * len(outs),
                          out_shape=[jax.ShapeDtypeStruct(s, f32) for s in outs],
                          compiler_params=_cparams())(*ins)


def whole_vjp(fn, ins, cts, *, name):
    n = len(ins)

    def body(*refs):
        _, vjp = jax.vjp(fn, *[r[...] for r in refs[:n]])
        grads = vjp(tuple(r[...] for r in refs[n:n + len(cts)]))
        for k, o in enumerate(refs[n + len(cts):]):
            o[...] = grads[k]
    return pl.pallas_call(body, name=name, in_specs=[VM] * (n + len(cts)), out_specs=[VM] * n,
                          out_shape=[jax.ShapeDtypeStruct(a.shape, f32) for a in ins],
                          compiler_params=_cparams())(*ins, *cts)


def ln_fn(x, w):
    return (x * lax.rsqrt(jnp.mean(x * x, axis=-1, keepdims=True) + EPS) * w,)


def ln_res_fn(x, w):
    return ln_fn(x, w)[0], x


def loss_fn(x, tgt, w):
    y = ln_fn(x, w)[0]
    return (0.5 * jnp.mean(jnp.square(y - tgt), axis=-1, keepdims=True),)


def s5_prep_fn(lam_re, lam_im, log_step, b_re, b_im):
    step = jnp.exp(log_step)
    mag = jnp.exp(lam_re * step)
    ang = lam_im * step
    abar_re = mag * jnp.cos(ang)
    abar_im = mag * jnp.sin(ang)
    num_re = abar_re - 1.0
    num_im = abar_im
    den = lam_re * lam_re + lam_im * lam_im
    coef_re = (num_re * lam_re + num_im * lam_im) / den
    coef_im = (num_im * lam_re - num_re * lam_im) / den
    bbar_re = coef_re * b_re - coef_im * b_im
    bbar_im = coef_re * b_im + coef_im * b_re
    return abar_re, abar_im, bbar_re, bbar_im


def lb_prep_fn(r0, r1, r2, r3):
    m = jnp.maximum(jnp.maximum(r0, r1), jnp.maximum(r2, r3))
    e0, e1, e2, e3 = jnp.exp(r0 - m), jnp.exp(r1 - m), jnp.exp(r2 - m), jnp.exp(r3 - m)
    s = e0 + e1 + e2 + e3
    p0, p1, p2, p3 = e0 / s, e1 / s, e2 / s, e3 / s
    c1 = p0 + p1
    c2 = c1 + p2
    c3 = c2 + p3
    return p0 - p0, c1 - p0, c2 - p0, c3 - p0


def s5_post_fn(yssm, u, ga, d, wglu, bglu):
    y = jax.nn.gelu(yssm + d * u)
    y = y * jax.nn.sigmoid(bdot(y, wglu) + bglu)
    return (y * jax.nn.silu(ga),)


def rg_gate_fn(tglob, xc, wa, ba, wx, bx, lam):
    r = jax.nn.sigmoid(bdot(xc, wa) + ba)
    i = jax.nn.sigmoid(bdot(xc, wx) + bx)
    log_a = -RG_C * r * jax.nn.softplus(-lam)
    a = jnp.exp(log_a)
    mult = jnp.sqrt(-jnp.tanh(log_a) * (a * a + 1.0))
    mult = jnp.where(tglob == 0, 1.0, mult)
    return a, mult * (i * xc)


def hg_pre_fn(q, fl, lb, tri, tri_t):
    f = lb + (1.0 - lb) * jax.nn.sigmoid(fl)
    return jax.nn.silu(q), 1.0 - f, cdot(tri, tri_t, jnp.log(f))


def branch_prep_fn(hb, gb, oc, gc, nw):
    yb = hb * jax.nn.silu(gb)
    on = oc * lax.rsqrt(jnp.mean(oc * oc, axis=-1, keepdims=True) + EPS) * nw
    return yb, on * jax.nn.silu(gc)


def merge_fn(b0, b1, b2, g0, g1, g2):
    return (jax.nn.sigmoid(g0) * b0 + jax.nn.sigmoid(g1) * b1 + jax.nn.sigmoid(g2) * b2,)


S5_TB = 512
S5_SC = 512


def _shift_down(v, k, row, fill=0.0):
    return jnp.where(row >= k, pltpu.roll(v, k, 0), fill)


def _shift_up(v, k, row, n, fill=0.0):
    return jnp.where(row < n - k, pltpu.roll(v, n - k, 0), fill)


def s5_scan_fwd(z, bd_re, bd_im, cd_re, cd_im, a_re, a_im):
    t = z.shape[0]
    tb = min(S5_TB, t)

    def body(u_ref, bre, bim, cre, cim, are, aim, y_ref, xre_ref, xim_ref, car_re, car_im):
        @pl.when(pl.program_id(1) == 0)
        def _():
            car_re[...] = jnp.zeros_like(car_re)
            car_im[...] = jnp.zeros_like(car_im)

        u = u_ref[...].astype(bf16)
        row = lax.broadcasted_iota(jnp.int32, (tb, 1), 0)
        ar, ai = are[0], aim[0]
        cr, ci = car_re[...], car_im[...]
        xr = jnp.dot(u, bre[0], preferred_element_type=f32) + jnp.where(row == 0, ar * cr - ai * ci, 0.0)
        xi = jnp.dot(u, bim[0], preferred_element_type=f32) + jnp.where(row == 0, ar * ci + ai * cr, 0.0)
        pr, pi_ = ar, ai
        k = 1
        while k < tb:
            sr, si = _shift_down(xr, k, row), _shift_down(xi, k, row)
            xr, xi = xr + pr * sr - pi_ * si, xi + pr * si + pi_ * sr
            pr, pi_ = pr * pr - pi_ * pi_, 2.0 * pr * pi_
            k *= 2
        car_re[...] = xr[tb - 1:tb, :]
        car_im[...] = xi[tb - 1:tb, :]
        xre_ref[...] = xr
        xim_ref[...] = xi
        y_ref[...] = (jnp.dot(xr.astype(bf16), cre[0], preferred_element_type=f32)
                      - jnp.dot(xi.astype(bf16), cim[0], preferred_element_type=f32))

    chunk = lambda r, c: pl.BlockSpec((1, r, c), lambda j, i: (j, 0, 0))
    return pl.pallas_call(
        body, name="s5_scan_fwd", grid=(NCH, t // tb),
        in_specs=[pl.BlockSpec((tb, LANE), lambda j, i: (i, C_UA // LANE + j)),
                  chunk(LANE, S5_SC), chunk(LANE, S5_SC), chunk(S5_SC, LANE), chunk(S5_SC, LANE),
                  chunk(1, S5_SC), chunk(1, S5_SC)],
        out_specs=[pl.BlockSpec((tb, LANE), lambda j, i: (i, j)),
                   pl.BlockSpec((tb, S5_SC), lambda j, i: (i, j)),
                   pl.BlockSpec((tb, S5_SC), lambda j, i: (i, j))],
        out_shape=[jax.ShapeDtypeStruct((t, W_MIX), f32),
                   jax.ShapeDtypeStruct((t, NCH * S5_SC), f32),
                   jax.ShapeDtypeStruct((t, NCH * S5_SC), f32)],
        scratch_shapes=[pltpu.VMEM((1, S5_SC), f32), pltpu.VMEM((1, S5_SC), f32)],
        compiler_params=_cparams(("parallel", "arbitrary")),
    )(z, bd_re.astype(bf16), bd_im.astype(bf16), cd_re.astype(bf16), cd_im.astype(bf16), a_re, a_im)


def s5_scan_bwd(dy, du1, z, xre, xim, bd_re, bd_im, cd_re, cd_im, a_re, a_im):
    t = z.shape[0]
    tb = min(S5_TB, t)
    nt = t // tb

    def body(dy_ref, du1_ref, u_ref, xre_ref, xim_ref, hre_ref, him_ref, bre, bim, cre, cim, are, aim,
             du_ref, dbre, dbim, dcre, dcim, dare, daim, car_re, car_im):
        step = pl.program_id(1)
        tt = nt - 1 - step

        @pl.when(step == 0)
        def _():
            car_re[...] = jnp.zeros_like(car_re)
            car_im[...] = jnp.zeros_like(car_im)

        nt_dims = (((1,), (1,)), ((), ()))
        tn_dims = (((0,), (0,)), ((), ()))
        dyb = dy_ref[...].astype(bf16)
        row = lax.broadcasted_iota(jnp.int32, (tb, 1), 0)
        xr, xi = xre_ref[...], xim_ref[...]
        ar, ai = are[0], aim[0]
        cr, ci = ar, -ai
        kr, ki = car_re[...], car_im[...]
        lr = lax.dot_general(dyb, cre[0], nt_dims, preferred_element_type=f32)
        li = -lax.dot_general(dyb, cim[0], nt_dims, preferred_element_type=f32)
        lr = lr + jnp.where(row == tb - 1, cr * kr - ci * ki, 0.0)
        li = li + jnp.where(row == tb - 1, cr * ki + ci * kr, 0.0)
        pr, pi_ = cr, ci
        k = 1
        while k < tb:
            sr, si = _shift_up(lr, k, row, tb), _shift_up(li, k, row, tb)
            lr, li = lr + pr * sr - pi_ * si, li + pr * si + pi_ * sr
            pr, pi_ = pr * pr - pi_ * pi_, 2.0 * pr * pi_
            k *= 2
        car_re[...] = lr[0:1, :]
        car_im[...] = li[0:1, :]
        lrb, lib = lr.astype(bf16), li.astype(bf16)
        ub = u_ref[...].astype(bf16)
        du = (lax.dot_general(lrb, bre[0], nt_dims, preferred_element_type=f32)
              + lax.dot_general(lib, bim[0], nt_dims, preferred_element_type=f32))
        du_ref[...] = (du + du1_ref[...].astype(f32)).astype(du_ref.dtype)
        live = (tt > 0).astype(f32)
        xpr = jnp.where(row == 0, hre_ref[7:8, :] * live, pltpu.roll(xr, 1, 0))
        xpi = jnp.where(row == 0, him_ref[7:8, :] * live, pltpu.roll(xi, 1, 0))
        acc = [
            lax.dot_general(ub, lrb, tn_dims, preferred_element_type=f32),
            lax.dot_general(ub, lib, tn_dims, preferred_element_type=f32),
            lax.dot_general(xr.astype(bf16), dyb, tn_dims, preferred_element_type=f32),
            -lax.dot_general(xi.astype(bf16), dyb, tn_dims, preferred_element_type=f32),
            jnp.sum(lr * xpr + li * xpi, axis=0, keepdims=True),
            jnp.sum(li * xpr - lr * xpi, axis=0, keepdims=True),
        ]
        outs = [dbre, dbim, dcre, dcim, dare, daim]

        @pl.when(step == 0)
        def _():
            for o, a in zip(outs, acc):
                o[0] = a

        @pl.when(step > 0)
        def _():
            for o, a in zip(outs, acc):
                o[0] = o[0] + a

    chunk = lambda r, c: pl.BlockSpec((1, r, c), lambda j, i: (j, 0, 0))
    rev = lambda w, base=0: pl.BlockSpec((tb, w), lambda j, i: (nt - 1 - i, base + j))
    halo = pl.BlockSpec((8, S5_SC), lambda j, i: (jnp.maximum((nt - 1 - i) * (tb // 8) - 1, 0), j))
    return pl.pallas_call(
        body, name="s5_scan_bwd", grid=(NCH, nt),
        in_specs=[rev(LANE), rev(LANE), rev(LANE, C_UA // LANE), rev(S5_SC), rev(S5_SC), halo, halo,
                  chunk(LANE, S5_SC), chunk(LANE, S5_SC), chunk(S5_SC, LANE), chunk(S5_SC, LANE),
                  chunk(1, S5_SC), chunk(1, S5_SC)],
        out_specs=[rev(LANE), chunk(LANE, S5_SC), chunk(LANE, S5_SC), chunk(S5_SC, LANE), chunk(S5_SC, LANE),
                   chunk(1, S5_SC), chunk(1, S5_SC)],
        out_shape=[jax.ShapeDtypeStruct((t, W_MIX), bf16),
                   jax.ShapeDtypeStruct((NCH, LANE, S5_SC), f32), jax.ShapeDtypeStruct((NCH, LANE, S5_SC), f32),
                   jax.ShapeDtypeStruct((NCH, S5_SC, LANE), f32), jax.ShapeDtypeStruct((NCH, S5_SC, LANE), f32),
                   jax.ShapeDtypeStruct((NCH, 1, S5_SC), f32), jax.ShapeDtypeStruct((NCH, 1, S5_SC), f32)],
        scratch_shapes=[pltpu.VMEM((1, S5_SC), f32), pltpu.VMEM((1, S5_SC), f32)],
        compiler_params=_cparams(("parallel", "arbitrary")),
    )(dy, du1, z, xre, xim, xre, xim, bd_re.astype(bf16), bd_im.astype(bf16), cd_re.astype(bf16),
      cd_im.astype(bf16), a_re, a_im)


RG_TB = 512


def rg_conv_fwd(z, cw, cb):
    t = z.shape[0]
    tb = min(RG_TB, t)

    def body(x_ref, h_ref, cw_ref, cb_ref, o_ref):
        live = (pl.program_id(1) > 0).astype(f32)
        ext = jnp.concatenate([h_ref[...] * live, x_ref[...]], axis=0)
        w = cw_ref[0]
        acc = cb_ref[0] + w[3:4, :] * ext[8:, :]
        for k in range(3):
            acc = acc + w[k:k + 1, :] * pltpu.roll(ext, 3 - k, 0)[8:, :]
        o_ref[...] = acc

    base = C_XB // LANE
    chunk = lambda r: pl.BlockSpec((1, r, LANE), lambda j, i: (j, 0, 0))
    return pl.pallas_call(
        body, name="rg_conv_fwd", grid=(NCH, t // tb),
        in_specs=[pl.BlockSpec((tb, LANE), lambda j, i: (i, base + j)),
                  pl.BlockSpec((8, LANE), lambda j, i: (jnp.maximum(i * (tb // 8) - 1, 0), base + j)),
                  chunk(RG_CONV), chunk(1)],
        out_specs=pl.BlockSpec((tb, LANE), lambda j, i: (i, j)),
        out_shape=jax.ShapeDtypeStruct((t, W_MIX), f32),
        compiler_params=_cparams(("parallel", "parallel")),
    )(z, z, cw, cb)


def rg_conv_bwd(dxc, z, cw):
    t = z.shape[0]
    tb = min(RG_TB, t)
    nt = t // tb

    def body(g_ref, gn_ref, x_ref, h_ref, cw_ref, dx_ref, dcw_ref, dcb_ref):
        i = pl.program_id(1)
        g = g_ref[...]
        gext = jnp.concatenate([g, gn_ref[...] * (i < nt - 1).astype(f32)], axis=0)
        xext = jnp.concatenate([h_ref[...] * (i > 0).astype(f32), x_ref[...]], axis=0)
        w = cw_ref[0]
        dx = w[3:4, :] * g
        rows = [None] * RG_CONV
        rows[3] = jnp.sum(g * xext[8:, :], axis=0, keepdims=True)
        for k in range(3):
            s = 3 - k
            dx = dx + w[k:k + 1, :] * pltpu.roll(gext, tb + 8 - s, 0)[:tb, :]
            rows[k] = jnp.sum(g * pltpu.roll(xext, s, 0)[8:, :], axis=0, keepdims=True)
        dx_ref[...] = dx.astype(dx_ref.dtype)
        dcw = jnp.concatenate(rows, axis=0)
        dcb = jnp.sum(g, axis=0, keepdims=True)

        @pl.when(i == 0)
        def _():
            dcw_ref[0] = dcw
            dcb_ref[0] = dcb

        @pl.when(i > 0)
        def _():
            dcw_ref[0] = dcw_ref[0] + dcw
            dcb_ref[0] = dcb_ref[0] + dcb

    base = C_XB // LANE
    chunk = lambda r: pl.BlockSpec((1, r, LANE), lambda j, i: (j, 0, 0))
    return pl.pallas_call(
        body, name="rg_conv_bwd", grid=(NCH, nt),
        in_specs=[pl.BlockSpec((tb, LANE), lambda j, i: (i, j)),
                  pl.BlockSpec((8, LANE), lambda j, i: (jnp.minimum((i + 1) * (tb // 8), t // 8 - 1), j)),
                  pl.BlockSpec((tb, LANE), lambda j, i: (i, base + j)),
                  pl.BlockSpec((8, LANE), lambda j, i: (jnp.maximum(i * (tb // 8) - 1, 0), base + j)),
                  chunk(RG_CONV)],
        out_specs=[pl.BlockSpec((tb, LANE), lambda j, i: (i, j)), chunk(RG_CONV), chunk(1)],
        out_shape=[jax.ShapeDtypeStruct((t, W_MIX), bf16), jax.ShapeDtypeStruct((NCH, RG_CONV, LANE), f32),
                   jax.ShapeDtypeStruct((NCH, 1, LANE), f32)],
        compiler_params=_cparams(("parallel", "arbitrary")),
    )(dxc, dxc, z, z, cw)


def rg_scan_fwd(a, b):
    t = a.shape[0]
    tb = min(RG_TB, t)

    def body(a_ref, b_ref, h_ref, car):
        @pl.when(pl.program_id(1) == 0)
        def _():
            car[...] = jnp.zeros_like(car)

        row = lax.broadcasted_iota(jnp.int32, (tb, 1), 0)
        aa, bb = a_ref[...], b_ref[...]
        k = 1
        while k < tb:
            bb = bb + aa * _shift_down(bb, k, row)
            aa = aa * _shift_down(aa, k, row, 1.0)
            k *= 2
        h = bb + aa * car[...]
        car[...] = h[tb - 1:tb, :]
        h_ref[...] = h

    spec = pl.BlockSpec((tb, LANE), lambda j, i: (i, j))
    return pl.pallas_call(
        body, name="rg_scan_fwd", grid=(NCH, t // tb), in_specs=[spec, spec], out_specs=spec,
        out_shape=jax.ShapeDtypeStruct((t, W_MIX), f32), scratch_shapes=[pltpu.VMEM((1, LANE), f32)],
        compiler_params=_cparams(("parallel", "arbitrary")),
    )(a, b)


def rg_scan_bwd(dh, a, h):
    t = a.shape[0]
    tb = min(RG_TB, t)
    nt = t // tb

    def body(g_ref, a_ref, an_ref, h_ref, hp_ref, da_ref, db_ref, car):
        step = pl.program_id(1)
        tt = nt - 1 - step

        @pl.when(step == 0)
        def _():
            car[...] = jnp.zeros_like(car)

        row = lax.broadcasted_iota(jnp.int32, (tb, 1), 0)
        an = an_ref[0:1, :] * (tt < nt - 1).astype(f32)
        aa = jnp.where(row == tb - 1, an, pltpu.roll(a_ref[...], tb - 1, 0))
        bb = g_ref[...]
        k = 1
        while k < tb:
            bb = bb + aa * _shift_up(bb, k, row, tb)
            aa = aa * _shift_up(aa, k, row, tb, 1.0)
            k *= 2
        lam = bb + aa * car[...]
        car[...] = lam[0:1, :]
        hp = jnp.where(row == 0, hp_ref[7:8, :] * (tt > 0).astype(f32), pltpu.roll(h_ref[...], 1, 0))
        da_ref[...] = lam * hp
        db_ref[...] = lam

    rev = pl.BlockSpec((tb, LANE), lambda j, i: (nt - 1 - i, j))
    nxt = pl.BlockSpec((8, LANE), lambda j, i: (jnp.minimum((nt - i) * (tb // 8), t // 8 - 1), j))
    prv = pl.BlockSpec((8, LANE), lambda j, i: (jnp.maximum((nt - 1 - i) * (tb // 8) - 1, 0), j))
    return pl.pallas_call(
        body, name="rg_scan_bwd", grid=(NCH, nt), in_specs=[rev, rev, nxt, rev, prv], out_specs=[rev, rev],
        out_shape=[jax.ShapeDtypeStruct((t, W_MIX), f32)] * 2, scratch_shapes=[pltpu.VMEM((1, LANE), f32)],
        compiler_params=_cparams(("parallel", "arbitrary")),
    )(dh, a, a, h, h)


HG_TB = 256


def _heads(v):
    return jnp.stack([v[:, LANE * h:LANE * (h + 1)] for h in range(HG_HEADS)])


def _unheads(v):
    return jnp.concatenate([v[h] for h in range(HG_HEADS)], axis=-1)


def _bmm(eq, a, b):
    return jnp.einsum(eq, a.astype(bf16), b.astype(bf16), preferred_element_type=f32)


def hg_chunk_fwd(qs, kk, gcum, z):
    t = qs.shape[0]
    tb = min(HG_TB, t)
    nc = tb // HG_SUB

    def body(q_ref, k_ref, g_ref, v_ref, o_ref, sall_ref, st_ref):
        @pl.when(pl.program_id(0) == 0)
        def _():
            st_ref[...] = jnp.zeros_like(st_ref)

        ri = lax.broadcasted_iota(jnp.int32, (1, HG_SUB, 1), 1)

        def chunk(c, carry):
            rows = pl.ds(pl.multiple_of(c * HG_SUB, HG_SUB), HG_SUB)
            q, k, g, v = _heads(q_ref[rows, :]), _heads(k_ref[rows, :]), _heads(g_ref[rows, :]), _heads(v_ref[rows, :])
            st = st_ref[...]
            sall_ref[c] = st
            o = _bmm('htk,hvk->htv', q * jnp.exp(g), st)
            for s in range(HG_SUB):
                p = jnp.where(ri >= s, jnp.exp(jnp.minimum(g - g[:, s:s + 1, :], 0.0)), 0.0)
                col = jnp.sum(q * k[:, s:s + 1, :] * p, axis=-1, keepdims=True)
                o = o + col * v[:, s:s + 1, :]
            gl = g[:, HG_SUB - 1:HG_SUB, :]
            st_ref[...] = st * jnp.exp(gl) + _bmm('htv,htk->hvk', v, k * jnp.exp(gl - g))
            o_ref[rows, :] = _unheads(o)
            return carry

        lax.fori_loop(0, nc, chunk, 0)

    spec = lambda base=0: pl.BlockSpec((tb, W_MIX), lambda i: (i, base))
    return pl.pallas_call(
        body, name="hg_chunk_fwd", grid=(t // tb,),
        in_specs=[spec(), spec(), spec(), spec(C_I // W_MIX)],
        out_specs=[spec(), pl.BlockSpec((nc, HG_HEADS, HG_DK, HG_DK), lambda i: (i, 0, 0, 0))],
        out_shape=[jax.ShapeDtypeStruct((t, W_MIX), f32),
                   jax.ShapeDtypeStruct((t // HG_SUB, HG_HEADS, HG_DK, HG_DK), f32)],
        scratch_shapes=[pltpu.VMEM((HG_HEADS, HG_DK, HG_DK), f32)],
        compiler_params=_cparams(("arbitrary",)),
    )(qs, kk, gcum, z)


def hg_chunk_bwd(do, qs, kk, gcum, z, sall):
    t = qs.shape[0]
    tb = min(HG_TB, t)
    nc = tb // HG_SUB
    nt = t // tb

    def body(do_ref, q_ref, k_ref, g_ref, v_ref, sall_ref, dq_ref, dk_ref, dg_ref, dv_ref, dst_ref):
        @pl.when(pl.program_id(0) == 0)
        def _():
            dst_ref[...] = jnp.zeros_like(dst_ref)

        ri = lax.broadcasted_iota(jnp.int32, (1, HG_SUB, 1), 1)

        def chunk(cc, carry):
            c = nc - 1 - cc
            rows = pl.ds(pl.multiple_of(c * HG_SUB, HG_SUB), HG_SUB)
            q, k, g, v = _heads(q_ref[rows, :]), _heads(k_ref[rows, :]), _heads(g_ref[rows, :]), _heads(v_ref[rows, :])
            d_o = _heads(do_ref[rows, :])
            st = sall_ref[c]
            dsn = dst_ref[...]
            eg = jnp.exp(g)
            qe = q * eg
            gl = g[:, HG_SUB - 1:HG_SUB, :]
            egl = jnp.exp(gl)
            dec = jnp.exp(gl - g)
            kd = k * dec
            dqe = _bmm('htv,hvk->htk', d_o, st)
            dst_ref[...] = _bmm('htv,htk->hvk', d_o, qe) + dsn * egl
            dgl_dec = jnp.sum(dsn * st, axis=1, keepdims=True) * egl
            dv = _bmm('htk,hvk->htv', kd, dsn)
            dkd = _bmm('htv,hvk->htk', v, dsn)
            a1 = jnp.zeros_like(q)
            a2 = jnp.zeros_like(q)
            for s in range(HG_SUB):
                p = jnp.where(ri >= s, jnp.exp(jnp.minimum(g - g[:, s:s + 1, :], 0.0)), 0.0)
                krow = k[:, s:s + 1, :]
                col = jnp.sum(q * krow * p, axis=-1, keepdims=True)
                dcol = jnp.sum(d_o * v[:, s:s + 1, :], axis=-1, keepdims=True)
                dv = jnp.where(ri == s, dv + jnp.sum(col * d_o, axis=1, keepdims=True), dv)
                t1 = dcol * p
                a1 = a1 + t1 * krow
                a2 = jnp.where(ri == s, jnp.sum(t1 * q, axis=1, keepdims=True), a2)
            dgl = jnp.sum(dkd * kd, axis=1, keepdims=True) + dgl_dec
            dg = dqe * qe + q * a1 - k * a2 - dkd * kd
            dg = jnp.where(ri == HG_SUB - 1, dg + dgl, dg)
            dq_ref[rows, :] = _unheads(dqe * eg + a1)
            dk_ref[rows, :] = _unheads(dkd * dec + a2)
            dg_ref[rows, :] = _unheads(dg)
            dv_ref[rows, :] = _unheads(dv).astype(dv_ref.dtype)
            return carry

        lax.fori_loop(0, nc, chunk, 0)

    spec = lambda base=0: pl.BlockSpec((tb, W_MIX), lambda i: (nt - 1 - i, base))
    return pl.pallas_call(
        body, name="hg_chunk_bwd", grid=(nt,),
        in_specs=[spec(), spec(), spec(), spec(), spec(C_I // W_MIX),
                  pl.BlockSpec((nc, HG_HEADS, HG_DK, HG_DK), lambda i: (nt - 1 - i, 0, 0, 0))],
        out_specs=[spec(), spec(), spec(), spec()],
        out_shape=[jax.ShapeDtypeStruct((t, W_MIX), f32)] * 3 + [jax.ShapeDtypeStruct((t, W_MIX), bf16)],
        scratch_shapes=[pltpu.VMEM((HG_HEADS, HG_DK, HG_DK), f32)],
        compiler_params=_cparams(("arbitrary",)),
    )(do, qs, kk, gcum, z, sall)


def adamw(w, m, v, slots, *, name, tr):
    nl, r, c = w.shape
    tr = min(tr, r)
    ns = slots[0].shape[0]
    c1 = 1.0 / (1.0 - ADAM_B1 ** ADAM_STEP)
    c2 = 1.0 / (1.0 - ADAM_B2 ** ADAM_STEP)

    def body(*refs):
        w_ref, m_ref, v_ref = refs[:3]
        s_refs = refs[3:3 + nl]
        g_ref, d_ref, mo_ref, vo_ref = refs[3 + nl:]
        for l in range(nl):
            g = s_refs[l][0].astype(f32)
            for s in range(1, ns):
                g = g + s_refs[l][s].astype(f32)
            mn = ADAM_B1 * m_ref[l] + (1.0 - ADAM_B1) * g
            vn = ADAM_B2 * v_ref[l] + (1.0 - ADAM_B2) * (g * g)
            g_ref[l] = g
            mo_ref[l] = mn
            vo_ref[l] = vn
            d_ref[l] = -ADAM_LR * ((mn * c1) / (jnp.sqrt(vn * c2) + ADAM_EPS) + ADAM_WD * w_ref[l])

    full = pl.BlockSpec((nl, tr, c), lambda i: (0, i, 0))
    slot = pl.BlockSpec((ns, tr, c), lambda i: (0, i, 0))
    return pl.pallas_call(
        body, name=name, grid=(r // tr,), in_specs=[full] * 3 + [slot] * nl, out_specs=[full] * 4,
        out_shape=[jax.ShapeDtypeStruct(w.shape, f32)] * 4, compiler_params=_cparams(("parallel",)),
    )(w, m, v, *slots)


def _slab(ref, axis, idx, n):
    return ref.at[tuple([slice(None)] * axis + [pl.ds(idx * n, n)])]


def all_gather(x, axis, *, name):
    n = x.shape[axis]
    out_shape = x.shape[:axis] + (N_DEV * n,) + x.shape[axis + 1:]

    def body(x_ref, out_ref, send_sems, recv_sems, local_sem):
        xx, yy, cc = lax.axis_index("x"), lax.axis_index("y"), lax.axis_index("c")
        me, sibling = (xx, yy, cc), (xx, yy, 1 - cc)
        chips = [(1 - xx, yy), (xx, 1 - yy), (1 - xx, 1 - yy)]

        def slab(px, py, pc):
            return _slab(out_ref, axis, 4 * px + 2 * py + pc, n)

        def copy(k, block, to, src=None):
            return pltpu.make_async_remote_copy(
                src_ref=slab(*block) if src is None else src, dst_ref=slab(*block),
                send_sem=send_sems.at[k], recv_sem=recv_sems.at[k], device_id=to, device_id_type=MESH)

        mine = pltpu.make_async_copy(x_ref, slab(*me), local_sem)
        mine.start()
        first = [copy(0, me, sibling, src=x_ref)]
        first += [copy(1 + j, me, (*chip, cc), src=x_ref) for j, chip in enumerate(chips)]
        for cp in first:
            cp.start()
        passed = [copy(4 + j, (*chip, cc), sibling) for j, chip in enumerate(chips)]
        for j, chip in enumerate(chips):
            copy(1 + j, (*chip, cc), me).wait_recv()
            passed[j].start()
        copy(0, sibling, me).wait_recv()
        for j, chip in enumerate(chips):
            copy(4 + j, (*chip, 1 - cc), me).wait_recv()
        for cp in first + passed:
            cp.wait_send()
        mine.wait()

    return pl.pallas_call(
        body, name=name, out_shape=jax.ShapeDtypeStruct(out_shape, x.dtype), in_specs=[ANY], out_specs=ANY,
        scratch_shapes=[pltpu.SemaphoreType.DMA((7,)), pltpu.SemaphoreType.DMA((7,)), pltpu.SemaphoreType.DMA],
    )(x)


N_CHIP = 4


def reduce_to_owners(g, axis, *, name):
    n = g.shape[axis] // N_DEV
    slab_shape = g.shape[:axis] + (n,) + g.shape[axis + 1:]
    half = jax.ShapeDtypeStruct((N_CHIP,) + slab_shape, g.dtype)

    def swap_body(g_ref, own_ref, got_ref, send_sems, recv_sems, local_sems):
        xx, yy, cc = lax.axis_index("x"), lax.axis_index("y"), lax.axis_index("c")
        copies = []
        for q in range(N_CHIP):
            copies.append(pltpu.make_async_copy(_slab(g_ref, axis, 2 * q + cc, n), own_ref.at[q], local_sems.at[q]))
            copies.append(pltpu.make_async_remote_copy(
                src_ref=_slab(g_ref, axis, 2 * q + 1 - cc, n), dst_ref=got_ref.at[q],
                send_sem=send_sems.at[q], recv_sem=recv_sems.at[q], device_id=(xx, yy, 1 - cc), device_id_type=MESH))
        for cp in copies:
            cp.start()
        for cp in copies:
            cp.wait()

    own, got = pl.pallas_call(
        swap_body, name=name + "_swap", out_shape=[half, half], in_specs=[ANY], out_specs=[ANY, ANY],
        scratch_shapes=[pltpu.SemaphoreType.DMA((N_CHIP,))] * 3,
    )(g)

    r3 = lambda a: a.reshape(N_CHIP, -1, slab_shape[-1])
    rows = math.prod(slab_shape[:-1])
    tr = min(256, rows)

    def add_body(a_ref, b_ref, o_ref):
        o_ref[...] = (a_ref[...].astype(f32) + b_ref[...].astype(f32)).astype(bf16)

    blk = pl.BlockSpec((1, tr, slab_shape[-1]), lambda q, i: (q, i, 0))
    pair = pl.pallas_call(
        add_body, name=name + "_add", grid=(N_CHIP, rows // tr), in_specs=[blk, blk], out_specs=blk,
        out_shape=jax.ShapeDtypeStruct((N_CHIP, rows, slab_shape[-1]), bf16),
        compiler_params=_cparams(("parallel", "parallel")),
    )(r3(own), r3(got))

    def send_body(p_ref, out_ref, send_sems, recv_sems, local_sem):
        xx, yy, cc = lax.axis_index("x"), lax.axis_index("y"), lax.axis_index("c")
        qme = 2 * xx + yy
        mine = pltpu.make_async_copy(p_ref.at[qme], out_ref.at[qme], local_sem)
        mine.start()
        copies = []
        for k in range(1, N_CHIP):
            px = 1 - xx if k & 2 else xx
            py = 1 - yy if k & 1 else yy
            copies.append(pltpu.make_async_remote_copy(
                src_ref=p_ref.at[2 * px + py], dst_ref=out_ref.at[qme],
                send_sem=send_sems.at[k - 1], recv_sem=recv_sems.at[k - 1], device_id=(px, py, cc), device_id_type=MESH))
        for cp in copies:
            cp.start()
        for cp in copies:
            cp.wait()
        mine.wait()

    return pl.pallas_call(
        send_body, name=name + "_send", out_shape=jax.ShapeDtypeStruct(pair.shape, bf16), in_specs=[ANY], out_specs=ANY,
        scratch_shapes=[pltpu.SemaphoreType.DMA((N_CHIP - 1,)), pltpu.SemaphoreType.DMA((N_CHIP - 1,)), pltpu.SemaphoreType.DMA],
    )(pair)


def _blockdiag(b, nb):
    j, _, r, c = b.shape
    eye = jnp.eye(nb, dtype=bool)[None, :, None, :, None]
    return jnp.where(eye, b[:, :, :, None, :], jnp.zeros((), b.dtype)).reshape(j, nb * r, nb * c)


def _diagblocks(d, nb):
    j, rr, cc = d.shape
    return jnp.einsum('jarac->jarc', d.reshape(j, nb, rr // nb, nb, cc // nb))


def _s5_b_dense(bbar):
    return _blockdiag(bbar.transpose(0, 2, 1).reshape(NCH, 8, S5_GROUP, S5_STATE), 8)


def _s5_b_undense(d):
    return _diagblocks(d, 8).reshape(S5_GROUPS, S5_GROUP, S5_STATE).transpose(0, 2, 1)


def _s5_c_dense(c):
    return _blockdiag(c.transpose(0, 2, 1).reshape(NCH, 8, S5_STATE, S5_GROUP), 8)


def _s5_c_undense(d):
    return _diagblocks(d, 8).reshape(S5_GROUPS, S5_STATE, S5_GROUP).transpose(0, 2, 1)


def _rg_dense(w):
    return _blockdiag(w.reshape(NCH, 2, RG_BLOCK, RG_BLOCK), 2)


def _rg_undense(d):
    return _diagblocks(d, 2).reshape(RG_BLOCKS, RG_BLOCK, RG_BLOCK)


def _chunks(v):
    return v.reshape(NCH, 1, LANE)


def _tri(tm):
    r = jnp.arange(tm)
    m = (r[:, None] >= r[None, :]) & (r[:, None] // HG_SUB == r[None, :] // HG_SUB)
    m = m.astype(f32)
    return m[None], m.T[None]


SMALL = ['norm_w', 's5_lambda_re', 's5_lambda_im', 's5_log_step', 's5_b_re', 's5_b_im', 's5_c_re', 's5_c_im',
         's5_d', 's5_b_glu', 'rg_conv_w', 'rg_conv_b', 'rg_w_a', 'rg_b_a', 'rg_w_x', 'rg_b_x', 'rg_lambda',
         'hg_lower_bounds', 'hg_norm_w', 'final_norm_w']
WEIGHTS = ['norm_w', 'w_in', 's5_lambda_re', 's5_lambda_im', 's5_log_step', 's5_b_re', 's5_b_im', 's5_c_re',
           's5_c_im', 's5_d', 's5_w_glu', 's5_b_glu', 'rg_conv_w', 'rg_conv_b', 'rg_w_a', 'rg_b_a', 'rg_w_x',
           'rg_b_x', 'rg_lambda', 'hg_lower_bounds', 'hg_norm_w', 'w_branch', 'w_out', 'final_norm_w']
PACK_ROWS = 512


def _pack(arrs):
    flat = jnp.concatenate([a.reshape(-1) for a in arrs])
    pad = (-flat.shape[0]) % (PACK_ROWS * LANE)
    return jnp.pad(flat, (0, pad)).reshape(1, -1, LANE)


def _unpack(buf, shapes):
    flat = buf.reshape(-1)
    out, off = [], 0
    for s in shapes:
        n = math.prod(s)
        out.append(flat[off:off + n].reshape(s))
        off += n
    return out


def _step(x, tgt, w, m, v):
    t = x.shape[0]
    tri, tri_t = _tri(min(LANE, t))
    me = 4 * lax.axis_index("x") + 2 * lax.axis_index("y") + lax.axis_index("c")

    win, wglu, wbr, wout = [], [], [], []
    for l in range(DEPTH):
        win.append(all_gather(w['w_in'][l].astype(bf16), 1, name="ag_w_in"))
        wglu.append(all_gather(w['s5_w_glu'][l].astype(bf16), 0, name="ag_w_glu"))
        wbr.append(all_gather(w['w_branch'][l].astype(bf16), 2, name="ag_w_branch"))
        wout.append(all_gather(w['w_out'][l].astype(bf16), 0, name="ag_w_out"))
    conv_w = all_gather(w['rg_conv_w'].reshape(DEPTH * RG_CONV, LANE), 1, name="ag_conv_w")
    conv_w = conv_w.reshape(DEPTH, RG_CONV, W_MIX)

    lb_rows = [w['hg_lower_bounds'][l][None] for l in range(DEPTH)]
    lbs = whole(lb_prep_fn, lb_rows, [(1, W_MIX)] * DEPTH, name="lb_prep")

    saved = []
    for l in range(DEPTH):
        s = {}
        nw = w['norm_w'][l].reshape(1, 1, D_MODEL)
        (h,) = rowwise(ln_fn, [(x, 0, D_MODEL)], [nw], [], [(D_MODEL, bf16)], name="ln_fwd")
        z = mm(h, win[l], name="mm_in")
        s5p = [w['s5_lambda_re'][l][..., None], w['s5_lambda_im'][l][..., None], w['s5_log_step'][l][:, None, None],
               w['s5_b_re'][l], w['s5_b_im'][l]]
        gp = (S5_GROUPS, S5_STATE)
        abar_re, abar_im, bbar_re, bbar_im = whole(
            s5_prep_fn, s5p, [gp + (1,), gp + (1,), gp + (S5_GROUP,), gp + (S5_GROUP,)], name="s5_prep")
        a_re, a_im = abar_re.reshape(NCH, 1, S5_SC), abar_im.reshape(NCH, 1, S5_SC)
        bd_re, bd_im = _s5_b_dense(bbar_re), _s5_b_dense(bbar_im)
        cd_re, cd_im = _s5_c_dense(w['s5_c_re'][l]), _s5_c_dense(w['s5_c_im'][l])
        yssm, xre, xim = s5_scan_fwd(z, bd_re, bd_im, cd_re, cd_im, a_re, a_im)
        s5post_p = [w['s5_d'][l].reshape(1, 1, W_MIX), wglu[l].astype(f32)[None], w['s5_b_glu'][l].reshape(1, 1, W_MIX)]
        s5post_rows = [(yssm, 0, W_MIX), (z, C_UA, W_MIX), (z, C_GA, W_MIX)]
        (ya,) = rowwise(s5_post_fn, s5post_rows, s5post_p, [], [(W_MIX, bf16)], name="s5_post_fwd")
        cw, cb = conv_w[l].reshape(RG_CONV, NCH, LANE).transpose(1, 0, 2), _chunks(w['rg_conv_b'][l])
        xc = rg_conv_fwd(z, cw, cb)
        rg_p = [_rg_dense(w['rg_w_a'][l]), _chunks(w['rg_b_a'][l]), _rg_dense(w['rg_w_x'][l]),
                _chunks(w['rg_b_x'][l]), _chunks(w['rg_lambda'][l])]
        ra, rb = rowwise(rg_gate_fn, [(xc, 0, W_MIX)], rg_p, [], [(W_MIX, f32)] * 2, name="rg_gate_fwd",
                         ncol=NCH, tm=TM_CHUNK, rowid=True)
        hb = rg_scan_fwd(ra, rb)
        hg_rows = [(z, C_Q, W_MIX), (z, C_F, W_MIX)]
        hg_p = [_chunks(lbs[l].reshape(W_MIX))]
        qs, kk, gcum = rowwise(hg_pre_fn, hg_rows, hg_p, [tri, tri_t], [(W_MIX, f32)] * 3, name="hg_pre_fwd", ncol=NCH, tm=TM_CHUNK)
        oc, sall = hg_chunk_fwd(qs, kk, gcum, z)
        bp_rows = [(hb, 0, W_MIX), (z, C_GB, W_MIX), (oc, 0, W_MIX), (z, C_GC, W_MIX)]
        bp_p = [_chunks(w['hg_norm_w'][l])]
        yb, yc = rowwise(branch_prep_fn, bp_rows, bp_p, [], [(W_MIX, bf16)] * 2, name="branch_prep_fwd", ncol=NCH, tm=TM_CHUNK)
        ys = [ya, yb, yc]
        br = [mm(ys[n], wbr[l][n], name="mm_branch") for n in range(N_BRANCH)]
        mg_rows = [(br[n], 0, D_MODEL) for n in range(N_BRANCH)] + [(z, C_GATE + n * D_MODEL, D_MODEL) for n in range(N_BRANCH)]
        (merged,) = rowwise(merge_fn, mg_rows, [], [], [(D_MODEL, bf16)], name="merge_fwd", ncol=2)
        x_new = mm(merged, wout[l], add=x, name="mm_out")
        s.update(x=x, h=h, z=z, s5p=s5p, s5=(bd_re, bd_im, cd_re, cd_im, a_re, a_im), xre=xre, xim=xim,
                 s5post_rows=s5post_rows, s5post_p=s5post_p, cw=cw, xc=xc, rg_p=rg_p, ra=ra, hb=hb,
                 hg_rows=hg_rows, hg_p=hg_p, qs=qs, kk=kk, gcum=gcum, sall=sall, bp_rows=bp_rows, bp_p=bp_p,
                 ys=ys, mg_rows=mg_rows, merged=merged, nw=nw)
        saved.append(s)
        x = x_new

    fnw = w['final_norm_w'].reshape(1, 1, D_MODEL)
    ones = jnp.ones((t, 1), f32)
    dx, d_fnw, loss_sum = rowwise_vjp(loss_fn, [(x, 0, D_MODEL), (tgt, 0, D_MODEL)], [fnw], [], [(ones, 0, 1)],
                                      [(0, f32)], name="loss_head", sum_primal=0)
    loss = lax.psum(loss_sum.reshape(()), ("x", "y", "c"))

    small_g = {k: [None] * DEPTH for k in SMALL if k != 'final_norm_w'}
    big_slots = {k: [None] * DEPTH for k in ('w_in', 's5_w_glu', 'w_branch', 'w_out')}
    d_lbs = [None] * DEPTH
    for l in reversed(range(DEPTH)):
        s = saved[l]
        z = s['z']
        dxb = dx.astype(bf16)
        d_merged = mm(dxb, wout[l].T, name="mm_out_dx", out_dtype=bf16)
        big_slots['w_out'][l] = reduce_to_owners(mm(s['merged'].T, dxb, name="mm_out_dw", out_dtype=bf16), 0, name="rs_w_out")
        mg = rowwise_vjp(merge_fn, s['mg_rows'], [], [], [(d_merged, 0, D_MODEL)],
                         [(n, bf16) for n in range(2 * N_BRANCH)], name="merge_bwd", ncol=2)
        d_br, d_gl = mg[:N_BRANCH], mg[N_BRANCH:]
        d_ys = [mm(d_br[n], wbr[l][n].T, name="mm_branch_dx") for n in range(N_BRANCH)]
        d_wbr = jnp.stack([mm(s['ys'][n].T, d_br[n], name="mm_branch_dw", out_dtype=bf16) for n in range(N_BRANCH)])
        big_slots['w_branch'][l] = reduce_to_owners(d_wbr, 2, name="rs_w_branch")
        d_hb, d_gb, d_oc, d_gc, d_hnw = rowwise_vjp(
            branch_prep_fn, s['bp_rows'], s['bp_p'], [], [(d_ys[1], 0, W_MIX), (d_ys[2], 0, W_MIX)],
            [(0, f32), (1, bf16), (2, f32), (3, bf16)], name="branch_prep_bwd", ncol=NCH, tm=TM_CHUNK)
        small_g['hg_norm_w'][l] = d_hnw.reshape(W_MIX)
        d_qs, d_kk, d_gcum, d_i = hg_chunk_bwd(d_oc, s['qs'], s['kk'], s['gcum'], z, s['sall'])
        d_q, d_f, d_lb = rowwise_vjp(
            hg_pre_fn, s['hg_rows'], s['hg_p'], [tri, tri_t], [(d_qs, 0, W_MIX), (d_kk, 0, W_MIX), (d_gcum, 0, W_MIX)],
            [(0, bf16), (1, bf16)], name="hg_pre_bwd", ncol=NCH, tm=TM_CHUNK)
        d_lbs[l] = d_lb.reshape(1, W_MIX)
        d_ra, d_rb = rg_scan_bwd(d_hb, s['ra'], s['hb'])
        rg = rowwise_vjp(rg_gate_fn, [(s['xc'], 0, W_MIX)], s['rg_p'], [], [(d_ra, 0, W_MIX), (d_rb, 0, W_MIX)],
                         [(0, f32)], name="rg_gate_bwd", ncol=NCH, tm=TM_CHUNK, rowid=True)
        d_xc, d_wa, d_ba, d_wx, d_bx, d_lam = rg
        d_xb, d_cw, d_cb = rg_conv_bwd(d_xc, z, s['cw'])
        small_g['rg_w_a'][l], small_g['rg_w_x'][l] = _rg_undense(d_wa), _rg_undense(d_wx)
        small_g['rg_b_a'][l], small_g['rg_b_x'][l] = d_ba.reshape(W_MIX), d_bx.reshape(W_MIX)
        small_g['rg_lambda'][l] = d_lam.reshape(W_MIX)
        small_g['rg_conv_w'][l] = d_cw.transpose(1, 0, 2).reshape(RG_CONV, W_MIX)
        small_g['rg_conv_b'][l] = d_cb.reshape(W_MIX)
        d_yssm, d_u1, d_ga, d_d, d_wglu, d_bglu = rowwise_vjp(
            s5_post_fn, s['s5post_rows'], s['s5post_p'], [], [(d_ys[0], 0, W_MIX)],
            [(0, bf16), (1, bf16), (2, bf16)], name="s5_post_bwd")
        big_slots['s5_w_glu'][l] = reduce_to_owners(d_wglu[0], 0, name="rs_w_glu")
        small_g['s5_d'][l], small_g['s5_b_glu'][l] = d_d.reshape(W_MIX), d_bglu.reshape(W_MIX)
        d_ua, d_bdre, d_bdim, d_cdre, d_cdim, d_are, d_aim = s5_scan_bwd(d_yssm, d_u1, z, s['xre'], s['xim'], *s['s5'])
        small_g['s5_c_re'][l], small_g['s5_c_im'][l] = _s5_c_undense(d_cdre), _s5_c_undense(d_cdim)
        gp = (S5_GROUPS, S5_STATE, 1)
        s5g = whole_vjp(s5_prep_fn, s['s5p'],
                        [d_are.reshape(gp), d_aim.reshape(gp), _s5_b_undense(d_bdre), _s5_b_undense(d_bdim)],
                        name="s5_prep_bwd")
        small_g['s5_lambda_re'][l] = s5g[0].reshape(S5_GROUPS, S5_STATE)
        small_g['s5_lambda_im'][l] = s5g[1].reshape(S5_GROUPS, S5_STATE)
        small_g['s5_log_step'][l] = s5g[2].reshape(S5_GROUPS)
        small_g['s5_b_re'][l], small_g['s5_b_im'][l] = s5g[3], s5g[4]
        dz = jnp.concatenate([d_ua, d_ga, d_xb, d_gb, d_q, d_f, d_i, d_gc] + list(d_gl), axis=1)
        d_h = mm(dz, win[l].T, name="mm_in_dx", tk=2048)
        big_slots['w_in'][l] = reduce_to_owners(mm(s['h'].T, dz, name="mm_in_dw", out_dtype=bf16), 1, name="rs_w_in")
        dx, d_nw = rowwise_vjp(ln_res_fn, [(s['x'], 0, D_MODEL)], [s['nw']], [], [(d_h, 0, D_MODEL), (dx, 0, D_MODEL)],
                               [(0, f32)], name="ln_bwd")
        small_g['norm_w'][l] = d_nw.reshape(D_MODEL)
    d_lb_raw = whole_vjp(lb_prep_fn, lb_rows, d_lbs, name="lb_prep_bwd")
    small_g['hg_lower_bounds'] = [r.reshape(W_MIX) for r in d_lb_raw]

    g_small = [jnp.stack(small_g[k]) for k in SMALL if k != 'final_norm_w'] + [d_fnw.reshape(D_MODEL)]
    shapes = [g.shape for g in g_small]
    g_all = all_gather(_pack(g_small)[0], 0, name="ag_small_grads")
    g_all = g_all.reshape(N_DEV, -1, LANE)

    def local(d, k):
        return jnp.zeros(shapes[SMALL.index(k)], f32) if k == 'rg_conv_w' else d[k]
    packed = [_pack([local(d, k) for k in SMALL]) for d in (w, m, v)]
    outs = adamw(*packed, [g_all], name="adamw_small", tr=512)
    res = {}
    for kind, buf in zip(('grad', 'delta', 'new_m', 'new_v'), outs):
        for k, a in zip(SMALL, _unpack(buf, shapes)):
            res[kind + '_' + k] = a
    g_cw = lax.dynamic_slice_in_dim(res['grad_rg_conv_w'], me * LANE, LANE, axis=2)
    cw3 = lambda a: a.reshape(1, DEPTH * RG_CONV, LANE)
    outs = adamw(cw3(w['rg_conv_w']), cw3(m['rg_conv_w']), cw3(v['rg_conv_w']), [cw3(g_cw)], name="adamw_conv_w", tr=16)
    for kind, buf in zip(('grad', 'delta', 'new_m', 'new_v'), outs):
        res[kind + '_rg_conv_w'] = buf.reshape(DEPTH, RG_CONV, LANE)

    for k, tr in (('w_in', 32), ('s5_w_glu', 32), ('w_branch', 128), ('w_out', 32)):
        shp = w[k].shape
        r3 = lambda a: a.reshape(DEPTH, -1, shp[-1])
        slots = big_slots[k]
        outs = adamw(r3(w[k]), r3(m[k]), r3(v[k]), slots, name="adamw_" + k, tr=tr)
        for kind, buf in zip(('grad', 'delta', 'new_m', 'new_v'), outs):
            res[kind + '_' + k] = buf.reshape(shp)

    return (loss, dx[None]) + tuple(res[kind + '_' + k] for kind in ('grad', 'delta', 'new_m', 'new_v') for k in WEIGHTS)


def kernel(x, norm_w, w_in, s5_lambda_re, s5_lambda_im, s5_log_step, s5_b_re, s5_b_im, s5_c_re, s5_c_im, s5_d, s5_w_glu, s5_b_glu, rg_conv_w, rg_conv_b, rg_w_a, rg_b_a, rg_w_x, rg_b_x, rg_lambda, hg_lower_bounds, hg_norm_w, w_branch, w_out, final_norm_w, loss_target, m_norm_w, m_w_in, m_s5_lambda_re, m_s5_lambda_im, m_s5_log_step, m_s5_b_re, m_s5_b_im, m_s5_c_re, m_s5_c_im, m_s5_d, m_s5_w_glu, m_s5_b_glu, m_rg_conv_w, m_rg_conv_b, m_rg_w_a, m_rg_b_a, m_rg_w_x, m_rg_b_x, m_rg_lambda, m_hg_lower_bounds, m_hg_norm_w, m_w_branch, m_w_out, m_final_norm_w, v_norm_w, v_w_in, v_s5_lambda_re, v_s5_lambda_im, v_s5_log_step, v_s5_b_re, v_s5_b_im, v_s5_c_re, v_s5_c_im, v_s5_d, v_s5_w_glu, v_s5_b_glu, v_rg_conv_w, v_rg_conv_b, v_rg_w_a, v_rg_b_a, v_rg_w_x, v_rg_b_x, v_rg_lambda, v_hg_lower_bounds, v_hg_norm_w, v_w_branch, v_w_out, v_final_norm_w):
    w = dict(zip(WEIGHTS, (norm_w, w_in, s5_lambda_re, s5_lambda_im, s5_log_step, s5_b_re, s5_b_im, s5_c_re, s5_c_im, s5_d, s5_w_glu, s5_b_glu, rg_conv_w, rg_conv_b, rg_w_a, rg_b_a, rg_w_x, rg_b_x, rg_lambda, hg_lower_bounds, hg_norm_w, w_branch, w_out, final_norm_w)))
    m = dict(zip(WEIGHTS, (m_norm_w, m_w_in, m_s5_lambda_re, m_s5_lambda_im, m_s5_log_step, m_s5_b_re, m_s5_b_im, m_s5_c_re, m_s5_c_im, m_s5_d, m_s5_w_glu, m_s5_b_glu, m_rg_conv_w, m_rg_conv_b, m_rg_w_a, m_rg_b_a, m_rg_w_x, m_rg_b_x, m_rg_lambda, m_hg_lower_bounds, m_hg_norm_w, m_w_branch, m_w_out, m_final_norm_w)))
    v = dict(zip(WEIGHTS, (v_norm_w, v_w_in, v_s5_lambda_re, v_s5_lambda_im, v_s5_log_step, v_s5_b_re, v_s5_b_im, v_s5_c_re, v_s5_c_im, v_s5_d, v_s5_w_glu, v_s5_b_glu, v_rg_conv_w, v_rg_conv_b, v_rg_w_a, v_rg_b_a, v_rg_w_x, v_rg_b_x, v_rg_lambda, v_hg_lower_bounds, v_hg_norm_w, v_w_branch, v_w_out, v_final_norm_w)))
    return _step(x[0], loss_target[0], w, m, v)
```

```python
import functools
import math

import jax
import jax.numpy as jnp
from jax import lax
from jax.experimental import pallas as pl
from jax.experimental.pallas import tpu as pltpu

f32 = jnp.float32
bf16 = jnp.bfloat16

D_MODEL = 2048
W_MIX = 1024
DEPTH = 4
N_BRANCH = 3
N_IN = 8 * W_MIX + N_BRANCH * D_MODEL
S5_GROUPS, S5_STATE, S5_GROUP = 64, 64, 16
RG_BLOCKS, RG_BLOCK, RG_CONV, RG_C = 16, 64, 4, 8.0
HG_HEADS, HG_DK = 8, 128
HG_SUB = 16
EPS = 1e-6
ADAM_LR, ADAM_B1, ADAM_B2, ADAM_EPS, ADAM_WD, ADAM_STEP = 0.001, 0.9, 0.999, 1e-08, 0.01, 10

N_DEV = 8
LANE = 128
NCH = W_MIX // LANE
TM_CHUNK = 1024
VMEM_LIMIT = 56 * 1024 * 1024
MESH = pl.DeviceIdType.MESH
ANY = pl.BlockSpec(memory_space=pl.ANY)
HIGHEST = lax.Precision.HIGHEST

C_UA, C_GA, C_XB, C_GB, C_Q, C_F, C_I, C_GC, C_GATE = (W_MIX * k for k in range(9))


def _cparams(sem=None):
    return pltpu.CompilerParams(dimension_semantics=sem, vmem_limit_bytes=VMEM_LIMIT)


@jax.custom_vjp
def bdot(a, w):
    return jnp.dot(a.astype(bf16), w.astype(bf16), preferred_element_type=f32)


def _bdot_fwd(a, w):
    return bdot(a, w), (a, w)


def _bdot_bwd(res, g):
    a, w = res
    gb = g.astype(bf16)
    da = lax.dot_general(gb, w.astype(bf16), (((1,), (1,)), ((), ())), preferred_element_type=f32)
    dw = lax.dot_general(a.astype(bf16), gb, (((0,), (0,)), ((), ())), preferred_element_type=f32)
    return da, dw


bdot.defvjp(_bdot_fwd, _bdot_bwd)


def _blockmm(c, a):
    n = c.shape[0]
    return jnp.concatenate([jnp.dot(c, a[i:i + n], preferred_element_type=f32, precision=HIGHEST)
                            for i in range(0, a.shape[0], n)], axis=0)


@jax.custom_vjp
def cdot(c, ct, a):
    return _blockmm(c, a)


def _cdot_fwd(c, ct, a):
    return cdot(c, ct, a), (c, ct)


def _cdot_bwd(res, g):
    c, ct = res
    return jnp.zeros_like(c), jnp.zeros_like(ct), _blockmm(ct, g)


cdot.defvjp(_cdot_fwd, _cdot_bwd)


def mm(a, b, *, name, out_dtype=f32, add=None, after=None, tm=512, tn=1024, tk=4096):
    m, k = a.shape
    _, n = b.shape
    tm, tn, tk = min(tm, m), min(tn, n), min(tk, k)
    assert m % tm == 0 and n % tn == 0 and k % tk == 0
    nk = k // tk

    def body(*refs):
        a_ref, b_ref = refs[:2]
        r_ref = refs[2] if add is not None else None
        o_ref, acc_ref = refs[-2:]
        kk = pl.program_id(2)
        part = jnp.dot(a_ref[...], b_ref[...], preferred_element_type=f32)

        @pl.when(kk == 0)
        def _():
            acc_ref[...] = part

        @pl.when(kk > 0)
        def _():
            acc_ref[...] = acc_ref[...] + part

        @pl.when(kk == nk - 1)
        def _():
            acc = acc_ref[...]
            if add is not None:
                acc = acc + r_ref[...]
            o_ref[...] = acc.astype(out_dtype)

    in_specs = [pl.BlockSpec((tm, tk), lambda i, j, q: (i, q)), pl.BlockSpec((tk, tn), lambda i, j, q: (q, j))]
    args = [a, b]
    if add is not None:
        in_specs.append(pl.BlockSpec((tm, tn), lambda i, j, q: (i, j)))
        args.append(add)
    if after is not None:
        in_specs.append(pl.BlockSpec(after.shape, lambda i, j, q: (0, 0)))
        args.append(after)
    return pl.pallas_call(
        body, name=name, grid=(m // tm, n // tn, nk), in_specs=in_specs,
        out_specs=pl.BlockSpec((tm, tn), lambda i, j, q: (i, j)),
        out_shape=jax.ShapeDtypeStruct((m, n), out_dtype),
        scratch_shapes=[pltpu.VMEM((tm, tn), f32)],
        compiler_params=_cparams(("parallel", "parallel", "arbitrary")),
    )(*args)


def _row_spec(tm, wc, col_off):
    base = col_off // wc
    assert col_off % wc == 0
    return pl.BlockSpec((tm, wc), lambda j, i: (i, base + j))


def _slab_spec(arr):
    r, c = arr.shape[1:]
    if arr.shape[0] == 1:
        return pl.BlockSpec((1, r, c), lambda j, i: (0, 0, 0))
    return pl.BlockSpec((1, r, c), lambda j, i: (j, 0, 0))


def rowwise(fn, rows, params, consts, outs, *, name, tm=256, ncol=1, rowid=False):
    t = rows[0][0].shape[0]
    tm = min(tm, t)
    nr, npar, nc, no = len(rows), len(params), len(consts), len(outs)

    def body(*refs):
        r = [refs[k][...].astype(f32) for k in range(nr)]
        p = [refs[nr + k][0] for k in range(npar + nc)]
        extra = ()
        if rowid:
            extra = (pl.program_id(1) * tm + lax.broadcasted_iota(jnp.int32, (tm, 1), 0),)
        res = fn(*extra, *r, *p)
        for k in range(no):
            refs[nr + npar + nc + k][...] = res[k].astype(outs[k][1])

    in_specs = [_row_spec(tm, w // ncol, off) for (_, off, w) in rows]
    in_specs += [_slab_spec(a) for a in list(params) + list(consts)]
    out_specs = [pl.BlockSpec((tm, w // ncol), lambda j, i: (i, j)) for (w, _) in outs]
    out_shape = [jax.ShapeDtypeStruct((t, w), dt) for (w, dt) in outs]
    return pl.pallas_call(
        body, name=name, grid=(ncol, t // tm), in_specs=in_specs, out_specs=out_specs, out_shape=out_shape,
        compiler_params=_cparams(("parallel", "parallel")),
    )(*[r[0] for r in rows], *params, *consts)


def rowwise_vjp(fn, rows, params, consts, cts, d_rows, *, name, tm=256, ncol=1, rowid=False, sum_primal=None):
    t = rows[0][0].shape[0]
    tm = min(tm, t)
    nr, npar, nc, nct, ndr = len(rows), len(params), len(consts), len(cts), len(d_rows)

    def body(*refs):
        i = pl.program_id(1)
        r = [refs[k][...].astype(f32) for k in range(nr)]
        p = [refs[nr + k][0] for k in range(npar)]
        c = [refs[nr + npar + k][0] for k in range(nc)]
        g = [refs[nr + npar + nc + k][...].astype(f32) for k in range(nct)]
        orefs = refs[nr + npar + nc + nct:]
        extra = ()
        if rowid:
            extra = (i * tm + lax.broadcasted_iota(jnp.int32, (tm, 1), 0),)
        res, vjp = jax.vjp(lambda *v: fn(*extra, *v, *c), *r, *p)
        grads = vjp(tuple(g))
        for k, (idx, dt) in enumerate(d_rows):
            orefs[k][...] = grads[idx].astype(dt)
        acc = [grads[nr + k] for k in range(npar)]
        if sum_primal is not None:
            acc.append(jnp.sum(res[sum_primal], axis=0, keepdims=True))

        @pl.when(i == 0)
        def _():
            for k, a in enumerate(acc):
                orefs[ndr + k][0] = a

        @pl.when(i > 0)
        def _():
            for k, a in enumerate(acc):
                orefs[ndr + k][0] = orefs[ndr + k][0] + a

    in_specs = [_row_spec(tm, w // ncol, off) for (_, off, w) in rows]
    in_specs += [_slab_spec(a) for a in list(params) + list(consts)]
    in_specs += [_row_spec(tm, w // ncol, off) for (_, off, w) in cts]
    out_specs = [pl.BlockSpec((tm, rows[idx][2] // ncol), lambda j, i: (i, j)) for (idx, _) in d_rows]
    out_shape = [jax.ShapeDtypeStruct((t, rows[idx][2]), dt) for (idx, dt) in d_rows]
    for a in params:
        out_specs.append(pl.BlockSpec((1,) + a.shape[1:], lambda j, i: (j, 0, 0)))
        out_shape.append(jax.ShapeDtypeStruct(a.shape, f32))
    if sum_primal is not None:
        w = cts[sum_primal][2]
        out_specs.append(pl.BlockSpec((1, 1, w // ncol), lambda j, i: (j, 0, 0)))
        out_shape.append(jax.ShapeDtypeStruct((ncol, 1, w // ncol), f32))
    return pl.pallas_call(
        body, name=name, grid=(ncol, t // tm), in_specs=in_specs, out_specs=out_specs, out_shape=out_shape,
        compiler_params=_cparams(("parallel", "arbitrary")),
    )(*[r[0] for r in rows], *params, *consts, *[c[0] for c in cts])


VM = pl.BlockSpec(memory_space=pltpu.VMEM)


def whole(fn, ins, outs, *, name):
    def body(*refs):
        res = fn(*[r[...] for r in refs[:len(ins)]])
        for k, o in enumerate(refs[len(ins):]):
            o[...] = res[k]
    return pl.pallas_call(body, name=name, in_specs=[VM] * len(ins), out_specs=[VM] * len(outs),
                          out_shape=[jax.ShapeDtypeStruct(s, f32) for s in outs],
                          compiler_params=_cparams())(*ins)


def whole_vjp(fn, ins, cts, *, name):
    n = len(ins)

    def body(*refs):
        _, vjp = jax.vjp(fn, *[r[...] for r in refs[:n]])
        grads = vjp(tuple(r[...] for r in refs[n:n + len(cts)]))
        for k, o in enumerate(refs[n + len(cts):]):
            o[...] = grads[k]
    return pl.pallas_call(body, name=name, in_specs=[VM] * (n + len(cts)), out_specs=[VM] * n,
                          out_shape=[jax.ShapeDtypeStruct(a.shape, f32) for a in ins],
                          compiler_params=_cparams())(*ins, *cts)


def ln_fn(x, w):
    return (x * lax.rsqrt(jnp.mean(x * x, axis=-1, keepdims=True) + EPS) * w,)


def ln_res_fn(x, w):
    return ln_fn(x, w)[0], x


def loss_fn(x, tgt, w):
    y = ln_fn(x, w)[0]
    return (0.5 * jnp.mean(jnp.square(y - tgt), axis=-1, keepdims=True),)


def s5_prep_fn(lam_re, lam_im, log_step, b_re, b_im):
    step = jnp.exp(log_step)
    mag = jnp.exp(lam_re * step)
    ang = lam_im * step
    abar_re = mag * jnp.cos(ang)
    abar_im = mag * jnp.sin(ang)
    num_re = abar_re - 1.0
    num_im = abar_im
    den = lam_re * lam_re + lam_im * lam_im
    coef_re = (num_re * lam_re + num_im * lam_im) / den
    coef_im = (num_im * lam_re - num_re * lam_im) / den
    bbar_re = coef_re * b_re - coef_im * b_im
    bbar_im = coef_re * b_im + coef_im * b_re
    return abar_re, abar_im, bbar_re, bbar_im


def lb_prep_fn(r0, r1, r2, r3):
    m = jnp.maximum(jnp.maximum(r0, r1), jnp.maximum(r2, r3))
    e0, e1, e2, e3 = jnp.exp(r0 - m), jnp.exp(r1 - m), jnp.exp(r2 - m), jnp.exp(r3 - m)
    s = e0 + e1 + e2 + e3
    p0, p1, p2, p3 = e0 / s, e1 / s, e2 / s, e3 / s
    c1 = p0 + p1
    c2 = c1 + p2
    c3 = c2 + p3
    return p0 - p0, c1 - p0, c2 - p0, c3 - p0


def s5_post_fn(yssm, u, ga, d, wglu, bglu):
    y = jax.nn.gelu(yssm + d * u)
    y = y * jax.nn.sigmoid(bdot(y, wglu) + bglu)
    return (y * jax.nn.silu(ga),)


def rg_gate_fn(tglob, xc, wa, ba, wx, bx, lam):
    r = jax.nn.sigmoid(bdot(xc, wa) + ba)
    i = jax.nn.sigmoid(bdot(xc, wx) + bx)
    log_a = -RG_C * r * jax.nn.softplus(-lam)
    a = jnp.exp(log_a)
    mult = jnp.sqrt(-jnp.tanh(log_a) * (a * a + 1.0))
    mult = jnp.where(tglob == 0, 1.0, mult)
    return a, mult * (i * xc)


def hg_pre_fn(q, fl, lb, tri, tri_t):
    f = lb + (1.0 - lb) * jax.nn.sigmoid(fl)
    return jax.nn.silu(q), 1.0 - f, cdot(tri, tri_t, jnp.log(f))


def branch_prep_fn(hb, gb, oc, gc, nw):
    yb = hb * jax.nn.silu(gb)
    on = oc * lax.rsqrt(jnp.mean(oc * oc, axis=-1, keepdims=True) + EPS) * nw
    return yb, on * jax.nn.silu(gc)


def merge_fn(b0, b1, b2, g0, g1, g2):
    return (jax.nn.sigmoid(g0) * b0 + jax.nn.sigmoid(g1) * b1 + jax.nn.sigmoid(g2) * b2,)


S5_TB = 512
S5_SC = 512


def _shift_down(v, k, row, fill=0.0):
    return jnp.where(row >= k, pltpu.roll(v, k, 0), fill)


def _shift_up(v, k, row, n, fill=0.0):
    return jnp.where(row < n - k, pltpu.roll(v, n - k, 0), fill)


def s5_scan_fwd(z, bd_re, bd_im, cd_re, cd_im, a_re, a_im):
    t = z.shape[0]
    tb = min(S5_TB, t)

    def body(u_ref, bre, bim, cre, cim, are, aim, y_ref, xre_ref, xim_ref, car_re, car_im):
        @pl.when(pl.program_id(1) == 0)
        def _():
            car_re[...] = jnp.zeros_like(car_re)
            car_im[...] = jnp.zeros_like(car_im)

        u = u_ref[...].astype(bf16)
        row = lax.broadcasted_iota(jnp.int32, (tb, 1), 0)
        ar, ai = are[0], aim[0]
        cr, ci = car_re[...], car_im[...]
        xr = jnp.dot(u, bre[0], preferred_element_type=f32) + jnp.where(row == 0, ar * cr - ai * ci, 0.0)
        xi = jnp.dot(u, bim[0], preferred_element_type=f32) + jnp.where(row == 0, ar * ci + ai * cr, 0.0)
        pr, pi_ = ar, ai
        k = 1
        while k < tb:
            sr, si = _shift_down(xr, k, row), _shift_down(xi, k, row)
            xr, xi = xr + pr * sr - pi_ * si, xi + pr * si + pi_ * sr
            pr, pi_ = pr * pr - pi_ * pi_, 2.0 * pr * pi_
            k *= 2
        car_re[...] = xr[tb - 1:tb, :]
        car_im[...] = xi[tb - 1:tb, :]
        xre_ref[...] = xr
        xim_ref[...] = xi
        y_ref[...] = (jnp.dot(xr.astype(bf16), cre[0], preferred_element_type=f32)
                      - jnp.dot(xi.astype(bf16), cim[0], preferred_element_type=f32))

    chunk = lambda r, c: pl.BlockSpec((1, r, c), lambda j, i: (j, 0, 0))
    return pl.pallas_call(
        body, name="s5_scan_fwd", grid=(NCH, t // tb),
        in_specs=[pl.BlockSpec((tb, LANE), lambda j, i: (i, C_UA // LANE + j)),
                  chunk(LANE, S5_SC), chunk(LANE, S5_SC), chunk(S5_SC, LANE), chunk(S5_SC, LANE),
                  chunk(1, S5_SC), chunk(1, S5_SC)],
        out_specs=[pl.BlockSpec((tb, LANE), lambda j, i: (i, j)),
                   pl.BlockSpec((tb, S5_SC), lambda j, i: (i, j)),
                   pl.BlockSpec((tb, S5_SC), lambda j, i: (i, j))],
        out_shape=[jax.ShapeDtypeStruct((t, W_MIX), f32),
                   jax.ShapeDtypeStruct((t, NCH * S5_SC), f32),
                   jax.ShapeDtypeStruct((t, NCH * S5_SC), f32)],
        scratch_shapes=[pltpu.VMEM((1, S5_SC), f32), pltpu.VMEM((1, S5_SC), f32)],
        compiler_params=_cparams(("parallel", "arbitrary")),
    )(z, bd_re.astype(bf16), bd_im.astype(bf16), cd_re.astype(bf16), cd_im.astype(bf16), a_re, a_im)


def s5_scan_bwd(dy, du1, z, xre, xim, bd_re, bd_im, cd_re, cd_im, a_re, a_im):
    t = z.shape[0]
    tb = min(S5_TB, t)
    nt = t // tb

    def body(dy_ref, du1_ref, u_ref, xre_ref, xim_ref, hre_ref, him_ref, bre, bim, cre, cim, are, aim,
             du_ref, dbre, dbim, dcre, dcim, dare, daim, car_re, car_im):
        step = pl.program_id(1)
        tt = nt - 1 - step

        @pl.when(step == 0)
        def _():
            car_re[...] = jnp.zeros_like(car_re)
            car_im[...] = jnp.zeros_like(car_im)

        nt_dims = (((1,), (1,)), ((), ()))
        tn_dims = (((0,), (0,)), ((), ()))
        dyb = dy_ref[...].astype(bf16)
        row = lax.broadcasted_iota(jnp.int32, (tb, 1), 0)
        xr, xi = xre_ref[...], xim_ref[...]
        ar, ai = are[0], aim[0]
        cr, ci = ar, -ai
        kr, ki = car_re[...], car_im[...]
        lr = lax.dot_general(dyb, cre[0], nt_dims, preferred_element_type=f32)
        li = -lax.dot_general(dyb, cim[0], nt_dims, preferred_element_type=f32)
        lr = lr + jnp.where(row == tb - 1, cr * kr - ci * ki, 0.0)
        li = li + jnp.where(row == tb - 1, cr * ki + ci * kr, 0.0)
        pr, pi_ = cr, ci
        k = 1
        while k < tb:
            sr, si = _shift_up(lr, k, row, tb), _shift_up(li, k, row, tb)
            lr, li = lr + pr * sr - pi_ * si, li + pr * si + pi_ * sr
            pr, pi_ = pr * pr - pi_ * pi_, 2.0 * pr * pi_
            k *= 2
        car_re[...] = lr[0:1, :]
        car_im[...] = li[0:1, :]
        lrb, lib = lr.astype(bf16), li.astype(bf16)
        ub = u_ref[...].astype(bf16)
        du = (lax.dot_general(lrb, bre[0], nt_dims, preferred_element_type=f32)
              + lax.dot_general(lib, bim[0], nt_dims, preferred_element_type=f32))
        du_ref[...] = (du + du1_ref[...].astype(f32)).astype(du_ref.dtype)
        live = (tt > 0).astype(f32)
        xpr = jnp.where(row == 0, hre_ref[7:8, :] * live, pltpu.roll(xr, 1, 0))
        xpi = jnp.where(row == 0, him_ref[7:8, :] * live, pltpu.roll(xi, 1, 0))
        acc = [
            lax.dot_general(ub, lrb, tn_dims, preferred_element_type=f32),
            lax.dot_general(ub, lib, tn_dims, preferred_element_type=f32),
            lax.dot_general(xr.astype(bf16), dyb, tn_dims, preferred_element_type=f32),
            -lax.dot_general(xi.astype(bf16), dyb, tn_dims, preferred_element_type=f32),
            jnp.sum(lr * xpr + li * xpi, axis=0, keepdims=True),
            jnp.sum(li * xpr - lr * xpi, axis=0, keepdims=True),
        ]
        outs = [dbre, dbim, dcre, dcim, dare, daim]

        @pl.when(step == 0)
        def _():
            for o, a in zip(outs, acc):
                o[0] = a

        @pl.when(step > 0)
        def _():
            for o, a in zip(outs, acc):
                o[0] = o[0] + a

    chunk = lambda r, c: pl.BlockSpec((1, r, c), lambda j, i: (j, 0, 0))
    rev = lambda w, base=0: pl.BlockSpec((tb, w), lambda j, i: (nt - 1 - i, base + j))
    halo = pl.BlockSpec((8, S5_SC), lambda j, i: (jnp.maximum((nt - 1 - i) * (tb // 8) - 1, 0), j))
    return pl.pallas_call(
        body, name="s5_scan_bwd", grid=(NCH, nt),
        in_specs=[rev(LANE), rev(LANE), rev(LANE, C_UA // LANE), rev(S5_SC), rev(S5_SC), halo, halo,
                  chunk(LANE, S5_SC), chunk(LANE, S5_SC), chunk(S5_SC, LANE), chunk(S5_SC, LANE),
                  chunk(1, S5_SC), chunk(1, S5_SC)],
        out_specs=[rev(LANE), chunk(LANE, S5_SC), chunk(LANE, S5_SC), chunk(S5_SC, LANE), chunk(S5_SC, LANE),
                   chunk(1, S5_SC), chunk(1, S5_SC)],
        out_shape=[jax.ShapeDtypeStruct((t, W_MIX), bf16),
                   jax.ShapeDtypeStruct((NCH, LANE, S5_SC), f32), jax.ShapeDtypeStruct((NCH, LANE, S5_SC), f32),
                   jax.ShapeDtypeStruct((NCH, S5_SC, LANE), f32), jax.ShapeDtypeStruct((NCH, S5_SC, LANE), f32),
                   jax.ShapeDtypeStruct((NCH, 1, S5_SC), f32), jax.ShapeDtypeStruct((NCH, 1, S5_SC), f32)],
        scratch_shapes=[pltpu.VMEM((1, S5_SC), f32), pltpu.VMEM((1, S5_SC), f32)],
        compiler_params=_cparams(("parallel", "arbitrary")),
    )(dy, du1, z, xre, xim, xre, xim, bd_re.astype(bf16), bd_im.astype(bf16), cd_re.astype(bf16),
      cd_im.astype(bf16), a_re, a_im)


RG_TB = 512


def rg_conv_fwd(z, cw, cb):
    t = z.shape[0]
    tb = min(RG_TB, t)

    def body(x_ref, h_ref, cw_ref, cb_ref, o_ref):
        live = (pl.program_id(1) > 0).astype(f32)
        ext = jnp.concatenate([h_ref[...] * live, x_ref[...]], axis=0)
        w = cw_ref[0]
        acc = cb_ref[0] + w[3:4, :] * ext[8:, :]
        for k in range(3):
            acc = acc + w[k:k + 1, :] * pltpu.roll(ext, 3 - k, 0)[8:, :]
        o_ref[...] = acc

    base = C_XB // LANE
    chunk = lambda r: pl.BlockSpec((1, r, LANE), lambda j, i: (j, 0, 0))
    return pl.pallas_call(
        body, name="rg_conv_fwd", grid=(NCH, t // tb),
        in_specs=[pl.BlockSpec((tb, LANE), lambda j, i: (i, base + j)),
                  pl.BlockSpec((8, LANE), lambda j, i: (jnp.maximum(i * (tb // 8) - 1, 0), base + j)),
                  chunk(RG_CONV), chunk(1)],
        out_specs=pl.BlockSpec((tb, LANE), lambda j, i: (i, j)),
        out_shape=jax.ShapeDtypeStruct((t, W_MIX), f32),
        compiler_params=_cparams(("parallel", "parallel")),
    )(z, z, cw, cb)


def rg_conv_bwd(dxc, z, cw):
    t = z.shape[0]
    tb = min(RG_TB, t)
    nt = t // tb

    def body(g_ref, gn_ref, x_ref, h_ref, cw_ref, dx_ref, dcw_ref, dcb_ref):
        i = pl.program_id(1)
        g = g_ref[...]
        gext = jnp.concatenate([g, gn_ref[...] * (i < nt - 1).astype(f32)], axis=0)
        xext = jnp.concatenate([h_ref[...] * (i > 0).astype(f32), x_ref[...]], axis=0)
        w = cw_ref[0]
        dx = w[3:4, :] * g
        rows = [None] * RG_CONV
        rows[3] = jnp.sum(g * xext[8:, :], axis=0, keepdims=True)
        for k in range(3):
            s = 3 - k
            dx = dx + w[k:k + 1, :] * pltpu.roll(gext, tb + 8 - s, 0)[:tb, :]
            rows[k] = jnp.sum(g * pltpu.roll(xext, s, 0)[8:, :], axis=0, keepdims=True)
        dx_ref[...] = dx.astype(dx_ref.dtype)
        dcw = jnp.concatenate(rows, axis=0)
        dcb = jnp.sum(g, axis=0, keepdims=True)

        @pl.when(i == 0)
        def _():
            dcw_ref[0] = dcw
            dcb_ref[0] = dcb

        @pl.when(i > 0)
        def _():
            dcw_ref[0] = dcw_ref[0] + dcw
            dcb_ref[0] = dcb_ref[0] + dcb

    base = C_XB // LANE
    chunk = lambda r: pl.BlockSpec((1, r, LANE), lambda j, i: (j, 0, 0))
    return pl.pallas_call(
        body, name="rg_conv_bwd", grid=(NCH, nt),
        in_specs=[pl.BlockSpec((tb, LANE), lambda j, i: (i, j)),
                  pl.BlockSpec((8, LANE), lambda j, i: (jnp.minimum((i + 1) * (tb // 8), t // 8 - 1), j)),
                  pl.BlockSpec((tb, LANE), lambda j, i: (i, base + j)),
                  pl.BlockSpec((8, LANE), lambda j, i: (jnp.maximum(i * (tb // 8) - 1, 0), base + j)),
                  chunk(RG_CONV)],
        out_specs=[pl.BlockSpec((tb, LANE), lambda j, i: (i, j)), chunk(RG_CONV), chunk(1)],
        out_shape=[jax.ShapeDtypeStruct((t, W_MIX), bf16), jax.ShapeDtypeStruct((NCH, RG_CONV, LANE), f32),
                   jax.ShapeDtypeStruct((NCH, 1, LANE), f32)],
        compiler_params=_cparams(("parallel", "arbitrary")),
    )(dxc, dxc, z, z, cw)


def rg_scan_fwd(a, b):
    t = a.shape[0]
    tb = min(RG_TB, t)

    def body(a_ref, b_ref, h_ref, car):
        @pl.when(pl.program_id(1) == 0)
        def _():
            car[...] = jnp.zeros_like(car)

        row = lax.broadcasted_iota(jnp.int32, (tb, 1), 0)
        aa, bb = a_ref[...], b_ref[...]
        k = 1
        while k < tb:
            bb = bb + aa * _shift_down(bb, k, row)
            aa = aa * _shift_down(aa, k, row, 1.0)
            k *= 2
        h = bb + aa * car[...]
        car[...] = h[tb - 1:tb, :]
        h_ref[...] = h

    spec = pl.BlockSpec((tb, LANE), lambda j, i: (i, j))
    return pl.pallas_call(
        body, name="rg_scan_fwd", grid=(NCH, t // tb), in_specs=[spec, spec], out_specs=spec,
        out_shape=jax.ShapeDtypeStruct((t, W_MIX), f32), scratch_shapes=[pltpu.VMEM((1, LANE), f32)],
        compiler_params=_cparams(("parallel", "arbitrary")),
    )(a, b)


def rg_scan_bwd(dh, a, h):
    t = a.shape[0]
    tb = min(RG_TB, t)
    nt = t // tb

    def body(g_ref, a_ref, an_ref, h_ref, hp_ref, da_ref, db_ref, car):
        step = pl.program_id(1)
        tt = nt - 1 - step

        @pl.when(step == 0)
        def _():
            car[...] = jnp.zeros_like(car)

        row = lax.broadcasted_iota(jnp.int32, (tb, 1), 0)
        an = an_ref[0:1, :] * (tt < nt - 1).astype(f32)
        aa = jnp.where(row == tb - 1, an, pltpu.roll(a_ref[...], tb - 1, 0))
        bb = g_ref[...]
        k = 1
        while k < tb:
            bb = bb + aa * _shift_up(bb, k, row, tb)
            aa = aa * _shift_up(aa, k, row, tb, 1.0)
            k *= 2
        lam = bb + aa * car[...]
        car[...] = lam[0:1, :]
        hp = jnp.where(row == 0, hp_ref[7:8, :] * (tt > 0).astype(f32), pltpu.roll(h_ref[...], 1, 0))
        da_ref[...] = lam * hp
        db_ref[...] = lam

    rev = pl.BlockSpec((tb, LANE), lambda j, i: (nt - 1 - i, j))
    nxt = pl.BlockSpec((8, LANE), lambda j, i: (jnp.minimum((nt - i) * (tb // 8), t // 8 - 1), j))
    prv = pl.BlockSpec((8, LANE), lambda j, i: (jnp.maximum((nt - 1 - i) * (tb // 8) - 1, 0), j))
    return pl.pallas_call(
        body, name="rg_scan_bwd", grid=(NCH, nt), in_specs=[rev, rev, nxt, rev, prv], out_specs=[rev, rev],
        out_shape=[jax.ShapeDtypeStruct((t, W_MIX), f32)] * 2, scratch_shapes=[pltpu.VMEM((1, LANE), f32)],
        compiler_params=_cparams(("parallel", "arbitrary")),
    )(dh, a, a, h, h)


HG_TB = 256


def _heads(v):
    return jnp.stack([v[:, LANE * h:LANE * (h + 1)] for h in range(HG_HEADS)])


def _unheads(v):
    return jnp.concatenate([v[h] for h in range(HG_HEADS)], axis=-1)


def _bmm(eq, a, b):
    return jnp.einsum(eq, a.astype(bf16), b.astype(bf16), preferred_element_type=f32)


def hg_chunk_fwd(qs, kk, gcum, z):
    t = qs.shape[0]
    tb = min(HG_TB, t)
    nc = tb // HG_SUB

    def body(q_ref, k_ref, g_ref, v_ref, o_ref, sall_ref, st_ref):
        @pl.when(pl.program_id(0) == 0)
        def _():
            st_ref[...] = jnp.zeros_like(st_ref)

        ri = lax.broadcasted_iota(jnp.int32, (1, HG_SUB, 1), 1)

        def chunk(c, carry):
            rows = pl.ds(pl.multiple_of(c * HG_SUB, HG_SUB), HG_SUB)
            q, k, g, v = _heads(q_ref[rows, :]), _heads(k_ref[rows, :]), _heads(g_ref[rows, :]), _heads(v_ref[rows, :])
            st = st_ref[...]
            sall_ref[c] = st
            o = _bmm('htk,hvk->htv', q * jnp.exp(g), st)
            for s in range(HG_SUB):
                p = jnp.where(ri >= s, jnp.exp(jnp.minimum(g - g[:, s:s + 1, :], 0.0)), 0.0)
                col = jnp.sum(q * k[:, s:s + 1, :] * p, axis=-1, keepdims=True)
                o = o + col * v[:, s:s + 1, :]
            gl = g[:, HG_SUB - 1:HG_SUB, :]
            st_ref[...] = st * jnp.exp(gl) + _bmm('htv,htk->hvk', v, k * jnp.exp(gl - g))
            o_ref[rows, :] = _unheads(o)
            return carry

        lax.fori_loop(0, nc, chunk, 0)

    spec = lambda base=0: pl.BlockSpec((tb, W_MIX), lambda i: (i, base))
    return pl.pallas_call(
        body, name="hg_chunk_fwd", grid=(t // tb,),
        in_specs=[spec(), spec(), spec(), spec(C_I // W_MIX)],
        out_specs=[spec(), pl.BlockSpec((nc, HG_HEADS, HG_DK, HG_DK), lambda i: (i, 0, 0, 0))],
        out_shape=[jax.ShapeDtypeStruct((t, W_MIX), f32),
                   jax.ShapeDtypeStruct((t // HG_SUB, HG_HEADS, HG_DK, HG_DK), f32)],
        scratch_shapes=[pltpu.VMEM((HG_HEADS, HG_DK, HG_DK), f32)],
        compiler_params=_cparams(("arbitrary",)),
    )(qs, kk, gcum, z)


def hg_chunk_bwd(do, qs, kk, gcum, z, sall):
    t = qs.shape[0]
    tb = min(HG_TB, t)
    nc = tb // HG_SUB
    nt = t // tb

    def body(do_ref, q_ref, k_ref, g_ref, v_ref, sall_ref, dq_ref, dk_ref, dg_ref, dv_ref, dst_ref):
        @pl.when(pl.program_id(0) == 0)
        def _():
            dst_ref[...] = jnp.zeros_like(dst_ref)

        ri = lax.broadcasted_iota(jnp.int32, (1, HG_SUB, 1), 1)

        def chunk(cc, carry):
            c = nc - 1 - cc
            rows = pl.ds(pl.multiple_of(c * HG_SUB, HG_SUB), HG_SUB)
            q, k, g, v = _heads(q_ref[rows, :]), _heads(k_ref[rows, :]), _heads(g_ref[rows, :]), _heads(v_ref[rows, :])
            d_o = _heads(do_ref[rows, :])
            st = sall_ref[c]
            dsn = dst_ref[...]
            eg = jnp.exp(g)
            qe = q * eg
            gl = g[:, HG_SUB - 1:HG_SUB, :]
            egl = jnp.exp(gl)
            dec = jnp.exp(gl - g)
            kd = k * dec
            dqe = _bmm('htv,hvk->htk', d_o, st)
            dst_ref[...] = _bmm('htv,htk->hvk', d_o, qe) + dsn * egl
            dgl_dec = jnp.sum(dsn * st, axis=1, keepdims=True) * egl
            dv = _bmm('htk,hvk->htv', kd, dsn)
            dkd = _bmm('htv,hvk->htk', v, dsn)
            a1 = jnp.zeros_like(q)
            a2 = jnp.zeros_like(q)
            for s in range(HG_SUB):
                p = jnp.where(ri >= s, jnp.exp(jnp.minimum(g - g[:, s:s + 1, :], 0.0)), 0.0)
                krow = k[:, s:s + 1, :]
                col = jnp.sum(q * krow * p, axis=-1, keepdims=True)
                dcol = jnp.sum(d_o * v[:, s:s + 1, :], axis=-1, keepdims=True)
                dv = jnp.where(ri == s, dv + jnp.sum(col * d_o, axis=1, keepdims=True), dv)
                t1 = dcol * p
                a1 = a1 + t1 * krow
                a2 = jnp.where(ri == s, jnp.sum(t1 * q, axis=1, keepdims=True), a2)
            dgl = jnp.sum(dkd * kd, axis=1, keepdims=True) + dgl_dec
            dg = dqe * qe + q * a1 - k * a2 - dkd * kd
            dg = jnp.where(ri == HG_SUB - 1, dg + dgl, dg)
            dq_ref[rows, :] = _unheads(dqe * eg + a1)
            dk_ref[rows, :] = _unheads(dkd * dec + a2)
            dg_ref[rows, :] = _unheads(dg)
            dv_ref[rows, :] = _unheads(dv).astype(dv_ref.dtype)
            return carry

        lax.fori_loop(0, nc, chunk, 0)

    spec = lambda base=0: pl.BlockSpec((tb, W_MIX), lambda i: (nt - 1 - i, base))
    return pl.pallas_call(
        body, name="hg_chunk_bwd", grid=(nt,),
        in_specs=[spec(), spec(), spec(), spec(), spec(C_I // W_MIX),
                  pl.BlockSpec((nc, HG_HEADS, HG_DK, HG_DK), lambda i: (nt - 1 - i, 0, 0, 0))],
        out_specs=[spec(), spec(), spec(), spec()],
        out_shape=[jax.ShapeDtypeStruct((t, W_MIX), f32)] * 3 + [jax.ShapeDtypeStruct((t, W_MIX), bf16)],
        scratch_shapes=[pltpu.VMEM((HG_HEADS, HG_DK, HG_DK), f32)],
        compiler_params=_cparams(("arbitrary",)),
    )(do, qs, kk, gcum, z, sall)


def adamw(w, m, v, slots, *, name, tr):
    nl, r, c = w.shape
    tr = min(tr, r)
    flat = [a for per_layer in slots for a in per_layer]
    c1 = 1.0 / (1.0 - ADAM_B1 ** ADAM_STEP)
    c2 = 1.0 / (1.0 - ADAM_B2 ** ADAM_STEP)

    def body(*refs):
        w_ref, m_ref, v_ref = refs[:3]
        s_refs = list(refs[3:3 + len(flat)])
        g_ref, d_ref, mo_ref, vo_ref = refs[3 + len(flat):]
        for l in range(nl):
            parts = [s_refs.pop(0) for _ in slots[l]]
            g = None
            for p in parts:
                for s in range(p.shape[0]):
                    term = p[s].astype(f32)
                    g = term if g is None else g + term
            mn = ADAM_B1 * m_ref[l] + (1.0 - ADAM_B1) * g
            vn = ADAM_B2 * v_ref[l] + (1.0 - ADAM_B2) * (g * g)
            g_ref[l] = g
            mo_ref[l] = mn
            vo_ref[l] = vn
            d_ref[l] = -ADAM_LR * ((mn * c1) / (jnp.sqrt(vn * c2) + ADAM_EPS) + ADAM_WD * w_ref[l])

    full = pl.BlockSpec((nl, tr, c), lambda i: (0, i, 0))
    slot = [pl.BlockSpec((a.shape[0], tr, c), lambda i: (0, i, 0)) for a in flat]
    return pl.pallas_call(
        body, name=name, grid=(r // tr,), in_specs=[full] * 3 + slot, out_specs=[full] * 4,
        out_shape=[jax.ShapeDtypeStruct(w.shape, f32)] * 4, compiler_params=_cparams(("parallel",)),
    )(w, m, v, *flat)


def _slab(ref, axis, idx, n):
    return ref.at[tuple([slice(None)] * axis + [pl.ds(idx * n, n)])]


def all_gather(x, axis, *, name):
    n = x.shape[axis]
    out_shape = x.shape[:axis] + (N_DEV * n,) + x.shape[axis + 1:]

    def body(x_ref, out_ref, send_sems, recv_sems, local_sem):
        xx, yy, cc = lax.axis_index("x"), lax.axis_index("y"), lax.axis_index("c")
        me, sibling = (xx, yy, cc), (xx, yy, 1 - cc)
        chips = [(1 - xx, yy), (xx, 1 - yy), (1 - xx, 1 - yy)]

        def slab(px, py, pc):
            return _slab(out_ref, axis, 4 * px + 2 * py + pc, n)

        def copy(k, block, to, src=None):
            return pltpu.make_async_remote_copy(
                src_ref=slab(*block) if src is None else src, dst_ref=slab(*block),
                send_sem=send_sems.at[k], recv_sem=recv_sems.at[k], device_id=to, device_id_type=MESH)

        mine = pltpu.make_async_copy(x_ref, slab(*me), local_sem)
        mine.start()
        first = [copy(0, me, sibling, src=x_ref)]
        first += [copy(1 + j, me, (*chip, cc), src=x_ref) for j, chip in enumerate(chips)]
        for cp in first:
            cp.start()
        passed = [copy(4 + j, (*chip, cc), sibling) for j, chip in enumerate(chips)]
        for j, chip in enumerate(chips):
            copy(1 + j, (*chip, cc), me).wait_recv()
            passed[j].start()
        copy(0, sibling, me).wait_recv()
        for j, chip in enumerate(chips):
            copy(4 + j, (*chip, 1 - cc), me).wait_recv()
        for cp in first + passed:
            cp.wait_send()
        mine.wait()

    return pl.pallas_call(
        body, name=name, out_shape=jax.ShapeDtypeStruct(out_shape, x.dtype), in_specs=[ANY], out_specs=ANY,
        scratch_shapes=[pltpu.SemaphoreType.DMA((7,)), pltpu.SemaphoreType.DMA((7,)), pltpu.SemaphoreType.DMA],
    )(x)


N_CHIP = 4


HBM = pl.BlockSpec(memory_space=pltpu.HBM)
SEM = pl.BlockSpec(memory_space=pltpu.SEMAPHORE)
EFFECT = pltpu.SideEffectType.DATAFLOW_SIDE_EFFECTING
TOKEN = jax.ShapeDtypeStruct((8, LANE), f32)


def _in_hbm(a):
    return pltpu.with_memory_space_constraint(a, pltpu.HBM)


def pair_sums(g, axis, *, name):
    n = g.shape[axis] // N_DEV
    slab_shape = g.shape[:axis] + (n,) + g.shape[axis + 1:]
    cols = slab_shape[-1]
    rows = math.prod(slab_shape[:-1])
    col_slabs = axis == g.ndim - 1
    assert col_slabs or (axis == 0 and g.ndim == 2)

    def swap_body(g_ref, got_ref, send_sems, recv_sems):
        xx, yy, cc = lax.axis_index("x"), lax.axis_index("y"), lax.axis_index("c")
        copies = [pltpu.make_async_remote_copy(
            src_ref=_slab(g_ref, axis, 2 * q + 1 - cc, n), dst_ref=got_ref.at[q],
            send_sem=send_sems.at[q], recv_sem=recv_sems.at[q], device_id=(xx, yy, 1 - cc), device_id_type=MESH)
            for q in range(N_CHIP)]
        for cp in copies:
            cp.start()
        for cp in copies:
            cp.wait()

    got = pl.pallas_call(
        swap_body, name=name + "_swap", out_shape=jax.ShapeDtypeStruct((N_CHIP,) + slab_shape, g.dtype),
        in_specs=[ANY], out_specs=ANY, scratch_shapes=[pltpu.SemaphoreType.DMA((N_CHIP,))] * 2,
    )(g)

    tr = min(256, rows)

    def add_body(a0_ref, a1_ref, b_ref, pair_ref, own_ref):
        xx, yy, cc = lax.axis_index("x"), lax.axis_index("y"), lax.axis_index("c")
        mine = jnp.where(cc == 0, a0_ref[...], a1_ref[...])
        s = (mine.astype(f32) + b_ref[0].astype(f32)).astype(bf16)
        pair_ref[0] = s

        @pl.when(pl.program_id(1) == 2 * xx + yy)
        def _():
            own_ref[0] = s

    if col_slabs:
        a_spec = lambda c: pl.BlockSpec((tr, cols), lambda i, q: (i, 2 * q + c))
    else:
        a_spec = lambda c: pl.BlockSpec((tr, cols), lambda i, q: ((2 * q + c) * (n // tr) + i, 0))
    by_chip = pl.BlockSpec((1, tr, cols), lambda i, q: (q, i, 0))
    g2 = g.reshape(-1, g.shape[-1])
    pair, own = pl.pallas_call(
        add_body, name=name + "_add", grid=(rows // tr, N_CHIP), in_specs=[a_spec(0), a_spec(1), by_chip],
        out_specs=[by_chip, pl.BlockSpec((1, tr, cols), lambda i, q: (0, i, 0))],
        out_shape=[jax.ShapeDtypeStruct((N_CHIP, rows, cols), bf16), jax.ShapeDtypeStruct((1, rows, cols), bf16)],
        compiler_params=_cparams(("parallel", "arbitrary")),
    )(g2, g2, got.reshape(N_CHIP, rows, cols))
    return own, pair


def _send_copies(p_refs, land_refs, send_sems, recv_sems):
    xx, yy, cc = lax.axis_index("x"), lax.axis_index("y"), lax.axis_index("c")
    copies = []
    for t, (p, land) in enumerate(zip(p_refs, land_refs)):
        for k in range(1, N_CHIP):
            px = 1 - xx if k & 2 else xx
            py = 1 - yy if k & 1 else yy
            s = (N_CHIP - 1) * t + k - 1
            copies.append(pltpu.make_async_remote_copy(
                src_ref=p.at[2 * px + py], dst_ref=land.at[k - 1], send_sem=send_sems.at[s], recv_sem=recv_sems.at[s],
                device_id=(px, py, cc), device_id_type=MESH))
    return copies


def send_pairs_start(pairs, *, name):
    nt = len(pairs)
    lands = [lax.empty((N_CHIP - 1,) + p.shape[1:], p.dtype) for p in pairs]

    def body(*refs):
        p_refs, land_refs = refs[:nt], refs[nt:2 * nt]
        send_sems, recv_sems = refs[2 * nt], refs[2 * nt + 1]
        token = refs[-1]
        for cp in _send_copies(p_refs, land_refs, send_sems, recv_sems):
            cp.start()
        token[...] = jnp.zeros_like(token)

    nsem = (N_CHIP - 1) * nt
    outs = pl.pallas_call(
        body, name=name,
        out_shape=(pltpu.SemaphoreType.DMA((nsem,)), pltpu.SemaphoreType.DMA((nsem,)))
        + tuple(pltpu.HBM(a.shape, a.dtype) for a in list(pairs) + lands) + (TOKEN,),
        in_specs=[HBM] * (2 * nt), out_specs=(SEM, SEM) + (HBM,) * (2 * nt) + (VM,),
        input_output_aliases={i: 2 + i for i in range(2 * nt)},
        compiler_params=pltpu.CompilerParams(has_side_effects=EFFECT),
    )(*[_in_hbm(a) for a in list(pairs) + lands])
    return outs[:-1], outs[-1]


def send_pairs_wait(handles, after, *, name):
    send_sems, recv_sems = handles[0], handles[1]
    bufs = handles[2:]
    nt = len(bufs) // 2

    def body(*refs):
        p_refs, land_refs = refs[:nt], refs[nt:2 * nt]
        send_sems, recv_sems = refs[2 * nt], refs[2 * nt + 1]
        for cp in _send_copies(p_refs, land_refs, send_sems, recv_sems):
            cp.wait_send()
            cp.wait_recv()

    outs = pl.pallas_call(
        body, name=name, out_shape=tuple(pltpu.HBM(a.shape, a.dtype) for a in bufs),
        in_specs=[HBM] * (2 * nt) + [SEM, SEM, ANY], out_specs=(HBM,) * (2 * nt),
        input_output_aliases={i: i for i in range(2 * nt)},
        compiler_params=pltpu.CompilerParams(has_side_effects=EFFECT),
    )(*bufs, send_sems, recv_sems, after)
    return outs[nt:]


def _gather_copies(x_refs, land_refs, axes, send_sems, recv_sems):
    xx, yy, cc = lax.axis_index("x"), lax.axis_index("y"), lax.axis_index("c")
    me = 4 * xx + 2 * yy + cc
    copies = []
    for t, (x_ref, land, axis) in enumerate(zip(x_refs, land_refs, axes)):
        n = x_ref.shape[axis]
        for k in range(1, N_DEV):
            px = 1 - xx if k & 4 else xx
            py = 1 - yy if k & 2 else yy
            pc = 1 - cc if k & 1 else cc
            s = (N_DEV - 1) * t + k - 1
            copies.append(pltpu.make_async_remote_copy(
                src_ref=x_ref, dst_ref=_slab(land, axis, me, n), send_sem=send_sems.at[s], recv_sem=recv_sems.at[s],
                device_id=(px, py, pc), device_id_type=MESH))
    return copies


def gather_start(xs, axes, *, name):
    nt = len(xs)
    me = 4 * lax.axis_index("x") + 2 * lax.axis_index("y") + lax.axis_index("c")
    lands = []
    for x, axis in zip(xs, axes):
        full = lax.empty(x.shape[:axis] + (N_DEV * x.shape[axis],) + x.shape[axis + 1:], x.dtype)
        lands.append(lax.dynamic_update_slice_in_dim(full, x, me * x.shape[axis], axis))

    def body(*refs):
        x_refs, land_refs = refs[:nt], refs[nt:2 * nt]
        send_sems, recv_sems = refs[2 * nt], refs[2 * nt + 1]
        token = refs[-1]
        for cp in _gather_copies(x_refs, land_refs, axes, send_sems, recv_sems):
            cp.start()
        token[...] = jnp.zeros_like(token)

    nsem = (N_DEV - 1) * nt
    outs = pl.pallas_call(
        body, name=name,
        out_shape=(pltpu.SemaphoreType.DMA((nsem,)), pltpu.SemaphoreType.DMA((nsem,)))
        + tuple(pltpu.HBM(a.shape, a.dtype) for a in list(xs) + lands) + (TOKEN,),
        in_specs=[HBM] * (2 * nt), out_specs=(SEM, SEM) + (HBM,) * (2 * nt) + (VM,),
        input_output_aliases={i: 2 + i for i in range(2 * nt)},
        compiler_params=pltpu.CompilerParams(has_side_effects=EFFECT),
    )(*[_in_hbm(a) for a in list(xs) + lands])
    return outs[:-1], outs[-1]


def gather_wait(handles, axes, after, *, name):
    send_sems, recv_sems = handles[0], handles[1]
    bufs = handles[2:]
    nt = len(bufs) // 2

    def body(*refs):
        x_refs, land_refs = refs[:nt], refs[nt:2 * nt]
        send_sems, recv_sems = refs[2 * nt], refs[2 * nt + 1]
        for cp in _gather_copies(x_refs, land_refs, axes, send_sems, recv_sems):
            cp.wait_send()
            cp.wait_recv()

    outs = pl.pallas_call(
        body, name=name, out_shape=tuple(pltpu.HBM(a.shape, a.dtype) for a in bufs),
        in_specs=[HBM] * (2 * nt) + [SEM, SEM, ANY], out_specs=(HBM,) * (2 * nt),
        input_output_aliases={i: i for i in range(2 * nt)},
        compiler_params=pltpu.CompilerParams(has_side_effects=EFFECT),
    )(*bufs, send_sems, recv_sems, after)
    return outs[nt:]


def _blockdiag(b, nb):
    j, _, r, c = b.shape
    eye = jnp.eye(nb, dtype=bool)[None, :, None, :, None]
    return jnp.where(eye, b[:, :, :, None, :], jnp.zeros((), b.dtype)).reshape(j, nb * r, nb * c)


def _diagblocks(d, nb):
    j, rr, cc = d.shape
    return jnp.einsum('jarac->jarc', d.reshape(j, nb, rr // nb, nb, cc // nb))


def _s5_b_dense(bbar):
    return _blockdiag(bbar.transpose(0, 2, 1).reshape(NCH, 8, S5_GROUP, S5_STATE), 8)


def _s5_b_undense(d):
    return _diagblocks(d, 8).reshape(S5_GROUPS, S5_GROUP, S5_STATE).transpose(0, 2, 1)


def _s5_c_dense(c):
    return _blockdiag(c.transpose(0, 2, 1).reshape(NCH, 8, S5_STATE, S5_GROUP), 8)


def _s5_c_undense(d):
    return _diagblocks(d, 8).reshape(S5_GROUPS, S5_STATE, S5_GROUP).transpose(0, 2, 1)


def _rg_dense(w):
    return _blockdiag(w.reshape(NCH, 2, RG_BLOCK, RG_BLOCK), 2)


def _rg_undense(d):
    return _diagblocks(d, 2).reshape(RG_BLOCKS, RG_BLOCK, RG_BLOCK)


def _chunks(v):
    return v.reshape(NCH, 1, LANE)


def _tri(tm):
    r = jnp.arange(tm)
    m = (r[:, None] >= r[None, :]) & (r[:, None] // HG_SUB == r[None, :] // HG_SUB)
    m = m.astype(f32)
    return m[None], m.T[None]


SMALL = ['norm_w', 's5_lambda_re', 's5_lambda_im', 's5_log_step', 's5_b_re', 's5_b_im', 's5_c_re', 's5_c_im',
         's5_d', 's5_b_glu', 'rg_conv_w', 'rg_conv_b', 'rg_w_a', 'rg_b_a', 'rg_w_x', 'rg_b_x', 'rg_lambda',
         'hg_lower_bounds', 'hg_norm_w', 'final_norm_w']
WEIGHTS = ['norm_w', 'w_in', 's5_lambda_re', 's5_lambda_im', 's5_log_step', 's5_b_re', 's5_b_im', 's5_c_re',
           's5_c_im', 's5_d', 's5_w_glu', 's5_b_glu', 'rg_conv_w', 'rg_conv_b', 'rg_w_a', 'rg_b_a', 'rg_w_x',
           'rg_b_x', 'rg_lambda', 'hg_lower_bounds', 'hg_norm_w', 'w_branch', 'w_out', 'final_norm_w']
PACK_ROWS = 512


def _pack(arrs):
    flat = jnp.concatenate([a.reshape(-1) for a in arrs])
    pad = (-flat.shape[0]) % (PACK_ROWS * LANE)
    return jnp.pad(flat, (0, pad)).reshape(1, -1, LANE)


def _unpack(buf, shapes):
    flat = buf.reshape(-1)
    out, off = [], 0
    for s in shapes:
        n = math.prod(s)
        out.append(flat[off:off + n].reshape(s))
        off += n
    return out


def _step(x, tgt, w, m, v):
    t = x.shape[0]
    tri, tri_t = _tri(min(LANE, t))
    me = 4 * lax.axis_index("x") + 2 * lax.axis_index("y") + lax.axis_index("c")

    big = ('w_in', 's5_w_glu', 'w_branch', 'w_out')
    big_axis = (1, 0, 2, 0)
    shards = lambda l: [w[k][l].astype(bf16) for k in big]
    win, wglu, wbr, wout = ([None] * DEPTH for _ in range(4))
    win[0], wglu[0], wbr[0], wout[0] = (all_gather(a, ax, name="ag_" + k) for a, ax, k in zip(shards(0), big_axis, big))
    conv_w = all_gather(w['rg_conv_w'].reshape(DEPTH * RG_CONV, LANE), 1, name="ag_conv_w")
    conv_w = conv_w.reshape(DEPTH, RG_CONV, W_MIX)

    lb_rows = [w['hg_lower_bounds'][l][None] for l in range(DEPTH)]
    lbs = whole(lb_prep_fn, lb_rows, [(1, W_MIX)] * DEPTH, name="lb_prep")

    saved = []
    for l in range(DEPTH):
        s = {}
        nw = w['norm_w'][l].reshape(1, 1, D_MODEL)
        (h,) = rowwise(ln_fn, [(x, 0, D_MODEL)], [nw], [], [(D_MODEL, bf16)], name="ln_fwd")
        token = None
        if l + 1 < DEPTH:
            handles, token = gather_start(shards(l + 1), big_axis, name=f"ag_start_{l + 1}")
        z = mm(h, win[l], after=token, name="mm_in")
        s5p = [w['s5_lambda_re'][l][..., None], w['s5_lambda_im'][l][..., None], w['s5_log_step'][l][:, None, None],
               w['s5_b_re'][l], w['s5_b_im'][l]]
        gp = (S5_GROUPS, S5_STATE)
        abar_re, abar_im, bbar_re, bbar_im = whole(
            s5_prep_fn, s5p, [gp + (1,), gp + (1,), gp + (S5_GROUP,), gp + (S5_GROUP,)], name="s5_prep")
        a_re, a_im = abar_re.reshape(NCH, 1, S5_SC), abar_im.reshape(NCH, 1, S5_SC)
        bd_re, bd_im = _s5_b_dense(bbar_re), _s5_b_dense(bbar_im)
        cd_re, cd_im = _s5_c_dense(w['s5_c_re'][l]), _s5_c_dense(w['s5_c_im'][l])
        yssm, xre, xim = s5_scan_fwd(z, bd_re, bd_im, cd_re, cd_im, a_re, a_im)
        s5post_p = [w['s5_d'][l].reshape(1, 1, W_MIX), wglu[l].astype(f32)[None], w['s5_b_glu'][l].reshape(1, 1, W_MIX)]
        s5post_rows = [(yssm, 0, W_MIX), (z, C_UA, W_MIX), (z, C_GA, W_MIX)]
        (ya,) = rowwise(s5_post_fn, s5post_rows, s5post_p, [], [(W_MIX, bf16)], name="s5_post_fwd")
        cw, cb = conv_w[l].reshape(RG_CONV, NCH, LANE).transpose(1, 0, 2), _chunks(w['rg_conv_b'][l])
        xc = rg_conv_fwd(z, cw, cb)
        rg_p = [_rg_dense(w['rg_w_a'][l]), _chunks(w['rg_b_a'][l]), _rg_dense(w['rg_w_x'][l]),
                _chunks(w['rg_b_x'][l]), _chunks(w['rg_lambda'][l])]
        ra, rb = rowwise(rg_gate_fn, [(xc, 0, W_MIX)], rg_p, [], [(W_MIX, f32)] * 2, name="rg_gate_fwd",
                         ncol=NCH, tm=TM_CHUNK, rowid=True)
        hb = rg_scan_fwd(ra, rb)
        hg_rows = [(z, C_Q, W_MIX), (z, C_F, W_MIX)]
        hg_p = [_chunks(lbs[l].reshape(W_MIX))]
        qs, kk, gcum = rowwise(hg_pre_fn, hg_rows, hg_p, [tri, tri_t], [(W_MIX, f32)] * 3, name="hg_pre_fwd", ncol=NCH, tm=TM_CHUNK)
        oc, sall = hg_chunk_fwd(qs, kk, gcum, z)
        bp_rows = [(hb, 0, W_MIX), (z, C_GB, W_MIX), (oc, 0, W_MIX), (z, C_GC, W_MIX)]
        bp_p = [_chunks(w['hg_norm_w'][l])]
        yb, yc = rowwise(branch_prep_fn, bp_rows, bp_p, [], [(W_MIX, bf16)] * 2, name="branch_prep_fwd", ncol=NCH, tm=TM_CHUNK)
        ys = [ya, yb, yc]
        br = [mm(ys[n], wbr[l][n], name="mm_branch") for n in range(N_BRANCH)]
        mg_rows = [(br[n], 0, D_MODEL) for n in range(N_BRANCH)] + [(z, C_GATE + n * D_MODEL, D_MODEL) for n in range(N_BRANCH)]
        (merged,) = rowwise(merge_fn, mg_rows, [], [], [(D_MODEL, bf16)], name="merge_fwd", ncol=2)
        x_new = mm(merged, wout[l], add=x, name="mm_out")
        s.update(x=x, h=h, z=z, s5p=s5p, s5=(bd_re, bd_im, cd_re, cd_im, a_re, a_im), xre=xre, xim=xim,
                 s5post_rows=s5post_rows, s5post_p=s5post_p, cw=cw, xc=xc, rg_p=rg_p, ra=ra, hb=hb,
                 hg_rows=hg_rows, hg_p=hg_p, qs=qs, kk=kk, gcum=gcum, sall=sall, bp_rows=bp_rows, bp_p=bp_p,
                 ys=ys, mg_rows=mg_rows, merged=merged, nw=nw)
        saved.append(s)
        x = x_new
        if l + 1 < DEPTH:
            win[l + 1], wglu[l + 1], wbr[l + 1], wout[l + 1] = gather_wait(handles, big_axis, x, name=f"ag_wait_{l + 1}")

    fnw = w['final_norm_w'].reshape(1, 1, D_MODEL)
    ones = jnp.ones((t, 1), f32)
    dx, d_fnw, loss_sum = rowwise_vjp(loss_fn, [(x, 0, D_MODEL), (tgt, 0, D_MODEL)], [fnw], [], [(ones, 0, 1)],
                                      [(0, f32)], name="loss_head", sum_primal=0)
    loss = lax.psum(loss_sum.reshape(()), ("x", "y", "c"))

    small_g = {k: [None] * DEPTH for k in SMALL if k != 'final_norm_w'}
    own_sums, in_flight = [None] * DEPTH, [None] * DEPTH
    d_lbs = [None] * DEPTH
    token = None
    for l in reversed(range(DEPTH)):
        s = saved[l]
        z = s['z']
        dxb = dx.astype(bf16)
        d_merged = mm(dxb, wout[l].T, after=token, name="mm_out_dx", out_dtype=bf16)
        d_wout = mm(s['merged'].T, dxb, name="mm_out_dw", out_dtype=bf16)
        mg = rowwise_vjp(merge_fn, s['mg_rows'], [], [], [(d_merged, 0, D_MODEL)],
                         [(n, bf16) for n in range(2 * N_BRANCH)], name="merge_bwd", ncol=2)
        d_br, d_gl = mg[:N_BRANCH], mg[N_BRANCH:]
        d_ys = [mm(d_br[n], wbr[l][n].T, name="mm_branch_dx") for n in range(N_BRANCH)]
        d_wbr = jnp.stack([mm(s['ys'][n].T, d_br[n], name="mm_branch_dw", out_dtype=bf16) for n in range(N_BRANCH)])
        d_hb, d_gb, d_oc, d_gc, d_hnw = rowwise_vjp(
            branch_prep_fn, s['bp_rows'], s['bp_p'], [], [(d_ys[1], 0, W_MIX), (d_ys[2], 0, W_MIX)],
            [(0, f32), (1, bf16), (2, f32), (3, bf16)], name="branch_prep_bwd", ncol=NCH, tm=TM_CHUNK)
        small_g['hg_norm_w'][l] = d_hnw.reshape(W_MIX)
        d_qs, d_kk, d_gcum, d_i = hg_chunk_bwd(d_oc, s['qs'], s['kk'], s['gcum'], z, s['sall'])
        d_q, d_f, d_lb = rowwise_vjp(
            hg_pre_fn, s['hg_rows'], s['hg_p'], [tri, tri_t], [(d_qs, 0, W_MIX), (d_kk, 0, W_MIX), (d_gcum, 0, W_MIX)],
            [(0, bf16), (1, bf16)], name="hg_pre_bwd", ncol=NCH, tm=TM_CHUNK)
        d_lbs[l] = d_lb.reshape(1, W_MIX)
        d_ra, d_rb = rg_scan_bwd(d_hb, s['ra'], s['hb'])
        rg = rowwise_vjp(rg_gate_fn, [(s['xc'], 0, W_MIX)], s['rg_p'], [], [(d_ra, 0, W_MIX), (d_rb, 0, W_MIX)],
                         [(0, f32)], name="rg_gate_bwd", ncol=NCH, tm=TM_CHUNK, rowid=True)
        d_xc, d_wa, d_ba, d_wx, d_bx, d_lam = rg
        d_xb, d_cw, d_cb = rg_conv_bwd(d_xc, z, s['cw'])
        small_g['rg_w_a'][l], small_g['rg_w_x'][l] = _rg_undense(d_wa), _rg_undense(d_wx)
        small_g['rg_b_a'][l], small_g['rg_b_x'][l] = d_ba.reshape(W_MIX), d_bx.reshape(W_MIX)
        small_g['rg_lambda'][l] = d_lam.reshape(W_MIX)
        small_g['rg_conv_w'][l] = d_cw.transpose(1, 0, 2).reshape(RG_CONV, W_MIX)
        small_g['rg_conv_b'][l] = d_cb.reshape(W_MIX)
        d_yssm, d_u1, d_ga, d_d, d_wglu, d_bglu = rowwise_vjp(
            s5_post_fn, s['s5post_rows'], s['s5post_p'], [], [(d_ys[0], 0, W_MIX)],
            [(0, bf16), (1, bf16), (2, bf16)], name="s5_post_bwd")
        small_g['s5_d'][l], small_g['s5_b_glu'][l] = d_d.reshape(W_MIX), d_bglu.reshape(W_MIX)
        d_ua, d_bdre, d_bdim, d_cdre, d_cdim, d_are, d_aim = s5_scan_bwd(d_yssm, d_u1, z, s['xre'], s['xim'], *s['s5'])
        small_g['s5_c_re'][l], small_g['s5_c_im'][l] = _s5_c_undense(d_cdre), _s5_c_undense(d_cdim)
        gp = (S5_GROUPS, S5_STATE, 1)
        s5g = whole_vjp(s5_prep_fn, s['s5p'],
                        [d_are.reshape(gp), d_aim.reshape(gp), _s5_b_undense(d_bdre), _s5_b_undense(d_bdim)],
                        name="s5_prep_bwd")
        small_g['s5_lambda_re'][l] = s5g[0].reshape(S5_GROUPS, S5_STATE)
        small_g['s5_lambda_im'][l] = s5g[1].reshape(S5_GROUPS, S5_STATE)
        small_g['s5_log_step'][l] = s5g[2].reshape(S5_GROUPS)
        small_g['s5_b_re'][l], small_g['s5_b_im'][l] = s5g[3], s5g[4]
        dz = jnp.concatenate([d_ua, d_ga, d_xb, d_gb, d_q, d_f, d_i, d_gc] + list(d_gl), axis=1)
        d_h = mm(dz, win[l].T, name="mm_in_dx", tk=2048)
        d_win = mm(s['h'].T, dz, name="mm_in_dw", out_dtype=bf16)
        dx, d_nw = rowwise_vjp(ln_res_fn, [(s['x'], 0, D_MODEL)], [s['nw']], [], [(d_h, 0, D_MODEL), (dx, 0, D_MODEL)],
                               [(0, f32)], name="ln_bwd")
        small_g['norm_w'][l] = d_nw.reshape(D_MODEL)
        sums = [pair_sums(g, ax, name="rs_" + k) for g, ax, k in zip((d_win, d_wglu[0], d_wbr, d_wout), big_axis, big)]
        own_sums[l] = [own for own, _ in sums]
        in_flight[l], token = send_pairs_start([pair for _, pair in sums], name=f"rs_start_{l}")
    d_lb_raw = whole_vjp(lb_prep_fn, lb_rows, d_lbs, name="lb_prep_bwd")
    small_g['hg_lower_bounds'] = [r.reshape(W_MIX) for r in d_lb_raw]

    g_small = [jnp.stack(small_g[k]) for k in SMALL if k != 'final_norm_w'] + [d_fnw.reshape(D_MODEL)]
    shapes = [g.shape for g in g_small]
    g_all = all_gather(_pack(g_small)[0], 0, name="ag_small_grads")
    g_all = g_all.reshape(N_DEV, -1, LANE)

    def local(d, k):
        return jnp.zeros(shapes[SMALL.index(k)], f32) if k == 'rg_conv_w' else d[k]
    packed = [_pack([local(d, k) for k in SMALL]) for d in (w, m, v)]
    outs = adamw(*packed, [[g_all]], name="adamw_small", tr=512)
    res = {}
    for kind, buf in zip(('grad', 'delta', 'new_m', 'new_v'), outs):
        for k, a in zip(SMALL, _unpack(buf, shapes)):
            res[kind + '_' + k] = a
    g_cw = lax.dynamic_slice_in_dim(res['grad_rg_conv_w'], me * LANE, LANE, axis=2)
    cw3 = lambda a: a.reshape(1, DEPTH * RG_CONV, LANE)
    outs = adamw(cw3(w['rg_conv_w']), cw3(m['rg_conv_w']), cw3(v['rg_conv_w']), [[cw3(g_cw)]], name="adamw_conv_w", tr=16)
    for kind, buf in zip(('grad', 'delta', 'new_m', 'new_v'), outs):
        res[kind + '_rg_conv_w'] = buf.reshape(DEPTH, RG_CONV, LANE)

    after = outs[0]
    arrived = [send_pairs_wait(in_flight[l], after, name=f"rs_wait_{l}") for l in range(DEPTH)]
    for i, (k, tr) in enumerate((('w_in', 32), ('s5_w_glu', 32), ('w_branch', 128), ('w_out', 32))):
        shp = w[k].shape
        r3 = lambda a: a.reshape(DEPTH, -1, shp[-1])
        slots = [[own_sums[l][i], arrived[l][i]] for l in range(DEPTH)]
        outs = adamw(r3(w[k]), r3(m[k]), r3(v[k]), slots, name="adamw_" + k, tr=tr)
        for kind, buf in zip(('grad', 'delta', 'new_m', 'new_v'), outs):
            res[kind + '_' + k] = buf.reshape(shp)

    return (loss, dx[None]) + tuple(res[kind + '_' + k] for kind in ('grad', 'delta', 'new_m', 'new_v') for k in WEIGHTS)


def kernel(x, norm_w, w_in, s5_lambda_re, s5_lambda_im, s5_log_step, s5_b_re, s5_b_im, s5_c_re, s5_c_im, s5_d, s5_w_glu, s5_b_glu, rg_conv_w, rg_conv_b, rg_w_a, rg_b_a, rg_w_x, rg_b_x, rg_lambda, hg_lower_bounds, hg_norm_w, w_branch, w_out, final_norm_w, loss_target, m_norm_w, m_w_in, m_s5_lambda_re, m_s5_lambda_im, m_s5_log_step, m_s5_b_re, m_s5_b_im, m_s5_c_re, m_s5_c_im, m_s5_d, m_s5_w_glu, m_s5_b_glu, m_rg_conv_w, m_rg_conv_b, m_rg_w_a, m_rg_b_a, m_rg_w_x, m_rg_b_x, m_rg_lambda, m_hg_lower_bounds, m_hg_norm_w, m_w_branch, m_w_out, m_final_norm_w, v_norm_w, v_w_in, v_s5_lambda_re, v_s5_lambda_im, v_s5_log_step, v_s5_b_re, v_s5_b_im, v_s5_c_re, v_s5_c_im, v_s5_d, v_s5_w_glu, v_s5_b_glu, v_rg_conv_w, v_rg_conv_b, v_rg_w_a, v_rg_b_a, v_rg_w_x, v_rg_b_x, v_rg_lambda, v_hg_lower_bounds, v_hg_norm_w, v_w_branch, v_w_out, v_final_norm_w):
    w = dict(zip(WEIGHTS, (norm_w, w_in, s5_lambda_re, s5_lambda_im, s5_log_step, s5_b_re, s5_b_im, s5_c_re, s5_c_im, s5_d, s5_w_glu, s5_b_glu, rg_conv_w, rg_conv_b, rg_w_a, rg_b_a, rg_w_x, rg_b_x, rg_lambda, hg_lower_bounds, hg_norm_w, w_branch, w_out, final_norm_w)))
    m = dict(zip(WEIGHTS, (m_norm_w, m_w_in, m_s5_lambda_re, m_s5_lambda_im, m_s5_log_step, m_s5_b_re, m_s5_b_im, m_s5_c_re, m_s5_c_im, m_s5_d, m_s5_w_glu, m_s5_b_glu, m_rg_conv_w, m_rg_conv_b, m_rg_w_a, m_rg_b_a, m_rg_w_x, m_rg_b_x, m_rg_lambda, m_hg_lower_bounds, m_hg_norm_w, m_w_branch, m_w_out, m_final_norm_w)))
    v = dict(zip(WEIGHTS, (v_norm_w, v_w_in, v_s5_lambda_re, v_s5_lambda_im, v_s5_log_step, v_s5_b_re, v_s5_b_im, v_s5_c_re, v_s5_c_im, v_s5_d, v_s5_w_glu, v_s5_b_glu, v_rg_conv_w, v_rg_conv_b, v_rg_w_a, v_rg_b_a, v_rg_w_x, v_rg_b_x, v_rg_lambda, v_hg_lower_bounds, v_hg_norm_w, v_w_branch, v_w_out, v_final_norm_w)))
    return _step(x[0], loss_target[0], w, m, v)
```

```python
import functools
import math

import jax
import jax.numpy as jnp
from jax import lax
from jax.experimental import pallas as pl
from jax.experimental.pallas import tpu as pltpu

f32 = jnp.float32
bf16 = jnp.bfloat16

D_MODEL = 2048
W_MIX = 1024
DEPTH = 4
N_BRANCH = 3
N_IN = 8 * W_MIX + N_BRANCH * D_MODEL
S5_GROUPS, S5_STATE, S5_GROUP = 64, 64, 16
RG_BLOCKS, RG_BLOCK, RG_CONV, RG_C = 16, 64, 4, 8.0
HG_HEADS, HG_DK = 8, 128
HG_SUB = 16
EPS = 1e-6
ADAM_LR, ADAM_B1, ADAM_B2, ADAM_EPS, ADAM_WD, ADAM_STEP = 0.001, 0.9, 0.999, 1e-08, 0.01, 10

N_DEV = 8
LANE = 128
NCH = W_MIX // LANE
TM_CHUNK = 1024
VMEM_LIMIT = 56 * 1024 * 1024
MESH = pl.DeviceIdType.MESH
ANY = pl.BlockSpec(memory_space=pl.ANY)
HIGHEST = lax.Precision.HIGHEST

C_UA, C_GA, C_XB, C_GB, C_Q, C_F, C_I, C_GC, C_GATE = (W_MIX * k for k in range(9))


def _cparams(sem=None):
    return pltpu.CompilerParams(dimension_semantics=sem, vmem_limit_bytes=VMEM_LIMIT)


@jax.custom_vjp
def bdot(a, w):
    return jnp.dot(a.astype(bf16), w.astype(bf16), preferred_element_type=f32)


def _bdot_fwd(a, w):
    return bdot(a, w), (a, w)


def _bdot_bwd(res, g):
    a, w = res
    gb = g.astype(bf16)
    da = lax.dot_general(gb, w.astype(bf16), (((1,), (1,)), ((), ())), preferred_element_type=f32)
    dw = lax.dot_general(a.astype(bf16), gb, (((0,), (0,)), ((), ())), preferred_element_type=f32)
    return da, dw


bdot.defvjp(_bdot_fwd, _bdot_bwd)


def _blockmm(c, a):
    n = c.shape[0]
    return jnp.concatenate([jnp.dot(c, a[i:i + n], preferred_element_type=f32, precision=HIGHEST)
                            for i in range(0, a.shape[0], n)], axis=0)


@jax.custom_vjp
def cdot(c, ct, a):
    return _blockmm(c, a)


def _cdot_fwd(c, ct, a):
    return cdot(c, ct, a), (c, ct)


def _cdot_bwd(res, g):
    c, ct = res
    return jnp.zeros_like(c), jnp.zeros_like(ct), _blockmm(ct, g)


cdot.defvjp(_cdot_fwd, _cdot_bwd)


def mm(a, b, *, name, out_dtype=f32, add=None, after=None, bt=False, tm=512, tn=1024, tk=4096):
    m, k = a.shape
    n = b.shape[0] if bt else b.shape[1]
    tm, tn, tk = min(tm, m), min(tn, n), min(tk, k)
    assert m % tm == 0 and n % tn == 0 and k % tk == 0
    nk = k // tk
    dims = (((1,), (1,)), ((), ())) if bt else (((1,), (0,)), ((), ()))

    def body(*refs):
        a_ref, b_ref = refs[:2]
        r_ref = refs[2] if add is not None else None
        o_ref, acc_ref = refs[-2:]
        kk = pl.program_id(2)
        part = lax.dot_general(a_ref[...], b_ref[...], dims, preferred_element_type=f32)

        @pl.when(kk == 0)
        def _():
            acc_ref[...] = part

        @pl.when(kk > 0)
        def _():
            acc_ref[...] = acc_ref[...] + part

        @pl.when(kk == nk - 1)
        def _():
            acc = acc_ref[...]
            if add is not None:
                acc = acc + r_ref[...]
            o_ref[...] = acc.astype(out_dtype)

    b_spec = pl.BlockSpec((tn, tk), lambda i, j, q: (j, q)) if bt else pl.BlockSpec((tk, tn), lambda i, j, q: (q, j))
    in_specs = [pl.BlockSpec((tm, tk), lambda i, j, q: (i, q)), b_spec]
    args = [a, b]
    if add is not None:
        in_specs.append(pl.BlockSpec((tm, tn), lambda i, j, q: (i, j)))
        args.append(add)
    if after is not None:
        in_specs.append(pl.BlockSpec(after.shape, lambda i, j, q: (0, 0)))
        args.append(after)
    return pl.pallas_call(
        body, name=name, grid=(m // tm, n // tn, nk), in_specs=in_specs,
        out_specs=pl.BlockSpec((tm, tn), lambda i, j, q: (i, j)),
        out_shape=jax.ShapeDtypeStruct((m, n), out_dtype),
        scratch_shapes=[pltpu.VMEM((tm, tn), f32)],
        compiler_params=_cparams(("parallel", "parallel", "arbitrary")),
    )(*args)


def _row_spec(tm, wc, col_off):
    base = col_off // wc
    assert col_off % wc == 0
    return pl.BlockSpec((tm, wc), lambda j, i: (i, base + j))


def _slab_spec(arr):
    r, c = arr.shape[1:]
    if arr.shape[0] == 1:
        return pl.BlockSpec((1, r, c), lambda j, i: (0, 0, 0))
    return pl.BlockSpec((1, r, c), lambda j, i: (j, 0, 0))


def rowwise(fn, rows, params, consts, outs, *, name, tm=256, ncol=1, rowid=False):
    t = rows[0][0].shape[0]
    tm = min(tm, t)
    nr, npar, nc, no = len(rows), len(params), len(consts), len(outs)

    def body(*refs):
        r = [refs[k][...].astype(f32) for k in range(nr)]
        p = [refs[nr + k][0] for k in range(npar + nc)]
        extra = ()
        if rowid:
            extra = (pl.program_id(1) * tm + lax.broadcasted_iota(jnp.int32, (tm, 1), 0),)
        res = fn(*extra, *r, *p)
        for k in range(no):
            refs[nr + npar + nc + k][...] = res[k].astype(outs[k][1])

    in_specs = [_row_spec(tm, w // ncol, off) for (_, off, w) in rows]
    in_specs += [_slab_spec(a) for a in list(params) + list(consts)]
    out_specs = [pl.BlockSpec((tm, w // ncol), lambda j, i: (i, j)) for (w, _) in outs]
    out_shape = [jax.ShapeDtypeStruct((t, w), dt) for (w, dt) in outs]
    return pl.pallas_call(
        body, name=name, grid=(ncol, t // tm), in_specs=in_specs, out_specs=out_specs, out_shape=out_shape,
        compiler_params=_cparams(("parallel", "parallel")),
    )(*[r[0] for r in rows], *params, *consts)


def rowwise_vjp(fn, rows, params, consts, cts, d_rows, *, name, tm=256, ncol=1, rowid=False, sum_primal=None):
    t = rows[0][0].shape[0]
    tm = min(tm, t)
    nr, npar, nc, nct, ndr = len(rows), len(params), len(consts), len(cts), len(d_rows)

    def body(*refs):
        i = pl.program_id(1)
        r = [refs[k][...].astype(f32) for k in range(nr)]
        p = [refs[nr + k][0] for k in range(npar)]
        c = [refs[nr + npar + k][0] for k in range(nc)]
        g = [refs[nr + npar + nc + k][...].astype(f32) for k in range(nct)]
        orefs = refs[nr + npar + nc + nct:]
        extra = ()
        if rowid:
            extra = (i * tm + lax.broadcasted_iota(jnp.int32, (tm, 1), 0),)
        res, vjp = jax.vjp(lambda *v: fn(*extra, *v, *c), *r, *p)
        grads = vjp(tuple(g))
        for k, (idx, dt) in enumerate(d_rows):
            orefs[k][...] = grads[idx].astype(dt)
        acc = [grads[nr + k] for k in range(npar)]
        if sum_primal is not None:
            acc.append(jnp.sum(res[sum_primal], axis=0, keepdims=True))

        @pl.when(i == 0)
        def _():
            for k, a in enumerate(acc):
                orefs[ndr + k][0] = a

        @pl.when(i > 0)
        def _():
            for k, a in enumerate(acc):
                orefs[ndr + k][0] = orefs[ndr + k][0] + a

    in_specs = [_row_spec(tm, w // ncol, off) for (_, off, w) in rows]
    in_specs += [_slab_spec(a) for a in list(params) + list(consts)]
    in_specs += [_row_spec(tm, w // ncol, off) for (_, off, w) in cts]
    out_specs = [pl.BlockSpec((tm, rows[idx][2] // ncol), lambda j, i: (i, j)) for (idx, _) in d_rows]
    out_shape = [jax.ShapeDtypeStruct((t, rows[idx][2]), dt) for (idx, dt) in d_rows]
    for a in params:
        out_specs.append(pl.BlockSpec((1,) + a.shape[1:], lambda j, i: (j, 0, 0)))
        out_shape.append(jax.ShapeDtypeStruct(a.shape, f32))
    if sum_primal is not None:
        w = cts[sum_primal][2]
        out_specs.append(pl.BlockSpec((1, 1, w // ncol), lambda j, i: (j, 0, 0)))
        out_shape.append(jax.ShapeDtypeStruct((ncol, 1, w // ncol), f32))
    return pl.pallas_call(
        body, name=name, grid=(ncol, t // tm), in_specs=in_specs, out_specs=out_specs, out_shape=out_shape,
        compiler_params=_cparams(("parallel", "arbitrary")),
    )(*[r[0] for r in rows], *params, *consts, *[c[0] for c in cts])


VM = pl.BlockSpec(memory_space=pltpu.VMEM)


def whole(fn, ins, outs, *, name):
    def body(*refs):
        res = fn(*[r[...] for r in refs[:len(ins)]])
        for k, o in enumerate(refs[len(ins):]):
            o[...] = res[k]
    return pl.pallas_call(body, name=name, in_specs=[VM] * len(ins), out_specs=[VM] * len(outs),
                          out_shape=[jax.ShapeDtypeStruct(s, f32) for s in outs],
                          compiler_params=_cparams())(*ins)


def whole_vjp(fn, ins, cts, *, name):
    n = len(ins)

    def body(*refs):
        _, vjp = jax.vjp(fn, *[r[...] for r in refs[:n]])
        grads = vjp(tuple(r[...] for r in refs[n:n + len(cts)]))
        for k, o in enumerate(refs[n + len(cts):]):
            o[...] = grads[k]
    return pl.pallas_call(body, name=name, in_specs=[VM] * (n + len(cts)), out_specs=[VM] * n,
                          out_shape=[jax.ShapeDtypeStruct(a.shape, f32) for a in ins],
                          compiler_params=_cparams())(*ins, *cts)


def ln_fn(x, w):
    return (x * lax.rsqrt(jnp.mean(x * x, axis=-1, keepdims=True) + EPS) * w,)


def ln_res_fn(x, w):
    return ln_fn(x, w)[0], x


def loss_fn(x, tgt, w):
    y = ln_fn(x, w)[0]
    return (0.5 * jnp.mean(jnp.square(y - tgt), axis=-1, keepdims=True),)


def s5_prep_fn(lam_re, lam_im, log_step, b_re, b_im):
    step = jnp.exp(log_step)
    mag = jnp.exp(lam_re * step)
    ang = lam_im * step
    abar_re = mag * jnp.cos(ang)
    abar_im = mag * jnp.sin(ang)
    num_re = abar_re - 1.0
    num_im = abar_im
    den = lam_re * lam_re + lam_im * lam_im
    coef_re = (num_re * lam_re + num_im * lam_im) / den
    coef_im = (num_im * lam_re - num_re * lam_im) / den
    bbar_re = coef_re * b_re - coef_im * b_im
    bbar_im = coef_re * b_im + coef_im * b_re
    return abar_re, abar_im, bbar_re, bbar_im


def lb_prep_fn(r0, r1, r2, r3):
    m = jnp.maximum(jnp.maximum(r0, r1), jnp.maximum(r2, r3))
    e0, e1, e2, e3 = jnp.exp(r0 - m), jnp.exp(r1 - m), jnp.exp(r2 - m), jnp.exp(r3 - m)
    s = e0 + e1 + e2 + e3
    p0, p1, p2, p3 = e0 / s, e1 / s, e2 / s, e3 / s
    c1 = p0 + p1
    c2 = c1 + p2
    c3 = c2 + p3
    return p0 - p0, c1 - p0, c2 - p0, c3 - p0


def s5_post_fn(yssm, u, ga, d, wglu, bglu):
    y = jax.nn.gelu(yssm + d * u)
    y = y * jax.nn.sigmoid(bdot(y, wglu) + bglu)
    return (y * jax.nn.silu(ga),)


def rg_gate_fn(tglob, xc, wa, ba, wx, bx, lam):
    r = jax.nn.sigmoid(bdot(xc, wa) + ba)
    i = jax.nn.sigmoid(bdot(xc, wx) + bx)
    log_a = -RG_C * r * jax.nn.softplus(-lam)
    a = jnp.exp(log_a)
    mult = jnp.sqrt(-jnp.tanh(log_a) * (a * a + 1.0))
    mult = jnp.where(tglob == 0, 1.0, mult)
    return a, mult * (i * xc)


def hg_pre_fn(q, fl, lb, tri, tri_t):
    f = lb + (1.0 - lb) * jax.nn.sigmoid(fl)
    return jax.nn.silu(q), 1.0 - f, cdot(tri, tri_t, jnp.log(f))


def branch_prep_fn(hb, gb, oc, gc, nw):
    yb = hb * jax.nn.silu(gb)
    on = oc * lax.rsqrt(jnp.mean(oc * oc, axis=-1, keepdims=True) + EPS) * nw
    return yb, on * jax.nn.silu(gc)


def merge_fn(b0, b1, b2, g0, g1, g2):
    return (jax.nn.sigmoid(g0) * b0 + jax.nn.sigmoid(g1) * b1 + jax.nn.sigmoid(g2) * b2,)


S5_TB = 512
S5_SC = 512


SEG = 8


def _shift(v, k, pos, period, reverse, fill):
    if reverse:
        return jnp.where(pos < period - k, pltpu.roll(v, v.shape[0] - k, 0), fill)
    return jnp.where(pos >= k, pltpu.roll(v, k, 0), fill)


def _cmul(ar, ai, br, bi):
    return ar * br - ai * bi, ar * bi + ai * br


def lti_scan(s_re, s_im, ar, ai, cr, ci, reverse=False):
    nl, n, _ = s_re.shape
    seg = n // SEG
    sub = lax.broadcasted_iota(jnp.int32, (SEG, 1), 0)
    tile = lambda v, c: v[:, LANE * c:LANE * (c + 1)]
    wide = lambda v, c: jnp.broadcast_to(tile(v, c), (SEG, LANE))
    a = [(wide(ar, c), wide(ai, c)) for c in range(nl)]
    rows = lambda j: pl.ds(seg - 1 - j if reverse else j, SEG, stride=seg)

    def local(j, xs):
        out = []
        for c in range(nl):
            nr, ni = _cmul(a[c][0], a[c][1], xs[2 * c], xs[2 * c + 1])
            nr, ni = nr + s_re[c, rows(j), :], ni + s_im[c, rows(j), :]
            s_re[c, rows(j), :] = nr
            s_im[c, rows(j), :] = ni
            out += [nr, ni]
        return tuple(out)

    ends = lax.fori_loop(0, seg, local, tuple(jnp.zeros((SEG, LANE), f32) for _ in range(2 * nl)), unroll=8)
    first, last = (SEG - 1, 0) if reverse else (0, SEG - 1)
    entering, leaving = [], []
    for c in range(nl):
        pr, pi_ = a[c]
        for _ in range(seg.bit_length() - 1):
            pr, pi_ = _cmul(pr, pi_, pr, pi_)
        kr, ki = tile(cr, c), tile(ci, c)
        jr, ji = _cmul(pr, pi_, kr, ki)
        xr = ends[2 * c] + jnp.where(sub == first, jr, 0.0)
        xi = ends[2 * c + 1] + jnp.where(sub == first, ji, 0.0)
        k = 1
        while k < SEG:
            tr, ti = _cmul(pr, pi_, _shift(xr, k, sub, SEG, reverse, 0.0), _shift(xi, k, sub, SEG, reverse, 0.0))
            xr, xi = xr + tr, xi + ti
            pr, pi_ = _cmul(pr, pi_, pr, pi_)
            k *= 2
        entering.append((_shift(xr, 1, sub, SEG, reverse, kr), _shift(xi, 1, sub, SEG, reverse, ki)))
        leaving.append((xr[last:last + 1, :], xi[last:last + 1, :]))

    def join(j, ws):
        out = []
        for c in range(nl):
            wr, wi = ws[2 * c], ws[2 * c + 1]
            er, ei = _cmul(wr, wi, entering[c][0], entering[c][1])
            s_re[c, rows(j), :] = s_re[c, rows(j), :] + er
            s_im[c, rows(j), :] = s_im[c, rows(j), :] + ei
            out += list(_cmul(wr, wi, a[c][0], a[c][1]))
        return tuple(out)

    lax.fori_loop(0, seg, join, tuple(v for c in range(nl) for v in a[c]), unroll=8)
    return (jnp.concatenate([l[0] for l in leaving], axis=1), jnp.concatenate([l[1] for l in leaving], axis=1))


def _to_slabs(ref, v):
    for c in range(ref.shape[0]):
        ref[c] = v[:, LANE * c:LANE * (c + 1)]


def _from_slabs(ref):
    return jnp.concatenate([ref[c] for c in range(ref.shape[0])], axis=1)


def tv_scan(aa, bb, carry, reverse=False):
    n = aa.shape[0]
    row = lax.broadcasted_iota(jnp.int32, (n, 1), 0)
    k = 1
    while k < n:
        bb = bb + aa * _shift(bb, k, row, n, reverse, 0.0)
        aa = aa * _shift(aa, k, row, n, reverse, 1.0)
        k *= 2
    h = bb + aa * carry
    last = 0 if reverse else n - 1
    return h, h[last:last + 1, :]


def s5_scan_fwd(z, bd_re, bd_im, cd_re, cd_im, a_re, a_im):
    t = z.shape[0]
    tb = min(S5_TB, t)

    def body(u_ref, bre, bim, cre, cim, are, aim, y_ref, xre_ref, xim_ref, car_re, car_im, s_re, s_im):
        @pl.when(pl.program_id(1) == 0)
        def _():
            car_re[...] = jnp.zeros_like(car_re)
            car_im[...] = jnp.zeros_like(car_im)

        u = u_ref[...].astype(bf16)
        _to_slabs(s_re, jnp.dot(u, bre[0], preferred_element_type=f32))
        _to_slabs(s_im, jnp.dot(u, bim[0], preferred_element_type=f32))
        car_re[...], car_im[...] = lti_scan(s_re, s_im, are[0], aim[0], car_re[...], car_im[...])
        xr, xi = _from_slabs(s_re), _from_slabs(s_im)
        xre_ref[...] = xr
        xim_ref[...] = xi
        y_ref[...] = (jnp.dot(xr.astype(bf16), cre[0], preferred_element_type=f32)
                      - jnp.dot(xi.astype(bf16), cim[0], preferred_element_type=f32))

    chunk = lambda r, c: pl.BlockSpec((1, r, c), lambda j, i: (j, 0, 0))
    return pl.pallas_call(
        body, name="s5_scan_fwd", grid=(NCH, t // tb),
        in_specs=[pl.BlockSpec((tb, LANE), lambda j, i: (i, C_UA // LANE + j)),
                  chunk(LANE, S5_SC), chunk(LANE, S5_SC), chunk(S5_SC, LANE), chunk(S5_SC, LANE),
                  chunk(1, S5_SC), chunk(1, S5_SC)],
        out_specs=[pl.BlockSpec((tb, LANE), lambda j, i: (i, j)),
                   pl.BlockSpec((tb, S5_SC), lambda j, i: (i, j)),
                   pl.BlockSpec((tb, S5_SC), lambda j, i: (i, j))],
        out_shape=[jax.ShapeDtypeStruct((t, W_MIX), f32),
                   jax.ShapeDtypeStruct((t, NCH * S5_SC), f32),
                   jax.ShapeDtypeStruct((t, NCH * S5_SC), f32)],
        scratch_shapes=[pltpu.VMEM((1, S5_SC), f32)] * 2 + [pltpu.VMEM((S5_SC // LANE, tb, LANE), f32)] * 2,
        compiler_params=_cparams(("parallel", "arbitrary")),
    )(z, bd_re.astype(bf16), bd_im.astype(bf16), cd_re.astype(bf16), cd_im.astype(bf16), a_re, a_im)


def s5_scan_bwd(dy, du1, z, xre, xim, bd_re, bd_im, cd_re, cd_im, a_re, a_im):
    t = z.shape[0]
    tb = min(S5_TB, t)
    nt = t // tb

    def body(dy_ref, du1_ref, u_ref, xre_ref, xim_ref, hre_ref, him_ref, bre, bim, cre, cim, are, aim,
             du_ref, dbre, dbim, dcre, dcim, dare, daim, car_re, car_im, s_re, s_im):
        step = pl.program_id(1)
        tt = nt - 1 - step

        @pl.when(step == 0)
        def _():
            car_re[...] = jnp.zeros_like(car_re)
            car_im[...] = jnp.zeros_like(car_im)

        nt_dims = (((1,), (1,)), ((), ()))
        tn_dims = (((0,), (0,)), ((), ()))
        dyb = dy_ref[...].astype(bf16)
        row = lax.broadcasted_iota(jnp.int32, (tb, 1), 0)
        xr, xi = xre_ref[...], xim_ref[...]
        ar, ai = are[0], aim[0]
        _to_slabs(s_re, lax.dot_general(dyb, cre[0], nt_dims, preferred_element_type=f32))
        _to_slabs(s_im, -lax.dot_general(dyb, cim[0], nt_dims, preferred_element_type=f32))
        car_re[...], car_im[...] = lti_scan(s_re, s_im, ar, -ai, car_re[...], car_im[...], reverse=True)
        lr, li = _from_slabs(s_re), _from_slabs(s_im)
        lrb, lib = lr.astype(bf16), li.astype(bf16)
        ub = u_ref[...].astype(bf16)
        du = (lax.dot_general(lrb, bre[0], nt_dims, preferred_element_type=f32)
              + lax.dot_general(lib, bim[0], nt_dims, preferred_element_type=f32))
        du_ref[...] = (du + du1_ref[...].astype(f32)).astype(du_ref.dtype)
        live = (tt > 0).astype(f32)
        xpr = jnp.where(row == 0, hre_ref[7:8, :] * live, pltpu.roll(xr, 1, 0))
        xpi = jnp.where(row == 0, him_ref[7:8, :] * live, pltpu.roll(xi, 1, 0))
        acc = [
            lax.dot_general(ub, lrb, tn_dims, preferred_element_type=f32),
            lax.dot_general(ub, lib, tn_dims, preferred_element_type=f32),
            lax.dot_general(xr.astype(bf16), dyb, tn_dims, preferred_element_type=f32),
            -lax.dot_general(xi.astype(bf16), dyb, tn_dims, preferred_element_type=f32),
            jnp.sum(lr * xpr + li * xpi, axis=0, keepdims=True),
            jnp.sum(li * xpr - lr * xpi, axis=0, keepdims=True),
        ]
        outs = [dbre, dbim, dcre, dcim, dare, daim]

        @pl.when(step == 0)
        def _():
            for o, a in zip(outs, acc):
                o[0] = a

        @pl.when(step > 0)
        def _():
            for o, a in zip(outs, acc):
                o[0] = o[0] + a

    chunk = lambda r, c: pl.BlockSpec((1, r, c), lambda j, i: (j, 0, 0))
    rev = lambda w, base=0: pl.BlockSpec((tb, w), lambda j, i: (nt - 1 - i, base + j))
    halo = pl.BlockSpec((8, S5_SC), lambda j, i: (jnp.maximum((nt - 1 - i) * (tb // 8) - 1, 0), j))
    return pl.pallas_call(
        body, name="s5_scan_bwd", grid=(NCH, nt),
        in_specs=[rev(LANE), rev(LANE), rev(LANE, C_UA // LANE), rev(S5_SC), rev(S5_SC), halo, halo,
                  chunk(LANE, S5_SC), chunk(LANE, S5_SC), chunk(S5_SC, LANE), chunk(S5_SC, LANE),
                  chunk(1, S5_SC), chunk(1, S5_SC)],
        out_specs=[rev(LANE), chunk(LANE, S5_SC), chunk(LANE, S5_SC), chunk(S5_SC, LANE), chunk(S5_SC, LANE),
                   chunk(1, S5_SC), chunk(1, S5_SC)],
        out_shape=[jax.ShapeDtypeStruct((t, W_MIX), bf16),
                   jax.ShapeDtypeStruct((NCH, LANE, S5_SC), f32), jax.ShapeDtypeStruct((NCH, LANE, S5_SC), f32),
                   jax.ShapeDtypeStruct((NCH, S5_SC, LANE), f32), jax.ShapeDtypeStruct((NCH, S5_SC, LANE), f32),
                   jax.ShapeDtypeStruct((NCH, 1, S5_SC), f32), jax.ShapeDtypeStruct((NCH, 1, S5_SC), f32)],
        scratch_shapes=[pltpu.VMEM((1, S5_SC), f32)] * 2 + [pltpu.VMEM((S5_SC // LANE, tb, LANE), f32)] * 2,
        compiler_params=_cparams(("parallel", "arbitrary")),
    )(dy, du1, z, xre, xim, xre, xim, bd_re.astype(bf16), bd_im.astype(bf16), cd_re.astype(bf16),
      cd_im.astype(bf16), a_re, a_im)


RG_TB = 512


def rg_conv_fwd(z, cw, cb):
    t = z.shape[0]
    tb = min(RG_TB, t)

    def body(x_ref, h_ref, cw_ref, cb_ref, o_ref):
        live = (pl.program_id(1) > 0).astype(f32)
        ext = jnp.concatenate([h_ref[...] * live, x_ref[...]], axis=0)
        w = cw_ref[0]
        acc = cb_ref[0] + w[3:4, :] * ext[8:, :]
        for k in range(3):
            acc = acc + w[k:k + 1, :] * pltpu.roll(ext, 3 - k, 0)[8:, :]
        o_ref[...] = acc

    base = C_XB // LANE
    chunk = lambda r: pl.BlockSpec((1, r, LANE), lambda j, i: (j, 0, 0))
    return pl.pallas_call(
        body, name="rg_conv_fwd", grid=(NCH, t // tb),
        in_specs=[pl.BlockSpec((tb, LANE), lambda j, i: (i, base + j)),
                  pl.BlockSpec((8, LANE), lambda j, i: (jnp.maximum(i * (tb // 8) - 1, 0), base + j)),
                  chunk(RG_CONV), chunk(1)],
        out_specs=pl.BlockSpec((tb, LANE), lambda j, i: (i, j)),
        out_shape=jax.ShapeDtypeStruct((t, W_MIX), f32),
        compiler_params=_cparams(("parallel", "parallel")),
    )(z, z, cw, cb)


def rg_conv_bwd(dxc, z, cw):
    t = z.shape[0]
    tb = min(RG_TB, t)
    nt = t // tb

    def body(g_ref, gn_ref, x_ref, h_ref, cw_ref, dx_ref, dcw_ref, dcb_ref):
        i = pl.program_id(1)
        g = g_ref[...]
        gext = jnp.concatenate([g, gn_ref[...] * (i < nt - 1).astype(f32)], axis=0)
        xext = jnp.concatenate([h_ref[...] * (i > 0).astype(f32), x_ref[...]], axis=0)
        w = cw_ref[0]
        dx = w[3:4, :] * g
        rows = [None] * RG_CONV
        rows[3] = jnp.sum(g * xext[8:, :], axis=0, keepdims=True)
        for k in range(3):
            s = 3 - k
            dx = dx + w[k:k + 1, :] * pltpu.roll(gext, tb + 8 - s, 0)[:tb, :]
            rows[k] = jnp.sum(g * pltpu.roll(xext, s, 0)[8:, :], axis=0, keepdims=True)
        dx_ref[...] = dx.astype(dx_ref.dtype)
        dcw = jnp.concatenate(rows, axis=0)
        dcb = jnp.sum(g, axis=0, keepdims=True)

        @pl.when(i == 0)
        def _():
            dcw_ref[0] = dcw
            dcb_ref[0] = dcb

        @pl.when(i > 0)
        def _():
            dcw_ref[0] = dcw_ref[0] + dcw
            dcb_ref[0] = dcb_ref[0] + dcb

    base = C_XB // LANE
    chunk = lambda r: pl.BlockSpec((1, r, LANE), lambda j, i: (j, 0, 0))
    return pl.pallas_call(
        body, name="rg_conv_bwd", grid=(NCH, nt),
        in_specs=[pl.BlockSpec((tb, LANE), lambda j, i: (i, j)),
                  pl.BlockSpec((8, LANE), lambda j, i: (jnp.minimum((i + 1) * (tb // 8), t // 8 - 1), j)),
                  pl.BlockSpec((tb, LANE), lambda j, i: (i, base + j)),
                  pl.BlockSpec((8, LANE), lambda j, i: (jnp.maximum(i * (tb // 8) - 1, 0), base + j)),
                  chunk(RG_CONV)],
        out_specs=[pl.BlockSpec((tb, LANE), lambda j, i: (i, j)), chunk(RG_CONV), chunk(1)],
        out_shape=[jax.ShapeDtypeStruct((t, W_MIX), bf16), jax.ShapeDtypeStruct((NCH, RG_CONV, LANE), f32),
                   jax.ShapeDtypeStruct((NCH, 1, LANE), f32)],
        compiler_params=_cparams(("parallel", "arbitrary")),
    )(dxc, dxc, z, z, cw)


def rg_scan_fwd(a, b):
    t = a.shape[0]
    tb = min(RG_TB, t)

    def body(a_ref, b_ref, h_ref, car):
        @pl.when(pl.program_id(1) == 0)
        def _():
            car[...] = jnp.zeros_like(car)

        h_ref[...], car[...] = tv_scan(a_ref[...], b_ref[...], car[...])

    spec = pl.BlockSpec((tb, LANE), lambda j, i: (i, j))
    return pl.pallas_call(
        body, name="rg_scan_fwd", grid=(NCH, t // tb), in_specs=[spec, spec], out_specs=spec,
        out_shape=jax.ShapeDtypeStruct((t, W_MIX), f32), scratch_shapes=[pltpu.VMEM((1, LANE), f32)],
        compiler_params=_cparams(("parallel", "arbitrary")),
    )(a, b)


def rg_scan_bwd(dh, a, h):
    t = a.shape[0]
    tb = min(RG_TB, t)
    nt = t // tb

    def body(g_ref, a_ref, an_ref, h_ref, hp_ref, da_ref, db_ref, car):
        step = pl.program_id(1)
        tt = nt - 1 - step

        @pl.when(step == 0)
        def _():
            car[...] = jnp.zeros_like(car)

        row = lax.broadcasted_iota(jnp.int32, (tb, 1), 0)
        an = an_ref[0:1, :] * (tt < nt - 1).astype(f32)
        aa = jnp.where(row == tb - 1, an, pltpu.roll(a_ref[...], tb - 1, 0))
        lam, car[...] = tv_scan(aa, g_ref[...], car[...], reverse=True)
        hp = jnp.where(row == 0, hp_ref[7:8, :] * (tt > 0).astype(f32), pltpu.roll(h_ref[...], 1, 0))
        da_ref[...] = lam * hp
        db_ref[...] = lam

    rev = pl.BlockSpec((tb, LANE), lambda j, i: (nt - 1 - i, j))
    nxt = pl.BlockSpec((8, LANE), lambda j, i: (jnp.minimum((nt - i) * (tb // 8), t // 8 - 1), j))
    prv = pl.BlockSpec((8, LANE), lambda j, i: (jnp.maximum((nt - 1 - i) * (tb // 8) - 1, 0), j))
    return pl.pallas_call(
        body, name="rg_scan_bwd", grid=(NCH, nt), in_specs=[rev, rev, nxt, rev, prv], out_specs=[rev, rev],
        out_shape=[jax.ShapeDtypeStruct((t, W_MIX), f32)] * 2, scratch_shapes=[pltpu.VMEM((1, LANE), f32)],
        compiler_params=_cparams(("parallel", "arbitrary")),
    )(dh, a, a, h, h)


HG_TB = 256


def _heads(v):
    return jnp.stack([v[:, LANE * h:LANE * (h + 1)] for h in range(HG_HEADS)])


def _unheads(v):
    return jnp.concatenate([v[h] for h in range(HG_HEADS)], axis=-1)


def _bmm(eq, a, b):
    return jnp.einsum(eq, a.astype(bf16), b.astype(bf16), preferred_element_type=f32)


def hg_chunk_fwd(qs, kk, gcum, z):
    t = qs.shape[0]
    tb = min(HG_TB, t)
    nc = tb // HG_SUB

    def body(q_ref, k_ref, g_ref, v_ref, o_ref, sall_ref, st_ref):
        @pl.when(pl.program_id(0) == 0)
        def _():
            st_ref[...] = jnp.zeros_like(st_ref)

        ri = lax.broadcasted_iota(jnp.int32, (1, HG_SUB, 1), 1)

        def chunk(c, carry):
            rows = pl.ds(pl.multiple_of(c * HG_SUB, HG_SUB), HG_SUB)
            q, k, g, v = _heads(q_ref[rows, :]), _heads(k_ref[rows, :]), _heads(g_ref[rows, :]), _heads(v_ref[rows, :])
            st = st_ref[...]
            sall_ref[c] = st
            o = _bmm('htk,hvk->htv', q * jnp.exp(g), st)
            for s in range(HG_SUB):
                p = jnp.where(ri >= s, jnp.exp(jnp.minimum(g - g[:, s:s + 1, :], 0.0)), 0.0)
                col = jnp.sum(q * k[:, s:s + 1, :] * p, axis=-1, keepdims=True)
                o = o + col * v[:, s:s + 1, :]
            gl = g[:, HG_SUB - 1:HG_SUB, :]
            st_ref[...] = st * jnp.exp(gl) + _bmm('htv,htk->hvk', v, k * jnp.exp(gl - g))
            o_ref[rows, :] = _unheads(o)
            return carry

        lax.fori_loop(0, nc, chunk, 0)

    spec = lambda base=0: pl.BlockSpec((tb, W_MIX), lambda i: (i, base))
    return pl.pallas_call(
        body, name="hg_chunk_fwd", grid=(t // tb,),
        in_specs=[spec(), spec(), spec(), spec(C_I // W_MIX)],
        out_specs=[spec(), pl.BlockSpec((nc, HG_HEADS, HG_DK, HG_DK), lambda i: (i, 0, 0, 0))],
        out_shape=[jax.ShapeDtypeStruct((t, W_MIX), f32),
                   jax.ShapeDtypeStruct((t // HG_SUB, HG_HEADS, HG_DK, HG_DK), f32)],
        scratch_shapes=[pltpu.VMEM((HG_HEADS, HG_DK, HG_DK), f32)],
        compiler_params=_cparams(("arbitrary",)),
    )(qs, kk, gcum, z)


def hg_chunk_bwd(do, qs, kk, gcum, z, sall):
    t = qs.shape[0]
    tb = min(HG_TB, t)
    nc = tb // HG_SUB
    nt = t // tb

    def body(do_ref, q_ref, k_ref, g_ref, v_ref, sall_ref, dq_ref, dk_ref, dg_ref, dv_ref, dst_ref):
        @pl.when(pl.program_id(0) == 0)
        def _():
            dst_ref[...] = jnp.zeros_like(dst_ref)

        ri = lax.broadcasted_iota(jnp.int32, (1, HG_SUB, 1), 1)

        def chunk(cc, carry):
            c = nc - 1 - cc
            rows = pl.ds(pl.multiple_of(c * HG_SUB, HG_SUB), HG_SUB)
            q, k, g, v = _heads(q_ref[rows, :]), _heads(k_ref[rows, :]), _heads(g_ref[rows, :]), _heads(v_ref[rows, :])
            d_o = _heads(do_ref[rows, :])
            st = sall_ref[c]
            dsn = dst_ref[...]
            eg = jnp.exp(g)
            qe = q * eg
            gl = g[:, HG_SUB - 1:HG_SUB, :]
            egl = jnp.exp(gl)
            dec = jnp.exp(gl - g)
            kd = k * dec
            dqe = _bmm('htv,hvk->htk', d_o, st)
            dst_ref[...] = _bmm('htv,htk->hvk', d_o, qe) + dsn * egl
            dgl_dec = jnp.sum(dsn * st, axis=1, keepdims=True) * egl
            dv = _bmm('htk,hvk->htv', kd, dsn)
            dkd = _bmm('htv,hvk->htk', v, dsn)
            a1 = jnp.zeros_like(q)
            a2 = jnp.zeros_like(q)
            for s in range(HG_SUB):
                p = jnp.where(ri >= s, jnp.exp(jnp.minimum(g - g[:, s:s + 1, :], 0.0)), 0.0)
                krow = k[:, s:s + 1, :]
                col = jnp.sum(q * krow * p, axis=-1, keepdims=True)
                dcol = jnp.sum(d_o * v[:, s:s + 1, :], axis=-1, keepdims=True)
                dv = jnp.where(ri == s, dv + jnp.sum(col * d_o, axis=1, keepdims=True), dv)
                t1 = dcol * p
                a1 = a1 + t1 * krow
                a2 = jnp.where(ri == s, jnp.sum(t1 * q, axis=1, keepdims=True), a2)
            dgl = jnp.sum(dkd * kd, axis=1, keepdims=True) + dgl_dec
            dg = dqe * qe + q * a1 - k * a2 - dkd * kd
            dg = jnp.where(ri == HG_SUB - 1, dg + dgl, dg)
            dq_ref[rows, :] = _unheads(dqe * eg + a1)
            dk_ref[rows, :] = _unheads(dkd * dec + a2)
            dg_ref[rows, :] = _unheads(dg)
            dv_ref[rows, :] = _unheads(dv).astype(dv_ref.dtype)
            return carry

        lax.fori_loop(0, nc, chunk, 0)

    spec = lambda base=0: pl.BlockSpec((tb, W_MIX), lambda i: (nt - 1 - i, base))
    return pl.pallas_call(
        body, name="hg_chunk_bwd", grid=(nt,),
        in_specs=[spec(), spec(), spec(), spec(), spec(C_I // W_MIX),
                  pl.BlockSpec((nc, HG_HEADS, HG_DK, HG_DK), lambda i: (nt - 1 - i, 0, 0, 0))],
        out_specs=[spec(), spec(), spec(), spec()],
        out_shape=[jax.ShapeDtypeStruct((t, W_MIX), f32)] * 3 + [jax.ShapeDtypeStruct((t, W_MIX), bf16)],
        scratch_shapes=[pltpu.VMEM((HG_HEADS, HG_DK, HG_DK), f32)],
        compiler_params=_cparams(("arbitrary",)),
    )(do, qs, kk, gcum, z, sall)


def adamw(w, m, v, slots, *, name, tr):
    nl, r, c = w.shape
    tr = min(tr, r)
    flat = [a for per_layer in slots for a in per_layer]
    c1 = 1.0 / (1.0 - ADAM_B1 ** ADAM_STEP)
    c2 = 1.0 / (1.0 - ADAM_B2 ** ADAM_STEP)

    def body(*refs):
        w_ref, m_ref, v_ref = refs[:3]
        s_refs = list(refs[3:3 + len(flat)])
        g_ref, d_ref, mo_ref, vo_ref = refs[3 + len(flat):]
        for l in range(nl):
            parts = [s_refs.pop(0) for _ in slots[l]]
            g = None
            for p in parts:
                for s in range(p.shape[0]):
                    term = p[s].astype(f32)
                    g = term if g is None else g + term
            mn = ADAM_B1 * m_ref[l] + (1.0 - ADAM_B1) * g
            vn = ADAM_B2 * v_ref[l] + (1.0 - ADAM_B2) * (g * g)
            g_ref[l] = g
            mo_ref[l] = mn
            vo_ref[l] = vn
            d_ref[l] = -ADAM_LR * ((mn * c1) / (jnp.sqrt(vn * c2) + ADAM_EPS) + ADAM_WD * w_ref[l])

    full = pl.BlockSpec((nl, tr, c), lambda i: (0, i, 0))
    slot = [pl.BlockSpec((a.shape[0], tr, c), lambda i: (0, i, 0)) for a in flat]
    return pl.pallas_call(
        body, name=name, grid=(r // tr,), in_specs=[full] * 3 + slot, out_specs=[full] * 4,
        out_shape=[jax.ShapeDtypeStruct(w.shape, f32)] * 4, compiler_params=_cparams(("parallel",)),
    )(w, m, v, *flat)


def _slab(ref, axis, idx, n):
    return ref.at[tuple([slice(None)] * axis + [pl.ds(idx * n, n)])]


def all_gather(x, axis, *, name):
    n = x.shape[axis]
    out_shape = x.shape[:axis] + (N_DEV * n,) + x.shape[axis + 1:]

    def body(x_ref, out_ref, send_sems, recv_sems, local_sem):
        xx, yy, cc = lax.axis_index("x"), lax.axis_index("y"), lax.axis_index("c")
        me, sibling = (xx, yy, cc), (xx, yy, 1 - cc)
        chips = [(1 - xx, yy), (xx, 1 - yy), (1 - xx, 1 - yy)]

        def slab(px, py, pc):
            return _slab(out_ref, axis, 4 * px + 2 * py + pc, n)

        def copy(k, block, to, src=None):
            return pltpu.make_async_remote_copy(
                src_ref=slab(*block) if src is None else src, dst_ref=slab(*block),
                send_sem=send_sems.at[k], recv_sem=recv_sems.at[k], device_id=to, device_id_type=MESH)

        mine = pltpu.make_async_copy(x_ref, slab(*me), local_sem)
        mine.start()
        first = [copy(0, me, sibling, src=x_ref)]
        first += [copy(1 + j, me, (*chip, cc), src=x_ref) for j, chip in enumerate(chips)]
        for cp in first:
            cp.start()
        passed = [copy(4 + j, (*chip, cc), sibling) for j, chip in enumerate(chips)]
        for j, chip in enumerate(chips):
            copy(1 + j, (*chip, cc), me).wait_recv()
            passed[j].start()
        copy(0, sibling, me).wait_recv()
        for j, chip in enumerate(chips):
            copy(4 + j, (*chip, 1 - cc), me).wait_recv()
        for cp in first + passed:
            cp.wait_send()
        mine.wait()

    return pl.pallas_call(
        body, name=name, out_shape=jax.ShapeDtypeStruct(out_shape, x.dtype), in_specs=[ANY], out_specs=ANY,
        scratch_shapes=[pltpu.SemaphoreType.DMA((7,)), pltpu.SemaphoreType.DMA((7,)), pltpu.SemaphoreType.DMA],
    )(x)


N_CHIP = 4


HBM = pl.BlockSpec(memory_space=pltpu.HBM)
SEM = pl.BlockSpec(memory_space=pltpu.SEMAPHORE)
EFFECT = pltpu.SideEffectType.DATAFLOW_SIDE_EFFECTING
TOKEN = jax.ShapeDtypeStruct((8, LANE), f32)


def _in_hbm(a):
    return pltpu.with_memory_space_constraint(a, pltpu.HBM)


def pair_sums(g, axis, *, name):
    n = g.shape[axis] // N_DEV
    slab_shape = g.shape[:axis] + (n,) + g.shape[axis + 1:]
    cols = slab_shape[-1]
    rows = math.prod(slab_shape[:-1])
    col_slabs = axis == g.ndim - 1
    assert col_slabs or (axis == 0 and g.ndim == 2)

    def swap_body(g_ref, got_ref, send_sems, recv_sems):
        xx, yy, cc = lax.axis_index("x"), lax.axis_index("y"), lax.axis_index("c")
        copies = [pltpu.make_async_remote_copy(
            src_ref=_slab(g_ref, axis, 2 * q + 1 - cc, n), dst_ref=got_ref.at[q],
            send_sem=send_sems.at[q], recv_sem=recv_sems.at[q], device_id=(xx, yy, 1 - cc), device_id_type=MESH)
            for q in range(N_CHIP)]
        for cp in copies:
            cp.start()
        for cp in copies:
            cp.wait()

    got = pl.pallas_call(
        swap_body, name=name + "_swap", out_shape=jax.ShapeDtypeStruct((N_CHIP,) + slab_shape, g.dtype),
        in_specs=[ANY], out_specs=ANY, scratch_shapes=[pltpu.SemaphoreType.DMA((N_CHIP,))] * 2,
    )(g)

    tr = min(256, rows)

    def add_body(a0_ref, a1_ref, b_ref, pair_ref, own_ref):
        xx, yy, cc = lax.axis_index("x"), lax.axis_index("y"), lax.axis_index("c")
        mine = jnp.where(cc == 0, a0_ref[...], a1_ref[...])
        s = (mine.astype(f32) + b_ref[0].astype(f32)).astype(bf16)
        pair_ref[0] = s

        @pl.when(pl.program_id(1) == 2 * xx + yy)
        def _():
            own_ref[0] = s

    if col_slabs:
        a_spec = lambda c: pl.BlockSpec((tr, cols), lambda i, q: (i, 2 * q + c))
    else:
        a_spec = lambda c: pl.BlockSpec((tr, cols), lambda i, q: ((2 * q + c) * (n // tr) + i, 0))
    by_chip = pl.BlockSpec((1, tr, cols), lambda i, q: (q, i, 0))
    g2 = g.reshape(-1, g.shape[-1])
    pair, own = pl.pallas_call(
        add_body, name=name + "_add", grid=(rows // tr, N_CHIP), in_specs=[a_spec(0), a_spec(1), by_chip],
        out_specs=[by_chip, pl.BlockSpec((1, tr, cols), lambda i, q: (0, i, 0))],
        out_shape=[jax.ShapeDtypeStruct((N_CHIP, rows, cols), bf16), jax.ShapeDtypeStruct((1, rows, cols), bf16)],
        compiler_params=_cparams(("parallel", "arbitrary")),
    )(g2, g2, got.reshape(N_CHIP, rows, cols))
    return own, pair


def _send_copies(p_refs, land_refs, send_sems, recv_sems):
    xx, yy, cc = lax.axis_index("x"), lax.axis_index("y"), lax.axis_index("c")
    copies = []
    for t, (p, land) in enumerate(zip(p_refs, land_refs)):
        for k in range(1, N_CHIP):
            px = 1 - xx if k & 2 else xx
            py = 1 - yy if k & 1 else yy
            s = (N_CHIP - 1) * t + k - 1
            copies.append(pltpu.make_async_remote_copy(
                src_ref=p.at[2 * px + py], dst_ref=land.at[k - 1], send_sem=send_sems.at[s], recv_sem=recv_sems.at[s],
                device_id=(px, py, cc), device_id_type=MESH))
    return copies


def send_pairs_start(pairs, *, name):
    nt = len(pairs)
    lands = [lax.empty((N_CHIP - 1,) + p.shape[1:], p.dtype) for p in pairs]

    def body(*refs):
        p_refs, land_refs = refs[:nt], refs[nt:2 * nt]
        send_sems, recv_sems = refs[2 * nt], refs[2 * nt + 1]
        token = refs[-1]
        for cp in _send_copies(p_refs, land_refs, send_sems, recv_sems):
            cp.start()
        token[...] = jnp.zeros_like(token)

    nsem = (N_CHIP - 1) * nt
    outs = pl.pallas_call(
        body, name=name,
        out_shape=(pltpu.SemaphoreType.DMA((nsem,)), pltpu.SemaphoreType.DMA((nsem,)))
        + tuple(pltpu.HBM(a.shape, a.dtype) for a in list(pairs) + lands) + (TOKEN,),
        in_specs=[HBM] * (2 * nt), out_specs=(SEM, SEM) + (HBM,) * (2 * nt) + (VM,),
        input_output_aliases={i: 2 + i for i in range(2 * nt)},
        compiler_params=pltpu.CompilerParams(has_side_effects=EFFECT),
    )(*[_in_hbm(a) for a in list(pairs) + lands])
    return outs[:-1], outs[-1]


def send_pairs_wait(handles, after, *, name):
    send_sems, recv_sems = handles[0], handles[1]
    bufs = handles[2:]
    nt = len(bufs) // 2

    def body(*refs):
        p_refs, land_refs = refs[:nt], refs[nt:2 * nt]
        send_sems, recv_sems = refs[2 * nt], refs[2 * nt + 1]
        for cp in _send_copies(p_refs, land_refs, send_sems, recv_sems):
            cp.wait_send()
            cp.wait_recv()

    outs = pl.pallas_call(
        body, name=name, out_shape=tuple(pltpu.HBM(a.shape, a.dtype) for a in bufs),
        in_specs=[HBM] * (2 * nt) + [SEM, SEM, ANY], out_specs=(HBM,) * (2 * nt),
        input_output_aliases={i: i for i in range(2 * nt)},
        compiler_params=pltpu.CompilerParams(has_side_effects=EFFECT),
    )(*bufs, send_sems, recv_sems, after)
    return outs[nt:]


def _gather_copies(x_refs, land_refs, axes, send_sems, recv_sems):
    xx, yy, cc = lax.axis_index("x"), lax.axis_index("y"), lax.axis_index("c")
    me = 4 * xx + 2 * yy + cc
    copies = []
    for t, (x_ref, land, axis) in enumerate(zip(x_refs, land_refs, axes)):
        n = x_ref.shape[axis]
        for k in range(1, N_DEV):
            px = 1 - xx if k & 4 else xx
            py = 1 - yy if k & 2 else yy
            pc = 1 - cc if k & 1 else cc
            s = (N_DEV - 1) * t + k - 1
            copies.append(pltpu.make_async_remote_copy(
                src_ref=x_ref, dst_ref=_slab(land, axis, me, n), send_sem=send_sems.at[s], recv_sem=recv_sems.at[s],
                device_id=(px, py, pc), device_id_type=MESH))
    return copies


def gather_start(xs, axes, after, *, name):
    nt = len(xs)
    me = 4 * lax.axis_index("x") + 2 * lax.axis_index("y") + lax.axis_index("c")
    lands = []
    for x, axis in zip(xs, axes):
        full = lax.empty(x.shape[:axis] + (N_DEV * x.shape[axis],) + x.shape[axis + 1:], x.dtype)
        lands.append(lax.dynamic_update_slice_in_dim(full, x, me * x.shape[axis], axis))

    def body(*refs):
        x_refs, land_refs = refs[:nt], refs[nt:2 * nt]
        send_sems, recv_sems = refs[2 * nt + 1], refs[2 * nt + 2]
        token = refs[-1]
        for cp in _gather_copies(x_refs, land_refs, axes, send_sems, recv_sems):
            cp.start()
        token[...] = jnp.zeros_like(token)

    nsem = (N_DEV - 1) * nt
    outs = pl.pallas_call(
        body, name=name,
        out_shape=(pltpu.SemaphoreType.DMA((nsem,)), pltpu.SemaphoreType.DMA((nsem,)))
        + tuple(pltpu.HBM(a.shape, a.dtype) for a in list(xs) + lands) + (TOKEN,),
        in_specs=[HBM] * (2 * nt) + [ANY], out_specs=(SEM, SEM) + (HBM,) * (2 * nt) + (VM,),
        input_output_aliases={i: 2 + i for i in range(2 * nt)},
        compiler_params=pltpu.CompilerParams(has_side_effects=EFFECT),
    )(*[_in_hbm(a) for a in list(xs) + lands], after)
    return outs[:-1], outs[-1]


def gather_wait(handles, axes, after, *, name):
    send_sems, recv_sems = handles[0], handles[1]
    bufs = handles[2:]
    nt = len(bufs) // 2

    def body(*refs):
        x_refs, land_refs = refs[:nt], refs[nt:2 * nt]
        send_sems, recv_sems = refs[2 * nt], refs[2 * nt + 1]
        for cp in _gather_copies(x_refs, land_refs, axes, send_sems, recv_sems):
            cp.wait_send()
            cp.wait_recv()

    outs = pl.pallas_call(
        body, name=name, out_shape=tuple(pltpu.HBM(a.shape, a.dtype) for a in bufs),
        in_specs=[HBM] * (2 * nt) + [SEM, SEM, ANY], out_specs=(HBM,) * (2 * nt),
        input_output_aliases={i: i for i in range(2 * nt)},
        compiler_params=pltpu.CompilerParams(has_side_effects=EFFECT),
    )(*bufs, send_sems, recv_sems, after)
    return outs[nt:]


def _blockdiag(b, nb):
    j, _, r, c = b.shape
    eye = jnp.eye(nb, dtype=bool)[None, :, None, :, None]
    return jnp.where(eye, b[:, :, :, None, :], jnp.zeros((), b.dtype)).reshape(j, nb * r, nb * c)


def _diagblocks(d, nb):
    j, rr, cc = d.shape
    return jnp.einsum('jarac->jarc', d.reshape(j, nb, rr // nb, nb, cc // nb))


def _s5_b_dense(bbar):
    return _blockdiag(bbar.transpose(0, 2, 1).reshape(NCH, 8, S5_GROUP, S5_STATE), 8)


def _s5_b_undense(d):
    return _diagblocks(d, 8).reshape(S5_GROUPS, S5_GROUP, S5_STATE).transpose(0, 2, 1)


def _s5_c_dense(c):
    return _blockdiag(c.transpose(0, 2, 1).reshape(NCH, 8, S5_STATE, S5_GROUP), 8)


def _s5_c_undense(d):
    return _diagblocks(d, 8).reshape(S5_GROUPS, S5_STATE, S5_GROUP).transpose(0, 2, 1)


def _rg_dense(w):
    return _blockdiag(w.reshape(NCH, 2, RG_BLOCK, RG_BLOCK), 2)


def _rg_undense(d):
    return _diagblocks(d, 2).reshape(RG_BLOCKS, RG_BLOCK, RG_BLOCK)


def _chunks(v):
    return v.reshape(NCH, 1, LANE)


def _tri(tm):
    r = jnp.arange(tm)
    m = (r[:, None] >= r[None, :]) & (r[:, None] // HG_SUB == r[None, :] // HG_SUB)
    m = m.astype(f32)
    return m[None], m.T[None]


SMALL = ['norm_w', 's5_lambda_re', 's5_lambda_im', 's5_log_step', 's5_b_re', 's5_b_im', 's5_c_re', 's5_c_im',
         's5_d', 's5_b_glu', 'rg_conv_w', 'rg_conv_b', 'rg_w_a', 'rg_b_a', 'rg_w_x', 'rg_b_x', 'rg_lambda',
         'hg_lower_bounds', 'hg_norm_w', 'final_norm_w']
WEIGHTS = ['norm_w', 'w_in', 's5_lambda_re', 's5_lambda_im', 's5_log_step', 's5_b_re', 's5_b_im', 's5_c_re',
           's5_c_im', 's5_d', 's5_w_glu', 's5_b_glu', 'rg_conv_w', 'rg_conv_b', 'rg_w_a', 'rg_b_a', 'rg_w_x',
           'rg_b_x', 'rg_lambda', 'hg_lower_bounds', 'hg_norm_w', 'w_branch', 'w_out', 'final_norm_w']
PACK_ROWS = 512


def _pack(arrs):
    flat = jnp.concatenate([a.reshape(-1) for a in arrs])
    pad = (-flat.shape[0]) % (PACK_ROWS * LANE)
    return jnp.pad(flat, (0, pad)).reshape(1, -1, LANE)


def _unpack(buf, shapes):
    flat = buf.reshape(-1)
    out, off = [], 0
    for s in shapes:
        n = math.prod(s)
        out.append(flat[off:off + n].reshape(s))
        off += n
    return out


def _step(x, tgt, w, m, v):
    t = x.shape[0]
    tri, tri_t = _tri(min(LANE, t))
    me = 4 * lax.axis_index("x") + 2 * lax.axis_index("y") + lax.axis_index("c")

    big = ('w_in', 's5_w_glu', 'w_branch', 'w_out')
    big_axis = (1, 0, 2, 0)
    shards = lambda l: [w[k][l].astype(bf16) for k in big]
    win, wglu, wbr, wout = ([None] * DEPTH for _ in range(4))
    win[0] = all_gather(shards(0)[0], big_axis[0], name="ag_w_in")
    rest0, rest0_token = gather_start(shards(0)[1:], big_axis[1:], win[0], name="ag_start_0")
    conv_w = all_gather(w['rg_conv_w'].reshape(DEPTH * RG_CONV, LANE), 1, name="ag_conv_w")
    conv_w = conv_w.reshape(DEPTH, RG_CONV, W_MIX)

    lb_rows = [w['hg_lower_bounds'][l][None] for l in range(DEPTH)]
    lbs = whole(lb_prep_fn, lb_rows, [(1, W_MIX)] * DEPTH, name="lb_prep")

    saved = []
    for l in range(DEPTH):
        s = {}
        nw = w['norm_w'][l].reshape(1, 1, D_MODEL)
        (h,) = rowwise(ln_fn, [(x, 0, D_MODEL)], [nw], [], [(D_MODEL, bf16)], name="ln_fwd")
        token = None
        if l + 1 < DEPTH:
            handles, token = gather_start(shards(l + 1), big_axis, rest0_token if l == 0 else x, name=f"ag_start_{l + 1}")
        z = mm(h, win[l], after=token, name="mm_in")
        if l == 0:
            wglu[0], wbr[0], wout[0] = gather_wait(rest0, big_axis[1:], z, name="ag_wait_0")
        s5p = [w['s5_lambda_re'][l][..., None], w['s5_lambda_im'][l][..., None], w['s5_log_step'][l][:, None, None],
               w['s5_b_re'][l], w['s5_b_im'][l]]
        gp = (S5_GROUPS, S5_STATE)
        abar_re, abar_im, bbar_re, bbar_im = whole(
            s5_prep_fn, s5p, [gp + (1,), gp + (1,), gp + (S5_GROUP,), gp + (S5_GROUP,)], name="s5_prep")
        a_re, a_im = abar_re.reshape(NCH, 1, S5_SC), abar_im.reshape(NCH, 1, S5_SC)
        bd_re, bd_im = _s5_b_dense(bbar_re), _s5_b_dense(bbar_im)
        cd_re, cd_im = _s5_c_dense(w['s5_c_re'][l]), _s5_c_dense(w['s5_c_im'][l])
        yssm, xre, xim = s5_scan_fwd(z, bd_re, bd_im, cd_re, cd_im, a_re, a_im)
        s5post_p = [w['s5_d'][l].reshape(1, 1, W_MIX), wglu[l].astype(f32)[None], w['s5_b_glu'][l].reshape(1, 1, W_MIX)]
        s5post_rows = [(yssm, 0, W_MIX), (z, C_UA, W_MIX), (z, C_GA, W_MIX)]
        (ya,) = rowwise(s5_post_fn, s5post_rows, s5post_p, [], [(W_MIX, bf16)], name="s5_post_fwd")
        cw, cb = conv_w[l].reshape(RG_CONV, NCH, LANE).transpose(1, 0, 2), _chunks(w['rg_conv_b'][l])
        xc = rg_conv_fwd(z, cw, cb)
        rg_p = [_rg_dense(w['rg_w_a'][l]), _chunks(w['rg_b_a'][l]), _rg_dense(w['rg_w_x'][l]),
                _chunks(w['rg_b_x'][l]), _chunks(w['rg_lambda'][l])]
        ra, rb = rowwise(rg_gate_fn, [(xc, 0, W_MIX)], rg_p, [], [(W_MIX, f32)] * 2, name="rg_gate_fwd",
                         ncol=NCH, tm=TM_CHUNK, rowid=True)
        hb = rg_scan_fwd(ra, rb)
        hg_rows = [(z, C_Q, W_MIX), (z, C_F, W_MIX)]
        hg_p = [_chunks(lbs[l].reshape(W_MIX))]
        qs, kk, gcum = rowwise(hg_pre_fn, hg_rows, hg_p, [tri, tri_t], [(W_MIX, f32)] * 3, name="hg_pre_fwd", ncol=NCH, tm=TM_CHUNK)
        oc, sall = hg_chunk_fwd(qs, kk, gcum, z)
        bp_rows = [(hb, 0, W_MIX), (z, C_GB, W_MIX), (oc, 0, W_MIX), (z, C_GC, W_MIX)]
        bp_p = [_chunks(w['hg_norm_w'][l])]
        yb, yc = rowwise(branch_prep_fn, bp_rows, bp_p, [], [(W_MIX, bf16)] * 2, name="branch_prep_fwd", ncol=NCH, tm=TM_CHUNK)
        ys = [ya, yb, yc]
        br = [mm(ys[n], wbr[l][n], name="mm_branch") for n in range(N_BRANCH)]
        mg_rows = [(br[n], 0, D_MODEL) for n in range(N_BRANCH)] + [(z, C_GATE + n * D_MODEL, D_MODEL) for n in range(N_BRANCH)]
        (merged,) = rowwise(merge_fn, mg_rows, [], [], [(D_MODEL, bf16)], name="merge_fwd", ncol=2)
        x_new = mm(merged, wout[l], add=x, name="mm_out")
        s.update(x=x, h=h, z=z, s5p=s5p, s5=(bd_re, bd_im, cd_re, cd_im, a_re, a_im), xre=xre, xim=xim,
                 s5post_rows=s5post_rows, s5post_p=s5post_p, cw=cw, xc=xc, rg_p=rg_p, ra=ra, hb=hb,
                 hg_rows=hg_rows, hg_p=hg_p, qs=qs, kk=kk, gcum=gcum, sall=sall, bp_rows=bp_rows, bp_p=bp_p,
                 ys=ys, mg_rows=mg_rows, merged=merged, nw=nw)
        saved.append(s)
        x = x_new
        if l + 1 < DEPTH:
            win[l + 1], wglu[l + 1], wbr[l + 1], wout[l + 1] = gather_wait(handles, big_axis, x, name=f"ag_wait_{l + 1}")

    fnw = w['final_norm_w'].reshape(1, 1, D_MODEL)
    ones = jnp.ones((t, 1), f32)
    dx, d_fnw, loss_sum = rowwise_vjp(loss_fn, [(x, 0, D_MODEL), (tgt, 0, D_MODEL)], [fnw], [], [(ones, 0, 1)],
                                      [(0, f32)], name="loss_head", sum_primal=0)
    loss = lax.psum(loss_sum.reshape(()), ("x", "y", "c"))

    small_g = {k: [None] * DEPTH for k in SMALL if k != 'final_norm_w'}
    own_sums, in_flight = [None] * DEPTH, [None] * DEPTH
    d_lbs = [None] * DEPTH
    token = None
    for l in reversed(range(DEPTH)):
        s = saved[l]
        z = s['z']
        dxb = dx.astype(bf16)
        d_merged = mm(dxb, wout[l], bt=True, after=token, name="mm_out_dx", out_dtype=bf16)
        d_wout = mm(s['merged'].T, dxb, name="mm_out_dw", out_dtype=bf16)
        mg = rowwise_vjp(merge_fn, s['mg_rows'], [], [], [(d_merged, 0, D_MODEL)],
                         [(n, bf16) for n in range(2 * N_BRANCH)], name="merge_bwd", ncol=2)
        d_br, d_gl = mg[:N_BRANCH], mg[N_BRANCH:]
        d_ys = [mm(d_br[n], wbr[l][n], bt=True, name="mm_branch_dx") for n in range(N_BRANCH)]
        d_wbr = jnp.stack([mm(s['ys'][n].T, d_br[n], name="mm_branch_dw", out_dtype=bf16) for n in range(N_BRANCH)])
        d_hb, d_gb, d_oc, d_gc, d_hnw = rowwise_vjp(
            branch_prep_fn, s['bp_rows'], s['bp_p'], [], [(d_ys[1], 0, W_MIX), (d_ys[2], 0, W_MIX)],
            [(0, f32), (1, bf16), (2, f32), (3, bf16)], name="branch_prep_bwd", ncol=NCH, tm=TM_CHUNK)
        small_g['hg_norm_w'][l] = d_hnw.reshape(W_MIX)
        d_qs, d_kk, d_gcum, d_i = hg_chunk_bwd(d_oc, s['qs'], s['kk'], s['gcum'], z, s['sall'])
        d_q, d_f, d_lb = rowwise_vjp(
            hg_pre_fn, s['hg_rows'], s['hg_p'], [tri, tri_t], [(d_qs, 0, W_MIX), (d_kk, 0, W_MIX), (d_gcum, 0, W_MIX)],
            [(0, bf16), (1, bf16)], name="hg_pre_bwd", ncol=NCH, tm=TM_CHUNK)
        d_lbs[l] = d_lb.reshape(1, W_MIX)
        d_ra, d_rb = rg_scan_bwd(d_hb, s['ra'], s['hb'])
        rg = rowwise_vjp(rg_gate_fn, [(s['xc'], 0, W_MIX)], s['rg_p'], [], [(d_ra, 0, W_MIX), (d_rb, 0, W_MIX)],
                         [(0, f32)], name="rg_gate_bwd", ncol=NCH, tm=TM_CHUNK, rowid=True)
        d_xc, d_wa, d_ba, d_wx, d_bx, d_lam = rg
        d_xb, d_cw, d_cb = rg_conv_bwd(d_xc, z, s['cw'])
        small_g['rg_w_a'][l], small_g['rg_w_x'][l] = _rg_undense(d_wa), _rg_undense(d_wx)
        small_g['rg_b_a'][l], small_g['rg_b_x'][l] = d_ba.reshape(W_MIX), d_bx.reshape(W_MIX)
        small_g['rg_lambda'][l] = d_lam.reshape(W_MIX)
        small_g['rg_conv_w'][l] = d_cw.transpose(1, 0, 2).reshape(RG_CONV, W_MIX)
        small_g['rg_conv_b'][l] = d_cb.reshape(W_MIX)
        d_yssm, d_u1, d_ga, d_d, d_wglu, d_bglu = rowwise_vjp(
            s5_post_fn, s['s5post_rows'], s['s5post_p'], [], [(d_ys[0], 0, W_MIX)],
            [(0, bf16), (1, bf16), (2, bf16)], name="s5_post_bwd")
        small_g['s5_d'][l], small_g['s5_b_glu'][l] = d_d.reshape(W_MIX), d_bglu.reshape(W_MIX)
        d_ua, d_bdre, d_bdim, d_cdre, d_cdim, d_are, d_aim = s5_scan_bwd(d_yssm, d_u1, z, s['xre'], s['xim'], *s['s5'])
        small_g['s5_c_re'][l], small_g['s5_c_im'][l] = _s5_c_undense(d_cdre), _s5_c_undense(d_cdim)
        gp = (S5_GROUPS, S5_STATE, 1)
        s5g = whole_vjp(s5_prep_fn, s['s5p'],
                        [d_are.reshape(gp), d_aim.reshape(gp), _s5_b_undense(d_bdre), _s5_b_undense(d_bdim)],
                        name="s5_prep_bwd")
        small_g['s5_lambda_re'][l] = s5g[0].reshape(S5_GROUPS, S5_STATE)
        small_g['s5_lambda_im'][l] = s5g[1].reshape(S5_GROUPS, S5_STATE)
        small_g['s5_log_step'][l] = s5g[2].reshape(S5_GROUPS)
        small_g['s5_b_re'][l], small_g['s5_b_im'][l] = s5g[3], s5g[4]
        dz = jnp.concatenate([d_ua, d_ga, d_xb, d_gb, d_q, d_f, d_i, d_gc] + list(d_gl), axis=1)
        d_win = mm(s['h'].T, dz, name="mm_in_dw", out_dtype=bf16)
        sums = [pair_sums(g, ax, name="rs_" + k) for g, ax, k in zip((d_win, d_wglu[0], d_wbr, d_wout), big_axis, big)]
        own_sums[l] = [own for own, _ in sums]
        in_flight[l], token = send_pairs_start([pair for _, pair in sums], name=f"rs_start_{l}")
        d_h = mm(dz, win[l], bt=True, after=token, name="mm_in_dx", tk=2048)
        dx, d_nw = rowwise_vjp(ln_res_fn, [(s['x'], 0, D_MODEL)], [s['nw']], [], [(d_h, 0, D_MODEL), (dx, 0, D_MODEL)],
                               [(0, f32)], name="ln_bwd")
        small_g['norm_w'][l] = d_nw.reshape(D_MODEL)
    d_lb_raw = whole_vjp(lb_prep_fn, lb_rows, d_lbs, name="lb_prep_bwd")
    small_g['hg_lower_bounds'] = [r.reshape(W_MIX) for r in d_lb_raw]

    g_small = [jnp.stack(small_g[k]) for k in SMALL if k != 'final_norm_w'] + [d_fnw.reshape(D_MODEL)]
    shapes = [g.shape for g in g_small]
    g_all = all_gather(_pack(g_small)[0], 0, name="ag_small_grads")
    g_all = g_all.reshape(N_DEV, -1, LANE)

    def local(d, k):
        return jnp.zeros(shapes[SMALL.index(k)], f32) if k == 'rg_conv_w' else d[k]
    packed = [_pack([local(d, k) for k in SMALL]) for d in (w, m, v)]
    outs = adamw(*packed, [[g_all]], name="adamw_small", tr=512)
    res = {}
    for kind, buf in zip(('grad', 'delta', 'new_m', 'new_v'), outs):
        for k, a in zip(SMALL, _unpack(buf, shapes)):
            res[kind + '_' + k] = a
    g_cw = lax.dynamic_slice_in_dim(res['grad_rg_conv_w'], me * LANE, LANE, axis=2)
    cw3 = lambda a: a.reshape(1, DEPTH * RG_CONV, LANE)
    outs = adamw(cw3(w['rg_conv_w']), cw3(m['rg_conv_w']), cw3(v['rg_conv_w']), [[cw3(g_cw)]], name="adamw_conv_w", tr=16)
    for kind, buf in zip(('grad', 'delta', 'new_m', 'new_v'), outs):
        res[kind + '_rg_conv_w'] = buf.reshape(DEPTH, RG_CONV, LANE)

    after = outs[0]
    arrived = [send_pairs_wait(in_flight[l], after, name=f"rs_wait_{l}") for l in range(DEPTH)]
    for i, (k, tr) in enumerate((('w_in', 32), ('s5_w_glu', 32), ('w_branch', 128), ('w_out', 32))):
        shp = w[k].shape
        r3 = lambda a: a.reshape(DEPTH, -1, shp[-1])
        slots = [[own_sums[l][i], arrived[l][i]] for l in range(DEPTH)]
        outs = adamw(r3(w[k]), r3(m[k]), r3(v[k]), slots, name="adamw_" + k, tr=tr)
        for kind, buf in zip(('grad', 'delta', 'new_m', 'new_v'), outs):
            res[kind + '_' + k] = buf.reshape(shp)

    return (loss, dx[None]) + tuple(res[kind + '_' + k] for kind in ('grad', 'delta', 'new_m', 'new_v') for k in WEIGHTS)


def kernel(x, norm_w, w_in, s5_lambda_re, s5_lambda_im, s5_log_step, s5_b_re, s5_b_im, s5_c_re, s5_c_im, s5_d, s5_w_glu, s5_b_glu, rg_conv_w, rg_conv_b, rg_w_a, rg_b_a, rg_w_x, rg_b_x, rg_lambda, hg_lower_bounds, hg_norm_w, w_branch, w_out, final_norm_w, loss_target, m_norm_w, m_w_in, m_s5_lambda_re, m_s5_lambda_im, m_s5_log_step, m_s5_b_re, m_s5_b_im, m_s5_c_re, m_s5_c_im, m_s5_d, m_s5_w_glu, m_s5_b_glu, m_rg_conv_w, m_rg_conv_b, m_rg_w_a, m_rg_b_a, m_rg_w_x, m_rg_b_x, m_rg_lambda, m_hg_lower_bounds, m_hg_norm_w, m_w_branch, m_w_out, m_final_norm_w, v_norm_w, v_w_in, v_s5_lambda_re, v_s5_lambda_im, v_s5_log_step, v_s5_b_re, v_s5_b_im, v_s5_c_re, v_s5_c_im, v_s5_d, v_s5_w_glu, v_s5_b_glu, v_rg_conv_w, v_rg_conv_b, v_rg_w_a, v_rg_b_a, v_rg_w_x, v_rg_b_x, v_rg_lambda, v_hg_lower_bounds, v_hg_norm_w, v_w_branch, v_w_out, v_final_norm_w):
    w = dict(zip(WEIGHTS, (norm_w, w_in, s5_lambda_re, s5_lambda_im, s5_log_step, s5_b_re, s5_b_im, s5_c_re, s5_c_im, s5_d, s5_w_glu, s5_b_glu, rg_conv_w, rg_conv_b, rg_w_a, rg_b_a, rg_w_x, rg_b_x, rg_lambda, hg_lower_bounds, hg_norm_w, w_branch, w_out, final_norm_w)))
    m = dict(zip(WEIGHTS, (m_norm_w, m_w_in, m_s5_lambda_re, m_s5_lambda_im, m_s5_log_step, m_s5_b_re, m_s5_b_im, m_s5_c_re, m_s5_c_im, m_s5_d, m_s5_w_glu, m_s5_b_glu, m_rg_conv_w, m_rg_conv_b, m_rg_w_a, m_rg_b_a, m_rg_w_x, m_rg_b_x, m_rg_lambda, m_hg_lower_bounds, m_hg_norm_w, m_w_branch, m_w_out, m_final_norm_w)))
    v = dict(zip(WEIGHTS, (v_norm_w, v_w_in, v_s5_lambda_re, v_s5_lambda_im, v_s5_log_step, v_s5_b_re, v_s5_b_im, v_s5_c_re, v_s5_c_im, v_s5_d, v_s5_w_glu, v_s5_b_glu, v_rg_conv_w, v_rg_conv_b, v_rg_w_a, v_rg_b_a, v_rg_w_x, v_rg_b_x, v_rg_lambda, v_hg_lower_bounds, v_hg_norm_w, v_w_branch, v_w_out, v_final_norm_w)))
    return _step(x[0], loss_target[0], w, m, v)
```

```python
import functools
import math

import jax
import jax.numpy as jnp
from jax import lax
from jax.experimental import pallas as pl
from jax.experimental.pallas import tpu as pltpu

f32 = jnp.float32
bf16 = jnp.bfloat16

D_MODEL = 2048
W_MIX = 1024
DEPTH = 4
N_BRANCH = 3
N_IN = 8 * W_MIX + N_BRANCH * D_MODEL
S5_GROUPS, S5_STATE, S5_GROUP = 64, 64, 16
RG_BLOCKS, RG_BLOCK, RG_CONV, RG_C = 16, 64, 4, 8.0
HG_HEADS, HG_DK = 8, 128
HG_SUB = 16
EPS = 1e-6
ADAM_LR, ADAM_B1, ADAM_B2, ADAM_EPS, ADAM_WD, ADAM_STEP = 0.001, 0.9, 0.999, 1e-08, 0.01, 10

N_DEV = 8
LANE = 128
NCH = W_MIX // LANE
TM_CHUNK = 1024
VMEM_LIMIT = 56 * 1024 * 1024
MESH = pl.DeviceIdType.MESH
ANY = pl.BlockSpec(memory_space=pl.ANY)
HIGHEST = lax.Precision.HIGHEST

C_UA, C_GA, C_XB, C_GB, C_Q, C_F, C_I, C_GC, C_GATE = (W_MIX * k for k in range(9))


def _cparams(sem=None):
    return pltpu.CompilerParams(dimension_semantics=sem, vmem_limit_bytes=VMEM_LIMIT)


@jax.custom_vjp
def bdot(a, w):
    return jnp.dot(a.astype(bf16), w.astype(bf16), preferred_element_type=f32)


def _bdot_fwd(a, w):
    return bdot(a, w), (a, w)


def _bdot_bwd(res, g):
    a, w = res
    gb = g.astype(bf16)
    da = lax.dot_general(gb, w.astype(bf16), (((1,), (1,)), ((), ())), preferred_element_type=f32)
    dw = lax.dot_general(a.astype(bf16), gb, (((0,), (0,)), ((), ())), preferred_element_type=f32)
    return da, dw


bdot.defvjp(_bdot_fwd, _bdot_bwd)


def _blockmm(c, a):
    n = c.shape[0]
    return jnp.concatenate([jnp.dot(c, a[i:i + n], preferred_element_type=f32, precision=HIGHEST)
                            for i in range(0, a.shape[0], n)], axis=0)


@jax.custom_vjp
def cdot(c, ct, a):
    return _blockmm(c, a)


def _cdot_fwd(c, ct, a):
    return cdot(c, ct, a), (c, ct)


def _cdot_bwd(res, g):
    c, ct = res
    return jnp.zeros_like(c), jnp.zeros_like(ct), _blockmm(ct, g)


cdot.defvjp(_cdot_fwd, _cdot_bwd)


def mm(a, b, *, name, out_dtype=f32, add=None, after=None, bt=False, tm=512, tn=1024, tk=4096):
    m, k = a.shape
    n = b.shape[0] if bt else b.shape[1]
    tm, tn, tk = min(tm, m), min(tn, n), min(tk, k)
    assert m % tm == 0 and n % tn == 0 and k % tk == 0
    nk = k // tk
    dims = (((1,), (1,)), ((), ())) if bt else (((1,), (0,)), ((), ()))

    def body(*refs):
        a_ref, b_ref = refs[:2]
        r_ref = refs[2] if add is not None else None
        o_ref = refs[-1] if nk == 1 else refs[-2]
        part = lax.dot_general(a_ref[...], b_ref[...], dims, preferred_element_type=f32)
        if nk == 1:
            if add is not None:
                part = part + r_ref[...]
            o_ref[...] = part.astype(out_dtype)
            return
        acc_ref = refs[-1]
        kk = pl.program_id(2)

        @pl.when(kk == 0)
        def _():
            acc_ref[...] = part

        @pl.when(kk > 0)
        def _():
            acc_ref[...] = acc_ref[...] + part

        @pl.when(kk == nk - 1)
        def _():
            acc = acc_ref[...]
            if add is not None:
                acc = acc + r_ref[...]
            o_ref[...] = acc.astype(out_dtype)

    b_spec = pl.BlockSpec((tn, tk), lambda i, j, q: (j, q)) if bt else pl.BlockSpec((tk, tn), lambda i, j, q: (q, j))
    in_specs = [pl.BlockSpec((tm, tk), lambda i, j, q: (i, q)), b_spec]
    args = [a, b]
    if add is not None:
        in_specs.append(pl.BlockSpec((tm, tn), lambda i, j, q: (i, j)))
        args.append(add)
    if after is not None:
        in_specs.append(pl.BlockSpec(after.shape, lambda i, j, q: (0, 0)))
        args.append(after)
    return pl.pallas_call(
        body, name=name, grid=(m // tm, n // tn, nk), in_specs=in_specs,
        out_specs=pl.BlockSpec((tm, tn), lambda i, j, q: (i, j)),
        out_shape=jax.ShapeDtypeStruct((m, n), out_dtype),
        scratch_shapes=[] if nk == 1 else [pltpu.VMEM((tm, tn), f32)],
        compiler_params=_cparams(("parallel", "parallel", "arbitrary")),
    )(*args)


def _row_spec(tm, wc, col_off):
    base = col_off // wc
    assert col_off % wc == 0
    return pl.BlockSpec((tm, wc), lambda j, i: (i, base + j))


def _slab_spec(arr):
    r, c = arr.shape[1:]
    if arr.shape[0] == 1:
        return pl.BlockSpec((1, r, c), lambda j, i: (0, 0, 0))
    return pl.BlockSpec((1, r, c), lambda j, i: (j, 0, 0))


def rowwise(fn, rows, params, consts, outs, *, name, tm=256, ncol=1, rowid=False):
    t = rows[0][0].shape[0]
    tm = min(tm, t)
    nr, npar, nc, no = len(rows), len(params), len(consts), len(outs)

    def body(*refs):
        r = [refs[k][...].astype(f32) for k in range(nr)]
        p = [refs[nr + k][0] for k in range(npar + nc)]
        extra = ()
        if rowid:
            extra = (pl.program_id(1) * tm + lax.broadcasted_iota(jnp.int32, (tm, 1), 0),)
        res = fn(*extra, *r, *p)
        for k in range(no):
            refs[nr + npar + nc + k][...] = res[k].astype(outs[k][1])

    in_specs = [_row_spec(tm, w // ncol, off) for (_, off, w) in rows]
    in_specs += [_slab_spec(a) for a in list(params) + list(consts)]
    out_specs = [pl.BlockSpec((tm, w // ncol), lambda j, i: (i, j)) for (w, _) in outs]
    out_shape = [jax.ShapeDtypeStruct((t, w), dt) for (w, dt) in outs]
    return pl.pallas_call(
        body, name=name, grid=(ncol, t // tm), in_specs=in_specs, out_specs=out_specs, out_shape=out_shape,
        compiler_params=_cparams(("parallel", "parallel")),
    )(*[r[0] for r in rows], *params, *consts)


def rowwise_vjp(fn, rows, params, consts, cts, d_rows, *, name, tm=256, ncol=1, rowid=False, sum_primal=None):
    t = rows[0][0].shape[0]
    tm = min(tm, t)
    nr, npar, nc, nct, ndr = len(rows), len(params), len(consts), len(cts), len(d_rows)

    def body(*refs):
        i = pl.program_id(1)
        r = [refs[k][...].astype(f32) for k in range(nr)]
        p = [refs[nr + k][0] for k in range(npar)]
        c = [refs[nr + npar + k][0] for k in range(nc)]
        g = [refs[nr + npar + nc + k][...].astype(f32) for k in range(nct)]
        orefs = refs[nr + npar + nc + nct:]
        extra = ()
        if rowid:
            extra = (i * tm + lax.broadcasted_iota(jnp.int32, (tm, 1), 0),)
        res, vjp = jax.vjp(lambda *v: fn(*extra, *v, *c), *r, *p)
        grads = vjp(tuple(g))
        for k, (idx, dt) in enumerate(d_rows):
            orefs[k][...] = grads[idx].astype(dt)
        acc = [grads[nr + k] for k in range(npar)]
        if sum_primal is not None:
            acc.append(jnp.sum(res[sum_primal], axis=0, keepdims=True))

        @pl.when(i == 0)
        def _():
            for k, a in enumerate(acc):
                orefs[ndr + k][0] = a

        @pl.when(i > 0)
        def _():
            for k, a in enumerate(acc):
                orefs[ndr + k][0] = orefs[ndr + k][0] + a

    in_specs = [_row_spec(tm, w // ncol, off) for (_, off, w) in rows]
    in_specs += [_slab_spec(a) for a in list(params) + list(consts)]
    in_specs += [_row_spec(tm, w // ncol, off) for (_, off, w) in cts]
    out_specs = [pl.BlockSpec((tm, rows[idx][2] // ncol), lambda j, i: (i, j)) for (idx, _) in d_rows]
    out_shape = [jax.ShapeDtypeStruct((t, rows[idx][2]), dt) for (idx, dt) in d_rows]
    for a in params:
        out_specs.append(pl.BlockSpec((1,) + a.shape[1:], lambda j, i: (j, 0, 0)))
        out_shape.append(jax.ShapeDtypeStruct(a.shape, f32))
    if sum_primal is not None:
        w = cts[sum_primal][2]
        out_specs.append(pl.BlockSpec((1, 1, w // ncol), lambda j, i: (j, 0, 0)))
        out_shape.append(jax.ShapeDtypeStruct((ncol, 1, w // ncol), f32))
    return pl.pallas_call(
        body, name=name, grid=(ncol, t // tm), in_specs=in_specs, out_specs=out_specs, out_shape=out_shape,
        compiler_params=_cparams(("parallel", "arbitrary")),
    )(*[r[0] for r in rows], *params, *consts, *[c[0] for c in cts])


VM = pl.BlockSpec(memory_space=pltpu.VMEM)


def whole(fn, ins, outs, *, name):
    def body(*refs):
        res = fn(*[r[...] for r in refs[:len(ins)]])
        for k, o in enumerate(refs[len(ins):]):
            o[...] = res[k]
    return pl.pallas_call(body, name=name, in_specs=[VM] * len(ins), out_specs=[VM] * len(outs),
                          out_shape=[jax.ShapeDtypeStruct(s, f32) for s in outs],
                          compiler_params=_cparams())(*ins)


def whole_vjp(fn, ins, cts, *, name):
    n = len(ins)

    def body(*refs):
        _, vjp = jax.vjp(fn, *[r[...] for r in refs[:n]])
        grads = vjp(tuple(r[...] for r in refs[n:n + len(cts)]))
        for k, o in enumerate(refs[n + len(cts):]):
            o[...] = grads[k]
    return pl.pallas_call(body, name=name, in_specs=[VM] * (n + len(cts)), out_specs=[VM] * n,
                          out_shape=[jax.ShapeDtypeStruct(a.shape, f32) for a in ins],
                          compiler_params=_cparams())(*ins, *cts)


def ln_fn(x, w):
    return (x * lax.rsqrt(jnp.mean(x * x, axis=-1, keepdims=True) + EPS) * w,)


def ln_res_fn(x, w):
    return ln_fn(x, w)[0], x


def loss_fn(x, tgt, w):
    y = ln_fn(x, w)[0]
    return (0.5 * jnp.mean(jnp.square(y - tgt), axis=-1, keepdims=True),)


def s5_prep_fn(lam_re, lam_im, log_step, b_re, b_im):
    step = jnp.exp(log_step)
    mag = jnp.exp(lam_re * step)
    ang = lam_im * step
    abar_re = mag * jnp.cos(ang)
    abar_im = mag * jnp.sin(ang)
    num_re = abar_re - 1.0
    num_im = abar_im
    den = lam_re * lam_re + lam_im * lam_im
    coef_re = (num_re * lam_re + num_im * lam_im) / den
    coef_im = (num_im * lam_re - num_re * lam_im) / den
    bbar_re = coef_re * b_re - coef_im * b_im
    bbar_im = coef_re * b_im + coef_im * b_re
    return abar_re, abar_im, bbar_re, bbar_im


def lb_prep_fn(r0, r1, r2, r3):
    m = jnp.maximum(jnp.maximum(r0, r1), jnp.maximum(r2, r3))
    e0, e1, e2, e3 = jnp.exp(r0 - m), jnp.exp(r1 - m), jnp.exp(r2 - m), jnp.exp(r3 - m)
    s = e0 + e1 + e2 + e3
    p0, p1, p2, p3 = e0 / s, e1 / s, e2 / s, e3 / s
    c1 = p0 + p1
    c2 = c1 + p2
    c3 = c2 + p3
    return p0 - p0, c1 - p0, c2 - p0, c3 - p0


def s5_post_fn(yssm, u, ga, d, wglu, bglu):
    y = jax.nn.gelu(yssm + d * u)
    y = y * jax.nn.sigmoid(bdot(y, wglu) + bglu)
    return (y * jax.nn.silu(ga),)


def rg_gate_fn(tglob, xc, wa, ba, wx, bx, lam):
    r = jax.nn.sigmoid(bdot(xc, wa) + ba)
    i = jax.nn.sigmoid(bdot(xc, wx) + bx)
    log_a = -RG_C * r * jax.nn.softplus(-lam)
    a = jnp.exp(log_a)
    mult = jnp.sqrt(-jnp.tanh(log_a) * (a * a + 1.0))
    mult = jnp.where(tglob == 0, 1.0, mult)
    return a, mult * (i * xc)


def hg_pre_fn(q, fl, lb, tri, tri_t):
    f = lb + (1.0 - lb) * jax.nn.sigmoid(fl)
    return jax.nn.silu(q), 1.0 - f, cdot(tri, tri_t, jnp.log(f))


def branch_prep_fn(hb, gb, oc, gc, nw):
    yb = hb * jax.nn.silu(gb)
    on = oc * lax.rsqrt(jnp.mean(oc * oc, axis=-1, keepdims=True) + EPS) * nw
    return yb, on * jax.nn.silu(gc)


def merge_fn(b0, b1, b2, g0, g1, g2):
    return (jax.nn.sigmoid(g0) * b0 + jax.nn.sigmoid(g1) * b1 + jax.nn.sigmoid(g2) * b2,)


S5_TB = 512
S5_SC = 512


SEG = 8


def _shift(v, k, pos, period, reverse, fill):
    if reverse:
        return jnp.where(pos < period - k, pltpu.roll(v, v.shape[0] - k, 0), fill)
    return jnp.where(pos >= k, pltpu.roll(v, k, 0), fill)


def _cmul(ar, ai, br, bi):
    return ar * br - ai * bi, ar * bi + ai * br


def _edge_rows(v, first):
    r0 = 0 if first else SEG - 1
    return jnp.concatenate([v[r:r + 1, :] for r in range(r0, v.shape[0], SEG)], axis=0)


def _spread(s):
    return jnp.concatenate([jnp.broadcast_to(s[g:g + 1, :], (SEG, s.shape[1])) for g in range(s.shape[0])], axis=0)


def lti_scan(xr, xi, ar, ai, cr, ci, reverse=False):
    n = xr.shape[0]
    g = n // SEG
    sub = lax.broadcasted_iota(jnp.int32, (n, 1), 0) & (SEG - 1)
    sub8 = lax.broadcasted_iota(jnp.int32, (SEG, 1), 0)
    grow = lax.broadcasted_iota(jnp.int32, (g, 1), 0)
    pr, pi_ = ar, ai
    wr, wi = jnp.broadcast_to(ar, (SEG, ar.shape[1])), jnp.broadcast_to(ai, (SEG, ai.shape[1]))
    k = 1
    while k < SEG:
        tr, ti = _cmul(pr, pi_, _shift(xr, k, sub, SEG, reverse, 0.0), _shift(xi, k, sub, SEG, reverse, 0.0))
        xr, xi = xr + tr, xi + ti
        pr, pi_ = _cmul(pr, pi_, pr, pi_)
        wr, wi = _cmul(wr, wi, _shift(wr, k, sub8, SEG, reverse, 1.0), _shift(wi, k, sub8, SEG, reverse, 0.0))
        k *= 2
    first, last = (g - 1, 0) if reverse else (0, g - 1)
    jr, ji = _cmul(pr, pi_, cr, ci)
    sr = _edge_rows(xr, reverse) + jnp.where(grow == first, jr, 0.0)
    si = _edge_rows(xi, reverse) + jnp.where(grow == first, ji, 0.0)
    k = 1
    while k < g:
        tr, ti = _cmul(pr, pi_, _shift(sr, k, grow, g, reverse, 0.0), _shift(si, k, grow, g, reverse, 0.0))
        sr, si = sr + tr, si + ti
        pr, pi_ = _cmul(pr, pi_, pr, pi_)
        k *= 2
    er, ei = _spread(_shift(sr, 1, grow, g, reverse, cr)), _spread(_shift(si, 1, grow, g, reverse, ci))
    tr, ti = _cmul(jnp.tile(wr, (g, 1)), jnp.tile(wi, (g, 1)), er, ei)
    return xr + tr, xi + ti, sr[last:last + 1, :], si[last:last + 1, :]


def tv_scan(aa, bb, carry, reverse=False):
    n = aa.shape[0]
    row = lax.broadcasted_iota(jnp.int32, (n, 1), 0)
    k = 1
    while k < n:
        bb = bb + aa * _shift(bb, k, row, n, reverse, 0.0)
        aa = aa * _shift(aa, k, row, n, reverse, 1.0)
        k *= 2
    h = bb + aa * carry
    last = 0 if reverse else n - 1
    return h, h[last:last + 1, :]


def s5_scan_fwd(z, bd_re, bd_im, cd_re, cd_im, a_re, a_im):
    t = z.shape[0]
    tb = min(S5_TB, t)

    def body(u_ref, bre, bim, cre, cim, are, aim, y_ref, xre_ref, xim_ref, car_re, car_im):
        @pl.when(pl.program_id(1) == 0)
        def _():
            car_re[...] = jnp.zeros_like(car_re)
            car_im[...] = jnp.zeros_like(car_im)

        u = u_ref[...].astype(bf16)
        xr, xi, car_re[...], car_im[...] = lti_scan(
            jnp.dot(u, bre[0], preferred_element_type=f32), jnp.dot(u, bim[0], preferred_element_type=f32),
            are[0], aim[0], car_re[...], car_im[...])
        xre_ref[...] = xr
        xim_ref[...] = xi
        y_ref[...] = (jnp.dot(xr.astype(bf16), cre[0], preferred_element_type=f32)
                      - jnp.dot(xi.astype(bf16), cim[0], preferred_element_type=f32))

    chunk = lambda r, c: pl.BlockSpec((1, r, c), lambda j, i: (j, 0, 0))
    return pl.pallas_call(
        body, name="s5_scan_fwd", grid=(NCH, t // tb),
        in_specs=[pl.BlockSpec((tb, LANE), lambda j, i: (i, C_UA // LANE + j)),
                  chunk(LANE, S5_SC), chunk(LANE, S5_SC), chunk(S5_SC, LANE), chunk(S5_SC, LANE),
                  chunk(1, S5_SC), chunk(1, S5_SC)],
        out_specs=[pl.BlockSpec((tb, LANE), lambda j, i: (i, j)),
                   pl.BlockSpec((tb, S5_SC), lambda j, i: (i, j)),
                   pl.BlockSpec((tb, S5_SC), lambda j, i: (i, j))],
        out_shape=[jax.ShapeDtypeStruct((t, W_MIX), f32),
                   jax.ShapeDtypeStruct((t, NCH * S5_SC), f32),
                   jax.ShapeDtypeStruct((t, NCH * S5_SC), f32)],
        scratch_shapes=[pltpu.VMEM((1, S5_SC), f32)] * 2,
        compiler_params=_cparams(("parallel", "arbitrary")),
    )(z, bd_re.astype(bf16), bd_im.astype(bf16), cd_re.astype(bf16), cd_im.astype(bf16), a_re, a_im)


def s5_scan_bwd(dy, du1, z, xre, xim, bd_re, bd_im, cd_re, cd_im, a_re, a_im):
    t = z.shape[0]
    tb = min(S5_TB, t)
    nt = t // tb

    def body(dy_ref, du1_ref, u_ref, xre_ref, xim_ref, hre_ref, him_ref, bre, bim, cre, cim, are, aim,
             du_ref, dbre, dbim, dcre, dcim, dare, daim, car_re, car_im):
        step = pl.program_id(1)
        tt = nt - 1 - step

        @pl.when(step == 0)
        def _():
            car_re[...] = jnp.zeros_like(car_re)
            car_im[...] = jnp.zeros_like(car_im)

        nt_dims = (((1,), (1,)), ((), ()))
        tn_dims = (((0,), (0,)), ((), ()))
        dyb = dy_ref[...].astype(bf16)
        row = lax.broadcasted_iota(jnp.int32, (tb, 1), 0)
        xr, xi = xre_ref[...], xim_ref[...]
        ar, ai = are[0], aim[0]
        lr, li, car_re[...], car_im[...] = lti_scan(
            lax.dot_general(dyb, cre[0], nt_dims, preferred_element_type=f32),
            -lax.dot_general(dyb, cim[0], nt_dims, preferred_element_type=f32),
            ar, -ai, car_re[...], car_im[...], reverse=True)
        lrb, lib = lr.astype(bf16), li.astype(bf16)
        ub = u_ref[...].astype(bf16)
        du = (lax.dot_general(lrb, bre[0], nt_dims, preferred_element_type=f32)
              + lax.dot_general(lib, bim[0], nt_dims, preferred_element_type=f32))
        du_ref[...] = (du + du1_ref[...].astype(f32)).astype(du_ref.dtype)
        live = (tt > 0).astype(f32)
        xpr = jnp.where(row == 0, hre_ref[7:8, :] * live, pltpu.roll(xr, 1, 0))
        xpi = jnp.where(row == 0, him_ref[7:8, :] * live, pltpu.roll(xi, 1, 0))
        acc = [
            lax.dot_general(ub, lrb, tn_dims, preferred_element_type=f32),
            lax.dot_general(ub, lib, tn_dims, preferred_element_type=f32),
            lax.dot_general(xr.astype(bf16), dyb, tn_dims, preferred_element_type=f32),
            -lax.dot_general(xi.astype(bf16), dyb, tn_dims, preferred_element_type=f32),
            jnp.sum(lr * xpr + li * xpi, axis=0, keepdims=True),
            jnp.sum(li * xpr - lr * xpi, axis=0, keepdims=True),
        ]
        outs = [dbre, dbim, dcre, dcim, dare, daim]

        @pl.when(step == 0)
        def _():
            for o, a in zip(outs, acc):
                o[0] = a

        @pl.when(step > 0)
        def _():
            for o, a in zip(outs, acc):
                o[0] = o[0] + a

    chunk = lambda r, c: pl.BlockSpec((1, r, c), lambda j, i: (j, 0, 0))
    rev = lambda w, base=0: pl.BlockSpec((tb, w), lambda j, i: (nt - 1 - i, base + j))
    halo = pl.BlockSpec((8, S5_SC), lambda j, i: (jnp.maximum((nt - 1 - i) * (tb // 8) - 1, 0), j))
    return pl.pallas_call(
        body, name="s5_scan_bwd", grid=(NCH, nt),
        in_specs=[rev(LANE), rev(LANE), rev(LANE, C_UA // LANE), rev(S5_SC), rev(S5_SC), halo, halo,
                  chunk(LANE, S5_SC), chunk(LANE, S5_SC), chunk(S5_SC, LANE), chunk(S5_SC, LANE),
                  chunk(1, S5_SC), chunk(1, S5_SC)],
        out_specs=[rev(LANE), chunk(LANE, S5_SC), chunk(LANE, S5_SC), chunk(S5_SC, LANE), chunk(S5_SC, LANE),
                   chunk(1, S5_SC), chunk(1, S5_SC)],
        out_shape=[jax.ShapeDtypeStruct((t, W_MIX), bf16),
                   jax.ShapeDtypeStruct((NCH, LANE, S5_SC), f32), jax.ShapeDtypeStruct((NCH, LANE, S5_SC), f32),
                   jax.ShapeDtypeStruct((NCH, S5_SC, LANE), f32), jax.ShapeDtypeStruct((NCH, S5_SC, LANE), f32),
                   jax.ShapeDtypeStruct((NCH, 1, S5_SC), f32), jax.ShapeDtypeStruct((NCH, 1, S5_SC), f32)],
        scratch_shapes=[pltpu.VMEM((1, S5_SC), f32)] * 2,
        compiler_params=_cparams(("parallel", "arbitrary")),
    )(dy, du1, z, xre, xim, xre, xim, bd_re.astype(bf16), bd_im.astype(bf16), cd_re.astype(bf16),
      cd_im.astype(bf16), a_re, a_im)


RG_TB = 512


def rg_conv_fwd(z, cw, cb):
    t = z.shape[0]
    tb = min(RG_TB, t)

    def body(x_ref, h_ref, cw_ref, cb_ref, o_ref):
        live = (pl.program_id(1) > 0).astype(f32)
        ext = jnp.concatenate([h_ref[...] * live, x_ref[...]], axis=0)
        w = cw_ref[0]
        acc = cb_ref[0] + w[3:4, :] * ext[8:, :]
        for k in range(3):
            acc = acc + w[k:k + 1, :] * pltpu.roll(ext, 3 - k, 0)[8:, :]
        o_ref[...] = acc

    base = C_XB // LANE
    chunk = lambda r: pl.BlockSpec((1, r, LANE), lambda j, i: (j, 0, 0))
    return pl.pallas_call(
        body, name="rg_conv_fwd", grid=(NCH, t // tb),
        in_specs=[pl.BlockSpec((tb, LANE), lambda j, i: (i, base + j)),
                  pl.BlockSpec((8, LANE), lambda j, i: (jnp.maximum(i * (tb // 8) - 1, 0), base + j)),
                  chunk(RG_CONV), chunk(1)],
        out_specs=pl.BlockSpec((tb, LANE), lambda j, i: (i, j)),
        out_shape=jax.ShapeDtypeStruct((t, W_MIX), f32),
        compiler_params=_cparams(("parallel", "parallel")),
    )(z, z, cw, cb)


def rg_conv_bwd(dxc, z, cw):
    t = z.shape[0]
    tb = min(RG_TB, t)
    nt = t // tb

    def body(g_ref, gn_ref, x_ref, h_ref, cw_ref, dx_ref, dcw_ref, dcb_ref):
        i = pl.program_id(1)
        g = g_ref[...]
        gext = jnp.concatenate([g, gn_ref[...] * (i < nt - 1).astype(f32)], axis=0)
        xext = jnp.concatenate([h_ref[...] * (i > 0).astype(f32), x_ref[...]], axis=0)
        w = cw_ref[0]
        dx = w[3:4, :] * g
        rows = [None] * RG_CONV
        rows[3] = jnp.sum(g * xext[8:, :], axis=0, keepdims=True)
        for k in range(3):
            s = 3 - k
            dx = dx + w[k:k + 1, :] * pltpu.roll(gext, tb + 8 - s, 0)[:tb, :]
            rows[k] = jnp.sum(g * pltpu.roll(xext, s, 0)[8:, :], axis=0, keepdims=True)
        dx_ref[...] = dx.astype(dx_ref.dtype)
        dcw = jnp.concatenate(rows, axis=0)
        dcb = jnp.sum(g, axis=0, keepdims=True)

        @pl.when(i == 0)
        def _():
            dcw_ref[0] = dcw
            dcb_ref[0] = dcb

        @pl.when(i > 0)
        def _():
            dcw_ref[0] = dcw_ref[0] + dcw
            dcb_ref[0] = dcb_ref[0] + dcb

    base = C_XB // LANE
    chunk = lambda r: pl.BlockSpec((1, r, LANE), lambda j, i: (j, 0, 0))
    return pl.pallas_call(
        body, name="rg_conv_bwd", grid=(NCH, nt),
        in_specs=[pl.BlockSpec((tb, LANE), lambda j, i: (i, j)),
                  pl.BlockSpec((8, LANE), lambda j, i: (jnp.minimum((i + 1) * (tb // 8), t // 8 - 1), j)),
                  pl.BlockSpec((tb, LANE), lambda j, i: (i, base + j)),
                  pl.BlockSpec((8, LANE), lambda j, i: (jnp.maximum(i * (tb // 8) - 1, 0), base + j)),
                  chunk(RG_CONV)],
        out_specs=[pl.BlockSpec((tb, LANE), lambda j, i: (i, j)), chunk(RG_CONV), chunk(1)],
        out_shape=[jax.ShapeDtypeStruct((t, W_MIX), bf16), jax.ShapeDtypeStruct((NCH, RG_CONV, LANE), f32),
                   jax.ShapeDtypeStruct((NCH, 1, LANE), f32)],
        compiler_params=_cparams(("parallel", "arbitrary")),
    )(dxc, dxc, z, z, cw)


def rg_scan_fwd(a, b):
    t = a.shape[0]
    tb = min(RG_TB, t)

    def body(a_ref, b_ref, h_ref, car):
        @pl.when(pl.program_id(1) == 0)
        def _():
            car[...] = jnp.zeros_like(car)

        h_ref[...], car[...] = tv_scan(a_ref[...], b_ref[...], car[...])

    spec = pl.BlockSpec((tb, LANE), lambda j, i: (i, j))
    return pl.pallas_call(
        body, name="rg_scan_fwd", grid=(NCH, t // tb), in_specs=[spec, spec], out_specs=spec,
        out_shape=jax.ShapeDtypeStruct((t, W_MIX), f32), scratch_shapes=[pltpu.VMEM((1, LANE), f32)],
        compiler_params=_cparams(("parallel", "arbitrary")),
    )(a, b)


def rg_scan_bwd(dh, a, h):
    t = a.shape[0]
    tb = min(RG_TB, t)
    nt = t // tb

    def body(g_ref, a_ref, an_ref, h_ref, hp_ref, da_ref, db_ref, car):
        step = pl.program_id(1)
        tt = nt - 1 - step

        @pl.when(step == 0)
        def _():
            car[...] = jnp.zeros_like(car)

        row = lax.broadcasted_iota(jnp.int32, (tb, 1), 0)
        an = an_ref[0:1, :] * (tt < nt - 1).astype(f32)
        aa = jnp.where(row == tb - 1, an, pltpu.roll(a_ref[...], tb - 1, 0))
        lam, car[...] = tv_scan(aa, g_ref[...], car[...], reverse=True)
        hp = jnp.where(row == 0, hp_ref[7:8, :] * (tt > 0).astype(f32), pltpu.roll(h_ref[...], 1, 0))
        da_ref[...] = lam * hp
        db_ref[...] = lam

    rev = pl.BlockSpec((tb, LANE), lambda j, i: (nt - 1 - i, j))
    nxt = pl.BlockSpec((8, LANE), lambda j, i: (jnp.minimum((nt - i) * (tb // 8), t // 8 - 1), j))
    prv = pl.BlockSpec((8, LANE), lambda j, i: (jnp.maximum((nt - 1 - i) * (tb // 8) - 1, 0), j))
    return pl.pallas_call(
        body, name="rg_scan_bwd", grid=(NCH, nt), in_specs=[rev, rev, nxt, rev, prv], out_specs=[rev, rev],
        out_shape=[jax.ShapeDtypeStruct((t, W_MIX), f32)] * 2, scratch_shapes=[pltpu.VMEM((1, LANE), f32)],
        compiler_params=_cparams(("parallel", "arbitrary")),
    )(dh, a, a, h, h)


HG_TB = 256


def _heads(v):
    return jnp.stack([v[:, LANE * h:LANE * (h + 1)] for h in range(HG_HEADS)])


def _unheads(v):
    return jnp.concatenate([v[h] for h in range(HG_HEADS)], axis=-1)


def _bmm(eq, a, b):
    return jnp.einsum(eq, a.astype(bf16), b.astype(bf16), preferred_element_type=f32)


def hg_chunk_fwd(qs, kk, gcum, z):
    t = qs.shape[0]
    tb = min(HG_TB, t)
    nc = tb // HG_SUB

    def body(q_ref, k_ref, g_ref, v_ref, o_ref, sall_ref, st_ref):
        @pl.when(pl.program_id(0) == 0)
        def _():
            st_ref[...] = jnp.zeros_like(st_ref)

        ri = lax.broadcasted_iota(jnp.int32, (1, HG_SUB, 1), 1)

        def chunk(c, carry):
            rows = pl.ds(pl.multiple_of(c * HG_SUB, HG_SUB), HG_SUB)
            q, k, g, v = _heads(q_ref[rows, :]), _heads(k_ref[rows, :]), _heads(g_ref[rows, :]), _heads(v_ref[rows, :])
            st = st_ref[...]
            sall_ref[c] = st
            o = _bmm('htk,hvk->htv', q * jnp.exp(g), st)
            for s in range(HG_SUB):
                p = jnp.where(ri >= s, jnp.exp(jnp.minimum(g - g[:, s:s + 1, :], 0.0)), 0.0)
                col = jnp.sum(q * k[:, s:s + 1, :] * p, axis=-1, keepdims=True)
                o = o + col * v[:, s:s + 1, :]
            gl = g[:, HG_SUB - 1:HG_SUB, :]
            st_ref[...] = st * jnp.exp(gl) + _bmm('htv,htk->hvk', v, k * jnp.exp(gl - g))
            o_ref[rows, :] = _unheads(o)
            return carry

        lax.fori_loop(0, nc, chunk, 0)

    spec = lambda base=0: pl.BlockSpec((tb, W_MIX), lambda i: (i, base))
    return pl.pallas_call(
        body, name="hg_chunk_fwd", grid=(t // tb,),
        in_specs=[spec(), spec(), spec(), spec(C_I // W_MIX)],
        out_specs=[spec(), pl.BlockSpec((nc, HG_HEADS, HG_DK, HG_DK), lambda i: (i, 0, 0, 0))],
        out_shape=[jax.ShapeDtypeStruct((t, W_MIX), f32),
                   jax.ShapeDtypeStruct((t // HG_SUB, HG_HEADS, HG_DK, HG_DK), f32)],
        scratch_shapes=[pltpu.VMEM((HG_HEADS, HG_DK, HG_DK), f32)],
        compiler_params=_cparams(("arbitrary",)),
    )(qs, kk, gcum, z)


def hg_chunk_bwd(do, qs, kk, gcum, z, sall):
    t = qs.shape[0]
    tb = min(HG_TB, t)
    nc = tb // HG_SUB
    nt = t // tb

    def body(do_ref, q_ref, k_ref, g_ref, v_ref, sall_ref, dq_ref, dk_ref, dg_ref, dv_ref, dst_ref):
        @pl.when(pl.program_id(0) == 0)
        def _():
            dst_ref[...] = jnp.zeros_like(dst_ref)

        ri = lax.broadcasted_iota(jnp.int32, (1, HG_SUB, 1), 1)

        def chunk(cc, carry):
            c = nc - 1 - cc
            rows = pl.ds(pl.multiple_of(c * HG_SUB, HG_SUB), HG_SUB)
            q, k, g, v = _heads(q_ref[rows, :]), _heads(k_ref[rows, :]), _heads(g_ref[rows, :]), _heads(v_ref[rows, :])
            d_o = _heads(do_ref[rows, :])
            st = sall_ref[c]
            dsn = dst_ref[...]
            eg = jnp.exp(g)
            qe = q * eg
            gl = g[:, HG_SUB - 1:HG_SUB, :]
            egl = jnp.exp(gl)
            dec = jnp.exp(gl - g)
            kd = k * dec
            dqe = _bmm('htv,hvk->htk', d_o, st)
            dst_ref[...] = _bmm('htv,htk->hvk', d_o, qe) + dsn * egl
            dgl_dec = jnp.sum(dsn * st, axis=1, keepdims=True) * egl
            dv = _bmm('htk,hvk->htv', kd, dsn)
            dkd = _bmm('htv,hvk->htk', v, dsn)
            a1 = jnp.zeros_like(q)
            a2 = jnp.zeros_like(q)
            for s in range(HG_SUB):
                p = jnp.where(ri >= s, jnp.exp(jnp.minimum(g - g[:, s:s + 1, :], 0.0)), 0.0)
                krow = k[:, s:s + 1, :]
                col = jnp.sum(q * krow * p, axis=-1, keepdims=True)
                dcol = jnp.sum(d_o * v[:, s:s + 1, :], axis=-1, keepdims=True)
                dv = jnp.where(ri == s, dv + jnp.sum(col * d_o, axis=1, keepdims=True), dv)
                t1 = dcol * p
                a1 = a1 + t1 * krow
                a2 = jnp.where(ri == s, jnp.sum(t1 * q, axis=1, keepdims=True), a2)
            dgl = jnp.sum(dkd * kd, axis=1, keepdims=True) + dgl_dec
            dg = dqe * qe + q * a1 - k * a2 - dkd * kd
            dg = jnp.where(ri == HG_SUB - 1, dg + dgl, dg)
            dq_ref[rows, :] = _unheads(dqe * eg + a1)
            dk_ref[rows, :] = _unheads(dkd * dec + a2)
            dg_ref[rows, :] = _unheads(dg)
            dv_ref[rows, :] = _unheads(dv).astype(dv_ref.dtype)
            return carry

        lax.fori_loop(0, nc, chunk, 0)

    spec = lambda base=0: pl.BlockSpec((tb, W_MIX), lambda i: (nt - 1 - i, base))
    return pl.pallas_call(
        body, name="hg_chunk_bwd", grid=(nt,),
        in_specs=[spec(), spec(), spec(), spec(), spec(C_I // W_MIX),
                  pl.BlockSpec((nc, HG_HEADS, HG_DK, HG_DK), lambda i: (nt - 1 - i, 0, 0, 0))],
        out_specs=[spec(), spec(), spec(), spec()],
        out_shape=[jax.ShapeDtypeStruct((t, W_MIX), f32)] * 3 + [jax.ShapeDtypeStruct((t, W_MIX), bf16)],
        scratch_shapes=[pltpu.VMEM((HG_HEADS, HG_DK, HG_DK), f32)],
        compiler_params=_cparams(("arbitrary",)),
    )(do, qs, kk, gcum, z, sall)


def adamw(w, m, v, slots, *, name, tr):
    nl, r, c = w.shape
    tr = min(tr, r)
    flat = [a for per_layer in slots for a in per_layer]
    c1 = 1.0 / (1.0 - ADAM_B1 ** ADAM_STEP)
    c2 = 1.0 / (1.0 - ADAM_B2 ** ADAM_STEP)

    def body(*refs):
        w_ref, m_ref, v_ref = refs[:3]
        s_refs = list(refs[3:3 + len(flat)])
        g_ref, d_ref, mo_ref, vo_ref = refs[3 + len(flat):]
        for l in range(nl):
            parts = [s_refs.pop(0) for _ in slots[l]]
            g = None
            for p in parts:
                for s in range(p.shape[0]):
                    term = p[s].astype(f32)
                    g = term if g is None else g + term
            mn = ADAM_B1 * m_ref[l] + (1.0 - ADAM_B1) * g
            vn = ADAM_B2 * v_ref[l] + (1.0 - ADAM_B2) * (g * g)
            g_ref[l] = g
            mo_ref[l] = mn
            vo_ref[l] = vn
            d_ref[l] = -ADAM_LR * ((mn * c1) / (jnp.sqrt(vn * c2) + ADAM_EPS) + ADAM_WD * w_ref[l])

    full = pl.BlockSpec((nl, tr, c), lambda i: (0, i, 0))
    slot = [pl.BlockSpec((a.shape[0], tr, c), lambda i: (0, i, 0)) for a in flat]
    return pl.pallas_call(
        body, name=name, grid=(r // tr,), in_specs=[full] * 3 + slot, out_specs=[full] * 4,
        out_shape=[jax.ShapeDtypeStruct(w.shape, f32)] * 4, compiler_params=_cparams(("parallel",)),
    )(w, m, v, *flat)


def _slab(ref, axis, idx, n):
    return ref.at[tuple([slice(None)] * axis + [pl.ds(idx * n, n)])]


def all_gather(x, axis, *, name):
    n = x.shape[axis]
    out_shape = x.shape[:axis] + (N_DEV * n,) + x.shape[axis + 1:]

    def body(x_ref, out_ref, send_sems, recv_sems, local_sem):
        xx, yy, cc = lax.axis_index("x"), lax.axis_index("y"), lax.axis_index("c")
        me, sibling = (xx, yy, cc), (xx, yy, 1 - cc)
        chips = [(1 - xx, yy), (xx, 1 - yy), (1 - xx, 1 - yy)]

        def slab(px, py, pc):
            return _slab(out_ref, axis, 4 * px + 2 * py + pc, n)

        def copy(k, block, to, src=None):
            return pltpu.make_async_remote_copy(
                src_ref=slab(*block) if src is None else src, dst_ref=slab(*block),
                send_sem=send_sems.at[k], recv_sem=recv_sems.at[k], device_id=to, device_id_type=MESH)

        mine = pltpu.make_async_copy(x_ref, slab(*me), local_sem)
        mine.start()
        first = [copy(0, me, sibling, src=x_ref)]
        first += [copy(1 + j, me, (*chip, cc), src=x_ref) for j, chip in enumerate(chips)]
        for cp in first:
            cp.start()
        passed = [copy(4 + j, (*chip, cc), sibling) for j, chip in enumerate(chips)]
        for j, chip in enumerate(chips):
            copy(1 + j, (*chip, cc), me).wait_recv()
            passed[j].start()
        copy(0, sibling, me).wait_recv()
        for j, chip in enumerate(chips):
            copy(4 + j, (*chip, 1 - cc), me).wait_recv()
        for cp in first + passed:
            cp.wait_send()
        mine.wait()

    return pl.pallas_call(
        body, name=name, out_shape=jax.ShapeDtypeStruct(out_shape, x.dtype), in_specs=[ANY], out_specs=ANY,
        scratch_shapes=[pltpu.SemaphoreType.DMA((7,)), pltpu.SemaphoreType.DMA((7,)), pltpu.SemaphoreType.DMA],
    )(x)


N_CHIP = 4


HBM = pl.BlockSpec(memory_space=pltpu.HBM)
SEM = pl.BlockSpec(memory_space=pltpu.SEMAPHORE)
EFFECT = pltpu.SideEffectType.DATAFLOW_SIDE_EFFECTING
TOKEN = jax.ShapeDtypeStruct((8, LANE), f32)


def _in_hbm(a):
    return pltpu.with_memory_space_constraint(a, pltpu.HBM)


def pair_sums(g, axis, *, name):
    n = g.shape[axis] // N_DEV
    slab_shape = g.shape[:axis] + (n,) + g.shape[axis + 1:]
    cols = slab_shape[-1]
    rows = math.prod(slab_shape[:-1])
    col_slabs = axis == g.ndim - 1
    assert col_slabs or (axis == 0 and g.ndim == 2)

    def swap_body(g_ref, got_ref, send_sems, recv_sems):
        xx, yy, cc = lax.axis_index("x"), lax.axis_index("y"), lax.axis_index("c")
        copies = [pltpu.make_async_remote_copy(
            src_ref=_slab(g_ref, axis, 2 * q + 1 - cc, n), dst_ref=got_ref.at[q],
            send_sem=send_sems.at[q], recv_sem=recv_sems.at[q], device_id=(xx, yy, 1 - cc), device_id_type=MESH)
            for q in range(N_CHIP)]
        for cp in copies:
            cp.start()
        for cp in copies:
            cp.wait()

    got = pl.pallas_call(
        swap_body, name=name + "_swap", out_shape=jax.ShapeDtypeStruct((N_CHIP,) + slab_shape, g.dtype),
        in_specs=[ANY], out_specs=ANY, scratch_shapes=[pltpu.SemaphoreType.DMA((N_CHIP,))] * 2,
    )(g)

    tr = min(256, rows)

    def add_body(a0_ref, a1_ref, b_ref, pair_ref, own_ref):
        xx, yy, cc = lax.axis_index("x"), lax.axis_index("y"), lax.axis_index("c")
        mine = jnp.where(cc == 0, a0_ref[...], a1_ref[...])
        s = (mine.astype(f32) + b_ref[0].astype(f32)).astype(bf16)
        pair_ref[0] = s

        @pl.when(pl.program_id(1) == 2 * xx + yy)
        def _():
            own_ref[0] = s

    if col_slabs:
        a_spec = lambda c: pl.BlockSpec((tr, cols), lambda i, q: (i, 2 * q + c))
    else:
        a_spec = lambda c: pl.BlockSpec((tr, cols), lambda i, q: ((2 * q + c) * (n // tr) + i, 0))
    by_chip = pl.BlockSpec((1, tr, cols), lambda i, q: (q, i, 0))
    g2 = g.reshape(-1, g.shape[-1])
    pair, own = pl.pallas_call(
        add_body, name=name + "_add", grid=(rows // tr, N_CHIP), in_specs=[a_spec(0), a_spec(1), by_chip],
        out_specs=[by_chip, pl.BlockSpec((1, tr, cols), lambda i, q: (0, i, 0))],
        out_shape=[jax.ShapeDtypeStruct((N_CHIP, rows, cols), bf16), jax.ShapeDtypeStruct((1, rows, cols), bf16)],
        compiler_params=_cparams(("parallel", "arbitrary")),
    )(g2, g2, got.reshape(N_CHIP, rows, cols))
    return own, pair


def _send_copies(p_refs, land_refs, send_sems, recv_sems):
    xx, yy, cc = lax.axis_index("x"), lax.axis_index("y"), lax.axis_index("c")
    copies = []
    for t, (p, land) in enumerate(zip(p_refs, land_refs)):
        for k in range(1, N_CHIP):
            px = 1 - xx if k & 2 else xx
            py = 1 - yy if k & 1 else yy
            s = (N_CHIP - 1) * t + k - 1
            copies.append(pltpu.make_async_remote_copy(
                src_ref=p.at[2 * px + py], dst_ref=land.at[k - 1], send_sem=send_sems.at[s], recv_sem=recv_sems.at[s],
                device_id=(px, py, cc), device_id_type=MESH))
    return copies


def send_pairs_start(pairs, *, name):
    nt = len(pairs)
    lands = [lax.empty((N_CHIP - 1,) + p.shape[1:], p.dtype) for p in pairs]

    def body(*refs):
        p_refs, land_refs = refs[:nt], refs[nt:2 * nt]
        send_sems, recv_sems = refs[2 * nt], refs[2 * nt + 1]
        token = refs[-1]
        for cp in _send_copies(p_refs, land_refs, send_sems, recv_sems):
            cp.start()
        token[...] = jnp.zeros_like(token)

    nsem = (N_CHIP - 1) * nt
    outs = pl.pallas_call(
        body, name=name,
        out_shape=(pltpu.SemaphoreType.DMA((nsem,)), pltpu.SemaphoreType.DMA((nsem,)))
        + tuple(pltpu.HBM(a.shape, a.dtype) for a in list(pairs) + lands) + (TOKEN,),
        in_specs=[HBM] * (2 * nt), out_specs=(SEM, SEM) + (HBM,) * (2 * nt) + (VM,),
        input_output_aliases={i: 2 + i for i in range(2 * nt)},
        compiler_params=pltpu.CompilerParams(has_side_effects=EFFECT),
    )(*[_in_hbm(a) for a in list(pairs) + lands])
    return outs[:-1], outs[-1]


def send_pairs_wait(handles, after, *, name):
    send_sems, recv_sems = handles[0], handles[1]
    bufs = handles[2:]
    nt = len(bufs) // 2

    def body(*refs):
        p_refs, land_refs = refs[:nt], refs[nt:2 * nt]
        send_sems, recv_sems = refs[2 * nt], refs[2 * nt + 1]
        for cp in _send_copies(p_refs, land_refs, send_sems, recv_sems):
            cp.wait_send()
            cp.wait_recv()

    outs = pl.pallas_call(
        body, name=name, out_shape=tuple(pltpu.HBM(a.shape, a.dtype) for a in bufs),
        in_specs=[HBM] * (2 * nt) + [SEM, SEM, ANY], out_specs=(HBM,) * (2 * nt),
        input_output_aliases={i: i for i in range(2 * nt)},
        compiler_params=pltpu.CompilerParams(has_side_effects=EFFECT),
    )(*bufs, send_sems, recv_sems, after)
    return outs[nt:]


def _gather_copies(x_refs, land_refs, axes, send_sems, recv_sems):
    xx, yy, cc = lax.axis_index("x"), lax.axis_index("y"), lax.axis_index("c")
    me = 4 * xx + 2 * yy + cc
    copies = []
    for t, (x_ref, land, axis) in enumerate(zip(x_refs, land_refs, axes)):
        n = x_ref.shape[axis]
        for k in range(1, N_DEV):
            px = 1 - xx if k & 4 else xx
            py = 1 - yy if k & 2 else yy
            pc = 1 - cc if k & 1 else cc
            s = (N_DEV - 1) * t + k - 1
            copies.append(pltpu.make_async_remote_copy(
                src_ref=x_ref, dst_ref=_slab(land, axis, me, n), send_sem=send_sems.at[s], recv_sem=recv_sems.at[s],
                device_id=(px, py, pc), device_id_type=MESH))
    return copies


def gather_start(xs, axes, after, *, name):
    nt = len(xs)
    me = 4 * lax.axis_index("x") + 2 * lax.axis_index("y") + lax.axis_index("c")
    lands = []
    for x, axis in zip(xs, axes):
        full = lax.empty(x.shape[:axis] + (N_DEV * x.shape[axis],) + x.shape[axis + 1:], x.dtype)
        lands.append(lax.dynamic_update_slice_in_dim(full, x, me * x.shape[axis], axis))

    def body(*refs):
        x_refs, land_refs = refs[:nt], refs[nt:2 * nt]
        send_sems, recv_sems = refs[2 * nt + 1], refs[2 * nt + 2]
        token = refs[-1]
        for cp in _gather_copies(x_refs, land_refs, axes, send_sems, recv_sems):
            cp.start()
        token[...] = jnp.zeros_like(token)

    nsem = (N_DEV - 1) * nt
    outs = pl.pallas_call(
        body, name=name,
        out_shape=(pltpu.SemaphoreType.DMA((nsem,)), pltpu.SemaphoreType.DMA((nsem,)))
        + tuple(pltpu.HBM(a.shape, a.dtype) for a in list(xs) + lands) + (TOKEN,),
        in_specs=[HBM] * (2 * nt) + [ANY], out_specs=(SEM, SEM) + (HBM,) * (2 * nt) + (VM,),
        input_output_aliases={i: 2 + i for i in range(2 * nt)},
        compiler_params=pltpu.CompilerParams(has_side_effects=EFFECT),
    )(*[_in_hbm(a) for a in list(xs) + lands], after)
    return outs[:-1], outs[-1]


def gather_wait(handles, axes, after, *, name):
    send_sems, recv_sems = handles[0], handles[1]
    bufs = handles[2:]
    nt = len(bufs) // 2

    def body(*refs):
        x_refs, land_refs = refs[:nt], refs[nt:2 * nt]
        send_sems, recv_sems = refs[2 * nt], refs[2 * nt + 1]
        for cp in _gather_copies(x_refs, land_refs, axes, send_sems, recv_sems):
            cp.wait_send()
            cp.wait_recv()

    outs = pl.pallas_call(
        body, name=name, out_shape=tuple(pltpu.HBM(a.shape, a.dtype) for a in bufs),
        in_specs=[HBM] * (2 * nt) + [SEM, SEM, ANY], out_specs=(HBM,) * (2 * nt),
        input_output_aliases={i: i for i in range(2 * nt)},
        compiler_params=pltpu.CompilerParams(has_side_effects=EFFECT),
    )(*bufs, send_sems, recv_sems, after)
    return outs[nt:]


def _blockdiag(b, nb):
    j, _, r, c = b.shape
    eye = jnp.eye(nb, dtype=bool)[None, :, None, :, None]
    return jnp.where(eye, b[:, :, :, None, :], jnp.zeros((), b.dtype)).reshape(j, nb * r, nb * c)


def _diagblocks(d, nb):
    j, rr, cc = d.shape
    return jnp.einsum('jarac->jarc', d.reshape(j, nb, rr // nb, nb, cc // nb))


def _s5_b_dense(bbar):
    return _blockdiag(bbar.transpose(0, 2, 1).reshape(NCH, 8, S5_GROUP, S5_STATE), 8)


def _s5_b_undense(d):
    return _diagblocks(d, 8).reshape(S5_GROUPS, S5_GROUP, S5_STATE).transpose(0, 2, 1)


def _s5_c_dense(c):
    return _blockdiag(c.transpose(0, 2, 1).reshape(NCH, 8, S5_STATE, S5_GROUP), 8)


def _s5_c_undense(d):
    return _diagblocks(d, 8).reshape(S5_GROUPS, S5_STATE, S5_GROUP).transpose(0, 2, 1)


def _rg_dense(w):
    return _blockdiag(w.reshape(NCH, 2, RG_BLOCK, RG_BLOCK), 2)


def _rg_undense(d):
    return _diagblocks(d, 2).reshape(RG_BLOCKS, RG_BLOCK, RG_BLOCK)


def _chunks(v):
    return v.reshape(NCH, 1, LANE)


def _tri(tm):
    r = jnp.arange(tm)
    m = (r[:, None] >= r[None, :]) & (r[:, None] // HG_SUB == r[None, :] // HG_SUB)
    m = m.astype(f32)
    return m[None], m.T[None]


SMALL = ['norm_w', 's5_lambda_re', 's5_lambda_im', 's5_log_step', 's5_b_re', 's5_b_im', 's5_c_re', 's5_c_im',
         's5_d', 's5_b_glu', 'rg_conv_w', 'rg_conv_b', 'rg_w_a', 'rg_b_a', 'rg_w_x', 'rg_b_x', 'rg_lambda',
         'hg_lower_bounds', 'hg_norm_w', 'final_norm_w']
WEIGHTS = ['norm_w', 'w_in', 's5_lambda_re', 's5_lambda_im', 's5_log_step', 's5_b_re', 's5_b_im', 's5_c_re',
           's5_c_im', 's5_d', 's5_w_glu', 's5_b_glu', 'rg_conv_w', 'rg_conv_b', 'rg_w_a', 'rg_b_a', 'rg_w_x',
           'rg_b_x', 'rg_lambda', 'hg_lower_bounds', 'hg_norm_w', 'w_branch', 'w_out', 'final_norm_w']
PACK_ROWS = 512


def _pack(arrs):
    flat = jnp.concatenate([a.reshape(-1) for a in arrs])
    pad = (-flat.shape[0]) % (PACK_ROWS * LANE)
    return jnp.pad(flat, (0, pad)).reshape(1, -1, LANE)


def _unpack(buf, shapes):
    flat = buf.reshape(-1)
    out, off = [], 0
    for s in shapes:
        n = math.prod(s)
        out.append(flat[off:off + n].reshape(s))
        off += n
    return out


def _step(x, tgt, w, m, v):
    t = x.shape[0]
    tri, tri_t = _tri(min(LANE, t))
    me = 4 * lax.axis_index("x") + 2 * lax.axis_index("y") + lax.axis_index("c")

    big = ('w_in', 's5_w_glu', 'w_branch', 'w_out')
    big_axis = (1, 0, 2, 0)
    shards = lambda l: [w[k][l].astype(bf16) for k in big]
    win, wglu, wbr, wout = ([None] * DEPTH for _ in range(4))
    win[0] = all_gather(shards(0)[0], big_axis[0], name="ag_w_in")
    rest0_axis = big_axis[1:] + (1,)
    rest0, rest0_token = gather_start(shards(0)[1:] + [w['rg_conv_w'].reshape(DEPTH * RG_CONV, LANE)], rest0_axis,
                                      win[0], name="ag_start_0")

    lb_rows = [w['hg_lower_bounds'][l][None] for l in range(DEPTH)]
    lbs = whole(lb_prep_fn, lb_rows, [(1, W_MIX)] * DEPTH, name="lb_prep")

    saved = []
    for l in range(DEPTH):
        s = {}
        nw = w['norm_w'][l].reshape(1, 1, D_MODEL)
        (h,) = rowwise(ln_fn, [(x, 0, D_MODEL)], [nw], [], [(D_MODEL, bf16)], name="ln_fwd")
        token = None
        if l + 1 < DEPTH:
            handles, token = gather_start(shards(l + 1), big_axis, rest0_token if l == 0 else x, name=f"ag_start_{l + 1}")
        z = mm(h, win[l], after=token, name="mm_in")
        if l == 0:
            wglu[0], wbr[0], wout[0], conv_w = gather_wait(rest0, rest0_axis, z, name="ag_wait_0")
            conv_w = conv_w.reshape(DEPTH, RG_CONV, W_MIX)
        s5p = [w['s5_lambda_re'][l][..., None], w['s5_lambda_im'][l][..., None], w['s5_log_step'][l][:, None, None],
               w['s5_b_re'][l], w['s5_b_im'][l]]
        gp = (S5_GROUPS, S5_STATE)
        abar_re, abar_im, bbar_re, bbar_im = whole(
            s5_prep_fn, s5p, [gp + (1,), gp + (1,), gp + (S5_GROUP,), gp + (S5_GROUP,)], name="s5_prep")
        a_re, a_im = abar_re.reshape(NCH, 1, S5_SC), abar_im.reshape(NCH, 1, S5_SC)
        bd_re, bd_im = _s5_b_dense(bbar_re), _s5_b_dense(bbar_im)
        cd_re, cd_im = _s5_c_dense(w['s5_c_re'][l]), _s5_c_dense(w['s5_c_im'][l])
        yssm, xre, xim = s5_scan_fwd(z, bd_re, bd_im, cd_re, cd_im, a_re, a_im)
        s5post_p = [w['s5_d'][l].reshape(1, 1, W_MIX), wglu[l].astype(f32)[None], w['s5_b_glu'][l].reshape(1, 1, W_MIX)]
        s5post_rows = [(yssm, 0, W_MIX), (z, C_UA, W_MIX), (z, C_GA, W_MIX)]
        (ya,) = rowwise(s5_post_fn, s5post_rows, s5post_p, [], [(W_MIX, bf16)], name="s5_post_fwd")
        cw, cb = conv_w[l].reshape(RG_CONV, NCH, LANE).transpose(1, 0, 2), _chunks(w['rg_conv_b'][l])
        xc = rg_conv_fwd(z, cw, cb)
        rg_p = [_rg_dense(w['rg_w_a'][l]), _chunks(w['rg_b_a'][l]), _rg_dense(w['rg_w_x'][l]),
                _chunks(w['rg_b_x'][l]), _chunks(w['rg_lambda'][l])]
        ra, rb = rowwise(rg_gate_fn, [(xc, 0, W_MIX)], rg_p, [], [(W_MIX, f32)] * 2, name="rg_gate_fwd",
                         ncol=NCH, tm=TM_CHUNK, rowid=True)
        hb = rg_scan_fwd(ra, rb)
        hg_rows = [(z, C_Q, W_MIX), (z, C_F, W_MIX)]
        hg_p = [_chunks(lbs[l].reshape(W_MIX))]
        qs, kk, gcum = rowwise(hg_pre_fn, hg_rows, hg_p, [tri, tri_t], [(W_MIX, f32)] * 3, name="hg_pre_fwd", ncol=NCH, tm=TM_CHUNK)
        oc, sall = hg_chunk_fwd(qs, kk, gcum, z)
        bp_rows = [(hb, 0, W_MIX), (z, C_GB, W_MIX), (oc, 0, W_MIX), (z, C_GC, W_MIX)]
        bp_p = [_chunks(w['hg_norm_w'][l])]
        yb, yc = rowwise(branch_prep_fn, bp_rows, bp_p, [], [(W_MIX, bf16)] * 2, name="branch_prep_fwd", ncol=NCH, tm=TM_CHUNK)
        ys = [ya, yb, yc]
        br = [mm(ys[n], wbr[l][n], name="mm_branch", out_dtype=bf16) for n in range(N_BRANCH)]
        mg_rows = [(br[n], 0, D_MODEL) for n in range(N_BRANCH)] + [(z, C_GATE + n * D_MODEL, D_MODEL) for n in range(N_BRANCH)]
        (merged,) = rowwise(merge_fn, mg_rows, [], [], [(D_MODEL, bf16)], name="merge_fwd", ncol=2)
        x_new = mm(merged, wout[l], add=x, name="mm_out")
        s.update(x=x, h=h, z=z, s5p=s5p, s5=(bd_re, bd_im, cd_re, cd_im, a_re, a_im), xre=xre, xim=xim,
                 s5post_rows=s5post_rows, s5post_p=s5post_p, cw=cw, xc=xc, rg_p=rg_p, ra=ra, hb=hb,
                 hg_rows=hg_rows, hg_p=hg_p, qs=qs, kk=kk, gcum=gcum, sall=sall, bp_rows=bp_rows, bp_p=bp_p,
                 ys=ys, mg_rows=mg_rows, merged=merged, nw=nw)
        saved.append(s)
        x = x_new
        if l + 1 < DEPTH:
            win[l + 1], wglu[l + 1], wbr[l + 1], wout[l + 1] = gather_wait(handles, big_axis, x, name=f"ag_wait_{l + 1}")

    fnw = w['final_norm_w'].reshape(1, 1, D_MODEL)
    ones = jnp.ones((t, 1), f32)
    dx, d_fnw, loss_sum = rowwise_vjp(loss_fn, [(x, 0, D_MODEL), (tgt, 0, D_MODEL)], [fnw], [], [(ones, 0, 1)],
                                      [(0, f32)], name="loss_head", sum_primal=0)
    loss = lax.psum(loss_sum.reshape(()), ("x", "y", "c"))

    small_g = {k: [None] * DEPTH for k in SMALL if k != 'final_norm_w'}
    own_sums, in_flight = [None] * DEPTH, [None] * DEPTH
    d_lbs = [None] * DEPTH
    token = None
    for l in reversed(range(DEPTH)):
        s = saved[l]
        z = s['z']
        dxb = dx.astype(bf16)
        d_merged = mm(dxb, wout[l], bt=True, after=token, name="mm_out_dx", out_dtype=bf16)
        d_wout = mm(s['merged'].T, dxb, name="mm_out_dw", out_dtype=bf16)
        mg = rowwise_vjp(merge_fn, s['mg_rows'], [], [], [(d_merged, 0, D_MODEL)],
                         [(n, bf16) for n in range(2 * N_BRANCH)], name="merge_bwd", ncol=2)
        d_br, d_gl = mg[:N_BRANCH], mg[N_BRANCH:]
        d_ys = [mm(d_br[n], wbr[l][n], bt=True, name="mm_branch_dx", out_dtype=bf16) for n in range(N_BRANCH)]
        d_wbr = jnp.stack([mm(s['ys'][n].T, d_br[n], name="mm_branch_dw", out_dtype=bf16) for n in range(N_BRANCH)])
        d_hb, d_gb, d_oc, d_gc, d_hnw = rowwise_vjp(
            branch_prep_fn, s['bp_rows'], s['bp_p'], [], [(d_ys[1], 0, W_MIX), (d_ys[2], 0, W_MIX)],
            [(0, f32), (1, bf16), (2, f32), (3, bf16)], name="branch_prep_bwd", ncol=NCH, tm=TM_CHUNK)
        small_g['hg_norm_w'][l] = d_hnw.reshape(W_MIX)
        d_qs, d_kk, d_gcum, d_i = hg_chunk_bwd(d_oc, s['qs'], s['kk'], s['gcum'], z, s['sall'])
        d_q, d_f, d_lb = rowwise_vjp(
            hg_pre_fn, s['hg_rows'], s['hg_p'], [tri, tri_t], [(d_qs, 0, W_MIX), (d_kk, 0, W_MIX), (d_gcum, 0, W_MIX)],
            [(0, bf16), (1, bf16)], name="hg_pre_bwd", ncol=NCH, tm=TM_CHUNK)
        d_lbs[l] = d_lb.reshape(1, W_MIX)
        d_ra, d_rb = rg_scan_bwd(d_hb, s['ra'], s['hb'])
        rg = rowwise_vjp(rg_gate_fn, [(s['xc'], 0, W_MIX)], s['rg_p'], [], [(d_ra, 0, W_MIX), (d_rb, 0, W_MIX)],
                         [(0, f32)], name="rg_gate_bwd", ncol=NCH, tm=TM_CHUNK, rowid=True)
        d_xc, d_wa, d_ba, d_wx, d_bx, d_lam = rg
        d_xb, d_cw, d_cb = rg_conv_bwd(d_xc, z, s['cw'])
        small_g['rg_w_a'][l], small_g['rg_w_x'][l] = _rg_undense(d_wa), _rg_undense(d_wx)
        small_g['rg_b_a'][l], small_g['rg_b_x'][l] = d_ba.reshape(W_MIX), d_bx.reshape(W_MIX)
        small_g['rg_lambda'][l] = d_lam.reshape(W_MIX)
        small_g['rg_conv_w'][l] = d_cw.transpose(1, 0, 2).reshape(RG_CONV, W_MIX)
        small_g['rg_conv_b'][l] = d_cb.reshape(W_MIX)
        d_yssm, d_u1, d_ga, d_d, d_wglu, d_bglu = rowwise_vjp(
            s5_post_fn, s['s5post_rows'], s['s5post_p'], [], [(d_ys[0], 0, W_MIX)],
            [(0, bf16), (1, bf16), (2, bf16)], name="s5_post_bwd")
        small_g['s5_d'][l], small_g['s5_b_glu'][l] = d_d.reshape(W_MIX), d_bglu.reshape(W_MIX)
        d_ua, d_bdre, d_bdim, d_cdre, d_cdim, d_are, d_aim = s5_scan_bwd(d_yssm, d_u1, z, s['xre'], s['xim'], *s['s5'])
        small_g['s5_c_re'][l], small_g['s5_c_im'][l] = _s5_c_undense(d_cdre), _s5_c_undense(d_cdim)
        gp = (S5_GROUPS, S5_STATE, 1)
        s5g = whole_vjp(s5_prep_fn, s['s5p'],
                        [d_are.reshape(gp), d_aim.reshape(gp), _s5_b_undense(d_bdre), _s5_b_undense(d_bdim)],
                        name="s5_prep_bwd")
        small_g['s5_lambda_re'][l] = s5g[0].reshape(S5_GROUPS, S5_STATE)
        small_g['s5_lambda_im'][l] = s5g[1].reshape(S5_GROUPS, S5_STATE)
        small_g['s5_log_step'][l] = s5g[2].reshape(S5_GROUPS)
        small_g['s5_b_re'][l], small_g['s5_b_im'][l] = s5g[3], s5g[4]
        dz = jnp.concatenate([d_ua, d_ga, d_xb, d_gb, d_q, d_f, d_i, d_gc] + list(d_gl), axis=1)
        d_win = mm(s['h'].T, dz, name="mm_in_dw", out_dtype=bf16)
        sums = [pair_sums(g, ax, name="rs_" + k) for g, ax, k in zip((d_win, d_wglu[0], d_wbr, d_wout), big_axis, big)]
        own_sums[l] = [own for own, _ in sums]
        in_flight[l], token = send_pairs_start([pair for _, pair in sums], name=f"rs_start_{l}")
        d_h = mm(dz, win[l], bt=True, after=token, name="mm_in_dx", tk=2048)
        dx, d_nw = rowwise_vjp(ln_res_fn, [(s['x'], 0, D_MODEL)], [s['nw']], [], [(d_h, 0, D_MODEL), (dx, 0, D_MODEL)],
                               [(0, f32)], name="ln_bwd")
        small_g['norm_w'][l] = d_nw.reshape(D_MODEL)
    d_lb_raw = whole_vjp(lb_prep_fn, lb_rows, d_lbs, name="lb_prep_bwd")
    small_g['hg_lower_bounds'] = [r.reshape(W_MIX) for r in d_lb_raw]

    g_small = [jnp.stack(small_g[k]) for k in SMALL if k != 'final_norm_w'] + [d_fnw.reshape(D_MODEL)]
    shapes = [g.shape for g in g_small]
    small_in_flight, _ = gather_start([_pack(g_small)[0].astype(bf16)], (0,), dx, name="ag_small_start")
    res = {}

    arrived = [send_pairs_wait(in_flight[l], dx, name=f"rs_wait_{l}") for l in range(DEPTH)]
    for i, (k, tr) in enumerate((('w_in', 32), ('s5_w_glu', 32), ('w_branch', 128), ('w_out', 32))):
        shp = w[k].shape
        r3 = lambda a: a.reshape(DEPTH, -1, shp[-1])
        slots = [[own_sums[l][i], arrived[l][i]] for l in range(DEPTH)]
        outs = adamw(r3(w[k]), r3(m[k]), r3(v[k]), slots, name="adamw_" + k, tr=tr)
        for kind, buf in zip(('grad', 'delta', 'new_m', 'new_v'), outs):
            res[kind + '_' + k] = buf.reshape(shp)

    (g_all,) = gather_wait(small_in_flight, (0,), outs[0], name="ag_small_wait")
    g_all = g_all.reshape(N_DEV, -1, LANE)

    def local(d, k):
        return jnp.zeros(shapes[SMALL.index(k)], f32) if k == 'rg_conv_w' else d[k]
    packed = [_pack([local(d, k) for k in SMALL]) for d in (w, m, v)]
    outs = adamw(*packed, [[g_all]], name="adamw_small", tr=512)
    for kind, buf in zip(('grad', 'delta', 'new_m', 'new_v'), outs):
        for k, a in zip(SMALL, _unpack(buf, shapes)):
            res[kind + '_' + k] = a
    g_cw = lax.dynamic_slice_in_dim(res['grad_rg_conv_w'], me * LANE, LANE, axis=2)
    cw3 = lambda a: a.reshape(1, DEPTH * RG_CONV, LANE)
    outs = adamw(cw3(w['rg_conv_w']), cw3(m['rg_conv_w']), cw3(v['rg_conv_w']), [[cw3(g_cw)]], name="adamw_conv_w", tr=16)
    for kind, buf in zip(('grad', 'delta', 'new_m', 'new_v'), outs):
        res[kind + '_rg_conv_w'] = buf.reshape(DEPTH, RG_CONV, LANE)

    return (loss, dx[None]) + tuple(res[kind + '_' + k] for kind in ('grad', 'delta', 'new_m', 'new_v') for k in WEIGHTS)


def kernel(x, norm_w, w_in, s5_lambda_re, s5_lambda_im, s5_log_step, s5_b_re, s5_b_im, s5_c_re, s5_c_im, s5_d, s5_w_glu, s5_b_glu, rg_conv_w, rg_conv_b, rg_w_a, rg_b_a, rg_w_x, rg_b_x, rg_lambda, hg_lower_bounds, hg_norm_w, w_branch, w_out, final_norm_w, loss_target, m_norm_w, m_w_in, m_s5_lambda_re, m_s5_lambda_im, m_s5_log_step, m_s5_b_re, m_s5_b_im, m_s5_c_re, m_s5_c_im, m_s5_d, m_s5_w_glu, m_s5_b_glu, m_rg_conv_w, m_rg_conv_b, m_rg_w_a, m_rg_b_a, m_rg_w_x, m_rg_b_x, m_rg_lambda, m_hg_lower_bounds, m_hg_norm_w, m_w_branch, m_w_out, m_final_norm_w, v_norm_w, v_w_in, v_s5_lambda_re, v_s5_lambda_im, v_s5_log_step, v_s5_b_re, v_s5_b_im, v_s5_c_re, v_s5_c_im, v_s5_d, v_s5_w_glu, v_s5_b_glu, v_rg_conv_w, v_rg_conv_b, v_rg_w_a, v_rg_b_a, v_rg_w_x, v_rg_b_x, v_rg_lambda, v_hg_lower_bounds, v_hg_norm_w, v_w_branch, v_w_out, v_final_norm_w):
    w = dict(zip(WEIGHTS, (norm_w, w_in, s5_lambda_re, s5_lambda_im, s5_log_step, s5_b_re, s5_b_im, s5_c_re, s5_c_im, s5_d, s5_w_glu, s5_b_glu, rg_conv_w, rg_conv_b, rg_w_a, rg_b_a, rg_w_x, rg_b_x, rg_lambda, hg_lower_bounds, hg_norm_w, w_branch, w_out, final_norm_w)))
    m = dict(zip(WEIGHTS, (m_norm_w, m_w_in, m_s5_lambda_re, m_s5_lambda_im, m_s5_log_step, m_s5_b_re, m_s5_b_im, m_s5_c_re, m_s5_c_im, m_s5_d, m_s5_w_glu, m_s5_b_glu, m_rg_conv_w, m_rg_conv_b, m_rg_w_a, m_rg_b_a, m_rg_w_x, m_rg_b_x, m_rg_lambda, m_hg_lower_bounds, m_hg_norm_w, m_w_branch, m_w_out, m_final_norm_w)))
    v = dict(zip(WEIGHTS, (v_norm_w, v_w_in, v_s5_lambda_re, v_s5_lambda_im, v_s5_log_step, v_s5_b_re, v_s5_b_im, v_s5_c_re, v_s5_c_im, v_s5_d, v_s5_w_glu, v_s5_b_glu, v_rg_conv_w, v_rg_conv_b, v_rg_w_a, v_rg_b_a, v_rg_w_x, v_rg_b_x, v_rg_lambda, v_hg_lower_bounds, v_hg_norm_w, v_w_branch, v_w_out, v_final_norm_w)))
    return _step(x[0], loss_target[0], w, m, v)
```

```python
import functools
import math

import jax
import jax.numpy as jnp
from jax import lax
from jax.experimental import pallas as pl
from jax.experimental.pallas import tpu as pltpu

f32 = jnp.float32
bf16 = jnp.bfloat16

D_MODEL = 2048
W_MIX = 1024
DEPTH = 4
N_BRANCH = 3
N_IN = 8 * W_MIX + N_BRANCH * D_MODEL
S5_GROUPS, S5_STATE, S5_GROUP = 64, 64, 16
RG_BLOCKS, RG_BLOCK, RG_CONV, RG_C = 16, 64, 4, 8.0
HG_HEADS, HG_DK = 8, 128
HG_SUB = 16
EPS = 1e-6
ADAM_LR, ADAM_B1, ADAM_B2, ADAM_EPS, ADAM_WD, ADAM_STEP = 0.001, 0.9, 0.999, 1e-08, 0.01, 10

N_DEV = 8
LANE = 128
NCH = W_MIX // LANE
TM_CHUNK = 1024
VMEM_LIMIT = 56 * 1024 * 1024
MESH = pl.DeviceIdType.MESH
ANY = pl.BlockSpec(memory_space=pl.ANY)
HIGHEST = lax.Precision.HIGHEST

C_UA, C_GA, C_XB, C_GB, C_Q, C_F, C_I, C_GC, C_GATE = (W_MIX * k for k in range(9))


def _cparams(sem=None):
    return pltpu.CompilerParams(dimension_semantics=sem, vmem_limit_bytes=VMEM_LIMIT)


@jax.custom_vjp
def bdot(a, w):
    return jnp.dot(a.astype(bf16), w.astype(bf16), preferred_element_type=f32)


def _bdot_fwd(a, w):
    return bdot(a, w), (a, w)


def _bdot_bwd(res, g):
    a, w = res
    gb = g.astype(bf16)
    da = lax.dot_general(gb, w.astype(bf16), (((1,), (1,)), ((), ())), preferred_element_type=f32)
    dw = lax.dot_general(a.astype(bf16), gb, (((0,), (0,)), ((), ())), preferred_element_type=f32)
    return da, dw


bdot.defvjp(_bdot_fwd, _bdot_bwd)


def _blockmm(c, a):
    n = c.shape[0]
    return jnp.concatenate([jnp.dot(c, a[i:i + n], preferred_element_type=f32, precision=HIGHEST)
                            for i in range(0, a.shape[0], n)], axis=0)


@jax.custom_vjp
def cdot(c, ct, a):
    return _blockmm(c, a)


def _cdot_fwd(c, ct, a):
    return cdot(c, ct, a), (c, ct)


def _cdot_bwd(res, g):
    c, ct = res
    return jnp.zeros_like(c), jnp.zeros_like(ct), _blockmm(ct, g)


cdot.defvjp(_cdot_fwd, _cdot_bwd)


def mm(a, b, *, name, out_dtype=f32, add=None, after=None, bt=False, tm=512, tn=1024, tk=4096):
    m, k = a.shape
    n = b.shape[0] if bt else b.shape[1]
    tm, tn, tk = min(tm, m), min(tn, n), min(tk, k)
    assert m % tm == 0 and n % tn == 0 and k % tk == 0
    nk = k // tk
    dims = (((1,), (1,)), ((), ())) if bt else (((1,), (0,)), ((), ()))

    def body(*refs):
        a_ref, b_ref = refs[:2]
        r_ref = refs[2] if add is not None else None
        o_ref = refs[-1] if nk == 1 else refs[-2]
        part = lax.dot_general(a_ref[...], b_ref[...], dims, preferred_element_type=f32)
        if nk == 1:
            if add is not None:
                part = part + r_ref[...]
            o_ref[...] = part.astype(out_dtype)
            return
        acc_ref = refs[-1]
        kk = pl.program_id(2)

        @pl.when(kk == 0)
        def _():
            acc_ref[...] = part

        @pl.when(kk > 0)
        def _():
            acc_ref[...] = acc_ref[...] + part

        @pl.when(kk == nk - 1)
        def _():
            acc = acc_ref[...]
            if add is not None:
                acc = acc + r_ref[...]
            o_ref[...] = acc.astype(out_dtype)

    b_spec = pl.BlockSpec((tn, tk), lambda i, j, q: (j, q)) if bt else pl.BlockSpec((tk, tn), lambda i, j, q: (q, j))
    in_specs = [pl.BlockSpec((tm, tk), lambda i, j, q: (i, q)), b_spec]
    args = [a, b]
    if add is not None:
        in_specs.append(pl.BlockSpec((tm, tn), lambda i, j, q: (i, j)))
        args.append(add)
    if after is not None:
        in_specs.append(pl.BlockSpec(after.shape, lambda i, j, q: (0, 0)))
        args.append(after)
    return pl.pallas_call(
        body, name=name, grid=(m // tm, n // tn, nk), in_specs=in_specs,
        out_specs=pl.BlockSpec((tm, tn), lambda i, j, q: (i, j)),
        out_shape=jax.ShapeDtypeStruct((m, n), out_dtype),
        scratch_shapes=[] if nk == 1 else [pltpu.VMEM((tm, tn), f32)],
        compiler_params=_cparams(("parallel", "parallel", "arbitrary")),
    )(*args)


def _row_spec(tm, wc, col_off):
    base = col_off // wc
    assert col_off % wc == 0
    return pl.BlockSpec((tm, wc), lambda j, i: (i, base + j))


def _slab_spec(arr):
    r, c = arr.shape[1:]
    if arr.shape[0] == 1:
        return pl.BlockSpec((1, r, c), lambda j, i: (0, 0, 0))
    return pl.BlockSpec((1, r, c), lambda j, i: (j, 0, 0))


def rowwise(fn, rows, params, consts, outs, *, name, tm=256, ncol=1, rowid=False):
    t = rows[0][0].shape[0]
    tm = min(tm, t)
    nr, npar, nc, no = len(rows), len(params), len(consts), len(outs)

    def body(*refs):
        r = [refs[k][...].astype(f32) for k in range(nr)]
        p = [refs[nr + k][0] for k in range(npar + nc)]
        extra = ()
        if rowid:
            extra = (pl.program_id(1) * tm + lax.broadcasted_iota(jnp.int32, (tm, 1), 0),)
        res = fn(*extra, *r, *p)
        for k in range(no):
            refs[nr + npar + nc + k][...] = res[k].astype(outs[k][1])

    in_specs = [_row_spec(tm, w // ncol, off) for (_, off, w) in rows]
    in_specs += [_slab_spec(a) for a in list(params) + list(consts)]
    out_specs = [pl.BlockSpec((tm, w // ncol), lambda j, i: (i, j)) for (w, _) in outs]
    out_shape = [jax.ShapeDtypeStruct((t, w), dt) for (w, dt) in outs]
    return pl.pallas_call(
        body, name=name, grid=(ncol, t // tm), in_specs=in_specs, out_specs=out_specs, out_shape=out_shape,
        compiler_params=_cparams(("parallel", "parallel")),
    )(*[r[0] for r in rows], *params, *consts)


def rowwise_vjp(fn, rows, params, consts, cts, d_rows, *, name, tm=256, ncol=1, rowid=False, sum_primal=None):
    t = rows[0][0].shape[0]
    tm = min(tm, t)
    nr, npar, nc, nct, ndr = len(rows), len(params), len(consts), len(cts), len(d_rows)

    def body(*refs):
        i = pl.program_id(1)
        r = [refs[k][...].astype(f32) for k in range(nr)]
        p = [refs[nr + k][0] for k in range(npar)]
        c = [refs[nr + npar + k][0] for k in range(nc)]
        g = [refs[nr + npar + nc + k][...].astype(f32) for k in range(nct)]
        orefs = refs[nr + npar + nc + nct:]
        extra = ()
        if rowid:
            extra = (i * tm + lax.broadcasted_iota(jnp.int32, (tm, 1), 0),)
        res, vjp = jax.vjp(lambda *v: fn(*extra, *v, *c), *r, *p)
        grads = vjp(tuple(g))
        for k, (idx, dt) in enumerate(d_rows):
            orefs[k][...] = grads[idx].astype(dt)
        acc = [grads[nr + k] for k in range(npar)]
        if sum_primal is not None:
            acc.append(jnp.sum(res[sum_primal], axis=0, keepdims=True))

        @pl.when(i == 0)
        def _():
            for k, a in enumerate(acc):
                orefs[ndr + k][0] = a

        @pl.when(i > 0)
        def _():
            for k, a in enumerate(acc):
                orefs[ndr + k][0] = orefs[ndr + k][0] + a

    in_specs = [_row_spec(tm, w // ncol, off) for (_, off, w) in rows]
    in_specs += [_slab_spec(a) for a in list(params) + list(consts)]
    in_specs += [_row_spec(tm, w // ncol, off) for (_, off, w) in cts]
    out_specs = [pl.BlockSpec((tm, rows[idx][2] // ncol), lambda j, i: (i, j)) for (idx, _) in d_rows]
    out_shape = [jax.ShapeDtypeStruct((t, rows[idx][2]), dt) for (idx, dt) in d_rows]
    for a in params:
        out_specs.append(pl.BlockSpec((1,) + a.shape[1:], lambda j, i: (j, 0, 0)))
        out_shape.append(jax.ShapeDtypeStruct(a.shape, f32))
    if sum_primal is not None:
        w = cts[sum_primal][2]
        out_specs.append(pl.BlockSpec((1, 1, w // ncol), lambda j, i: (j, 0, 0)))
        out_shape.append(jax.ShapeDtypeStruct((ncol, 1, w // ncol), f32))
    return pl.pallas_call(
        body, name=name, grid=(ncol, t // tm), in_specs=in_specs, out_specs=out_specs, out_shape=out_shape,
        compiler_params=_cparams(("parallel", "arbitrary")),
    )(*[r[0] for r in rows], *params, *consts, *[c[0] for c in cts])


VM = pl.BlockSpec(memory_space=pltpu.VMEM)


def whole(fn, ins, outs, *, name):
    def body(*refs):
        res = fn(*[r[...] for r in refs[:len(ins)]])
        for k, o in enumerate(refs[len(ins):]):
            o[...] = res[k]
    return pl.pallas_call(body, name=name, in_specs=[VM] * len(ins), out_specs=[VM] * len(outs),
                          out_shape=[jax.ShapeDtypeStruct(s, f32) for s in outs],
                          compiler_params=_cparams())(*ins)


def whole_vjp(fn, ins, cts, *, name):
    n = len(ins)

    def body(*refs):
        _, vjp = jax.vjp(fn, *[r[...] for r in refs[:n]])
        grads = vjp(tuple(r[...] for r in refs[n:n + len(cts)]))
        for k, o in enumerate(refs[n + len(cts):]):
            o[...] = grads[k]
    return pl.pallas_call(body, name=name, in_specs=[VM] * (n + len(cts)), out_specs=[VM] * n,
                          out_shape=[jax.ShapeDtypeStruct(a.shape, f32) for a in ins],
                          compiler_params=_cparams())(*ins, *cts)


def ln_fn(x, w):
    return (x * lax.rsqrt(jnp.mean(x * x, axis=-1, keepdims=True) + EPS) * w,)


def ln_res_fn(x, w):
    return ln_fn(x, w)[0], x


def loss_fn(x, tgt, w):
    y = ln_fn(x, w)[0]
    return (0.5 * jnp.mean(jnp.square(y - tgt), axis=-1, keepdims=True),)


def s5_prep_fn(lam_re, lam_im, log_step, b_re, b_im):
    step = jnp.exp(log_step)
    mag = jnp.exp(lam_re * step)
    ang = lam_im * step
    abar_re = mag * jnp.cos(ang)
    abar_im = mag * jnp.sin(ang)
    num_re = abar_re - 1.0
    num_im = abar_im
    den = lam_re * lam_re + lam_im * lam_im
    coef_re = (num_re * lam_re + num_im * lam_im) / den
    coef_im = (num_im * lam_re - num_re * lam_im) / den
    bbar_re = coef_re * b_re - coef_im * b_im
    bbar_im = coef_re * b_im + coef_im * b_re
    return abar_re, abar_im, bbar_re, bbar_im


def lb_prep_fn(r0, r1, r2, r3):
    m = jnp.maximum(jnp.maximum(r0, r1), jnp.maximum(r2, r3))
    e0, e1, e2, e3 = jnp.exp(r0 - m), jnp.exp(r1 - m), jnp.exp(r2 - m), jnp.exp(r3 - m)
    s = e0 + e1 + e2 + e3
    p0, p1, p2, p3 = e0 / s, e1 / s, e2 / s, e3 / s
    c1 = p0 + p1
    c2 = c1 + p2
    c3 = c2 + p3
    return p0 - p0, c1 - p0, c2 - p0, c3 - p0


def s5_post_fn(yssm, u, ga, d, wglu, bglu):
    y = jax.nn.gelu(yssm + d * u)
    y = y * jax.nn.sigmoid(bdot(y, wglu) + bglu)
    return (y * jax.nn.silu(ga),)


def rg_gate_fn(tglob, xc, wa, ba, wx, bx, lam):
    r = jax.nn.sigmoid(bdot(xc, wa) + ba)
    i = jax.nn.sigmoid(bdot(xc, wx) + bx)
    log_a = -RG_C * r * jax.nn.softplus(-lam)
    a = jnp.exp(log_a)
    mult = jnp.sqrt(-jnp.tanh(log_a) * (a * a + 1.0))
    mult = jnp.where(tglob == 0, 1.0, mult)
    return a, mult * (i * xc)


def hg_pre_fn(q, fl, lb, tri, tri_t):
    f = lb + (1.0 - lb) * jax.nn.sigmoid(fl)
    return jax.nn.silu(q), 1.0 - f, cdot(tri, tri_t, jnp.log(f))


def branch_prep_fn(hb, gb, oc, gc, nw):
    yb = hb * jax.nn.silu(gb)
    on = oc * lax.rsqrt(jnp.mean(oc * oc, axis=-1, keepdims=True) + EPS) * nw
    return yb, on * jax.nn.silu(gc)


def merge_fn(b0, b1, b2, g0, g1, g2):
    return (jax.nn.sigmoid(g0) * b0 + jax.nn.sigmoid(g1) * b1 + jax.nn.sigmoid(g2) * b2,)


S5_TB = 512
S5_SC = 512


SEG = 8


def _shift(v, k, pos, period, reverse, fill):
    if reverse:
        return jnp.where(pos < period - k, pltpu.roll(v, v.shape[0] - k, 0), fill)
    return jnp.where(pos >= k, pltpu.roll(v, k, 0), fill)


def _cmul(ar, ai, br, bi):
    return ar * br - ai * bi, ar * bi + ai * br


def _edge_rows(v, first):
    r0 = 0 if first else SEG - 1
    return jnp.concatenate([v[r:r + 1, :] for r in range(r0, v.shape[0], SEG)], axis=0)


def _spread(s):
    return jnp.concatenate([jnp.broadcast_to(s[g:g + 1, :], (SEG, s.shape[1])) for g in range(s.shape[0])], axis=0)


def lti_scan(xr, xi, ar, ai, cr, ci, reverse=False):
    n = xr.shape[0]
    g = n // SEG
    sub = lax.broadcasted_iota(jnp.int32, (n, 1), 0) & (SEG - 1)
    sub8 = lax.broadcasted_iota(jnp.int32, (SEG, 1), 0)
    grow = lax.broadcasted_iota(jnp.int32, (g, 1), 0)
    pr, pi_ = ar, ai
    wr, wi = jnp.broadcast_to(ar, (SEG, ar.shape[1])), jnp.broadcast_to(ai, (SEG, ai.shape[1]))
    k = 1
    while k < SEG:
        tr, ti = _cmul(pr, pi_, _shift(xr, k, sub, SEG, reverse, 0.0), _shift(xi, k, sub, SEG, reverse, 0.0))
        xr, xi = xr + tr, xi + ti
        pr, pi_ = _cmul(pr, pi_, pr, pi_)
        wr, wi = _cmul(wr, wi, _shift(wr, k, sub8, SEG, reverse, 1.0), _shift(wi, k, sub8, SEG, reverse, 0.0))
        k *= 2
    first, last = (g - 1, 0) if reverse else (0, g - 1)
    jr, ji = _cmul(pr, pi_, cr, ci)
    sr = _edge_rows(xr, reverse) + jnp.where(grow == first, jr, 0.0)
    si = _edge_rows(xi, reverse) + jnp.where(grow == first, ji, 0.0)
    k = 1
    while k < g:
        tr, ti = _cmul(pr, pi_, _shift(sr, k, grow, g, reverse, 0.0), _shift(si, k, grow, g, reverse, 0.0))
        sr, si = sr + tr, si + ti
        pr, pi_ = _cmul(pr, pi_, pr, pi_)
        k *= 2
    er, ei = _spread(_shift(sr, 1, grow, g, reverse, cr)), _spread(_shift(si, 1, grow, g, reverse, ci))
    tr, ti = _cmul(jnp.tile(wr, (g, 1)), jnp.tile(wi, (g, 1)), er, ei)
    return xr + tr, xi + ti, sr[last:last + 1, :], si[last:last + 1, :]


def tv_scan(aa, bb, carry, reverse=False):
    n = aa.shape[0]
    row = lax.broadcasted_iota(jnp.int32, (n, 1), 0)
    k = 1
    while k < n:
        bb = bb + aa * _shift(bb, k, row, n, reverse, 0.0)
        aa = aa * _shift(aa, k, row, n, reverse, 1.0)
        k *= 2
    h = bb + aa * carry
    last = 0 if reverse else n - 1
    return h, h[last:last + 1, :]


def s5_scan_fwd(z, bd_re, bd_im, cd_re, cd_im, a_re, a_im):
    t = z.shape[0]
    tb = min(S5_TB, t)

    def body(u_ref, bre, bim, cre, cim, are, aim, y_ref, xre_ref, xim_ref, car_re, car_im):
        @pl.when(pl.program_id(1) == 0)
        def _():
            car_re[...] = jnp.zeros_like(car_re)
            car_im[...] = jnp.zeros_like(car_im)

        u = u_ref[...].astype(bf16)
        xr, xi, car_re[...], car_im[...] = lti_scan(
            jnp.dot(u, bre[0], preferred_element_type=f32), jnp.dot(u, bim[0], preferred_element_type=f32),
            are[0], aim[0], car_re[...], car_im[...])
        xre_ref[...] = xr
        xim_ref[...] = xi
        y_ref[...] = (jnp.dot(xr.astype(bf16), cre[0], preferred_element_type=f32)
                      - jnp.dot(xi.astype(bf16), cim[0], preferred_element_type=f32))

    chunk = lambda r, c: pl.BlockSpec((1, r, c), lambda j, i: (j, 0, 0))
    return pl.pallas_call(
        body, name="s5_scan_fwd", grid=(NCH, t // tb),
        in_specs=[pl.BlockSpec((tb, LANE), lambda j, i: (i, C_UA // LANE + j)),
                  chunk(LANE, S5_SC), chunk(LANE, S5_SC), chunk(S5_SC, LANE), chunk(S5_SC, LANE),
                  chunk(1, S5_SC), chunk(1, S5_SC)],
        out_specs=[pl.BlockSpec((tb, LANE), lambda j, i: (i, j)),
                   pl.BlockSpec((tb, S5_SC), lambda j, i: (i, j)),
                   pl.BlockSpec((tb, S5_SC), lambda j, i: (i, j))],
        out_shape=[jax.ShapeDtypeStruct((t, W_MIX), f32),
                   jax.ShapeDtypeStruct((t, NCH * S5_SC), f32),
                   jax.ShapeDtypeStruct((t, NCH * S5_SC), f32)],
        scratch_shapes=[pltpu.VMEM((1, S5_SC), f32)] * 2,
        compiler_params=_cparams(("parallel", "arbitrary")),
    )(z, bd_re.astype(bf16), bd_im.astype(bf16), cd_re.astype(bf16), cd_im.astype(bf16), a_re, a_im)


def s5_scan_bwd(dy, du1, z, xre, xim, bd_re, bd_im, cd_re, cd_im, a_re, a_im):
    t = z.shape[0]
    tb = min(S5_TB, t)
    nt = t // tb

    def body(dy_ref, du1_ref, u_ref, xre_ref, xim_ref, hre_ref, him_ref, bre, bim, cre, cim, are, aim,
             du_ref, dbre, dbim, dcre, dcim, dare, daim, car_re, car_im):
        step = pl.program_id(1)
        tt = nt - 1 - step

        @pl.when(step == 0)
        def _():
            car_re[...] = jnp.zeros_like(car_re)
            car_im[...] = jnp.zeros_like(car_im)

        nt_dims = (((1,), (1,)), ((), ()))
        tn_dims = (((0,), (0,)), ((), ()))
        dyb = dy_ref[...].astype(bf16)
        row = lax.broadcasted_iota(jnp.int32, (tb, 1), 0)
        xr, xi = xre_ref[...], xim_ref[...]
        ar, ai = are[0], aim[0]
        lr, li, car_re[...], car_im[...] = lti_scan(
            lax.dot_general(dyb, cre[0], nt_dims, preferred_element_type=f32),
            -lax.dot_general(dyb, cim[0], nt_dims, preferred_element_type=f32),
            ar, -ai, car_re[...], car_im[...], reverse=True)
        lrb, lib = lr.astype(bf16), li.astype(bf16)
        ub = u_ref[...].astype(bf16)
        du = (lax.dot_general(lrb, bre[0], nt_dims, preferred_element_type=f32)
              + lax.dot_general(lib, bim[0], nt_dims, preferred_element_type=f32))
        du_ref[...] = (du + du1_ref[...].astype(f32)).astype(du_ref.dtype)
        live = (tt > 0).astype(f32)
        xpr = jnp.where(row == 0, hre_ref[7:8, :] * live, pltpu.roll(xr, 1, 0))
        xpi = jnp.where(row == 0, him_ref[7:8, :] * live, pltpu.roll(xi, 1, 0))
        acc = [
            lax.dot_general(ub, lrb, tn_dims, preferred_element_type=f32),
            lax.dot_general(ub, lib, tn_dims, preferred_element_type=f32),
            lax.dot_general(xr.astype(bf16), dyb, tn_dims, preferred_element_type=f32),
            -lax.dot_general(xi.astype(bf16), dyb, tn_dims, preferred_element_type=f32),
            jnp.sum(lr * xpr + li * xpi, axis=0, keepdims=True),
            jnp.sum(li * xpr - lr * xpi, axis=0, keepdims=True),
        ]
        outs = [dbre, dbim, dcre, dcim, dare, daim]

        @pl.when(step == 0)
        def _():
            for o, a in zip(outs, acc):
                o[0] = a

        @pl.when(step > 0)
        def _():
            for o, a in zip(outs, acc):
                o[0] = o[0] + a

    chunk = lambda r, c: pl.BlockSpec((1, r, c), lambda j, i: (j, 0, 0))
    rev = lambda w, base=0: pl.BlockSpec((tb, w), lambda j, i: (nt - 1 - i, base + j))
    halo = pl.BlockSpec((8, S5_SC), lambda j, i: (jnp.maximum((nt - 1 - i) * (tb // 8) - 1, 0), j))
    return pl.pallas_call(
        body, name="s5_scan_bwd", grid=(NCH, nt),
        in_specs=[rev(LANE), rev(LANE), rev(LANE, C_UA // LANE), rev(S5_SC), rev(S5_SC), halo, halo,
                  chunk(LANE, S5_SC), chunk(LANE, S5_SC), chunk(S5_SC, LANE), chunk(S5_SC, LANE),
                  chunk(1, S5_SC), chunk(1, S5_SC)],
        out_specs=[rev(LANE), chunk(LANE, S5_SC), chunk(LANE, S5_SC), chunk(S5_SC, LANE), chunk(S5_SC, LANE),
                   chunk(1, S5_SC), chunk(1, S5_SC)],
        out_shape=[jax.ShapeDtypeStruct((t, W_MIX), bf16),
                   jax.ShapeDtypeStruct((NCH, LANE, S5_SC), f32), jax.ShapeDtypeStruct((NCH, LANE, S5_SC), f32),
                   jax.ShapeDtypeStruct((NCH, S5_SC, LANE), f32), jax.ShapeDtypeStruct((NCH, S5_SC, LANE), f32),
                   jax.ShapeDtypeStruct((NCH, 1, S5_SC), f32), jax.ShapeDtypeStruct((NCH, 1, S5_SC), f32)],
        scratch_shapes=[pltpu.VMEM((1, S5_SC), f32)] * 2,
        compiler_params=_cparams(("parallel", "arbitrary")),
    )(dy, du1, z, xre, xim, xre, xim, bd_re.astype(bf16), bd_im.astype(bf16), cd_re.astype(bf16),
      cd_im.astype(bf16), a_re, a_im)


RG_TB = 512


def rg_conv_fwd(z, cw, cb):
    t = z.shape[0]
    tb = min(RG_TB, t)

    def body(x_ref, h_ref, cw_ref, cb_ref, o_ref):
        live = (pl.program_id(1) > 0).astype(f32)
        ext = jnp.concatenate([h_ref[...] * live, x_ref[...]], axis=0)
        w = cw_ref[0]
        acc = cb_ref[0] + w[3:4, :] * ext[8:, :]
        for k in range(3):
            acc = acc + w[k:k + 1, :] * pltpu.roll(ext, 3 - k, 0)[8:, :]
        o_ref[...] = acc

    base = C_XB // LANE
    chunk = lambda r: pl.BlockSpec((1, r, LANE), lambda j, i: (j, 0, 0))
    return pl.pallas_call(
        body, name="rg_conv_fwd", grid=(NCH, t // tb),
        in_specs=[pl.BlockSpec((tb, LANE), lambda j, i: (i, base + j)),
                  pl.BlockSpec((8, LANE), lambda j, i: (jnp.maximum(i * (tb // 8) - 1, 0), base + j)),
                  chunk(RG_CONV), chunk(1)],
        out_specs=pl.BlockSpec((tb, LANE), lambda j, i: (i, j)),
        out_shape=jax.ShapeDtypeStruct((t, W_MIX), f32),
        compiler_params=_cparams(("parallel", "parallel")),
    )(z, z, cw, cb)


def rg_conv_bwd(dxc, z, cw):
    t = z.shape[0]
    tb = min(RG_TB, t)
    nt = t // tb

    def body(g_ref, gn_ref, x_ref, h_ref, cw_ref, dx_ref, dcw_ref, dcb_ref):
        i = pl.program_id(1)
        g = g_ref[...]
        gext = jnp.concatenate([g, gn_ref[...] * (i < nt - 1).astype(f32)], axis=0)
        xext = jnp.concatenate([h_ref[...] * (i > 0).astype(f32), x_ref[...]], axis=0)
        w = cw_ref[0]
        dx = w[3:4, :] * g
        rows = [None] * RG_CONV
        rows[3] = jnp.sum(g * xext[8:, :], axis=0, keepdims=True)
        for k in range(3):
            s = 3 - k
            dx = dx + w[k:k + 1, :] * pltpu.roll(gext, tb + 8 - s, 0)[:tb, :]
            rows[k] = jnp.sum(g * pltpu.roll(xext, s, 0)[8:, :], axis=0, keepdims=True)
        dx_ref[...] = dx.astype(dx_ref.dtype)
        dcw = jnp.concatenate(rows, axis=0)
        dcb = jnp.sum(g, axis=0, keepdims=True)

        @pl.when(i == 0)
        def _():
            dcw_ref[0] = dcw
            dcb_ref[0] = dcb

        @pl.when(i > 0)
        def _():
            dcw_ref[0] = dcw_ref[0] + dcw
            dcb_ref[0] = dcb_ref[0] + dcb

    base = C_XB // LANE
    chunk = lambda r: pl.BlockSpec((1, r, LANE), lambda j, i: (j, 0, 0))
    return pl.pallas_call(
        body, name="rg_conv_bwd", grid=(NCH, nt),
        in_specs=[pl.BlockSpec((tb, LANE), lambda j, i: (i, j)),
                  pl.BlockSpec((8, LANE), lambda j, i: (jnp.minimum((i + 1) * (tb // 8), t // 8 - 1), j)),
                  pl.BlockSpec((tb, LANE), lambda j, i: (i, base + j)),
                  pl.BlockSpec((8, LANE), lambda j, i: (jnp.maximum(i * (tb // 8) - 1, 0), base + j)),
                  chunk(RG_CONV)],
        out_specs=[pl.BlockSpec((tb, LANE), lambda j, i: (i, j)), chunk(RG_CONV), chunk(1)],
        out_shape=[jax.ShapeDtypeStruct((t, W_MIX), bf16), jax.ShapeDtypeStruct((NCH, RG_CONV, LANE), f32),
                   jax.ShapeDtypeStruct((NCH, 1, LANE), f32)],
        compiler_params=_cparams(("parallel", "arbitrary")),
    )(dxc, dxc, z, z, cw)


def rg_scan_fwd(a, b):
    t = a.shape[0]
    tb = min(RG_TB, t)

    def body(a_ref, b_ref, h_ref, car):
        @pl.when(pl.program_id(1) == 0)
        def _():
            car[...] = jnp.zeros_like(car)

        h_ref[...], car[...] = tv_scan(a_ref[...], b_ref[...], car[...])

    spec = pl.BlockSpec((tb, LANE), lambda j, i: (i, j))
    return pl.pallas_call(
        body, name="rg_scan_fwd", grid=(NCH, t // tb), in_specs=[spec, spec], out_specs=spec,
        out_shape=jax.ShapeDtypeStruct((t, W_MIX), f32), scratch_shapes=[pltpu.VMEM((1, LANE), f32)],
        compiler_params=_cparams(("parallel", "arbitrary")),
    )(a, b)


def rg_scan_bwd(dh, a, h):
    t = a.shape[0]
    tb = min(RG_TB, t)
    nt = t // tb

    def body(g_ref, a_ref, an_ref, h_ref, hp_ref, da_ref, db_ref, car):
        step = pl.program_id(1)
        tt = nt - 1 - step

        @pl.when(step == 0)
        def _():
            car[...] = jnp.zeros_like(car)

        row = lax.broadcasted_iota(jnp.int32, (tb, 1), 0)
        an = an_ref[0:1, :] * (tt < nt - 1).astype(f32)
        aa = jnp.where(row == tb - 1, an, pltpu.roll(a_ref[...], tb - 1, 0))
        lam, car[...] = tv_scan(aa, g_ref[...], car[...], reverse=True)
        hp = jnp.where(row == 0, hp_ref[7:8, :] * (tt > 0).astype(f32), pltpu.roll(h_ref[...], 1, 0))
        da_ref[...] = lam * hp
        db_ref[...] = lam

    rev = pl.BlockSpec((tb, LANE), lambda j, i: (nt - 1 - i, j))
    nxt = pl.BlockSpec((8, LANE), lambda j, i: (jnp.minimum((nt - i) * (tb // 8), t // 8 - 1), j))
    prv = pl.BlockSpec((8, LANE), lambda j, i: (jnp.maximum((nt - 1 - i) * (tb // 8) - 1, 0), j))
    return pl.pallas_call(
        body, name="rg_scan_bwd", grid=(NCH, nt), in_specs=[rev, rev, nxt, rev, prv], out_specs=[rev, rev],
        out_shape=[jax.ShapeDtypeStruct((t, W_MIX), f32)] * 2, scratch_shapes=[pltpu.VMEM((1, LANE), f32)],
        compiler_params=_cparams(("parallel", "arbitrary")),
    )(dh, a, a, h, h)


HG_TB = 256
HALF = HG_SUB // 2


def _heads(v):
    return jnp.stack([v[:, LANE * h:LANE * (h + 1)] for h in range(HG_HEADS)])


def _unheads(v):
    return jnp.concatenate([v[h] for h in range(HG_HEADS)], axis=-1)


def _bmm(eq, a, b):
    return jnp.einsum(eq, a.astype(bf16), b.astype(bf16), preferred_element_type=f32)


def hg_chunk_fwd(qs, kk, gcum, z):
    t = qs.shape[0]
    tb = min(HG_TB, t)
    nc = tb // HG_SUB

    def body(q_ref, k_ref, g_ref, v_ref, o_ref, sall_ref, st_ref):
        @pl.when(pl.program_id(0) == 0)
        def _():
            st_ref[...] = jnp.zeros_like(st_ref)

        ri = lax.broadcasted_iota(jnp.int32, (1, HALF, 1), 1)

        def chunk(c, carry):
            rows = pl.ds(pl.multiple_of(c * HG_SUB, HG_SUB), HG_SUB)
            q, k, g, v = _heads(q_ref[rows, :]), _heads(k_ref[rows, :]), _heads(g_ref[rows, :]), _heads(v_ref[rows, :])
            st = st_ref[...]
            sall_ref[c] = st
            o = _bmm('htk,hvk->htv', q * jnp.exp(g), st)
            halves = [[q[:, :HALF], g[:, :HALF], o[:, :HALF]], [q[:, HALF:], g[:, HALF:], o[:, HALF:]]]
            for s in range(HG_SUB):
                grow, krow, vrow = g[:, s:s + 1, :], k[:, s:s + 1, :], v[:, s:s + 1, :]
                for h in range(s // HALF, 2):
                    qh, gh, oh = halves[h]
                    p = jnp.exp(jnp.minimum(gh - grow, 0.0))
                    if s // HALF == h:
                        p = jnp.where(ri >= s - h * HALF, p, 0.0)
                    halves[h][2] = oh + jnp.sum(qh * krow * p, axis=-1, keepdims=True) * vrow
            o = jnp.concatenate([halves[0][2], halves[1][2]], axis=1)
            gl = g[:, HG_SUB - 1:HG_SUB, :]
            st_ref[...] = st * jnp.exp(gl) + _bmm('htv,htk->hvk', v, k * jnp.exp(gl - g))
            o_ref[rows, :] = _unheads(o)
            return carry

        lax.fori_loop(0, nc, chunk, 0)

    spec = lambda base=0: pl.BlockSpec((tb, W_MIX), lambda i: (i, base))
    return pl.pallas_call(
        body, name="hg_chunk_fwd", grid=(t // tb,),
        in_specs=[spec(), spec(), spec(), spec(C_I // W_MIX)],
        out_specs=[spec(), pl.BlockSpec((nc, HG_HEADS, HG_DK, HG_DK), lambda i: (i, 0, 0, 0))],
        out_shape=[jax.ShapeDtypeStruct((t, W_MIX), f32),
                   jax.ShapeDtypeStruct((t // HG_SUB, HG_HEADS, HG_DK, HG_DK), f32)],
        scratch_shapes=[pltpu.VMEM((HG_HEADS, HG_DK, HG_DK), f32)],
        compiler_params=_cparams(("arbitrary",)),
    )(qs, kk, gcum, z)


def hg_chunk_bwd(do, qs, kk, gcum, z, sall):
    t = qs.shape[0]
    tb = min(HG_TB, t)
    nc = tb // HG_SUB
    nt = t // tb

    def body(do_ref, q_ref, k_ref, g_ref, v_ref, sall_ref, dq_ref, dk_ref, dg_ref, dv_ref, dst_ref):
        @pl.when(pl.program_id(0) == 0)
        def _():
            dst_ref[...] = jnp.zeros_like(dst_ref)

        ri = lax.broadcasted_iota(jnp.int32, (1, HALF, 1), 1)
        ri_chunk = lax.broadcasted_iota(jnp.int32, (1, HG_SUB, 1), 1)

        def chunk(cc, carry):
            c = nc - 1 - cc
            rows = pl.ds(pl.multiple_of(c * HG_SUB, HG_SUB), HG_SUB)
            q, k, g, v = _heads(q_ref[rows, :]), _heads(k_ref[rows, :]), _heads(g_ref[rows, :]), _heads(v_ref[rows, :])
            d_o = _heads(do_ref[rows, :])
            st = sall_ref[c]
            dsn = dst_ref[...]
            eg = jnp.exp(g)
            qe = q * eg
            gl = g[:, HG_SUB - 1:HG_SUB, :]
            egl = jnp.exp(gl)
            dec = jnp.exp(gl - g)
            kd = k * dec
            dqe = _bmm('htv,hvk->htk', d_o, st)
            dst_ref[...] = _bmm('htv,htk->hvk', d_o, qe) + dsn * egl
            dgl_dec = jnp.sum(dsn * st, axis=1, keepdims=True) * egl
            dv = _bmm('htk,hvk->htv', kd, dsn)
            dkd = _bmm('htv,hvk->htk', v, dsn)
            tq, tg, tdo = [q[:, :HALF], q[:, HALF:]], [g[:, :HALF], g[:, HALF:]], [d_o[:, :HALF], d_o[:, HALF:]]
            a1 = [jnp.zeros_like(tq[0]), jnp.zeros_like(tq[1])]
            a2 = [jnp.zeros_like(tq[0]), jnp.zeros_like(tq[1])]
            dvh = [dv[:, :HALF], dv[:, HALF:]]
            for s in range(HG_SUB):
                grow, krow, vrow = g[:, s:s + 1, :], k[:, s:s + 1, :], v[:, s:s + 1, :]
                sh, sr = s // HALF, s % HALF
                dv_s, a2_s = 0.0, 0.0
                for h in range(sh, 2):
                    p = jnp.exp(jnp.minimum(tg[h] - grow, 0.0))
                    if sh == h:
                        p = jnp.where(ri >= sr, p, 0.0)
                    col = jnp.sum(tq[h] * krow * p, axis=-1, keepdims=True)
                    t1 = jnp.sum(tdo[h] * vrow, axis=-1, keepdims=True) * p
                    a1[h] = a1[h] + t1 * krow
                    dv_s = dv_s + jnp.sum(col * tdo[h], axis=1, keepdims=True)
                    a2_s = a2_s + jnp.sum(t1 * tq[h], axis=1, keepdims=True)
                dvh[sh] = jnp.where(ri == sr, dvh[sh] + dv_s, dvh[sh])
                a2[sh] = jnp.where(ri == sr, a2_s, a2[sh])
            a1, a2 = jnp.concatenate(a1, axis=1), jnp.concatenate(a2, axis=1)
            dv = jnp.concatenate(dvh, axis=1)
            dgl = jnp.sum(dkd * kd, axis=1, keepdims=True) + dgl_dec
            dg = dqe * qe + q * a1 - k * a2 - dkd * kd
            dg = jnp.where(ri_chunk == HG_SUB - 1, dg + dgl, dg)
            dq_ref[rows, :] = _unheads(dqe * eg + a1)
            dk_ref[rows, :] = _unheads(dkd * dec + a2)
            dg_ref[rows, :] = _unheads(dg)
            dv_ref[rows, :] = _unheads(dv).astype(dv_ref.dtype)
            return carry

        lax.fori_loop(0, nc, chunk, 0)

    spec = lambda base=0: pl.BlockSpec((tb, W_MIX), lambda i: (nt - 1 - i, base))
    return pl.pallas_call(
        body, name="hg_chunk_bwd", grid=(nt,),
        in_specs=[spec(), spec(), spec(), spec(), spec(C_I // W_MIX),
                  pl.BlockSpec((nc, HG_HEADS, HG_DK, HG_DK), lambda i: (nt - 1 - i, 0, 0, 0))],
        out_specs=[spec(), spec(), spec(), spec()],
        out_shape=[jax.ShapeDtypeStruct((t, W_MIX), f32)] * 3 + [jax.ShapeDtypeStruct((t, W_MIX), bf16)],
        scratch_shapes=[pltpu.VMEM((HG_HEADS, HG_DK, HG_DK), f32)],
        compiler_params=_cparams(("arbitrary",)),
    )(do, qs, kk, gcum, z, sall)


def adamw(w, m, v, slots, *, name, tr):
    nl, r, c = w.shape
    tr = min(tr, r)
    flat = [a for per_layer in slots for a in per_layer]
    c1 = 1.0 / (1.0 - ADAM_B1 ** ADAM_STEP)
    c2 = 1.0 / (1.0 - ADAM_B2 ** ADAM_STEP)

    def body(*refs):
        w_ref, m_ref, v_ref = refs[:3]
        s_refs = list(refs[3:3 + len(flat)])
        g_ref, d_ref, mo_ref, vo_ref = refs[3 + len(flat):]
        for l in range(nl):
            parts = [s_refs.pop(0) for _ in slots[l]]
            g = None
            for p in parts:
                for s in range(p.shape[0]):
                    term = p[s].astype(f32)
                    g = term if g is None else g + term
            mn = ADAM_B1 * m_ref[l] + (1.0 - ADAM_B1) * g
            vn = ADAM_B2 * v_ref[l] + (1.0 - ADAM_B2) * (g * g)
            g_ref[l] = g
            mo_ref[l] = mn
            vo_ref[l] = vn
            d_ref[l] = -ADAM_LR * ((mn * c1) / (jnp.sqrt(vn * c2) + ADAM_EPS) + ADAM_WD * w_ref[l])

    full = pl.BlockSpec((nl, tr, c), lambda i: (0, i, 0))
    slot = [pl.BlockSpec((a.shape[0], tr, c), lambda i: (0, i, 0)) for a in flat]
    return pl.pallas_call(
        body, name=name, grid=(r // tr,), in_specs=[full] * 3 + slot, out_specs=[full] * 4,
        out_shape=[jax.ShapeDtypeStruct(w.shape, f32)] * 4, compiler_params=_cparams(("parallel",)),
    )(w, m, v, *flat)


def _slab(ref, axis, idx, n):
    return ref.at[tuple([slice(None)] * axis + [pl.ds(idx * n, n)])]


def all_gather(x, axis, *, name):
    n = x.shape[axis]
    out_shape = x.shape[:axis] + (N_DEV * n,) + x.shape[axis + 1:]

    def body(x_ref, out_ref, send_sems, recv_sems, local_sem):
        xx, yy, cc = lax.axis_index("x"), lax.axis_index("y"), lax.axis_index("c")
        me, sibling = (xx, yy, cc), (xx, yy, 1 - cc)
        chips = [(1 - xx, yy), (xx, 1 - yy), (1 - xx, 1 - yy)]

        def slab(px, py, pc):
            return _slab(out_ref, axis, 4 * px + 2 * py + pc, n)

        def copy(k, block, to, src=None):
            return pltpu.make_async_remote_copy(
                src_ref=slab(*block) if src is None else src, dst_ref=slab(*block),
                send_sem=send_sems.at[k], recv_sem=recv_sems.at[k], device_id=to, device_id_type=MESH)

        mine = pltpu.make_async_copy(x_ref, slab(*me), local_sem)
        mine.start()
        first = [copy(0, me, sibling, src=x_ref)]
        first += [copy(1 + j, me, (*chip, cc), src=x_ref) for j, chip in enumerate(chips)]
        for cp in first:
            cp.start()
        passed = [copy(4 + j, (*chip, cc), sibling) for j, chip in enumerate(chips)]
        for j, chip in enumerate(chips):
            copy(1 + j, (*chip, cc), me).wait_recv()
            passed[j].start()
        copy(0, sibling, me).wait_recv()
        for j, chip in enumerate(chips):
            copy(4 + j, (*chip, 1 - cc), me).wait_recv()
        for cp in first + passed:
            cp.wait_send()
        mine.wait()

    return pl.pallas_call(
        body, name=name, out_shape=jax.ShapeDtypeStruct(out_shape, x.dtype), in_specs=[ANY], out_specs=ANY,
        scratch_shapes=[pltpu.SemaphoreType.DMA((7,)), pltpu.SemaphoreType.DMA((7,)), pltpu.SemaphoreType.DMA],
    )(x)


N_CHIP = 4


HBM = pl.BlockSpec(memory_space=pltpu.HBM)
SEM = pl.BlockSpec(memory_space=pltpu.SEMAPHORE)
EFFECT = pltpu.SideEffectType.DATAFLOW_SIDE_EFFECTING
TOKEN = jax.ShapeDtypeStruct((8, LANE), f32)


def _in_hbm(a):
    return pltpu.with_memory_space_constraint(a, pltpu.HBM)


def pair_sums(g, axis, *, name):
    n = g.shape[axis] // N_DEV
    slab_shape = g.shape[:axis] + (n,) + g.shape[axis + 1:]
    cols = slab_shape[-1]
    rows = math.prod(slab_shape[:-1])
    col_slabs = axis == g.ndim - 1
    assert col_slabs or (axis == 0 and g.ndim == 2)

    def swap_body(g_ref, got_ref, send_sems, recv_sems):
        xx, yy, cc = lax.axis_index("x"), lax.axis_index("y"), lax.axis_index("c")
        copies = [pltpu.make_async_remote_copy(
            src_ref=_slab(g_ref, axis, 2 * q + 1 - cc, n), dst_ref=got_ref.at[q],
            send_sem=send_sems.at[q], recv_sem=recv_sems.at[q], device_id=(xx, yy, 1 - cc), device_id_type=MESH)
            for q in range(N_CHIP)]
        for cp in copies:
            cp.start()
        for cp in copies:
            cp.wait()

    got = pl.pallas_call(
        swap_body, name=name + "_swap", out_shape=jax.ShapeDtypeStruct((N_CHIP,) + slab_shape, g.dtype),
        in_specs=[ANY], out_specs=ANY, scratch_shapes=[pltpu.SemaphoreType.DMA((N_CHIP,))] * 2,
    )(g)

    tr = min(256, rows)

    def add_body(a0_ref, a1_ref, b_ref, pair_ref, own_ref):
        xx, yy, cc = lax.axis_index("x"), lax.axis_index("y"), lax.axis_index("c")
        mine = jnp.where(cc == 0, a0_ref[...], a1_ref[...])
        s = (mine.astype(f32) + b_ref[0].astype(f32)).astype(bf16)
        pair_ref[0] = s

        @pl.when(pl.program_id(1) == 2 * xx + yy)
        def _():
            own_ref[0] = s

    if col_slabs:
        a_spec = lambda c: pl.BlockSpec((tr, cols), lambda i, q: (i, 2 * q + c))
    else:
        a_spec = lambda c: pl.BlockSpec((tr, cols), lambda i, q: ((2 * q + c) * (n // tr) + i, 0))
    by_chip = pl.BlockSpec((1, tr, cols), lambda i, q: (q, i, 0))
    g2 = g.reshape(-1, g.shape[-1])
    pair, own = pl.pallas_call(
        add_body, name=name + "_add", grid=(rows // tr, N_CHIP), in_specs=[a_spec(0), a_spec(1), by_chip],
        out_specs=[by_chip, pl.BlockSpec((1, tr, cols), lambda i, q: (0, i, 0))],
        out_shape=[jax.ShapeDtypeStruct((N_CHIP, rows, cols), bf16), jax.ShapeDtypeStruct((1, rows, cols), bf16)],
        compiler_params=_cparams(("parallel", "arbitrary")),
    )(g2, g2, got.reshape(N_CHIP, rows, cols))
    return own, pair


def _send_copies(p_refs, land_refs, send_sems, recv_sems):
    xx, yy, cc = lax.axis_index("x"), lax.axis_index("y"), lax.axis_index("c")
    copies = []
    for t, (p, land) in enumerate(zip(p_refs, land_refs)):
        for k in range(1, N_CHIP):
            px = 1 - xx if k & 2 else xx
            py = 1 - yy if k & 1 else yy
            s = (N_CHIP - 1) * t + k - 1
            copies.append(pltpu.make_async_remote_copy(
                src_ref=p.at[2 * px + py], dst_ref=land.at[k - 1], send_sem=send_sems.at[s], recv_sem=recv_sems.at[s],
                device_id=(px, py, cc), device_id_type=MESH))
    return copies


def send_pairs_start(pairs, after, *, name):
    nt = len(pairs)
    lands = [lax.empty((N_CHIP - 1,) + p.shape[1:], p.dtype) for p in pairs]

    def body(*refs):
        p_refs, land_refs = refs[:nt], refs[nt:2 * nt]
        send_sems, recv_sems = refs[2 * nt + 1], refs[2 * nt + 2]
        token = refs[-1]
        for cp in _send_copies(p_refs, land_refs, send_sems, recv_sems):
            cp.start()
        token[...] = jnp.zeros_like(token)

    nsem = (N_CHIP - 1) * nt
    outs = pl.pallas_call(
        body, name=name,
        out_shape=(pltpu.SemaphoreType.DMA((nsem,)), pltpu.SemaphoreType.DMA((nsem,)))
        + tuple(pltpu.HBM(a.shape, a.dtype) for a in list(pairs) + lands) + (TOKEN,),
        in_specs=[HBM] * (2 * nt) + [ANY], out_specs=(SEM, SEM) + (HBM,) * (2 * nt) + (VM,),
        input_output_aliases={i: 2 + i for i in range(2 * nt)},
        compiler_params=pltpu.CompilerParams(has_side_effects=EFFECT),
    )(*[_in_hbm(a) for a in list(pairs) + lands], after)
    return outs[:-1], outs[-1]


def send_pairs_wait(handles, after, *, name):
    send_sems, recv_sems = handles[0], handles[1]
    bufs = handles[2:]
    nt = len(bufs) // 2

    def body(*refs):
        p_refs, land_refs = refs[:nt], refs[nt:2 * nt]
        send_sems, recv_sems = refs[2 * nt], refs[2 * nt + 1]
        for cp in _send_copies(p_refs, land_refs, send_sems, recv_sems):
            cp.wait_send()
            cp.wait_recv()

    outs = pl.pallas_call(
        body, name=name, out_shape=tuple(pltpu.HBM(a.shape, a.dtype) for a in bufs),
        in_specs=[HBM] * (2 * nt) + [SEM, SEM, ANY], out_specs=(HBM,) * (2 * nt),
        input_output_aliases={i: i for i in range(2 * nt)},
        compiler_params=pltpu.CompilerParams(has_side_effects=EFFECT),
    )(*bufs, send_sems, recv_sems, after)
    return outs[nt:]


def _gather_copies(x_refs, land_refs, axes, send_sems, recv_sems):
    xx, yy, cc = lax.axis_index("x"), lax.axis_index("y"), lax.axis_index("c")
    me = 4 * xx + 2 * yy + cc
    copies = []
    for t, (x_ref, land, axis) in enumerate(zip(x_refs, land_refs, axes)):
        n = x_ref.shape[axis]
        for k in range(1, N_DEV):
            px = 1 - xx if k & 4 else xx
            py = 1 - yy if k & 2 else yy
            pc = 1 - cc if k & 1 else cc
            s = (N_DEV - 1) * t + k - 1
            copies.append(pltpu.make_async_remote_copy(
                src_ref=x_ref, dst_ref=_slab(land, axis, me, n), send_sem=send_sems.at[s], recv_sem=recv_sems.at[s],
                device_id=(px, py, pc), device_id_type=MESH))
    return copies


def gather_start(xs, axes, after, *, name):
    nt = len(xs)
    me = 4 * lax.axis_index("x") + 2 * lax.axis_index("y") + lax.axis_index("c")
    lands = []
    for x, axis in zip(xs, axes):
        full = lax.empty(x.shape[:axis] + (N_DEV * x.shape[axis],) + x.shape[axis + 1:], x.dtype)
        lands.append(lax.dynamic_update_slice_in_dim(full, x, me * x.shape[axis], axis))

    def body(*refs):
        x_refs, land_refs = refs[:nt], refs[nt:2 * nt]
        send_sems, recv_sems = refs[2 * nt + 1], refs[2 * nt + 2]
        token = refs[-1]
        for cp in _gather_copies(x_refs, land_refs, axes, send_sems, recv_sems):
            cp.start()
        token[...] = jnp.zeros_like(token)

    nsem = (N_DEV - 1) * nt
    outs = pl.pallas_call(
        body, name=name,
        out_shape=(pltpu.SemaphoreType.DMA((nsem,)), pltpu.SemaphoreType.DMA((nsem,)))
        + tuple(pltpu.HBM(a.shape, a.dtype) for a in list(xs) + lands) + (TOKEN,),
        in_specs=[HBM] * (2 * nt) + [ANY], out_specs=(SEM, SEM) + (HBM,) * (2 * nt) + (VM,),
        input_output_aliases={i: 2 + i for i in range(2 * nt)},
        compiler_params=pltpu.CompilerParams(has_side_effects=EFFECT),
    )(*[_in_hbm(a) for a in list(xs) + lands], after)
    return outs[:-1], outs[-1]


def gather_wait(handles, axes, after, *, name):
    send_sems, recv_sems = handles[0], handles[1]
    bufs = handles[2:]
    nt = len(bufs) // 2

    def body(*refs):
        x_refs, land_refs = refs[:nt], refs[nt:2 * nt]
        send_sems, recv_sems = refs[2 * nt], refs[2 * nt + 1]
        for cp in _gather_copies(x_refs, land_refs, axes, send_sems, recv_sems):
            cp.wait_send()
            cp.wait_recv()

    outs = pl.pallas_call(
        body, name=name, out_shape=tuple(pltpu.HBM(a.shape, a.dtype) for a in bufs),
        in_specs=[HBM] * (2 * nt) + [SEM, SEM, ANY], out_specs=(HBM,) * (2 * nt),
        input_output_aliases={i: i for i in range(2 * nt)},
        compiler_params=pltpu.CompilerParams(has_side_effects=EFFECT),
    )(*bufs, send_sems, recv_sems, after)
    return outs[nt:]


def _blockdiag(b, nb):
    j, _, r, c = b.shape
    eye = jnp.eye(nb, dtype=bool)[None, :, None, :, None]
    return jnp.where(eye, b[:, :, :, None, :], jnp.zeros((), b.dtype)).reshape(j, nb * r, nb * c)


def _diagblocks(d, nb):
    j, rr, cc = d.shape
    return jnp.einsum('jarac->jarc', d.reshape(j, nb, rr // nb, nb, cc // nb))


def _s5_b_dense(bbar):
    return _blockdiag(bbar.transpose(0, 2, 1).reshape(NCH, 8, S5_GROUP, S5_STATE), 8)


def _s5_b_undense(d):
    return _diagblocks(d, 8).reshape(S5_GROUPS, S5_GROUP, S5_STATE).transpose(0, 2, 1)


def _s5_c_dense(c):
    return _blockdiag(c.transpose(0, 2, 1).reshape(NCH, 8, S5_STATE, S5_GROUP), 8)


def _s5_c_undense(d):
    return _diagblocks(d, 8).reshape(S5_GROUPS, S5_STATE, S5_GROUP).transpose(0, 2, 1)


def _rg_dense(w):
    return _blockdiag(w.reshape(NCH, 2, RG_BLOCK, RG_BLOCK), 2)


def _rg_undense(d):
    return _diagblocks(d, 2).reshape(RG_BLOCKS, RG_BLOCK, RG_BLOCK)


def _chunks(v):
    return v.reshape(NCH, 1, LANE)


def _tri(tm):
    r = jnp.arange(tm)
    m = (r[:, None] >= r[None, :]) & (r[:, None] // HG_SUB == r[None, :] // HG_SUB)
    m = m.astype(f32)
    return m[None], m.T[None]


SMALL = ['norm_w', 's5_lambda_re', 's5_lambda_im', 's5_log_step', 's5_b_re', 's5_b_im', 's5_c_re', 's5_c_im',
         's5_d', 's5_b_glu', 'rg_conv_w', 'rg_conv_b', 'rg_w_a', 'rg_b_a', 'rg_w_x', 'rg_b_x', 'rg_lambda',
         'hg_lower_bounds', 'hg_norm_w', 'final_norm_w']
WEIGHTS = ['norm_w', 'w_in', 's5_lambda_re', 's5_lambda_im', 's5_log_step', 's5_b_re', 's5_b_im', 's5_c_re',
           's5_c_im', 's5_d', 's5_w_glu', 's5_b_glu', 'rg_conv_w', 'rg_conv_b', 'rg_w_a', 'rg_b_a', 'rg_w_x',
           'rg_b_x', 'rg_lambda', 'hg_lower_bounds', 'hg_norm_w', 'w_branch', 'w_out', 'final_norm_w']
PACK_ROWS = 512


def _pack(arrs):
    flat = jnp.concatenate([a.reshape(-1) for a in arrs])
    pad = (-flat.shape[0]) % (PACK_ROWS * LANE)
    return jnp.pad(flat, (0, pad)).reshape(1, -1, LANE)


def _unpack(buf, shapes):
    flat = buf.reshape(-1)
    out, off = [], 0
    for s in shapes:
        n = math.prod(s)
        out.append(flat[off:off + n].reshape(s))
        off += n
    return out


def _step(x, tgt, w, m, v):
    t = x.shape[0]
    tri, tri_t = _tri(min(LANE, t))
    me = 4 * lax.axis_index("x") + 2 * lax.axis_index("y") + lax.axis_index("c")

    big = ('w_in', 's5_w_glu', 'w_branch', 'w_out')
    big_axis = (1, 0, 2, 0)
    shards = lambda l: [w[k][l].astype(bf16) for k in big]
    win, wglu, wbr, wout = ([None] * DEPTH for _ in range(4))
    win[0] = all_gather(shards(0)[0], big_axis[0], name="ag_w_in")
    rest0_axis = big_axis[1:] + (1,)
    rest0, rest0_token = gather_start(shards(0)[1:] + [w['rg_conv_w'].reshape(DEPTH * RG_CONV, LANE)], rest0_axis,
                                      win[0], name="ag_start_0")

    lb_rows = [w['hg_lower_bounds'][l][None] for l in range(DEPTH)]
    lbs = whole(lb_prep_fn, lb_rows, [(1, W_MIX)] * DEPTH, name="lb_prep")

    saved = []
    for l in range(DEPTH):
        s = {}
        nw = w['norm_w'][l].reshape(1, 1, D_MODEL)
        (h,) = rowwise(ln_fn, [(x, 0, D_MODEL)], [nw], [], [(D_MODEL, bf16)], name="ln_fwd")
        token = None
        if l + 1 < DEPTH:
            handles, token = gather_start(shards(l + 1), big_axis, rest0_token if l == 0 else x, name=f"ag_start_{l + 1}")
        z = mm(h, win[l], after=token, name="mm_in", tm=1024)
        if l == 0:
            wglu[0], wbr[0], wout[0], conv_w = gather_wait(rest0, rest0_axis, z, name="ag_wait_0")
            conv_w = conv_w.reshape(DEPTH, RG_CONV, W_MIX)
        s5p = [w['s5_lambda_re'][l][..., None], w['s5_lambda_im'][l][..., None], w['s5_log_step'][l][:, None, None],
               w['s5_b_re'][l], w['s5_b_im'][l]]
        gp = (S5_GROUPS, S5_STATE)
        abar_re, abar_im, bbar_re, bbar_im = whole(
            s5_prep_fn, s5p, [gp + (1,), gp + (1,), gp + (S5_GROUP,), gp + (S5_GROUP,)], name="s5_prep")
        a_re, a_im = abar_re.reshape(NCH, 1, S5_SC), abar_im.reshape(NCH, 1, S5_SC)
        bd_re, bd_im = _s5_b_dense(bbar_re), _s5_b_dense(bbar_im)
        cd_re, cd_im = _s5_c_dense(w['s5_c_re'][l]), _s5_c_dense(w['s5_c_im'][l])
        yssm, xre, xim = s5_scan_fwd(z, bd_re, bd_im, cd_re, cd_im, a_re, a_im)
        s5post_p = [w['s5_d'][l].reshape(1, 1, W_MIX), wglu[l].astype(f32)[None], w['s5_b_glu'][l].reshape(1, 1, W_MIX)]
        s5post_rows = [(yssm, 0, W_MIX), (z, C_UA, W_MIX), (z, C_GA, W_MIX)]
        (ya,) = rowwise(s5_post_fn, s5post_rows, s5post_p, [], [(W_MIX, bf16)], name="s5_post_fwd")
        cw, cb = conv_w[l].reshape(RG_CONV, NCH, LANE).transpose(1, 0, 2), _chunks(w['rg_conv_b'][l])
        xc = rg_conv_fwd(z, cw, cb)
        rg_p = [_rg_dense(w['rg_w_a'][l]), _chunks(w['rg_b_a'][l]), _rg_dense(w['rg_w_x'][l]),
                _chunks(w['rg_b_x'][l]), _chunks(w['rg_lambda'][l])]
        ra, rb = rowwise(rg_gate_fn, [(xc, 0, W_MIX)], rg_p, [], [(W_MIX, f32)] * 2, name="rg_gate_fwd",
                         ncol=NCH, tm=TM_CHUNK, rowid=True)
        hb = rg_scan_fwd(ra, rb)
        hg_rows = [(z, C_Q, W_MIX), (z, C_F, W_MIX)]
        hg_p = [_chunks(lbs[l].reshape(W_MIX))]
        qs, kk, gcum = rowwise(hg_pre_fn, hg_rows, hg_p, [tri, tri_t], [(W_MIX, f32)] * 3, name="hg_pre_fwd", ncol=NCH, tm=TM_CHUNK)
        oc, sall = hg_chunk_fwd(qs, kk, gcum, z)
        bp_rows = [(hb, 0, W_MIX), (z, C_GB, W_MIX), (oc, 0, W_MIX), (z, C_GC, W_MIX)]
        bp_p = [_chunks(w['hg_norm_w'][l])]
        yb, yc = rowwise(branch_prep_fn, bp_rows, bp_p, [], [(W_MIX, bf16)] * 2, name="branch_prep_fwd", ncol=NCH, tm=TM_CHUNK)
        ys = [ya, yb, yc]
        br = [mm(ys[n], wbr[l][n], name="mm_branch", out_dtype=bf16) for n in range(N_BRANCH)]
        mg_rows = [(br[n], 0, D_MODEL) for n in range(N_BRANCH)] + [(z, C_GATE + n * D_MODEL, D_MODEL) for n in range(N_BRANCH)]
        (merged,) = rowwise(merge_fn, mg_rows, [], [], [(D_MODEL, bf16)], name="merge_fwd", ncol=2)
        x_new = mm(merged, wout[l], add=x, name="mm_out")
        s.update(x=x, h=h, z=z, s5p=s5p, s5=(bd_re, bd_im, cd_re, cd_im, a_re, a_im), xre=xre, xim=xim,
                 s5post_rows=s5post_rows, s5post_p=s5post_p, cw=cw, xc=xc, rg_p=rg_p, ra=ra, hb=hb,
                 hg_rows=hg_rows, hg_p=hg_p, qs=qs, kk=kk, gcum=gcum, sall=sall, bp_rows=bp_rows, bp_p=bp_p,
                 ys=ys, mg_rows=mg_rows, merged=merged, nw=nw)
        saved.append(s)
        x = x_new
        if l + 1 < DEPTH:
            win[l + 1], wglu[l + 1], wbr[l + 1], wout[l + 1] = gather_wait(handles, big_axis, x, name=f"ag_wait_{l + 1}")

    fnw = w['final_norm_w'].reshape(1, 1, D_MODEL)
    ones = jnp.ones((t, 1), f32)
    dx, d_fnw, loss_sum = rowwise_vjp(loss_fn, [(x, 0, D_MODEL), (tgt, 0, D_MODEL)], [fnw], [], [(ones, 0, 1)],
                                      [(0, f32)], name="loss_head", sum_primal=0)
    loss = lax.psum(loss_sum.reshape(()), ("x", "y", "c"))

    small_g = {k: [None] * DEPTH for k in SMALL if k != 'final_norm_w'}
    own_sums, in_flight = [None] * DEPTH, [None] * DEPTH
    d_lbs = [None] * DEPTH
    token = None
    for l in reversed(range(DEPTH)):
        s = saved[l]
        z = s['z']
        dxb = dx.astype(bf16)
        d_merged = mm(dxb, wout[l], bt=True, after=token, name="mm_out_dx", out_dtype=bf16)
        d_wout = mm(s['merged'].T, dxb, name="mm_out_dw", out_dtype=bf16)
        mg = rowwise_vjp(merge_fn, s['mg_rows'], [], [], [(d_merged, 0, D_MODEL)],
                         [(n, bf16) for n in range(2 * N_BRANCH)], name="merge_bwd", ncol=2)
        d_br, d_gl = mg[:N_BRANCH], mg[N_BRANCH:]
        d_ys = [mm(d_br[n], wbr[l][n], bt=True, name="mm_branch_dx", out_dtype=bf16) for n in range(N_BRANCH)]
        d_wbr = jnp.stack([mm(s['ys'][n].T, d_br[n], name="mm_branch_dw", out_dtype=bf16) for n in range(N_BRANCH)])
        d_hb, d_gb, d_oc, d_gc, d_hnw = rowwise_vjp(
            branch_prep_fn, s['bp_rows'], s['bp_p'], [], [(d_ys[1], 0, W_MIX), (d_ys[2], 0, W_MIX)],
            [(0, f32), (1, bf16), (2, f32), (3, bf16)], name="branch_prep_bwd", ncol=NCH, tm=TM_CHUNK)
        small_g['hg_norm_w'][l] = d_hnw.reshape(W_MIX)
        d_qs, d_kk, d_gcum, d_i = hg_chunk_bwd(d_oc, s['qs'], s['kk'], s['gcum'], z, s['sall'])
        d_q, d_f, d_lb = rowwise_vjp(
            hg_pre_fn, s['hg_rows'], s['hg_p'], [tri, tri_t], [(d_qs, 0, W_MIX), (d_kk, 0, W_MIX), (d_gcum, 0, W_MIX)],
            [(0, bf16), (1, bf16)], name="hg_pre_bwd", ncol=NCH, tm=TM_CHUNK)
        d_lbs[l] = d_lb.reshape(1, W_MIX)
        d_ra, d_rb = rg_scan_bwd(d_hb, s['ra'], s['hb'])
        rg = rowwise_vjp(rg_gate_fn, [(s['xc'], 0, W_MIX)], s['rg_p'], [], [(d_ra, 0, W_MIX), (d_rb, 0, W_MIX)],
                         [(0, f32)], name="rg_gate_bwd", ncol=NCH, tm=TM_CHUNK, rowid=True)
        d_xc, d_wa, d_ba, d_wx, d_bx, d_lam = rg
        d_xb, d_cw, d_cb = rg_conv_bwd(d_xc, z, s['cw'])
        small_g['rg_w_a'][l], small_g['rg_w_x'][l] = _rg_undense(d_wa), _rg_undense(d_wx)
        small_g['rg_b_a'][l], small_g['rg_b_x'][l] = d_ba.reshape(W_MIX), d_bx.reshape(W_MIX)
        small_g['rg_lambda'][l] = d_lam.reshape(W_MIX)
        small_g['rg_conv_w'][l] = d_cw.transpose(1, 0, 2).reshape(RG_CONV, W_MIX)
        small_g['rg_conv_b'][l] = d_cb.reshape(W_MIX)
        d_yssm, d_u1, d_ga, d_d, d_wglu, d_bglu = rowwise_vjp(
            s5_post_fn, s['s5post_rows'], s['s5post_p'], [], [(d_ys[0], 0, W_MIX)],
            [(0, bf16), (1, bf16), (2, bf16)], name="s5_post_bwd")
        small_g['s5_d'][l], small_g['s5_b_glu'][l] = d_d.reshape(W_MIX), d_bglu.reshape(W_MIX)
        d_ua, d_bdre, d_bdim, d_cdre, d_cdim, d_are, d_aim = s5_scan_bwd(d_yssm, d_u1, z, s['xre'], s['xim'], *s['s5'])
        small_g['s5_c_re'][l], small_g['s5_c_im'][l] = _s5_c_undense(d_cdre), _s5_c_undense(d_cdim)
        gp = (S5_GROUPS, S5_STATE, 1)
        s5g = whole_vjp(s5_prep_fn, s['s5p'],
                        [d_are.reshape(gp), d_aim.reshape(gp), _s5_b_undense(d_bdre), _s5_b_undense(d_bdim)],
                        name="s5_prep_bwd")
        small_g['s5_lambda_re'][l] = s5g[0].reshape(S5_GROUPS, S5_STATE)
        small_g['s5_lambda_im'][l] = s5g[1].reshape(S5_GROUPS, S5_STATE)
        small_g['s5_log_step'][l] = s5g[2].reshape(S5_GROUPS)
        small_g['s5_b_re'][l], small_g['s5_b_im'][l] = s5g[3], s5g[4]
        dz = jnp.concatenate([d_ua, d_ga, d_xb, d_gb, d_q, d_f, d_i, d_gc] + list(d_gl), axis=1)
        d_win = mm(s['h'].T, dz, name="mm_in_dw", out_dtype=bf16)
        sums = [pair_sums(g, ax, name="rs_" + k) for g, ax, k in zip((d_win, d_wglu[0], d_wbr, d_wout), big_axis, big)]
        own_sums[l] = [own for own, _ in sums]
        in_flight[l], token = send_pairs_start([pair for _, pair in sums], s5g[0], name=f"rs_start_{l}")
        d_h = mm(dz, win[l], bt=True, after=token, name="mm_in_dx", tm=1024, tk=2048)
        dx, d_nw = rowwise_vjp(ln_res_fn, [(s['x'], 0, D_MODEL)], [s['nw']], [], [(d_h, 0, D_MODEL), (dx, 0, D_MODEL)],
                               [(0, f32)], name="ln_bwd")
        small_g['norm_w'][l] = d_nw.reshape(D_MODEL)
    d_lb_raw = whole_vjp(lb_prep_fn, lb_rows, d_lbs, name="lb_prep_bwd")
    small_g['hg_lower_bounds'] = [r.reshape(W_MIX) for r in d_lb_raw]

    g_small = [jnp.stack(small_g[k]) for k in SMALL if k != 'final_norm_w'] + [d_fnw.reshape(D_MODEL)]
    shapes = [g.shape for g in g_small]
    small_in_flight, small_token = gather_start([_pack(g_small)[0].astype(bf16)], (0,), dx, name="ag_small_start")
    res = {}

    arrived = [send_pairs_wait(in_flight[l], small_token, name=f"rs_wait_{l}") for l in range(DEPTH)]
    for i, (k, tr) in enumerate((('w_in', 32), ('s5_w_glu', 32), ('w_branch', 128), ('w_out', 32))):
        shp = w[k].shape
        r3 = lambda a: a.reshape(DEPTH, -1, shp[-1])
        slots = [[own_sums[l][i], arrived[l][i]] for l in range(DEPTH)]
        outs = adamw(r3(w[k]), r3(m[k]), r3(v[k]), slots, name="adamw_" + k, tr=tr)
        for kind, buf in zip(('grad', 'delta', 'new_m', 'new_v'), outs):
            res[kind + '_' + k] = buf.reshape(shp)

    (g_all,) = gather_wait(small_in_flight, (0,), outs[0], name="ag_small_wait")
    g_all = g_all.reshape(N_DEV, -1, LANE)

    def local(d, k):
        return jnp.zeros(shapes[SMALL.index(k)], f32) if k == 'rg_conv_w' else d[k]
    packed = [_pack([local(d, k) for k in SMALL]) for d in (w, m, v)]
    outs = adamw(*packed, [[g_all]], name="adamw_small", tr=512)
    for kind, buf in zip(('grad', 'delta', 'new_m', 'new_v'), outs):
        for k, a in zip(SMALL, _unpack(buf, shapes)):
            res[kind + '_' + k] = a
    g_cw = lax.dynamic_slice_in_dim(res['grad_rg_conv_w'], me * LANE, LANE, axis=2)
    cw3 = lambda a: a.reshape(1, DEPTH * RG_CONV, LANE)
    outs = adamw(cw3(w['rg_conv_w']), cw3(m['rg_conv_w']), cw3(v['rg_conv_w']), [[cw3(g_cw)]], name="adamw_conv_w", tr=16)
    for kind, buf in zip(('grad', 'delta', 'new_m', 'new_v'), outs):
        res[kind + '_rg_conv_w'] = buf.reshape(DEPTH, RG_CONV, LANE)

    return (loss, dx[None]) + tuple(res[kind + '_' + k] for kind in ('grad', 'delta', 'new_m', 'new_v') for k in WEIGHTS)


def kernel(x, norm_w, w_in, s5_lambda_re, s5_lambda_im, s5_log_step, s5_b_re, s5_b_im, s5_c_re, s5_c_im, s5_d, s5_w_glu, s5_b_glu, rg_conv_w, rg_conv_b, rg_w_a, rg_b_a, rg_w_x, rg_b_x, rg_lambda, hg_lower_bounds, hg_norm_w, w_branch, w_out, final_norm_w, loss_target, m_norm_w, m_w_in, m_s5_lambda_re, m_s5_lambda_im, m_s5_log_step, m_s5_b_re, m_s5_b_im, m_s5_c_re, m_s5_c_im, m_s5_d, m_s5_w_glu, m_s5_b_glu, m_rg_conv_w, m_rg_conv_b, m_rg_w_a, m_rg_b_a, m_rg_w_x, m_rg_b_x, m_rg_lambda, m_hg_lower_bounds, m_hg_norm_w, m_w_branch, m_w_out, m_final_norm_w, v_norm_w, v_w_in, v_s5_lambda_re, v_s5_lambda_im, v_s5_log_step, v_s5_b_re, v_s5_b_im, v_s5_c_re, v_s5_c_im, v_s5_d, v_s5_w_glu, v_s5_b_glu, v_rg_conv_w, v_rg_conv_b, v_rg_w_a, v_rg_b_a, v_rg_w_x, v_rg_b_x, v_rg_lambda, v_hg_lower_bounds, v_hg_norm_w, v_w_branch, v_w_out, v_final_norm_w):
    w = dict(zip(WEIGHTS, (norm_w, w_in, s5_lambda_re, s5_lambda_im, s5_log_step, s5_b_re, s5_b_im, s5_c_re, s5_c_im, s5_d, s5_w_glu, s5_b_glu, rg_conv_w, rg_conv_b, rg_w_a, rg_b_a, rg_w_x, rg_b_x, rg_lambda, hg_lower_bounds, hg_norm_w, w_branch, w_out, final_norm_w)))
    m = dict(zip(WEIGHTS, (m_norm_w, m_w_in, m_s5_lambda_re, m_s5_lambda_im, m_s5_log_step, m_s5_b_re, m_s5_b_im, m_s5_c_re, m_s5_c_im, m_s5_d, m_s5_w_glu, m_s5_b_glu, m_rg_conv_w, m_rg_conv_b, m_rg_w_a, m_rg_b_a, m_rg_w_x, m_rg_b_x, m_rg_lambda, m_hg_lower_bounds, m_hg_norm_w, m_w_branch, m_w_out, m_final_norm_w)))
    v = dict(zip(WEIGHTS, (v_norm_w, v_w_in, v_s5_lambda_re, v_s5_lambda_im, v_s5_log_step, v_s5_b_re, v_s5_b_im, v_s5_c_re, v_s5_c_im, v_s5_d, v_s5_w_glu, v_s5_b_glu, v_rg_conv_w, v_rg_conv_b, v_rg_w_a, v_rg_b_a, v_rg_w_x, v_rg_b_x, v_rg_lambda, v_hg_lower_bounds, v_hg_norm_w, v_w_branch, v_w_out, v_final_norm_w)))
    return _step(x[0], loss_target[0], w, m, v)
```

```python
import functools
import math

import jax
import jax.numpy as jnp
from jax import lax
from jax.experimental import pallas as pl
from jax.experimental.pallas import tpu as pltpu

f32 = jnp.float32
bf16 = jnp.bfloat16

D_MODEL = 2048
W_MIX = 1024
DEPTH = 4
N_BRANCH = 3
N_IN = 8 * W_MIX + N_BRANCH * D_MODEL
S5_GROUPS, S5_STATE, S5_GROUP = 64, 64, 16
RG_BLOCKS, RG_BLOCK, RG_CONV, RG_C = 16, 64, 4, 8.0
HG_HEADS, HG_DK = 8, 128
HG_SUB = 16
EPS = 1e-6
ADAM_LR, ADAM_B1, ADAM_B2, ADAM_EPS, ADAM_WD, ADAM_STEP = 0.001, 0.9, 0.999, 1e-08, 0.01, 10

N_DEV = 8
LANE = 128
NCH = W_MIX // LANE
TM_CHUNK = 1024
VMEM_LIMIT = 56 * 1024 * 1024
MESH = pl.DeviceIdType.MESH
ANY = pl.BlockSpec(memory_space=pl.ANY)
HIGHEST = lax.Precision.HIGHEST

C_UA, C_GA, C_XB, C_GB, C_Q, C_F, C_I, C_GC, C_GATE = (W_MIX * k for k in range(9))


def _cparams(sem=None):
    return pltpu.CompilerParams(dimension_semantics=sem, vmem_limit_bytes=VMEM_LIMIT)


@jax.custom_vjp
def bdot(a, w):
    return jnp.dot(a.astype(bf16), w.astype(bf16), preferred_element_type=f32)


def _bdot_fwd(a, w):
    return bdot(a, w), (a, w)


def _bdot_bwd(res, g):
    a, w = res
    gb = g.astype(bf16)
    da = lax.dot_general(gb, w.astype(bf16), (((1,), (1,)), ((), ())), preferred_element_type=f32)
    dw = lax.dot_general(a.astype(bf16), gb, (((0,), (0,)), ((), ())), preferred_element_type=f32)
    return da, dw


bdot.defvjp(_bdot_fwd, _bdot_bwd)


def _blockmm(c, a):
    n = c.shape[0]
    return jnp.concatenate([jnp.dot(c, a[i:i + n], preferred_element_type=f32, precision=HIGHEST)
                            for i in range(0, a.shape[0], n)], axis=0)


@jax.custom_vjp
def cdot(c, ct, a):
    return _blockmm(c, a)


def _cdot_fwd(c, ct, a):
    return cdot(c, ct, a), (c, ct)


def _cdot_bwd(res, g):
    c, ct = res
    return jnp.zeros_like(c), jnp.zeros_like(ct), _blockmm(ct, g)


cdot.defvjp(_cdot_fwd, _cdot_bwd)


def mm(a, b, *, name, out_dtype=f32, add=None, after=None, at=False, bt=False, tm=1024, tn=1024, tk=4096):
    k, m = a.shape if at else a.shape[::-1]
    n = b.shape[0] if bt else b.shape[1]
    tm, tn, tk = min(tm, m), min(tn, n), min(tk, k)
    assert m % tm == 0 and n % tn == 0 and k % tk == 0
    nk = k // tk
    dims = (((0 if at else 1,), (1 if bt else 0,)), ((), ()))

    def body(*refs):
        a_ref, b_ref = refs[:2]
        r_ref = refs[2] if add is not None else None
        o_ref = refs[-1] if nk == 1 else refs[-2]
        part = lax.dot_general(a_ref[...], b_ref[...], dims, preferred_element_type=f32)
        if nk == 1:
            if add is not None:
                part = part + r_ref[...]
            o_ref[...] = part.astype(out_dtype)
            return
        acc_ref = refs[-1]
        kk = pl.program_id(2)

        @pl.when(kk == 0)
        def _():
            acc_ref[...] = part

        @pl.when(kk > 0)
        def _():
            acc_ref[...] = acc_ref[...] + part

        @pl.when(kk == nk - 1)
        def _():
            acc = acc_ref[...]
            if add is not None:
                acc = acc + r_ref[...]
            o_ref[...] = acc.astype(out_dtype)

    b_spec = pl.BlockSpec((tn, tk), lambda i, j, q: (j, q)) if bt else pl.BlockSpec((tk, tn), lambda i, j, q: (q, j))
    a_spec = pl.BlockSpec((tk, tm), lambda i, j, q: (q, i)) if at else pl.BlockSpec((tm, tk), lambda i, j, q: (i, q))
    in_specs = [a_spec, b_spec]
    args = [a, b]
    if add is not None:
        in_specs.append(pl.BlockSpec((tm, tn), lambda i, j, q: (i, j)))
        args.append(add)
    if after is not None:
        in_specs.append(pl.BlockSpec(after.shape, lambda i, j, q: (0, 0)))
        args.append(after)
    return pl.pallas_call(
        body, name=name, grid=(m // tm, n // tn, nk), in_specs=in_specs,
        out_specs=pl.BlockSpec((tm, tn), lambda i, j, q: (i, j)),
        out_shape=jax.ShapeDtypeStruct((m, n), out_dtype),
        scratch_shapes=[] if nk == 1 else [pltpu.VMEM((tm, tn), f32)],
        compiler_params=_cparams(("parallel", "parallel", "arbitrary")),
    )(*args)


def _row_spec(tm, wc, col_off):
    base = col_off // wc
    assert col_off % wc == 0
    return pl.BlockSpec((tm, wc), lambda j, i: (i, base + j))


def _slab_spec(arr):
    r, c = arr.shape[1:]
    if arr.shape[0] == 1:
        return pl.BlockSpec((1, r, c), lambda j, i: (0, 0, 0))
    return pl.BlockSpec((1, r, c), lambda j, i: (j, 0, 0))


def rowwise(fn, rows, params, consts, outs, *, name, tm=256, ncol=1, rowid=False):
    t = rows[0][0].shape[0]
    tm = min(tm, t)
    nr, npar, nc, no = len(rows), len(params), len(consts), len(outs)

    def body(*refs):
        r = [refs[k][...].astype(f32) for k in range(nr)]
        p = [refs[nr + k][0] for k in range(npar + nc)]
        extra = ()
        if rowid:
            extra = (pl.program_id(1) * tm + lax.broadcasted_iota(jnp.int32, (tm, 1), 0),)
        res = fn(*extra, *r, *p)
        for k in range(no):
            refs[nr + npar + nc + k][...] = res[k].astype(outs[k][1])

    in_specs = [_row_spec(tm, w // ncol, off) for (_, off, w) in rows]
    in_specs += [_slab_spec(a) for a in list(params) + list(consts)]
    out_specs = [pl.BlockSpec((tm, w // ncol), lambda j, i: (i, j)) for (w, _) in outs]
    out_shape = [jax.ShapeDtypeStruct((t, w), dt) for (w, dt) in outs]
    return pl.pallas_call(
        body, name=name, grid=(ncol, t // tm), in_specs=in_specs, out_specs=out_specs, out_shape=out_shape,
        compiler_params=_cparams(("parallel", "parallel")),
    )(*[r[0] for r in rows], *params, *consts)


def rowwise_vjp(fn, rows, params, consts, cts, d_rows, *, name, tm=256, ncol=1, rowid=False, sum_primal=None):
    t = rows[0][0].shape[0]
    tm = min(tm, t)
    nr, npar, nc, nct, ndr = len(rows), len(params), len(consts), len(cts), len(d_rows)

    def body(*refs):
        i = pl.program_id(1)
        r = [refs[k][...].astype(f32) for k in range(nr)]
        p = [refs[nr + k][0] for k in range(npar)]
        c = [refs[nr + npar + k][0] for k in range(nc)]
        g = [refs[nr + npar + nc + k][...].astype(f32) for k in range(nct)]
        orefs = refs[nr + npar + nc + nct:]
        extra = ()
        if rowid:
            extra = (i * tm + lax.broadcasted_iota(jnp.int32, (tm, 1), 0),)
        res, vjp = jax.vjp(lambda *v: fn(*extra, *v, *c), *r, *p)
        grads = vjp(tuple(g))
        for k, (idx, dt) in enumerate(d_rows):
            orefs[k][...] = grads[idx].astype(dt)
        acc = [grads[nr + k] for k in range(npar)]
        if sum_primal is not None:
            acc.append(jnp.sum(res[sum_primal], axis=0, keepdims=True))

        @pl.when(i == 0)
        def _():
            for k, a in enumerate(acc):
                orefs[ndr + k][0] = a

        @pl.when(i > 0)
        def _():
            for k, a in enumerate(acc):
                orefs[ndr + k][0] = orefs[ndr + k][0] + a

    in_specs = [_row_spec(tm, w // ncol, off) for (_, off, w) in rows]
    in_specs += [_slab_spec(a) for a in list(params) + list(consts)]
    in_specs += [_row_spec(tm, w // ncol, off) for (_, off, w) in cts]
    out_specs = [pl.BlockSpec((tm, rows[idx][2] // ncol), lambda j, i: (i, j)) for (idx, _) in d_rows]
    out_shape = [jax.ShapeDtypeStruct((t, rows[idx][2]), dt) for (idx, dt) in d_rows]
    for a in params:
        out_specs.append(pl.BlockSpec((1,) + a.shape[1:], lambda j, i: (j, 0, 0)))
        out_shape.append(jax.ShapeDtypeStruct(a.shape, f32))
    if sum_primal is not None:
        w = cts[sum_primal][2]
        out_specs.append(pl.BlockSpec((1, 1, w // ncol), lambda j, i: (j, 0, 0)))
        out_shape.append(jax.ShapeDtypeStruct((ncol, 1, w // ncol), f32))
    return pl.pallas_call(
        body, name=name, grid=(ncol, t // tm), in_specs=in_specs, out_specs=out_specs, out_shape=out_shape,
        compiler_params=_cparams(("parallel", "arbitrary")),
    )(*[r[0] for r in rows], *params, *consts, *[c[0] for c in cts])


VM = pl.BlockSpec(memory_space=pltpu.VMEM)


def whole(fn, ins, outs, *, name):
    def body(*refs):
        res = fn(*[r[...] for r in refs[:len(ins)]])
        for k, o in enumerate(refs[len(ins):]):
            o[...] = res[k]
    return pl.pallas_call(body, name=name, in_specs=[VM] * len(ins), out_specs=[VM] * len(outs),
                          out_shape=[jax.ShapeDtypeStruct(s, f32) for s in outs],
                          compiler_params=_cparams())(*ins)


def whole_vjp(fn, ins, cts, *, name):
    n = len(ins)

    def body(*refs):
        _, vjp = jax.vjp(fn, *[r[...] for r in refs[:n]])
        grads = vjp(tuple(r[...] for r in refs[n:n + len(cts)]))
        for k, o in enumerate(refs[n + len(cts):]):
            o[...] = grads[k]
    return pl.pallas_call(body, name=name, in_specs=[VM] * (n + len(cts)), out_specs=[VM] * n,
                          out_shape=[jax.ShapeDtypeStruct(a.shape, f32) for a in ins],
                          compiler_params=_cparams())(*ins, *cts)


def ln_fn(x, w):
    return (x * lax.rsqrt(jnp.mean(x * x, axis=-1, keepdims=True) + EPS) * w,)


def ln_res_fn(x, w):
    return ln_fn(x, w)[0], x


def loss_fn(x, tgt, w):
    y = ln_fn(x, w)[0]
    return (0.5 * jnp.mean(jnp.square(y - tgt), axis=-1, keepdims=True),)


def s5_prep_fn(lam_re, lam_im, log_step, b_re, b_im):
    step = jnp.exp(log_step)
    mag = jnp.exp(lam_re * step)
    ang = lam_im * step
    abar_re = mag * jnp.cos(ang)
    abar_im = mag * jnp.sin(ang)
    num_re = abar_re - 1.0
    num_im = abar_im
    den = lam_re * lam_re + lam_im * lam_im
    coef_re = (num_re * lam_re + num_im * lam_im) / den
    coef_im = (num_im * lam_re - num_re * lam_im) / den
    bbar_re = coef_re * b_re - coef_im * b_im
    bbar_im = coef_re * b_im + coef_im * b_re
    return abar_re, abar_im, bbar_re, bbar_im


def lb_prep_fn(r0, r1, r2, r3):
    m = jnp.maximum(jnp.maximum(r0, r1), jnp.maximum(r2, r3))
    e0, e1, e2, e3 = jnp.exp(r0 - m), jnp.exp(r1 - m), jnp.exp(r2 - m), jnp.exp(r3 - m)
    s = e0 + e1 + e2 + e3
    p0, p1, p2, p3 = e0 / s, e1 / s, e2 / s, e3 / s
    c1 = p0 + p1
    c2 = c1 + p2
    c3 = c2 + p3
    return p0 - p0, c1 - p0, c2 - p0, c3 - p0


def s5_post_fn(yssm, u, ga, d, wglu, bglu):
    y = jax.nn.gelu(yssm + d * u)
    y = y * jax.nn.sigmoid(bdot(y, wglu) + bglu)
    return (y * jax.nn.silu(ga),)


def rg_gate_fn(tglob, xc, wa, ba, wx, bx, lam):
    r = jax.nn.sigmoid(bdot(xc, wa) + ba)
    i = jax.nn.sigmoid(bdot(xc, wx) + bx)
    log_a = -RG_C * r * jax.nn.softplus(-lam)
    a = jnp.exp(log_a)
    mult = jnp.sqrt(-jnp.tanh(log_a) * (a * a + 1.0))
    mult = jnp.where(tglob == 0, 1.0, mult)
    return a, mult * (i * xc)


def hg_pre_fn(q, fl, lb, tri, tri_t):
    f = lb + (1.0 - lb) * jax.nn.sigmoid(fl)
    return jax.nn.silu(q), 1.0 - f, cdot(tri, tri_t, jnp.log(f))


def branch_prep_fn(hb, gb, oc, gc, nw):
    yb = hb * jax.nn.silu(gb)
    on = oc * lax.rsqrt(jnp.mean(oc * oc, axis=-1, keepdims=True) + EPS) * nw
    return yb, on * jax.nn.silu(gc)


def merge_fn(b0, b1, b2, g0, g1, g2):
    return (jax.nn.sigmoid(g0) * b0 + jax.nn.sigmoid(g1) * b1 + jax.nn.sigmoid(g2) * b2,)


S5_TB = 512
S5_SC = 512


SEG = 8


def _shift(v, k, pos, period, reverse, fill):
    if reverse:
        return jnp.where(pos < period - k, pltpu.roll(v, v.shape[0] - k, 0), fill)
    return jnp.where(pos >= k, pltpu.roll(v, k, 0), fill)


def _cmul(ar, ai, br, bi):
    return ar * br - ai * bi, ar * bi + ai * br


def _edge_rows(v, first):
    r0 = 0 if first else SEG - 1
    return jnp.concatenate([v[r:r + 1, :] for r in range(r0, v.shape[0], SEG)], axis=0)


def _spread(s):
    return jnp.concatenate([jnp.broadcast_to(s[g:g + 1, :], (SEG, s.shape[1])) for g in range(s.shape[0])], axis=0)


def lti_scan(xr, xi, ar, ai, cr, ci, reverse=False):
    n = xr.shape[0]
    g = n // SEG
    sub = lax.broadcasted_iota(jnp.int32, (n, 1), 0) & (SEG - 1)
    sub8 = lax.broadcasted_iota(jnp.int32, (SEG, 1), 0)
    grow = lax.broadcasted_iota(jnp.int32, (g, 1), 0)
    pr, pi_ = ar, ai
    wr, wi = jnp.broadcast_to(ar, (SEG, ar.shape[1])), jnp.broadcast_to(ai, (SEG, ai.shape[1]))
    k = 1
    while k < SEG:
        tr, ti = _cmul(pr, pi_, _shift(xr, k, sub, SEG, reverse, 0.0), _shift(xi, k, sub, SEG, reverse, 0.0))
        xr, xi = xr + tr, xi + ti
        pr, pi_ = _cmul(pr, pi_, pr, pi_)
        wr, wi = _cmul(wr, wi, _shift(wr, k, sub8, SEG, reverse, 1.0), _shift(wi, k, sub8, SEG, reverse, 0.0))
        k *= 2
    first, last = (g - 1, 0) if reverse else (0, g - 1)
    jr, ji = _cmul(pr, pi_, cr, ci)
    sr = _edge_rows(xr, reverse) + jnp.where(grow == first, jr, 0.0)
    si = _edge_rows(xi, reverse) + jnp.where(grow == first, ji, 0.0)
    k = 1
    while k < g:
        tr, ti = _cmul(pr, pi_, _shift(sr, k, grow, g, reverse, 0.0), _shift(si, k, grow, g, reverse, 0.0))
        sr, si = sr + tr, si + ti
        pr, pi_ = _cmul(pr, pi_, pr, pi_)
        k *= 2
    er, ei = _spread(_shift(sr, 1, grow, g, reverse, cr)), _spread(_shift(si, 1, grow, g, reverse, ci))
    tr, ti = _cmul(jnp.tile(wr, (g, 1)), jnp.tile(wi, (g, 1)), er, ei)
    return xr + tr, xi + ti, sr[last:last + 1, :], si[last:last + 1, :]


def tv_scan(aa, bb, carry, reverse=False):
    n = aa.shape[0]
    row = lax.broadcasted_iota(jnp.int32, (n, 1), 0)
    k = 1
    while k < n:
        bb = bb + aa * _shift(bb, k, row, n, reverse, 0.0)
        aa = aa * _shift(aa, k, row, n, reverse, 1.0)
        k *= 2
    h = bb + aa * carry
    last = 0 if reverse else n - 1
    return h, h[last:last + 1, :]


def s5_scan_fwd(z, bd_re, bd_im, cd_re, cd_im, a_re, a_im):
    t = z.shape[0]
    tb = min(S5_TB, t)

    def body(u_ref, bre, bim, cre, cim, are, aim, y_ref, xre_ref, xim_ref, car_re, car_im):
        @pl.when(pl.program_id(1) == 0)
        def _():
            car_re[...] = jnp.zeros_like(car_re)
            car_im[...] = jnp.zeros_like(car_im)

        u = u_ref[...].astype(bf16)
        xr, xi, car_re[...], car_im[...] = lti_scan(
            jnp.dot(u, bre[0], preferred_element_type=f32), jnp.dot(u, bim[0], preferred_element_type=f32),
            are[0], aim[0], car_re[...], car_im[...])
        xre_ref[...] = xr
        xim_ref[...] = xi
        y_ref[...] = (jnp.dot(xr.astype(bf16), cre[0], preferred_element_type=f32)
                      - jnp.dot(xi.astype(bf16), cim[0], preferred_element_type=f32))

    chunk = lambda r, c: pl.BlockSpec((1, r, c), lambda j, i: (j, 0, 0))
    return pl.pallas_call(
        body, name="s5_scan_fwd", grid=(NCH, t // tb),
        in_specs=[pl.BlockSpec((tb, LANE), lambda j, i: (i, C_UA // LANE + j)),
                  chunk(LANE, S5_SC), chunk(LANE, S5_SC), chunk(S5_SC, LANE), chunk(S5_SC, LANE),
                  chunk(1, S5_SC), chunk(1, S5_SC)],
        out_specs=[pl.BlockSpec((tb, LANE), lambda j, i: (i, j)),
                   pl.BlockSpec((tb, S5_SC), lambda j, i: (i, j)),
                   pl.BlockSpec((tb, S5_SC), lambda j, i: (i, j))],
        out_shape=[jax.ShapeDtypeStruct((t, W_MIX), f32),
                   jax.ShapeDtypeStruct((t, NCH * S5_SC), f32),
                   jax.ShapeDtypeStruct((t, NCH * S5_SC), f32)],
        scratch_shapes=[pltpu.VMEM((1, S5_SC), f32)] * 2,
        compiler_params=_cparams(("parallel", "arbitrary")),
    )(z, bd_re.astype(bf16), bd_im.astype(bf16), cd_re.astype(bf16), cd_im.astype(bf16), a_re, a_im)


def s5_scan_bwd(dy, du1, z, xre, xim, bd_re, bd_im, cd_re, cd_im, a_re, a_im):
    t = z.shape[0]
    tb = min(S5_TB, t)
    nt = t // tb

    def body(dy_ref, du1_ref, u_ref, xre_ref, xim_ref, hre_ref, him_ref, bre, bim, cre, cim, are, aim,
             du_ref, dbre, dbim, dcre, dcim, dare, daim, car_re, car_im):
        step = pl.program_id(1)
        tt = nt - 1 - step

        @pl.when(step == 0)
        def _():
            car_re[...] = jnp.zeros_like(car_re)
            car_im[...] = jnp.zeros_like(car_im)

        nt_dims = (((1,), (1,)), ((), ()))
        tn_dims = (((0,), (0,)), ((), ()))
        dyb = dy_ref[...].astype(bf16)
        row = lax.broadcasted_iota(jnp.int32, (tb, 1), 0)
        xr, xi = xre_ref[...], xim_ref[...]
        ar, ai = are[0], aim[0]
        lr, li, car_re[...], car_im[...] = lti_scan(
            lax.dot_general(dyb, cre[0], nt_dims, preferred_element_type=f32),
            -lax.dot_general(dyb, cim[0], nt_dims, preferred_element_type=f32),
            ar, -ai, car_re[...], car_im[...], reverse=True)
        lrb, lib = lr.astype(bf16), li.astype(bf16)
        ub = u_ref[...].astype(bf16)
        du = (lax.dot_general(lrb, bre[0], nt_dims, preferred_element_type=f32)
              + lax.dot_general(lib, bim[0], nt_dims, preferred_element_type=f32))
        du_ref[...] = (du + du1_ref[...].astype(f32)).astype(du_ref.dtype)
        live = (tt > 0).astype(f32)
        xpr = jnp.where(row == 0, hre_ref[7:8, :] * live, pltpu.roll(xr, 1, 0))
        xpi = jnp.where(row == 0, him_ref[7:8, :] * live, pltpu.roll(xi, 1, 0))
        acc = [
            lax.dot_general(ub, lrb, tn_dims, preferred_element_type=f32),
            lax.dot_general(ub, lib, tn_dims, preferred_element_type=f32),
            lax.dot_general(xr.astype(bf16), dyb, tn_dims, preferred_element_type=f32),
            -lax.dot_general(xi.astype(bf16), dyb, tn_dims, preferred_element_type=f32),
            jnp.sum(lr * xpr + li * xpi, axis=0, keepdims=True),
            jnp.sum(li * xpr - lr * xpi, axis=0, keepdims=True),
        ]
        outs = [dbre, dbim, dcre, dcim, dare, daim]

        @pl.when(step == 0)
        def _():
            for o, a in zip(outs, acc):
                o[0] = a

        @pl.when(step > 0)
        def _():
            for o, a in zip(outs, acc):
                o[0] = o[0] + a

    chunk = lambda r, c: pl.BlockSpec((1, r, c), lambda j, i: (j, 0, 0))
    rev = lambda w, base=0: pl.BlockSpec((tb, w), lambda j, i: (nt - 1 - i, base + j))
    halo = pl.BlockSpec((8, S5_SC), lambda j, i: (jnp.maximum((nt - 1 - i) * (tb // 8) - 1, 0), j))
    return pl.pallas_call(
        body, name="s5_scan_bwd", grid=(NCH, nt),
        in_specs=[rev(LANE), rev(LANE), rev(LANE, C_UA // LANE), rev(S5_SC), rev(S5_SC), halo, halo,
                  chunk(LANE, S5_SC), chunk(LANE, S5_SC), chunk(S5_SC, LANE), chunk(S5_SC, LANE),
                  chunk(1, S5_SC), chunk(1, S5_SC)],
        out_specs=[rev(LANE), chunk(LANE, S5_SC), chunk(LANE, S5_SC), chunk(S5_SC, LANE), chunk(S5_SC, LANE),
                   chunk(1, S5_SC), chunk(1, S5_SC)],
        out_shape=[jax.ShapeDtypeStruct((t, W_MIX), bf16),
                   jax.ShapeDtypeStruct((NCH, LANE, S5_SC), f32), jax.ShapeDtypeStruct((NCH, LANE, S5_SC), f32),
                   jax.ShapeDtypeStruct((NCH, S5_SC, LANE), f32), jax.ShapeDtypeStruct((NCH, S5_SC, LANE), f32),
                   jax.ShapeDtypeStruct((NCH, 1, S5_SC), f32), jax.ShapeDtypeStruct((NCH, 1, S5_SC), f32)],
        scratch_shapes=[pltpu.VMEM((1, S5_SC), f32)] * 2,
        compiler_params=_cparams(("parallel", "arbitrary")),
    )(dy, du1, z, xre, xim, xre, xim, bd_re.astype(bf16), bd_im.astype(bf16), cd_re.astype(bf16),
      cd_im.astype(bf16), a_re, a_im)


RG_TB = 512


def rg_conv_fwd(z, cw, cb):
    t = z.shape[0]
    tb = min(RG_TB, t)

    def body(x_ref, h_ref, cw_ref, cb_ref, o_ref):
        live = (pl.program_id(1) > 0).astype(f32)
        ext = jnp.concatenate([h_ref[...] * live, x_ref[...]], axis=0)
        w = cw_ref[0]
        acc = cb_ref[0] + w[3:4, :] * ext[8:, :]
        for k in range(3):
            acc = acc + w[k:k + 1, :] * pltpu.roll(ext, 3 - k, 0)[8:, :]
        o_ref[...] = acc

    base = C_XB // LANE
    chunk = lambda r: pl.BlockSpec((1, r, LANE), lambda j, i: (j, 0, 0))
    return pl.pallas_call(
        body, name="rg_conv_fwd", grid=(NCH, t // tb),
        in_specs=[pl.BlockSpec((tb, LANE), lambda j, i: (i, base + j)),
                  pl.BlockSpec((8, LANE), lambda j, i: (jnp.maximum(i * (tb // 8) - 1, 0), base + j)),
                  chunk(RG_CONV), chunk(1)],
        out_specs=pl.BlockSpec((tb, LANE), lambda j, i: (i, j)),
        out_shape=jax.ShapeDtypeStruct((t, W_MIX), f32),
        compiler_params=_cparams(("parallel", "parallel")),
    )(z, z, cw, cb)


def rg_conv_bwd(dxc, z, cw):
    t = z.shape[0]
    tb = min(RG_TB, t)
    nt = t // tb

    def body(g_ref, gn_ref, x_ref, h_ref, cw_ref, dx_ref, dcw_ref, dcb_ref):
        i = pl.program_id(1)
        g = g_ref[...]
        gext = jnp.concatenate([g, gn_ref[...] * (i < nt - 1).astype(f32)], axis=0)
        xext = jnp.concatenate([h_ref[...] * (i > 0).astype(f32), x_ref[...]], axis=0)
        w = cw_ref[0]
        dx = w[3:4, :] * g
        rows = [None] * RG_CONV
        rows[3] = jnp.sum(g * xext[8:, :], axis=0, keepdims=True)
        for k in range(3):
            s = 3 - k
            dx = dx + w[k:k + 1, :] * pltpu.roll(gext, tb + 8 - s, 0)[:tb, :]
            rows[k] = jnp.sum(g * pltpu.roll(xext, s, 0)[8:, :], axis=0, keepdims=True)
        dx_ref[...] = dx.astype(dx_ref.dtype)
        dcw = jnp.concatenate(rows, axis=0)
        dcb = jnp.sum(g, axis=0, keepdims=True)

        @pl.when(i == 0)
        def _():
            dcw_ref[0] = dcw
            dcb_ref[0] = dcb

        @pl.when(i > 0)
        def _():
            dcw_ref[0] = dcw_ref[0] + dcw
            dcb_ref[0] = dcb_ref[0] + dcb

    base = C_XB // LANE
    chunk = lambda r: pl.BlockSpec((1, r, LANE), lambda j, i: (j, 0, 0))
    return pl.pallas_call(
        body, name="rg_conv_bwd", grid=(NCH, nt),
        in_specs=[pl.BlockSpec((tb, LANE), lambda j, i: (i, j)),
                  pl.BlockSpec((8, LANE), lambda j, i: (jnp.minimum((i + 1) * (tb // 8), t // 8 - 1), j)),
                  pl.BlockSpec((tb, LANE), lambda j, i: (i, base + j)),
                  pl.BlockSpec((8, LANE), lambda j, i: (jnp.maximum(i * (tb // 8) - 1, 0), base + j)),
                  chunk(RG_CONV)],
        out_specs=[pl.BlockSpec((tb, LANE), lambda j, i: (i, j)), chunk(RG_CONV), chunk(1)],
        out_shape=[jax.ShapeDtypeStruct((t, W_MIX), bf16), jax.ShapeDtypeStruct((NCH, RG_CONV, LANE), f32),
                   jax.ShapeDtypeStruct((NCH, 1, LANE), f32)],
        compiler_params=_cparams(("parallel", "arbitrary")),
    )(dxc, dxc, z, z, cw)


def rg_scan_fwd(a, b):
    t = a.shape[0]
    tb = min(RG_TB, t)

    def body(a_ref, b_ref, h_ref, car):
        @pl.when(pl.program_id(1) == 0)
        def _():
            car[...] = jnp.zeros_like(car)

        h_ref[...], car[...] = tv_scan(a_ref[...], b_ref[...], car[...])

    spec = pl.BlockSpec((tb, LANE), lambda j, i: (i, j))
    return pl.pallas_call(
        body, name="rg_scan_fwd", grid=(NCH, t // tb), in_specs=[spec, spec], out_specs=spec,
        out_shape=jax.ShapeDtypeStruct((t, W_MIX), f32), scratch_shapes=[pltpu.VMEM((1, LANE), f32)],
        compiler_params=_cparams(("parallel", "arbitrary")),
    )(a, b)


def rg_scan_bwd(dh, a, h):
    t = a.shape[0]
    tb = min(RG_TB, t)
    nt = t // tb

    def body(g_ref, a_ref, an_ref, h_ref, hp_ref, da_ref, db_ref, car):
        step = pl.program_id(1)
        tt = nt - 1 - step

        @pl.when(step == 0)
        def _():
            car[...] = jnp.zeros_like(car)

        row = lax.broadcasted_iota(jnp.int32, (tb, 1), 0)
        an = an_ref[0:1, :] * (tt < nt - 1).astype(f32)
        aa = jnp.where(row == tb - 1, an, pltpu.roll(a_ref[...], tb - 1, 0))
        lam, car[...] = tv_scan(aa, g_ref[...], car[...], reverse=True)
        hp = jnp.where(row == 0, hp_ref[7:8, :] * (tt > 0).astype(f32), pltpu.roll(h_ref[...], 1, 0))
        da_ref[...] = lam * hp
        db_ref[...] = lam

    rev = pl.BlockSpec((tb, LANE), lambda j, i: (nt - 1 - i, j))
    nxt = pl.BlockSpec((8, LANE), lambda j, i: (jnp.minimum((nt - i) * (tb // 8), t // 8 - 1), j))
    prv = pl.BlockSpec((8, LANE), lambda j, i: (jnp.maximum((nt - 1 - i) * (tb // 8) - 1, 0), j))
    return pl.pallas_call(
        body, name="rg_scan_bwd", grid=(NCH, nt), in_specs=[rev, rev, nxt, rev, prv], out_specs=[rev, rev],
        out_shape=[jax.ShapeDtypeStruct((t, W_MIX), f32)] * 2, scratch_shapes=[pltpu.VMEM((1, LANE), f32)],
        compiler_params=_cparams(("parallel", "arbitrary")),
    )(dh, a, a, h, h)


HG_TB = 256
HALF = HG_SUB // 2


def _heads(v):
    return jnp.stack([v[:, LANE * h:LANE * (h + 1)] for h in range(HG_HEADS)])


def _unheads(v):
    return jnp.concatenate([v[h] for h in range(HG_HEADS)], axis=-1)


def _bmm(eq, a, b):
    return jnp.einsum(eq, a.astype(bf16), b.astype(bf16), preferred_element_type=f32)


def hg_chunk_fwd(qs, kk, gcum, z):
    t = qs.shape[0]
    tb = min(HG_TB, t)
    nc = tb // HG_SUB

    def body(q_ref, k_ref, g_ref, v_ref, o_ref, sall_ref, st_ref):
        @pl.when(pl.program_id(0) == 0)
        def _():
            st_ref[...] = jnp.zeros_like(st_ref)

        ri = lax.broadcasted_iota(jnp.int32, (1, HALF, 1), 1)

        def chunk(c, carry):
            rows = pl.ds(pl.multiple_of(c * HG_SUB, HG_SUB), HG_SUB)
            q, k, g, v = _heads(q_ref[rows, :]), _heads(k_ref[rows, :]), _heads(g_ref[rows, :]), _heads(v_ref[rows, :])
            st = st_ref[...]
            sall_ref[c] = st
            o = _bmm('htk,hvk->htv', q * jnp.exp(g), st)
            halves = [[q[:, :HALF], g[:, :HALF], o[:, :HALF]], [q[:, HALF:], g[:, HALF:], o[:, HALF:]]]
            for s in range(HG_SUB):
                grow, krow, vrow = g[:, s:s + 1, :], k[:, s:s + 1, :], v[:, s:s + 1, :]
                for h in range(s // HALF, 2):
                    qh, gh, oh = halves[h]
                    p = jnp.exp(jnp.minimum(gh - grow, 0.0))
                    if s // HALF == h:
                        p = jnp.where(ri >= s - h * HALF, p, 0.0)
                    halves[h][2] = oh + jnp.sum(qh * krow * p, axis=-1, keepdims=True) * vrow
            o = jnp.concatenate([halves[0][2], halves[1][2]], axis=1)
            gl = g[:, HG_SUB - 1:HG_SUB, :]
            st_ref[...] = st * jnp.exp(gl) + _bmm('htv,htk->hvk', v, k * jnp.exp(gl - g))
            o_ref[rows, :] = _unheads(o)
            return carry

        lax.fori_loop(0, nc, chunk, 0)

    spec = lambda base=0: pl.BlockSpec((tb, W_MIX), lambda i: (i, base))
    return pl.pallas_call(
        body, name="hg_chunk_fwd", grid=(t // tb,),
        in_specs=[spec(), spec(), spec(), spec(C_I // W_MIX)],
        out_specs=[spec(), pl.BlockSpec((nc, HG_HEADS, HG_DK, HG_DK), lambda i: (i, 0, 0, 0))],
        out_shape=[jax.ShapeDtypeStruct((t, W_MIX), f32),
                   jax.ShapeDtypeStruct((t // HG_SUB, HG_HEADS, HG_DK, HG_DK), f32)],
        scratch_shapes=[pltpu.VMEM((HG_HEADS, HG_DK, HG_DK), f32)],
        compiler_params=_cparams(("arbitrary",)),
    )(qs, kk, gcum, z)


def hg_chunk_bwd(do, qs, kk, gcum, z, sall):
    t = qs.shape[0]
    tb = min(HG_TB, t)
    nc = tb // HG_SUB
    nt = t // tb

    def body(do_ref, q_ref, k_ref, g_ref, v_ref, sall_ref, dq_ref, dk_ref, dg_ref, dv_ref, dst_ref):
        @pl.when(pl.program_id(0) == 0)
        def _():
            dst_ref[...] = jnp.zeros_like(dst_ref)

        ri = lax.broadcasted_iota(jnp.int32, (1, HALF, 1), 1)
        ri_chunk = lax.broadcasted_iota(jnp.int32, (1, HG_SUB, 1), 1)

        def chunk(cc, carry):
            c = nc - 1 - cc
            rows = pl.ds(pl.multiple_of(c * HG_SUB, HG_SUB), HG_SUB)
            q, k, g, v = _heads(q_ref[rows, :]), _heads(k_ref[rows, :]), _heads(g_ref[rows, :]), _heads(v_ref[rows, :])
            d_o = _heads(do_ref[rows, :])
            st = sall_ref[c]
            dsn = dst_ref[...]
            eg = jnp.exp(g)
            qe = q * eg
            gl = g[:, HG_SUB - 1:HG_SUB, :]
            egl = jnp.exp(gl)
            dec = jnp.exp(gl - g)
            kd = k * dec
            dqe = _bmm('htv,hvk->htk', d_o, st)
            dst_ref[...] = _bmm('htv,htk->hvk', d_o, qe) + dsn * egl
            dgl_dec = jnp.sum(dsn * st, axis=1, keepdims=True) * egl
            dv = _bmm('htk,hvk->htv', kd, dsn)
            dkd = _bmm('htv,hvk->htk', v, dsn)
            tq, tg, tdo = [q[:, :HALF], q[:, HALF:]], [g[:, :HALF], g[:, HALF:]], [d_o[:, :HALF], d_o[:, HALF:]]
            a1 = [jnp.zeros_like(tq[0]), jnp.zeros_like(tq[1])]
            a2 = [jnp.zeros_like(tq[0]), jnp.zeros_like(tq[1])]
            dvh = [dv[:, :HALF], dv[:, HALF:]]
            for s in range(HG_SUB):
                grow, krow, vrow = g[:, s:s + 1, :], k[:, s:s + 1, :], v[:, s:s + 1, :]
                sh, sr = s // HALF, s % HALF
                dv_s, a2_s = 0.0, 0.0
                for h in range(sh, 2):
                    p = jnp.exp(jnp.minimum(tg[h] - grow, 0.0))
                    if sh == h:
                        p = jnp.where(ri >= sr, p, 0.0)
                    col = jnp.sum(tq[h] * krow * p, axis=-1, keepdims=True)
                    t1 = jnp.sum(tdo[h] * vrow, axis=-1, keepdims=True) * p
                    a1[h] = a1[h] + t1 * krow
                    dv_s = dv_s + jnp.sum(col * tdo[h], axis=1, keepdims=True)
                    a2_s = a2_s + jnp.sum(t1 * tq[h], axis=1, keepdims=True)
                dvh[sh] = jnp.where(ri == sr, dvh[sh] + dv_s, dvh[sh])
                a2[sh] = jnp.where(ri == sr, a2_s, a2[sh])
            a1, a2 = jnp.concatenate(a1, axis=1), jnp.concatenate(a2, axis=1)
            dv = jnp.concatenate(dvh, axis=1)
            dgl = jnp.sum(dkd * kd, axis=1, keepdims=True) + dgl_dec
            dg = dqe * qe + q * a1 - k * a2 - dkd * kd
            dg = jnp.where(ri_chunk == HG_SUB - 1, dg + dgl, dg)
            dq_ref[rows, :] = _unheads(dqe * eg + a1)
            dk_ref[rows, :] = _unheads(dkd * dec + a2)
            dg_ref[rows, :] = _unheads(dg)
            dv_ref[rows, :] = _unheads(dv).astype(dv_ref.dtype)
            return carry

        lax.fori_loop(0, nc, chunk, 0)

    spec = lambda base=0: pl.BlockSpec((tb, W_MIX), lambda i: (nt - 1 - i, base))
    return pl.pallas_call(
        body, name="hg_chunk_bwd", grid=(nt,),
        in_specs=[spec(), spec(), spec(), spec(), spec(C_I // W_MIX),
                  pl.BlockSpec((nc, HG_HEADS, HG_DK, HG_DK), lambda i: (nt - 1 - i, 0, 0, 0))],
        out_specs=[spec(), spec(), spec(), spec()],
        out_shape=[jax.ShapeDtypeStruct((t, W_MIX), f32)] * 3 + [jax.ShapeDtypeStruct((t, W_MIX), bf16)],
        scratch_shapes=[pltpu.VMEM((HG_HEADS, HG_DK, HG_DK), f32)],
        compiler_params=_cparams(("arbitrary",)),
    )(do, qs, kk, gcum, z, sall)


def adamw(w, m, v, slots, *, name, tr):
    nl, r, c = w.shape
    tr = min(tr, r)
    flat = [a for per_layer in slots for a in per_layer]
    c1 = 1.0 / (1.0 - ADAM_B1 ** ADAM_STEP)
    c2 = 1.0 / (1.0 - ADAM_B2 ** ADAM_STEP)

    def body(*refs):
        w_ref, m_ref, v_ref = refs[:3]
        s_refs = list(refs[3:3 + len(flat)])
        g_ref, d_ref, mo_ref, vo_ref = refs[3 + len(flat):]
        for l in range(nl):
            parts = [s_refs.pop(0) for _ in slots[l]]
            g = None
            for p in parts:
                for s in range(p.shape[0]):
                    term = p[s].astype(f32)
                    g = term if g is None else g + term
            mn = ADAM_B1 * m_ref[l] + (1.0 - ADAM_B1) * g
            vn = ADAM_B2 * v_ref[l] + (1.0 - ADAM_B2) * (g * g)
            g_ref[l] = g
            mo_ref[l] = mn
            vo_ref[l] = vn
            d_ref[l] = -ADAM_LR * ((mn * c1) / (jnp.sqrt(vn * c2) + ADAM_EPS) + ADAM_WD * w_ref[l])

    full = pl.BlockSpec((nl, tr, c), lambda i: (0, i, 0))
    slot = [pl.BlockSpec((a.shape[0], tr, c), lambda i: (0, i, 0)) for a in flat]
    return pl.pallas_call(
        body, name=name, grid=(r // tr,), in_specs=[full] * 3 + slot, out_specs=[full] * 4,
        out_shape=[jax.ShapeDtypeStruct(w.shape, f32)] * 4, compiler_params=_cparams(("parallel",)),
    )(w, m, v, *flat)


def _slab(ref, axis, idx, n):
    return ref.at[tuple([slice(None)] * axis + [pl.ds(idx * n, n)])]


def all_gather(x, axis, *, name):
    n = x.shape[axis]
    out_shape = x.shape[:axis] + (N_DEV * n,) + x.shape[axis + 1:]

    def body(x_ref, out_ref, send_sems, recv_sems, local_sem):
        xx, yy, cc = lax.axis_index("x"), lax.axis_index("y"), lax.axis_index("c")
        me, sibling = (xx, yy, cc), (xx, yy, 1 - cc)
        chips = [(1 - xx, yy), (xx, 1 - yy), (1 - xx, 1 - yy)]

        def slab(px, py, pc):
            return _slab(out_ref, axis, 4 * px + 2 * py + pc, n)

        def copy(k, block, to, src=None):
            return pltpu.make_async_remote_copy(
                src_ref=slab(*block) if src is None else src, dst_ref=slab(*block),
                send_sem=send_sems.at[k], recv_sem=recv_sems.at[k], device_id=to, device_id_type=MESH)

        mine = pltpu.make_async_copy(x_ref, slab(*me), local_sem)
        mine.start()
        first = [copy(0, me, sibling, src=x_ref)]
        first += [copy(1 + j, me, (*chip, cc), src=x_ref) for j, chip in enumerate(chips)]
        for cp in first:
            cp.start()
        passed = [copy(4 + j, (*chip, cc), sibling) for j, chip in enumerate(chips)]
        for j, chip in enumerate(chips):
            copy(1 + j, (*chip, cc), me).wait_recv()
            passed[j].start()
        copy(0, sibling, me).wait_recv()
        for j, chip in enumerate(chips):
            copy(4 + j, (*chip, 1 - cc), me).wait_recv()
        for cp in first + passed:
            cp.wait_send()
        mine.wait()

    return pl.pallas_call(
        body, name=name, out_shape=jax.ShapeDtypeStruct(out_shape, x.dtype), in_specs=[ANY], out_specs=ANY,
        scratch_shapes=[pltpu.SemaphoreType.DMA((7,)), pltpu.SemaphoreType.DMA((7,)), pltpu.SemaphoreType.DMA],
    )(x)


N_CHIP = 4


HBM = pl.BlockSpec(memory_space=pltpu.HBM)
SEM = pl.BlockSpec(memory_space=pltpu.SEMAPHORE)
EFFECT = pltpu.SideEffectType.DATAFLOW_SIDE_EFFECTING
TOKEN = jax.ShapeDtypeStruct((8, LANE), f32)


def _in_hbm(a):
    return pltpu.with_memory_space_constraint(a, pltpu.HBM)


def pair_sums(g, axis, *, name):
    n = g.shape[axis] // N_DEV
    slab_shape = g.shape[:axis] + (n,) + g.shape[axis + 1:]
    cols = slab_shape[-1]
    rows = math.prod(slab_shape[:-1])
    col_slabs = axis == g.ndim - 1
    assert col_slabs or (axis == 0 and g.ndim == 2)

    def swap_body(g_ref, got_ref, send_sems, recv_sems):
        xx, yy, cc = lax.axis_index("x"), lax.axis_index("y"), lax.axis_index("c")
        copies = [pltpu.make_async_remote_copy(
            src_ref=_slab(g_ref, axis, 2 * q + 1 - cc, n), dst_ref=got_ref.at[q],
            send_sem=send_sems.at[q], recv_sem=recv_sems.at[q], device_id=(xx, yy, 1 - cc), device_id_type=MESH)
            for q in range(N_CHIP)]
        for cp in copies:
            cp.start()
        for cp in copies:
            cp.wait()

    got = pl.pallas_call(
        swap_body, name=name + "_swap", out_shape=jax.ShapeDtypeStruct((N_CHIP,) + slab_shape, g.dtype),
        in_specs=[ANY], out_specs=ANY, scratch_shapes=[pltpu.SemaphoreType.DMA((N_CHIP,))] * 2,
    )(g)

    tr = min(256, rows)

    def add_body(a0_ref, a1_ref, b_ref, pair_ref, own_ref):
        xx, yy, cc = lax.axis_index("x"), lax.axis_index("y"), lax.axis_index("c")
        mine = jnp.where(cc == 0, a0_ref[...], a1_ref[...])
        s = (mine.astype(f32) + b_ref[0].astype(f32)).astype(bf16)
        pair_ref[0] = s

        @pl.when(pl.program_id(1) == 2 * xx + yy)
        def _():
            own_ref[0] = s

    if col_slabs:
        a_spec = lambda c: pl.BlockSpec((tr, cols), lambda i, q: (i, 2 * q + c))
    else:
        a_spec = lambda c: pl.BlockSpec((tr, cols), lambda i, q: ((2 * q + c) * (n // tr) + i, 0))
    by_chip = pl.BlockSpec((1, tr, cols), lambda i, q: (q, i, 0))
    g2 = g.reshape(-1, g.shape[-1])
    pair, own = pl.pallas_call(
        add_body, name=name + "_add", grid=(rows // tr, N_CHIP), in_specs=[a_spec(0), a_spec(1), by_chip],
        out_specs=[by_chip, pl.BlockSpec((1, tr, cols), lambda i, q: (0, i, 0))],
        out_shape=[jax.ShapeDtypeStruct((N_CHIP, rows, cols), bf16), jax.ShapeDtypeStruct((1, rows, cols), bf16)],
        compiler_params=_cparams(("parallel", "arbitrary")),
    )(g2, g2, got.reshape(N_CHIP, rows, cols))
    return own, pair


def _send_copies(p_refs, land_refs, send_sems, recv_sems):
    xx, yy, cc = lax.axis_index("x"), lax.axis_index("y"), lax.axis_index("c")
    copies = []
    for t, (p, land) in enumerate(zip(p_refs, land_refs)):
        for k in range(1, N_CHIP):
            px = 1 - xx if k & 2 else xx
            py = 1 - yy if k & 1 else yy
            s = (N_CHIP - 1) * t + k - 1
            copies.append(pltpu.make_async_remote_copy(
                src_ref=p.at[2 * px + py], dst_ref=land.at[k - 1], send_sem=send_sems.at[s], recv_sem=recv_sems.at[s],
                device_id=(px, py, cc), device_id_type=MESH))
    return copies


def send_pairs_start(pairs, after, *, name):
    nt = len(pairs)
    lands = [lax.empty((N_CHIP - 1,) + p.shape[1:], p.dtype) for p in pairs]

    def body(*refs):
        p_refs, land_refs = refs[:nt], refs[nt:2 * nt]
        send_sems, recv_sems = refs[2 * nt + 1], refs[2 * nt + 2]
        token = refs[-1]
        for cp in _send_copies(p_refs, land_refs, send_sems, recv_sems):
            cp.start()
        token[...] = jnp.zeros_like(token)

    nsem = (N_CHIP - 1) * nt
    outs = pl.pallas_call(
        body, name=name,
        out_shape=(pltpu.SemaphoreType.DMA((nsem,)), pltpu.SemaphoreType.DMA((nsem,)))
        + tuple(pltpu.HBM(a.shape, a.dtype) for a in list(pairs) + lands) + (TOKEN,),
        in_specs=[HBM] * (2 * nt) + [ANY], out_specs=(SEM, SEM) + (HBM,) * (2 * nt) + (VM,),
        input_output_aliases={i: 2 + i for i in range(2 * nt)},
        compiler_params=pltpu.CompilerParams(has_side_effects=EFFECT),
    )(*[_in_hbm(a) for a in list(pairs) + lands], after)
    return outs[:-1], outs[-1]


def send_pairs_wait(handles, after, *, name):
    send_sems, recv_sems = handles[0], handles[1]
    bufs = handles[2:]
    nt = len(bufs) // 2

    def body(*refs):
        p_refs, land_refs = refs[:nt], refs[nt:2 * nt]
        send_sems, recv_sems = refs[2 * nt], refs[2 * nt + 1]
        for cp in _send_copies(p_refs, land_refs, send_sems, recv_sems):
            cp.wait_send()
            cp.wait_recv()

    outs = pl.pallas_call(
        body, name=name, out_shape=tuple(pltpu.HBM(a.shape, a.dtype) for a in bufs),
        in_specs=[HBM] * (2 * nt) + [SEM, SEM, ANY], out_specs=(HBM,) * (2 * nt),
        input_output_aliases={i: i for i in range(2 * nt)},
        compiler_params=pltpu.CompilerParams(has_side_effects=EFFECT),
    )(*bufs, send_sems, recv_sems, after)
    return outs[nt:]


def _gather_copies(x_refs, land_refs, axes, send_sems, recv_sems):
    xx, yy, cc = lax.axis_index("x"), lax.axis_index("y"), lax.axis_index("c")
    me = 4 * xx + 2 * yy + cc
    copies = []
    for t, (x_ref, land, axis) in enumerate(zip(x_refs, land_refs, axes)):
        n = x_ref.shape[axis]
        for k in range(1, N_DEV):
            px = 1 - xx if k & 4 else xx
            py = 1 - yy if k & 2 else yy
            pc = 1 - cc if k & 1 else cc
            s = (N_DEV - 1) * t + k - 1
            copies.append(pltpu.make_async_remote_copy(
                src_ref=x_ref, dst_ref=_slab(land, axis, me, n), send_sem=send_sems.at[s], recv_sem=recv_sems.at[s],
                device_id=(px, py, pc), device_id_type=MESH))
    return copies


def _place_own(x, axis, *, name):
    full = x.shape[:axis] + (N_DEV * x.shape[axis],) + x.shape[axis + 1:]
    lead = x.shape[0]
    tile = lead if axis == 0 else (256 if x.ndim == 2 and lead % 256 == 0 else 1 if x.ndim == 3 else lead)
    rest = (0,) * (x.ndim - 1)

    def body(land_ref, x_ref, o_ref):
        o_ref[...] = x_ref[...]

    def where(i):
        me = 4 * lax.axis_index("x") + 2 * lax.axis_index("y") + lax.axis_index("c")
        idx = [i] + list(rest)
        idx[axis] = me
        return tuple(idx)

    return pl.pallas_call(
        body, name=name, grid=(lead // tile,),
        in_specs=[ANY, pl.BlockSpec((tile,) + x.shape[1:], lambda i: (i,) + rest)],
        out_specs=pl.BlockSpec((tile,) + x.shape[1:], where),
        out_shape=jax.ShapeDtypeStruct(full, x.dtype), input_output_aliases={0: 0},
        compiler_params=_cparams(("arbitrary",)),
    )(lax.empty(full, x.dtype), x)


def gather_start(xs, axes, after, *, name):
    nt = len(xs)
    lands = [_place_own(x, axis, name=name + "_own") for x, axis in zip(xs, axes)]

    def body(*refs):
        x_refs, land_refs = refs[:nt], refs[nt:2 * nt]
        send_sems, recv_sems = refs[2 * nt + 1], refs[2 * nt + 2]
        token = refs[-1]
        for cp in _gather_copies(x_refs, land_refs, axes, send_sems, recv_sems):
            cp.start()
        token[...] = jnp.zeros_like(token)

    nsem = (N_DEV - 1) * nt
    outs = pl.pallas_call(
        body, name=name,
        out_shape=(pltpu.SemaphoreType.DMA((nsem,)), pltpu.SemaphoreType.DMA((nsem,)))
        + tuple(pltpu.HBM(a.shape, a.dtype) for a in list(xs) + lands) + (TOKEN,),
        in_specs=[HBM] * (2 * nt) + [ANY], out_specs=(SEM, SEM) + (HBM,) * (2 * nt) + (VM,),
        input_output_aliases={i: 2 + i for i in range(2 * nt)},
        compiler_params=pltpu.CompilerParams(has_side_effects=EFFECT),
    )(*[_in_hbm(a) for a in list(xs) + lands], after)
    return outs[:-1], outs[-1]


def gather_wait(handles, axes, after, *, name):
    send_sems, recv_sems = handles[0], handles[1]
    bufs = handles[2:]
    nt = len(bufs) // 2

    def body(*refs):
        x_refs, land_refs = refs[:nt], refs[nt:2 * nt]
        send_sems, recv_sems = refs[2 * nt], refs[2 * nt + 1]
        for cp in _gather_copies(x_refs, land_refs, axes, send_sems, recv_sems):
            cp.wait_send()
            cp.wait_recv()

    outs = pl.pallas_call(
        body, name=name, out_shape=tuple(pltpu.HBM(a.shape, a.dtype) for a in bufs),
        in_specs=[HBM] * (2 * nt) + [SEM, SEM, ANY], out_specs=(HBM,) * (2 * nt),
        input_output_aliases={i: i for i in range(2 * nt)},
        compiler_params=pltpu.CompilerParams(has_side_effects=EFFECT),
    )(*bufs, send_sems, recv_sems, after)
    return outs[nt:]


def _blockdiag(b, nb):
    j, _, r, c = b.shape
    eye = jnp.eye(nb, dtype=bool)[None, :, None, :, None]
    return jnp.where(eye, b[:, :, :, None, :], jnp.zeros((), b.dtype)).reshape(j, nb * r, nb * c)


def _diagblocks(d, nb):
    j, rr, cc = d.shape
    return jnp.einsum('jarac->jarc', d.reshape(j, nb, rr // nb, nb, cc // nb))


def _s5_b_dense(bbar):
    return _blockdiag(bbar.transpose(0, 2, 1).reshape(NCH, 8, S5_GROUP, S5_STATE), 8)


def _s5_b_undense(d):
    return _diagblocks(d, 8).reshape(S5_GROUPS, S5_GROUP, S5_STATE).transpose(0, 2, 1)


def _s5_c_dense(c):
    return _blockdiag(c.transpose(0, 2, 1).reshape(NCH, 8, S5_STATE, S5_GROUP), 8)


def _s5_c_undense(d):
    return _diagblocks(d, 8).reshape(S5_GROUPS, S5_STATE, S5_GROUP).transpose(0, 2, 1)


def _rg_dense(w):
    return _blockdiag(w.reshape(NCH, 2, RG_BLOCK, RG_BLOCK), 2)


def _rg_undense(d):
    return _diagblocks(d, 2).reshape(RG_BLOCKS, RG_BLOCK, RG_BLOCK)


def _chunks(v):
    return v.reshape(NCH, 1, LANE)


def _tri(tm):
    r = jnp.arange(tm)
    m = (r[:, None] >= r[None, :]) & (r[:, None] // HG_SUB == r[None, :] // HG_SUB)
    m = m.astype(f32)
    return m[None], m.T[None]


SMALL = ['norm_w', 's5_lambda_re', 's5_lambda_im', 's5_log_step', 's5_b_re', 's5_b_im', 's5_c_re', 's5_c_im',
         's5_d', 's5_b_glu', 'rg_conv_w', 'rg_conv_b', 'rg_w_a', 'rg_b_a', 'rg_w_x', 'rg_b_x', 'rg_lambda',
         'hg_lower_bounds', 'hg_norm_w', 'final_norm_w']
WEIGHTS = ['norm_w', 'w_in', 's5_lambda_re', 's5_lambda_im', 's5_log_step', 's5_b_re', 's5_b_im', 's5_c_re',
           's5_c_im', 's5_d', 's5_w_glu', 's5_b_glu', 'rg_conv_w', 'rg_conv_b', 'rg_w_a', 'rg_b_a', 'rg_w_x',
           'rg_b_x', 'rg_lambda', 'hg_lower_bounds', 'hg_norm_w', 'w_branch', 'w_out', 'final_norm_w']
PACK_ROWS = 512


def _pack(arrs):
    flat = jnp.concatenate([a.reshape(-1) for a in arrs])
    pad = (-flat.shape[0]) % (PACK_ROWS * LANE)
    return jnp.pad(flat, (0, pad)).reshape(1, -1, LANE)


def _unpack(buf, shapes):
    flat = buf.reshape(-1)
    out, off = [], 0
    for s in shapes:
        n = math.prod(s)
        out.append(flat[off:off + n].reshape(s))
        off += n
    return out


def _step(x, tgt, w, m, v):
    t = x.shape[0]
    tri, tri_t = _tri(min(LANE, t))
    me = 4 * lax.axis_index("x") + 2 * lax.axis_index("y") + lax.axis_index("c")

    big = ('w_in', 's5_w_glu', 'w_branch', 'w_out')
    big_axis = (1, 0, 2, 0)
    shards = lambda l: [w[k][l].astype(bf16) for k in big]
    win, wglu, wbr, wout = ([None] * DEPTH for _ in range(4))
    win[0] = all_gather(shards(0)[0], big_axis[0], name="ag_w_in")
    rest0_axis = big_axis[1:] + (1,)
    rest0, rest0_token = gather_start(shards(0)[1:] + [w['rg_conv_w'].reshape(DEPTH * RG_CONV, LANE)], rest0_axis,
                                      win[0], name="ag_start_0")

    lb_rows = [w['hg_lower_bounds'][l][None] for l in range(DEPTH)]
    lbs = whole(lb_prep_fn, lb_rows, [(1, W_MIX)] * DEPTH, name="lb_prep")

    saved = []
    for l in range(DEPTH):
        s = {}
        nw = w['norm_w'][l].reshape(1, 1, D_MODEL)
        (h,) = rowwise(ln_fn, [(x, 0, D_MODEL)], [nw], [], [(D_MODEL, bf16)], name="ln_fwd")
        token = None
        if l + 1 < DEPTH:
            handles, token = gather_start(shards(l + 1), big_axis, rest0_token if l == 0 else x, name=f"ag_start_{l + 1}")
        z = mm(h, win[l], after=token, name="mm_in", tm=1024)
        if l == 0:
            wglu[0], wbr[0], wout[0], conv_w = gather_wait(rest0, rest0_axis, z, name="ag_wait_0")
            conv_w = conv_w.reshape(DEPTH, RG_CONV, W_MIX)
        s5p = [w['s5_lambda_re'][l][..., None], w['s5_lambda_im'][l][..., None], w['s5_log_step'][l][:, None, None],
               w['s5_b_re'][l], w['s5_b_im'][l]]
        gp = (S5_GROUPS, S5_STATE)
        abar_re, abar_im, bbar_re, bbar_im = whole(
            s5_prep_fn, s5p, [gp + (1,), gp + (1,), gp + (S5_GROUP,), gp + (S5_GROUP,)], name="s5_prep")
        a_re, a_im = abar_re.reshape(NCH, 1, S5_SC), abar_im.reshape(NCH, 1, S5_SC)
        bd_re, bd_im = _s5_b_dense(bbar_re), _s5_b_dense(bbar_im)
        cd_re, cd_im = _s5_c_dense(w['s5_c_re'][l]), _s5_c_dense(w['s5_c_im'][l])
        yssm, xre, xim = s5_scan_fwd(z, bd_re, bd_im, cd_re, cd_im, a_re, a_im)
        s5post_p = [w['s5_d'][l].reshape(1, 1, W_MIX), wglu[l].astype(f32)[None], w['s5_b_glu'][l].reshape(1, 1, W_MIX)]
        s5post_rows = [(yssm, 0, W_MIX), (z, C_UA, W_MIX), (z, C_GA, W_MIX)]
        (ya,) = rowwise(s5_post_fn, s5post_rows, s5post_p, [], [(W_MIX, bf16)], name="s5_post_fwd")
        cw, cb = conv_w[l].reshape(RG_CONV, NCH, LANE).transpose(1, 0, 2), _chunks(w['rg_conv_b'][l])
        xc = rg_conv_fwd(z, cw, cb)
        rg_p = [_rg_dense(w['rg_w_a'][l]), _chunks(w['rg_b_a'][l]), _rg_dense(w['rg_w_x'][l]),
                _chunks(w['rg_b_x'][l]), _chunks(w['rg_lambda'][l])]
        ra, rb = rowwise(rg_gate_fn, [(xc, 0, W_MIX)], rg_p, [], [(W_MIX, f32)] * 2, name="rg_gate_fwd",
                         ncol=NCH, tm=TM_CHUNK, rowid=True)
        hb = rg_scan_fwd(ra, rb)
        hg_rows = [(z, C_Q, W_MIX), (z, C_F, W_MIX)]
        hg_p = [_chunks(lbs[l].reshape(W_MIX))]
        qs, kk, gcum = rowwise(hg_pre_fn, hg_rows, hg_p, [tri, tri_t], [(W_MIX, f32)] * 3, name="hg_pre_fwd", ncol=NCH, tm=TM_CHUNK)
        oc, sall = hg_chunk_fwd(qs, kk, gcum, z)
        bp_rows = [(hb, 0, W_MIX), (z, C_GB, W_MIX), (oc, 0, W_MIX), (z, C_GC, W_MIX)]
        bp_p = [_chunks(w['hg_norm_w'][l])]
        yb, yc = rowwise(branch_prep_fn, bp_rows, bp_p, [], [(W_MIX, bf16)] * 2, name="branch_prep_fwd", ncol=NCH, tm=TM_CHUNK)
        ys = [ya, yb, yc]
        br = [mm(ys[n], wbr[l][n], name="mm_branch", out_dtype=bf16) for n in range(N_BRANCH)]
        mg_rows = [(br[n], 0, D_MODEL) for n in range(N_BRANCH)] + [(z, C_GATE + n * D_MODEL, D_MODEL) for n in range(N_BRANCH)]
        (merged,) = rowwise(merge_fn, mg_rows, [], [], [(D_MODEL, bf16)], name="merge_fwd", ncol=2)
        x_new = mm(merged, wout[l], add=x, name="mm_out")
        s.update(x=x, h=h, z=z, s5p=s5p, s5=(bd_re, bd_im, cd_re, cd_im, a_re, a_im), xre=xre, xim=xim,
                 s5post_rows=s5post_rows, s5post_p=s5post_p, cw=cw, xc=xc, rg_p=rg_p, ra=ra, hb=hb,
                 hg_rows=hg_rows, hg_p=hg_p, qs=qs, kk=kk, gcum=gcum, sall=sall, bp_rows=bp_rows, bp_p=bp_p,
                 ys=ys, mg_rows=mg_rows, merged=merged, nw=nw)
        saved.append(s)
        x = x_new
        if l + 1 < DEPTH:
            win[l + 1], wglu[l + 1], wbr[l + 1], wout[l + 1] = gather_wait(handles, big_axis, x, name=f"ag_wait_{l + 1}")

    fnw = w['final_norm_w'].reshape(1, 1, D_MODEL)
    ones = jnp.ones((t, 1), f32)
    dx, d_fnw, loss_sum = rowwise_vjp(loss_fn, [(x, 0, D_MODEL), (tgt, 0, D_MODEL)], [fnw], [], [(ones, 0, 1)],
                                      [(0, f32)], name="loss_head", sum_primal=0)
    loss = lax.psum(loss_sum.reshape(()), ("x", "y", "c"))

    small_g = {k: [None] * DEPTH for k in SMALL if k != 'final_norm_w'}
    own_sums, in_flight = [None] * DEPTH, [None] * DEPTH
    d_lbs = [None] * DEPTH
    token = None
    for l in reversed(range(DEPTH)):
        s = saved[l]
        z = s['z']
        dxb = dx.astype(bf16)
        d_merged = mm(dxb, wout[l], bt=True, after=token, name="mm_out_dx", out_dtype=bf16)
        d_wout = mm(s['merged'], dxb, at=True, name="mm_out_dw", out_dtype=bf16)
        mg = rowwise_vjp(merge_fn, s['mg_rows'], [], [], [(d_merged, 0, D_MODEL)],
                         [(n, bf16) for n in range(2 * N_BRANCH)], name="merge_bwd", ncol=2)
        d_br, d_gl = mg[:N_BRANCH], mg[N_BRANCH:]
        d_ys = [mm(d_br[n], wbr[l][n], bt=True, name="mm_branch_dx", out_dtype=bf16) for n in range(N_BRANCH)]
        d_wbr = jnp.stack([mm(s['ys'][n], d_br[n], at=True, name="mm_branch_dw", out_dtype=bf16) for n in range(N_BRANCH)])
        d_hb, d_gb, d_oc, d_gc, d_hnw = rowwise_vjp(
            branch_prep_fn, s['bp_rows'], s['bp_p'], [], [(d_ys[1], 0, W_MIX), (d_ys[2], 0, W_MIX)],
            [(0, f32), (1, bf16), (2, f32), (3, bf16)], name="branch_prep_bwd", ncol=NCH, tm=TM_CHUNK)
        small_g['hg_norm_w'][l] = d_hnw.reshape(W_MIX)
        d_qs, d_kk, d_gcum, d_i = hg_chunk_bwd(d_oc, s['qs'], s['kk'], s['gcum'], z, s['sall'])
        d_q, d_f, d_lb = rowwise_vjp(
            hg_pre_fn, s['hg_rows'], s['hg_p'], [tri, tri_t], [(d_qs, 0, W_MIX), (d_kk, 0, W_MIX), (d_gcum, 0, W_MIX)],
            [(0, bf16), (1, bf16)], name="hg_pre_bwd", ncol=NCH, tm=TM_CHUNK)
        d_lbs[l] = d_lb.reshape(1, W_MIX)
        d_ra, d_rb = rg_scan_bwd(d_hb, s['ra'], s['hb'])
        rg = rowwise_vjp(rg_gate_fn, [(s['xc'], 0, W_MIX)], s['rg_p'], [], [(d_ra, 0, W_MIX), (d_rb, 0, W_MIX)],
                         [(0, f32)], name="rg_gate_bwd", ncol=NCH, tm=TM_CHUNK, rowid=True)
        d_xc, d_wa, d_ba, d_wx, d_bx, d_lam = rg
        d_xb, d_cw, d_cb = rg_conv_bwd(d_xc, z, s['cw'])
        small_g['rg_w_a'][l], small_g['rg_w_x'][l] = _rg_undense(d_wa), _rg_undense(d_wx)
        small_g['rg_b_a'][l], small_g['rg_b_x'][l] = d_ba.reshape(W_MIX), d_bx.reshape(W_MIX)
        small_g['rg_lambda'][l] = d_lam.reshape(W_MIX)
        small_g['rg_conv_w'][l] = d_cw.transpose(1, 0, 2).reshape(RG_CONV, W_MIX)
        small_g['rg_conv_b'][l] = d_cb.reshape(W_MIX)
        d_yssm, d_u1, d_ga, d_d, d_wglu, d_bglu = rowwise_vjp(
            s5_post_fn, s['s5post_rows'], s['s5post_p'], [], [(d_ys[0], 0, W_MIX)],
            [(0, bf16), (1, bf16), (2, bf16)], name="s5_post_bwd")
        small_g['s5_d'][l], small_g['s5_b_glu'][l] = d_d.reshape(W_MIX), d_bglu.reshape(W_MIX)
        d_ua, d_bdre, d_bdim, d_cdre, d_cdim, d_are, d_aim = s5_scan_bwd(d_yssm, d_u1, z, s['xre'], s['xim'], *s['s5'])
        small_g['s5_c_re'][l], small_g['s5_c_im'][l] = _s5_c_undense(d_cdre), _s5_c_undense(d_cdim)
        gp = (S5_GROUPS, S5_STATE, 1)
        s5g = whole_vjp(s5_prep_fn, s['s5p'],
                        [d_are.reshape(gp), d_aim.reshape(gp), _s5_b_undense(d_bdre), _s5_b_undense(d_bdim)],
                        name="s5_prep_bwd")
        small_g['s5_lambda_re'][l] = s5g[0].reshape(S5_GROUPS, S5_STATE)
        small_g['s5_lambda_im'][l] = s5g[1].reshape(S5_GROUPS, S5_STATE)
        small_g['s5_log_step'][l] = s5g[2].reshape(S5_GROUPS)
        small_g['s5_b_re'][l], small_g['s5_b_im'][l] = s5g[3], s5g[4]
        dz = jnp.concatenate([d_ua, d_ga, d_xb, d_gb, d_q, d_f, d_i, d_gc] + list(d_gl), axis=1)
        d_win = mm(s['h'], dz, at=True, name="mm_in_dw", out_dtype=bf16)
        sums = [pair_sums(g, ax, name="rs_" + k) for g, ax, k in zip((d_win, d_wglu[0], d_wbr, d_wout), big_axis, big)]
        own_sums[l] = [own for own, _ in sums]
        in_flight[l], token = send_pairs_start([pair for _, pair in sums], s5g[0], name=f"rs_start_{l}")
        d_h = mm(dz, win[l], bt=True, after=token, name="mm_in_dx", tm=1024, tk=2048)
        dx, d_nw = rowwise_vjp(ln_res_fn, [(s['x'], 0, D_MODEL)], [s['nw']], [], [(d_h, 0, D_MODEL), (dx, 0, D_MODEL)],
                               [(0, f32)], name="ln_bwd")
        small_g['norm_w'][l] = d_nw.reshape(D_MODEL)
    d_lb_raw = whole_vjp(lb_prep_fn, lb_rows, d_lbs, name="lb_prep_bwd")
    small_g['hg_lower_bounds'] = [r.reshape(W_MIX) for r in d_lb_raw]

    g_small = [jnp.stack(small_g[k]) for k in SMALL if k != 'final_norm_w'] + [d_fnw.reshape(D_MODEL)]
    shapes = [g.shape for g in g_small]
    small_in_flight, small_token = gather_start([_pack(g_small)[0].astype(bf16)], (0,), dx, name="ag_small_start")
    res = {}

    arrived = [send_pairs_wait(in_flight[l], small_token, name=f"rs_wait_{l}") for l in range(DEPTH)]
    for i, (k, tr) in enumerate((('w_in', 32), ('s5_w_glu', 32), ('w_branch', 128), ('w_out', 32))):
        shp = w[k].shape
        r3 = lambda a: a.reshape(DEPTH, -1, shp[-1])
        slots = [[own_sums[l][i], arrived[l][i]] for l in range(DEPTH)]
        outs = adamw(r3(w[k]), r3(m[k]), r3(v[k]), slots, name="adamw_" + k, tr=tr)
        for kind, buf in zip(('grad', 'delta', 'new_m', 'new_v'), outs):
            res[kind + '_' + k] = buf.reshape(shp)

    (g_all,) = gather_wait(small_in_flight, (0,), outs[0], name="ag_small_wait")
    g_all = g_all.reshape(N_DEV, -1, LANE)

    def local(d, k):
        return jnp.zeros(shapes[SMALL.index(k)], f32) if k == 'rg_conv_w' else d[k]
    packed = [_pack([local(d, k) for k in SMALL]) for d in (w, m, v)]
    outs = adamw(*packed, [[g_all]], name="adamw_small", tr=512)
    for kind, buf in zip(('grad', 'delta', 'new_m', 'new_v'), outs):
        for k, a in zip(SMALL, _unpack(buf, shapes)):
            res[kind + '_' + k] = a
    g_cw = lax.dynamic_slice_in_dim(res['grad_rg_conv_w'], me * LANE, LANE, axis=2)
    cw3 = lambda a: a.reshape(1, DEPTH * RG_CONV, LANE)
    outs = adamw(cw3(w['rg_conv_w']), cw3(m['rg_conv_w']), cw3(v['rg_conv_w']), [[cw3(g_cw)]], name="adamw_conv_w", tr=16)
    for kind, buf in zip(('grad', 'delta', 'new_m', 'new_v'), outs):
        res[kind + '_rg_conv_w'] = buf.reshape(DEPTH, RG_CONV, LANE)

    return (loss, dx[None]) + tuple(res[kind + '_' + k] for kind in ('grad', 'delta', 'new_m', 'new_v') for k in WEIGHTS)


def kernel(x, norm_w, w_in, s5_lambda_re, s5_lambda_im, s5_log_step, s5_b_re, s5_b_im, s5_c_re, s5_c_im, s5_d, s5_w_glu, s5_b_glu, rg_conv_w, rg_conv_b, rg_w_a, rg_b_a, rg_w_x, rg_b_x, rg_lambda, hg_lower_bounds, hg_norm_w, w_branch, w_out, final_norm_w, loss_target, m_norm_w, m_w_in, m_s5_lambda_re, m_s5_lambda_im, m_s5_log_step, m_s5_b_re, m_s5_b_im, m_s5_c_re, m_s5_c_im, m_s5_d, m_s5_w_glu, m_s5_b_glu, m_rg_conv_w, m_rg_conv_b, m_rg_w_a, m_rg_b_a, m_rg_w_x, m_rg_b_x, m_rg_lambda, m_hg_lower_bounds, m_hg_norm_w, m_w_branch, m_w_out, m_final_norm_w, v_norm_w, v_w_in, v_s5_lambda_re, v_s5_lambda_im, v_s5_log_step, v_s5_b_re, v_s5_b_im, v_s5_c_re, v_s5_c_im, v_s5_d, v_s5_w_glu, v_s5_b_glu, v_rg_conv_w, v_rg_conv_b, v_rg_w_a, v_rg_b_a, v_rg_w_x, v_rg_b_x, v_rg_lambda, v_hg_lower_bounds, v_hg_norm_w, v_w_branch, v_w_out, v_final_norm_w):
    w = dict(zip(WEIGHTS, (norm_w, w_in, s5_lambda_re, s5_lambda_im, s5_log_step, s5_b_re, s5_b_im, s5_c_re, s5_c_im, s5_d, s5_w_glu, s5_b_glu, rg_conv_w, rg_conv_b, rg_w_a, rg_b_a, rg_w_x, rg_b_x, rg_lambda, hg_lower_bounds, hg_norm_w, w_branch, w_out, final_norm_w)))
    m = dict(zip(WEIGHTS, (m_norm_w, m_w_in, m_s5_lambda_re, m_s5_lambda_im, m_s5_log_step, m_s5_b_re, m_s5_b_im, m_s5_c_re, m_s5_c_im, m_s5_d, m_s5_w_glu, m_s5_b_glu, m_rg_conv_w, m_rg_conv_b, m_rg_w_a, m_rg_b_a, m_rg_w_x, m_rg_b_x, m_rg_lambda, m_hg_lower_bounds, m_hg_norm_w, m_w_branch, m_w_out, m_final_norm_w)))
    v = dict(zip(WEIGHTS, (v_norm_w, v_w_in, v_s5_lambda_re, v_s5_lambda_im, v_s5_log_step, v_s5_b_re, v_s5_b_im, v_s5_c_re, v_s5_c_im, v_s5_d, v_s5_w_glu, v_s5_b_glu, v_rg_conv_w, v_rg_conv_b, v_rg_w_a, v_rg_b_a, v_rg_w_x, v_rg_b_x, v_rg_lambda, v_hg_lower_bounds, v_hg_norm_w, v_w_branch, v_w_out, v_final_norm_w)))
    return _step(x[0], loss_target[0], w, m, v)
```

```python
import functools
import math

import jax
import jax.numpy as jnp
from jax import lax
from jax.experimental import pallas as pl
from jax.experimental.pallas import tpu as pltpu

f32 = jnp.float32
bf16 = jnp.bfloat16

D_MODEL = 2048
W_MIX = 1024
DEPTH = 4
N_BRANCH = 3
N_IN = 8 * W_MIX + N_BRANCH * D_MODEL
S5_GROUPS, S5_STATE, S5_GROUP = 64, 64, 16
RG_BLOCKS, RG_BLOCK, RG_CONV, RG_C = 16, 64, 4, 8.0
HG_HEADS, HG_DK = 8, 128
HG_SUB = 16
EPS = 1e-6
ADAM_LR, ADAM_B1, ADAM_B2, ADAM_EPS, ADAM_WD, ADAM_STEP = 0.001, 0.9, 0.999, 1e-08, 0.01, 10

N_DEV = 8
LANE = 128
NCH = W_MIX // LANE
TM_CHUNK = 1024
VMEM_LIMIT = 56 * 1024 * 1024
MESH = pl.DeviceIdType.MESH
ANY = pl.BlockSpec(memory_space=pl.ANY)
HIGHEST = lax.Precision.HIGHEST

C_UA, C_GA, C_XB, C_GB, C_Q, C_F, C_I, C_GC, C_GATE = (W_MIX * k for k in range(9))


def _cparams(sem=None):
    return pltpu.CompilerParams(dimension_semantics=sem, vmem_limit_bytes=VMEM_LIMIT)


@jax.custom_vjp
def bdot(a, w):
    return jnp.dot(a.astype(bf16), w.astype(bf16), preferred_element_type=f32)


def _bdot_fwd(a, w):
    return bdot(a, w), (a, w)


def _bdot_bwd(res, g):
    a, w = res
    gb = g.astype(bf16)
    da = lax.dot_general(gb, w.astype(bf16), (((1,), (1,)), ((), ())), preferred_element_type=f32)
    dw = lax.dot_general(a.astype(bf16), gb, (((0,), (0,)), ((), ())), preferred_element_type=f32)
    return da, dw


bdot.defvjp(_bdot_fwd, _bdot_bwd)


def _blockmm(c, a):
    n = c.shape[0]
    return jnp.concatenate([jnp.dot(c, a[i:i + n], preferred_element_type=f32, precision=HIGHEST)
                            for i in range(0, a.shape[0], n)], axis=0)


@jax.custom_vjp
def cdot(c, ct, a):
    return _blockmm(c, a)


def _cdot_fwd(c, ct, a):
    return cdot(c, ct, a), (c, ct)


def _cdot_bwd(res, g):
    c, ct = res
    return jnp.zeros_like(c), jnp.zeros_like(ct), _blockmm(ct, g)


cdot.defvjp(_cdot_fwd, _cdot_bwd)


def mm(a, b, *, name, out_dtype=f32, add=None, after=None, at=False, bt=False, tm=1024, tn=1024, tk=4096):
    k, m = a.shape if at else a.shape[::-1]
    n = b.shape[0] if bt else b.shape[1]
    tm, tn, tk = min(tm, m), min(tn, n), min(tk, k)
    assert m % tm == 0 and n % tn == 0 and k % tk == 0
    nk = k // tk
    dims = (((0 if at else 1,), (1 if bt else 0,)), ((), ()))

    def body(*refs):
        a_ref, b_ref = refs[:2]
        r_ref = refs[2] if add is not None else None
        o_ref = refs[-1] if nk == 1 else refs[-2]
        part = lax.dot_general(a_ref[...], b_ref[...], dims, preferred_element_type=f32)
        if nk == 1:
            if add is not None:
                part = part + r_ref[...]
            o_ref[...] = part.astype(out_dtype)
            return
        acc_ref = refs[-1]
        kk = pl.program_id(2)

        @pl.when(kk == 0)
        def _():
            acc_ref[...] = part

        @pl.when(kk > 0)
        def _():
            acc_ref[...] = acc_ref[...] + part

        @pl.when(kk == nk - 1)
        def _():
            acc = acc_ref[...]
            if add is not None:
                acc = acc + r_ref[...]
            o_ref[...] = acc.astype(out_dtype)

    b_spec = pl.BlockSpec((tn, tk), lambda i, j, q: (j, q)) if bt else pl.BlockSpec((tk, tn), lambda i, j, q: (q, j))
    a_spec = pl.BlockSpec((tk, tm), lambda i, j, q: (q, i)) if at else pl.BlockSpec((tm, tk), lambda i, j, q: (i, q))
    in_specs = [a_spec, b_spec]
    args = [a, b]
    if add is not None:
        in_specs.append(pl.BlockSpec((tm, tn), lambda i, j, q: (i, j)))
        args.append(add)
    if after is not None:
        in_specs.append(pl.BlockSpec(after.shape, lambda i, j, q: (0, 0)))
        args.append(after)
    return pl.pallas_call(
        body, name=name, grid=(m // tm, n // tn, nk), in_specs=in_specs,
        out_specs=pl.BlockSpec((tm, tn), lambda i, j, q: (i, j)),
        out_shape=jax.ShapeDtypeStruct((m, n), out_dtype),
        scratch_shapes=[] if nk == 1 else [pltpu.VMEM((tm, tn), f32)],
        compiler_params=_cparams(("parallel", "parallel", "arbitrary")),
    )(*args)


def _row_spec(tm, wc, col_off):
    base = col_off // wc
    assert col_off % wc == 0
    return pl.BlockSpec((tm, wc), lambda j, i: (i, base + j))


def _slab_spec(arr):
    r, c = arr.shape[1:]
    if arr.shape[0] == 1:
        return pl.BlockSpec((1, r, c), lambda j, i: (0, 0, 0))
    return pl.BlockSpec((1, r, c), lambda j, i: (j, 0, 0))


def rowwise(fn, rows, params, consts, outs, *, name, tm=256, ncol=1, rowid=False):
    t = rows[0][0].shape[0]
    tm = min(tm, t)
    nr, npar, nc, no = len(rows), len(params), len(consts), len(outs)

    def body(*refs):
        r = [refs[k][...].astype(f32) for k in range(nr)]
        p = [refs[nr + k][0] for k in range(npar + nc)]
        extra = ()
        if rowid:
            extra = (pl.program_id(1) * tm + lax.broadcasted_iota(jnp.int32, (tm, 1), 0),)
        res = fn(*extra, *r, *p)
        for k in range(no):
            refs[nr + npar + nc + k][...] = res[k].astype(outs[k][1])

    in_specs = [_row_spec(tm, w // ncol, off) for (_, off, w) in rows]
    in_specs += [_slab_spec(a) for a in list(params) + list(consts)]
    out_specs = [pl.BlockSpec((tm, w // ncol), lambda j, i: (i, j)) for (w, _) in outs]
    out_shape = [jax.ShapeDtypeStruct((t, w), dt) for (w, dt) in outs]
    return pl.pallas_call(
        body, name=name, grid=(ncol, t // tm), in_specs=in_specs, out_specs=out_specs, out_shape=out_shape,
        compiler_params=_cparams(("parallel", "parallel")),
    )(*[r[0] for r in rows], *params, *consts)


def rowwise_vjp(fn, rows, params, consts, cts, d_rows, *, name, tm=256, ncol=1, rowid=False, sum_primal=None):
    t = rows[0][0].shape[0]
    tm = min(tm, t)
    nr, npar, nc, nct, ndr = len(rows), len(params), len(consts), len(cts), len(d_rows)

    def body(*refs):
        i = pl.program_id(1)
        r = [refs[k][...].astype(f32) for k in range(nr)]
        p = [refs[nr + k][0] for k in range(npar)]
        c = [refs[nr + npar + k][0] for k in range(nc)]
        g = [refs[nr + npar + nc + k][...].astype(f32) for k in range(nct)]
        orefs = refs[nr + npar + nc + nct:]
        extra = ()
        if rowid:
            extra = (i * tm + lax.broadcasted_iota(jnp.int32, (tm, 1), 0),)
        res, vjp = jax.vjp(lambda *v: fn(*extra, *v, *c), *r, *p)
        grads = vjp(tuple(g))
        for k, (idx, dt) in enumerate(d_rows):
            orefs[k][...] = grads[idx].astype(dt)
        acc = [grads[nr + k] for k in range(npar)]
        if sum_primal is not None:
            acc.append(jnp.sum(res[sum_primal], axis=0, keepdims=True))

        @pl.when(i == 0)
        def _():
            for k, a in enumerate(acc):
                orefs[ndr + k][0] = a

        @pl.when(i > 0)
        def _():
            for k, a in enumerate(acc):
                orefs[ndr + k][0] = orefs[ndr + k][0] + a

    in_specs = [_row_spec(tm, w // ncol, off) for (_, off, w) in rows]
    in_specs += [_slab_spec(a) for a in list(params) + list(consts)]
    in_specs += [_row_spec(tm, w // ncol, off) for (_, off, w) in cts]
    out_specs = [pl.BlockSpec((tm, rows[idx][2] // ncol), lambda j, i: (i, j)) for (idx, _) in d_rows]
    out_shape = [jax.ShapeDtypeStruct((t, rows[idx][2]), dt) for (idx, dt) in d_rows]
    for a in params:
        out_specs.append(pl.BlockSpec((1,) + a.shape[1:], lambda j, i: (j, 0, 0)))
        out_shape.append(jax.ShapeDtypeStruct(a.shape, f32))
    if sum_primal is not None:
        w = cts[sum_primal][2]
        out_specs.append(pl.BlockSpec((1, 1, w // ncol), lambda j, i: (j, 0, 0)))
        out_shape.append(jax.ShapeDtypeStruct((ncol, 1, w // ncol), f32))
    return pl.pallas_call(
        body, name=name, grid=(ncol, t // tm), in_specs=in_specs, out_specs=out_specs, out_shape=out_shape,
        compiler_params=_cparams(("parallel", "arbitrary")),
    )(*[r[0] for r in rows], *params, *consts, *[c[0] for c in cts])


VM = pl.BlockSpec(memory_space=pltpu.VMEM)


def whole(fn, ins, outs, *, name):
    def body(*refs):
        res = fn(*[r[...] for r in refs[:len(ins)]])
        for k, o in enumerate(refs[len(ins):]):
            o[...] = res[k]
    return pl.pallas_call(body, name=name, in_specs=[VM] * len(ins), out_specs=[VM] * len(outs),
                          out_shape=[jax.ShapeDtypeStruct(s, f32) for s in outs],
                          compiler_params=_cparams())(*ins)


def whole_vjp(fn, ins, cts, *, name):
    n = len(ins)

    def body(*refs):
        _, vjp = jax.vjp(fn, *[r[...] for r in refs[:n]])
        grads = vjp(tuple(r[...] for r in refs[n:n + len(cts)]))
        for k, o in enumerate(refs[n + len(cts):]):
            o[...] = grads[k]
    return pl.pallas_call(body, name=name, in_specs=[VM] * (n + len(cts)), out_specs=[VM] * n,
                          out_shape=[jax.ShapeDtypeStruct(a.shape, f32) for a in ins],
                          compiler_params=_cparams())(*ins, *cts)


def ln_fn(x, w):
    return (x * lax.rsqrt(jnp.mean(x * x, axis=-1, keepdims=True) + EPS) * w,)


def ln_res_fn(x, w):
    return ln_fn(x, w)[0], x


def loss_fn(x, tgt, w):
    y = ln_fn(x, w)[0]
    return (0.5 * jnp.mean(jnp.square(y - tgt), axis=-1, keepdims=True),)


def s5_prep_fn(lam_re, lam_im, log_step, b_re, b_im):
    step = jnp.exp(log_step)
    mag = jnp.exp(lam_re * step)
    ang = lam_im * step
    abar_re = mag * jnp.cos(ang)
    abar_im = mag * jnp.sin(ang)
    num_re = abar_re - 1.0
    num_im = abar_im
    den = lam_re * lam_re + lam_im * lam_im
    coef_re = (num_re * lam_re + num_im * lam_im) / den
    coef_im = (num_im * lam_re - num_re * lam_im) / den
    bbar_re = coef_re * b_re - coef_im * b_im
    bbar_im = coef_re * b_im + coef_im * b_re
    return abar_re, abar_im, bbar_re, bbar_im


def lb_prep_fn(r0, r1, r2, r3):
    m = jnp.maximum(jnp.maximum(r0, r1), jnp.maximum(r2, r3))
    e0, e1, e2, e3 = jnp.exp(r0 - m), jnp.exp(r1 - m), jnp.exp(r2 - m), jnp.exp(r3 - m)
    s = e0 + e1 + e2 + e3
    p0, p1, p2, p3 = e0 / s, e1 / s, e2 / s, e3 / s
    c1 = p0 + p1
    c2 = c1 + p2
    c3 = c2 + p3
    return p0 - p0, c1 - p0, c2 - p0, c3 - p0


def s5_post_fn(yssm, u, ga, d, wglu, bglu):
    y = jax.nn.gelu(yssm + d * u)
    y = y * jax.nn.sigmoid(bdot(y, wglu) + bglu)
    return (y * jax.nn.silu(ga),)


def rg_gate_fn(tglob, xc, wa, ba, wx, bx, lam):
    r = jax.nn.sigmoid(bdot(xc, wa) + ba)
    i = jax.nn.sigmoid(bdot(xc, wx) + bx)
    log_a = -RG_C * r * jax.nn.softplus(-lam)
    a = jnp.exp(log_a)
    mult = jnp.sqrt(-jnp.tanh(log_a) * (a * a + 1.0))
    mult = jnp.where(tglob == 0, 1.0, mult)
    return a, mult * (i * xc)


def hg_pre_fn(q, fl, lb, tri, tri_t):
    f = lb + (1.0 - lb) * jax.nn.sigmoid(fl)
    return jax.nn.silu(q), 1.0 - f, cdot(tri, tri_t, jnp.log(f))


def branch_prep_fn(hb, gb, oc, gc, nw):
    yb = hb * jax.nn.silu(gb)
    on = oc * lax.rsqrt(jnp.mean(oc * oc, axis=-1, keepdims=True) + EPS) * nw
    return yb, on * jax.nn.silu(gc)


def merge_fn(b0, b1, b2, g0, g1, g2):
    return (jax.nn.sigmoid(g0) * b0 + jax.nn.sigmoid(g1) * b1 + jax.nn.sigmoid(g2) * b2,)


S5_TB = 512
S5_SC = 512


SEG = 8


def _shift(v, k, pos, period, reverse, fill):
    if reverse:
        return jnp.where(pos < period - k, pltpu.roll(v, v.shape[0] - k, 0), fill)
    return jnp.where(pos >= k, pltpu.roll(v, k, 0), fill)


def _cmul(ar, ai, br, bi):
    return ar * br - ai * bi, ar * bi + ai * br


def _edge_rows(v, first):
    r0 = 0 if first else SEG - 1
    return jnp.concatenate([v[r:r + 1, :] for r in range(r0, v.shape[0], SEG)], axis=0)


def _spread(s):
    return jnp.concatenate([jnp.broadcast_to(s[g:g + 1, :], (SEG, s.shape[1])) for g in range(s.shape[0])], axis=0)


def lti_scan(xr, xi, ar, ai, cr, ci, reverse=False):
    n = xr.shape[0]
    g = n // SEG
    sub = lax.broadcasted_iota(jnp.int32, (n, 1), 0) & (SEG - 1)
    sub8 = lax.broadcasted_iota(jnp.int32, (SEG, 1), 0)
    grow = lax.broadcasted_iota(jnp.int32, (g, 1), 0)
    pr, pi_ = ar, ai
    wr, wi = jnp.broadcast_to(ar, (SEG, ar.shape[1])), jnp.broadcast_to(ai, (SEG, ai.shape[1]))
    k = 1
    while k < SEG:
        tr, ti = _cmul(pr, pi_, _shift(xr, k, sub, SEG, reverse, 0.0), _shift(xi, k, sub, SEG, reverse, 0.0))
        xr, xi = xr + tr, xi + ti
        pr, pi_ = _cmul(pr, pi_, pr, pi_)
        wr, wi = _cmul(wr, wi, _shift(wr, k, sub8, SEG, reverse, 1.0), _shift(wi, k, sub8, SEG, reverse, 0.0))
        k *= 2
    first, last = (g - 1, 0) if reverse else (0, g - 1)
    jr, ji = _cmul(pr, pi_, cr, ci)
    sr = _edge_rows(xr, reverse) + jnp.where(grow == first, jr, 0.0)
    si = _edge_rows(xi, reverse) + jnp.where(grow == first, ji, 0.0)
    k = 1
    while k < g:
        tr, ti = _cmul(pr, pi_, _shift(sr, k, grow, g, reverse, 0.0), _shift(si, k, grow, g, reverse, 0.0))
        sr, si = sr + tr, si + ti
        pr, pi_ = _cmul(pr, pi_, pr, pi_)
        k *= 2
    er, ei = _spread(_shift(sr, 1, grow, g, reverse, cr)), _spread(_shift(si, 1, grow, g, reverse, ci))
    tr, ti = _cmul(jnp.tile(wr, (g, 1)), jnp.tile(wi, (g, 1)), er, ei)
    return xr + tr, xi + ti, sr[last:last + 1, :], si[last:last + 1, :]


def tv_scan(aa, bb, carry, reverse=False):
    n = aa.shape[0]
    row = lax.broadcasted_iota(jnp.int32, (n, 1), 0)
    k = 1
    while k < n:
        bb = bb + aa * _shift(bb, k, row, n, reverse, 0.0)
        aa = aa * _shift(aa, k, row, n, reverse, 1.0)
        k *= 2
    h = bb + aa * carry
    last = 0 if reverse else n - 1
    return h, h[last:last + 1, :]


def s5_scan_fwd(z, bd_re, bd_im, cd_re, cd_im, a_re, a_im):
    t = z.shape[0]
    tb = min(S5_TB, t)

    def body(u_ref, bre, bim, cre, cim, are, aim, y_ref, xre_ref, xim_ref, car_re, car_im):
        @pl.when(pl.program_id(1) == 0)
        def _():
            car_re[...] = jnp.zeros_like(car_re)
            car_im[...] = jnp.zeros_like(car_im)

        u = u_ref[...].astype(bf16)
        xr, xi, car_re[...], car_im[...] = lti_scan(
            jnp.dot(u, bre[0], preferred_element_type=f32), jnp.dot(u, bim[0], preferred_element_type=f32),
            are[0], aim[0], car_re[...], car_im[...])
        xre_ref[...] = xr
        xim_ref[...] = xi
        y_ref[...] = (jnp.dot(xr.astype(bf16), cre[0], preferred_element_type=f32)
                      - jnp.dot(xi.astype(bf16), cim[0], preferred_element_type=f32))

    chunk = lambda r, c: pl.BlockSpec((1, r, c), lambda j, i: (j, 0, 0))
    return pl.pallas_call(
        body, name="s5_scan_fwd", grid=(NCH, t // tb),
        in_specs=[pl.BlockSpec((tb, LANE), lambda j, i: (i, C_UA // LANE + j)),
                  chunk(LANE, S5_SC), chunk(LANE, S5_SC), chunk(S5_SC, LANE), chunk(S5_SC, LANE),
                  chunk(1, S5_SC), chunk(1, S5_SC)],
        out_specs=[pl.BlockSpec((tb, LANE), lambda j, i: (i, j)),
                   pl.BlockSpec((tb, S5_SC), lambda j, i: (i, j)),
                   pl.BlockSpec((tb, S5_SC), lambda j, i: (i, j))],
        out_shape=[jax.ShapeDtypeStruct((t, W_MIX), f32),
                   jax.ShapeDtypeStruct((t, NCH * S5_SC), f32),
                   jax.ShapeDtypeStruct((t, NCH * S5_SC), f32)],
        scratch_shapes=[pltpu.VMEM((1, S5_SC), f32)] * 2,
        compiler_params=_cparams(("parallel", "arbitrary")),
    )(z, bd_re.astype(bf16), bd_im.astype(bf16), cd_re.astype(bf16), cd_im.astype(bf16), a_re, a_im)


def s5_scan_bwd(dy, du1, z, xre, xim, bd_re, bd_im, cd_re, cd_im, a_re, a_im):
    t = z.shape[0]
    tb = min(S5_TB, t)
    nt = t // tb

    def body(dy_ref, du1_ref, u_ref, xre_ref, xim_ref, hre_ref, him_ref, bre, bim, cre, cim, are, aim,
             du_ref, dbre, dbim, dcre, dcim, dare, daim, car_re, car_im):
        step = pl.program_id(1)
        tt = nt - 1 - step

        @pl.when(step == 0)
        def _():
            car_re[...] = jnp.zeros_like(car_re)
            car_im[...] = jnp.zeros_like(car_im)

        nt_dims = (((1,), (1,)), ((), ()))
        tn_dims = (((0,), (0,)), ((), ()))
        dyb = dy_ref[...].astype(bf16)
        row = lax.broadcasted_iota(jnp.int32, (tb, 1), 0)
        xr, xi = xre_ref[...], xim_ref[...]
        ar, ai = are[0], aim[0]
        lr, li, car_re[...], car_im[...] = lti_scan(
            lax.dot_general(dyb, cre[0], nt_dims, preferred_element_type=f32),
            -lax.dot_general(dyb, cim[0], nt_dims, preferred_element_type=f32),
            ar, -ai, car_re[...], car_im[...], reverse=True)
        lrb, lib = lr.astype(bf16), li.astype(bf16)
        ub = u_ref[...].astype(bf16)
        du = (lax.dot_general(lrb, bre[0], nt_dims, preferred_element_type=f32)
              + lax.dot_general(lib, bim[0], nt_dims, preferred_element_type=f32))
        du_ref[...] = (du + du1_ref[...].astype(f32)).astype(du_ref.dtype)
        live = (tt > 0).astype(f32)
        xpr = jnp.where(row == 0, hre_ref[7:8, :] * live, pltpu.roll(xr, 1, 0))
        xpi = jnp.where(row == 0, him_ref[7:8, :] * live, pltpu.roll(xi, 1, 0))
        acc = [
            lax.dot_general(ub, lrb, tn_dims, preferred_element_type=f32),
            lax.dot_general(ub, lib, tn_dims, preferred_element_type=f32),
            lax.dot_general(xr.astype(bf16), dyb, tn_dims, preferred_element_type=f32),
            -lax.dot_general(xi.astype(bf16), dyb, tn_dims, preferred_element_type=f32),
            jnp.sum(lr * xpr + li * xpi, axis=0, keepdims=True),
            jnp.sum(li * xpr - lr * xpi, axis=0, keepdims=True),
        ]
        outs = [dbre, dbim, dcre, dcim, dare, daim]

        @pl.when(step == 0)
        def _():
            for o, a in zip(outs, acc):
                o[0] = a

        @pl.when(step > 0)
        def _():
            for o, a in zip(outs, acc):
                o[0] = o[0] + a

    chunk = lambda r, c: pl.BlockSpec((1, r, c), lambda j, i: (j, 0, 0))
    rev = lambda w, base=0: pl.BlockSpec((tb, w), lambda j, i: (nt - 1 - i, base + j))
    halo = pl.BlockSpec((8, S5_SC), lambda j, i: (jnp.maximum((nt - 1 - i) * (tb // 8) - 1, 0), j))
    return pl.pallas_call(
        body, name="s5_scan_bwd", grid=(NCH, nt),
        in_specs=[rev(LANE), rev(LANE), rev(LANE, C_UA // LANE), rev(S5_SC), rev(S5_SC), halo, halo,
                  chunk(LANE, S5_SC), chunk(LANE, S5_SC), chunk(S5_SC, LANE), chunk(S5_SC, LANE),
                  chunk(1, S5_SC), chunk(1, S5_SC)],
        out_specs=[rev(LANE), chunk(LANE, S5_SC), chunk(LANE, S5_SC), chunk(S5_SC, LANE), chunk(S5_SC, LANE),
                   chunk(1, S5_SC), chunk(1, S5_SC)],
        out_shape=[jax.ShapeDtypeStruct((t, W_MIX), bf16),
                   jax.ShapeDtypeStruct((NCH, LANE, S5_SC), f32), jax.ShapeDtypeStruct((NCH, LANE, S5_SC), f32),
                   jax.ShapeDtypeStruct((NCH, S5_SC, LANE), f32), jax.ShapeDtypeStruct((NCH, S5_SC, LANE), f32),
                   jax.ShapeDtypeStruct((NCH, 1, S5_SC), f32), jax.ShapeDtypeStruct((NCH, 1, S5_SC), f32)],
        scratch_shapes=[pltpu.VMEM((1, S5_SC), f32)] * 2,
        compiler_params=_cparams(("parallel", "arbitrary")),
    )(dy, du1, z, xre, xim, xre, xim, bd_re.astype(bf16), bd_im.astype(bf16), cd_re.astype(bf16),
      cd_im.astype(bf16), a_re, a_im)


RG_TB = 512


def _rg_conv(ext, w, cb, tb):
    acc = cb + w[3:4, :] * ext[8:, :]
    for k in range(3):
        acc = acc + w[k:k + 1, :] * pltpu.roll(ext, 3 - k, 0)[8:, :]
    return acc


def rg_fwd(z, cw, cb, gate_p):
    t = z.shape[0]
    tb = min(RG_TB, t)

    def body(x_ref, halo_ref, cw_ref, cb_ref, wa, ba, wx, bx, lam, xc_ref, a_ref, h_ref, car):
        i = pl.program_id(1)

        @pl.when(i == 0)
        def _():
            car[...] = jnp.zeros_like(car)

        ext = jnp.concatenate([halo_ref[...] * (i > 0).astype(f32), x_ref[...]], axis=0)
        xc = _rg_conv(ext, cw_ref[0], cb_ref[0], tb)
        tglob = i * tb + lax.broadcasted_iota(jnp.int32, (tb, 1), 0)
        a, b = rg_gate_fn(tglob, xc, wa[0], ba[0], wx[0], bx[0], lam[0])
        xc_ref[...] = xc
        a_ref[...] = a
        h_ref[...], car[...] = tv_scan(a, b, car[...])

    base = C_XB // LANE
    out = pl.BlockSpec((tb, LANE), lambda j, i: (i, j))
    return pl.pallas_call(
        body, name="rg_fwd", grid=(NCH, t // tb),
        in_specs=[pl.BlockSpec((tb, LANE), lambda j, i: (i, base + j)),
                  pl.BlockSpec((8, LANE), lambda j, i: (jnp.maximum(i * (tb // 8) - 1, 0), base + j)),
                  _slab_spec(cw), _slab_spec(cb)] + [_slab_spec(p) for p in gate_p],
        out_specs=[out, out, out], out_shape=[jax.ShapeDtypeStruct((t, W_MIX), f32)] * 3,
        scratch_shapes=[pltpu.VMEM((1, LANE), f32)],
        compiler_params=_cparams(("parallel", "arbitrary")),
    )(z, z, cw, cb, *gate_p)


def rg_bwd(dh, z, xc, a, h, cw, gate_p):
    t = z.shape[0]
    tb = min(RG_TB, t)
    nt = t // tb

    def body(g_ref, a_ref, an_ref, h_ref, hp_ref, xc_ref, x_ref, xh_ref, cw_ref, wa, ba, wx, bx, lam,
             dx_ref, dcw_ref, dcb_ref, dwa, dba, dwx, dbx, dlam, car, later):
        step = pl.program_id(1)
        tt = nt - 1 - step

        @pl.when(step == 0)
        def _():
            car[...] = jnp.zeros_like(car)
            later[...] = jnp.zeros_like(later)

        row = lax.broadcasted_iota(jnp.int32, (tb, 1), 0)
        an = an_ref[0:1, :] * (tt < nt - 1).astype(f32)
        aa = jnp.where(row == tb - 1, an, pltpu.roll(a_ref[...], tb - 1, 0))
        lmb, car[...] = tv_scan(aa, g_ref[...], car[...], reverse=True)
        hp = jnp.where(row == 0, hp_ref[7:8, :] * (tt > 0).astype(f32), pltpu.roll(h_ref[...], 1, 0))
        tglob = tt * tb + row
        _, vjp = jax.vjp(lambda *v: rg_gate_fn(tglob, *v), xc_ref[...], wa[0], ba[0], wx[0], bx[0], lam[0])
        g, *dp = vjp((lmb * hp, lmb))
        gext = jnp.concatenate([g, later[...]], axis=0)
        later[...] = g[0:8, :]
        xext = jnp.concatenate([xh_ref[...] * (tt > 0).astype(f32), x_ref[...]], axis=0)
        w = cw_ref[0]
        dx = w[3:4, :] * g
        taps = [None] * RG_CONV
        taps[3] = jnp.sum(g * xext[8:, :], axis=0, keepdims=True)
        for k in range(3):
            s = 3 - k
            dx = dx + w[k:k + 1, :] * pltpu.roll(gext, tb + 8 - s, 0)[:tb, :]
            taps[k] = jnp.sum(g * pltpu.roll(xext, s, 0)[8:, :], axis=0, keepdims=True)
        dx_ref[...] = dx.astype(dx_ref.dtype)
        acc = [jnp.concatenate(taps, axis=0), jnp.sum(g, axis=0, keepdims=True)] + dp
        outs = [dcw_ref, dcb_ref, dwa, dba, dwx, dbx, dlam]

        @pl.when(step == 0)
        def _():
            for o, v in zip(outs, acc):
                o[0] = v

        @pl.when(step > 0)
        def _():
            for o, v in zip(outs, acc):
                o[0] = o[0] + v

    base = C_XB // LANE
    rev = lambda b=0: pl.BlockSpec((tb, LANE), lambda j, i: (nt - 1 - i, b + j))
    nxt = pl.BlockSpec((8, LANE), lambda j, i: (jnp.minimum((nt - i) * (tb // 8), t // 8 - 1), j))
    prv = lambda b=0: pl.BlockSpec((8, LANE), lambda j, i: (jnp.maximum((nt - 1 - i) * (tb // 8) - 1, 0), b + j))
    params = [cw] + list(gate_p)
    grads = [jax.ShapeDtypeStruct(s, f32) for s in [cw.shape, (NCH, 1, LANE)] + [p.shape for p in gate_p]]
    return pl.pallas_call(
        body, name="rg_bwd", grid=(NCH, nt),
        in_specs=[rev(), rev(), nxt, rev(), prv(), rev(), rev(base), prv(base)] + [_slab_spec(p) for p in params],
        out_specs=[rev()] + [_slab_spec(p) for p in grads], out_shape=[jax.ShapeDtypeStruct((t, W_MIX), bf16)] + grads,
        scratch_shapes=[pltpu.VMEM((1, LANE), f32), pltpu.VMEM((8, LANE), f32)],
        compiler_params=_cparams(("parallel", "arbitrary")),
    )(dh, a, a, h, h, xc, z, z, *params)


HG_TB = 256
HALF = HG_SUB // 2


def _heads(v):
    return jnp.stack([v[:, LANE * h:LANE * (h + 1)] for h in range(HG_HEADS)])


def _unheads(v):
    return jnp.concatenate([v[h] for h in range(HG_HEADS)], axis=-1)


def _bmm(eq, a, b):
    return jnp.einsum(eq, a.astype(bf16), b.astype(bf16), preferred_element_type=f32)


def hg_chunk_fwd(qs, kk, gcum, z):
    t = qs.shape[0]
    tb = min(HG_TB, t)
    nc = tb // HG_SUB

    def body(q_ref, k_ref, g_ref, v_ref, o_ref, sall_ref, st_ref):
        @pl.when(pl.program_id(0) == 0)
        def _():
            st_ref[...] = jnp.zeros_like(st_ref)

        ri = lax.broadcasted_iota(jnp.int32, (1, HALF, 1), 1)

        def chunk(c, carry):
            rows = pl.ds(pl.multiple_of(c * HG_SUB, HG_SUB), HG_SUB)
            q, k, g, v = _heads(q_ref[rows, :]), _heads(k_ref[rows, :]), _heads(g_ref[rows, :]), _heads(v_ref[rows, :])
            st = st_ref[...]
            sall_ref[c] = st
            o = _bmm('htk,hvk->htv', q * jnp.exp(g), st)
            halves = [[q[:, :HALF], g[:, :HALF], o[:, :HALF]], [q[:, HALF:], g[:, HALF:], o[:, HALF:]]]
            for s in range(HG_SUB):
                grow, krow, vrow = g[:, s:s + 1, :], k[:, s:s + 1, :], v[:, s:s + 1, :]
                for h in range(s // HALF, 2):
                    qh, gh, oh = halves[h]
                    p = jnp.exp(jnp.minimum(gh - grow, 0.0))
                    if s // HALF == h:
                        p = jnp.where(ri >= s - h * HALF, p, 0.0)
                    halves[h][2] = oh + jnp.sum(qh * krow * p, axis=-1, keepdims=True) * vrow
            o = jnp.concatenate([halves[0][2], halves[1][2]], axis=1)
            gl = g[:, HG_SUB - 1:HG_SUB, :]
            st_ref[...] = st * jnp.exp(gl) + _bmm('htv,htk->hvk', v, k * jnp.exp(gl - g))
            o_ref[rows, :] = _unheads(o)
            return carry

        lax.fori_loop(0, nc, chunk, 0)

    spec = lambda base=0: pl.BlockSpec((tb, W_MIX), lambda i: (i, base))
    return pl.pallas_call(
        body, name="hg_chunk_fwd", grid=(t // tb,),
        in_specs=[spec(), spec(), spec(), spec(C_I // W_MIX)],
        out_specs=[spec(), pl.BlockSpec((nc, HG_HEADS, HG_DK, HG_DK), lambda i: (i, 0, 0, 0))],
        out_shape=[jax.ShapeDtypeStruct((t, W_MIX), f32),
                   jax.ShapeDtypeStruct((t // HG_SUB, HG_HEADS, HG_DK, HG_DK), f32)],
        scratch_shapes=[pltpu.VMEM((HG_HEADS, HG_DK, HG_DK), f32)],
        compiler_params=_cparams(("arbitrary",)),
    )(qs, kk, gcum, z)


def hg_chunk_bwd(do, qs, kk, gcum, z, sall):
    t = qs.shape[0]
    tb = min(HG_TB, t)
    nc = tb // HG_SUB
    nt = t // tb

    def body(do_ref, q_ref, k_ref, g_ref, v_ref, sall_ref, dq_ref, dk_ref, dg_ref, dv_ref, dst_ref):
        @pl.when(pl.program_id(0) == 0)
        def _():
            dst_ref[...] = jnp.zeros_like(dst_ref)

        ri = lax.broadcasted_iota(jnp.int32, (1, HALF, 1), 1)
        ri_chunk = lax.broadcasted_iota(jnp.int32, (1, HG_SUB, 1), 1)

        def chunk(cc, carry):
            c = nc - 1 - cc
            rows = pl.ds(pl.multiple_of(c * HG_SUB, HG_SUB), HG_SUB)
            q, k, g, v = _heads(q_ref[rows, :]), _heads(k_ref[rows, :]), _heads(g_ref[rows, :]), _heads(v_ref[rows, :])
            d_o = _heads(do_ref[rows, :])
            st = sall_ref[c]
            dsn = dst_ref[...]
            eg = jnp.exp(g)
            qe = q * eg
            gl = g[:, HG_SUB - 1:HG_SUB, :]
            egl = jnp.exp(gl)
            dec = jnp.exp(gl - g)
            kd = k * dec
            dqe = _bmm('htv,hvk->htk', d_o, st)
            dst_ref[...] = _bmm('htv,htk->hvk', d_o, qe) + dsn * egl
            dgl_dec = jnp.sum(dsn * st, axis=1, keepdims=True) * egl
            dv = _bmm('htk,hvk->htv', kd, dsn)
            dkd = _bmm('htv,hvk->htk', v, dsn)
            tq, tg, tdo = [q[:, :HALF], q[:, HALF:]], [g[:, :HALF], g[:, HALF:]], [d_o[:, :HALF], d_o[:, HALF:]]
            a1 = [jnp.zeros_like(tq[0]), jnp.zeros_like(tq[1])]
            a2 = [jnp.zeros_like(tq[0]), jnp.zeros_like(tq[1])]
            dvh = [dv[:, :HALF], dv[:, HALF:]]
            for s in range(HG_SUB):
                grow, krow, vrow = g[:, s:s + 1, :], k[:, s:s + 1, :], v[:, s:s + 1, :]
                sh, sr = s // HALF, s % HALF
                dv_s, a2_s = 0.0, 0.0
                for h in range(sh, 2):
                    p = jnp.exp(jnp.minimum(tg[h] - grow, 0.0))
                    if sh == h:
                        p = jnp.where(ri >= sr, p, 0.0)
                    col = jnp.sum(tq[h] * krow * p, axis=-1, keepdims=True)
                    t1 = jnp.sum(tdo[h] * vrow, axis=-1, keepdims=True) * p
                    a1[h] = a1[h] + t1 * krow
                    dv_s = dv_s + jnp.sum(col * tdo[h], axis=1, keepdims=True)
                    a2_s = a2_s + jnp.sum(t1 * tq[h], axis=1, keepdims=True)
                dvh[sh] = jnp.where(ri == sr, dvh[sh] + dv_s, dvh[sh])
                a2[sh] = jnp.where(ri == sr, a2_s, a2[sh])
            a1, a2 = jnp.concatenate(a1, axis=1), jnp.concatenate(a2, axis=1)
            dv = jnp.concatenate(dvh, axis=1)
            dgl = jnp.sum(dkd * kd, axis=1, keepdims=True) + dgl_dec
            dg = dqe * qe + q * a1 - k * a2 - dkd * kd
            dg = jnp.where(ri_chunk == HG_SUB - 1, dg + dgl, dg)
            dq_ref[rows, :] = _unheads(dqe * eg + a1)
            dk_ref[rows, :] = _unheads(dkd * dec + a2)
            dg_ref[rows, :] = _unheads(dg)
            dv_ref[rows, :] = _unheads(dv).astype(dv_ref.dtype)
            return carry

        lax.fori_loop(0, nc, chunk, 0)

    spec = lambda base=0: pl.BlockSpec((tb, W_MIX), lambda i: (nt - 1 - i, base))
    return pl.pallas_call(
        body, name="hg_chunk_bwd", grid=(nt,),
        in_specs=[spec(), spec(), spec(), spec(), spec(C_I // W_MIX),
                  pl.BlockSpec((nc, HG_HEADS, HG_DK, HG_DK), lambda i: (nt - 1 - i, 0, 0, 0))],
        out_specs=[spec(), spec(), spec(), spec()],
        out_shape=[jax.ShapeDtypeStruct((t, W_MIX), f32)] * 3 + [jax.ShapeDtypeStruct((t, W_MIX), bf16)],
        scratch_shapes=[pltpu.VMEM((HG_HEADS, HG_DK, HG_DK), f32)],
        compiler_params=_cparams(("arbitrary",)),
    )(do, qs, kk, gcum, z, sall)


def adamw(w, m, v, slots, *, name, tr):
    nl, r, c = w.shape
    tr = min(tr, r)
    flat = [a for per_layer in slots for a in per_layer]
    c1 = 1.0 / (1.0 - ADAM_B1 ** ADAM_STEP)
    c2 = 1.0 / (1.0 - ADAM_B2 ** ADAM_STEP)

    def body(*refs):
        w_ref, m_ref, v_ref = refs[:3]
        s_refs = list(refs[3:3 + len(flat)])
        g_ref, d_ref, mo_ref, vo_ref = refs[3 + len(flat):]
        for l in range(nl):
            parts = [s_refs.pop(0) for _ in slots[l]]
            g = None
            for p in parts:
                for s in range(p.shape[0]):
                    term = p[s].astype(f32)
                    g = term if g is None else g + term
            mn = ADAM_B1 * m_ref[l] + (1.0 - ADAM_B1) * g
            vn = ADAM_B2 * v_ref[l] + (1.0 - ADAM_B2) * (g * g)
            g_ref[l] = g
            mo_ref[l] = mn
            vo_ref[l] = vn
            d_ref[l] = -ADAM_LR * ((mn * c1) / (jnp.sqrt(vn * c2) + ADAM_EPS) + ADAM_WD * w_ref[l])

    full = pl.BlockSpec((nl, tr, c), lambda i: (0, i, 0))
    slot = [pl.BlockSpec((a.shape[0], tr, c), lambda i: (0, i, 0)) for a in flat]
    return pl.pallas_call(
        body, name=name, grid=(r // tr,), in_specs=[full] * 3 + slot, out_specs=[full] * 4,
        out_shape=[jax.ShapeDtypeStruct(w.shape, f32)] * 4, compiler_params=_cparams(("parallel",)),
    )(w, m, v, *flat)


def _slab(ref, axis, idx, n):
    return ref.at[tuple([slice(None)] * axis + [pl.ds(idx * n, n)])]


def all_gather(x, axis, *, name):
    n = x.shape[axis]
    out_shape = x.shape[:axis] + (N_DEV * n,) + x.shape[axis + 1:]

    def body(x_ref, out_ref, send_sems, recv_sems, local_sem):
        xx, yy, cc = lax.axis_index("x"), lax.axis_index("y"), lax.axis_index("c")
        me, sibling = (xx, yy, cc), (xx, yy, 1 - cc)
        chips = [(1 - xx, yy), (xx, 1 - yy), (1 - xx, 1 - yy)]

        def slab(px, py, pc):
            return _slab(out_ref, axis, 4 * px + 2 * py + pc, n)

        def copy(k, block, to, src=None):
            return pltpu.make_async_remote_copy(
                src_ref=slab(*block) if src is None else src, dst_ref=slab(*block),
                send_sem=send_sems.at[k], recv_sem=recv_sems.at[k], device_id=to, device_id_type=MESH)

        mine = pltpu.make_async_copy(x_ref, slab(*me), local_sem)
        mine.start()
        first = [copy(0, me, sibling, src=x_ref)]
        first += [copy(1 + j, me, (*chip, cc), src=x_ref) for j, chip in enumerate(chips)]
        for cp in first:
            cp.start()
        passed = [copy(4 + j, (*chip, cc), sibling) for j, chip in enumerate(chips)]
        for j, chip in enumerate(chips):
            copy(1 + j, (*chip, cc), me).wait_recv()
            passed[j].start()
        copy(0, sibling, me).wait_recv()
        for j, chip in enumerate(chips):
            copy(4 + j, (*chip, 1 - cc), me).wait_recv()
        for cp in first + passed:
            cp.wait_send()
        mine.wait()

    return pl.pallas_call(
        body, name=name, out_shape=jax.ShapeDtypeStruct(out_shape, x.dtype), in_specs=[ANY], out_specs=ANY,
        scratch_shapes=[pltpu.SemaphoreType.DMA((7,)), pltpu.SemaphoreType.DMA((7,)), pltpu.SemaphoreType.DMA],
    )(x)


N_CHIP = 4


HBM = pl.BlockSpec(memory_space=pltpu.HBM)
SEM = pl.BlockSpec(memory_space=pltpu.SEMAPHORE)
EFFECT = pltpu.SideEffectType.DATAFLOW_SIDE_EFFECTING
TOKEN = jax.ShapeDtypeStruct((8, LANE), f32)


def _in_hbm(a):
    return pltpu.with_memory_space_constraint(a, pltpu.HBM)


def pair_sums(g, axis, *, name):
    n = g.shape[axis] // N_DEV
    slab_shape = g.shape[:axis] + (n,) + g.shape[axis + 1:]
    cols = slab_shape[-1]
    rows = math.prod(slab_shape[:-1])
    col_slabs = axis == g.ndim - 1
    assert col_slabs or (axis == 0 and g.ndim == 2)

    def swap_body(g_ref, got_ref, send_sems, recv_sems):
        xx, yy, cc = lax.axis_index("x"), lax.axis_index("y"), lax.axis_index("c")
        copies = [pltpu.make_async_remote_copy(
            src_ref=_slab(g_ref, axis, 2 * q + 1 - cc, n), dst_ref=got_ref.at[q],
            send_sem=send_sems.at[q], recv_sem=recv_sems.at[q], device_id=(xx, yy, 1 - cc), device_id_type=MESH)
            for q in range(N_CHIP)]
        for cp in copies:
            cp.start()
        for cp in copies:
            cp.wait()

    got = pl.pallas_call(
        swap_body, name=name + "_swap", out_shape=jax.ShapeDtypeStruct((N_CHIP,) + slab_shape, g.dtype),
        in_specs=[ANY], out_specs=ANY, scratch_shapes=[pltpu.SemaphoreType.DMA((N_CHIP,))] * 2,
    )(g)

    tr = min(256, rows)

    def add_body(a0_ref, a1_ref, b_ref, pair_ref, own_ref):
        xx, yy, cc = lax.axis_index("x"), lax.axis_index("y"), lax.axis_index("c")
        mine = jnp.where(cc == 0, a0_ref[...], a1_ref[...])
        s = (mine.astype(f32) + b_ref[0].astype(f32)).astype(bf16)
        pair_ref[0] = s

        @pl.when(pl.program_id(1) == 2 * xx + yy)
        def _():
            own_ref[0] = s

    if col_slabs:
        a_spec = lambda c: pl.BlockSpec((tr, cols), lambda i, q: (i, 2 * q + c))
    else:
        a_spec = lambda c: pl.BlockSpec((tr, cols), lambda i, q: ((2 * q + c) * (n // tr) + i, 0))
    by_chip = pl.BlockSpec((1, tr, cols), lambda i, q: (q, i, 0))
    g2 = g.reshape(-1, g.shape[-1])
    pair, own = pl.pallas_call(
        add_body, name=name + "_add", grid=(rows // tr, N_CHIP), in_specs=[a_spec(0), a_spec(1), by_chip],
        out_specs=[by_chip, pl.BlockSpec((1, tr, cols), lambda i, q: (0, i, 0))],
        out_shape=[jax.ShapeDtypeStruct((N_CHIP, rows, cols), bf16), jax.ShapeDtypeStruct((1, rows, cols), bf16)],
        compiler_params=_cparams(("parallel", "arbitrary")),
    )(g2, g2, got.reshape(N_CHIP, rows, cols))
    return own, pair


def _send_copies(p_refs, land_refs, send_sems, recv_sems):
    xx, yy, cc = lax.axis_index("x"), lax.axis_index("y"), lax.axis_index("c")
    copies = []
    for t, (p, land) in enumerate(zip(p_refs, land_refs)):
        for k in range(1, N_CHIP):
            px = 1 - xx if k & 2 else xx
            py = 1 - yy if k & 1 else yy
            s = (N_CHIP - 1) * t + k - 1
            copies.append(pltpu.make_async_remote_copy(
                src_ref=p.at[2 * px + py], dst_ref=land.at[k - 1], send_sem=send_sems.at[s], recv_sem=recv_sems.at[s],
                device_id=(px, py, cc), device_id_type=MESH))
    return copies


def send_pairs_start(pairs, after, *, name):
    nt = len(pairs)
    lands = [lax.empty((N_CHIP - 1,) + p.shape[1:], p.dtype) for p in pairs]

    def body(*refs):
        p_refs, land_refs = refs[:nt], refs[nt:2 * nt]
        send_sems, recv_sems = refs[2 * nt + 1], refs[2 * nt + 2]
        token = refs[-1]
        for cp in _send_copies(p_refs, land_refs, send_sems, recv_sems):
            cp.start()
        token[...] = jnp.zeros_like(token)

    nsem = (N_CHIP - 1) * nt
    outs = pl.pallas_call(
        body, name=name,
        out_shape=(pltpu.SemaphoreType.DMA((nsem,)), pltpu.SemaphoreType.DMA((nsem,)))
        + tuple(pltpu.HBM(a.shape, a.dtype) for a in list(pairs) + lands) + (TOKEN,),
        in_specs=[HBM] * (2 * nt) + [ANY], out_specs=(SEM, SEM) + (HBM,) * (2 * nt) + (VM,),
        input_output_aliases={i: 2 + i for i in range(2 * nt)},
        compiler_params=pltpu.CompilerParams(has_side_effects=EFFECT),
    )(*[_in_hbm(a) for a in list(pairs) + lands], after)
    return outs[:-1], outs[-1]


def send_pairs_wait(handles, after, *, name):
    send_sems, recv_sems = handles[0], handles[1]
    bufs = handles[2:]
    nt = len(bufs) // 2

    def body(*refs):
        p_refs, land_refs = refs[:nt], refs[nt:2 * nt]
        send_sems, recv_sems = refs[2 * nt], refs[2 * nt + 1]
        for cp in _send_copies(p_refs, land_refs, send_sems, recv_sems):
            cp.wait_send()
            cp.wait_recv()

    outs = pl.pallas_call(
        body, name=name, out_shape=tuple(pltpu.HBM(a.shape, a.dtype) for a in bufs),
        in_specs=[HBM] * (2 * nt) + [SEM, SEM, ANY], out_specs=(HBM,) * (2 * nt),
        input_output_aliases={i: i for i in range(2 * nt)},
        compiler_params=pltpu.CompilerParams(has_side_effects=EFFECT),
    )(*bufs, send_sems, recv_sems, after)
    return outs[nt:]


def _gather_copies(x_refs, land_refs, axes, send_sems, recv_sems):
    xx, yy, cc = lax.axis_index("x"), lax.axis_index("y"), lax.axis_index("c")
    me = 4 * xx + 2 * yy + cc
    copies = []
    for t, (x_ref, land, axis) in enumerate(zip(x_refs, land_refs, axes)):
        n = x_ref.shape[axis]
        for k in range(1, N_DEV):
            px = 1 - xx if k & 4 else xx
            py = 1 - yy if k & 2 else yy
            pc = 1 - cc if k & 1 else cc
            s = (N_DEV - 1) * t + k - 1
            copies.append(pltpu.make_async_remote_copy(
                src_ref=x_ref, dst_ref=_slab(land, axis, me, n), send_sem=send_sems.at[s], recv_sem=recv_sems.at[s],
                device_id=(px, py, pc), device_id_type=MESH))
    return copies


def _place_own(x, axis, *, name):
    full = x.shape[:axis] + (N_DEV * x.shape[axis],) + x.shape[axis + 1:]
    lead = x.shape[0]
    tile = lead if axis == 0 else (256 if x.ndim == 2 and lead % 256 == 0 else 1 if x.ndim == 3 else lead)
    rest = (0,) * (x.ndim - 1)

    def body(land_ref, x_ref, o_ref):
        o_ref[...] = x_ref[...]

    def where(i):
        me = 4 * lax.axis_index("x") + 2 * lax.axis_index("y") + lax.axis_index("c")
        idx = [i] + list(rest)
        idx[axis] = me
        return tuple(idx)

    return pl.pallas_call(
        body, name=name, grid=(lead // tile,),
        in_specs=[ANY, pl.BlockSpec((tile,) + x.shape[1:], lambda i: (i,) + rest)],
        out_specs=pl.BlockSpec((tile,) + x.shape[1:], where),
        out_shape=jax.ShapeDtypeStruct(full, x.dtype), input_output_aliases={0: 0},
        compiler_params=_cparams(("arbitrary",)),
    )(lax.empty(full, x.dtype), x)


def gather_start(xs, axes, after, *, name):
    nt = len(xs)
    lands = [_place_own(x, axis, name=name + "_own") for x, axis in zip(xs, axes)]

    def body(*refs):
        x_refs, land_refs = refs[:nt], refs[nt:2 * nt]
        send_sems, recv_sems = refs[2 * nt + 1], refs[2 * nt + 2]
        token = refs[-1]
        for cp in _gather_copies(x_refs, land_refs, axes, send_sems, recv_sems):
            cp.start()
        token[...] = jnp.zeros_like(token)

    nsem = (N_DEV - 1) * nt
    outs = pl.pallas_call(
        body, name=name,
        out_shape=(pltpu.SemaphoreType.DMA((nsem,)), pltpu.SemaphoreType.DMA((nsem,)))
        + tuple(pltpu.HBM(a.shape, a.dtype) for a in list(xs) + lands) + (TOKEN,),
        in_specs=[HBM] * (2 * nt) + [ANY], out_specs=(SEM, SEM) + (HBM,) * (2 * nt) + (VM,),
        input_output_aliases={i: 2 + i for i in range(2 * nt)},
        compiler_params=pltpu.CompilerParams(has_side_effects=EFFECT),
    )(*[_in_hbm(a) for a in list(xs) + lands], after)
    return outs[:-1], outs[-1]


def gather_wait(handles, axes, after, *, name):
    send_sems, recv_sems = handles[0], handles[1]
    bufs = handles[2:]
    nt = len(bufs) // 2

    def body(*refs):
        x_refs, land_refs = refs[:nt], refs[nt:2 * nt]
        send_sems, recv_sems = refs[2 * nt], refs[2 * nt + 1]
        for cp in _gather_copies(x_refs, land_refs, axes, send_sems, recv_sems):
            cp.wait_send()
            cp.wait_recv()

    outs = pl.pallas_call(
        body, name=name, out_shape=tuple(pltpu.HBM(a.shape, a.dtype) for a in bufs),
        in_specs=[HBM] * (2 * nt) + [SEM, SEM, ANY], out_specs=(HBM,) * (2 * nt),
        input_output_aliases={i: i for i in range(2 * nt)},
        compiler_params=pltpu.CompilerParams(has_side_effects=EFFECT),
    )(*bufs, send_sems, recv_sems, after)
    return outs[nt:]


def _blockdiag(b, nb):
    j, _, r, c = b.shape
    eye = jnp.eye(nb, dtype=bool)[None, :, None, :, None]
    return jnp.where(eye, b[:, :, :, None, :], jnp.zeros((), b.dtype)).reshape(j, nb * r, nb * c)


def _diagblocks(d, nb):
    j, rr, cc = d.shape
    return jnp.einsum('jarac->jarc', d.reshape(j, nb, rr // nb, nb, cc // nb))


def _s5_b_dense(bbar):
    return _blockdiag(bbar.transpose(0, 2, 1).reshape(NCH, 8, S5_GROUP, S5_STATE), 8)


def _s5_b_undense(d):
    return _diagblocks(d, 8).reshape(S5_GROUPS, S5_GROUP, S5_STATE).transpose(0, 2, 1)


def _s5_c_dense(c):
    return _blockdiag(c.transpose(0, 2, 1).reshape(NCH, 8, S5_STATE, S5_GROUP), 8)


def _s5_c_undense(d):
    return _diagblocks(d, 8).reshape(S5_GROUPS, S5_STATE, S5_GROUP).transpose(0, 2, 1)


def _rg_dense(w):
    return _blockdiag(w.reshape(NCH, 2, RG_BLOCK, RG_BLOCK), 2)


def _rg_undense(d):
    return _diagblocks(d, 2).reshape(RG_BLOCKS, RG_BLOCK, RG_BLOCK)


def _chunks(v):
    return v.reshape(NCH, 1, LANE)


def _tri(tm):
    r = jnp.arange(tm)
    m = (r[:, None] >= r[None, :]) & (r[:, None] // HG_SUB == r[None, :] // HG_SUB)
    m = m.astype(f32)
    return m[None], m.T[None]


SMALL = ['norm_w', 's5_lambda_re', 's5_lambda_im', 's5_log_step', 's5_b_re', 's5_b_im', 's5_c_re', 's5_c_im',
         's5_d', 's5_b_glu', 'rg_conv_w', 'rg_conv_b', 'rg_w_a', 'rg_b_a', 'rg_w_x', 'rg_b_x', 'rg_lambda',
         'hg_lower_bounds', 'hg_norm_w', 'final_norm_w']
WEIGHTS = ['norm_w', 'w_in', 's5_lambda_re', 's5_lambda_im', 's5_log_step', 's5_b_re', 's5_b_im', 's5_c_re',
           's5_c_im', 's5_d', 's5_w_glu', 's5_b_glu', 'rg_conv_w', 'rg_conv_b', 'rg_w_a', 'rg_b_a', 'rg_w_x',
           'rg_b_x', 'rg_lambda', 'hg_lower_bounds', 'hg_norm_w', 'w_branch', 'w_out', 'final_norm_w']
PACK_ROWS = 512


def _pack(arrs):
    flat = jnp.concatenate([a.reshape(-1) for a in arrs])
    pad = (-flat.shape[0]) % (PACK_ROWS * LANE)
    return jnp.pad(flat, (0, pad)).reshape(1, -1, LANE)


def _unpack(buf, shapes):
    flat = buf.reshape(-1)
    out, off = [], 0
    for s in shapes:
        n = math.prod(s)
        out.append(flat[off:off + n].reshape(s))
        off += n
    return out


def _step(x, tgt, w, m, v):
    t = x.shape[0]
    tri, tri_t = _tri(min(LANE, t))
    me = 4 * lax.axis_index("x") + 2 * lax.axis_index("y") + lax.axis_index("c")

    big = ('w_in', 's5_w_glu', 'w_branch', 'w_out')
    big_axis = (1, 0, 2, 0)
    shards = lambda l: [w[k][l].astype(bf16) for k in big]
    win, wglu, wbr, wout = ([None] * DEPTH for _ in range(4))
    win[0] = all_gather(shards(0)[0], big_axis[0], name="ag_w_in")
    rest0_axis = big_axis[1:] + (1,)
    rest0, rest0_token = gather_start(shards(0)[1:] + [w['rg_conv_w'].reshape(DEPTH * RG_CONV, LANE)], rest0_axis,
                                      win[0], name="ag_start_0")

    lb_rows = [w['hg_lower_bounds'][l][None] for l in range(DEPTH)]
    lbs = whole(lb_prep_fn, lb_rows, [(1, W_MIX)] * DEPTH, name="lb_prep")

    saved = []
    for l in range(DEPTH):
        s = {}
        nw = w['norm_w'][l].reshape(1, 1, D_MODEL)
        (h,) = rowwise(ln_fn, [(x, 0, D_MODEL)], [nw], [], [(D_MODEL, bf16)], name="ln_fwd")
        token = None
        if l + 1 < DEPTH:
            handles, token = gather_start(shards(l + 1), big_axis, rest0_token if l == 0 else x, name=f"ag_start_{l + 1}")
        z = mm(h, win[l], after=token, name="mm_in", tm=1024)
        if l == 0:
            wglu[0], wbr[0], wout[0], conv_w = gather_wait(rest0, rest0_axis, z, name="ag_wait_0")
            conv_w = conv_w.reshape(DEPTH, RG_CONV, W_MIX)
        s5p = [w['s5_lambda_re'][l][..., None], w['s5_lambda_im'][l][..., None], w['s5_log_step'][l][:, None, None],
               w['s5_b_re'][l], w['s5_b_im'][l]]
        gp = (S5_GROUPS, S5_STATE)
        abar_re, abar_im, bbar_re, bbar_im = whole(
            s5_prep_fn, s5p, [gp + (1,), gp + (1,), gp + (S5_GROUP,), gp + (S5_GROUP,)], name="s5_prep")
        a_re, a_im = abar_re.reshape(NCH, 1, S5_SC), abar_im.reshape(NCH, 1, S5_SC)
        bd_re, bd_im = _s5_b_dense(bbar_re), _s5_b_dense(bbar_im)
        cd_re, cd_im = _s5_c_dense(w['s5_c_re'][l]), _s5_c_dense(w['s5_c_im'][l])
        yssm, xre, xim = s5_scan_fwd(z, bd_re, bd_im, cd_re, cd_im, a_re, a_im)
        s5post_p = [w['s5_d'][l].reshape(1, 1, W_MIX), wglu[l].astype(f32)[None], w['s5_b_glu'][l].reshape(1, 1, W_MIX)]
        s5post_rows = [(yssm, 0, W_MIX), (z, C_UA, W_MIX), (z, C_GA, W_MIX)]
        (ya,) = rowwise(s5_post_fn, s5post_rows, s5post_p, [], [(W_MIX, bf16)], name="s5_post_fwd")
        cw, cb = conv_w[l].reshape(RG_CONV, NCH, LANE).transpose(1, 0, 2), _chunks(w['rg_conv_b'][l])
        rg_p = [_rg_dense(w['rg_w_a'][l]), _chunks(w['rg_b_a'][l]), _rg_dense(w['rg_w_x'][l]),
                _chunks(w['rg_b_x'][l]), _chunks(w['rg_lambda'][l])]
        xc, ra, hb = rg_fwd(z, cw, cb, rg_p)
        hg_rows = [(z, C_Q, W_MIX), (z, C_F, W_MIX)]
        hg_p = [_chunks(lbs[l].reshape(W_MIX))]
        qs, kk, gcum = rowwise(hg_pre_fn, hg_rows, hg_p, [tri, tri_t], [(W_MIX, f32)] * 3, name="hg_pre_fwd", ncol=NCH, tm=TM_CHUNK)
        oc, sall = hg_chunk_fwd(qs, kk, gcum, z)
        bp_rows = [(hb, 0, W_MIX), (z, C_GB, W_MIX), (oc, 0, W_MIX), (z, C_GC, W_MIX)]
        bp_p = [_chunks(w['hg_norm_w'][l])]
        yb, yc = rowwise(branch_prep_fn, bp_rows, bp_p, [], [(W_MIX, bf16)] * 2, name="branch_prep_fwd", ncol=NCH, tm=TM_CHUNK)
        ys = [ya, yb, yc]
        br = [mm(ys[n], wbr[l][n], name="mm_branch", out_dtype=bf16) for n in range(N_BRANCH)]
        mg_rows = [(br[n], 0, D_MODEL) for n in range(N_BRANCH)] + [(z, C_GATE + n * D_MODEL, D_MODEL) for n in range(N_BRANCH)]
        (merged,) = rowwise(merge_fn, mg_rows, [], [], [(D_MODEL, bf16)], name="merge_fwd", ncol=2)
        x_new = mm(merged, wout[l], add=x, name="mm_out")
        s.update(x=x, h=h, z=z, s5p=s5p, s5=(bd_re, bd_im, cd_re, cd_im, a_re, a_im), xre=xre, xim=xim,
                 s5post_rows=s5post_rows, s5post_p=s5post_p, cw=cw, xc=xc, rg_p=rg_p, ra=ra, hb=hb,
                 hg_rows=hg_rows, hg_p=hg_p, qs=qs, kk=kk, gcum=gcum, sall=sall, bp_rows=bp_rows, bp_p=bp_p,
                 ys=ys, mg_rows=mg_rows, merged=merged, nw=nw)
        saved.append(s)
        x = x_new
        if l + 1 < DEPTH:
            win[l + 1], wglu[l + 1], wbr[l + 1], wout[l + 1] = gather_wait(handles, big_axis, x, name=f"ag_wait_{l + 1}")

    fnw = w['final_norm_w'].reshape(1, 1, D_MODEL)
    ones = jnp.ones((t, 1), f32)
    dx, d_fnw, loss_sum = rowwise_vjp(loss_fn, [(x, 0, D_MODEL), (tgt, 0, D_MODEL)], [fnw], [], [(ones, 0, 1)],
                                      [(0, f32)], name="loss_head", sum_primal=0)
    loss = lax.psum(loss_sum.reshape(()), ("x", "y", "c"))

    small_g = {k: [None] * DEPTH for k in SMALL if k != 'final_norm_w'}
    own_sums, in_flight = [None] * DEPTH, [None] * DEPTH
    d_lbs = [None] * DEPTH
    token = None
    for l in reversed(range(DEPTH)):
        s = saved[l]
        z = s['z']
        dxb = dx.astype(bf16)
        d_merged = mm(dxb, wout[l], bt=True, after=token, name="mm_out_dx", out_dtype=bf16)
        d_wout = mm(s['merged'], dxb, at=True, name="mm_out_dw", out_dtype=bf16)
        mg = rowwise_vjp(merge_fn, s['mg_rows'], [], [], [(d_merged, 0, D_MODEL)],
                         [(n, bf16) for n in range(2 * N_BRANCH)], name="merge_bwd", ncol=2)
        d_br, d_gl = mg[:N_BRANCH], mg[N_BRANCH:]
        d_ys = [mm(d_br[n], wbr[l][n], bt=True, name="mm_branch_dx", out_dtype=bf16) for n in range(N_BRANCH)]
        d_wbr = jnp.stack([mm(s['ys'][n], d_br[n], at=True, name="mm_branch_dw", out_dtype=bf16) for n in range(N_BRANCH)])
        d_hb, d_gb, d_oc, d_gc, d_hnw = rowwise_vjp(
            branch_prep_fn, s['bp_rows'], s['bp_p'], [], [(d_ys[1], 0, W_MIX), (d_ys[2], 0, W_MIX)],
            [(0, f32), (1, bf16), (2, f32), (3, bf16)], name="branch_prep_bwd", ncol=NCH, tm=TM_CHUNK)
        small_g['hg_norm_w'][l] = d_hnw.reshape(W_MIX)
        d_qs, d_kk, d_gcum, d_i = hg_chunk_bwd(d_oc, s['qs'], s['kk'], s['gcum'], z, s['sall'])
        d_q, d_f, d_lb = rowwise_vjp(
            hg_pre_fn, s['hg_rows'], s['hg_p'], [tri, tri_t], [(d_qs, 0, W_MIX), (d_kk, 0, W_MIX), (d_gcum, 0, W_MIX)],
            [(0, bf16), (1, bf16)], name="hg_pre_bwd", ncol=NCH, tm=TM_CHUNK)
        d_lbs[l] = d_lb.reshape(1, W_MIX)
        d_xb, d_cw, d_cb, d_wa, d_ba, d_wx, d_bx, d_lam = rg_bwd(d_hb, z, s['xc'], s['ra'], s['hb'], s['cw'], s['rg_p'])
        small_g['rg_w_a'][l], small_g['rg_w_x'][l] = _rg_undense(d_wa), _rg_undense(d_wx)
        small_g['rg_b_a'][l], small_g['rg_b_x'][l] = d_ba.reshape(W_MIX), d_bx.reshape(W_MIX)
        small_g['rg_lambda'][l] = d_lam.reshape(W_MIX)
        small_g['rg_conv_w'][l] = d_cw.transpose(1, 0, 2).reshape(RG_CONV, W_MIX)
        small_g['rg_conv_b'][l] = d_cb.reshape(W_MIX)
        d_yssm, d_u1, d_ga, d_d, d_wglu, d_bglu = rowwise_vjp(
            s5_post_fn, s['s5post_rows'], s['s5post_p'], [], [(d_ys[0], 0, W_MIX)],
            [(0, bf16), (1, bf16), (2, bf16)], name="s5_post_bwd")
        small_g['s5_d'][l], small_g['s5_b_glu'][l] = d_d.reshape(W_MIX), d_bglu.reshape(W_MIX)
        d_ua, d_bdre, d_bdim, d_cdre, d_cdim, d_are, d_aim = s5_scan_bwd(d_yssm, d_u1, z, s['xre'], s['xim'], *s['s5'])
        small_g['s5_c_re'][l], small_g['s5_c_im'][l] = _s5_c_undense(d_cdre), _s5_c_undense(d_cdim)
        gp = (S5_GROUPS, S5_STATE, 1)
        s5g = whole_vjp(s5_prep_fn, s['s5p'],
                        [d_are.reshape(gp), d_aim.reshape(gp), _s5_b_undense(d_bdre), _s5_b_undense(d_bdim)],
                        name="s5_prep_bwd")
        small_g['s5_lambda_re'][l] = s5g[0].reshape(S5_GROUPS, S5_STATE)
        small_g['s5_lambda_im'][l] = s5g[1].reshape(S5_GROUPS, S5_STATE)
        small_g['s5_log_step'][l] = s5g[2].reshape(S5_GROUPS)
        small_g['s5_b_re'][l], small_g['s5_b_im'][l] = s5g[3], s5g[4]
        dz = jnp.concatenate([d_ua, d_ga, d_xb, d_gb, d_q, d_f, d_i, d_gc] + list(d_gl), axis=1)
        d_win = mm(s['h'], dz, at=True, name="mm_in_dw", out_dtype=bf16)
        sums = [pair_sums(g, ax, name="rs_" + k) for g, ax, k in zip((d_win, d_wglu[0], d_wbr, d_wout), big_axis, big)]
        own_sums[l] = [own for own, _ in sums]
        in_flight[l], token = send_pairs_start([pair for _, pair in sums], s5g[0], name=f"rs_start_{l}")
        d_h = mm(dz, win[l], bt=True, after=token, name="mm_in_dx", tm=1024, tk=2048)
        dx, d_nw = rowwise_vjp(ln_res_fn, [(s['x'], 0, D_MODEL)], [s['nw']], [], [(d_h, 0, D_MODEL), (dx, 0, D_MODEL)],
                               [(0, f32)], name="ln_bwd")
        small_g['norm_w'][l] = d_nw.reshape(D_MODEL)
    d_lb_raw = whole_vjp(lb_prep_fn, lb_rows, d_lbs, name="lb_prep_bwd")
    small_g['hg_lower_bounds'] = [r.reshape(W_MIX) for r in d_lb_raw]

    per_layer = [k for k in SMALL if k != 'final_norm_w']
    shapes = [(DEPTH,) + small_g[k][0].shape for k in per_layer] + [(D_MODEL,)]
    pieces = [small_g[k][l] for k in per_layer for l in range(DEPTH)] + [d_fnw]
    small_in_flight, small_token = gather_start([_pack(pieces)[0].astype(bf16)], (0,), dx, name="ag_small_start")
    res = {}

    arrived = [send_pairs_wait(in_flight[l], small_token, name=f"rs_wait_{l}") for l in range(DEPTH)]
    for i, (k, tr) in enumerate((('w_in', 32), ('s5_w_glu', 32), ('w_branch', 128), ('w_out', 32))):
        shp = w[k].shape
        r3 = lambda a: a.reshape(DEPTH, -1, shp[-1])
        slots = [[own_sums[l][i], arrived[l][i]] for l in range(DEPTH)]
        outs = adamw(r3(w[k]), r3(m[k]), r3(v[k]), slots, name="adamw_" + k, tr=tr)
        for kind, buf in zip(('grad', 'delta', 'new_m', 'new_v'), outs):
            res[kind + '_' + k] = buf.reshape(shp)

    (g_all,) = gather_wait(small_in_flight, (0,), outs[0], name="ag_small_wait")
    g_all = g_all.reshape(N_DEV, -1, LANE)

    def local(d, k):
        return jnp.zeros(shapes[SMALL.index(k)], f32) if k == 'rg_conv_w' else d[k]
    packed = [_pack([local(d, k) for k in SMALL]) for d in (w, m, v)]
    outs = adamw(*packed, [[g_all]], name="adamw_small", tr=512)
    for kind, buf in zip(('grad', 'delta', 'new_m', 'new_v'), outs):
        for k, a in zip(SMALL, _unpack(buf, shapes)):
            res[kind + '_' + k] = a
    g_cw = lax.dynamic_slice_in_dim(res['grad_rg_conv_w'], me * LANE, LANE, axis=2)
    cw3 = lambda a: a.reshape(1, DEPTH * RG_CONV, LANE)
    outs = adamw(cw3(w['rg_conv_w']), cw3(m['rg_conv_w']), cw3(v['rg_conv_w']), [[cw3(g_cw)]], name="adamw_conv_w", tr=16)
    for kind, buf in zip(('grad', 'delta', 'new_m', 'new_v'), outs):
        res[kind + '_rg_conv_w'] = buf.reshape(DEPTH, RG_CONV, LANE)

    return (loss, dx[None]) + tuple(res[kind + '_' + k] for kind in ('grad', 'delta', 'new_m', 'new_v') for k in WEIGHTS)


def kernel(x, norm_w, w_in, s5_lambda_re, s5_lambda_im, s5_log_step, s5_b_re, s5_b_im, s5_c_re, s5_c_im, s5_d, s5_w_glu, s5_b_glu, rg_conv_w, rg_conv_b, rg_w_a, rg_b_a, rg_w_x, rg_b_x, rg_lambda, hg_lower_bounds, hg_norm_w, w_branch, w_out, final_norm_w, loss_target, m_norm_w, m_w_in, m_s5_lambda_re, m_s5_lambda_im, m_s5_log_step, m_s5_b_re, m_s5_b_im, m_s5_c_re, m_s5_c_im, m_s5_d, m_s5_w_glu, m_s5_b_glu, m_rg_conv_w, m_rg_conv_b, m_rg_w_a, m_rg_b_a, m_rg_w_x, m_rg_b_x, m_rg_lambda, m_hg_lower_bounds, m_hg_norm_w, m_w_branch, m_w_out, m_final_norm_w, v_norm_w, v_w_in, v_s5_lambda_re, v_s5_lambda_im, v_s5_log_step, v_s5_b_re, v_s5_b_im, v_s5_c_re, v_s5_c_im, v_s5_d, v_s5_w_glu, v_s5_b_glu, v_rg_conv_w, v_rg_conv_b, v_rg_w_a, v_rg_b_a, v_rg_w_x, v_rg_b_x, v_rg_lambda, v_hg_lower_bounds, v_hg_norm_w, v_w_branch, v_w_out, v_final_norm_w):
    w = dict(zip(WEIGHTS, (norm_w, w_in, s5_lambda_re, s5_lambda_im, s5_log_step, s5_b_re, s5_b_im, s5_c_re, s5_c_im, s5_d, s5_w_glu, s5_b_glu, rg_conv_w, rg_conv_b, rg_w_a, rg_b_a, rg_w_x, rg_b_x, rg_lambda, hg_lower_bounds, hg_norm_w, w_branch, w_out, final_norm_w)))
    m = dict(zip(WEIGHTS, (m_norm_w, m_w_in, m_s5_lambda_re, m_s5_lambda_im, m_s5_log_step, m_s5_b_re, m_s5_b_im, m_s5_c_re, m_s5_c_im, m_s5_d, m_s5_w_glu, m_s5_b_glu, m_rg_conv_w, m_rg_conv_b, m_rg_w_a, m_rg_b_a, m_rg_w_x, m_rg_b_x, m_rg_lambda, m_hg_lower_bounds, m_hg_norm_w, m_w_branch, m_w_out, m_final_norm_w)))
    v = dict(zip(WEIGHTS, (v_norm_w, v_w_in, v_s5_lambda_re, v_s5_lambda_im, v_s5_log_step, v_s5_b_re, v_s5_b_im, v_s5_c_re, v_s5_c_im, v_s5_d, v_s5_w_glu, v_s5_b_glu, v_rg_conv_w, v_rg_conv_b, v_rg_w_a, v_rg_b_a, v_rg_w_x, v_rg_b_x, v_rg_lambda, v_hg_lower_bounds, v_hg_norm_w, v_w_branch, v_w_out, v_final_norm_w)))
    return _step(x[0], loss_target[0], w, m, v)
```

```python
import functools
import math

import jax
import jax.numpy as jnp
from jax import lax
from jax.experimental import pallas as pl
from jax.experimental.pallas import tpu as pltpu

f32 = jnp.float32
bf16 = jnp.bfloat16

D_MODEL = 2048
W_MIX = 1024
DEPTH = 4
N_BRANCH = 3
N_IN = 8 * W_MIX + N_BRANCH * D_MODEL
S5_GROUPS, S5_STATE, S5_GROUP = 64, 64, 16
RG_BLOCKS, RG_BLOCK, RG_CONV, RG_C = 16, 64, 4, 8.0
HG_HEADS, HG_DK = 8, 128
HG_SUB = 16
EPS = 1e-6
ADAM_LR, ADAM_B1, ADAM_B2, ADAM_EPS, ADAM_WD, ADAM_STEP = 0.001, 0.9, 0.999, 1e-08, 0.01, 10

N_DEV = 8
LANE = 128
NCH = W_MIX // LANE
TM_CHUNK = 1024
VMEM_LIMIT = 56 * 1024 * 1024
MESH = pl.DeviceIdType.MESH
ANY = pl.BlockSpec(memory_space=pl.ANY)
HIGHEST = lax.Precision.HIGHEST

C_UA, C_GA, C_XB, C_GB, C_Q, C_F, C_I, C_GC, C_GATE = (W_MIX * k for k in range(9))


def _cparams(sem=None):
    return pltpu.CompilerParams(dimension_semantics=sem, vmem_limit_bytes=VMEM_LIMIT)


@jax.custom_vjp
def bdot(a, w):
    return jnp.dot(a.astype(bf16), w.astype(bf16), preferred_element_type=f32)


def _bdot_fwd(a, w):
    return bdot(a, w), (a, w)


def _bdot_bwd(res, g):
    a, w = res
    gb = g.astype(bf16)
    da = lax.dot_general(gb, w.astype(bf16), (((1,), (1,)), ((), ())), preferred_element_type=f32)
    dw = lax.dot_general(a.astype(bf16), gb, (((0,), (0,)), ((), ())), preferred_element_type=f32)
    return da, dw


bdot.defvjp(_bdot_fwd, _bdot_bwd)


def _blockmm(c, a):
    n = c.shape[0]
    return jnp.concatenate([jnp.dot(c, a[i:i + n], preferred_element_type=f32, precision=HIGHEST)
                            for i in range(0, a.shape[0], n)], axis=0)


@jax.custom_vjp
def cdot(c, ct, a):
    return _blockmm(c, a)


def _cdot_fwd(c, ct, a):
    return cdot(c, ct, a), (c, ct)


def _cdot_bwd(res, g):
    c, ct = res
    return jnp.zeros_like(c), jnp.zeros_like(ct), _blockmm(ct, g)


cdot.defvjp(_cdot_fwd, _cdot_bwd)


def mm(a, b, *, name, out_dtype=f32, add=None, after=None, at=False, bt=False, tm=1024, tn=1024, tk=4096):
    k, m = a.shape if at else a.shape[::-1]
    n = b.shape[0] if bt else b.shape[1]
    tm, tn, tk = min(tm, m), min(tn, n), min(tk, k)
    assert m % tm == 0 and n % tn == 0 and k % tk == 0
    nk = k // tk
    dims = (((0 if at else 1,), (1 if bt else 0,)), ((), ()))

    def body(*refs):
        a_ref, b_ref = refs[:2]
        r_ref = refs[2] if add is not None else None
        o_ref = refs[-1] if nk == 1 else refs[-2]
        part = lax.dot_general(a_ref[...], b_ref[...], dims, preferred_element_type=f32)
        if nk == 1:
            if add is not None:
                part = part + r_ref[...]
            o_ref[...] = part.astype(out_dtype)
            return
        acc_ref = refs[-1]
        kk = pl.program_id(2)

        @pl.when(kk == 0)
        def _():
            acc_ref[...] = part

        @pl.when(kk > 0)
        def _():
            acc_ref[...] = acc_ref[...] + part

        @pl.when(kk == nk - 1)
        def _():
            acc = acc_ref[...]
            if add is not None:
                acc = acc + r_ref[...]
            o_ref[...] = acc.astype(out_dtype)

    b_spec = pl.BlockSpec((tn, tk), lambda i, j, q: (j, q)) if bt else pl.BlockSpec((tk, tn), lambda i, j, q: (q, j))
    a_spec = pl.BlockSpec((tk, tm), lambda i, j, q: (q, i)) if at else pl.BlockSpec((tm, tk), lambda i, j, q: (i, q))
    in_specs = [a_spec, b_spec]
    args = [a, b]
    if add is not None:
        in_specs.append(pl.BlockSpec((tm, tn), lambda i, j, q: (i, j)))
        args.append(add)
    if after is not None:
        in_specs.append(pl.BlockSpec(after.shape, lambda i, j, q: (0, 0)))
        args.append(after)
    return pl.pallas_call(
        body, name=name, grid=(m // tm, n // tn, nk), in_specs=in_specs,
        out_specs=pl.BlockSpec((tm, tn), lambda i, j, q: (i, j)),
        out_shape=jax.ShapeDtypeStruct((m, n), out_dtype),
        scratch_shapes=[] if nk == 1 else [pltpu.VMEM((tm, tn), f32)],
        compiler_params=_cparams(("parallel", "parallel", "arbitrary")),
    )(*args)


def _row_spec(tm, wc, col_off):
    base = col_off // wc
    assert col_off % wc == 0
    return pl.BlockSpec((tm, wc), lambda j, i: (i, base + j))


def _slab_spec(arr):
    r, c = arr.shape[1:]
    if arr.shape[0] == 1:
        return pl.BlockSpec((1, r, c), lambda j, i: (0, 0, 0))
    return pl.BlockSpec((1, r, c), lambda j, i: (j, 0, 0))


def rowwise(fn, rows, params, consts, outs, *, name, tm=256, ncol=1, rowid=False):
    t = rows[0][0].shape[0]
    tm = min(tm, t)
    nr, npar, nc, no = len(rows), len(params), len(consts), len(outs)

    def body(*refs):
        r = [refs[k][...].astype(f32) for k in range(nr)]
        p = [refs[nr + k][0] for k in range(npar + nc)]
        extra = ()
        if rowid:
            extra = (pl.program_id(1) * tm + lax.broadcasted_iota(jnp.int32, (tm, 1), 0),)
        res = fn(*extra, *r, *p)
        for k in range(no):
            refs[nr + npar + nc + k][...] = res[k].astype(outs[k][1])

    in_specs = [_row_spec(tm, w // ncol, off) for (_, off, w) in rows]
    in_specs += [_slab_spec(a) for a in list(params) + list(consts)]
    out_specs = [pl.BlockSpec((tm, w // ncol), lambda j, i: (i, j)) for (w, _) in outs]
    out_shape = [jax.ShapeDtypeStruct((t, w), dt) for (w, dt) in outs]
    return pl.pallas_call(
        body, name=name, grid=(ncol, t // tm), in_specs=in_specs, out_specs=out_specs, out_shape=out_shape,
        compiler_params=_cparams(("parallel", "parallel")),
    )(*[r[0] for r in rows], *params, *consts)


def rowwise_vjp(fn, rows, params, consts, cts, d_rows, *, name, tm=256, ncol=1, rowid=False, sum_primal=None):
    t = rows[0][0].shape[0]
    tm = min(tm, t)
    nr, npar, nc, nct, ndr = len(rows), len(params), len(consts), len(cts), len(d_rows)

    def body(*refs):
        i = pl.program_id(1)
        r = [refs[k][...].astype(f32) for k in range(nr)]
        p = [refs[nr + k][0] for k in range(npar)]
        c = [refs[nr + npar + k][0] for k in range(nc)]
        g = [refs[nr + npar + nc + k][...].astype(f32) for k in range(nct)]
        orefs = refs[nr + npar + nc + nct:]
        extra = ()
        if rowid:
            extra = (i * tm + lax.broadcasted_iota(jnp.int32, (tm, 1), 0),)
        res, vjp = jax.vjp(lambda *v: fn(*extra, *v, *c), *r, *p)
        grads = vjp(tuple(g))
        for k, (idx, dt) in enumerate(d_rows):
            orefs[k][...] = grads[idx].astype(dt)
        acc = [grads[nr + k] for k in range(npar)]
        if sum_primal is not None:
            acc.append(jnp.sum(res[sum_primal], axis=0, keepdims=True))

        @pl.when(i == 0)
        def _():
            for k, a in enumerate(acc):
                orefs[ndr + k][0] = a

        @pl.when(i > 0)
        def _():
            for k, a in enumerate(acc):
                orefs[ndr + k][0] = orefs[ndr + k][0] + a

    in_specs = [_row_spec(tm, w // ncol, off) for (_, off, w) in rows]
    in_specs += [_slab_spec(a) for a in list(params) + list(consts)]
    in_specs += [_row_spec(tm, w // ncol, off) for (_, off, w) in cts]
    out_specs = [pl.BlockSpec((tm, rows[idx][2] // ncol), lambda j, i: (i, j)) for (idx, _) in d_rows]
    out_shape = [jax.ShapeDtypeStruct((t, rows[idx][2]), dt) for (idx, dt) in d_rows]
    for a in params:
        out_specs.append(pl.BlockSpec((1,) + a.shape[1:], lambda j, i: (j, 0, 0)))
        out_shape.append(jax.ShapeDtypeStruct(a.shape, f32))
    if sum_primal is not None:
        w = cts[sum_primal][2]
        out_specs.append(pl.BlockSpec((1, 1, w // ncol), lambda j, i: (j, 0, 0)))
        out_shape.append(jax.ShapeDtypeStruct((ncol, 1, w // ncol), f32))
    return pl.pallas_call(
        body, name=name, grid=(ncol, t // tm), in_specs=in_specs, out_specs=out_specs, out_shape=out_shape,
        compiler_params=_cparams(("parallel", "arbitrary")),
    )(*[r[0] for r in rows], *params, *consts, *[c[0] for c in cts])


VM = pl.BlockSpec(memory_space=pltpu.VMEM)


def whole(fn, ins, outs, *, name):
    def body(*refs):
        res = fn(*[r[...] for r in refs[:len(ins)]])
        for k, o in enumerate(refs[len(ins):]):
            o[...] = res[k]
    return pl.pallas_call(body, name=name, in_specs=[VM] * len(ins), out_specs=[VM] * len(outs),
                          out_shape=[jax.ShapeDtypeStruct(s, f32) for s in outs],
                          compiler_params=_cparams())(*ins)


def whole_vjp(fn, ins, cts, *, name):
    n = len(ins)

    def body(*refs):
        _, vjp = jax.vjp(fn, *[r[...] for r in refs[:n]])
        grads = vjp(tuple(r[...] for r in refs[n:n + len(cts)]))
        for k, o in enumerate(refs[n + len(cts):]):
            o[...] = grads[k]
    return pl.pallas_call(body, name=name, in_specs=[VM] * (n + len(cts)), out_specs=[VM] * n,
                          out_shape=[jax.ShapeDtypeStruct(a.shape, f32) for a in ins],
                          compiler_params=_cparams())(*ins, *cts)


def ln_fn(x, w):
    return (x * lax.rsqrt(jnp.mean(x * x, axis=-1, keepdims=True) + EPS) * w,)


def ln_res_fn(x, w):
    return ln_fn(x, w)[0], x


def loss_fn(x, tgt, w):
    y = ln_fn(x, w)[0]
    return (0.5 * jnp.mean(jnp.square(y - tgt), axis=-1, keepdims=True),)


def s5_prep_fn(lam_re, lam_im, log_step, b_re, b_im):
    step = jnp.exp(log_step)
    mag = jnp.exp(lam_re * step)
    ang = lam_im * step
    abar_re = mag * jnp.cos(ang)
    abar_im = mag * jnp.sin(ang)
    num_re = abar_re - 1.0
    num_im = abar_im
    den = lam_re * lam_re + lam_im * lam_im
    coef_re = (num_re * lam_re + num_im * lam_im) / den
    coef_im = (num_im * lam_re - num_re * lam_im) / den
    bbar_re = coef_re * b_re - coef_im * b_im
    bbar_im = coef_re * b_im + coef_im * b_re
    return abar_re, abar_im, bbar_re, bbar_im


def lb_prep_fn(r0, r1, r2, r3):
    m = jnp.maximum(jnp.maximum(r0, r1), jnp.maximum(r2, r3))
    e0, e1, e2, e3 = jnp.exp(r0 - m), jnp.exp(r1 - m), jnp.exp(r2 - m), jnp.exp(r3 - m)
    s = e0 + e1 + e2 + e3
    p0, p1, p2, p3 = e0 / s, e1 / s, e2 / s, e3 / s
    c1 = p0 + p1
    c2 = c1 + p2
    c3 = c2 + p3
    return p0 - p0, c1 - p0, c2 - p0, c3 - p0


def s5_post_fn(yssm, u, ga, d, wglu, bglu):
    y = jax.nn.gelu(yssm + d * u)
    y = y * jax.nn.sigmoid(bdot(y, wglu) + bglu)
    return (y * jax.nn.silu(ga),)


def rg_gate_fn(tglob, xc, wa, ba, wx, bx, lam):
    r = jax.nn.sigmoid(bdot(xc, wa) + ba)
    i = jax.nn.sigmoid(bdot(xc, wx) + bx)
    log_a = -RG_C * r * jax.nn.softplus(-lam)
    a = jnp.exp(log_a)
    mult = jnp.sqrt(-jnp.tanh(log_a) * (a * a + 1.0))
    mult = jnp.where(tglob == 0, 1.0, mult)
    return a, mult * (i * xc)


def hg_pre_fn(q, fl, lb, tri, tri_t):
    f = lb + (1.0 - lb) * jax.nn.sigmoid(fl)
    return jax.nn.silu(q), 1.0 - f, cdot(tri, tri_t, jnp.log(f))


def branch_prep_fn(hb, gb, oc, gc, nw):
    yb = hb * jax.nn.silu(gb)
    on = oc * lax.rsqrt(jnp.mean(oc * oc, axis=-1, keepdims=True) + EPS) * nw
    return yb, on * jax.nn.silu(gc)


def merge_fn(b0, b1, b2, g0, g1, g2):
    return (jax.nn.sigmoid(g0) * b0 + jax.nn.sigmoid(g1) * b1 + jax.nn.sigmoid(g2) * b2,)


S5_TB = 512
S5_SC = 512


SEG = 8


def _shift(v, k, pos, period, reverse, fill):
    if reverse:
        return jnp.where(pos < period - k, pltpu.roll(v, v.shape[0] - k, 0), fill)
    return jnp.where(pos >= k, pltpu.roll(v, k, 0), fill)


def _cmul(ar, ai, br, bi):
    return ar * br - ai * bi, ar * bi + ai * br


def _edge_rows(v, first):
    r0 = 0 if first else SEG - 1
    return jnp.concatenate([v[r:r + 1, :] for r in range(r0, v.shape[0], SEG)], axis=0)


def _spread(s):
    return jnp.concatenate([jnp.broadcast_to(s[g:g + 1, :], (SEG, s.shape[1])) for g in range(s.shape[0])], axis=0)


def lti_scan(xr, xi, ar, ai, cr, ci, reverse=False):
    n = xr.shape[0]
    g = n // SEG
    sub = lax.broadcasted_iota(jnp.int32, (n, 1), 0) & (SEG - 1)
    sub8 = lax.broadcasted_iota(jnp.int32, (SEG, 1), 0)
    grow = lax.broadcasted_iota(jnp.int32, (g, 1), 0)
    pr, pi_ = ar, ai
    wr, wi = jnp.broadcast_to(ar, (SEG, ar.shape[1])), jnp.broadcast_to(ai, (SEG, ai.shape[1]))
    k = 1
    while k < SEG:
        tr, ti = _cmul(pr, pi_, _shift(xr, k, sub, SEG, reverse, 0.0), _shift(xi, k, sub, SEG, reverse, 0.0))
        xr, xi = xr + tr, xi + ti
        pr, pi_ = _cmul(pr, pi_, pr, pi_)
        wr, wi = _cmul(wr, wi, _shift(wr, k, sub8, SEG, reverse, 1.0), _shift(wi, k, sub8, SEG, reverse, 0.0))
        k *= 2
    first, last = (g - 1, 0) if reverse else (0, g - 1)
    jr, ji = _cmul(pr, pi_, cr, ci)
    sr = _edge_rows(xr, reverse) + jnp.where(grow == first, jr, 0.0)
    si = _edge_rows(xi, reverse) + jnp.where(grow == first, ji, 0.0)
    k = 1
    while k < g:
        tr, ti = _cmul(pr, pi_, _shift(sr, k, grow, g, reverse, 0.0), _shift(si, k, grow, g, reverse, 0.0))
        sr, si = sr + tr, si + ti
        pr, pi_ = _cmul(pr, pi_, pr, pi_)
        k *= 2
    er, ei = _spread(_shift(sr, 1, grow, g, reverse, cr)), _spread(_shift(si, 1, grow, g, reverse, ci))
    tr, ti = _cmul(jnp.tile(wr, (g, 1)), jnp.tile(wi, (g, 1)), er, ei)
    return xr + tr, xi + ti, sr[last:last + 1, :], si[last:last + 1, :]


def tv_scan(aa, bb, carry, reverse=False):
    n = aa.shape[0]
    row = lax.broadcasted_iota(jnp.int32, (n, 1), 0)
    k = 1
    while k < n:
        bb = bb + aa * _shift(bb, k, row, n, reverse, 0.0)
        aa = aa * _shift(aa, k, row, n, reverse, 1.0)
        k *= 2
    h = bb + aa * carry
    last = 0 if reverse else n - 1
    return h, h[last:last + 1, :]


def s5_scan_fwd(z, bd_re, bd_im, cd_re, cd_im, a_re, a_im):
    t = z.shape[0]
    tb = min(S5_TB, t)

    def body(u_ref, bre, bim, cre, cim, are, aim, y_ref, xre_ref, xim_ref, car_re, car_im):
        @pl.when(pl.program_id(1) == 0)
        def _():
            car_re[...] = jnp.zeros_like(car_re)
            car_im[...] = jnp.zeros_like(car_im)

        u = u_ref[...].astype(bf16)
        xr, xi, car_re[...], car_im[...] = lti_scan(
            jnp.dot(u, bre[0], preferred_element_type=f32), jnp.dot(u, bim[0], preferred_element_type=f32),
            are[0], aim[0], car_re[...], car_im[...])
        xre_ref[...] = xr
        xim_ref[...] = xi
        y_ref[...] = (jnp.dot(xr.astype(bf16), cre[0], preferred_element_type=f32)
                      - jnp.dot(xi.astype(bf16), cim[0], preferred_element_type=f32))

    chunk = lambda r, c: pl.BlockSpec((1, r, c), lambda j, i: (j, 0, 0))
    return pl.pallas_call(
        body, name="s5_scan_fwd", grid=(NCH, t // tb),
        in_specs=[pl.BlockSpec((tb, LANE), lambda j, i: (i, C_UA // LANE + j)),
                  chunk(LANE, S5_SC), chunk(LANE, S5_SC), chunk(S5_SC, LANE), chunk(S5_SC, LANE),
                  chunk(1, S5_SC), chunk(1, S5_SC)],
        out_specs=[pl.BlockSpec((tb, LANE), lambda j, i: (i, j)),
                   pl.BlockSpec((tb, S5_SC), lambda j, i: (i, j)),
                   pl.BlockSpec((tb, S5_SC), lambda j, i: (i, j))],
        out_shape=[jax.ShapeDtypeStruct((t, W_MIX), f32),
                   jax.ShapeDtypeStruct((t, NCH * S5_SC), f32),
                   jax.ShapeDtypeStruct((t, NCH * S5_SC), f32)],
        scratch_shapes=[pltpu.VMEM((1, S5_SC), f32)] * 2,
        compiler_params=_cparams(("parallel", "arbitrary")),
    )(z, bd_re.astype(bf16), bd_im.astype(bf16), cd_re.astype(bf16), cd_im.astype(bf16), a_re, a_im)


def s5_scan_bwd(dy, du1, z, xre, xim, bd_re, bd_im, cd_re, cd_im, a_re, a_im):
    t = z.shape[0]
    tb = min(S5_TB, t)
    nt = t // tb

    def body(dy_ref, du1_ref, u_ref, xre_ref, xim_ref, hre_ref, him_ref, bre, bim, cre, cim, are, aim,
             du_ref, dbre, dbim, dcre, dcim, dare, daim, car_re, car_im):
        step = pl.program_id(1)
        tt = nt - 1 - step

        @pl.when(step == 0)
        def _():
            car_re[...] = jnp.zeros_like(car_re)
            car_im[...] = jnp.zeros_like(car_im)

        nt_dims = (((1,), (1,)), ((), ()))
        tn_dims = (((0,), (0,)), ((), ()))
        dyb = dy_ref[...].astype(bf16)
        row = lax.broadcasted_iota(jnp.int32, (tb, 1), 0)
        xr, xi = xre_ref[...], xim_ref[...]
        ar, ai = are[0], aim[0]
        lr, li, car_re[...], car_im[...] = lti_scan(
            lax.dot_general(dyb, cre[0], nt_dims, preferred_element_type=f32),
            -lax.dot_general(dyb, cim[0], nt_dims, preferred_element_type=f32),
            ar, -ai, car_re[...], car_im[...], reverse=True)
        lrb, lib = lr.astype(bf16), li.astype(bf16)
        ub = u_ref[...].astype(bf16)
        du = (lax.dot_general(lrb, bre[0], nt_dims, preferred_element_type=f32)
              + lax.dot_general(lib, bim[0], nt_dims, preferred_element_type=f32))
        du_ref[...] = (du + du1_ref[...].astype(f32)).astype(du_ref.dtype)
        live = (tt > 0).astype(f32)
        xpr = jnp.where(row == 0, hre_ref[7:8, :] * live, pltpu.roll(xr, 1, 0))
        xpi = jnp.where(row == 0, him_ref[7:8, :] * live, pltpu.roll(xi, 1, 0))
        acc = [
            lax.dot_general(ub, lrb, tn_dims, preferred_element_type=f32),
            lax.dot_general(ub, lib, tn_dims, preferred_element_type=f32),
            lax.dot_general(xr.astype(bf16), dyb, tn_dims, preferred_element_type=f32),
            -lax.dot_general(xi.astype(bf16), dyb, tn_dims, preferred_element_type=f32),
            jnp.sum(lr * xpr + li * xpi, axis=0, keepdims=True),
            jnp.sum(li * xpr - lr * xpi, axis=0, keepdims=True),
        ]
        outs = [dbre, dbim, dcre, dcim, dare, daim]

        @pl.when(step == 0)
        def _():
            for o, a in zip(outs, acc):
                o[0] = a

        @pl.when(step > 0)
        def _():
            for o, a in zip(outs, acc):
                o[0] = o[0] + a

    chunk = lambda r, c: pl.BlockSpec((1, r, c), lambda j, i: (j, 0, 0))
    rev = lambda w, base=0: pl.BlockSpec((tb, w), lambda j, i: (nt - 1 - i, base + j))
    halo = pl.BlockSpec((8, S5_SC), lambda j, i: (jnp.maximum((nt - 1 - i) * (tb // 8) - 1, 0), j))
    return pl.pallas_call(
        body, name="s5_scan_bwd", grid=(NCH, nt),
        in_specs=[rev(LANE), rev(LANE), rev(LANE, C_UA // LANE), rev(S5_SC), rev(S5_SC), halo, halo,
                  chunk(LANE, S5_SC), chunk(LANE, S5_SC), chunk(S5_SC, LANE), chunk(S5_SC, LANE),
                  chunk(1, S5_SC), chunk(1, S5_SC)],
        out_specs=[rev(LANE), chunk(LANE, S5_SC), chunk(LANE, S5_SC), chunk(S5_SC, LANE), chunk(S5_SC, LANE),
                   chunk(1, S5_SC), chunk(1, S5_SC)],
        out_shape=[jax.ShapeDtypeStruct((t, W_MIX), bf16),
                   jax.ShapeDtypeStruct((NCH, LANE, S5_SC), f32), jax.ShapeDtypeStruct((NCH, LANE, S5_SC), f32),
                   jax.ShapeDtypeStruct((NCH, S5_SC, LANE), f32), jax.ShapeDtypeStruct((NCH, S5_SC, LANE), f32),
                   jax.ShapeDtypeStruct((NCH, 1, S5_SC), f32), jax.ShapeDtypeStruct((NCH, 1, S5_SC), f32)],
        scratch_shapes=[pltpu.VMEM((1, S5_SC), f32)] * 2,
        compiler_params=_cparams(("parallel", "arbitrary")),
    )(dy, du1, z, xre, xim, xre, xim, bd_re.astype(bf16), bd_im.astype(bf16), cd_re.astype(bf16),
      cd_im.astype(bf16), a_re, a_im)


RG_TB = 512


def _rg_conv(ext, w, cb, tb):
    acc = cb + w[3:4, :] * ext[8:, :]
    for k in range(3):
        acc = acc + w[k:k + 1, :] * pltpu.roll(ext, 3 - k, 0)[8:, :]
    return acc


def rg_fwd(z, cw, cb, gate_p):
    t = z.shape[0]
    tb = min(RG_TB, t)

    def body(x_ref, halo_ref, cw_ref, cb_ref, wa, ba, wx, bx, lam, xc_ref, a_ref, h_ref, car):
        i = pl.program_id(1)

        @pl.when(i == 0)
        def _():
            car[...] = jnp.zeros_like(car)

        ext = jnp.concatenate([halo_ref[...] * (i > 0).astype(f32), x_ref[...]], axis=0)
        xc = _rg_conv(ext, cw_ref[0], cb_ref[0], tb)
        tglob = i * tb + lax.broadcasted_iota(jnp.int32, (tb, 1), 0)
        a, b = rg_gate_fn(tglob, xc, wa[0], ba[0], wx[0], bx[0], lam[0])
        xc_ref[...] = xc
        a_ref[...] = a
        h_ref[...], car[...] = tv_scan(a, b, car[...])

    base = C_XB // LANE
    out = pl.BlockSpec((tb, LANE), lambda j, i: (i, j))
    return pl.pallas_call(
        body, name="rg_fwd", grid=(NCH, t // tb),
        in_specs=[pl.BlockSpec((tb, LANE), lambda j, i: (i, base + j)),
                  pl.BlockSpec((8, LANE), lambda j, i: (jnp.maximum(i * (tb // 8) - 1, 0), base + j)),
                  _slab_spec(cw), _slab_spec(cb)] + [_slab_spec(p) for p in gate_p],
        out_specs=[out, out, out], out_shape=[jax.ShapeDtypeStruct((t, W_MIX), f32)] * 3,
        scratch_shapes=[pltpu.VMEM((1, LANE), f32)],
        compiler_params=_cparams(("parallel", "arbitrary")),
    )(z, z, cw, cb, *gate_p)


def rg_bwd(dh, z, xc, a, h, cw, gate_p):
    t = z.shape[0]
    tb = min(RG_TB, t)
    nt = t // tb

    def body(g_ref, a_ref, an_ref, h_ref, hp_ref, xc_ref, x_ref, xh_ref, cw_ref, wa, ba, wx, bx, lam,
             dx_ref, dcw_ref, dcb_ref, dwa, dba, dwx, dbx, dlam, car, later):
        step = pl.program_id(1)
        tt = nt - 1 - step

        @pl.when(step == 0)
        def _():
            car[...] = jnp.zeros_like(car)
            later[...] = jnp.zeros_like(later)

        row = lax.broadcasted_iota(jnp.int32, (tb, 1), 0)
        an = an_ref[0:1, :] * (tt < nt - 1).astype(f32)
        aa = jnp.where(row == tb - 1, an, pltpu.roll(a_ref[...], tb - 1, 0))
        lmb, car[...] = tv_scan(aa, g_ref[...], car[...], reverse=True)
        hp = jnp.where(row == 0, hp_ref[7:8, :] * (tt > 0).astype(f32), pltpu.roll(h_ref[...], 1, 0))
        tglob = tt * tb + row
        _, vjp = jax.vjp(lambda *v: rg_gate_fn(tglob, *v), xc_ref[...], wa[0], ba[0], wx[0], bx[0], lam[0])
        g, *dp = vjp((lmb * hp, lmb))
        gext = jnp.concatenate([g, later[...]], axis=0)
        later[...] = g[0:8, :]
        xext = jnp.concatenate([xh_ref[...] * (tt > 0).astype(f32), x_ref[...]], axis=0)
        w = cw_ref[0]
        dx = w[3:4, :] * g
        taps = [None] * RG_CONV
        taps[3] = jnp.sum(g * xext[8:, :], axis=0, keepdims=True)
        for k in range(3):
            s = 3 - k
            dx = dx + w[k:k + 1, :] * pltpu.roll(gext, tb + 8 - s, 0)[:tb, :]
            taps[k] = jnp.sum(g * pltpu.roll(xext, s, 0)[8:, :], axis=0, keepdims=True)
        dx_ref[...] = dx.astype(dx_ref.dtype)
        acc = [jnp.concatenate(taps, axis=0), jnp.sum(g, axis=0, keepdims=True)] + dp
        outs = [dcw_ref, dcb_ref, dwa, dba, dwx, dbx, dlam]

        @pl.when(step == 0)
        def _():
            for o, v in zip(outs, acc):
                o[0] = v

        @pl.when(step > 0)
        def _():
            for o, v in zip(outs, acc):
                o[0] = o[0] + v

    base = C_XB // LANE
    rev = lambda b=0: pl.BlockSpec((tb, LANE), lambda j, i: (nt - 1 - i, b + j))
    nxt = pl.BlockSpec((8, LANE), lambda j, i: (jnp.minimum((nt - i) * (tb // 8), t // 8 - 1), j))
    prv = lambda b=0: pl.BlockSpec((8, LANE), lambda j, i: (jnp.maximum((nt - 1 - i) * (tb // 8) - 1, 0), b + j))
    params = [cw] + list(gate_p)
    grads = [jax.ShapeDtypeStruct(s, f32) for s in [cw.shape, (NCH, 1, LANE)] + [p.shape for p in gate_p]]
    return pl.pallas_call(
        body, name="rg_bwd", grid=(NCH, nt),
        in_specs=[rev(), rev(), nxt, rev(), prv(), rev(), rev(base), prv(base)] + [_slab_spec(p) for p in params],
        out_specs=[rev()] + [_slab_spec(p) for p in grads], out_shape=[jax.ShapeDtypeStruct((t, W_MIX), bf16)] + grads,
        scratch_shapes=[pltpu.VMEM((1, LANE), f32), pltpu.VMEM((8, LANE), f32)],
        compiler_params=_cparams(("parallel", "arbitrary")),
    )(dh, a, a, h, h, xc, z, z, *params)


HG_TB = 256
HALF = HG_SUB // 2


def _heads(v):
    return jnp.stack([v[:, LANE * h:LANE * (h + 1)] for h in range(HG_HEADS)])


def _unheads(v):
    return jnp.concatenate([v[h] for h in range(HG_HEADS)], axis=-1)


def _bmm(eq, a, b):
    return jnp.einsum(eq, a.astype(bf16), b.astype(bf16), preferred_element_type=f32)


def hg_chunk_fwd(qs, kk, gcum, z):
    t = qs.shape[0]
    tb = min(HG_TB, t)
    nc = tb // HG_SUB

    def body(q_ref, k_ref, g_ref, v_ref, o_ref, sall_ref, st_ref):
        @pl.when(pl.program_id(0) == 0)
        def _():
            st_ref[...] = jnp.zeros_like(st_ref)

        ri = lax.broadcasted_iota(jnp.int32, (1, HALF, 1), 1)

        def chunk(c, carry):
            rows = pl.ds(pl.multiple_of(c * HG_SUB, HG_SUB), HG_SUB)
            q, k, g, v = _heads(q_ref[rows, :]), _heads(k_ref[rows, :]), _heads(g_ref[rows, :]), _heads(v_ref[rows, :])
            st = st_ref[...]
            sall_ref[c] = st
            o = _bmm('htk,hvk->htv', q * jnp.exp(g), st)
            halves = [[q[:, :HALF], g[:, :HALF], o[:, :HALF]], [q[:, HALF:], g[:, HALF:], o[:, HALF:]]]
            for s in range(HG_SUB):
                grow, krow, vrow = g[:, s:s + 1, :], k[:, s:s + 1, :], v[:, s:s + 1, :]
                for h in range(s // HALF, 2):
                    qh, gh, oh = halves[h]
                    p = jnp.exp(jnp.minimum(gh - grow, 0.0))
                    if s // HALF == h:
                        p = jnp.where(ri >= s - h * HALF, p, 0.0)
                    halves[h][2] = oh + jnp.sum(qh * krow * p, axis=-1, keepdims=True) * vrow
            o = jnp.concatenate([halves[0][2], halves[1][2]], axis=1)
            gl = g[:, HG_SUB - 1:HG_SUB, :]
            st_ref[...] = st * jnp.exp(gl) + _bmm('htv,htk->hvk', v, k * jnp.exp(gl - g))
            o_ref[rows, :] = _unheads(o)
            return carry

        lax.fori_loop(0, nc, chunk, 0)

    spec = lambda base=0: pl.BlockSpec((tb, W_MIX), lambda i: (i, base))
    return pl.pallas_call(
        body, name="hg_chunk_fwd", grid=(t // tb,),
        in_specs=[spec(), spec(), spec(), spec(C_I // W_MIX)],
        out_specs=[spec(), pl.BlockSpec((nc, HG_HEADS, HG_DK, HG_DK), lambda i: (i, 0, 0, 0))],
        out_shape=[jax.ShapeDtypeStruct((t, W_MIX), f32),
                   jax.ShapeDtypeStruct((t // HG_SUB, HG_HEADS, HG_DK, HG_DK), f32)],
        scratch_shapes=[pltpu.VMEM((HG_HEADS, HG_DK, HG_DK), f32)],
        compiler_params=_cparams(("arbitrary",)),
    )(qs, kk, gcum, z)


def hg_chunk_bwd(do, qs, kk, gcum, z, sall):
    t = qs.shape[0]
    tb = min(HG_TB, t)
    nc = tb // HG_SUB
    nt = t // tb

    def body(do_ref, q_ref, k_ref, g_ref, v_ref, sall_ref, dq_ref, dk_ref, dg_ref, dv_ref, dst_ref):
        @pl.when(pl.program_id(0) == 0)
        def _():
            dst_ref[...] = jnp.zeros_like(dst_ref)

        ri = lax.broadcasted_iota(jnp.int32, (1, HALF, 1), 1)
        ri_chunk = lax.broadcasted_iota(jnp.int32, (1, HG_SUB, 1), 1)

        def chunk(cc, carry):
            c = nc - 1 - cc
            rows = pl.ds(pl.multiple_of(c * HG_SUB, HG_SUB), HG_SUB)
            q, k, g, v = _heads(q_ref[rows, :]), _heads(k_ref[rows, :]), _heads(g_ref[rows, :]), _heads(v_ref[rows, :])
            d_o = _heads(do_ref[rows, :])
            st = sall_ref[c]
            dsn = dst_ref[...]
            eg = jnp.exp(g)
            qe = q * eg
            gl = g[:, HG_SUB - 1:HG_SUB, :]
            egl = jnp.exp(gl)
            dec = jnp.exp(gl - g)
            kd = k * dec
            dqe = _bmm('htv,hvk->htk', d_o, st)
            dst_ref[...] = _bmm('htv,htk->hvk', d_o, qe) + dsn * egl
            dgl_dec = jnp.sum(dsn * st, axis=1, keepdims=True) * egl
            dv = _bmm('htk,hvk->htv', kd, dsn)
            dkd = _bmm('htv,hvk->htk', v, dsn)
            tq, tg, tdo = [q[:, :HALF], q[:, HALF:]], [g[:, :HALF], g[:, HALF:]], [d_o[:, :HALF], d_o[:, HALF:]]
            a1 = [jnp.zeros_like(tq[0]), jnp.zeros_like(tq[1])]
            a2 = [jnp.zeros_like(tq[0]), jnp.zeros_like(tq[1])]
            dvh = [dv[:, :HALF], dv[:, HALF:]]
            for s in range(HG_SUB):
                grow, krow, vrow = g[:, s:s + 1, :], k[:, s:s + 1, :], v[:, s:s + 1, :]
                sh, sr = s // HALF, s % HALF
                dv_s, a2_s = 0.0, 0.0
                for h in range(sh, 2):
                    p = jnp.exp(jnp.minimum(tg[h] - grow, 0.0))
                    if sh == h:
                        p = jnp.where(ri >= sr, p, 0.0)
                    col = jnp.sum(tq[h] * krow * p, axis=-1, keepdims=True)
                    t1 = jnp.sum(tdo[h] * vrow, axis=-1, keepdims=True) * p
                    a1[h] = a1[h] + t1 * krow
                    dv_s = dv_s + jnp.sum(col * tdo[h], axis=1, keepdims=True)
                    a2_s = a2_s + jnp.sum(t1 * tq[h], axis=1, keepdims=True)
                dvh[sh] = jnp.where(ri == sr, dvh[sh] + dv_s, dvh[sh])
                a2[sh] = jnp.where(ri == sr, a2_s, a2[sh])
            a1, a2 = jnp.concatenate(a1, axis=1), jnp.concatenate(a2, axis=1)
            dv = jnp.concatenate(dvh, axis=1)
            dgl = jnp.sum(dkd * kd, axis=1, keepdims=True) + dgl_dec
            dg = dqe * qe + q * a1 - k * a2 - dkd * kd
            dg = jnp.where(ri_chunk == HG_SUB - 1, dg + dgl, dg)
            dq_ref[rows, :] = _unheads(dqe * eg + a1)
            dk_ref[rows, :] = _unheads(dkd * dec + a2)
            dg_ref[rows, :] = _unheads(dg)
            dv_ref[rows, :] = _unheads(dv).astype(dv_ref.dtype)
            return carry

        lax.fori_loop(0, nc, chunk, 0)

    spec = lambda base=0: pl.BlockSpec((tb, W_MIX), lambda i: (nt - 1 - i, base))
    return pl.pallas_call(
        body, name="hg_chunk_bwd", grid=(nt,),
        in_specs=[spec(), spec(), spec(), spec(), spec(C_I // W_MIX),
                  pl.BlockSpec((nc, HG_HEADS, HG_DK, HG_DK), lambda i: (nt - 1 - i, 0, 0, 0))],
        out_specs=[spec(), spec(), spec(), spec()],
        out_shape=[jax.ShapeDtypeStruct((t, W_MIX), f32)] * 3 + [jax.ShapeDtypeStruct((t, W_MIX), bf16)],
        scratch_shapes=[pltpu.VMEM((HG_HEADS, HG_DK, HG_DK), f32)],
        compiler_params=_cparams(("arbitrary",)),
    )(do, qs, kk, gcum, z, sall)


def adamw(w, m, v, slots, *, name, tr):
    nl, r, c = w.shape
    tr = min(tr, r)
    flat = [a for per_layer in slots for a in per_layer]
    c1 = 1.0 / (1.0 - ADAM_B1 ** ADAM_STEP)
    c2 = 1.0 / (1.0 - ADAM_B2 ** ADAM_STEP)

    def body(*refs):
        w_ref, m_ref, v_ref = refs[:3]
        s_refs = list(refs[3:3 + len(flat)])
        g_ref, d_ref, mo_ref, vo_ref = refs[3 + len(flat):]
        for l in range(nl):
            parts = [s_refs.pop(0) for _ in slots[l]]
            g = None
            for p in parts:
                for s in range(p.shape[0]):
                    term = p[s].astype(f32)
                    g = term if g is None else g + term
            mn = ADAM_B1 * m_ref[l] + (1.0 - ADAM_B1) * g
            vn = ADAM_B2 * v_ref[l] + (1.0 - ADAM_B2) * (g * g)
            g_ref[l] = g
            mo_ref[l] = mn
            vo_ref[l] = vn
            d_ref[l] = -ADAM_LR * ((mn * c1) / (jnp.sqrt(vn * c2) + ADAM_EPS) + ADAM_WD * w_ref[l])

    full = pl.BlockSpec((nl, tr, c), lambda i: (0, i, 0))
    slot = [pl.BlockSpec((a.shape[0], tr, c), lambda i: (0, i, 0)) for a in flat]
    return pl.pallas_call(
        body, name=name, grid=(r // tr,), in_specs=[full] * 3 + slot, out_specs=[full] * 4,
        out_shape=[jax.ShapeDtypeStruct(w.shape, f32)] * 4, compiler_params=_cparams(("parallel",)),
    )(w, m, v, *flat)


def _slab(ref, axis, idx, n):
    return ref.at[tuple([slice(None)] * axis + [pl.ds(idx * n, n)])]


def all_gather(x, axis, *, name):
    n = x.shape[axis]
    out_shape = x.shape[:axis] + (N_DEV * n,) + x.shape[axis + 1:]

    def body(x_ref, out_ref, send_sems, recv_sems, local_sem):
        xx, yy, cc = lax.axis_index("x"), lax.axis_index("y"), lax.axis_index("c")
        me, sibling = (xx, yy, cc), (xx, yy, 1 - cc)
        chips = [(1 - xx, yy), (xx, 1 - yy), (1 - xx, 1 - yy)]

        def slab(px, py, pc):
            return _slab(out_ref, axis, 4 * px + 2 * py + pc, n)

        def copy(k, block, to, src=None):
            return pltpu.make_async_remote_copy(
                src_ref=slab(*block) if src is None else src, dst_ref=slab(*block),
                send_sem=send_sems.at[k], recv_sem=recv_sems.at[k], device_id=to, device_id_type=MESH)

        mine = pltpu.make_async_copy(x_ref, slab(*me), local_sem)
        mine.start()
        first = [copy(0, me, sibling, src=x_ref)]
        first += [copy(1 + j, me, (*chip, cc), src=x_ref) for j, chip in enumerate(chips)]
        for cp in first:
            cp.start()
        passed = [copy(4 + j, (*chip, cc), sibling) for j, chip in enumerate(chips)]
        for j, chip in enumerate(chips):
            copy(1 + j, (*chip, cc), me).wait_recv()
            passed[j].start()
        copy(0, sibling, me).wait_recv()
        for j, chip in enumerate(chips):
            copy(4 + j, (*chip, 1 - cc), me).wait_recv()
        for cp in first + passed:
            cp.wait_send()
        mine.wait()

    return pl.pallas_call(
        body, name=name, out_shape=jax.ShapeDtypeStruct(out_shape, x.dtype), in_specs=[ANY], out_specs=ANY,
        scratch_shapes=[pltpu.SemaphoreType.DMA((7,)), pltpu.SemaphoreType.DMA((7,)), pltpu.SemaphoreType.DMA],
    )(x)


N_CHIP = 4


HBM = pl.BlockSpec(memory_space=pltpu.HBM)
SEM = pl.BlockSpec(memory_space=pltpu.SEMAPHORE)
EFFECT = pltpu.SideEffectType.DATAFLOW_SIDE_EFFECTING
TOKEN = jax.ShapeDtypeStruct((8, LANE), f32)


def _in_hbm(a):
    return pltpu.with_memory_space_constraint(a, pltpu.HBM)


def pair_sums(g, axis, *, name):
    n = g.shape[axis] // N_DEV
    slab_shape = g.shape[:axis] + (n,) + g.shape[axis + 1:]
    cols = slab_shape[-1]
    rows = math.prod(slab_shape[:-1])
    col_slabs = axis == g.ndim - 1
    assert col_slabs or (axis == 0 and g.ndim == 2)

    def swap_body(g_ref, got_ref, send_sems, recv_sems):
        xx, yy, cc = lax.axis_index("x"), lax.axis_index("y"), lax.axis_index("c")
        copies = [pltpu.make_async_remote_copy(
            src_ref=_slab(g_ref, axis, 2 * q + 1 - cc, n), dst_ref=got_ref.at[q],
            send_sem=send_sems.at[q], recv_sem=recv_sems.at[q], device_id=(xx, yy, 1 - cc), device_id_type=MESH)
            for q in range(N_CHIP)]
        for cp in copies:
            cp.start()
        for cp in copies:
            cp.wait()

    got = pl.pallas_call(
        swap_body, name=name + "_swap", out_shape=jax.ShapeDtypeStruct((N_CHIP,) + slab_shape, g.dtype),
        in_specs=[ANY], out_specs=ANY, scratch_shapes=[pltpu.SemaphoreType.DMA((N_CHIP,))] * 2,
    )(g)

    tr = min(256, rows)

    def add_body(a0_ref, a1_ref, b_ref, pair_ref, own_ref):
        xx, yy, cc = lax.axis_index("x"), lax.axis_index("y"), lax.axis_index("c")
        mine = jnp.where(cc == 0, a0_ref[...], a1_ref[...])
        s = (mine.astype(f32) + b_ref[0].astype(f32)).astype(bf16)
        pair_ref[0] = s

        @pl.when(pl.program_id(1) == 2 * xx + yy)
        def _():
            own_ref[0] = s

    if col_slabs:
        a_spec = lambda c: pl.BlockSpec((tr, cols), lambda i, q: (i, 2 * q + c))
    else:
        a_spec = lambda c: pl.BlockSpec((tr, cols), lambda i, q: ((2 * q + c) * (n // tr) + i, 0))
    by_chip = pl.BlockSpec((1, tr, cols), lambda i, q: (q, i, 0))
    g2 = g.reshape(-1, g.shape[-1])
    pair, own = pl.pallas_call(
        add_body, name=name + "_add", grid=(rows // tr, N_CHIP), in_specs=[a_spec(0), a_spec(1), by_chip],
        out_specs=[by_chip, pl.BlockSpec((1, tr, cols), lambda i, q: (0, i, 0))],
        out_shape=[jax.ShapeDtypeStruct((N_CHIP, rows, cols), bf16), jax.ShapeDtypeStruct((1, rows, cols), bf16)],
        compiler_params=_cparams(("parallel", "arbitrary")),
    )(g2, g2, got.reshape(N_CHIP, rows, cols))
    return own, pair


def _send_copies(p_refs, land_refs, send_sems, recv_sems):
    xx, yy, cc = lax.axis_index("x"), lax.axis_index("y"), lax.axis_index("c")
    copies = []
    for t, (p, land) in enumerate(zip(p_refs, land_refs)):
        for k in range(1, N_CHIP):
            px = 1 - xx if k & 2 else xx
            py = 1 - yy if k & 1 else yy
            s = (N_CHIP - 1) * t + k - 1
            copies.append(pltpu.make_async_remote_copy(
                src_ref=p.at[2 * px + py], dst_ref=land.at[k - 1], send_sem=send_sems.at[s], recv_sem=recv_sems.at[s],
                device_id=(px, py, cc), device_id_type=MESH))
    return copies


def _direct_copies(axes, g_refs, land_refs, send_sems, recv_sems):
    xx, yy, cc = lax.axis_index("x"), lax.axis_index("y"), lax.axis_index("c")
    copies = []
    for t, (g_ref, land, axis) in enumerate(zip(g_refs, land_refs, axes)):
        n = g_ref.shape[axis] // N_DEV
        for k in range(1, N_DEV):
            px = 1 - xx if k & 4 else xx
            py = 1 - yy if k & 2 else yy
            pc = 1 - cc if k & 1 else cc
            s = (N_DEV - 1) * t + k - 1
            copies.append(pltpu.make_async_remote_copy(
                src_ref=_slab(g_ref, axis, 4 * px + 2 * py + pc, n), dst_ref=land.at[k - 1],
                send_sem=send_sems.at[s], recv_sem=recv_sems.at[s], device_id=(px, py, pc), device_id_type=MESH))
    return copies


def own_slab(g, axis, *, name):
    n = g.shape[axis] // N_DEV
    slab_shape = g.shape[:axis] + (n,) + g.shape[axis + 1:]
    cols, rows = slab_shape[-1], math.prod(slab_shape[:-1])
    tr = min(256, rows)

    def body(g_ref, o_ref):
        o_ref[0] = g_ref[...]

    def where(i):
        me = 4 * lax.axis_index("x") + 2 * lax.axis_index("y") + lax.axis_index("c")
        return (i, me) if axis == g.ndim - 1 else (me * (n // tr) + i, 0)

    return pl.pallas_call(
        body, name=name, grid=(rows // tr,), in_specs=[pl.BlockSpec((tr, cols), where)],
        out_specs=pl.BlockSpec((1, tr, cols), lambda i: (0, i, 0)),
        out_shape=jax.ShapeDtypeStruct((1, rows, cols), g.dtype), compiler_params=_cparams(("parallel",)),
    )(g.reshape(-1, g.shape[-1]))


def send_pairs_start(pairs, after, *, name, direct_axes=None):
    nt = len(pairs)
    if direct_axes is None:
        build = _send_copies
        lands = [lax.empty((N_CHIP - 1,) + p.shape[1:], p.dtype) for p in pairs]
    else:
        build = functools.partial(_direct_copies, direct_axes)
        lands = [lax.empty((N_DEV - 1,) + p.shape[:ax] + (p.shape[ax] // N_DEV,) + p.shape[ax + 1:], p.dtype)
                 for p, ax in zip(pairs, direct_axes)]

    def body(*refs):
        p_refs, land_refs = refs[:nt], refs[nt:2 * nt]
        send_sems, recv_sems = refs[2 * nt + 1], refs[2 * nt + 2]
        token = refs[-1]
        for cp in build(p_refs, land_refs, send_sems, recv_sems):
            cp.start()
        token[...] = jnp.zeros_like(token)

    nsem = sum(l.shape[0] for l in lands)
    outs = pl.pallas_call(
        body, name=name,
        out_shape=(pltpu.SemaphoreType.DMA((nsem,)), pltpu.SemaphoreType.DMA((nsem,)))
        + tuple(pltpu.HBM(a.shape, a.dtype) for a in list(pairs) + lands) + (TOKEN,),
        in_specs=[HBM] * (2 * nt) + [ANY], out_specs=(SEM, SEM) + (HBM,) * (2 * nt) + (VM,),
        input_output_aliases={i: 2 + i for i in range(2 * nt)},
        compiler_params=pltpu.CompilerParams(has_side_effects=EFFECT),
    )(*[_in_hbm(a) for a in list(pairs) + lands], after)
    return outs[:-1], outs[-1]


def send_pairs_wait(handles, after, *, name, direct_axes=None):
    send_sems, recv_sems = handles[0], handles[1]
    bufs = handles[2:]
    nt = len(bufs) // 2
    build = _send_copies if direct_axes is None else functools.partial(_direct_copies, direct_axes)

    def body(*refs):
        p_refs, land_refs = refs[:nt], refs[nt:2 * nt]
        send_sems, recv_sems = refs[2 * nt], refs[2 * nt + 1]
        for cp in build(p_refs, land_refs, send_sems, recv_sems):
            cp.wait_send()
            cp.wait_recv()

    outs = pl.pallas_call(
        body, name=name, out_shape=tuple(pltpu.HBM(a.shape, a.dtype) for a in bufs),
        in_specs=[HBM] * (2 * nt) + [SEM, SEM, ANY], out_specs=(HBM,) * (2 * nt),
        input_output_aliases={i: i for i in range(2 * nt)},
        compiler_params=pltpu.CompilerParams(has_side_effects=EFFECT),
    )(*bufs, send_sems, recv_sems, after)
    return outs[nt:]


def _gather_copies(x_refs, land_refs, axes, send_sems, recv_sems):
    xx, yy, cc = lax.axis_index("x"), lax.axis_index("y"), lax.axis_index("c")
    me = 4 * xx + 2 * yy + cc
    copies = []
    for t, (x_ref, land, axis) in enumerate(zip(x_refs, land_refs, axes)):
        n = x_ref.shape[axis]
        for k in range(1, N_DEV):
            px = 1 - xx if k & 4 else xx
            py = 1 - yy if k & 2 else yy
            pc = 1 - cc if k & 1 else cc
            s = (N_DEV - 1) * t + k - 1
            copies.append(pltpu.make_async_remote_copy(
                src_ref=x_ref, dst_ref=_slab(land, axis, me, n), send_sem=send_sems.at[s], recv_sem=recv_sems.at[s],
                device_id=(px, py, pc), device_id_type=MESH))
    return copies


def _place_own(x, axis, *, name):
    full = x.shape[:axis] + (N_DEV * x.shape[axis],) + x.shape[axis + 1:]
    lead = x.shape[0]
    tile = lead if axis == 0 else (256 if x.ndim == 2 and lead % 256 == 0 else 1 if x.ndim == 3 else lead)
    rest = (0,) * (x.ndim - 1)

    def body(land_ref, x_ref, o_ref):
        o_ref[...] = x_ref[...]

    def where(i):
        me = 4 * lax.axis_index("x") + 2 * lax.axis_index("y") + lax.axis_index("c")
        idx = [i] + list(rest)
        idx[axis] = me
        return tuple(idx)

    return pl.pallas_call(
        body, name=name, grid=(lead // tile,),
        in_specs=[ANY, pl.BlockSpec((tile,) + x.shape[1:], lambda i: (i,) + rest)],
        out_specs=pl.BlockSpec((tile,) + x.shape[1:], where),
        out_shape=jax.ShapeDtypeStruct(full, x.dtype), input_output_aliases={0: 0},
        compiler_params=_cparams(("arbitrary",)),
    )(lax.empty(full, x.dtype), x)


def gather_start(xs, axes, after, *, name):
    nt = len(xs)
    lands = [_place_own(x, axis, name=name + "_own") for x, axis in zip(xs, axes)]

    def body(*refs):
        x_refs, land_refs = refs[:nt], refs[nt:2 * nt]
        send_sems, recv_sems = refs[2 * nt + 1], refs[2 * nt + 2]
        token = refs[-1]
        for cp in _gather_copies(x_refs, land_refs, axes, send_sems, recv_sems):
            cp.start()
        token[...] = jnp.zeros_like(token)

    nsem = (N_DEV - 1) * nt
    outs = pl.pallas_call(
        body, name=name,
        out_shape=(pltpu.SemaphoreType.DMA((nsem,)), pltpu.SemaphoreType.DMA((nsem,)))
        + tuple(pltpu.HBM(a.shape, a.dtype) for a in list(xs) + lands) + (TOKEN,),
        in_specs=[HBM] * (2 * nt) + [ANY], out_specs=(SEM, SEM) + (HBM,) * (2 * nt) + (VM,),
        input_output_aliases={i: 2 + i for i in range(2 * nt)},
        compiler_params=pltpu.CompilerParams(has_side_effects=EFFECT),
    )(*[_in_hbm(a) for a in list(xs) + lands], after)
    return outs[:-1], outs[-1]


def gather_wait(handles, axes, after, *, name):
    send_sems, recv_sems = handles[0], handles[1]
    bufs = handles[2:]
    nt = len(bufs) // 2

    def body(*refs):
        x_refs, land_refs = refs[:nt], refs[nt:2 * nt]
        send_sems, recv_sems = refs[2 * nt], refs[2 * nt + 1]
        for cp in _gather_copies(x_refs, land_refs, axes, send_sems, recv_sems):
            cp.wait_send()
            cp.wait_recv()

    outs = pl.pallas_call(
        body, name=name, out_shape=tuple(pltpu.HBM(a.shape, a.dtype) for a in bufs),
        in_specs=[HBM] * (2 * nt) + [SEM, SEM, ANY], out_specs=(HBM,) * (2 * nt),
        input_output_aliases={i: i for i in range(2 * nt)},
        compiler_params=pltpu.CompilerParams(has_side_effects=EFFECT),
    )(*bufs, send_sems, recv_sems, after)
    return outs[nt:]


def _blockdiag(b, nb):
    j, _, r, c = b.shape
    eye = jnp.eye(nb, dtype=bool)[None, :, None, :, None]
    return jnp.where(eye, b[:, :, :, None, :], jnp.zeros((), b.dtype)).reshape(j, nb * r, nb * c)


def _diagblocks(d, nb):
    j, rr, cc = d.shape
    return jnp.einsum('jarac->jarc', d.reshape(j, nb, rr // nb, nb, cc // nb))


def _s5_b_dense(bbar):
    return _blockdiag(bbar.transpose(0, 2, 1).reshape(NCH, 8, S5_GROUP, S5_STATE), 8)


def _s5_b_undense(d):
    return _diagblocks(d, 8).reshape(S5_GROUPS, S5_GROUP, S5_STATE).transpose(0, 2, 1)


def _s5_c_dense(c):
    return _blockdiag(c.transpose(0, 2, 1).reshape(NCH, 8, S5_STATE, S5_GROUP), 8)


def _s5_c_undense(d):
    return _diagblocks(d, 8).reshape(S5_GROUPS, S5_STATE, S5_GROUP).transpose(0, 2, 1)


def _rg_dense(w):
    return _blockdiag(w.reshape(NCH, 2, RG_BLOCK, RG_BLOCK), 2)


def _rg_undense(d):
    return _diagblocks(d, 2).reshape(RG_BLOCKS, RG_BLOCK, RG_BLOCK)


def _chunks(v):
    return v.reshape(NCH, 1, LANE)


def _tri(tm):
    r = jnp.arange(tm)
    m = (r[:, None] >= r[None, :]) & (r[:, None] // HG_SUB == r[None, :] // HG_SUB)
    m = m.astype(f32)
    return m[None], m.T[None]


SMALL = ['norm_w', 's5_lambda_re', 's5_lambda_im', 's5_log_step', 's5_b_re', 's5_b_im', 's5_c_re', 's5_c_im',
         's5_d', 's5_b_glu', 'rg_conv_w', 'rg_conv_b', 'rg_w_a', 'rg_b_a', 'rg_w_x', 'rg_b_x', 'rg_lambda',
         'hg_lower_bounds', 'hg_norm_w', 'final_norm_w']
WEIGHTS = ['norm_w', 'w_in', 's5_lambda_re', 's5_lambda_im', 's5_log_step', 's5_b_re', 's5_b_im', 's5_c_re',
           's5_c_im', 's5_d', 's5_w_glu', 's5_b_glu', 'rg_conv_w', 'rg_conv_b', 'rg_w_a', 'rg_b_a', 'rg_w_x',
           'rg_b_x', 'rg_lambda', 'hg_lower_bounds', 'hg_norm_w', 'w_branch', 'w_out', 'final_norm_w']
PACK_ROWS = 512


def _pack(arrs):
    flat = jnp.concatenate([a.reshape(-1) for a in arrs])
    pad = (-flat.shape[0]) % (PACK_ROWS * LANE)
    return jnp.pad(flat, (0, pad)).reshape(1, -1, LANE)


def _unpack(buf, shapes):
    flat = buf.reshape(-1)
    out, off = [], 0
    for s in shapes:
        n = math.prod(s)
        out.append(flat[off:off + n].reshape(s))
        off += n
    return out


def _step(x, tgt, w, m, v):
    t = x.shape[0]
    tri, tri_t = _tri(min(LANE, t))
    me = 4 * lax.axis_index("x") + 2 * lax.axis_index("y") + lax.axis_index("c")

    big = ('w_in', 's5_w_glu', 'w_branch', 'w_out')
    big_axis = (1, 0, 2, 0)
    shards = lambda l: [w[k][l].astype(bf16) for k in big]
    win, wglu, wbr, wout = ([None] * DEPTH for _ in range(4))
    win[0] = all_gather(shards(0)[0], big_axis[0], name="ag_w_in")
    rest0_axis = big_axis[1:] + (1,)
    rest0, rest0_token = gather_start(shards(0)[1:] + [w['rg_conv_w'].reshape(DEPTH * RG_CONV, LANE)], rest0_axis,
                                      win[0], name="ag_start_0")

    lb_rows = [w['hg_lower_bounds'][l][None] for l in range(DEPTH)]
    lbs = whole(lb_prep_fn, lb_rows, [(1, W_MIX)] * DEPTH, name="lb_prep")

    saved = []
    for l in range(DEPTH):
        s = {}
        nw = w['norm_w'][l].reshape(1, 1, D_MODEL)
        (h,) = rowwise(ln_fn, [(x, 0, D_MODEL)], [nw], [], [(D_MODEL, bf16)], name="ln_fwd")
        token = None
        if l + 1 < DEPTH:
            handles, token = gather_start(shards(l + 1), big_axis, rest0_token if l == 0 else x, name=f"ag_start_{l + 1}")
        z = mm(h, win[l], after=token, name="mm_in", tm=1024)
        if l == 0:
            wglu[0], wbr[0], wout[0], conv_w = gather_wait(rest0, rest0_axis, z, name="ag_wait_0")
            conv_w = conv_w.reshape(DEPTH, RG_CONV, W_MIX)
        s5p = [w['s5_lambda_re'][l][..., None], w['s5_lambda_im'][l][..., None], w['s5_log_step'][l][:, None, None],
               w['s5_b_re'][l], w['s5_b_im'][l]]
        gp = (S5_GROUPS, S5_STATE)
        abar_re, abar_im, bbar_re, bbar_im = whole(
            s5_prep_fn, s5p, [gp + (1,), gp + (1,), gp + (S5_GROUP,), gp + (S5_GROUP,)], name="s5_prep")
        a_re, a_im = abar_re.reshape(NCH, 1, S5_SC), abar_im.reshape(NCH, 1, S5_SC)
        bd_re, bd_im = _s5_b_dense(bbar_re), _s5_b_dense(bbar_im)
        cd_re, cd_im = _s5_c_dense(w['s5_c_re'][l]), _s5_c_dense(w['s5_c_im'][l])
        yssm, xre, xim = s5_scan_fwd(z, bd_re, bd_im, cd_re, cd_im, a_re, a_im)
        s5post_p = [w['s5_d'][l].reshape(1, 1, W_MIX), wglu[l].astype(f32)[None], w['s5_b_glu'][l].reshape(1, 1, W_MIX)]
        s5post_rows = [(yssm, 0, W_MIX), (z, C_UA, W_MIX), (z, C_GA, W_MIX)]
        (ya,) = rowwise(s5_post_fn, s5post_rows, s5post_p, [], [(W_MIX, bf16)], name="s5_post_fwd")
        cw, cb = conv_w[l].reshape(RG_CONV, NCH, LANE).transpose(1, 0, 2), _chunks(w['rg_conv_b'][l])
        rg_p = [_rg_dense(w['rg_w_a'][l]), _chunks(w['rg_b_a'][l]), _rg_dense(w['rg_w_x'][l]),
                _chunks(w['rg_b_x'][l]), _chunks(w['rg_lambda'][l])]
        xc, ra, hb = rg_fwd(z, cw, cb, rg_p)
        hg_rows = [(z, C_Q, W_MIX), (z, C_F, W_MIX)]
        hg_p = [_chunks(lbs[l].reshape(W_MIX))]
        qs, kk, gcum = rowwise(hg_pre_fn, hg_rows, hg_p, [tri, tri_t], [(W_MIX, f32)] * 3, name="hg_pre_fwd", ncol=NCH, tm=TM_CHUNK)
        oc, sall = hg_chunk_fwd(qs, kk, gcum, z)
        bp_rows = [(hb, 0, W_MIX), (z, C_GB, W_MIX), (oc, 0, W_MIX), (z, C_GC, W_MIX)]
        bp_p = [_chunks(w['hg_norm_w'][l])]
        yb, yc = rowwise(branch_prep_fn, bp_rows, bp_p, [], [(W_MIX, bf16)] * 2, name="branch_prep_fwd", ncol=NCH, tm=TM_CHUNK)
        ys = [ya, yb, yc]
        br = [mm(ys[n], wbr[l][n], name="mm_branch", out_dtype=bf16) for n in range(N_BRANCH)]
        mg_rows = [(br[n], 0, D_MODEL) for n in range(N_BRANCH)] + [(z, C_GATE + n * D_MODEL, D_MODEL) for n in range(N_BRANCH)]
        (merged,) = rowwise(merge_fn, mg_rows, [], [], [(D_MODEL, bf16)], name="merge_fwd", ncol=2)
        x_new = mm(merged, wout[l], add=x, name="mm_out")
        s.update(x=x, h=h, z=z, s5p=s5p, s5=(bd_re, bd_im, cd_re, cd_im, a_re, a_im), xre=xre, xim=xim,
                 s5post_rows=s5post_rows, s5post_p=s5post_p, cw=cw, xc=xc, rg_p=rg_p, ra=ra, hb=hb,
                 hg_rows=hg_rows, hg_p=hg_p, qs=qs, kk=kk, gcum=gcum, sall=sall, bp_rows=bp_rows, bp_p=bp_p,
                 ys=ys, mg_rows=mg_rows, merged=merged, nw=nw)
        saved.append(s)
        x = x_new
        if l + 1 < DEPTH:
            win[l + 1], wglu[l + 1], wbr[l + 1], wout[l + 1] = gather_wait(handles, big_axis, x, name=f"ag_wait_{l + 1}")

    fnw = w['final_norm_w'].reshape(1, 1, D_MODEL)
    ones = jnp.ones((t, 1), f32)
    dx, d_fnw, loss_sum = rowwise_vjp(loss_fn, [(x, 0, D_MODEL), (tgt, 0, D_MODEL)], [fnw], [], [(ones, 0, 1)],
                                      [(0, f32)], name="loss_head", sum_primal=0)
    loss = lax.psum(loss_sum.reshape(()), ("x", "y", "c"))

    small_g = {k: [None] * DEPTH for k in SMALL if k != 'final_norm_w'}
    own_sums, in_flight = [None] * DEPTH, [None] * DEPTH
    d_lbs = [None] * DEPTH
    token = None
    for l in reversed(range(DEPTH)):
        s = saved[l]
        z = s['z']
        dxb = dx.astype(bf16)
        d_merged = mm(dxb, wout[l], bt=True, after=token, name="mm_out_dx", out_dtype=bf16)
        d_wout = mm(s['merged'], dxb, at=True, name="mm_out_dw", out_dtype=bf16)
        mg = rowwise_vjp(merge_fn, s['mg_rows'], [], [], [(d_merged, 0, D_MODEL)],
                         [(n, bf16) for n in range(2 * N_BRANCH)], name="merge_bwd", ncol=2)
        d_br, d_gl = mg[:N_BRANCH], mg[N_BRANCH:]
        d_ys = [mm(d_br[n], wbr[l][n], bt=True, name="mm_branch_dx", out_dtype=bf16) for n in range(N_BRANCH)]
        d_wbr = jnp.stack([mm(s['ys'][n], d_br[n], at=True, name="mm_branch_dw", out_dtype=bf16) for n in range(N_BRANCH)])
        d_hb, d_gb, d_oc, d_gc, d_hnw = rowwise_vjp(
            branch_prep_fn, s['bp_rows'], s['bp_p'], [], [(d_ys[1], 0, W_MIX), (d_ys[2], 0, W_MIX)],
            [(0, f32), (1, bf16), (2, f32), (3, bf16)], name="branch_prep_bwd", ncol=NCH, tm=TM_CHUNK)
        small_g['hg_norm_w'][l] = d_hnw.reshape(W_MIX)
        d_qs, d_kk, d_gcum, d_i = hg_chunk_bwd(d_oc, s['qs'], s['kk'], s['gcum'], z, s['sall'])
        d_q, d_f, d_lb = rowwise_vjp(
            hg_pre_fn, s['hg_rows'], s['hg_p'], [tri, tri_t], [(d_qs, 0, W_MIX), (d_kk, 0, W_MIX), (d_gcum, 0, W_MIX)],
            [(0, bf16), (1, bf16)], name="hg_pre_bwd", ncol=NCH, tm=TM_CHUNK)
        d_lbs[l] = d_lb.reshape(1, W_MIX)
        d_xb, d_cw, d_cb, d_wa, d_ba, d_wx, d_bx, d_lam = rg_bwd(d_hb, z, s['xc'], s['ra'], s['hb'], s['cw'], s['rg_p'])
        small_g['rg_w_a'][l], small_g['rg_w_x'][l] = _rg_undense(d_wa), _rg_undense(d_wx)
        small_g['rg_b_a'][l], small_g['rg_b_x'][l] = d_ba.reshape(W_MIX), d_bx.reshape(W_MIX)
        small_g['rg_lambda'][l] = d_lam.reshape(W_MIX)
        small_g['rg_conv_w'][l] = d_cw.transpose(1, 0, 2).reshape(RG_CONV, W_MIX)
        small_g['rg_conv_b'][l] = d_cb.reshape(W_MIX)
        d_yssm, d_u1, d_ga, d_d, d_wglu, d_bglu = rowwise_vjp(
            s5_post_fn, s['s5post_rows'], s['s5post_p'], [], [(d_ys[0], 0, W_MIX)],
            [(0, bf16), (1, bf16), (2, bf16)], name="s5_post_bwd")
        small_g['s5_d'][l], small_g['s5_b_glu'][l] = d_d.reshape(W_MIX), d_bglu.reshape(W_MIX)
        d_ua, d_bdre, d_bdim, d_cdre, d_cdim, d_are, d_aim = s5_scan_bwd(d_yssm, d_u1, z, s['xre'], s['xim'], *s['s5'])
        small_g['s5_c_re'][l], small_g['s5_c_im'][l] = _s5_c_undense(d_cdre), _s5_c_undense(d_cdim)
        gp = (S5_GROUPS, S5_STATE, 1)
        s5g = whole_vjp(s5_prep_fn, s['s5p'],
                        [d_are.reshape(gp), d_aim.reshape(gp), _s5_b_undense(d_bdre), _s5_b_undense(d_bdim)],
                        name="s5_prep_bwd")
        small_g['s5_lambda_re'][l] = s5g[0].reshape(S5_GROUPS, S5_STATE)
        small_g['s5_lambda_im'][l] = s5g[1].reshape(S5_GROUPS, S5_STATE)
        small_g['s5_log_step'][l] = s5g[2].reshape(S5_GROUPS)
        small_g['s5_b_re'][l], small_g['s5_b_im'][l] = s5g[3], s5g[4]
        dz = jnp.concatenate([d_ua, d_ga, d_xb, d_gb, d_q, d_f, d_i, d_gc] + list(d_gl), axis=1)
        d_win = mm(s['h'], dz, at=True, name="mm_in_dw", out_dtype=bf16)
        grads = (d_win, d_wglu[0].astype(bf16), d_wbr, d_wout)
        if l == 0:
            sums = [pair_sums(g, ax, name="rs_" + k) for g, ax, k in zip(grads, big_axis, big)]
            own_sums[l] = [own for own, _ in sums]
            in_flight[l], token = send_pairs_start([pair for _, pair in sums], s5g[0], name=f"rs_start_{l}")
        else:
            own_sums[l] = [own_slab(g, ax, name="rs_own_" + k) for g, ax, k in zip(grads, big_axis, big)]
            in_flight[l], token = send_pairs_start(grads, s5g[0], name=f"rs_start_{l}", direct_axes=big_axis)
        d_h = mm(dz, win[l], bt=True, after=token, name="mm_in_dx", tm=1024, tk=2048)
        dx, d_nw = rowwise_vjp(ln_res_fn, [(s['x'], 0, D_MODEL)], [s['nw']], [], [(d_h, 0, D_MODEL), (dx, 0, D_MODEL)],
                               [(0, f32)], name="ln_bwd")
        small_g['norm_w'][l] = d_nw.reshape(D_MODEL)
    d_lb_raw = whole_vjp(lb_prep_fn, lb_rows, d_lbs, name="lb_prep_bwd")
    small_g['hg_lower_bounds'] = [r.reshape(W_MIX) for r in d_lb_raw]

    per_layer = [k for k in SMALL if k != 'final_norm_w']
    shapes = [(DEPTH,) + small_g[k][0].shape for k in per_layer] + [(D_MODEL,)]
    pieces = [small_g[k][l] for k in per_layer for l in range(DEPTH)] + [d_fnw]
    small_in_flight, small_token = gather_start([_pack(pieces)[0].astype(bf16)], (0,), dx, name="ag_small_start")
    res = {}

    arrived = [send_pairs_wait(in_flight[l], small_token, name=f"rs_wait_{l}", direct_axes=None if l == 0 else big_axis)
               for l in range(DEPTH)]
    for i, (k, tr) in enumerate((('w_in', 32), ('s5_w_glu', 32), ('w_branch', 128), ('w_out', 32))):
        shp = w[k].shape
        r3 = lambda a: a.reshape(DEPTH, -1, shp[-1])
        slots = [[own_sums[l][i], arrived[l][i].reshape(arrived[l][i].shape[0], -1, shp[-1])] for l in range(DEPTH)]
        outs = adamw(r3(w[k]), r3(m[k]), r3(v[k]), slots, name="adamw_" + k, tr=tr)
        for kind, buf in zip(('grad', 'delta', 'new_m', 'new_v'), outs):
            res[kind + '_' + k] = buf.reshape(shp)

    (g_all,) = gather_wait(small_in_flight, (0,), outs[0], name="ag_small_wait")
    g_all = g_all.reshape(N_DEV, -1, LANE)

    def local(d, k):
        return jnp.zeros(shapes[SMALL.index(k)], f32) if k == 'rg_conv_w' else d[k]
    packed = [_pack([local(d, k) for k in SMALL]) for d in (w, m, v)]
    outs = adamw(*packed, [[g_all]], name="adamw_small", tr=512)
    for kind, buf in zip(('grad', 'delta', 'new_m', 'new_v'), outs):
        for k, a in zip(SMALL, _unpack(buf, shapes)):
            res[kind + '_' + k] = a
    g_cw = lax.dynamic_slice_in_dim(res['grad_rg_conv_w'], me * LANE, LANE, axis=2)
    cw3 = lambda a: a.reshape(1, DEPTH * RG_CONV, LANE)
    outs = adamw(cw3(w['rg_conv_w']), cw3(m['rg_conv_w']), cw3(v['rg_conv_w']), [[cw3(g_cw)]], name="adamw_conv_w", tr=16)
    for kind, buf in zip(('grad', 'delta', 'new_m', 'new_v'), outs):
        res[kind + '_rg_conv_w'] = buf.reshape(DEPTH, RG_CONV, LANE)

    return (loss, dx[None]) + tuple(res[kind + '_' + k] for kind in ('grad', 'delta', 'new_m', 'new_v') for k in WEIGHTS)


def kernel(x, norm_w, w_in, s5_lambda_re, s5_lambda_im, s5_log_step, s5_b_re, s5_b_im, s5_c_re, s5_c_im, s5_d, s5_w_glu, s5_b_glu, rg_conv_w, rg_conv_b, rg_w_a, rg_b_a, rg_w_x, rg_b_x, rg_lambda, hg_lower_bounds, hg_norm_w, w_branch, w_out, final_norm_w, loss_target, m_norm_w, m_w_in, m_s5_lambda_re, m_s5_lambda_im, m_s5_log_step, m_s5_b_re, m_s5_b_im, m_s5_c_re, m_s5_c_im, m_s5_d, m_s5_w_glu, m_s5_b_glu, m_rg_conv_w, m_rg_conv_b, m_rg_w_a, m_rg_b_a, m_rg_w_x, m_rg_b_x, m_rg_lambda, m_hg_lower_bounds, m_hg_norm_w, m_w_branch, m_w_out, m_final_norm_w, v_norm_w, v_w_in, v_s5_lambda_re, v_s5_lambda_im, v_s5_log_step, v_s5_b_re, v_s5_b_im, v_s5_c_re, v_s5_c_im, v_s5_d, v_s5_w_glu, v_s5_b_glu, v_rg_conv_w, v_rg_conv_b, v_rg_w_a, v_rg_b_a, v_rg_w_x, v_rg_b_x, v_rg_lambda, v_hg_lower_bounds, v_hg_norm_w, v_w_branch, v_w_out, v_final_norm_w):
    w = dict(zip(WEIGHTS, (norm_w, w_in, s5_lambda_re, s5_lambda_im, s5_log_step, s5_b_re, s5_b_im, s5_c_re, s5_c_im, s5_d, s5_w_glu, s5_b_glu, rg_conv_w, rg_conv_b, rg_w_a, rg_b_a, rg_w_x, rg_b_x, rg_lambda, hg_lower_bounds, hg_norm_w, w_branch, w_out, final_norm_w)))
    m = dict(zip(WEIGHTS, (m_norm_w, m_w_in, m_s5_lambda_re, m_s5_lambda_im, m_s5_log_step, m_s5_b_re, m_s5_b_im, m_s5_c_re, m_s5_c_im, m_s5_d, m_s5_w_glu, m_s5_b_glu, m_rg_conv_w, m_rg_conv_b, m_rg_w_a, m_rg_b_a, m_rg_w_x, m_rg_b_x, m_rg_lambda, m_hg_lower_bounds, m_hg_norm_w, m_w_branch, m_w_out, m_final_norm_w)))
    v = dict(zip(WEIGHTS, (v_norm_w, v_w_in, v_s5_lambda_re, v_s5_lambda_im, v_s5_log_step, v_s5_b_re, v_s5_b_im, v_s5_c_re, v_s5_c_im, v_s5_d, v_s5_w_glu, v_s5_b_glu, v_rg_conv_w, v_rg_conv_b, v_rg_w_a, v_rg_b_a, v_rg_w_x, v_rg_b_x, v_rg_lambda, v_hg_lower_bounds, v_hg_norm_w, v_w_branch, v_w_out, v_final_norm_w)))
    return _step(x[0], loss_target[0], w, m, v)
```

```python
import functools
import math

import jax
import jax.numpy as jnp
from jax import lax
from jax.experimental import pallas as pl
from jax.experimental.pallas import tpu as pltpu

f32 = jnp.float32
bf16 = jnp.bfloat16

D_MODEL = 2048
W_MIX = 1024
DEPTH = 4
N_BRANCH = 3
N_IN = 8 * W_MIX + N_BRANCH * D_MODEL
S5_GROUPS, S5_STATE, S5_GROUP = 64, 64, 16
RG_BLOCKS, RG_BLOCK, RG_CONV, RG_C = 16, 64, 4, 8.0
HG_HEADS, HG_DK = 8, 128
HG_SUB = 16
EPS = 1e-6
ADAM_LR, ADAM_B1, ADAM_B2, ADAM_EPS, ADAM_WD, ADAM_STEP = 0.001, 0.9, 0.999, 1e-08, 0.01, 10

N_DEV = 8
LANE = 128
NCH = W_MIX // LANE
TM_CHUNK = 1024
VMEM_LIMIT = 56 * 1024 * 1024
MESH = pl.DeviceIdType.MESH
ANY = pl.BlockSpec(memory_space=pl.ANY)
HIGHEST = lax.Precision.HIGHEST

C_UA, C_GA, C_XB, C_GB, C_Q, C_F, C_I, C_GC, C_GATE = (W_MIX * k for k in range(9))


def _cparams(sem=None):
    return pltpu.CompilerParams(dimension_semantics=sem, vmem_limit_bytes=VMEM_LIMIT)


@jax.custom_vjp
def bdot(a, w):
    return jnp.dot(a.astype(bf16), w.astype(bf16), preferred_element_type=f32)


def _bdot_fwd(a, w):
    return bdot(a, w), (a, w)


def _bdot_bwd(res, g):
    a, w = res
    gb = g.astype(bf16)
    da = lax.dot_general(gb, w.astype(bf16), (((1,), (1,)), ((), ())), preferred_element_type=f32)
    dw = lax.dot_general(a.astype(bf16), gb, (((0,), (0,)), ((), ())), preferred_element_type=f32)
    return da, dw


bdot.defvjp(_bdot_fwd, _bdot_bwd)


def _blockmm(c, a):
    n = c.shape[0]
    return jnp.concatenate([jnp.dot(c, a[i:i + n], preferred_element_type=f32, precision=HIGHEST)
                            for i in range(0, a.shape[0], n)], axis=0)


@jax.custom_vjp
def cdot(c, ct, a):
    return _blockmm(c, a)


def _cdot_fwd(c, ct, a):
    return cdot(c, ct, a), (c, ct)


def _cdot_bwd(res, g):
    c, ct = res
    return jnp.zeros_like(c), jnp.zeros_like(ct), _blockmm(ct, g)


cdot.defvjp(_cdot_fwd, _cdot_bwd)


def mm(a, b, *, name, out_dtype=f32, add=None, after=None, at=False, bt=False, tm=1024, tn=1024, tk=4096):
    k, m = a.shape if at else a.shape[::-1]
    n = b.shape[0] if bt else b.shape[1]
    tm, tn, tk = min(tm, m), min(tn, n), min(tk, k)
    assert m % tm == 0 and n % tn == 0 and k % tk == 0
    nk = k // tk
    dims = (((0 if at else 1,), (1 if bt else 0,)), ((), ()))

    def body(*refs):
        a_ref, b_ref = refs[:2]
        r_ref = refs[2] if add is not None else None
        o_ref = refs[-1] if nk == 1 else refs[-2]
        part = lax.dot_general(a_ref[...], b_ref[...], dims, preferred_element_type=f32)
        if nk == 1:
            if add is not None:
                part = part + r_ref[...]
            o_ref[...] = part.astype(out_dtype)
            return
        acc_ref = refs[-1]
        kk = pl.program_id(2)

        @pl.when(kk == 0)
        def _():
            acc_ref[...] = part

        @pl.when(kk > 0)
        def _():
            acc_ref[...] = acc_ref[...] + part

        @pl.when(kk == nk - 1)
        def _():
            acc = acc_ref[...]
            if add is not None:
                acc = acc + r_ref[...]
            o_ref[...] = acc.astype(out_dtype)

    b_spec = pl.BlockSpec((tn, tk), lambda i, j, q: (j, q)) if bt else pl.BlockSpec((tk, tn), lambda i, j, q: (q, j))
    a_spec = pl.BlockSpec((tk, tm), lambda i, j, q: (q, i)) if at else pl.BlockSpec((tm, tk), lambda i, j, q: (i, q))
    in_specs = [a_spec, b_spec]
    args = [a, b]
    if add is not None:
        in_specs.append(pl.BlockSpec((tm, tn), lambda i, j, q: (i, j)))
        args.append(add)
    if after is not None:
        in_specs.append(pl.BlockSpec(after.shape, lambda i, j, q: (0, 0)))
        args.append(after)
    return pl.pallas_call(
        body, name=name, grid=(m // tm, n // tn, nk), in_specs=in_specs,
        out_specs=pl.BlockSpec((tm, tn), lambda i, j, q: (i, j)),
        out_shape=jax.ShapeDtypeStruct((m, n), out_dtype),
        scratch_shapes=[] if nk == 1 else [pltpu.VMEM((tm, tn), f32)],
        compiler_params=_cparams(("parallel", "parallel", "arbitrary")),
    )(*args)


def _row_spec(tm, wc, col_off):
    base = col_off // wc
    assert col_off % wc == 0
    return pl.BlockSpec((tm, wc), lambda j, i: (i, base + j))


def _slab_spec(arr):
    r, c = arr.shape[1:]
    if arr.shape[0] == 1:
        return pl.BlockSpec((1, r, c), lambda j, i: (0, 0, 0))
    return pl.BlockSpec((1, r, c), lambda j, i: (j, 0, 0))


def rowwise(fn, rows, params, consts, outs, *, name, tm=256, ncol=1, rowid=False):
    t = rows[0][0].shape[0]
    tm = min(tm, t)
    nr, npar, nc, no = len(rows), len(params), len(consts), len(outs)

    def body(*refs):
        r = [refs[k][...].astype(f32) for k in range(nr)]
        p = [refs[nr + k][0] for k in range(npar + nc)]
        extra = ()
        if rowid:
            extra = (pl.program_id(1) * tm + lax.broadcasted_iota(jnp.int32, (tm, 1), 0),)
        res = fn(*extra, *r, *p)
        for k in range(no):
            refs[nr + npar + nc + k][...] = res[k].astype(outs[k][1])

    in_specs = [_row_spec(tm, w // ncol, off) for (_, off, w) in rows]
    in_specs += [_slab_spec(a) for a in list(params) + list(consts)]
    out_specs = [pl.BlockSpec((tm, w // ncol), lambda j, i: (i, j)) for (w, _) in outs]
    out_shape = [jax.ShapeDtypeStruct((t, w), dt) for (w, dt) in outs]
    return pl.pallas_call(
        body, name=name, grid=(ncol, t // tm), in_specs=in_specs, out_specs=out_specs, out_shape=out_shape,
        compiler_params=_cparams(("parallel", "parallel")),
    )(*[r[0] for r in rows], *params, *consts)


def rowwise_vjp(fn, rows, params, consts, cts, d_rows, *, name, tm=256, ncol=1, rowid=False, sum_primal=None):
    t = rows[0][0].shape[0]
    tm = min(tm, t)
    nr, npar, nc, nct, ndr = len(rows), len(params), len(consts), len(cts), len(d_rows)

    def body(*refs):
        i = pl.program_id(1)
        r = [refs[k][...].astype(f32) for k in range(nr)]
        p = [refs[nr + k][0] for k in range(npar)]
        c = [refs[nr + npar + k][0] for k in range(nc)]
        g = [refs[nr + npar + nc + k][...].astype(f32) for k in range(nct)]
        orefs = refs[nr + npar + nc + nct:]
        extra = ()
        if rowid:
            extra = (i * tm + lax.broadcasted_iota(jnp.int32, (tm, 1), 0),)
        res, vjp = jax.vjp(lambda *v: fn(*extra, *v, *c), *r, *p)
        grads = vjp(tuple(g))
        for k, (idx, dt) in enumerate(d_rows):
            orefs[k][...] = grads[idx].astype(dt)
        acc = [grads[nr + k] for k in range(npar)]
        if sum_primal is not None:
            acc.append(jnp.sum(res[sum_primal], axis=0, keepdims=True))

        @pl.when(i == 0)
        def _():
            for k, a in enumerate(acc):
                orefs[ndr + k][0] = a

        @pl.when(i > 0)
        def _():
            for k, a in enumerate(acc):
                orefs[ndr + k][0] = orefs[ndr + k][0] + a

    in_specs = [_row_spec(tm, w // ncol, off) for (_, off, w) in rows]
    in_specs += [_slab_spec(a) for a in list(params) + list(consts)]
    in_specs += [_row_spec(tm, w // ncol, off) for (_, off, w) in cts]
    out_specs = [pl.BlockSpec((tm, rows[idx][2] // ncol), lambda j, i: (i, j)) for (idx, _) in d_rows]
    out_shape = [jax.ShapeDtypeStruct((t, rows[idx][2]), dt) for (idx, dt) in d_rows]
    for a in params:
        out_specs.append(pl.BlockSpec((1,) + a.shape[1:], lambda j, i: (j, 0, 0)))
        out_shape.append(jax.ShapeDtypeStruct(a.shape, f32))
    if sum_primal is not None:
        w = cts[sum_primal][2]
        out_specs.append(pl.BlockSpec((1, 1, w // ncol), lambda j, i: (j, 0, 0)))
        out_shape.append(jax.ShapeDtypeStruct((ncol, 1, w // ncol), f32))
    return pl.pallas_call(
        body, name=name, grid=(ncol, t // tm), in_specs=in_specs, out_specs=out_specs, out_shape=out_shape,
        compiler_params=_cparams(("parallel", "arbitrary")),
    )(*[r[0] for r in rows], *params, *consts, *[c[0] for c in cts])


VM = pl.BlockSpec(memory_space=pltpu.VMEM)


def whole(fn, ins, outs, *, name):
    def body(*refs):
        res = fn(*[r[...] for r in refs[:len(ins)]])
        for k, o in enumerate(refs[len(ins):]):
            o[...] = res[k]
    return pl.pallas_call(body, name=name, in_specs=[VM] * len(ins), out_specs=[VM] * len(outs),
                          out_shape=[jax.ShapeDtypeStruct(s, f32) for s in outs],
                          compiler_params=_cparams())(*ins)


def whole_vjp(fn, ins, cts, *, name):
    n = len(ins)

    def body(*refs):
        _, vjp = jax.vjp(fn, *[r[...] for r in refs[:n]])
        grads = vjp(tuple(r[...] for r in refs[n:n + len(cts)]))
        for k, o in enumerate(refs[n + len(cts):]):
            o[...] = grads[k]
    return pl.pallas_call(body, name=name, in_specs=[VM] * (n + len(cts)), out_specs=[VM] * n,
                          out_shape=[jax.ShapeDtypeStruct(a.shape, f32) for a in ins],
                          compiler_params=_cparams())(*ins, *cts)


def ln_fn(x, w):
    return (x * lax.rsqrt(jnp.mean(x * x, axis=-1, keepdims=True) + EPS) * w,)


def ln_res_fn(x, w):
    return ln_fn(x, w)[0], x


def loss_fn(x, tgt, w):
    y = ln_fn(x, w)[0]
    return (0.5 * jnp.mean(jnp.square(y - tgt), axis=-1, keepdims=True),)


def s5_prep_fn(lam_re, lam_im, log_step, b_re, b_im):
    step = jnp.exp(log_step)
    mag = jnp.exp(lam_re * step)
    ang = lam_im * step
    abar_re = mag * jnp.cos(ang)
    abar_im = mag * jnp.sin(ang)
    num_re = abar_re - 1.0
    num_im = abar_im
    den = lam_re * lam_re + lam_im * lam_im
    coef_re = (num_re * lam_re + num_im * lam_im) / den
    coef_im = (num_im * lam_re - num_re * lam_im) / den
    bbar_re = coef_re * b_re - coef_im * b_im
    bbar_im = coef_re * b_im + coef_im * b_re
    return abar_re, abar_im, bbar_re, bbar_im


def lb_prep_fn(r0, r1, r2, r3):
    m = jnp.maximum(jnp.maximum(r0, r1), jnp.maximum(r2, r3))
    e0, e1, e2, e3 = jnp.exp(r0 - m), jnp.exp(r1 - m), jnp.exp(r2 - m), jnp.exp(r3 - m)
    s = e0 + e1 + e2 + e3
    p0, p1, p2, p3 = e0 / s, e1 / s, e2 / s, e3 / s
    c1 = p0 + p1
    c2 = c1 + p2
    c3 = c2 + p3
    return p0 - p0, c1 - p0, c2 - p0, c3 - p0


def s5_post_fn(yssm, u, ga, d, wglu, bglu):
    y = jax.nn.gelu(yssm + d * u)
    y = y * jax.nn.sigmoid(bdot(y, wglu) + bglu)
    return (y * jax.nn.silu(ga),)


def rg_gate_fn(tglob, xc, wa, ba, wx, bx, lam):
    r = jax.nn.sigmoid(bdot(xc, wa) + ba)
    i = jax.nn.sigmoid(bdot(xc, wx) + bx)
    log_a = -RG_C * r * jax.nn.softplus(-lam)
    a = jnp.exp(log_a)
    mult = jnp.sqrt(-jnp.tanh(log_a) * (a * a + 1.0))
    mult = jnp.where(tglob == 0, 1.0, mult)
    return a, mult * (i * xc)


def hg_pre_fn(q, fl, lb, tri, tri_t):
    f = lb + (1.0 - lb) * jax.nn.sigmoid(fl)
    return jax.nn.silu(q), 1.0 - f, cdot(tri, tri_t, jnp.log(f))


def branch_prep_fn(hb, gb, oc, gc, nw):
    yb = hb * jax.nn.silu(gb)
    on = oc * lax.rsqrt(jnp.mean(oc * oc, axis=-1, keepdims=True) + EPS) * nw
    return yb, on * jax.nn.silu(gc)


def merge_fn(b0, b1, b2, g0, g1, g2):
    return (jax.nn.sigmoid(g0) * b0 + jax.nn.sigmoid(g1) * b1 + jax.nn.sigmoid(g2) * b2,)


S5_TB = 512
S5_SC = 512


SEG = 8


def _shift(v, k, pos, period, reverse, fill):
    if reverse:
        return jnp.where(pos < period - k, pltpu.roll(v, v.shape[0] - k, 0), fill)
    return jnp.where(pos >= k, pltpu.roll(v, k, 0), fill)


def _cmul(ar, ai, br, bi):
    return ar * br - ai * bi, ar * bi + ai * br


def _edge_rows(v, first):
    r0 = 0 if first else SEG - 1
    return jnp.concatenate([v[r:r + 1, :] for r in range(r0, v.shape[0], SEG)], axis=0)


def _spread(s):
    return jnp.concatenate([jnp.broadcast_to(s[g:g + 1, :], (SEG, s.shape[1])) for g in range(s.shape[0])], axis=0)


def lti_scan(xr, xi, ar, ai, cr, ci, reverse=False):
    n = xr.shape[0]
    g = n // SEG
    sub = lax.broadcasted_iota(jnp.int32, (n, 1), 0) & (SEG - 1)
    sub8 = lax.broadcasted_iota(jnp.int32, (SEG, 1), 0)
    grow = lax.broadcasted_iota(jnp.int32, (g, 1), 0)
    pr, pi_ = ar, ai
    wr, wi = jnp.broadcast_to(ar, (SEG, ar.shape[1])), jnp.broadcast_to(ai, (SEG, ai.shape[1]))
    k = 1
    while k < SEG:
        tr, ti = _cmul(pr, pi_, _shift(xr, k, sub, SEG, reverse, 0.0), _shift(xi, k, sub, SEG, reverse, 0.0))
        xr, xi = xr + tr, xi + ti
        pr, pi_ = _cmul(pr, pi_, pr, pi_)
        wr, wi = _cmul(wr, wi, _shift(wr, k, sub8, SEG, reverse, 1.0), _shift(wi, k, sub8, SEG, reverse, 0.0))
        k *= 2
    first, last = (g - 1, 0) if reverse else (0, g - 1)
    jr, ji = _cmul(pr, pi_, cr, ci)
    sr = _edge_rows(xr, reverse) + jnp.where(grow == first, jr, 0.0)
    si = _edge_rows(xi, reverse) + jnp.where(grow == first, ji, 0.0)
    k = 1
    while k < g:
        tr, ti = _cmul(pr, pi_, _shift(sr, k, grow, g, reverse, 0.0), _shift(si, k, grow, g, reverse, 0.0))
        sr, si = sr + tr, si + ti
        pr, pi_ = _cmul(pr, pi_, pr, pi_)
        k *= 2
    er, ei = _spread(_shift(sr, 1, grow, g, reverse, cr)), _spread(_shift(si, 1, grow, g, reverse, ci))
    tr, ti = _cmul(jnp.tile(wr, (g, 1)), jnp.tile(wi, (g, 1)), er, ei)
    return xr + tr, xi + ti, sr[last:last + 1, :], si[last:last + 1, :]


def tv_scan(aa, bb, carry, reverse=False):
    n = aa.shape[0]
    row = lax.broadcasted_iota(jnp.int32, (n, 1), 0)
    k = 1
    while k < n:
        bb = bb + aa * _shift(bb, k, row, n, reverse, 0.0)
        aa = aa * _shift(aa, k, row, n, reverse, 1.0)
        k *= 2
    h = bb + aa * carry
    last = 0 if reverse else n - 1
    return h, h[last:last + 1, :]


def s5_scan_fwd(z, bd_re, bd_im, cd_re, cd_im, a_re, a_im):
    t = z.shape[0]
    tb = min(S5_TB, t)

    def body(u_ref, bre, bim, cre, cim, are, aim, y_ref, xre_ref, xim_ref, car_re, car_im):
        @pl.when(pl.program_id(1) == 0)
        def _():
            car_re[...] = jnp.zeros_like(car_re)
            car_im[...] = jnp.zeros_like(car_im)

        u = u_ref[...].astype(bf16)
        xr, xi, car_re[...], car_im[...] = lti_scan(
            jnp.dot(u, bre[0], preferred_element_type=f32), jnp.dot(u, bim[0], preferred_element_type=f32),
            are[0], aim[0], car_re[...], car_im[...])
        xre_ref[...] = xr
        xim_ref[...] = xi
        y_ref[...] = (jnp.dot(xr.astype(bf16), cre[0], preferred_element_type=f32)
                      - jnp.dot(xi.astype(bf16), cim[0], preferred_element_type=f32))

    chunk = lambda r, c: pl.BlockSpec((1, r, c), lambda j, i: (j, 0, 0))
    return pl.pallas_call(
        body, name="s5_scan_fwd", grid=(NCH, t // tb),
        in_specs=[pl.BlockSpec((tb, LANE), lambda j, i: (i, C_UA // LANE + j)),
                  chunk(LANE, S5_SC), chunk(LANE, S5_SC), chunk(S5_SC, LANE), chunk(S5_SC, LANE),
                  chunk(1, S5_SC), chunk(1, S5_SC)],
        out_specs=[pl.BlockSpec((tb, LANE), lambda j, i: (i, j)),
                   pl.BlockSpec((tb, S5_SC), lambda j, i: (i, j)),
                   pl.BlockSpec((tb, S5_SC), lambda j, i: (i, j))],
        out_shape=[jax.ShapeDtypeStruct((t, W_MIX), f32),
                   jax.ShapeDtypeStruct((t, NCH * S5_SC), f32),
                   jax.ShapeDtypeStruct((t, NCH * S5_SC), f32)],
        scratch_shapes=[pltpu.VMEM((1, S5_SC), f32)] * 2,
        compiler_params=_cparams(("parallel", "arbitrary")),
    )(z, bd_re.astype(bf16), bd_im.astype(bf16), cd_re.astype(bf16), cd_im.astype(bf16), a_re, a_im)


def s5_scan_bwd(dy, du1, z, xre, xim, bd_re, bd_im, cd_re, cd_im, a_re, a_im):
    t = z.shape[0]
    tb = min(S5_TB, t)
    nt = t // tb

    def body(dy_ref, du1_ref, u_ref, xre_ref, xim_ref, hre_ref, him_ref, bre, bim, cre, cim, are, aim,
             du_ref, dbre, dbim, dcre, dcim, dare, daim, car_re, car_im):
        step = pl.program_id(1)
        tt = nt - 1 - step

        @pl.when(step == 0)
        def _():
            car_re[...] = jnp.zeros_like(car_re)
            car_im[...] = jnp.zeros_like(car_im)

        nt_dims = (((1,), (1,)), ((), ()))
        tn_dims = (((0,), (0,)), ((), ()))
        dyb = dy_ref[...].astype(bf16)
        row = lax.broadcasted_iota(jnp.int32, (tb, 1), 0)
        xr, xi = xre_ref[...], xim_ref[...]
        ar, ai = are[0], aim[0]
        lr, li, car_re[...], car_im[...] = lti_scan(
            lax.dot_general(dyb, cre[0], nt_dims, preferred_element_type=f32),
            -lax.dot_general(dyb, cim[0], nt_dims, preferred_element_type=f32),
            ar, -ai, car_re[...], car_im[...], reverse=True)
        lrb, lib = lr.astype(bf16), li.astype(bf16)
        ub = u_ref[...].astype(bf16)
        du = (lax.dot_general(lrb, bre[0], nt_dims, preferred_element_type=f32)
              + lax.dot_general(lib, bim[0], nt_dims, preferred_element_type=f32))
        du_ref[...] = (du + du1_ref[...].astype(f32)).astype(du_ref.dtype)
        live = (tt > 0).astype(f32)
        xpr = jnp.where(row == 0, hre_ref[7:8, :] * live, pltpu.roll(xr, 1, 0))
        xpi = jnp.where(row == 0, him_ref[7:8, :] * live, pltpu.roll(xi, 1, 0))
        acc = [
            lax.dot_general(ub, lrb, tn_dims, preferred_element_type=f32),
            lax.dot_general(ub, lib, tn_dims, preferred_element_type=f32),
            lax.dot_general(xr.astype(bf16), dyb, tn_dims, preferred_element_type=f32),
            -lax.dot_general(xi.astype(bf16), dyb, tn_dims, preferred_element_type=f32),
            jnp.sum(lr * xpr + li * xpi, axis=0, keepdims=True),
            jnp.sum(li * xpr - lr * xpi, axis=0, keepdims=True),
        ]
        outs = [dbre, dbim, dcre, dcim, dare, daim]

        @pl.when(step == 0)
        def _():
            for o, a in zip(outs, acc):
                o[0] = a

        @pl.when(step > 0)
        def _():
            for o, a in zip(outs, acc):
                o[0] = o[0] + a

    chunk = lambda r, c: pl.BlockSpec((1, r, c), lambda j, i: (j, 0, 0))
    rev = lambda w, base=0: pl.BlockSpec((tb, w), lambda j, i: (nt - 1 - i, base + j))
    halo = pl.BlockSpec((8, S5_SC), lambda j, i: (jnp.maximum((nt - 1 - i) * (tb // 8) - 1, 0), j))
    return pl.pallas_call(
        body, name="s5_scan_bwd", grid=(NCH, nt),
        in_specs=[rev(LANE), rev(LANE), rev(LANE, C_UA // LANE), rev(S5_SC), rev(S5_SC), halo, halo,
                  chunk(LANE, S5_SC), chunk(LANE, S5_SC), chunk(S5_SC, LANE), chunk(S5_SC, LANE),
                  chunk(1, S5_SC), chunk(1, S5_SC)],
        out_specs=[rev(LANE), chunk(LANE, S5_SC), chunk(LANE, S5_SC), chunk(S5_SC, LANE), chunk(S5_SC, LANE),
                   chunk(1, S5_SC), chunk(1, S5_SC)],
        out_shape=[jax.ShapeDtypeStruct((t, W_MIX), bf16),
                   jax.ShapeDtypeStruct((NCH, LANE, S5_SC), f32), jax.ShapeDtypeStruct((NCH, LANE, S5_SC), f32),
                   jax.ShapeDtypeStruct((NCH, S5_SC, LANE), f32), jax.ShapeDtypeStruct((NCH, S5_SC, LANE), f32),
                   jax.ShapeDtypeStruct((NCH, 1, S5_SC), f32), jax.ShapeDtypeStruct((NCH, 1, S5_SC), f32)],
        scratch_shapes=[pltpu.VMEM((1, S5_SC), f32)] * 2,
        compiler_params=_cparams(("parallel", "arbitrary")),
    )(dy, du1, z, xre, xim, xre, xim, bd_re.astype(bf16), bd_im.astype(bf16), cd_re.astype(bf16),
      cd_im.astype(bf16), a_re, a_im)


RG_TB = 512


def _rg_conv(ext, w, cb, tb):
    acc = cb + w[3:4, :] * ext[8:, :]
    for k in range(3):
        acc = acc + w[k:k + 1, :] * pltpu.roll(ext, 3 - k, 0)[8:, :]
    return acc


def rg_fwd(z, cw, cb, gate_p):
    t = z.shape[0]
    tb = min(RG_TB, t)

    def body(x_ref, halo_ref, cw_ref, cb_ref, wa, ba, wx, bx, lam, xc_ref, a_ref, h_ref, car):
        i = pl.program_id(1)

        @pl.when(i == 0)
        def _():
            car[...] = jnp.zeros_like(car)

        ext = jnp.concatenate([halo_ref[...] * (i > 0).astype(f32), x_ref[...]], axis=0)
        xc = _rg_conv(ext, cw_ref[0], cb_ref[0], tb)
        tglob = i * tb + lax.broadcasted_iota(jnp.int32, (tb, 1), 0)
        a, b = rg_gate_fn(tglob, xc, wa[0], ba[0], wx[0], bx[0], lam[0])
        xc_ref[...] = xc
        a_ref[...] = a
        h_ref[...], car[...] = tv_scan(a, b, car[...])

    base = C_XB // LANE
    out = pl.BlockSpec((tb, LANE), lambda j, i: (i, j))
    return pl.pallas_call(
        body, name="rg_fwd", grid=(NCH, t // tb),
        in_specs=[pl.BlockSpec((tb, LANE), lambda j, i: (i, base + j)),
                  pl.BlockSpec((8, LANE), lambda j, i: (jnp.maximum(i * (tb // 8) - 1, 0), base + j)),
                  _slab_spec(cw), _slab_spec(cb)] + [_slab_spec(p) for p in gate_p],
        out_specs=[out, out, out], out_shape=[jax.ShapeDtypeStruct((t, W_MIX), f32)] * 3,
        scratch_shapes=[pltpu.VMEM((1, LANE), f32)],
        compiler_params=_cparams(("parallel", "arbitrary")),
    )(z, z, cw, cb, *gate_p)


def rg_bwd(dh, z, xc, a, h, cw, gate_p):
    t = z.shape[0]
    tb = min(RG_TB, t)
    nt = t // tb

    def body(g_ref, a_ref, an_ref, h_ref, hp_ref, xc_ref, x_ref, xh_ref, cw_ref, wa, ba, wx, bx, lam,
             dx_ref, dcw_ref, dcb_ref, dwa, dba, dwx, dbx, dlam, car, later):
        step = pl.program_id(1)
        tt = nt - 1 - step

        @pl.when(step == 0)
        def _():
            car[...] = jnp.zeros_like(car)
            later[...] = jnp.zeros_like(later)

        row = lax.broadcasted_iota(jnp.int32, (tb, 1), 0)
        an = an_ref[0:1, :] * (tt < nt - 1).astype(f32)
        aa = jnp.where(row == tb - 1, an, pltpu.roll(a_ref[...], tb - 1, 0))
        lmb, car[...] = tv_scan(aa, g_ref[...], car[...], reverse=True)
        hp = jnp.where(row == 0, hp_ref[7:8, :] * (tt > 0).astype(f32), pltpu.roll(h_ref[...], 1, 0))
        tglob = tt * tb + row
        _, vjp = jax.vjp(lambda *v: rg_gate_fn(tglob, *v), xc_ref[...], wa[0], ba[0], wx[0], bx[0], lam[0])
        g, *dp = vjp((lmb * hp, lmb))
        gext = jnp.concatenate([g, later[...]], axis=0)
        later[...] = g[0:8, :]
        xext = jnp.concatenate([xh_ref[...] * (tt > 0).astype(f32), x_ref[...]], axis=0)
        w = cw_ref[0]
        dx = w[3:4, :] * g
        taps = [None] * RG_CONV
        taps[3] = jnp.sum(g * xext[8:, :], axis=0, keepdims=True)
        for k in range(3):
            s = 3 - k
            dx = dx + w[k:k + 1, :] * pltpu.roll(gext, tb + 8 - s, 0)[:tb, :]
            taps[k] = jnp.sum(g * pltpu.roll(xext, s, 0)[8:, :], axis=0, keepdims=True)
        dx_ref[...] = dx.astype(dx_ref.dtype)
        acc = [jnp.concatenate(taps, axis=0), jnp.sum(g, axis=0, keepdims=True)] + dp
        outs = [dcw_ref, dcb_ref, dwa, dba, dwx, dbx, dlam]

        @pl.when(step == 0)
        def _():
            for o, v in zip(outs, acc):
                o[0] = v

        @pl.when(step > 0)
        def _():
            for o, v in zip(outs, acc):
                o[0] = o[0] + v

    base = C_XB // LANE
    rev = lambda b=0: pl.BlockSpec((tb, LANE), lambda j, i: (nt - 1 - i, b + j))
    nxt = pl.BlockSpec((8, LANE), lambda j, i: (jnp.minimum((nt - i) * (tb // 8), t // 8 - 1), j))
    prv = lambda b=0: pl.BlockSpec((8, LANE), lambda j, i: (jnp.maximum((nt - 1 - i) * (tb // 8) - 1, 0), b + j))
    params = [cw] + list(gate_p)
    grads = [jax.ShapeDtypeStruct(s, f32) for s in [cw.shape, (NCH, 1, LANE)] + [p.shape for p in gate_p]]
    return pl.pallas_call(
        body, name="rg_bwd", grid=(NCH, nt),
        in_specs=[rev(), rev(), nxt, rev(), prv(), rev(), rev(base), prv(base)] + [_slab_spec(p) for p in params],
        out_specs=[rev()] + [_slab_spec(p) for p in grads], out_shape=[jax.ShapeDtypeStruct((t, W_MIX), bf16)] + grads,
        scratch_shapes=[pltpu.VMEM((1, LANE), f32), pltpu.VMEM((8, LANE), f32)],
        compiler_params=_cparams(("parallel", "arbitrary")),
    )(dh, a, a, h, h, xc, z, z, *params)


HG_TB = 256
HALF = HG_SUB // 2


def _heads(v):
    return jnp.stack([v[:, LANE * h:LANE * (h + 1)] for h in range(HG_HEADS)])


def _unheads(v):
    return jnp.concatenate([v[h] for h in range(HG_HEADS)], axis=-1)


def _bmm(eq, a, b):
    return jnp.einsum(eq, a.astype(bf16), b.astype(bf16), preferred_element_type=f32)


def hg_chunk_fwd(qs, kk, gcum, z):
    t = qs.shape[0]
    tb = min(HG_TB, t)
    nc = tb // HG_SUB

    def body(q_ref, k_ref, g_ref, v_ref, o_ref, sall_ref, st_ref):
        @pl.when(pl.program_id(0) == 0)
        def _():
            st_ref[...] = jnp.zeros_like(st_ref)

        ri = lax.broadcasted_iota(jnp.int32, (1, HALF, 1), 1)

        def chunk(c, carry):
            rows = pl.ds(pl.multiple_of(c * HG_SUB, HG_SUB), HG_SUB)
            q, k, g, v = _heads(q_ref[rows, :]), _heads(k_ref[rows, :]), _heads(g_ref[rows, :]), _heads(v_ref[rows, :])
            st = st_ref[...]
            sall_ref[c] = st
            o = _bmm('htk,hvk->htv', q * jnp.exp(g), st)
            halves = [[q[:, :HALF], g[:, :HALF], o[:, :HALF]], [q[:, HALF:], g[:, HALF:], o[:, HALF:]]]
            for s in range(HG_SUB):
                grow, krow, vrow = g[:, s:s + 1, :], k[:, s:s + 1, :], v[:, s:s + 1, :]
                for h in range(s // HALF, 2):
                    qh, gh, oh = halves[h]
                    p = jnp.exp(jnp.minimum(gh - grow, 0.0))
                    if s // HALF == h:
                        p = jnp.where(ri >= s - h * HALF, p, 0.0)
                    halves[h][2] = oh + jnp.sum(qh * krow * p, axis=-1, keepdims=True) * vrow
            o = jnp.concatenate([halves[0][2], halves[1][2]], axis=1)
            gl = g[:, HG_SUB - 1:HG_SUB, :]
            st_ref[...] = st * jnp.exp(gl) + _bmm('htv,htk->hvk', v, k * jnp.exp(gl - g))
            o_ref[rows, :] = _unheads(o)
            return carry

        lax.fori_loop(0, nc, chunk, 0)

    spec = lambda base=0: pl.BlockSpec((tb, W_MIX), lambda i: (i, base))
    return pl.pallas_call(
        body, name="hg_chunk_fwd", grid=(t // tb,),
        in_specs=[spec(), spec(), spec(), spec(C_I // W_MIX)],
        out_specs=[spec(), pl.BlockSpec((nc, HG_HEADS, HG_DK, HG_DK), lambda i: (i, 0, 0, 0))],
        out_shape=[jax.ShapeDtypeStruct((t, W_MIX), f32),
                   jax.ShapeDtypeStruct((t // HG_SUB, HG_HEADS, HG_DK, HG_DK), f32)],
        scratch_shapes=[pltpu.VMEM((HG_HEADS, HG_DK, HG_DK), f32)],
        compiler_params=_cparams(("arbitrary",)),
    )(qs, kk, gcum, z)


def hg_chunk_bwd(do, qs, kk, gcum, z, sall):
    t = qs.shape[0]
    tb = min(HG_TB, t)
    nc = tb // HG_SUB
    nt = t // tb

    def body(do_ref, q_ref, k_ref, g_ref, v_ref, sall_ref, dq_ref, dk_ref, dg_ref, dv_ref, dst_ref):
        @pl.when(pl.program_id(0) == 0)
        def _():
            dst_ref[...] = jnp.zeros_like(dst_ref)

        ri = lax.broadcasted_iota(jnp.int32, (1, HALF, 1), 1)
        ri_chunk = lax.broadcasted_iota(jnp.int32, (1, HG_SUB, 1), 1)

        def chunk(cc, carry):
            c = nc - 1 - cc
            rows = pl.ds(pl.multiple_of(c * HG_SUB, HG_SUB), HG_SUB)
            q, k, g, v = _heads(q_ref[rows, :]), _heads(k_ref[rows, :]), _heads(g_ref[rows, :]), _heads(v_ref[rows, :])
            d_o = _heads(do_ref[rows, :])
            st = sall_ref[c]
            dsn = dst_ref[...]
            eg = jnp.exp(g)
            qe = q * eg
            gl = g[:, HG_SUB - 1:HG_SUB, :]
            egl = jnp.exp(gl)
            dec = jnp.exp(gl - g)
            kd = k * dec
            dqe = _bmm('htv,hvk->htk', d_o, st)
            dst_ref[...] = _bmm('htv,htk->hvk', d_o, qe) + dsn * egl
            dgl_dec = jnp.sum(dsn * st, axis=1, keepdims=True) * egl
            dv = _bmm('htk,hvk->htv', kd, dsn)
            dkd = _bmm('htv,hvk->htk', v, dsn)
            tq, tg, tdo = [q[:, :HALF], q[:, HALF:]], [g[:, :HALF], g[:, HALF:]], [d_o[:, :HALF], d_o[:, HALF:]]
            a1 = [jnp.zeros_like(tq[0]), jnp.zeros_like(tq[1])]
            a2 = [jnp.zeros_like(tq[0]), jnp.zeros_like(tq[1])]
            dvh = [dv[:, :HALF], dv[:, HALF:]]
            for s in range(HG_SUB):
                grow, krow, vrow = g[:, s:s + 1, :], k[:, s:s + 1, :], v[:, s:s + 1, :]
                sh, sr = s // HALF, s % HALF
                dv_s, a2_s = 0.0, 0.0
                for h in range(sh, 2):
                    p = jnp.exp(jnp.minimum(tg[h] - grow, 0.0))
                    if sh == h:
                        p = jnp.where(ri >= sr, p, 0.0)
                    col = jnp.sum(tq[h] * krow * p, axis=-1, keepdims=True)
                    t1 = jnp.sum(tdo[h] * vrow, axis=-1, keepdims=True) * p
                    a1[h] = a1[h] + t1 * krow
                    dv_s = dv_s + jnp.sum(col * tdo[h], axis=1, keepdims=True)
                    a2_s = a2_s + jnp.sum(t1 * tq[h], axis=1, keepdims=True)
                dvh[sh] = jnp.where(ri == sr, dvh[sh] + dv_s, dvh[sh])
                a2[sh] = jnp.where(ri == sr, a2_s, a2[sh])
            a1, a2 = jnp.concatenate(a1, axis=1), jnp.concatenate(a2, axis=1)
            dv = jnp.concatenate(dvh, axis=1)
            dgl = jnp.sum(dkd * kd, axis=1, keepdims=True) + dgl_dec
            dg = dqe * qe + q * a1 - k * a2 - dkd * kd
            dg = jnp.where(ri_chunk == HG_SUB - 1, dg + dgl, dg)
            dq_ref[rows, :] = _unheads(dqe * eg + a1)
            dk_ref[rows, :] = _unheads(dkd * dec + a2)
            dg_ref[rows, :] = _unheads(dg)
            dv_ref[rows, :] = _unheads(dv).astype(dv_ref.dtype)
            return carry

        lax.fori_loop(0, nc, chunk, 0)

    spec = lambda base=0: pl.BlockSpec((tb, W_MIX), lambda i: (nt - 1 - i, base))
    return pl.pallas_call(
        body, name="hg_chunk_bwd", grid=(nt,),
        in_specs=[spec(), spec(), spec(), spec(), spec(C_I // W_MIX),
                  pl.BlockSpec((nc, HG_HEADS, HG_DK, HG_DK), lambda i: (nt - 1 - i, 0, 0, 0))],
        out_specs=[spec(), spec(), spec(), spec()],
        out_shape=[jax.ShapeDtypeStruct((t, W_MIX), f32)] * 3 + [jax.ShapeDtypeStruct((t, W_MIX), bf16)],
        scratch_shapes=[pltpu.VMEM((HG_HEADS, HG_DK, HG_DK), f32)],
        compiler_params=_cparams(("arbitrary",)),
    )(do, qs, kk, gcum, z, sall)


def adamw(w, m, v, slots, *, name, tr):
    nl, r, c = w.shape
    tr = min(tr, r)
    flat = [a for per_layer in slots for a in per_layer]
    c1 = 1.0 / (1.0 - ADAM_B1 ** ADAM_STEP)
    c2 = 1.0 / (1.0 - ADAM_B2 ** ADAM_STEP)

    def body(*refs):
        w_ref, m_ref, v_ref = refs[:3]
        s_refs = list(refs[3:3 + len(flat)])
        g_ref, d_ref, mo_ref, vo_ref = refs[3 + len(flat):]
        for l in range(nl):
            parts = [s_refs.pop(0) for _ in slots[l]]
            g = None
            for p in parts:
                for s in range(p.shape[0]):
                    term = p[s].astype(f32)
                    g = term if g is None else g + term
            mn = ADAM_B1 * m_ref[l] + (1.0 - ADAM_B1) * g
            vn = ADAM_B2 * v_ref[l] + (1.0 - ADAM_B2) * (g * g)
            g_ref[l] = g
            mo_ref[l] = mn
            vo_ref[l] = vn
            d_ref[l] = -ADAM_LR * ((mn * c1) / (jnp.sqrt(vn * c2) + ADAM_EPS) + ADAM_WD * w_ref[l])

    full = pl.BlockSpec((nl, tr, c), lambda i: (0, i, 0))
    slot = [pl.BlockSpec((a.shape[0], tr, c), lambda i: (0, i, 0)) for a in flat]
    return pl.pallas_call(
        body, name=name, grid=(r // tr,), in_specs=[full] * 3 + slot, out_specs=[full] * 4,
        out_shape=[jax.ShapeDtypeStruct(w.shape, f32)] * 4, compiler_params=_cparams(("parallel",)),
    )(w, m, v, *flat)


def _slab(ref, axis, idx, n):
    return ref.at[tuple([slice(None)] * axis + [pl.ds(idx * n, n)])]


def all_gather(x, axis, *, name):
    n = x.shape[axis]
    out_shape = x.shape[:axis] + (N_DEV * n,) + x.shape[axis + 1:]

    def body(x_ref, out_ref, send_sems, recv_sems, local_sem):
        xx, yy, cc = lax.axis_index("x"), lax.axis_index("y"), lax.axis_index("c")
        me, sibling = (xx, yy, cc), (xx, yy, 1 - cc)
        chips = [(1 - xx, yy), (xx, 1 - yy), (1 - xx, 1 - yy)]

        def slab(px, py, pc):
            return _slab(out_ref, axis, 4 * px + 2 * py + pc, n)

        def copy(k, block, to, src=None):
            return pltpu.make_async_remote_copy(
                src_ref=slab(*block) if src is None else src, dst_ref=slab(*block),
                send_sem=send_sems.at[k], recv_sem=recv_sems.at[k], device_id=to, device_id_type=MESH)

        mine = pltpu.make_async_copy(x_ref, slab(*me), local_sem)
        mine.start()
        first = [copy(0, me, sibling, src=x_ref)]
        first += [copy(1 + j, me, (*chip, cc), src=x_ref) for j, chip in enumerate(chips)]
        for cp in first:
            cp.start()
        passed = [copy(4 + j, (*chip, cc), sibling) for j, chip in enumerate(chips)]
        for j, chip in enumerate(chips):
            copy(1 + j, (*chip, cc), me).wait_recv()
            passed[j].start()
        copy(0, sibling, me).wait_recv()
        for j, chip in enumerate(chips):
            copy(4 + j, (*chip, 1 - cc), me).wait_recv()
        for cp in first + passed:
            cp.wait_send()
        mine.wait()

    return pl.pallas_call(
        body, name=name, out_shape=jax.ShapeDtypeStruct(out_shape, x.dtype), in_specs=[ANY], out_specs=ANY,
        scratch_shapes=[pltpu.SemaphoreType.DMA((7,)), pltpu.SemaphoreType.DMA((7,)), pltpu.SemaphoreType.DMA],
    )(x)


N_CHIP = 4


HBM = pl.BlockSpec(memory_space=pltpu.HBM)
SEM = pl.BlockSpec(memory_space=pltpu.SEMAPHORE)
EFFECT = pltpu.SideEffectType.DATAFLOW_SIDE_EFFECTING
TOKEN = jax.ShapeDtypeStruct((8, LANE), f32)


def _in_hbm(a):
    return pltpu.with_memory_space_constraint(a, pltpu.HBM)


def pair_sums(g, axis, *, name):
    n = g.shape[axis] // N_DEV
    slab_shape = g.shape[:axis] + (n,) + g.shape[axis + 1:]
    cols = slab_shape[-1]
    rows = math.prod(slab_shape[:-1])
    col_slabs = axis == g.ndim - 1
    assert col_slabs or (axis == 0 and g.ndim == 2)

    def swap_body(g_ref, got_ref, send_sems, recv_sems):
        xx, yy, cc = lax.axis_index("x"), lax.axis_index("y"), lax.axis_index("c")
        copies = [pltpu.make_async_remote_copy(
            src_ref=_slab(g_ref, axis, 2 * q + 1 - cc, n), dst_ref=got_ref.at[q],
            send_sem=send_sems.at[q], recv_sem=recv_sems.at[q], device_id=(xx, yy, 1 - cc), device_id_type=MESH)
            for q in range(N_CHIP)]
        for cp in copies:
            cp.start()
        for cp in copies:
            cp.wait()

    got = pl.pallas_call(
        swap_body, name=name + "_swap", out_shape=jax.ShapeDtypeStruct((N_CHIP,) + slab_shape, g.dtype),
        in_specs=[ANY], out_specs=ANY, scratch_shapes=[pltpu.SemaphoreType.DMA((N_CHIP,))] * 2,
    )(g)

    tr = min(256, rows)

    def add_body(a0_ref, a1_ref, b_ref, pair_ref, own_ref):
        xx, yy, cc = lax.axis_index("x"), lax.axis_index("y"), lax.axis_index("c")
        mine = jnp.where(cc == 0, a0_ref[...], a1_ref[...])
        s = (mine.astype(f32) + b_ref[0].astype(f32)).astype(bf16)
        pair_ref[0] = s

        @pl.when(pl.program_id(1) == 2 * xx + yy)
        def _():
            own_ref[0] = s

    if col_slabs:
        a_spec = lambda c: pl.BlockSpec((tr, cols), lambda i, q: (i, 2 * q + c))
    else:
        a_spec = lambda c: pl.BlockSpec((tr, cols), lambda i, q: ((2 * q + c) * (n // tr) + i, 0))
    by_chip = pl.BlockSpec((1, tr, cols), lambda i, q: (q, i, 0))
    g2 = g.reshape(-1, g.shape[-1])
    pair, own = pl.pallas_call(
        add_body, name=name + "_add", grid=(rows // tr, N_CHIP), in_specs=[a_spec(0), a_spec(1), by_chip],
        out_specs=[by_chip, pl.BlockSpec((1, tr, cols), lambda i, q: (0, i, 0))],
        out_shape=[jax.ShapeDtypeStruct((N_CHIP, rows, cols), bf16), jax.ShapeDtypeStruct((1, rows, cols), bf16)],
        compiler_params=_cparams(("parallel", "arbitrary")),
    )(g2, g2, got.reshape(N_CHIP, rows, cols))
    return own, pair


def _send_copies(p_refs, land_refs, send_sems, recv_sems):
    xx, yy, cc = lax.axis_index("x"), lax.axis_index("y"), lax.axis_index("c")
    copies = []
    for t, (p, land) in enumerate(zip(p_refs, land_refs)):
        for k in range(1, N_CHIP):
            px = 1 - xx if k & 2 else xx
            py = 1 - yy if k & 1 else yy
            s = (N_CHIP - 1) * t + k - 1
            copies.append(pltpu.make_async_remote_copy(
                src_ref=p.at[2 * px + py], dst_ref=land.at[k - 1], send_sem=send_sems.at[s], recv_sem=recv_sems.at[s],
                device_id=(px, py, cc), device_id_type=MESH))
    return copies


def _direct_copies(axes, g_refs, land_refs, send_sems, recv_sems):
    xx, yy, cc = lax.axis_index("x"), lax.axis_index("y"), lax.axis_index("c")
    copies = []
    for t, (g_ref, land, axis) in enumerate(zip(g_refs, land_refs, axes)):
        n = g_ref.shape[axis] // N_DEV
        for k in range(1, N_DEV):
            px = 1 - xx if k & 4 else xx
            py = 1 - yy if k & 2 else yy
            pc = 1 - cc if k & 1 else cc
            s = (N_DEV - 1) * t + k - 1
            copies.append(pltpu.make_async_remote_copy(
                src_ref=_slab(g_ref, axis, 4 * px + 2 * py + pc, n), dst_ref=land.at[k - 1],
                send_sem=send_sems.at[s], recv_sem=recv_sems.at[s], device_id=(px, py, pc), device_id_type=MESH))
    return copies


def own_slab(g, axis, *, name):
    n = g.shape[axis] // N_DEV
    slab_shape = g.shape[:axis] + (n,) + g.shape[axis + 1:]
    cols, rows = slab_shape[-1], math.prod(slab_shape[:-1])
    tr = min(256, rows)

    def body(g_ref, o_ref):
        o_ref[0] = g_ref[...]

    def where(i):
        me = 4 * lax.axis_index("x") + 2 * lax.axis_index("y") + lax.axis_index("c")
        return (i, me) if axis == g.ndim - 1 else (me * (n // tr) + i, 0)

    return pl.pallas_call(
        body, name=name, grid=(rows // tr,), in_specs=[pl.BlockSpec((tr, cols), where)],
        out_specs=pl.BlockSpec((1, tr, cols), lambda i: (0, i, 0)),
        out_shape=jax.ShapeDtypeStruct((1, rows, cols), g.dtype), compiler_params=_cparams(("parallel",)),
    )(g.reshape(-1, g.shape[-1]))


def send_pairs_start(pairs, after, *, name, direct_axes=None):
    nt = len(pairs)
    if direct_axes is None:
        build = _send_copies
        lands = [lax.empty((N_CHIP - 1,) + p.shape[1:], p.dtype) for p in pairs]
    else:
        build = functools.partial(_direct_copies, direct_axes)
        lands = [lax.empty((N_DEV - 1,) + p.shape[:ax] + (p.shape[ax] // N_DEV,) + p.shape[ax + 1:], p.dtype)
                 for p, ax in zip(pairs, direct_axes)]

    def body(*refs):
        p_refs, land_refs = refs[:nt], refs[nt:2 * nt]
        send_sems, recv_sems = refs[2 * nt + 1], refs[2 * nt + 2]
        token = refs[-1]
        for cp in build(p_refs, land_refs, send_sems, recv_sems):
            cp.start()
        token[...] = jnp.zeros_like(token)

    nsem = sum(l.shape[0] for l in lands)
    outs = pl.pallas_call(
        body, name=name,
        out_shape=(pltpu.SemaphoreType.DMA((nsem,)), pltpu.SemaphoreType.DMA((nsem,)))
        + tuple(pltpu.HBM(a.shape, a.dtype) for a in list(pairs) + lands) + (TOKEN,),
        in_specs=[HBM] * (2 * nt) + [ANY], out_specs=(SEM, SEM) + (HBM,) * (2 * nt) + (VM,),
        input_output_aliases={i: 2 + i for i in range(2 * nt)},
        compiler_params=pltpu.CompilerParams(has_side_effects=EFFECT),
    )(*[_in_hbm(a) for a in list(pairs) + lands], after)
    return outs[:-1], outs[-1]


def send_pairs_wait(handles, after, *, name, direct_axes=None):
    send_sems, recv_sems = handles[0], handles[1]
    bufs = handles[2:]
    nt = len(bufs) // 2
    build = _send_copies if direct_axes is None else functools.partial(_direct_copies, direct_axes)

    def body(*refs):
        p_refs, land_refs = refs[:nt], refs[nt:2 * nt]
        send_sems, recv_sems = refs[2 * nt], refs[2 * nt + 1]
        for cp in build(p_refs, land_refs, send_sems, recv_sems):
            cp.wait_send()
            cp.wait_recv()

    outs = pl.pallas_call(
        body, name=name, out_shape=tuple(pltpu.HBM(a.shape, a.dtype) for a in bufs),
        in_specs=[HBM] * (2 * nt) + [SEM, SEM, ANY], out_specs=(HBM,) * (2 * nt),
        input_output_aliases={i: i for i in range(2 * nt)},
        compiler_params=pltpu.CompilerParams(has_side_effects=EFFECT),
    )(*bufs, send_sems, recv_sems, after)
    return outs[nt:]


def _gather_copies(x_refs, land_refs, axes, send_sems, recv_sems):
    xx, yy, cc = lax.axis_index("x"), lax.axis_index("y"), lax.axis_index("c")
    me = 4 * xx + 2 * yy + cc
    copies = []
    for t, (x_ref, land, axis) in enumerate(zip(x_refs, land_refs, axes)):
        n = x_ref.shape[axis]
        for k in range(1, N_DEV):
            px = 1 - xx if k & 4 else xx
            py = 1 - yy if k & 2 else yy
            pc = 1 - cc if k & 1 else cc
            s = (N_DEV - 1) * t + k - 1
            copies.append(pltpu.make_async_remote_copy(
                src_ref=x_ref, dst_ref=_slab(land, axis, me, n), send_sem=send_sems.at[s], recv_sem=recv_sems.at[s],
                device_id=(px, py, pc), device_id_type=MESH))
    return copies


def _place_own(x, axis, *, name):
    full = x.shape[:axis] + (N_DEV * x.shape[axis],) + x.shape[axis + 1:]
    lead = x.shape[0]
    tile = lead if axis == 0 else (256 if x.ndim == 2 and lead % 256 == 0 else 1 if x.ndim == 3 else lead)
    rest = (0,) * (x.ndim - 1)

    def body(land_ref, x_ref, o_ref):
        o_ref[...] = x_ref[...]

    def where(i):
        me = 4 * lax.axis_index("x") + 2 * lax.axis_index("y") + lax.axis_index("c")
        idx = [i] + list(rest)
        idx[axis] = me
        return tuple(idx)

    return pl.pallas_call(
        body, name=name, grid=(lead // tile,),
        in_specs=[ANY, pl.BlockSpec((tile,) + x.shape[1:], lambda i: (i,) + rest)],
        out_specs=pl.BlockSpec((tile,) + x.shape[1:], where),
        out_shape=jax.ShapeDtypeStruct(full, x.dtype), input_output_aliases={0: 0},
        compiler_params=_cparams(("arbitrary",)),
    )(lax.empty(full, x.dtype), x)


def gather_start(xs, axes, after, *, name):
    nt = len(xs)
    lands = [_place_own(x, axis, name=name + "_own") for x, axis in zip(xs, axes)]

    def body(*refs):
        x_refs, land_refs = refs[:nt], refs[nt:2 * nt]
        send_sems, recv_sems = refs[2 * nt + 1], refs[2 * nt + 2]
        token = refs[-1]
        for cp in _gather_copies(x_refs, land_refs, axes, send_sems, recv_sems):
            cp.start()
        token[...] = jnp.zeros_like(token)

    nsem = (N_DEV - 1) * nt
    outs = pl.pallas_call(
        body, name=name,
        out_shape=(pltpu.SemaphoreType.DMA((nsem,)), pltpu.SemaphoreType.DMA((nsem,)))
        + tuple(pltpu.HBM(a.shape, a.dtype) for a in list(xs) + lands) + (TOKEN,),
        in_specs=[HBM] * (2 * nt) + [ANY], out_specs=(SEM, SEM) + (HBM,) * (2 * nt) + (VM,),
        input_output_aliases={i: 2 + i for i in range(2 * nt)},
        compiler_params=pltpu.CompilerParams(has_side_effects=EFFECT),
    )(*[_in_hbm(a) for a in list(xs) + lands], after)
    return outs[:-1], outs[-1]


def gather_wait(handles, axes, after, *, name):
    send_sems, recv_sems = handles[0], handles[1]
    bufs = handles[2:]
    nt = len(bufs) // 2

    def body(*refs):
        x_refs, land_refs = refs[:nt], refs[nt:2 * nt]
        send_sems, recv_sems = refs[2 * nt], refs[2 * nt + 1]
        for cp in _gather_copies(x_refs, land_refs, axes, send_sems, recv_sems):
            cp.wait_send()
            cp.wait_recv()

    outs = pl.pallas_call(
        body, name=name, out_shape=tuple(pltpu.HBM(a.shape, a.dtype) for a in bufs),
        in_specs=[HBM] * (2 * nt) + [SEM, SEM, ANY], out_specs=(HBM,) * (2 * nt),
        input_output_aliases={i: i for i in range(2 * nt)},
        compiler_params=pltpu.CompilerParams(has_side_effects=EFFECT),
    )(*bufs, send_sems, recv_sems, after)
    return outs[nt:]


def _blockdiag(b, nb):
    j, _, r, c = b.shape
    eye = jnp.eye(nb, dtype=bool)[None, :, None, :, None]
    return jnp.where(eye, b[:, :, :, None, :], jnp.zeros((), b.dtype)).reshape(j, nb * r, nb * c)


def _diagblocks(d, nb):
    j, rr, cc = d.shape
    return jnp.einsum('jarac->jarc', d.reshape(j, nb, rr // nb, nb, cc // nb))


def _s5_b_dense(bbar):
    return _blockdiag(bbar.transpose(0, 2, 1).reshape(NCH, 8, S5_GROUP, S5_STATE), 8)


def _s5_b_undense(d):
    return _diagblocks(d, 8).reshape(S5_GROUPS, S5_GROUP, S5_STATE).transpose(0, 2, 1)


def _s5_c_dense(c):
    return _blockdiag(c.transpose(0, 2, 1).reshape(NCH, 8, S5_STATE, S5_GROUP), 8)


def _s5_c_undense(d):
    return _diagblocks(d, 8).reshape(S5_GROUPS, S5_STATE, S5_GROUP).transpose(0, 2, 1)


def _rg_dense(w):
    return _blockdiag(w.reshape(NCH, 2, RG_BLOCK, RG_BLOCK), 2)


def _rg_undense(d):
    return _diagblocks(d, 2).reshape(RG_BLOCKS, RG_BLOCK, RG_BLOCK)


def _chunks(v):
    return v.reshape(NCH, 1, LANE)


def _tri(tm):
    r = jnp.arange(tm)
    m = (r[:, None] >= r[None, :]) & (r[:, None] // HG_SUB == r[None, :] // HG_SUB)
    m = m.astype(f32)
    return m[None], m.T[None]


SMALL = ['norm_w', 's5_lambda_re', 's5_lambda_im', 's5_log_step', 's5_b_re', 's5_b_im', 's5_c_re', 's5_c_im',
         's5_d', 's5_b_glu', 'rg_conv_w', 'rg_conv_b', 'rg_w_a', 'rg_b_a', 'rg_w_x', 'rg_b_x', 'rg_lambda',
         'hg_lower_bounds', 'hg_norm_w', 'final_norm_w']
WEIGHTS = ['norm_w', 'w_in', 's5_lambda_re', 's5_lambda_im', 's5_log_step', 's5_b_re', 's5_b_im', 's5_c_re',
           's5_c_im', 's5_d', 's5_w_glu', 's5_b_glu', 'rg_conv_w', 'rg_conv_b', 'rg_w_a', 'rg_b_a', 'rg_w_x',
           'rg_b_x', 'rg_lambda', 'hg_lower_bounds', 'hg_norm_w', 'w_branch', 'w_out', 'final_norm_w']
PACK_ROWS = 512


def _pack(arrs):
    flat = jnp.concatenate([a.reshape(-1) for a in arrs])
    pad = (-flat.shape[0]) % (PACK_ROWS * LANE)
    return jnp.pad(flat, (0, pad)).reshape(1, -1, LANE)


def _unpack(buf, shapes):
    flat = buf.reshape(-1)
    out, off = [], 0
    for s in shapes:
        n = math.prod(s)
        out.append(flat[off:off + n].reshape(s))
        off += n
    return out


def _step(x, tgt, w, m, v):
    t = x.shape[0]
    tri, tri_t = _tri(min(LANE, t))
    me = 4 * lax.axis_index("x") + 2 * lax.axis_index("y") + lax.axis_index("c")

    big = ('w_in', 's5_w_glu', 'w_branch', 'w_out')
    big_axis = (1, 0, 2, 0)
    shards = lambda l: [w[k][l].astype(bf16) for k in big]
    win, wglu, wbr, wout = ([None] * DEPTH for _ in range(4))
    win[0] = all_gather(shards(0)[0], big_axis[0], name="ag_w_in")
    rest0_axis = big_axis[1:] + (1,)
    rest0, rest0_token = gather_start(shards(0)[1:] + [w['rg_conv_w'].reshape(DEPTH * RG_CONV, LANE)], rest0_axis,
                                      win[0], name="ag_start_0")

    lb_rows = [w['hg_lower_bounds'][l][None] for l in range(DEPTH)]
    lbs = whole(lb_prep_fn, lb_rows, [(1, W_MIX)] * DEPTH, name="lb_prep")

    saved = []
    for l in range(DEPTH):
        s = {}
        nw = w['norm_w'][l].reshape(1, 1, D_MODEL)
        (h,) = rowwise(ln_fn, [(x, 0, D_MODEL)], [nw], [], [(D_MODEL, bf16)], name="ln_fwd", tm=512)
        token = None
        if l + 1 < DEPTH:
            handles, token = gather_start(shards(l + 1), big_axis, rest0_token if l == 0 else x, name=f"ag_start_{l + 1}")
        z = mm(h, win[l], after=token, name="mm_in", tn=2048)
        if l == 0:
            wglu[0], wbr[0], wout[0], conv_w = gather_wait(rest0, rest0_axis, z, name="ag_wait_0")
            conv_w = conv_w.reshape(DEPTH, RG_CONV, W_MIX)
        s5p = [w['s5_lambda_re'][l][..., None], w['s5_lambda_im'][l][..., None], w['s5_log_step'][l][:, None, None],
               w['s5_b_re'][l], w['s5_b_im'][l]]
        gp = (S5_GROUPS, S5_STATE)
        abar_re, abar_im, bbar_re, bbar_im = whole(
            s5_prep_fn, s5p, [gp + (1,), gp + (1,), gp + (S5_GROUP,), gp + (S5_GROUP,)], name="s5_prep")
        a_re, a_im = abar_re.reshape(NCH, 1, S5_SC), abar_im.reshape(NCH, 1, S5_SC)
        bd_re, bd_im = _s5_b_dense(bbar_re), _s5_b_dense(bbar_im)
        cd_re, cd_im = _s5_c_dense(w['s5_c_re'][l]), _s5_c_dense(w['s5_c_im'][l])
        yssm, xre, xim = s5_scan_fwd(z, bd_re, bd_im, cd_re, cd_im, a_re, a_im)
        s5post_p = [w['s5_d'][l].reshape(1, 1, W_MIX), wglu[l].astype(f32)[None], w['s5_b_glu'][l].reshape(1, 1, W_MIX)]
        s5post_rows = [(yssm, 0, W_MIX), (z, C_UA, W_MIX), (z, C_GA, W_MIX)]
        (ya,) = rowwise(s5_post_fn, s5post_rows, s5post_p, [], [(W_MIX, bf16)], name="s5_post_fwd", tm=512)
        cw, cb = conv_w[l].reshape(RG_CONV, NCH, LANE).transpose(1, 0, 2), _chunks(w['rg_conv_b'][l])
        rg_p = [_rg_dense(w['rg_w_a'][l]), _chunks(w['rg_b_a'][l]), _rg_dense(w['rg_w_x'][l]),
                _chunks(w['rg_b_x'][l]), _chunks(w['rg_lambda'][l])]
        xc, ra, hb = rg_fwd(z, cw, cb, rg_p)
        hg_rows = [(z, C_Q, W_MIX), (z, C_F, W_MIX)]
        hg_p = [_chunks(lbs[l].reshape(W_MIX))]
        qs, kk, gcum = rowwise(hg_pre_fn, hg_rows, hg_p, [tri, tri_t], [(W_MIX, f32)] * 3, name="hg_pre_fwd", ncol=NCH, tm=TM_CHUNK)
        oc, sall = hg_chunk_fwd(qs, kk, gcum, z)
        bp_rows = [(hb, 0, W_MIX), (z, C_GB, W_MIX), (oc, 0, W_MIX), (z, C_GC, W_MIX)]
        bp_p = [_chunks(w['hg_norm_w'][l])]
        yb, yc = rowwise(branch_prep_fn, bp_rows, bp_p, [], [(W_MIX, bf16)] * 2, name="branch_prep_fwd", ncol=NCH, tm=TM_CHUNK)
        ys = [ya, yb, yc]
        br = [mm(ys[n], wbr[l][n], name="mm_branch", out_dtype=bf16) for n in range(N_BRANCH)]
        mg_rows = [(br[n], 0, D_MODEL) for n in range(N_BRANCH)] + [(z, C_GATE + n * D_MODEL, D_MODEL) for n in range(N_BRANCH)]
        (merged,) = rowwise(merge_fn, mg_rows, [], [], [(D_MODEL, bf16)], name="merge_fwd", ncol=2)
        x_new = mm(merged, wout[l], add=x, name="mm_out")
        s.update(x=x, h=h, z=z, s5p=s5p, s5=(bd_re, bd_im, cd_re, cd_im, a_re, a_im), xre=xre, xim=xim,
                 s5post_rows=s5post_rows, s5post_p=s5post_p, cw=cw, xc=xc, rg_p=rg_p, ra=ra, hb=hb,
                 hg_rows=hg_rows, hg_p=hg_p, qs=qs, kk=kk, gcum=gcum, sall=sall, bp_rows=bp_rows, bp_p=bp_p,
                 ys=ys, mg_rows=mg_rows, merged=merged, nw=nw)
        saved.append(s)
        x = x_new
        if l + 1 < DEPTH:
            win[l + 1], wglu[l + 1], wbr[l + 1], wout[l + 1] = gather_wait(handles, big_axis, x, name=f"ag_wait_{l + 1}")

    fnw = w['final_norm_w'].reshape(1, 1, D_MODEL)
    ones = jnp.ones((t, 1), f32)
    dx, d_fnw, loss_sum = rowwise_vjp(loss_fn, [(x, 0, D_MODEL), (tgt, 0, D_MODEL)], [fnw], [], [(ones, 0, 1)],
                                      [(0, f32)], name="loss_head", sum_primal=0)
    loss = lax.psum(loss_sum.reshape(()), ("x", "y", "c"))

    small_g = {k: [None] * DEPTH for k in SMALL if k != 'final_norm_w'}
    own_sums, in_flight = [None] * DEPTH, [None] * DEPTH
    d_lbs = [None] * DEPTH
    token = None
    for l in reversed(range(DEPTH)):
        s = saved[l]
        z = s['z']
        dxb = dx.astype(bf16)
        d_merged = mm(dxb, wout[l], bt=True, after=token, name="mm_out_dx", out_dtype=bf16)
        d_wout = mm(s['merged'], dxb, at=True, name="mm_out_dw", out_dtype=bf16)
        mg = rowwise_vjp(merge_fn, s['mg_rows'], [], [], [(d_merged, 0, D_MODEL)],
                         [(n, bf16) for n in range(2 * N_BRANCH)], name="merge_bwd", ncol=2)
        d_br, d_gl = mg[:N_BRANCH], mg[N_BRANCH:]
        d_ys = [mm(d_br[n], wbr[l][n], bt=True, name="mm_branch_dx", out_dtype=bf16) for n in range(N_BRANCH)]
        d_wbr = jnp.stack([mm(s['ys'][n], d_br[n], at=True, name="mm_branch_dw", out_dtype=bf16) for n in range(N_BRANCH)])
        d_hb, d_gb, d_oc, d_gc, d_hnw = rowwise_vjp(
            branch_prep_fn, s['bp_rows'], s['bp_p'], [], [(d_ys[1], 0, W_MIX), (d_ys[2], 0, W_MIX)],
            [(0, f32), (1, bf16), (2, f32), (3, bf16)], name="branch_prep_bwd", ncol=NCH, tm=TM_CHUNK)
        small_g['hg_norm_w'][l] = d_hnw.reshape(W_MIX)
        d_qs, d_kk, d_gcum, d_i = hg_chunk_bwd(d_oc, s['qs'], s['kk'], s['gcum'], z, s['sall'])
        d_q, d_f, d_lb = rowwise_vjp(
            hg_pre_fn, s['hg_rows'], s['hg_p'], [tri, tri_t], [(d_qs, 0, W_MIX), (d_kk, 0, W_MIX), (d_gcum, 0, W_MIX)],
            [(0, bf16), (1, bf16)], name="hg_pre_bwd", ncol=NCH, tm=TM_CHUNK)
        d_lbs[l] = d_lb.reshape(1, W_MIX)
        d_xb, d_cw, d_cb, d_wa, d_ba, d_wx, d_bx, d_lam = rg_bwd(d_hb, z, s['xc'], s['ra'], s['hb'], s['cw'], s['rg_p'])
        small_g['rg_w_a'][l], small_g['rg_w_x'][l] = _rg_undense(d_wa), _rg_undense(d_wx)
        small_g['rg_b_a'][l], small_g['rg_b_x'][l] = d_ba.reshape(W_MIX), d_bx.reshape(W_MIX)
        small_g['rg_lambda'][l] = d_lam.reshape(W_MIX)
        small_g['rg_conv_w'][l] = d_cw.transpose(1, 0, 2).reshape(RG_CONV, W_MIX)
        small_g['rg_conv_b'][l] = d_cb.reshape(W_MIX)
        d_yssm, d_u1, d_ga, d_d, d_wglu, d_bglu = rowwise_vjp(
            s5_post_fn, s['s5post_rows'], s['s5post_p'], [], [(d_ys[0], 0, W_MIX)],
            [(0, bf16), (1, bf16), (2, bf16)], name="s5_post_bwd", tm=512)
        small_g['s5_d'][l], small_g['s5_b_glu'][l] = d_d.reshape(W_MIX), d_bglu.reshape(W_MIX)
        d_ua, d_bdre, d_bdim, d_cdre, d_cdim, d_are, d_aim = s5_scan_bwd(d_yssm, d_u1, z, s['xre'], s['xim'], *s['s5'])
        small_g['s5_c_re'][l], small_g['s5_c_im'][l] = _s5_c_undense(d_cdre), _s5_c_undense(d_cdim)
        gp = (S5_GROUPS, S5_STATE, 1)
        s5g = whole_vjp(s5_prep_fn, s['s5p'],
                        [d_are.reshape(gp), d_aim.reshape(gp), _s5_b_undense(d_bdre), _s5_b_undense(d_bdim)],
                        name="s5_prep_bwd")
        small_g['s5_lambda_re'][l] = s5g[0].reshape(S5_GROUPS, S5_STATE)
        small_g['s5_lambda_im'][l] = s5g[1].reshape(S5_GROUPS, S5_STATE)
        small_g['s5_log_step'][l] = s5g[2].reshape(S5_GROUPS)
        small_g['s5_b_re'][l], small_g['s5_b_im'][l] = s5g[3], s5g[4]
        dz = jnp.concatenate([d_ua, d_ga, d_xb, d_gb, d_q, d_f, d_i, d_gc] + list(d_gl), axis=1)
        d_win = mm(s['h'], dz, at=True, name="mm_in_dw", out_dtype=bf16)
        grads = (d_win, d_wglu[0].astype(bf16), d_wbr, d_wout)
        if l == 0:
            sums = [pair_sums(g, ax, name="rs_" + k) for g, ax, k in zip(grads, big_axis, big)]
            own_sums[l] = [own for own, _ in sums]
            in_flight[l], token = send_pairs_start([pair for _, pair in sums], s5g[0], name=f"rs_start_{l}")
        else:
            own_sums[l] = [own_slab(g, ax, name="rs_own_" + k) for g, ax, k in zip(grads, big_axis, big)]
            in_flight[l], token = send_pairs_start(grads, s5g[0], name=f"rs_start_{l}", direct_axes=big_axis)
        d_h = mm(dz, win[l], bt=True, after=token, name="mm_in_dx", tk=N_IN // 4)
        dx, d_nw = rowwise_vjp(ln_res_fn, [(s['x'], 0, D_MODEL)], [s['nw']], [], [(d_h, 0, D_MODEL), (dx, 0, D_MODEL)],
                               [(0, f32)], name="ln_bwd", tm=512)
        small_g['norm_w'][l] = d_nw.reshape(D_MODEL)
    d_lb_raw = whole_vjp(lb_prep_fn, lb_rows, d_lbs, name="lb_prep_bwd")
    small_g['hg_lower_bounds'] = [r.reshape(W_MIX) for r in d_lb_raw]

    per_layer = [k for k in SMALL if k != 'final_norm_w']
    shapes = [(DEPTH,) + small_g[k][0].shape for k in per_layer] + [(D_MODEL,)]
    pieces = [small_g[k][l] for k in per_layer for l in range(DEPTH)] + [d_fnw]
    small_in_flight, small_token = gather_start([_pack(pieces)[0].astype(bf16)], (0,), dx, name="ag_small_start")
    res = {}

    arrived = [send_pairs_wait(in_flight[l], small_token, name=f"rs_wait_{l}", direct_axes=None if l == 0 else big_axis)
               for l in range(DEPTH)]
    for i, (k, tr) in enumerate((('w_in', 32), ('s5_w_glu', 32), ('w_branch', 128), ('w_out', 32))):
        shp = w[k].shape
        r3 = lambda a: a.reshape(DEPTH, -1, shp[-1])
        slots = [[own_sums[l][i], arrived[l][i].reshape(arrived[l][i].shape[0], -1, shp[-1])] for l in range(DEPTH)]
        outs = adamw(r3(w[k]), r3(m[k]), r3(v[k]), slots, name="adamw_" + k, tr=tr)
        for kind, buf in zip(('grad', 'delta', 'new_m', 'new_v'), outs):
            res[kind + '_' + k] = buf.reshape(shp)

    (g_all,) = gather_wait(small_in_flight, (0,), outs[0], name="ag_small_wait")
    g_all = g_all.reshape(N_DEV, -1, LANE)

    def local(d, k):
        return jnp.zeros(shapes[SMALL.index(k)], f32) if k == 'rg_conv_w' else d[k]
    packed = [_pack([local(d, k) for k in SMALL]) for d in (w, m, v)]
    outs = adamw(*packed, [[g_all]], name="adamw_small", tr=512)
    for kind, buf in zip(('grad', 'delta', 'new_m', 'new_v'), outs):
        for k, a in zip(SMALL, _unpack(buf, shapes)):
            res[kind + '_' + k] = a
    g_cw = lax.dynamic_slice_in_dim(res['grad_rg_conv_w'], me * LANE, LANE, axis=2)
    cw3 = lambda a: a.reshape(1, DEPTH * RG_CONV, LANE)
    outs = adamw(cw3(w['rg_conv_w']), cw3(m['rg_conv_w']), cw3(v['rg_conv_w']), [[cw3(g_cw)]], name="adamw_conv_w", tr=16)
    for kind, buf in zip(('grad', 'delta', 'new_m', 'new_v'), outs):
        res[kind + '_rg_conv_w'] = buf.reshape(DEPTH, RG_CONV, LANE)

    return (loss, dx[None]) + tuple(res[kind + '_' + k] for kind in ('grad', 'delta', 'new_m', 'new_v') for k in WEIGHTS)


def kernel(x, norm_w, w_in, s5_lambda_re, s5_lambda_im, s5_log_step, s5_b_re, s5_b_im, s5_c_re, s5_c_im, s5_d, s5_w_glu, s5_b_glu, rg_conv_w, rg_conv_b, rg_w_a, rg_b_a, rg_w_x, rg_b_x, rg_lambda, hg_lower_bounds, hg_norm_w, w_branch, w_out, final_norm_w, loss_target, m_norm_w, m_w_in, m_s5_lambda_re, m_s5_lambda_im, m_s5_log_step, m_s5_b_re, m_s5_b_im, m_s5_c_re, m_s5_c_im, m_s5_d, m_s5_w_glu, m_s5_b_glu, m_rg_conv_w, m_rg_conv_b, m_rg_w_a, m_rg_b_a, m_rg_w_x, m_rg_b_x, m_rg_lambda, m_hg_lower_bounds, m_hg_norm_w, m_w_branch, m_w_out, m_final_norm_w, v_norm_w, v_w_in, v_s5_lambda_re, v_s5_lambda_im, v_s5_log_step, v_s5_b_re, v_s5_b_im, v_s5_c_re, v_s5_c_im, v_s5_d, v_s5_w_glu, v_s5_b_glu, v_rg_conv_w, v_rg_conv_b, v_rg_w_a, v_rg_b_a, v_rg_w_x, v_rg_b_x, v_rg_lambda, v_hg_lower_bounds, v_hg_norm_w, v_w_branch, v_w_out, v_final_norm_w):
    w = dict(zip(WEIGHTS, (norm_w, w_in, s5_lambda_re, s5_lambda_im, s5_log_step, s5_b_re, s5_b_im, s5_c_re, s5_c_im, s5_d, s5_w_glu, s5_b_glu, rg_conv_w, rg_conv_b, rg_w_a, rg_b_a, rg_w_x, rg_b_x, rg_lambda, hg_lower_bounds, hg_norm_w, w_branch, w_out, final_norm_w)))
    m = dict(zip(WEIGHTS, (m_norm_w, m_w_in, m_s5_lambda_re, m_s5_lambda_im, m_s5_log_step, m_s5_b_re, m_s5_b_im, m_s5_c_re, m_s5_c_im, m_s5_d, m_s5_w_glu, m_s5_b_glu, m_rg_conv_w, m_rg_conv_b, m_rg_w_a, m_rg_b_a, m_rg_w_x, m_rg_b_x, m_rg_lambda, m_hg_lower_bounds, m_hg_norm_w, m_w_branch, m_w_out, m_final_norm_w)))
    v = dict(zip(WEIGHTS, (v_norm_w, v_w_in, v_s5_lambda_re, v_s5_lambda_im, v_s5_log_step, v_s5_b_re, v_s5_b_im, v_s5_c_re, v_s5_c_im, v_s5_d, v_s5_w_glu, v_s5_b_glu, v_rg_conv_w, v_rg_conv_b, v_rg_w_a, v_rg_b_a, v_rg_w_x, v_rg_b_x, v_rg_lambda, v_hg_lower_bounds, v_hg_norm_w, v_w_branch, v_w_out, v_final_norm_w)))
    return _step(x[0], loss_target[0], w, m, v)
```

```python
import functools
import math

import jax
import jax.numpy as jnp
from jax import lax
from jax.experimental import pallas as pl
from jax.experimental.pallas import tpu as pltpu

f32 = jnp.float32
bf16 = jnp.bfloat16

D_MODEL = 2048
W_MIX = 1024
DEPTH = 4
N_BRANCH = 3
N_IN = 8 * W_MIX + N_BRANCH * D_MODEL
S5_GROUPS, S5_STATE, S5_GROUP = 64, 64, 16
RG_BLOCKS, RG_BLOCK, RG_CONV, RG_C = 16, 64, 4, 8.0
HG_HEADS, HG_DK = 8, 128
HG_SUB = 16
EPS = 1e-6
ADAM_LR, ADAM_B1, ADAM_B2, ADAM_EPS, ADAM_WD, ADAM_STEP = 0.001, 0.9, 0.999, 1e-08, 0.01, 10

N_DEV = 8
LANE = 128
NCH = W_MIX // LANE
TM_CHUNK = 1024
VMEM_LIMIT = 56 * 1024 * 1024
MESH = pl.DeviceIdType.MESH
ANY = pl.BlockSpec(memory_space=pl.ANY)
HIGHEST = lax.Precision.HIGHEST

C_UA, C_GA, C_XB, C_GB, C_Q, C_F, C_I, C_GC, C_GATE = (W_MIX * k for k in range(9))


def _cparams(sem=None):
    return pltpu.CompilerParams(dimension_semantics=sem, vmem_limit_bytes=VMEM_LIMIT)


@jax.custom_vjp
def bdot(a, w):
    return jnp.dot(a.astype(bf16), w.astype(bf16), preferred_element_type=f32)


def _bdot_fwd(a, w):
    return bdot(a, w), (a, w)


def _bdot_bwd(res, g):
    a, w = res
    gb = g.astype(bf16)
    da = lax.dot_general(gb, w.astype(bf16), (((1,), (1,)), ((), ())), preferred_element_type=f32)
    dw = lax.dot_general(a.astype(bf16), gb, (((0,), (0,)), ((), ())), preferred_element_type=f32)
    return da, dw


bdot.defvjp(_bdot_fwd, _bdot_bwd)


def _blockmm(c, a):
    n = c.shape[0]
    return jnp.concatenate([jnp.dot(c, a[i:i + n], preferred_element_type=f32, precision=HIGHEST)
                            for i in range(0, a.shape[0], n)], axis=0)


@jax.custom_vjp
def cdot(c, ct, a):
    return _blockmm(c, a)


def _cdot_fwd(c, ct, a):
    return cdot(c, ct, a), (c, ct)


def _cdot_bwd(res, g):
    c, ct = res
    return jnp.zeros_like(c), jnp.zeros_like(ct), _blockmm(ct, g)


cdot.defvjp(_cdot_fwd, _cdot_bwd)


def mm(a, b, *, name, out_dtype=f32, add=None, after=None, at=False, bt=False, tm=1024, tn=1024, tk=4096):
    k, m = a.shape if at else a.shape[::-1]
    n = b.shape[0] if bt else b.shape[1]
    tm, tn, tk = min(tm, m), min(tn, n), min(tk, k)
    assert m % tm == 0 and n % tn == 0 and k % tk == 0
    nk = k // tk
    dims = (((0 if at else 1,), (1 if bt else 0,)), ((), ()))

    def body(*refs):
        a_ref, b_ref = refs[:2]
        r_ref = refs[2] if add is not None else None
        o_ref = refs[-1] if nk == 1 else refs[-2]
        part = lax.dot_general(a_ref[...], b_ref[...], dims, preferred_element_type=f32)
        if nk == 1:
            if add is not None:
                part = part + r_ref[...]
            o_ref[...] = part.astype(out_dtype)
            return
        acc_ref = refs[-1]
        kk = pl.program_id(2)

        @pl.when(kk == 0)
        def _():
            acc_ref[...] = part

        @pl.when(kk > 0)
        def _():
            acc_ref[...] = acc_ref[...] + part

        @pl.when(kk == nk - 1)
        def _():
            acc = acc_ref[...]
            if add is not None:
                acc = acc + r_ref[...]
            o_ref[...] = acc.astype(out_dtype)

    b_spec = pl.BlockSpec((tn, tk), lambda i, j, q: (j, q)) if bt else pl.BlockSpec((tk, tn), lambda i, j, q: (q, j))
    a_spec = pl.BlockSpec((tk, tm), lambda i, j, q: (q, i)) if at else pl.BlockSpec((tm, tk), lambda i, j, q: (i, q))
    in_specs = [a_spec, b_spec]
    args = [a, b]
    if add is not None:
        in_specs.append(pl.BlockSpec((tm, tn), lambda i, j, q: (i, j)))
        args.append(add)
    if after is not None:
        in_specs.append(pl.BlockSpec(after.shape, lambda i, j, q: (0, 0)))
        args.append(after)
    return pl.pallas_call(
        body, name=name, grid=(m // tm, n // tn, nk), in_specs=in_specs,
        out_specs=pl.BlockSpec((tm, tn), lambda i, j, q: (i, j)),
        out_shape=jax.ShapeDtypeStruct((m, n), out_dtype),
        scratch_shapes=[] if nk == 1 else [pltpu.VMEM((tm, tn), f32)],
        compiler_params=_cparams(("parallel", "parallel", "arbitrary")),
    )(*args)


def _row_spec(tm, wc, col_off):
    base = col_off // wc
    assert col_off % wc == 0
    return pl.BlockSpec((tm, wc), lambda j, i: (i, base + j))


def _slab_spec(arr):
    r, c = arr.shape[1:]
    if arr.shape[0] == 1:
        return pl.BlockSpec((1, r, c), lambda j, i: (0, 0, 0))
    return pl.BlockSpec((1, r, c), lambda j, i: (j, 0, 0))


def rowwise(fn, rows, params, consts, outs, *, name, tm=256, ncol=1, rowid=False):
    t = rows[0][0].shape[0]
    tm = min(tm, t)
    nr, npar, nc, no = len(rows), len(params), len(consts), len(outs)

    def body(*refs):
        r = [refs[k][...].astype(f32) for k in range(nr)]
        p = [refs[nr + k][0] for k in range(npar + nc)]
        extra = ()
        if rowid:
            extra = (pl.program_id(1) * tm + lax.broadcasted_iota(jnp.int32, (tm, 1), 0),)
        res = fn(*extra, *r, *p)
        for k in range(no):
            refs[nr + npar + nc + k][...] = res[k].astype(outs[k][1])

    in_specs = [_row_spec(tm, w // ncol, off) for (_, off, w) in rows]
    in_specs += [_slab_spec(a) for a in list(params) + list(consts)]
    out_specs = [pl.BlockSpec((tm, w // ncol), lambda j, i: (i, j)) for (w, _) in outs]
    out_shape = [jax.ShapeDtypeStruct((t, w), dt) for (w, dt) in outs]
    return pl.pallas_call(
        body, name=name, grid=(ncol, t // tm), in_specs=in_specs, out_specs=out_specs, out_shape=out_shape,
        compiler_params=_cparams(("parallel", "parallel")),
    )(*[r[0] for r in rows], *params, *consts)


def rowwise_vjp(fn, rows, params, consts, cts, d_rows, *, name, tm=256, ncol=1, rowid=False, sum_primal=None):
    t = rows[0][0].shape[0]
    tm = min(tm, t)
    nr, npar, nc, nct, ndr = len(rows), len(params), len(consts), len(cts), len(d_rows)

    def body(*refs):
        i = pl.program_id(1)
        r = [refs[k][...].astype(f32) for k in range(nr)]
        p = [refs[nr + k][0] for k in range(npar)]
        c = [refs[nr + npar + k][0] for k in range(nc)]
        g = [refs[nr + npar + nc + k][...].astype(f32) for k in range(nct)]
        orefs = refs[nr + npar + nc + nct:]
        extra = ()
        if rowid:
            extra = (i * tm + lax.broadcasted_iota(jnp.int32, (tm, 1), 0),)
        res, vjp = jax.vjp(lambda *v: fn(*extra, *v, *c), *r, *p)
        grads = vjp(tuple(g))
        for k, (idx, dt) in enumerate(d_rows):
            orefs[k][...] = grads[idx].astype(dt)
        acc = [grads[nr + k] for k in range(npar)]
        if sum_primal is not None:
            acc.append(jnp.sum(res[sum_primal], axis=0, keepdims=True))

        @pl.when(i == 0)
        def _():
            for k, a in enumerate(acc):
                orefs[ndr + k][0] = a

        @pl.when(i > 0)
        def _():
            for k, a in enumerate(acc):
                orefs[ndr + k][0] = orefs[ndr + k][0] + a

    in_specs = [_row_spec(tm, w // ncol, off) for (_, off, w) in rows]
    in_specs += [_slab_spec(a) for a in list(params) + list(consts)]
    in_specs += [_row_spec(tm, w // ncol, off) for (_, off, w) in cts]
    out_specs = [pl.BlockSpec((tm, rows[idx][2] // ncol), lambda j, i: (i, j)) for (idx, _) in d_rows]
    out_shape = [jax.ShapeDtypeStruct((t, rows[idx][2]), dt) for (idx, dt) in d_rows]
    for a in params:
        out_specs.append(pl.BlockSpec((1,) + a.shape[1:], lambda j, i: (j, 0, 0)))
        out_shape.append(jax.ShapeDtypeStruct(a.shape, f32))
    if sum_primal is not None:
        w = cts[sum_primal][2]
        out_specs.append(pl.BlockSpec((1, 1, w // ncol), lambda j, i: (j, 0, 0)))
        out_shape.append(jax.ShapeDtypeStruct((ncol, 1, w // ncol), f32))
    return pl.pallas_call(
        body, name=name, grid=(ncol, t // tm), in_specs=in_specs, out_specs=out_specs, out_shape=out_shape,
        compiler_params=_cparams(("parallel", "arbitrary")),
    )(*[r[0] for r in rows], *params, *consts, *[c[0] for c in cts])


VM = pl.BlockSpec(memory_space=pltpu.VMEM)


def whole(fn, ins, outs, *, name):
    def body(*refs):
        res = fn(*[r[...] for r in refs[:len(ins)]])
        for k, o in enumerate(refs[len(ins):]):
            o[...] = res[k]
    return pl.pallas_call(body, name=name, in_specs=[VM] * len(ins), out_specs=[VM] * len(outs),
                          out_shape=[jax.ShapeDtypeStruct(s, f32) for s in outs],
                          compiler_params=_cparams())(*ins)


def whole_vjp(fn, ins, cts, *, name):
    n = len(ins)

    def body(*refs):
        _, vjp = jax.vjp(fn, *[r[...] for r in refs[:n]])
        grads = vjp(tuple(r[...] for r in refs[n:n + len(cts)]))
        for k, o in enumerate(refs[n + len(cts):]):
            o[...] = grads[k]
    return pl.pallas_call(body, name=name, in_specs=[VM] * (n + len(cts)), out_specs=[VM] * n,
                          out_shape=[jax.ShapeDtypeStruct(a.shape, f32) for a in ins],
                          compiler_params=_cparams())(*ins, *cts)


def ln_fn(x, w):
    return (x * lax.rsqrt(jnp.mean(x * x, axis=-1, keepdims=True) + EPS) * w,)


def ln_res_fn(x, w):
    return ln_fn(x, w)[0], x


def loss_fn(x, tgt, w):
    y = ln_fn(x, w)[0]
    return (0.5 * jnp.mean(jnp.square(y - tgt), axis=-1, keepdims=True),)


def s5_prep_fn(lam_re, lam_im, log_step, b_re, b_im):
    step = jnp.exp(log_step)
    mag = jnp.exp(lam_re * step)
    ang = lam_im * step
    abar_re = mag * jnp.cos(ang)
    abar_im = mag * jnp.sin(ang)
    num_re = abar_re - 1.0
    num_im = abar_im
    den = lam_re * lam_re + lam_im * lam_im
    coef_re = (num_re * lam_re + num_im * lam_im) / den
    coef_im = (num_im * lam_re - num_re * lam_im) / den
    bbar_re = coef_re * b_re - coef_im * b_im
    bbar_im = coef_re * b_im + coef_im * b_re
    return abar_re, abar_im, bbar_re, bbar_im


def lb_prep_fn(r0, r1, r2, r3):
    m = jnp.maximum(jnp.maximum(r0, r1), jnp.maximum(r2, r3))
    e0, e1, e2, e3 = jnp.exp(r0 - m), jnp.exp(r1 - m), jnp.exp(r2 - m), jnp.exp(r3 - m)
    s = e0 + e1 + e2 + e3
    p0, p1, p2, p3 = e0 / s, e1 / s, e2 / s, e3 / s
    c1 = p0 + p1
    c2 = c1 + p2
    c3 = c2 + p3
    return p0 - p0, c1 - p0, c2 - p0, c3 - p0


def s5_post_fn(yssm, u, ga, d, wglu, bglu):
    y = jax.nn.gelu(yssm + d * u)
    y = y * jax.nn.sigmoid(bdot(y, wglu) + bglu)
    return (y * jax.nn.silu(ga),)


def rg_gate_fn(tglob, xc, wa, ba, wx, bx, lam):
    r = jax.nn.sigmoid(bdot(xc, wa) + ba)
    i = jax.nn.sigmoid(bdot(xc, wx) + bx)
    log_a = -RG_C * r * jax.nn.softplus(-lam)
    a = jnp.exp(log_a)
    mult = jnp.sqrt(-jnp.tanh(log_a) * (a * a + 1.0))
    mult = jnp.where(tglob == 0, 1.0, mult)
    return a, mult * (i * xc)


def hg_pre_fn(q, fl, lb, tri, tri_t):
    f = lb + (1.0 - lb) * jax.nn.sigmoid(fl)
    return jax.nn.silu(q), 1.0 - f, cdot(tri, tri_t, jnp.log(f))


def branch_prep_fn(hb, gb, oc, gc, nw):
    yb = hb * jax.nn.silu(gb)
    on = oc * lax.rsqrt(jnp.mean(oc * oc, axis=-1, keepdims=True) + EPS) * nw
    return yb, on * jax.nn.silu(gc)


def merge_fn(b0, b1, b2, g0, g1, g2):
    return (jax.nn.sigmoid(g0) * b0 + jax.nn.sigmoid(g1) * b1 + jax.nn.sigmoid(g2) * b2,)


S5_TB = 512
S5_SC = 512


SEG = 8


def _shift(v, k, pos, period, reverse, fill):
    if reverse:
        return jnp.where(pos < period - k, pltpu.roll(v, v.shape[0] - k, 0), fill)
    return jnp.where(pos >= k, pltpu.roll(v, k, 0), fill)


def _cmul(ar, ai, br, bi):
    return ar * br - ai * bi, ar * bi + ai * br


def _edge_rows(v, first):
    r0 = 0 if first else SEG - 1
    return jnp.concatenate([v[r:r + 1, :] for r in range(r0, v.shape[0], SEG)], axis=0)


def _spread(s):
    return jnp.concatenate([jnp.broadcast_to(s[g:g + 1, :], (SEG, s.shape[1])) for g in range(s.shape[0])], axis=0)


def lti_scan(xr, xi, ar, ai, cr, ci, reverse=False):
    n = xr.shape[0]
    g = n // SEG
    sub = lax.broadcasted_iota(jnp.int32, (n, 1), 0) & (SEG - 1)
    sub8 = lax.broadcasted_iota(jnp.int32, (SEG, 1), 0)
    grow = lax.broadcasted_iota(jnp.int32, (g, 1), 0)
    pr, pi_ = ar, ai
    wr, wi = jnp.broadcast_to(ar, (SEG, ar.shape[1])), jnp.broadcast_to(ai, (SEG, ai.shape[1]))
    k = 1
    while k < SEG:
        tr, ti = _cmul(pr, pi_, _shift(xr, k, sub, SEG, reverse, 0.0), _shift(xi, k, sub, SEG, reverse, 0.0))
        xr, xi = xr + tr, xi + ti
        pr, pi_ = _cmul(pr, pi_, pr, pi_)
        wr, wi = _cmul(wr, wi, _shift(wr, k, sub8, SEG, reverse, 1.0), _shift(wi, k, sub8, SEG, reverse, 0.0))
        k *= 2
    first, last = (g - 1, 0) if reverse else (0, g - 1)
    jr, ji = _cmul(pr, pi_, cr, ci)
    sr = _edge_rows(xr, reverse) + jnp.where(grow == first, jr, 0.0)
    si = _edge_rows(xi, reverse) + jnp.where(grow == first, ji, 0.0)
    k = 1
    while k < g:
        tr, ti = _cmul(pr, pi_, _shift(sr, k, grow, g, reverse, 0.0), _shift(si, k, grow, g, reverse, 0.0))
        sr, si = sr + tr, si + ti
        pr, pi_ = _cmul(pr, pi_, pr, pi_)
        k *= 2
    er, ei = _spread(_shift(sr, 1, grow, g, reverse, cr)), _spread(_shift(si, 1, grow, g, reverse, ci))
    tr, ti = _cmul(jnp.tile(wr, (g, 1)), jnp.tile(wi, (g, 1)), er, ei)
    return xr + tr, xi + ti, sr[last:last + 1, :], si[last:last + 1, :]


def tv_scan(aa, bb, carry, reverse=False):
    n = aa.shape[0]
    row = lax.broadcasted_iota(jnp.int32, (n, 1), 0)
    k = 1
    while k < n:
        bb = bb + aa * _shift(bb, k, row, n, reverse, 0.0)
        aa = aa * _shift(aa, k, row, n, reverse, 1.0)
        k *= 2
    h = bb + aa * carry
    last = 0 if reverse else n - 1
    return h, h[last:last + 1, :]


def s5_scan_fwd(z, bd_re, bd_im, cd_re, cd_im, a_re, a_im):
    t = z.shape[0]
    tb = min(S5_TB, t)

    def body(u_ref, bre, bim, cre, cim, are, aim, y_ref, xre_ref, xim_ref, car_re, car_im):
        @pl.when(pl.program_id(1) == 0)
        def _():
            car_re[...] = jnp.zeros_like(car_re)
            car_im[...] = jnp.zeros_like(car_im)

        u = u_ref[...].astype(bf16)
        xr, xi, car_re[...], car_im[...] = lti_scan(
            jnp.dot(u, bre[0], preferred_element_type=f32), jnp.dot(u, bim[0], preferred_element_type=f32),
            are[0], aim[0], car_re[...], car_im[...])
        xre_ref[...] = xr
        xim_ref[...] = xi
        y_ref[...] = (jnp.dot(xr.astype(bf16), cre[0], preferred_element_type=f32)
                      - jnp.dot(xi.astype(bf16), cim[0], preferred_element_type=f32))

    chunk = lambda r, c: pl.BlockSpec((1, r, c), lambda j, i: (j, 0, 0))
    return pl.pallas_call(
        body, name="s5_scan_fwd", grid=(NCH, t // tb),
        in_specs=[pl.BlockSpec((tb, LANE), lambda j, i: (i, C_UA // LANE + j)),
                  chunk(LANE, S5_SC), chunk(LANE, S5_SC), chunk(S5_SC, LANE), chunk(S5_SC, LANE),
                  chunk(1, S5_SC), chunk(1, S5_SC)],
        out_specs=[pl.BlockSpec((tb, LANE), lambda j, i: (i, j)),
                   pl.BlockSpec((tb, S5_SC), lambda j, i: (i, j)),
                   pl.BlockSpec((tb, S5_SC), lambda j, i: (i, j))],
        out_shape=[jax.ShapeDtypeStruct((t, W_MIX), f32),
                   jax.ShapeDtypeStruct((t, NCH * S5_SC), f32),
                   jax.ShapeDtypeStruct((t, NCH * S5_SC), f32)],
        scratch_shapes=[pltpu.VMEM((1, S5_SC), f32)] * 2,
        compiler_params=_cparams(("parallel", "arbitrary")),
    )(z, bd_re.astype(bf16), bd_im.astype(bf16), cd_re.astype(bf16), cd_im.astype(bf16), a_re, a_im)


def s5_scan_bwd(dy, du1, z, xre, xim, bd_re, bd_im, cd_re, cd_im, a_re, a_im):
    t = z.shape[0]
    tb = min(S5_TB, t)
    nt = t // tb

    def body(dy_ref, du1_ref, u_ref, xre_ref, xim_ref, hre_ref, him_ref, bre, bim, cre, cim, are, aim,
             du_ref, dbre, dbim, dcre, dcim, dare, daim, car_re, car_im):
        step = pl.program_id(1)
        tt = nt - 1 - step

        @pl.when(step == 0)
        def _():
            car_re[...] = jnp.zeros_like(car_re)
            car_im[...] = jnp.zeros_like(car_im)

        nt_dims = (((1,), (1,)), ((), ()))
        tn_dims = (((0,), (0,)), ((), ()))
        dyb = dy_ref[...].astype(bf16)
        row = lax.broadcasted_iota(jnp.int32, (tb, 1), 0)
        xr, xi = xre_ref[...], xim_ref[...]
        ar, ai = are[0], aim[0]
        lr, li, car_re[...], car_im[...] = lti_scan(
            lax.dot_general(dyb, cre[0], nt_dims, preferred_element_type=f32),
            -lax.dot_general(dyb, cim[0], nt_dims, preferred_element_type=f32),
            ar, -ai, car_re[...], car_im[...], reverse=True)
        lrb, lib = lr.astype(bf16), li.astype(bf16)
        ub = u_ref[...].astype(bf16)
        du = (lax.dot_general(lrb, bre[0], nt_dims, preferred_element_type=f32)
              + lax.dot_general(lib, bim[0], nt_dims, preferred_element_type=f32))
        du_ref[...] = (du + du1_ref[...].astype(f32)).astype(du_ref.dtype)
        live = (tt > 0).astype(f32)
        xpr = jnp.where(row == 0, hre_ref[7:8, :] * live, pltpu.roll(xr, 1, 0))
        xpi = jnp.where(row == 0, him_ref[7:8, :] * live, pltpu.roll(xi, 1, 0))
        acc = [
            lax.dot_general(ub, lrb, tn_dims, preferred_element_type=f32),
            lax.dot_general(ub, lib, tn_dims, preferred_element_type=f32),
            lax.dot_general(xr.astype(bf16), dyb, tn_dims, preferred_element_type=f32),
            -lax.dot_general(xi.astype(bf16), dyb, tn_dims, preferred_element_type=f32),
            jnp.sum(lr * xpr + li * xpi, axis=0, keepdims=True),
            jnp.sum(li * xpr - lr * xpi, axis=0, keepdims=True),
        ]
        outs = [dbre, dbim, dcre, dcim, dare, daim]

        @pl.when(step == 0)
        def _():
            for o, a in zip(outs, acc):
                o[0] = a

        @pl.when(step > 0)
        def _():
            for o, a in zip(outs, acc):
                o[0] = o[0] + a

    chunk = lambda r, c: pl.BlockSpec((1, r, c), lambda j, i: (j, 0, 0))
    rev = lambda w, base=0: pl.BlockSpec((tb, w), lambda j, i: (nt - 1 - i, base + j))
    halo = pl.BlockSpec((8, S5_SC), lambda j, i: (jnp.maximum((nt - 1 - i) * (tb // 8) - 1, 0), j))
    return pl.pallas_call(
        body, name="s5_scan_bwd", grid=(NCH, nt),
        in_specs=[rev(LANE), rev(LANE), rev(LANE, C_UA // LANE), rev(S5_SC), rev(S5_SC), halo, halo,
                  chunk(LANE, S5_SC), chunk(LANE, S5_SC), chunk(S5_SC, LANE), chunk(S5_SC, LANE),
                  chunk(1, S5_SC), chunk(1, S5_SC)],
        out_specs=[rev(LANE), chunk(LANE, S5_SC), chunk(LANE, S5_SC), chunk(S5_SC, LANE), chunk(S5_SC, LANE),
                   chunk(1, S5_SC), chunk(1, S5_SC)],
        out_shape=[jax.ShapeDtypeStruct((t, W_MIX), bf16),
                   jax.ShapeDtypeStruct((NCH, LANE, S5_SC), f32), jax.ShapeDtypeStruct((NCH, LANE, S5_SC), f32),
                   jax.ShapeDtypeStruct((NCH, S5_SC, LANE), f32), jax.ShapeDtypeStruct((NCH, S5_SC, LANE), f32),
                   jax.ShapeDtypeStruct((NCH, 1, S5_SC), f32), jax.ShapeDtypeStruct((NCH, 1, S5_SC), f32)],
        scratch_shapes=[pltpu.VMEM((1, S5_SC), f32)] * 2,
        compiler_params=_cparams(("parallel", "arbitrary")),
    )(dy, du1, z, xre, xim, xre, xim, bd_re.astype(bf16), bd_im.astype(bf16), cd_re.astype(bf16),
      cd_im.astype(bf16), a_re, a_im)


RG_TB = 512


def _rg_conv(ext, w, cb, tb):
    acc = cb + w[3:4, :] * ext[8:, :]
    for k in range(3):
        acc = acc + w[k:k + 1, :] * pltpu.roll(ext, 3 - k, 0)[8:, :]
    return acc


def rg_fwd(z, cw, cb, gate_p):
    t = z.shape[0]
    tb = min(RG_TB, t)

    def body(x_ref, halo_ref, cw_ref, cb_ref, wa, ba, wx, bx, lam, xc_ref, a_ref, h_ref, car):
        i = pl.program_id(1)

        @pl.when(i == 0)
        def _():
            car[...] = jnp.zeros_like(car)

        ext = jnp.concatenate([halo_ref[...] * (i > 0).astype(f32), x_ref[...]], axis=0)
        xc = _rg_conv(ext, cw_ref[0], cb_ref[0], tb)
        tglob = i * tb + lax.broadcasted_iota(jnp.int32, (tb, 1), 0)
        a, b = rg_gate_fn(tglob, xc, wa[0], ba[0], wx[0], bx[0], lam[0])
        xc_ref[...] = xc
        a_ref[...] = a
        h_ref[...], car[...] = tv_scan(a, b, car[...])

    base = C_XB // LANE
    out = pl.BlockSpec((tb, LANE), lambda j, i: (i, j))
    return pl.pallas_call(
        body, name="rg_fwd", grid=(NCH, t // tb),
        in_specs=[pl.BlockSpec((tb, LANE), lambda j, i: (i, base + j)),
                  pl.BlockSpec((8, LANE), lambda j, i: (jnp.maximum(i * (tb // 8) - 1, 0), base + j)),
                  _slab_spec(cw), _slab_spec(cb)] + [_slab_spec(p) for p in gate_p],
        out_specs=[out, out, out], out_shape=[jax.ShapeDtypeStruct((t, W_MIX), f32)] * 3,
        scratch_shapes=[pltpu.VMEM((1, LANE), f32)],
        compiler_params=_cparams(("parallel", "arbitrary")),
    )(z, z, cw, cb, *gate_p)


def rg_bwd(dh, z, xc, a, h, cw, gate_p):
    t = z.shape[0]
    tb = min(RG_TB, t)
    nt = t // tb

    def body(g_ref, a_ref, an_ref, h_ref, hp_ref, xc_ref, x_ref, xh_ref, cw_ref, wa, ba, wx, bx, lam,
             dx_ref, dcw_ref, dcb_ref, dwa, dba, dwx, dbx, dlam, car, later):
        step = pl.program_id(1)
        tt = nt - 1 - step

        @pl.when(step == 0)
        def _():
            car[...] = jnp.zeros_like(car)
            later[...] = jnp.zeros_like(later)

        row = lax.broadcasted_iota(jnp.int32, (tb, 1), 0)
        an = an_ref[0:1, :] * (tt < nt - 1).astype(f32)
        aa = jnp.where(row == tb - 1, an, pltpu.roll(a_ref[...], tb - 1, 0))
        lmb, car[...] = tv_scan(aa, g_ref[...], car[...], reverse=True)
        hp = jnp.where(row == 0, hp_ref[7:8, :] * (tt > 0).astype(f32), pltpu.roll(h_ref[...], 1, 0))
        tglob = tt * tb + row
        _, vjp = jax.vjp(lambda *v: rg_gate_fn(tglob, *v), xc_ref[...], wa[0], ba[0], wx[0], bx[0], lam[0])
        g, *dp = vjp((lmb * hp, lmb))
        gext = jnp.concatenate([g, later[...]], axis=0)
        later[...] = g[0:8, :]
        xext = jnp.concatenate([xh_ref[...] * (tt > 0).astype(f32), x_ref[...]], axis=0)
        w = cw_ref[0]
        dx = w[3:4, :] * g
        taps = [None] * RG_CONV
        taps[3] = jnp.sum(g * xext[8:, :], axis=0, keepdims=True)
        for k in range(3):
            s = 3 - k
            dx = dx + w[k:k + 1, :] * pltpu.roll(gext, tb + 8 - s, 0)[:tb, :]
            taps[k] = jnp.sum(g * pltpu.roll(xext, s, 0)[8:, :], axis=0, keepdims=True)
        dx_ref[...] = dx.astype(dx_ref.dtype)
        acc = [jnp.concatenate(taps, axis=0), jnp.sum(g, axis=0, keepdims=True)] + dp
        outs = [dcw_ref, dcb_ref, dwa, dba, dwx, dbx, dlam]

        @pl.when(step == 0)
        def _():
            for o, v in zip(outs, acc):
                o[0] = v

        @pl.when(step > 0)
        def _():
            for o, v in zip(outs, acc):
                o[0] = o[0] + v

    base = C_XB // LANE
    rev = lambda b=0: pl.BlockSpec((tb, LANE), lambda j, i: (nt - 1 - i, b + j))
    nxt = pl.BlockSpec((8, LANE), lambda j, i: (jnp.minimum((nt - i) * (tb // 8), t // 8 - 1), j))
    prv = lambda b=0: pl.BlockSpec((8, LANE), lambda j, i: (jnp.maximum((nt - 1 - i) * (tb // 8) - 1, 0), b + j))
    params = [cw] + list(gate_p)
    grads = [jax.ShapeDtypeStruct(s, f32) for s in [cw.shape, (NCH, 1, LANE)] + [p.shape for p in gate_p]]
    return pl.pallas_call(
        body, name="rg_bwd", grid=(NCH, nt),
        in_specs=[rev(), rev(), nxt, rev(), prv(), rev(), rev(base), prv(base)] + [_slab_spec(p) for p in params],
        out_specs=[rev()] + [_slab_spec(p) for p in grads], out_shape=[jax.ShapeDtypeStruct((t, W_MIX), bf16)] + grads,
        scratch_shapes=[pltpu.VMEM((1, LANE), f32), pltpu.VMEM((8, LANE), f32)],
        compiler_params=_cparams(("parallel", "arbitrary")),
    )(dh, a, a, h, h, xc, z, z, *params)


HG_TB = 256
HALF = HG_SUB // 2


def _heads(v):
    return jnp.stack([v[:, LANE * h:LANE * (h + 1)] for h in range(HG_HEADS)])


def _unheads(v):
    return jnp.concatenate([v[h] for h in range(HG_HEADS)], axis=-1)


def _bmm(eq, a, b):
    return jnp.einsum(eq, a.astype(bf16), b.astype(bf16), preferred_element_type=f32)


def hg_chunk_fwd(qs, kk, gcum, z):
    t = qs.shape[0]
    tb = min(HG_TB, t)
    nc = tb // HG_SUB

    def body(q_ref, k_ref, g_ref, v_ref, o_ref, sall_ref, st_ref):
        @pl.when(pl.program_id(0) == 0)
        def _():
            st_ref[...] = jnp.zeros_like(st_ref)

        ri = lax.broadcasted_iota(jnp.int32, (1, HALF, 1), 1)

        def chunk(c, carry):
            rows = pl.ds(pl.multiple_of(c * HG_SUB, HG_SUB), HG_SUB)
            q, k, g, v = _heads(q_ref[rows, :]), _heads(k_ref[rows, :]), _heads(g_ref[rows, :]), _heads(v_ref[rows, :])
            st = st_ref[...]
            sall_ref[c] = st
            o = _bmm('htk,hvk->htv', q * jnp.exp(g), st)
            halves = [[q[:, :HALF], g[:, :HALF], o[:, :HALF]], [q[:, HALF:], g[:, HALF:], o[:, HALF:]]]
            for s in range(HG_SUB):
                grow, krow, vrow = g[:, s:s + 1, :], k[:, s:s + 1, :], v[:, s:s + 1, :]
                for h in range(s // HALF, 2):
                    qh, gh, oh = halves[h]
                    p = jnp.exp(jnp.minimum(gh - grow, 0.0))
                    if s // HALF == h:
                        p = jnp.where(ri >= s - h * HALF, p, 0.0)
                    halves[h][2] = oh + jnp.sum(qh * krow * p, axis=-1, keepdims=True) * vrow
            o = jnp.concatenate([halves[0][2], halves[1][2]], axis=1)
            gl = g[:, HG_SUB - 1:HG_SUB, :]
            st_ref[...] = st * jnp.exp(gl) + _bmm('htv,htk->hvk', v, k * jnp.exp(gl - g))
            o_ref[rows, :] = _unheads(o)
            return carry

        lax.fori_loop(0, nc, chunk, 0)

    spec = lambda base=0: pl.BlockSpec((tb, W_MIX), lambda i: (i, base))
    return pl.pallas_call(
        body, name="hg_chunk_fwd", grid=(t // tb,),
        in_specs=[spec(), spec(), spec(), spec(C_I // W_MIX)],
        out_specs=[spec(), pl.BlockSpec((nc, HG_HEADS, HG_DK, HG_DK), lambda i: (i, 0, 0, 0))],
        out_shape=[jax.ShapeDtypeStruct((t, W_MIX), f32),
                   jax.ShapeDtypeStruct((t // HG_SUB, HG_HEADS, HG_DK, HG_DK), f32)],
        scratch_shapes=[pltpu.VMEM((HG_HEADS, HG_DK, HG_DK), f32)],
        compiler_params=_cparams(("arbitrary",)),
    )(qs, kk, gcum, z)


def hg_chunk_bwd(do, qs, kk, gcum, z, sall):
    t = qs.shape[0]
    tb = min(HG_TB, t)
    nc = tb // HG_SUB
    nt = t // tb

    def body(do_ref, q_ref, k_ref, g_ref, v_ref, sall_ref, dq_ref, dk_ref, dg_ref, dv_ref, dst_ref):
        @pl.when(pl.program_id(0) == 0)
        def _():
            dst_ref[...] = jnp.zeros_like(dst_ref)

        ri = lax.broadcasted_iota(jnp.int32, (1, HALF, 1), 1)
        ri_chunk = lax.broadcasted_iota(jnp.int32, (1, HG_SUB, 1), 1)

        def chunk(cc, carry):
            c = nc - 1 - cc
            rows = pl.ds(pl.multiple_of(c * HG_SUB, HG_SUB), HG_SUB)
            q, k, g, v = _heads(q_ref[rows, :]), _heads(k_ref[rows, :]), _heads(g_ref[rows, :]), _heads(v_ref[rows, :])
            d_o = _heads(do_ref[rows, :])
            st = sall_ref[c]
            dsn = dst_ref[...]
            eg = jnp.exp(g)
            qe = q * eg
            gl = g[:, HG_SUB - 1:HG_SUB, :]
            egl = jnp.exp(gl)
            dec = jnp.exp(gl - g)
            kd = k * dec
            dqe = _bmm('htv,hvk->htk', d_o, st)
            dst_ref[...] = _bmm('htv,htk->hvk', d_o, qe) + dsn * egl
            dgl_dec = jnp.sum(dsn * st, axis=1, keepdims=True) * egl
            dv = _bmm('htk,hvk->htv', kd, dsn)
            dkd = _bmm('htv,hvk->htk', v, dsn)
            tq, tg, tdo = [q[:, :HALF], q[:, HALF:]], [g[:, :HALF], g[:, HALF:]], [d_o[:, :HALF], d_o[:, HALF:]]
            a1 = [jnp.zeros_like(tq[0]), jnp.zeros_like(tq[1])]
            a2 = [jnp.zeros_like(tq[0]), jnp.zeros_like(tq[1])]
            dvh = [dv[:, :HALF], dv[:, HALF:]]
            for s in range(HG_SUB):
                grow, krow, vrow = g[:, s:s + 1, :], k[:, s:s + 1, :], v[:, s:s + 1, :]
                sh, sr = s // HALF, s % HALF
                dv_s, a2_s = 0.0, 0.0
                for h in range(sh, 2):
                    p = jnp.exp(jnp.minimum(tg[h] - grow, 0.0))
                    if sh == h:
                        p = jnp.where(ri >= sr, p, 0.0)
                    col = jnp.sum(tq[h] * krow * p, axis=-1, keepdims=True)
                    t1 = jnp.sum(tdo[h] * vrow, axis=-1, keepdims=True) * p
                    a1[h] = a1[h] + t1 * krow
                    dv_s = dv_s + jnp.sum(col * tdo[h], axis=1, keepdims=True)
                    a2_s = a2_s + jnp.sum(t1 * tq[h], axis=1, keepdims=True)
                dvh[sh] = jnp.where(ri == sr, dvh[sh] + dv_s, dvh[sh])
                a2[sh] = jnp.where(ri == sr, a2_s, a2[sh])
            a1, a2 = jnp.concatenate(a1, axis=1), jnp.concatenate(a2, axis=1)
            dv = jnp.concatenate(dvh, axis=1)
            dgl = jnp.sum(dkd * kd, axis=1, keepdims=True) + dgl_dec
            dg = dqe * qe + q * a1 - k * a2 - dkd * kd
            dg = jnp.where(ri_chunk == HG_SUB - 1, dg + dgl, dg)
            dq_ref[rows, :] = _unheads(dqe * eg + a1)
            dk_ref[rows, :] = _unheads(dkd * dec + a2)
            dg_ref[rows, :] = _unheads(dg)
            dv_ref[rows, :] = _unheads(dv).astype(dv_ref.dtype)
            return carry

        lax.fori_loop(0, nc, chunk, 0)

    spec = lambda base=0: pl.BlockSpec((tb, W_MIX), lambda i: (nt - 1 - i, base))
    return pl.pallas_call(
        body, name="hg_chunk_bwd", grid=(nt,),
        in_specs=[spec(), spec(), spec(), spec(), spec(C_I // W_MIX),
                  pl.BlockSpec((nc, HG_HEADS, HG_DK, HG_DK), lambda i: (nt - 1 - i, 0, 0, 0))],
        out_specs=[spec(), spec(), spec(), spec()],
        out_shape=[jax.ShapeDtypeStruct((t, W_MIX), f32)] * 3 + [jax.ShapeDtypeStruct((t, W_MIX), bf16)],
        scratch_shapes=[pltpu.VMEM((HG_HEADS, HG_DK, HG_DK), f32)],
        compiler_params=_cparams(("arbitrary",)),
    )(do, qs, kk, gcum, z, sall)


def adamw(w, m, v, slots, *, name, tr):
    nl, r, c = w.shape
    tr = min(tr, r)
    flat = [a for per_layer in slots for a in per_layer]
    c1 = 1.0 / (1.0 - ADAM_B1 ** ADAM_STEP)
    c2 = 1.0 / (1.0 - ADAM_B2 ** ADAM_STEP)

    def body(*refs):
        w_ref, m_ref, v_ref = refs[:3]
        s_refs = list(refs[3:3 + len(flat)])
        g_ref, d_ref, mo_ref, vo_ref = refs[3 + len(flat):]
        for l in range(nl):
            parts = [s_refs.pop(0) for _ in slots[l]]
            g = None
            for p in parts:
                for s in range(p.shape[0]):
                    term = p[s].astype(f32)
                    g = term if g is None else g + term
            mn = ADAM_B1 * m_ref[l] + (1.0 - ADAM_B1) * g
            vn = ADAM_B2 * v_ref[l] + (1.0 - ADAM_B2) * (g * g)
            g_ref[l] = g
            mo_ref[l] = mn
            vo_ref[l] = vn
            d_ref[l] = -ADAM_LR * ((mn * c1) / (jnp.sqrt(vn * c2) + ADAM_EPS) + ADAM_WD * w_ref[l])

    full = pl.BlockSpec((nl, tr, c), lambda i: (0, i, 0))
    slot = [pl.BlockSpec((a.shape[0], tr, c), lambda i: (0, i, 0)) for a in flat]
    return pl.pallas_call(
        body, name=name, grid=(r // tr,), in_specs=[full] * 3 + slot, out_specs=[full] * 4,
        out_shape=[jax.ShapeDtypeStruct(w.shape, f32)] * 4, compiler_params=_cparams(("parallel",)),
    )(w, m, v, *flat)


def _slab(ref, axis, idx, n):
    return ref.at[tuple([slice(None)] * axis + [pl.ds(idx * n, n)])]


def all_gather(x, axis, *, name):
    n = x.shape[axis]
    out_shape = x.shape[:axis] + (N_DEV * n,) + x.shape[axis + 1:]

    def body(x_ref, out_ref, send_sems, recv_sems, local_sem):
        xx, yy, cc = lax.axis_index("x"), lax.axis_index("y"), lax.axis_index("c")
        me, sibling = (xx, yy, cc), (xx, yy, 1 - cc)
        chips = [(1 - xx, yy), (xx, 1 - yy), (1 - xx, 1 - yy)]

        def slab(px, py, pc):
            return _slab(out_ref, axis, 4 * px + 2 * py + pc, n)

        def copy(k, block, to, src=None):
            return pltpu.make_async_remote_copy(
                src_ref=slab(*block) if src is None else src, dst_ref=slab(*block),
                send_sem=send_sems.at[k], recv_sem=recv_sems.at[k], device_id=to, device_id_type=MESH)

        mine = pltpu.make_async_copy(x_ref, slab(*me), local_sem)
        mine.start()
        first = [copy(0, me, sibling, src=x_ref)]
        first += [copy(1 + j, me, (*chip, cc), src=x_ref) for j, chip in enumerate(chips)]
        for cp in first:
            cp.start()
        passed = [copy(4 + j, (*chip, cc), sibling) for j, chip in enumerate(chips)]
        for j, chip in enumerate(chips):
            copy(1 + j, (*chip, cc), me).wait_recv()
            passed[j].start()
        copy(0, sibling, me).wait_recv()
        for j, chip in enumerate(chips):
            copy(4 + j, (*chip, 1 - cc), me).wait_recv()
        for cp in first + passed:
            cp.wait_send()
        mine.wait()

    return pl.pallas_call(
        body, name=name, out_shape=jax.ShapeDtypeStruct(out_shape, x.dtype), in_specs=[ANY], out_specs=ANY,
        scratch_shapes=[pltpu.SemaphoreType.DMA((7,)), pltpu.SemaphoreType.DMA((7,)), pltpu.SemaphoreType.DMA],
    )(x)


N_CHIP = 4


HBM = pl.BlockSpec(memory_space=pltpu.HBM)
SEM = pl.BlockSpec(memory_space=pltpu.SEMAPHORE)
EFFECT = pltpu.SideEffectType.DATAFLOW_SIDE_EFFECTING
TOKEN = jax.ShapeDtypeStruct((8, LANE), f32)


def _in_hbm(a):
    return pltpu.with_memory_space_constraint(a, pltpu.HBM)


def pair_sums(g, axis, *, name):
    n = g.shape[axis] // N_DEV
    slab_shape = g.shape[:axis] + (n,) + g.shape[axis + 1:]
    cols = slab_shape[-1]
    rows = math.prod(slab_shape[:-1])
    col_slabs = axis == g.ndim - 1
    assert col_slabs or (axis == 0 and g.ndim == 2)

    def swap_body(g_ref, got_ref, send_sems, recv_sems):
        xx, yy, cc = lax.axis_index("x"), lax.axis_index("y"), lax.axis_index("c")
        copies = [pltpu.make_async_remote_copy(
            src_ref=_slab(g_ref, axis, 2 * q + 1 - cc, n), dst_ref=got_ref.at[q],
            send_sem=send_sems.at[q], recv_sem=recv_sems.at[q], device_id=(xx, yy, 1 - cc), device_id_type=MESH)
            for q in range(N_CHIP)]
        for cp in copies:
            cp.start()
        for cp in copies:
            cp.wait()

    got = pl.pallas_call(
        swap_body, name=name + "_swap", out_shape=jax.ShapeDtypeStruct((N_CHIP,) + slab_shape, g.dtype),
        in_specs=[ANY], out_specs=ANY, scratch_shapes=[pltpu.SemaphoreType.DMA((N_CHIP,))] * 2,
    )(g)

    tr = min(256, rows)

    def add_body(a0_ref, a1_ref, b_ref, pair_ref, own_ref):
        xx, yy, cc = lax.axis_index("x"), lax.axis_index("y"), lax.axis_index("c")
        mine = jnp.where(cc == 0, a0_ref[...], a1_ref[...])
        s = (mine.astype(f32) + b_ref[0].astype(f32)).astype(bf16)
        pair_ref[0] = s

        @pl.when(pl.program_id(1) == 2 * xx + yy)
        def _():
            own_ref[0] = s

    if col_slabs:
        a_spec = lambda c: pl.BlockSpec((tr, cols), lambda i, q: (i, 2 * q + c))
    else:
        a_spec = lambda c: pl.BlockSpec((tr, cols), lambda i, q: ((2 * q + c) * (n // tr) + i, 0))
    by_chip = pl.BlockSpec((1, tr, cols), lambda i, q: (q, i, 0))
    g2 = g.reshape(-1, g.shape[-1])
    pair, own = pl.pallas_call(
        add_body, name=name + "_add", grid=(rows // tr, N_CHIP), in_specs=[a_spec(0), a_spec(1), by_chip],
        out_specs=[by_chip, pl.BlockSpec((1, tr, cols), lambda i, q: (0, i, 0))],
        out_shape=[jax.ShapeDtypeStruct((N_CHIP, rows, cols), bf16), jax.ShapeDtypeStruct((1, rows, cols), bf16)],
        compiler_params=_cparams(("parallel", "arbitrary")),
    )(g2, g2, got.reshape(N_CHIP, rows, cols))
    return own, pair


def _send_copies(p_refs, land_refs, send_sems, recv_sems):
    xx, yy, cc = lax.axis_index("x"), lax.axis_index("y"), lax.axis_index("c")
    copies = []
    for t, (p, land) in enumerate(zip(p_refs, land_refs)):
        for k in range(1, N_CHIP):
            px = 1 - xx if k & 2 else xx
            py = 1 - yy if k & 1 else yy
            s = (N_CHIP - 1) * t + k - 1
            copies.append(pltpu.make_async_remote_copy(
                src_ref=p.at[2 * px + py], dst_ref=land.at[k - 1], send_sem=send_sems.at[s], recv_sem=recv_sems.at[s],
                device_id=(px, py, cc), device_id_type=MESH))
    return copies


def _direct_copies(axes, g_refs, land_refs, send_sems, recv_sems):
    xx, yy, cc = lax.axis_index("x"), lax.axis_index("y"), lax.axis_index("c")
    copies = []
    for t, (g_ref, land, axis) in enumerate(zip(g_refs, land_refs, axes)):
        n = g_ref.shape[axis] // N_DEV
        for k in range(1, N_DEV):
            px = 1 - xx if k & 4 else xx
            py = 1 - yy if k & 2 else yy
            pc = 1 - cc if k & 1 else cc
            s = (N_DEV - 1) * t + k - 1
            copies.append(pltpu.make_async_remote_copy(
                src_ref=_slab(g_ref, axis, 4 * px + 2 * py + pc, n), dst_ref=land.at[k - 1],
                send_sem=send_sems.at[s], recv_sem=recv_sems.at[s], device_id=(px, py, pc), device_id_type=MESH))
    return copies


def own_slab(g, axis, *, name):
    n = g.shape[axis] // N_DEV
    slab_shape = g.shape[:axis] + (n,) + g.shape[axis + 1:]
    cols, rows = slab_shape[-1], math.prod(slab_shape[:-1])
    tr = min(256, rows)

    def body(g_ref, o_ref):
        o_ref[0] = g_ref[...]

    def where(i):
        me = 4 * lax.axis_index("x") + 2 * lax.axis_index("y") + lax.axis_index("c")
        return (i, me) if axis == g.ndim - 1 else (me * (n // tr) + i, 0)

    return pl.pallas_call(
        body, name=name, grid=(rows // tr,), in_specs=[pl.BlockSpec((tr, cols), where)],
        out_specs=pl.BlockSpec((1, tr, cols), lambda i: (0, i, 0)),
        out_shape=jax.ShapeDtypeStruct((1, rows, cols), g.dtype), compiler_params=_cparams(("parallel",)),
    )(g.reshape(-1, g.shape[-1]))


def send_pairs_start(pairs, after, *, name, direct_axes=None):
    nt = len(pairs)
    if direct_axes is None:
        build = _send_copies
        lands = [lax.empty((N_CHIP - 1,) + p.shape[1:], p.dtype) for p in pairs]
    else:
        build = functools.partial(_direct_copies, direct_axes)
        lands = [lax.empty((N_DEV - 1,) + p.shape[:ax] + (p.shape[ax] // N_DEV,) + p.shape[ax + 1:], p.dtype)
                 for p, ax in zip(pairs, direct_axes)]

    def body(*refs):
        p_refs, land_refs = refs[:nt], refs[nt:2 * nt]
        send_sems, recv_sems = refs[2 * nt + 1], refs[2 * nt + 2]
        token = refs[-1]
        for cp in build(p_refs, land_refs, send_sems, recv_sems):
            cp.start()
        token[...] = jnp.zeros_like(token)

    nsem = sum(l.shape[0] for l in lands)
    outs = pl.pallas_call(
        body, name=name,
        out_shape=(pltpu.SemaphoreType.DMA((nsem,)), pltpu.SemaphoreType.DMA((nsem,)))
        + tuple(pltpu.HBM(a.shape, a.dtype) for a in list(pairs) + lands) + (TOKEN,),
        in_specs=[HBM] * (2 * nt) + [ANY], out_specs=(SEM, SEM) + (HBM,) * (2 * nt) + (VM,),
        input_output_aliases={i: 2 + i for i in range(2 * nt)},
        compiler_params=pltpu.CompilerParams(has_side_effects=EFFECT),
    )(*[_in_hbm(a) for a in list(pairs) + lands], after)
    return outs[:-1], outs[-1]


def send_pairs_wait(handles, after, *, name, direct_axes=None):
    send_sems, recv_sems = handles[0], handles[1]
    bufs = handles[2:]
    nt = len(bufs) // 2
    build = _send_copies if direct_axes is None else functools.partial(_direct_copies, direct_axes)

    def body(*refs):
        p_refs, land_refs = refs[:nt], refs[nt:2 * nt]
        send_sems, recv_sems = refs[2 * nt], refs[2 * nt + 1]
        for cp in build(p_refs, land_refs, send_sems, recv_sems):
            cp.wait_send()
            cp.wait_recv()

    outs = pl.pallas_call(
        body, name=name, out_shape=tuple(pltpu.HBM(a.shape, a.dtype) for a in bufs),
        in_specs=[HBM] * (2 * nt) + [SEM, SEM, ANY], out_specs=(HBM,) * (2 * nt),
        input_output_aliases={i: i for i in range(2 * nt)},
        compiler_params=pltpu.CompilerParams(has_side_effects=EFFECT),
    )(*bufs, send_sems, recv_sems, after)
    return outs[nt:]


def _gather_copies(x_refs, land_refs, axes, send_sems, recv_sems):
    xx, yy, cc = lax.axis_index("x"), lax.axis_index("y"), lax.axis_index("c")
    me = 4 * xx + 2 * yy + cc
    copies = []
    for t, (x_ref, land, axis) in enumerate(zip(x_refs, land_refs, axes)):
        n = x_ref.shape[axis]
        for k in range(1, N_DEV):
            px = 1 - xx if k & 4 else xx
            py = 1 - yy if k & 2 else yy
            pc = 1 - cc if k & 1 else cc
            s = (N_DEV - 1) * t + k - 1
            copies.append(pltpu.make_async_remote_copy(
                src_ref=x_ref, dst_ref=_slab(land, axis, me, n), send_sem=send_sems.at[s], recv_sem=recv_sems.at[s],
                device_id=(px, py, pc), device_id_type=MESH))
    return copies


def _place_own(x, axis, *, name):
    full = x.shape[:axis] + (N_DEV * x.shape[axis],) + x.shape[axis + 1:]
    lead = x.shape[0]
    tile = lead if axis == 0 else (256 if x.ndim == 2 and lead % 256 == 0 else 1 if x.ndim == 3 else lead)
    rest = (0,) * (x.ndim - 1)

    def body(land_ref, x_ref, o_ref):
        o_ref[...] = x_ref[...]

    def where(i):
        me = 4 * lax.axis_index("x") + 2 * lax.axis_index("y") + lax.axis_index("c")
        idx = [i] + list(rest)
        idx[axis] = me
        return tuple(idx)

    return pl.pallas_call(
        body, name=name, grid=(lead // tile,),
        in_specs=[ANY, pl.BlockSpec((tile,) + x.shape[1:], lambda i: (i,) + rest)],
        out_specs=pl.BlockSpec((tile,) + x.shape[1:], where),
        out_shape=jax.ShapeDtypeStruct(full, x.dtype), input_output_aliases={0: 0},
        compiler_params=_cparams(("arbitrary",)),
    )(lax.empty(full, x.dtype), x)


def gather_start(xs, axes, after, *, name):
    nt = len(xs)
    lands = [_place_own(x, axis, name=name + "_own") for x, axis in zip(xs, axes)]

    def body(*refs):
        x_refs, land_refs = refs[:nt], refs[nt:2 * nt]
        send_sems, recv_sems = refs[2 * nt + 1], refs[2 * nt + 2]
        token = refs[-1]
        for cp in _gather_copies(x_refs, land_refs, axes, send_sems, recv_sems):
            cp.start()
        token[...] = jnp.zeros_like(token)

    nsem = (N_DEV - 1) * nt
    outs = pl.pallas_call(
        body, name=name,
        out_shape=(pltpu.SemaphoreType.DMA((nsem,)), pltpu.SemaphoreType.DMA((nsem,)))
        + tuple(pltpu.HBM(a.shape, a.dtype) for a in list(xs) + lands) + (TOKEN,),
        in_specs=[HBM] * (2 * nt) + [ANY], out_specs=(SEM, SEM) + (HBM,) * (2 * nt) + (VM,),
        input_output_aliases={i: 2 + i for i in range(2 * nt)},
        compiler_params=pltpu.CompilerParams(has_side_effects=EFFECT),
    )(*[_in_hbm(a) for a in list(xs) + lands], after)
    return outs[:-1], outs[-1]


def gather_wait(handles, axes, after, *, name):
    send_sems, recv_sems = handles[0], handles[1]
    bufs = handles[2:]
    nt = len(bufs) // 2

    def body(*refs):
        x_refs, land_refs = refs[:nt], refs[nt:2 * nt]
        send_sems, recv_sems = refs[2 * nt], refs[2 * nt + 1]
        for cp in _gather_copies(x_refs, land_refs, axes, send_sems, recv_sems):
            cp.wait_send()
            cp.wait_recv()

    outs = pl.pallas_call(
        body, name=name, out_shape=tuple(pltpu.HBM(a.shape, a.dtype) for a in bufs),
        in_specs=[HBM] * (2 * nt) + [SEM, SEM, ANY], out_specs=(HBM,) * (2 * nt),
        input_output_aliases={i: i for i in range(2 * nt)},
        compiler_params=pltpu.CompilerParams(has_side_effects=EFFECT),
    )(*bufs, send_sems, recv_sems, after)
    return outs[nt:]


def _blockdiag(b, nb):
    j, _, r, c = b.shape
    eye = jnp.eye(nb, dtype=bool)[None, :, None, :, None]
    return jnp.where(eye, b[:, :, :, None, :], jnp.zeros((), b.dtype)).reshape(j, nb * r, nb * c)


def _diagblocks(d, nb):
    j, rr, cc = d.shape
    return jnp.einsum('jarac->jarc', d.reshape(j, nb, rr // nb, nb, cc // nb))


def _s5_b_dense(bbar):
    return _blockdiag(bbar.transpose(0, 2, 1).reshape(NCH, 8, S5_GROUP, S5_STATE), 8)


def _s5_b_undense(d):
    return _diagblocks(d, 8).reshape(S5_GROUPS, S5_GROUP, S5_STATE).transpose(0, 2, 1)


def _s5_c_dense(c):
    return _blockdiag(c.transpose(0, 2, 1).reshape(NCH, 8, S5_STATE, S5_GROUP), 8)


def _s5_c_undense(d):
    return _diagblocks(d, 8).reshape(S5_GROUPS, S5_STATE, S5_GROUP).transpose(0, 2, 1)


def _rg_dense(w):
    return _blockdiag(w.reshape(NCH, 2, RG_BLOCK, RG_BLOCK), 2)


def _rg_undense(d):
    return _diagblocks(d, 2).reshape(RG_BLOCKS, RG_BLOCK, RG_BLOCK)


def _chunks(v):
    return v.reshape(NCH, 1, LANE)


def _tri(tm):
    r = jnp.arange(tm)
    m = (r[:, None] >= r[None, :]) & (r[:, None] // HG_SUB == r[None, :] // HG_SUB)
    m = m.astype(f32)
    return m[None], m.T[None]


SMALL = ['norm_w', 's5_lambda_re', 's5_lambda_im', 's5_log_step', 's5_b_re', 's5_b_im', 's5_c_re', 's5_c_im',
         's5_d', 's5_b_glu', 'rg_conv_w', 'rg_conv_b', 'rg_w_a', 'rg_b_a', 'rg_w_x', 'rg_b_x', 'rg_lambda',
         'hg_lower_bounds', 'hg_norm_w', 'final_norm_w']
WEIGHTS = ['norm_w', 'w_in', 's5_lambda_re', 's5_lambda_im', 's5_log_step', 's5_b_re', 's5_b_im', 's5_c_re',
           's5_c_im', 's5_d', 's5_w_glu', 's5_b_glu', 'rg_conv_w', 'rg_conv_b', 'rg_w_a', 'rg_b_a', 'rg_w_x',
           'rg_b_x', 'rg_lambda', 'hg_lower_bounds', 'hg_norm_w', 'w_branch', 'w_out', 'final_norm_w']
PACK_ROWS = 512


def _pack(arrs):
    flat = jnp.concatenate([a.reshape(-1) for a in arrs])
    pad = (-flat.shape[0]) % (PACK_ROWS * LANE)
    return jnp.pad(flat, (0, pad)).reshape(1, -1, LANE)


def _unpack(buf, shapes):
    flat = buf.reshape(-1)
    out, off = [], 0
    for s in shapes:
        n = math.prod(s)
        out.append(flat[off:off + n].reshape(s))
        off += n
    return out


def _step(x, tgt, w, m, v):
    t = x.shape[0]
    tri, tri_t = _tri(min(LANE, t))
    me = 4 * lax.axis_index("x") + 2 * lax.axis_index("y") + lax.axis_index("c")

    big = ('w_in', 's5_w_glu', 'w_branch', 'w_out')
    big_axis = (1, 0, 2, 0)
    shards = lambda l: [w[k][l].astype(bf16) for k in big]
    win, wglu, wbr, wout = ([None] * DEPTH for _ in range(4))
    win[0] = all_gather(shards(0)[0], big_axis[0], name="ag_w_in")
    rest0_axis = big_axis[1:] + (1,)
    rest0, rest0_token = gather_start(shards(0)[1:] + [w['rg_conv_w'].reshape(DEPTH * RG_CONV, LANE)], rest0_axis,
                                      win[0], name="ag_start_0")

    lb_rows = [w['hg_lower_bounds'][l][None] for l in range(DEPTH)]
    lbs = whole(lb_prep_fn, lb_rows, [(1, W_MIX)] * DEPTH, name="lb_prep")

    saved = []
    for l in range(DEPTH):
        s = {}
        nw = w['norm_w'][l].reshape(1, 1, D_MODEL)
        (h,) = rowwise(ln_fn, [(x, 0, D_MODEL)], [nw], [], [(D_MODEL, bf16)], name="ln_fwd", tm=512)
        token = None
        if l + 1 < DEPTH:
            handles, token = gather_start(shards(l + 1), big_axis, rest0_token if l == 0 else x, name=f"ag_start_{l + 1}")
        z = mm(h, win[l], after=token, name="mm_in", tn=2048)
        if l == 0:
            wglu[0], wbr[0], wout[0], conv_w = gather_wait(rest0, rest0_axis, z, name="ag_wait_0")
            conv_w = conv_w.reshape(DEPTH, RG_CONV, W_MIX)
        s5p = [w['s5_lambda_re'][l][..., None], w['s5_lambda_im'][l][..., None], w['s5_log_step'][l][:, None, None],
               w['s5_b_re'][l], w['s5_b_im'][l]]
        gp = (S5_GROUPS, S5_STATE)
        abar_re, abar_im, bbar_re, bbar_im = whole(
            s5_prep_fn, s5p, [gp + (1,), gp + (1,), gp + (S5_GROUP,), gp + (S5_GROUP,)], name="s5_prep")
        a_re, a_im = abar_re.reshape(NCH, 1, S5_SC), abar_im.reshape(NCH, 1, S5_SC)
        bd_re, bd_im = _s5_b_dense(bbar_re), _s5_b_dense(bbar_im)
        cd_re, cd_im = _s5_c_dense(w['s5_c_re'][l]), _s5_c_dense(w['s5_c_im'][l])
        yssm, xre, xim = s5_scan_fwd(z, bd_re, bd_im, cd_re, cd_im, a_re, a_im)
        s5post_p = [w['s5_d'][l].reshape(1, 1, W_MIX), wglu[l].astype(f32)[None], w['s5_b_glu'][l].reshape(1, 1, W_MIX)]
        s5post_rows = [(yssm, 0, W_MIX), (z, C_UA, W_MIX), (z, C_GA, W_MIX)]
        (ya,) = rowwise(s5_post_fn, s5post_rows, s5post_p, [], [(W_MIX, bf16)], name="s5_post_fwd", tm=512)
        cw, cb = conv_w[l].reshape(RG_CONV, NCH, LANE).transpose(1, 0, 2), _chunks(w['rg_conv_b'][l])
        rg_p = [_rg_dense(w['rg_w_a'][l]), _chunks(w['rg_b_a'][l]), _rg_dense(w['rg_w_x'][l]),
                _chunks(w['rg_b_x'][l]), _chunks(w['rg_lambda'][l])]
        xc, ra, hb = rg_fwd(z, cw, cb, rg_p)
        hg_rows = [(z, C_Q, W_MIX), (z, C_F, W_MIX)]
        hg_p = [_chunks(lbs[l].reshape(W_MIX))]
        qs, kk, gcum = rowwise(hg_pre_fn, hg_rows, hg_p, [tri, tri_t], [(W_MIX, f32)] * 3, name="hg_pre_fwd", ncol=NCH, tm=TM_CHUNK)
        oc, sall = hg_chunk_fwd(qs, kk, gcum, z)
        bp_rows = [(hb, 0, W_MIX), (z, C_GB, W_MIX), (oc, 0, W_MIX), (z, C_GC, W_MIX)]
        bp_p = [_chunks(w['hg_norm_w'][l])]
        yb, yc = rowwise(branch_prep_fn, bp_rows, bp_p, [], [(W_MIX, bf16)] * 2, name="branch_prep_fwd", ncol=NCH, tm=TM_CHUNK)
        ys = [ya, yb, yc]
        br = [mm(ys[n], wbr[l][n], name="mm_branch", out_dtype=bf16, tn=2048) for n in range(N_BRANCH)]
        mg_rows = [(br[n], 0, D_MODEL) for n in range(N_BRANCH)] + [(z, C_GATE + n * D_MODEL, D_MODEL) for n in range(N_BRANCH)]
        (merged,) = rowwise(merge_fn, mg_rows, [], [], [(D_MODEL, bf16)], name="merge_fwd", ncol=2, tm=512)
        x_new = mm(merged, wout[l], add=x, name="mm_out")
        s.update(x=x, h=h, z=z, s5p=s5p, s5=(bd_re, bd_im, cd_re, cd_im, a_re, a_im), xre=xre, xim=xim,
                 s5post_rows=s5post_rows, s5post_p=s5post_p, cw=cw, xc=xc, rg_p=rg_p, ra=ra, hb=hb,
                 hg_rows=hg_rows, hg_p=hg_p, qs=qs, kk=kk, gcum=gcum, sall=sall, bp_rows=bp_rows, bp_p=bp_p,
                 ys=ys, mg_rows=mg_rows, merged=merged, nw=nw)
        saved.append(s)
        x = x_new
        if l + 1 < DEPTH:
            win[l + 1], wglu[l + 1], wbr[l + 1], wout[l + 1] = gather_wait(handles, big_axis, x, name=f"ag_wait_{l + 1}")

    fnw = w['final_norm_w'].reshape(1, 1, D_MODEL)
    ones = jnp.ones((t, 1), f32)
    dx, d_fnw, loss_sum = rowwise_vjp(loss_fn, [(x, 0, D_MODEL), (tgt, 0, D_MODEL)], [fnw], [], [(ones, 0, 1)],
                                      [(0, f32)], name="loss_head", sum_primal=0)
    loss = lax.psum(loss_sum.reshape(()), ("x", "y", "c"))

    small_g = {k: [None] * DEPTH for k in SMALL if k != 'final_norm_w'}
    own_sums, in_flight = [None] * DEPTH, [None] * DEPTH
    d_lbs = [None] * DEPTH
    token = None
    for l in reversed(range(DEPTH)):
        s = saved[l]
        z = s['z']
        dxb = dx.astype(bf16)
        d_merged = mm(dxb, wout[l], bt=True, after=token, name="mm_out_dx", out_dtype=bf16, tn=2048)
        d_wout = mm(s['merged'], dxb, at=True, name="mm_out_dw", out_dtype=bf16)
        mg = rowwise_vjp(merge_fn, s['mg_rows'], [], [], [(d_merged, 0, D_MODEL)],
                         [(n, bf16) for n in range(2 * N_BRANCH)], name="merge_bwd", ncol=2)
        d_br, d_gl = mg[:N_BRANCH], mg[N_BRANCH:]
        d_ys = [mm(d_br[n], wbr[l][n], bt=True, name="mm_branch_dx", out_dtype=bf16) for n in range(N_BRANCH)]
        d_wbr = jnp.stack([mm(s['ys'][n], d_br[n], at=True, name="mm_branch_dw", out_dtype=bf16) for n in range(N_BRANCH)])
        d_hb, d_gb, d_oc, d_gc, d_hnw = rowwise_vjp(
            branch_prep_fn, s['bp_rows'], s['bp_p'], [], [(d_ys[1], 0, W_MIX), (d_ys[2], 0, W_MIX)],
            [(0, f32), (1, bf16), (2, f32), (3, bf16)], name="branch_prep_bwd", ncol=NCH, tm=TM_CHUNK)
        small_g['hg_norm_w'][l] = d_hnw.reshape(W_MIX)
        d_qs, d_kk, d_gcum, d_i = hg_chunk_bwd(d_oc, s['qs'], s['kk'], s['gcum'], z, s['sall'])
        d_q, d_f, d_lb = rowwise_vjp(
            hg_pre_fn, s['hg_rows'], s['hg_p'], [tri, tri_t], [(d_qs, 0, W_MIX), (d_kk, 0, W_MIX), (d_gcum, 0, W_MIX)],
            [(0, bf16), (1, bf16)], name="hg_pre_bwd", ncol=NCH, tm=TM_CHUNK)
        d_lbs[l] = d_lb.reshape(1, W_MIX)
        d_xb, d_cw, d_cb, d_wa, d_ba, d_wx, d_bx, d_lam = rg_bwd(d_hb, z, s['xc'], s['ra'], s['hb'], s['cw'], s['rg_p'])
        small_g['rg_w_a'][l], small_g['rg_w_x'][l] = _rg_undense(d_wa), _rg_undense(d_wx)
        small_g['rg_b_a'][l], small_g['rg_b_x'][l] = d_ba.reshape(W_MIX), d_bx.reshape(W_MIX)
        small_g['rg_lambda'][l] = d_lam.reshape(W_MIX)
        small_g['rg_conv_w'][l] = d_cw.transpose(1, 0, 2).reshape(RG_CONV, W_MIX)
        small_g['rg_conv_b'][l] = d_cb.reshape(W_MIX)
        d_yssm, d_u1, d_ga, d_d, d_wglu, d_bglu = rowwise_vjp(
            s5_post_fn, s['s5post_rows'], s['s5post_p'], [], [(d_ys[0], 0, W_MIX)],
            [(0, bf16), (1, bf16), (2, bf16)], name="s5_post_bwd", tm=512)
        small_g['s5_d'][l], small_g['s5_b_glu'][l] = d_d.reshape(W_MIX), d_bglu.reshape(W_MIX)
        d_ua, d_bdre, d_bdim, d_cdre, d_cdim, d_are, d_aim = s5_scan_bwd(d_yssm, d_u1, z, s['xre'], s['xim'], *s['s5'])
        small_g['s5_c_re'][l], small_g['s5_c_im'][l] = _s5_c_undense(d_cdre), _s5_c_undense(d_cdim)
        gp = (S5_GROUPS, S5_STATE, 1)
        s5g = whole_vjp(s5_prep_fn, s['s5p'],
                        [d_are.reshape(gp), d_aim.reshape(gp), _s5_b_undense(d_bdre), _s5_b_undense(d_bdim)],
                        name="s5_prep_bwd")
        small_g['s5_lambda_re'][l] = s5g[0].reshape(S5_GROUPS, S5_STATE)
        small_g['s5_lambda_im'][l] = s5g[1].reshape(S5_GROUPS, S5_STATE)
        small_g['s5_log_step'][l] = s5g[2].reshape(S5_GROUPS)
        small_g['s5_b_re'][l], small_g['s5_b_im'][l] = s5g[3], s5g[4]
        dz = jnp.concatenate([d_ua, d_ga, d_xb, d_gb, d_q, d_f, d_i, d_gc] + list(d_gl), axis=1)
        d_win = mm(s['h'], dz, at=True, name="mm_in_dw", out_dtype=bf16)
        grads = (d_win, d_wglu[0].astype(bf16), d_wbr, d_wout)
        if l == 0:
            sums = [pair_sums(g, ax, name="rs_" + k) for g, ax, k in zip(grads, big_axis, big)]
            own_sums[l] = [own for own, _ in sums]
            in_flight[l], token = send_pairs_start([pair for _, pair in sums], s5g[0], name=f"rs_start_{l}")
        else:
            own_sums[l] = [own_slab(g, ax, name="rs_own_" + k) for g, ax, k in zip(grads, big_axis, big)]
            in_flight[l], token = send_pairs_start(grads, s5g[0], name=f"rs_start_{l}", direct_axes=big_axis)
        d_h = mm(dz, win[l], bt=True, after=token, name="mm_in_dx", tk=N_IN // 4)
        dx, d_nw = rowwise_vjp(ln_res_fn, [(s['x'], 0, D_MODEL)], [s['nw']], [], [(d_h, 0, D_MODEL), (dx, 0, D_MODEL)],
                               [(0, f32)], name="ln_bwd", tm=512)
        small_g['norm_w'][l] = d_nw.reshape(D_MODEL)
    d_lb_raw = whole_vjp(lb_prep_fn, lb_rows, d_lbs, name="lb_prep_bwd")
    small_g['hg_lower_bounds'] = [r.reshape(W_MIX) for r in d_lb_raw]

    per_layer = [k for k in SMALL if k != 'final_norm_w']
    shapes = [(DEPTH,) + small_g[k][0].shape for k in per_layer] + [(D_MODEL,)]
    pieces = [small_g[k][l] for k in per_layer for l in range(DEPTH)] + [d_fnw]
    small_in_flight, small_token = gather_start([_pack(pieces)[0].astype(bf16)], (0,), dx, name="ag_small_start")
    res = {}

    arrived = [send_pairs_wait(in_flight[l], small_token, name=f"rs_wait_{l}", direct_axes=None if l == 0 else big_axis)
               for l in range(DEPTH)]
    for i, (k, tr) in enumerate((('w_in', 32), ('s5_w_glu', 32), ('w_branch', 128), ('w_out', 32))):
        shp = w[k].shape
        r3 = lambda a: a.reshape(DEPTH, -1, shp[-1])
        slots = [[own_sums[l][i], arrived[l][i].reshape(arrived[l][i].shape[0], -1, shp[-1])] for l in range(DEPTH)]
        outs = adamw(r3(w[k]), r3(m[k]), r3(v[k]), slots, name="adamw_" + k, tr=tr)
        for kind, buf in zip(('grad', 'delta', 'new_m', 'new_v'), outs):
            res[kind + '_' + k] = buf.reshape(shp)

    (g_all,) = gather_wait(small_in_flight, (0,), outs[0], name="ag_small_wait")
    g_all = g_all.reshape(N_DEV, -1, LANE)

    def local(d, k):
        return jnp.zeros(shapes[SMALL.index(k)], f32) if k == 'rg_conv_w' else d[k]
    packed = [_pack([local(d, k) for k in SMALL]) for d in (w, m, v)]
    outs = adamw(*packed, [[g_all]], name="adamw_small", tr=512)
    for kind, buf in zip(('grad', 'delta', 'new_m', 'new_v'), outs):
        for k, a in zip(SMALL, _unpack(buf, shapes)):
            res[kind + '_' + k] = a
    g_cw = lax.dynamic_slice_in_dim(res['grad_rg_conv_w'], me * LANE, LANE, axis=2)
    cw3 = lambda a: a.reshape(1, DEPTH * RG_CONV, LANE)
    outs = adamw(cw3(w['rg_conv_w']), cw3(m['rg_conv_w']), cw3(v['rg_conv_w']), [[cw3(g_cw)]], name="adamw_conv_w", tr=16)
    for kind, buf in zip(('grad', 'delta', 'new_m', 'new_v'), outs):
        res[kind + '_rg_conv_w'] = buf.reshape(DEPTH, RG_CONV, LANE)

    return (loss, dx[None]) + tuple(res[kind + '_' + k] for kind in ('grad', 'delta', 'new_m', 'new_v') for k in WEIGHTS)


def kernel(x, norm_w, w_in, s5_lambda_re, s5_lambda_im, s5_log_step, s5_b_re, s5_b_im, s5_c_re, s5_c_im, s5_d, s5_w_glu, s5_b_glu, rg_conv_w, rg_conv_b, rg_w_a, rg_b_a, rg_w_x, rg_b_x, rg_lambda, hg_lower_bounds, hg_norm_w, w_branch, w_out, final_norm_w, loss_target, m_norm_w, m_w_in, m_s5_lambda_re, m_s5_lambda_im, m_s5_log_step, m_s5_b_re, m_s5_b_im, m_s5_c_re, m_s5_c_im, m_s5_d, m_s5_w_glu, m_s5_b_glu, m_rg_conv_w, m_rg_conv_b, m_rg_w_a, m_rg_b_a, m_rg_w_x, m_rg_b_x, m_rg_lambda, m_hg_lower_bounds, m_hg_norm_w, m_w_branch, m_w_out, m_final_norm_w, v_norm_w, v_w_in, v_s5_lambda_re, v_s5_lambda_im, v_s5_log_step, v_s5_b_re, v_s5_b_im, v_s5_c_re, v_s5_c_im, v_s5_d, v_s5_w_glu, v_s5_b_glu, v_rg_conv_w, v_rg_conv_b, v_rg_w_a, v_rg_b_a, v_rg_w_x, v_rg_b_x, v_rg_lambda, v_hg_lower_bounds, v_hg_norm_w, v_w_branch, v_w_out, v_final_norm_w):
    w = dict(zip(WEIGHTS, (norm_w, w_in, s5_lambda_re, s5_lambda_im, s5_log_step, s5_b_re, s5_b_im, s5_c_re, s5_c_im, s5_d, s5_w_glu, s5_b_glu, rg_conv_w, rg_conv_b, rg_w_a, rg_b_a, rg_w_x, rg_b_x, rg_lambda, hg_lower_bounds, hg_norm_w, w_branch, w_out, final_norm_w)))
    m = dict(zip(WEIGHTS, (m_norm_w, m_w_in, m_s5_lambda_re, m_s5_lambda_im, m_s5_log_step, m_s5_b_re, m_s5_b_im, m_s5_c_re, m_s5_c_im, m_s5_d, m_s5_w_glu, m_s5_b_glu, m_rg_conv_w, m_rg_conv_b, m_rg_w_a, m_rg_b_a, m_rg_w_x, m_rg_b_x, m_rg_lambda, m_hg_lower_bounds, m_hg_norm_w, m_w_branch, m_w_out, m_final_norm_w)))
    v = dict(zip(WEIGHTS, (v_norm_w, v_w_in, v_s5_lambda_re, v_s5_lambda_im, v_s5_log_step, v_s5_b_re, v_s5_b_im, v_s5_c_re, v_s5_c_im, v_s5_d, v_s5_w_glu, v_s5_b_glu, v_rg_conv_w, v_rg_conv_b, v_rg_w_a, v_rg_b_a, v_rg_w_x, v_rg_b_x, v_rg_lambda, v_hg_lower_bounds, v_hg_norm_w, v_w_branch, v_w_out, v_final_norm_w)))
    return _step(x[0], loss_target[0], w, m, v)
```

```python
import functools
import math

import jax
import jax.numpy as jnp
from jax import lax
from jax.experimental import pallas as pl
from jax.experimental.pallas import tpu as pltpu

f32 = jnp.float32
bf16 = jnp.bfloat16

D_MODEL = 2048
W_MIX = 1024
DEPTH = 4
N_BRANCH = 3
N_IN = 8 * W_MIX + N_BRANCH * D_MODEL
S5_GROUPS, S5_STATE, S5_GROUP = 64, 64, 16
RG_BLOCKS, RG_BLOCK, RG_CONV, RG_C = 16, 64, 4, 8.0
HG_HEADS, HG_DK = 8, 128
HG_SUB = 16
EPS = 1e-6
ADAM_LR, ADAM_B1, ADAM_B2, ADAM_EPS, ADAM_WD, ADAM_STEP = 0.001, 0.9, 0.999, 1e-08, 0.01, 10

N_DEV = 8
LANE = 128
NCH = W_MIX // LANE
TM_CHUNK = 1024
VMEM_LIMIT = 56 * 1024 * 1024
MESH = pl.DeviceIdType.MESH
ANY = pl.BlockSpec(memory_space=pl.ANY)
HIGHEST = lax.Precision.HIGHEST

C_UA, C_GA, C_XB, C_GB, C_Q, C_F, C_I, C_GC, C_GATE = (W_MIX * k for k in range(9))


def _cparams(sem=None):
    return pltpu.CompilerParams(dimension_semantics=sem, vmem_limit_bytes=VMEM_LIMIT)


@jax.custom_vjp
def bdot(a, w):
    return jnp.dot(a.astype(bf16), w.astype(bf16), preferred_element_type=f32)


def _bdot_fwd(a, w):
    return bdot(a, w), (a, w)


def _bdot_bwd(res, g):
    a, w = res
    gb = g.astype(bf16)
    da = lax.dot_general(gb, w.astype(bf16), (((1,), (1,)), ((), ())), preferred_element_type=f32)
    dw = lax.dot_general(a.astype(bf16), gb, (((0,), (0,)), ((), ())), preferred_element_type=f32)
    return da, dw


bdot.defvjp(_bdot_fwd, _bdot_bwd)


def _blockmm(c, a):
    n = c.shape[0]
    return jnp.concatenate([jnp.dot(c, a[i:i + n], preferred_element_type=f32, precision=HIGHEST)
                            for i in range(0, a.shape[0], n)], axis=0)


@jax.custom_vjp
def cdot(c, ct, a):
    return _blockmm(c, a)


def _cdot_fwd(c, ct, a):
    return cdot(c, ct, a), (c, ct)


def _cdot_bwd(res, g):
    c, ct = res
    return jnp.zeros_like(c), jnp.zeros_like(ct), _blockmm(ct, g)


cdot.defvjp(_cdot_fwd, _cdot_bwd)


def mm(a, b, *, name, out_dtype=f32, add=None, after=None, at=False, bt=False, tm=1024, tn=1024, tk=4096):
    k, m = a.shape if at else a.shape[::-1]
    n = b.shape[0] if bt else b.shape[1]
    tm, tn, tk = min(tm, m), min(tn, n), min(tk, k)
    assert m % tm == 0 and n % tn == 0 and k % tk == 0
    nk = k // tk
    dims = (((0 if at else 1,), (1 if bt else 0,)), ((), ()))

    def body(*refs):
        a_ref, b_ref = refs[:2]
        r_ref = refs[2] if add is not None else None
        o_ref = refs[-1] if nk == 1 else refs[-2]
        part = lax.dot_general(a_ref[...], b_ref[...], dims, preferred_element_type=f32)
        if nk == 1:
            if add is not None:
                part = part + r_ref[...]
            o_ref[...] = part.astype(out_dtype)
            return
        acc_ref = refs[-1]
        kk = pl.program_id(2)

        @pl.when(kk == 0)
        def _():
            acc_ref[...] = part

        @pl.when(kk > 0)
        def _():
            acc_ref[...] = acc_ref[...] + part

        @pl.when(kk == nk - 1)
        def _():
            acc = acc_ref[...]
            if add is not None:
                acc = acc + r_ref[...]
            o_ref[...] = acc.astype(out_dtype)

    b_spec = pl.BlockSpec((tn, tk), lambda i, j, q: (j, q)) if bt else pl.BlockSpec((tk, tn), lambda i, j, q: (q, j))
    a_spec = pl.BlockSpec((tk, tm), lambda i, j, q: (q, i)) if at else pl.BlockSpec((tm, tk), lambda i, j, q: (i, q))
    in_specs = [a_spec, b_spec]
    args = [a, b]
    if add is not None:
        in_specs.append(pl.BlockSpec((tm, tn), lambda i, j, q: (i, j)))
        args.append(add)
    if after is not None:
        in_specs.append(pl.BlockSpec(after.shape, lambda i, j, q: (0, 0)))
        args.append(after)
    return pl.pallas_call(
        body, name=name, grid=(m // tm, n // tn, nk), in_specs=in_specs,
        out_specs=pl.BlockSpec((tm, tn), lambda i, j, q: (i, j)),
        out_shape=jax.ShapeDtypeStruct((m, n), out_dtype),
        scratch_shapes=[] if nk == 1 else [pltpu.VMEM((tm, tn), f32)],
        compiler_params=_cparams(("parallel", "parallel", "arbitrary")),
    )(*args)


def _row_spec(tm, wc, col_off):
    base = col_off // wc
    assert col_off % wc == 0
    return pl.BlockSpec((tm, wc), lambda j, i: (i, base + j))


def _slab_spec(arr):
    r, c = arr.shape[1:]
    if arr.shape[0] == 1:
        return pl.BlockSpec((1, r, c), lambda j, i: (0, 0, 0))
    return pl.BlockSpec((1, r, c), lambda j, i: (j, 0, 0))


def rowwise(fn, rows, params, consts, outs, *, name, tm=256, ncol=1, rowid=False):
    t = rows[0][0].shape[0]
    tm = min(tm, t)
    nr, npar, nc, no = len(rows), len(params), len(consts), len(outs)

    def body(*refs):
        r = [refs[k][...].astype(f32) for k in range(nr)]
        p = [refs[nr + k][0] for k in range(npar + nc)]
        extra = ()
        if rowid:
            extra = (pl.program_id(1) * tm + lax.broadcasted_iota(jnp.int32, (tm, 1), 0),)
        res = fn(*extra, *r, *p)
        for k in range(no):
            refs[nr + npar + nc + k][...] = res[k].astype(outs[k][1])

    in_specs = [_row_spec(tm, w // ncol, off) for (_, off, w) in rows]
    in_specs += [_slab_spec(a) for a in list(params) + list(consts)]
    out_specs = [pl.BlockSpec((tm, w // ncol), lambda j, i: (i, j)) for (w, _) in outs]
    out_shape = [jax.ShapeDtypeStruct((t, w), dt) for (w, dt) in outs]
    return pl.pallas_call(
        body, name=name, grid=(ncol, t // tm), in_specs=in_specs, out_specs=out_specs, out_shape=out_shape,
        compiler_params=_cparams(("parallel", "parallel")),
    )(*[r[0] for r in rows], *params, *consts)


def rowwise_vjp(fn, rows, params, consts, cts, d_rows, *, name, tm=256, ncol=1, rowid=False, sum_primal=None):
    t = rows[0][0].shape[0]
    tm = min(tm, t)
    nr, npar, nc, nct, ndr = len(rows), len(params), len(consts), len(cts), len(d_rows)

    def body(*refs):
        i = pl.program_id(1)
        r = [refs[k][...].astype(f32) for k in range(nr)]
        p = [refs[nr + k][0] for k in range(npar)]
        c = [refs[nr + npar + k][0] for k in range(nc)]
        g = [refs[nr + npar + nc + k][...].astype(f32) for k in range(nct)]
        orefs = refs[nr + npar + nc + nct:]
        extra = ()
        if rowid:
            extra = (i * tm + lax.broadcasted_iota(jnp.int32, (tm, 1), 0),)
        res, vjp = jax.vjp(lambda *v: fn(*extra, *v, *c), *r, *p)
        grads = vjp(tuple(g))
        for k, (idx, dt) in enumerate(d_rows):
            orefs[k][...] = grads[idx].astype(dt)
        acc = [grads[nr + k] for k in range(npar)]
        if sum_primal is not None:
            acc.append(jnp.sum(res[sum_primal], axis=0, keepdims=True))

        @pl.when(i == 0)
        def _():
            for k, a in enumerate(acc):
                orefs[ndr + k][0] = a

        @pl.when(i > 0)
        def _():
            for k, a in enumerate(acc):
                orefs[ndr + k][0] = orefs[ndr + k][0] + a

    in_specs = [_row_spec(tm, w // ncol, off) for (_, off, w) in rows]
    in_specs += [_slab_spec(a) for a in list(params) + list(consts)]
    in_specs += [_row_spec(tm, w // ncol, off) for (_, off, w) in cts]
    out_specs = [pl.BlockSpec((tm, rows[idx][2] // ncol), lambda j, i: (i, j)) for (idx, _) in d_rows]
    out_shape = [jax.ShapeDtypeStruct((t, rows[idx][2]), dt) for (idx, dt) in d_rows]
    for a in params:
        out_specs.append(pl.BlockSpec((1,) + a.shape[1:], lambda j, i: (j, 0, 0)))
        out_shape.append(jax.ShapeDtypeStruct(a.shape, f32))
    if sum_primal is not None:
        w = cts[sum_primal][2]
        out_specs.append(pl.BlockSpec((1, 1, w // ncol), lambda j, i: (j, 0, 0)))
        out_shape.append(jax.ShapeDtypeStruct((ncol, 1, w // ncol), f32))
    return pl.pallas_call(
        body, name=name, grid=(ncol, t // tm), in_specs=in_specs, out_specs=out_specs, out_shape=out_shape,
        compiler_params=_cparams(("parallel", "arbitrary")),
    )(*[r[0] for r in rows], *params, *consts, *[c[0] for c in cts])


VM = pl.BlockSpec(memory_space=pltpu.VMEM)


def whole(fn, ins, outs, *, name):
    def body(*refs):
        res = fn(*[r[...] for r in refs[:len(ins)]])
        for k, o in enumerate(refs[len(ins):]):
            o[...] = res[k]
    return pl.pallas_call(body, name=name, in_specs=[VM] * len(ins), out_specs=[VM] * len(outs),
                          out_shape=[jax.ShapeDtypeStruct(s, f32) for s in outs],
                          compiler_params=_cparams())(*ins)


def whole_vjp(fn, ins, cts, *, name):
    n = len(ins)

    def body(*refs):
        _, vjp = jax.vjp(fn, *[r[...] for r in refs[:n]])
        grads = vjp(tuple(r[...] for r in refs[n:n + len(cts)]))
        for k, o in enumerate(refs[n + len(cts):]):
            o[...] = grads[k]
    return pl.pallas_call(body, name=name, in_specs=[VM] * (n + len(cts)), out_specs=[VM] * n,
                          out_shape=[jax.ShapeDtypeStruct(a.shape, f32) for a in ins],
                          compiler_params=_cparams())(*ins, *cts)


def ln_fn(x, w):
    return (x * lax.rsqrt(jnp.mean(x * x, axis=-1, keepdims=True) + EPS) * w,)


def ln_res_fn(x, w):
    return ln_fn(x, w)[0], x


def loss_fn(x, tgt, w):
    y = ln_fn(x, w)[0]
    return (0.5 * jnp.mean(jnp.square(y - tgt), axis=-1, keepdims=True),)


def s5_prep_fn(lam_re, lam_im, log_step, b_re, b_im):
    step = jnp.exp(log_step)
    mag = jnp.exp(lam_re * step)
    ang = lam_im * step
    abar_re = mag * jnp.cos(ang)
    abar_im = mag * jnp.sin(ang)
    num_re = abar_re - 1.0
    num_im = abar_im
    den = lam_re * lam_re + lam_im * lam_im
    coef_re = (num_re * lam_re + num_im * lam_im) / den
    coef_im = (num_im * lam_re - num_re * lam_im) / den
    bbar_re = coef_re * b_re - coef_im * b_im
    bbar_im = coef_re * b_im + coef_im * b_re
    return abar_re, abar_im, bbar_re, bbar_im


def lb_prep_fn(r0, r1, r2, r3):
    m = jnp.maximum(jnp.maximum(r0, r1), jnp.maximum(r2, r3))
    e0, e1, e2, e3 = jnp.exp(r0 - m), jnp.exp(r1 - m), jnp.exp(r2 - m), jnp.exp(r3 - m)
    s = e0 + e1 + e2 + e3
    p0, p1, p2, p3 = e0 / s, e1 / s, e2 / s, e3 / s
    c1 = p0 + p1
    c2 = c1 + p2
    c3 = c2 + p3
    return p0 - p0, c1 - p0, c2 - p0, c3 - p0


def s5_post_fn(yssm, u, ga, d, wglu, bglu):
    y = jax.nn.gelu(yssm + d * u)
    y = y * jax.nn.sigmoid(bdot(y, wglu) + bglu)
    return (y * jax.nn.silu(ga),)


def rg_gate_fn(tglob, xc, wa, ba, wx, bx, lam):
    r = jax.nn.sigmoid(bdot(xc, wa) + ba)
    i = jax.nn.sigmoid(bdot(xc, wx) + bx)
    log_a = -RG_C * r * jax.nn.softplus(-lam)
    a = jnp.exp(log_a)
    mult = jnp.sqrt(-jnp.tanh(log_a) * (a * a + 1.0))
    mult = jnp.where(tglob == 0, 1.0, mult)
    return a, mult * (i * xc)


def hg_pre_fn(q, fl, lb, tri, tri_t):
    f = lb + (1.0 - lb) * jax.nn.sigmoid(fl)
    return jax.nn.silu(q), 1.0 - f, cdot(tri, tri_t, jnp.log(f))


def branch_prep_fn(hb, gb, oc, gc, nw):
    yb = hb * jax.nn.silu(gb)
    on = oc * lax.rsqrt(jnp.mean(oc * oc, axis=-1, keepdims=True) + EPS) * nw
    return yb, on * jax.nn.silu(gc)


def merge_fn(b0, b1, b2, g0, g1, g2):
    return (jax.nn.sigmoid(g0) * b0 + jax.nn.sigmoid(g1) * b1 + jax.nn.sigmoid(g2) * b2,)


S5_TB = 512
S5_SC = 512


SEG = 8


def _shift(v, k, pos, period, reverse, fill):
    if reverse:
        return jnp.where(pos < period - k, pltpu.roll(v, v.shape[0] - k, 0), fill)
    return jnp.where(pos >= k, pltpu.roll(v, k, 0), fill)


def _cmul(ar, ai, br, bi):
    return ar * br - ai * bi, ar * bi + ai * br


def _edge_rows(v, first):
    r0 = 0 if first else SEG - 1
    return jnp.concatenate([v[r:r + 1, :] for r in range(r0, v.shape[0], SEG)], axis=0)


def _spread(s):
    return jnp.concatenate([jnp.broadcast_to(s[g:g + 1, :], (SEG, s.shape[1])) for g in range(s.shape[0])], axis=0)


def lti_scan(xr, xi, ar, ai, cr, ci, reverse=False):
    n = xr.shape[0]
    g = n // SEG
    sub = lax.broadcasted_iota(jnp.int32, (n, 1), 0) & (SEG - 1)
    sub8 = lax.broadcasted_iota(jnp.int32, (SEG, 1), 0)
    grow = lax.broadcasted_iota(jnp.int32, (g, 1), 0)
    pr, pi_ = ar, ai
    wr, wi = jnp.broadcast_to(ar, (SEG, ar.shape[1])), jnp.broadcast_to(ai, (SEG, ai.shape[1]))
    k = 1
    while k < SEG:
        tr, ti = _cmul(pr, pi_, _shift(xr, k, sub, SEG, reverse, 0.0), _shift(xi, k, sub, SEG, reverse, 0.0))
        xr, xi = xr + tr, xi + ti
        pr, pi_ = _cmul(pr, pi_, pr, pi_)
        wr, wi = _cmul(wr, wi, _shift(wr, k, sub8, SEG, reverse, 1.0), _shift(wi, k, sub8, SEG, reverse, 0.0))
        k *= 2
    first, last = (g - 1, 0) if reverse else (0, g - 1)
    jr, ji = _cmul(pr, pi_, cr, ci)
    sr = _edge_rows(xr, reverse) + jnp.where(grow == first, jr, 0.0)
    si = _edge_rows(xi, reverse) + jnp.where(grow == first, ji, 0.0)
    k = 1
    while k < g:
        tr, ti = _cmul(pr, pi_, _shift(sr, k, grow, g, reverse, 0.0), _shift(si, k, grow, g, reverse, 0.0))
        sr, si = sr + tr, si + ti
        pr, pi_ = _cmul(pr, pi_, pr, pi_)
        k *= 2
    er, ei = _spread(_shift(sr, 1, grow, g, reverse, cr)), _spread(_shift(si, 1, grow, g, reverse, ci))
    tr, ti = _cmul(jnp.tile(wr, (g, 1)), jnp.tile(wi, (g, 1)), er, ei)
    return xr + tr, xi + ti, sr[last:last + 1, :], si[last:last + 1, :]


def tv_scan(aa, bb, carry, reverse=False):
    n = aa.shape[0]
    row = lax.broadcasted_iota(jnp.int32, (n, 1), 0)
    k = 1
    while k < n:
        bb = bb + aa * _shift(bb, k, row, n, reverse, 0.0)
        aa = aa * _shift(aa, k, row, n, reverse, 1.0)
        k *= 2
    h = bb + aa * carry
    last = 0 if reverse else n - 1
    return h, h[last:last + 1, :]


def s5_scan_fwd(z, bd_re, bd_im, cd_re, cd_im, a_re, a_im):
    t = z.shape[0]
    tb = min(S5_TB, t)

    def body(u_ref, bre, bim, cre, cim, are, aim, y_ref, xre_ref, xim_ref, car_re, car_im):
        @pl.when(pl.program_id(1) == 0)
        def _():
            car_re[...] = jnp.zeros_like(car_re)
            car_im[...] = jnp.zeros_like(car_im)

        u = u_ref[...].astype(bf16)
        xr, xi, car_re[...], car_im[...] = lti_scan(
            jnp.dot(u, bre[0], preferred_element_type=f32), jnp.dot(u, bim[0], preferred_element_type=f32),
            are[0], aim[0], car_re[...], car_im[...])
        xre_ref[...] = xr
        xim_ref[...] = xi
        y_ref[...] = (jnp.dot(xr.astype(bf16), cre[0], preferred_element_type=f32)
                      - jnp.dot(xi.astype(bf16), cim[0], preferred_element_type=f32))

    chunk = lambda r, c: pl.BlockSpec((1, r, c), lambda j, i: (j, 0, 0))
    return pl.pallas_call(
        body, name="s5_scan_fwd", grid=(NCH, t // tb),
        in_specs=[pl.BlockSpec((tb, LANE), lambda j, i: (i, C_UA // LANE + j)),
                  chunk(LANE, S5_SC), chunk(LANE, S5_SC), chunk(S5_SC, LANE), chunk(S5_SC, LANE),
                  chunk(1, S5_SC), chunk(1, S5_SC)],
        out_specs=[pl.BlockSpec((tb, LANE), lambda j, i: (i, j)),
                   pl.BlockSpec((tb, S5_SC), lambda j, i: (i, j)),
                   pl.BlockSpec((tb, S5_SC), lambda j, i: (i, j))],
        out_shape=[jax.ShapeDtypeStruct((t, W_MIX), f32),
                   jax.ShapeDtypeStruct((t, NCH * S5_SC), f32),
                   jax.ShapeDtypeStruct((t, NCH * S5_SC), f32)],
        scratch_shapes=[pltpu.VMEM((1, S5_SC), f32)] * 2,
        compiler_params=_cparams(("parallel", "arbitrary")),
    )(z, bd_re.astype(bf16), bd_im.astype(bf16), cd_re.astype(bf16), cd_im.astype(bf16), a_re, a_im)


def s5_scan_bwd(dy, du1, z, xre, xim, bd_re, bd_im, cd_re, cd_im, a_re, a_im):
    t = z.shape[0]
    tb = min(S5_TB, t)
    nt = t // tb

    def body(dy_ref, du1_ref, u_ref, xre_ref, xim_ref, hre_ref, him_ref, bre, bim, cre, cim, are, aim,
             du_ref, dbre, dbim, dcre, dcim, dare, daim, car_re, car_im):
        step = pl.program_id(1)
        tt = nt - 1 - step

        @pl.when(step == 0)
        def _():
            car_re[...] = jnp.zeros_like(car_re)
            car_im[...] = jnp.zeros_like(car_im)

        nt_dims = (((1,), (1,)), ((), ()))
        tn_dims = (((0,), (0,)), ((), ()))
        dyb = dy_ref[...].astype(bf16)
        row = lax.broadcasted_iota(jnp.int32, (tb, 1), 0)
        xr, xi = xre_ref[...], xim_ref[...]
        ar, ai = are[0], aim[0]
        lr, li, car_re[...], car_im[...] = lti_scan(
            lax.dot_general(dyb, cre[0], nt_dims, preferred_element_type=f32),
            -lax.dot_general(dyb, cim[0], nt_dims, preferred_element_type=f32),
            ar, -ai, car_re[...], car_im[...], reverse=True)
        lrb, lib = lr.astype(bf16), li.astype(bf16)
        ub = u_ref[...].astype(bf16)
        du = (lax.dot_general(lrb, bre[0], nt_dims, preferred_element_type=f32)
              + lax.dot_general(lib, bim[0], nt_dims, preferred_element_type=f32))
        du_ref[...] = (du + du1_ref[...].astype(f32)).astype(du_ref.dtype)
        live = (tt > 0).astype(f32)
        xpr = jnp.where(row == 0, hre_ref[7:8, :] * live, pltpu.roll(xr, 1, 0))
        xpi = jnp.where(row == 0, him_ref[7:8, :] * live, pltpu.roll(xi, 1, 0))
        acc = [
            lax.dot_general(ub, lrb, tn_dims, preferred_element_type=f32),
            lax.dot_general(ub, lib, tn_dims, preferred_element_type=f32),
            lax.dot_general(xr.astype(bf16), dyb, tn_dims, preferred_element_type=f32),
            -lax.dot_general(xi.astype(bf16), dyb, tn_dims, preferred_element_type=f32),
            jnp.sum(lr * xpr + li * xpi, axis=0, keepdims=True),
            jnp.sum(li * xpr - lr * xpi, axis=0, keepdims=True),
        ]
        outs = [dbre, dbim, dcre, dcim, dare, daim]

        @pl.when(step == 0)
        def _():
            for o, a in zip(outs, acc):
                o[0] = a

        @pl.when(step > 0)
        def _():
            for o, a in zip(outs, acc):
                o[0] = o[0] + a

    chunk = lambda r, c: pl.BlockSpec((1, r, c), lambda j, i: (j, 0, 0))
    rev = lambda w, base=0: pl.BlockSpec((tb, w), lambda j, i: (nt - 1 - i, base + j))
    halo = pl.BlockSpec((8, S5_SC), lambda j, i: (jnp.maximum((nt - 1 - i) * (tb // 8) - 1, 0), j))
    return pl.pallas_call(
        body, name="s5_scan_bwd", grid=(NCH, nt),
        in_specs=[rev(LANE), rev(LANE), rev(LANE, C_UA // LANE), rev(S5_SC), rev(S5_SC), halo, halo,
                  chunk(LANE, S5_SC), chunk(LANE, S5_SC), chunk(S5_SC, LANE), chunk(S5_SC, LANE),
                  chunk(1, S5_SC), chunk(1, S5_SC)],
        out_specs=[rev(LANE), chunk(LANE, S5_SC), chunk(LANE, S5_SC), chunk(S5_SC, LANE), chunk(S5_SC, LANE),
                   chunk(1, S5_SC), chunk(1, S5_SC)],
        out_shape=[jax.ShapeDtypeStruct((t, W_MIX), bf16),
                   jax.ShapeDtypeStruct((NCH, LANE, S5_SC), f32), jax.ShapeDtypeStruct((NCH, LANE, S5_SC), f32),
                   jax.ShapeDtypeStruct((NCH, S5_SC, LANE), f32), jax.ShapeDtypeStruct((NCH, S5_SC, LANE), f32),
                   jax.ShapeDtypeStruct((NCH, 1, S5_SC), f32), jax.ShapeDtypeStruct((NCH, 1, S5_SC), f32)],
        scratch_shapes=[pltpu.VMEM((1, S5_SC), f32)] * 2,
        compiler_params=_cparams(("parallel", "arbitrary")),
    )(dy, du1, z, xre, xim, xre, xim, bd_re.astype(bf16), bd_im.astype(bf16), cd_re.astype(bf16),
      cd_im.astype(bf16), a_re, a_im)


RG_TB = 512


def _rg_conv(ext, w, cb, tb):
    acc = cb + w[3:4, :] * ext[8:, :]
    for k in range(3):
        acc = acc + w[k:k + 1, :] * pltpu.roll(ext, 3 - k, 0)[8:, :]
    return acc


def rg_fwd(z, cw, cb, gate_p):
    t = z.shape[0]
    tb = min(RG_TB, t)

    def body(x_ref, halo_ref, cw_ref, cb_ref, wa, ba, wx, bx, lam, xc_ref, a_ref, h_ref, car):
        i = pl.program_id(1)

        @pl.when(i == 0)
        def _():
            car[...] = jnp.zeros_like(car)

        ext = jnp.concatenate([halo_ref[...] * (i > 0).astype(f32), x_ref[...]], axis=0)
        xc = _rg_conv(ext, cw_ref[0], cb_ref[0], tb)
        tglob = i * tb + lax.broadcasted_iota(jnp.int32, (tb, 1), 0)
        a, b = rg_gate_fn(tglob, xc, wa[0], ba[0], wx[0], bx[0], lam[0])
        xc_ref[...] = xc
        a_ref[...] = a
        h_ref[...], car[...] = tv_scan(a, b, car[...])

    base = C_XB // LANE
    out = pl.BlockSpec((tb, LANE), lambda j, i: (i, j))
    return pl.pallas_call(
        body, name="rg_fwd", grid=(NCH, t // tb),
        in_specs=[pl.BlockSpec((tb, LANE), lambda j, i: (i, base + j)),
                  pl.BlockSpec((8, LANE), lambda j, i: (jnp.maximum(i * (tb // 8) - 1, 0), base + j)),
                  _slab_spec(cw), _slab_spec(cb)] + [_slab_spec(p) for p in gate_p],
        out_specs=[out, out, out], out_shape=[jax.ShapeDtypeStruct((t, W_MIX), f32)] * 3,
        scratch_shapes=[pltpu.VMEM((1, LANE), f32)],
        compiler_params=_cparams(("parallel", "arbitrary")),
    )(z, z, cw, cb, *gate_p)


def rg_bwd(dh, z, xc, a, h, cw, gate_p):
    t = z.shape[0]
    tb = min(RG_TB, t)
    nt = t // tb

    def body(g_ref, a_ref, an_ref, h_ref, hp_ref, xc_ref, x_ref, xh_ref, cw_ref, wa, ba, wx, bx, lam,
             dx_ref, dcw_ref, dcb_ref, dwa, dba, dwx, dbx, dlam, car, later):
        step = pl.program_id(1)
        tt = nt - 1 - step

        @pl.when(step == 0)
        def _():
            car[...] = jnp.zeros_like(car)
            later[...] = jnp.zeros_like(later)

        row = lax.broadcasted_iota(jnp.int32, (tb, 1), 0)
        an = an_ref[0:1, :] * (tt < nt - 1).astype(f32)
        aa = jnp.where(row == tb - 1, an, pltpu.roll(a_ref[...], tb - 1, 0))
        lmb, car[...] = tv_scan(aa, g_ref[...], car[...], reverse=True)
        hp = jnp.where(row == 0, hp_ref[7:8, :] * (tt > 0).astype(f32), pltpu.roll(h_ref[...], 1, 0))
        tglob = tt * tb + row
        _, vjp = jax.vjp(lambda *v: rg_gate_fn(tglob, *v), xc_ref[...], wa[0], ba[0], wx[0], bx[0], lam[0])
        g, *dp = vjp((lmb * hp, lmb))
        gext = jnp.concatenate([g, later[...]], axis=0)
        later[...] = g[0:8, :]
        xext = jnp.concatenate([xh_ref[...] * (tt > 0).astype(f32), x_ref[...]], axis=0)
        w = cw_ref[0]
        dx = w[3:4, :] * g
        taps = [None] * RG_CONV
        taps[3] = jnp.sum(g * xext[8:, :], axis=0, keepdims=True)
        for k in range(3):
            s = 3 - k
            dx = dx + w[k:k + 1, :] * pltpu.roll(gext, tb + 8 - s, 0)[:tb, :]
            taps[k] = jnp.sum(g * pltpu.roll(xext, s, 0)[8:, :], axis=0, keepdims=True)
        dx_ref[...] = dx.astype(dx_ref.dtype)
        acc = [jnp.concatenate(taps, axis=0), jnp.sum(g, axis=0, keepdims=True)] + dp
        outs = [dcw_ref, dcb_ref, dwa, dba, dwx, dbx, dlam]

        @pl.when(step == 0)
        def _():
            for o, v in zip(outs, acc):
                o[0] = v

        @pl.when(step > 0)
        def _():
            for o, v in zip(outs, acc):
                o[0] = o[0] + v

    base = C_XB // LANE
    rev = lambda b=0: pl.BlockSpec((tb, LANE), lambda j, i: (nt - 1 - i, b + j))
    nxt = pl.BlockSpec((8, LANE), lambda j, i: (jnp.minimum((nt - i) * (tb // 8), t // 8 - 1), j))
    prv = lambda b=0: pl.BlockSpec((8, LANE), lambda j, i: (jnp.maximum((nt - 1 - i) * (tb // 8) - 1, 0), b + j))
    params = [cw] + list(gate_p)
    grads = [jax.ShapeDtypeStruct(s, f32) for s in [cw.shape, (NCH, 1, LANE)] + [p.shape for p in gate_p]]
    return pl.pallas_call(
        body, name="rg_bwd", grid=(NCH, nt),
        in_specs=[rev(), rev(), nxt, rev(), prv(), rev(), rev(base), prv(base)] + [_slab_spec(p) for p in params],
        out_specs=[rev()] + [_slab_spec(p) for p in grads], out_shape=[jax.ShapeDtypeStruct((t, W_MIX), bf16)] + grads,
        scratch_shapes=[pltpu.VMEM((1, LANE), f32), pltpu.VMEM((8, LANE), f32)],
        compiler_params=_cparams(("parallel", "arbitrary")),
    )(dh, a, a, h, h, xc, z, z, *params)


HG_TB = 256
HALF = HG_SUB // 2


def _heads(v):
    return jnp.stack([v[:, LANE * h:LANE * (h + 1)] for h in range(HG_HEADS)])


def _unheads(v):
    return jnp.concatenate([v[h] for h in range(HG_HEADS)], axis=-1)


def _bmm(eq, a, b):
    return jnp.einsum(eq, a.astype(bf16), b.astype(bf16), preferred_element_type=f32)


def hg_chunk_fwd(qs, kk, gcum, z):
    t = qs.shape[0]
    tb = min(HG_TB, t)
    nc = tb // HG_SUB

    def body(q_ref, k_ref, g_ref, v_ref, o_ref, sall_ref, st_ref):
        @pl.when(pl.program_id(0) == 0)
        def _():
            st_ref[...] = jnp.zeros_like(st_ref)

        ri = lax.broadcasted_iota(jnp.int32, (1, HALF, 1), 1)

        def chunk(c, carry):
            rows = pl.ds(pl.multiple_of(c * HG_SUB, HG_SUB), HG_SUB)
            q, k, g, v = _heads(q_ref[rows, :]), _heads(k_ref[rows, :]), _heads(g_ref[rows, :]), _heads(v_ref[rows, :])
            st = st_ref[...]
            sall_ref[c] = st
            o = _bmm('htk,hvk->htv', q * jnp.exp(g), st)
            halves = [[q[:, :HALF], g[:, :HALF], o[:, :HALF]], [q[:, HALF:], g[:, HALF:], o[:, HALF:]]]
            for s in range(HG_SUB):
                grow, krow, vrow = g[:, s:s + 1, :], k[:, s:s + 1, :], v[:, s:s + 1, :]
                for h in range(s // HALF, 2):
                    qh, gh, oh = halves[h]
                    p = jnp.exp(jnp.minimum(gh - grow, 0.0))
                    if s // HALF == h:
                        p = jnp.where(ri >= s - h * HALF, p, 0.0)
                    halves[h][2] = oh + jnp.sum(qh * krow * p, axis=-1, keepdims=True) * vrow
            o = jnp.concatenate([halves[0][2], halves[1][2]], axis=1)
            gl = g[:, HG_SUB - 1:HG_SUB, :]
            st_ref[...] = st * jnp.exp(gl) + _bmm('htv,htk->hvk', v, k * jnp.exp(gl - g))
            o_ref[rows, :] = _unheads(o)
            return carry

        lax.fori_loop(0, nc, chunk, 0, unroll=2)

    spec = lambda base=0: pl.BlockSpec((tb, W_MIX), lambda i: (i, base))
    return pl.pallas_call(
        body, name="hg_chunk_fwd", grid=(t // tb,),
        in_specs=[spec(), spec(), spec(), spec(C_I // W_MIX)],
        out_specs=[spec(), pl.BlockSpec((nc, HG_HEADS, HG_DK, HG_DK), lambda i: (i, 0, 0, 0))],
        out_shape=[jax.ShapeDtypeStruct((t, W_MIX), f32),
                   jax.ShapeDtypeStruct((t // HG_SUB, HG_HEADS, HG_DK, HG_DK), f32)],
        scratch_shapes=[pltpu.VMEM((HG_HEADS, HG_DK, HG_DK), f32)],
        compiler_params=_cparams(("arbitrary",)),
    )(qs, kk, gcum, z)


def hg_chunk_bwd(do, qs, kk, gcum, z, sall):
    t = qs.shape[0]
    tb = min(HG_TB, t)
    nc = tb // HG_SUB
    nt = t // tb

    def body(do_ref, q_ref, k_ref, g_ref, v_ref, sall_ref, dq_ref, dk_ref, dg_ref, dv_ref, dst_ref):
        @pl.when(pl.program_id(0) == 0)
        def _():
            dst_ref[...] = jnp.zeros_like(dst_ref)

        ri = lax.broadcasted_iota(jnp.int32, (1, HALF, 1), 1)
        ri_chunk = lax.broadcasted_iota(jnp.int32, (1, HG_SUB, 1), 1)

        def chunk(cc, carry):
            c = nc - 1 - cc
            rows = pl.ds(pl.multiple_of(c * HG_SUB, HG_SUB), HG_SUB)
            q, k, g, v = _heads(q_ref[rows, :]), _heads(k_ref[rows, :]), _heads(g_ref[rows, :]), _heads(v_ref[rows, :])
            d_o = _heads(do_ref[rows, :])
            st = sall_ref[c]
            dsn = dst_ref[...]
            eg = jnp.exp(g)
            qe = q * eg
            gl = g[:, HG_SUB - 1:HG_SUB, :]
            egl = jnp.exp(gl)
            dec = jnp.exp(gl - g)
            kd = k * dec
            dqe = _bmm('htv,hvk->htk', d_o, st)
            dst_ref[...] = _bmm('htv,htk->hvk', d_o, qe) + dsn * egl
            dgl_dec = jnp.sum(dsn * st, axis=1, keepdims=True) * egl
            dv = _bmm('htk,hvk->htv', kd, dsn)
            dkd = _bmm('htv,hvk->htk', v, dsn)
            tq, tg, tdo = [q[:, :HALF], q[:, HALF:]], [g[:, :HALF], g[:, HALF:]], [d_o[:, :HALF], d_o[:, HALF:]]
            a1 = [jnp.zeros_like(tq[0]), jnp.zeros_like(tq[1])]
            a2 = [jnp.zeros_like(tq[0]), jnp.zeros_like(tq[1])]
            dvh = [dv[:, :HALF], dv[:, HALF:]]
            for s in range(HG_SUB):
                grow, krow, vrow = g[:, s:s + 1, :], k[:, s:s + 1, :], v[:, s:s + 1, :]
                sh, sr = s // HALF, s % HALF
                dv_s, a2_s = 0.0, 0.0
                for h in range(sh, 2):
                    p = jnp.exp(jnp.minimum(tg[h] - grow, 0.0))
                    if sh == h:
                        p = jnp.where(ri >= sr, p, 0.0)
                    col = jnp.sum(tq[h] * krow * p, axis=-1, keepdims=True)
                    t1 = jnp.sum(tdo[h] * vrow, axis=-1, keepdims=True) * p
                    a1[h] = a1[h] + t1 * krow
                    dv_s = dv_s + jnp.sum(col * tdo[h], axis=1, keepdims=True)
                    a2_s = a2_s + jnp.sum(t1 * tq[h], axis=1, keepdims=True)
                dvh[sh] = jnp.where(ri == sr, dvh[sh] + dv_s, dvh[sh])
                a2[sh] = jnp.where(ri == sr, a2_s, a2[sh])
            a1, a2 = jnp.concatenate(a1, axis=1), jnp.concatenate(a2, axis=1)
            dv = jnp.concatenate(dvh, axis=1)
            dgl = jnp.sum(dkd * kd, axis=1, keepdims=True) + dgl_dec
            dg = dqe * qe + q * a1 - k * a2 - dkd * kd
            dg = jnp.where(ri_chunk == HG_SUB - 1, dg + dgl, dg)
            dq_ref[rows, :] = _unheads(dqe * eg + a1)
            dk_ref[rows, :] = _unheads(dkd * dec + a2)
            dg_ref[rows, :] = _unheads(dg)
            dv_ref[rows, :] = _unheads(dv).astype(dv_ref.dtype)
            return carry

        lax.fori_loop(0, nc, chunk, 0, unroll=2)

    spec = lambda base=0: pl.BlockSpec((tb, W_MIX), lambda i: (nt - 1 - i, base))
    return pl.pallas_call(
        body, name="hg_chunk_bwd", grid=(nt,),
        in_specs=[spec(), spec(), spec(), spec(), spec(C_I // W_MIX),
                  pl.BlockSpec((nc, HG_HEADS, HG_DK, HG_DK), lambda i: (nt - 1 - i, 0, 0, 0))],
        out_specs=[spec(), spec(), spec(), spec()],
        out_shape=[jax.ShapeDtypeStruct((t, W_MIX), f32)] * 3 + [jax.ShapeDtypeStruct((t, W_MIX), bf16)],
        scratch_shapes=[pltpu.VMEM((HG_HEADS, HG_DK, HG_DK), f32)],
        compiler_params=_cparams(("arbitrary",)),
    )(do, qs, kk, gcum, z, sall)


def adamw(w, m, v, slots, *, name, tr):
    nl, r, c = w.shape
    tr = min(tr, r)
    flat = [a for per_layer in slots for a in per_layer]
    c1 = 1.0 / (1.0 - ADAM_B1 ** ADAM_STEP)
    c2 = 1.0 / (1.0 - ADAM_B2 ** ADAM_STEP)

    def body(*refs):
        w_ref, m_ref, v_ref = refs[:3]
        s_refs = list(refs[3:3 + len(flat)])
        g_ref, d_ref, mo_ref, vo_ref = refs[3 + len(flat):]
        for l in range(nl):
            parts = [s_refs.pop(0) for _ in slots[l]]
            g = None
            for p in parts:
                for s in range(p.shape[0]):
                    term = p[s].astype(f32)
                    g = term if g is None else g + term
            mn = ADAM_B1 * m_ref[l] + (1.0 - ADAM_B1) * g
            vn = ADAM_B2 * v_ref[l] + (1.0 - ADAM_B2) * (g * g)
            g_ref[l] = g
            mo_ref[l] = mn
            vo_ref[l] = vn
            d_ref[l] = -ADAM_LR * ((mn * c1) / (jnp.sqrt(vn * c2) + ADAM_EPS) + ADAM_WD * w_ref[l])

    full = pl.BlockSpec((nl, tr, c), lambda i: (0, i, 0))
    slot = [pl.BlockSpec((a.shape[0], tr, c), lambda i: (0, i, 0)) for a in flat]
    return pl.pallas_call(
        body, name=name, grid=(r // tr,), in_specs=[full] * 3 + slot, out_specs=[full] * 4,
        out_shape=[jax.ShapeDtypeStruct(w.shape, f32)] * 4, compiler_params=_cparams(("parallel",)),
    )(w, m, v, *flat)


def _slab(ref, axis, idx, n):
    return ref.at[tuple([slice(None)] * axis + [pl.ds(idx * n, n)])]


def all_gather(x, axis, *, name):
    n = x.shape[axis]
    out_shape = x.shape[:axis] + (N_DEV * n,) + x.shape[axis + 1:]

    def body(x_ref, out_ref, send_sems, recv_sems, local_sem):
        xx, yy, cc = lax.axis_index("x"), lax.axis_index("y"), lax.axis_index("c")
        me, sibling = (xx, yy, cc), (xx, yy, 1 - cc)
        chips = [(1 - xx, yy), (xx, 1 - yy), (1 - xx, 1 - yy)]

        def slab(px, py, pc):
            return _slab(out_ref, axis, 4 * px + 2 * py + pc, n)

        def copy(k, block, to, src=None):
            return pltpu.make_async_remote_copy(
                src_ref=slab(*block) if src is None else src, dst_ref=slab(*block),
                send_sem=send_sems.at[k], recv_sem=recv_sems.at[k], device_id=to, device_id_type=MESH)

        mine = pltpu.make_async_copy(x_ref, slab(*me), local_sem)
        mine.start()
        first = [copy(0, me, sibling, src=x_ref)]
        first += [copy(1 + j, me, (*chip, cc), src=x_ref) for j, chip in enumerate(chips)]
        for cp in first:
            cp.start()
        passed = [copy(4 + j, (*chip, cc), sibling) for j, chip in enumerate(chips)]
        for j, chip in enumerate(chips):
            copy(1 + j, (*chip, cc), me).wait_recv()
            passed[j].start()
        copy(0, sibling, me).wait_recv()
        for j, chip in enumerate(chips):
            copy(4 + j, (*chip, 1 - cc), me).wait_recv()
        for cp in first + passed:
            cp.wait_send()
        mine.wait()

    return pl.pallas_call(
        body, name=name, out_shape=jax.ShapeDtypeStruct(out_shape, x.dtype), in_specs=[ANY], out_specs=ANY,
        scratch_shapes=[pltpu.SemaphoreType.DMA((7,)), pltpu.SemaphoreType.DMA((7,)), pltpu.SemaphoreType.DMA],
    )(x)


N_CHIP = 4


HBM = pl.BlockSpec(memory_space=pltpu.HBM)
SEM = pl.BlockSpec(memory_space=pltpu.SEMAPHORE)
EFFECT = pltpu.SideEffectType.DATAFLOW_SIDE_EFFECTING
TOKEN = jax.ShapeDtypeStruct((8, LANE), f32)


def _in_hbm(a):
    return pltpu.with_memory_space_constraint(a, pltpu.HBM)


def pair_sums(g, axis, *, name):
    n = g.shape[axis] // N_DEV
    slab_shape = g.shape[:axis] + (n,) + g.shape[axis + 1:]
    cols = slab_shape[-1]
    rows = math.prod(slab_shape[:-1])
    col_slabs = axis == g.ndim - 1
    assert col_slabs or (axis == 0 and g.ndim == 2)

    def swap_body(g_ref, got_ref, send_sems, recv_sems):
        xx, yy, cc = lax.axis_index("x"), lax.axis_index("y"), lax.axis_index("c")
        copies = [pltpu.make_async_remote_copy(
            src_ref=_slab(g_ref, axis, 2 * q + 1 - cc, n), dst_ref=got_ref.at[q],
            send_sem=send_sems.at[q], recv_sem=recv_sems.at[q], device_id=(xx, yy, 1 - cc), device_id_type=MESH)
            for q in range(N_CHIP)]
        for cp in copies:
            cp.start()
        for cp in copies:
            cp.wait()

    got = pl.pallas_call(
        swap_body, name=name + "_swap", out_shape=jax.ShapeDtypeStruct((N_CHIP,) + slab_shape, g.dtype),
        in_specs=[ANY], out_specs=ANY, scratch_shapes=[pltpu.SemaphoreType.DMA((N_CHIP,))] * 2,
    )(g)

    tr = min(256, rows)

    def add_body(a0_ref, a1_ref, b_ref, pair_ref, own_ref):
        xx, yy, cc = lax.axis_index("x"), lax.axis_index("y"), lax.axis_index("c")
        mine = jnp.where(cc == 0, a0_ref[...], a1_ref[...])
        s = (mine.astype(f32) + b_ref[0].astype(f32)).astype(bf16)
        pair_ref[0] = s

        @pl.when(pl.program_id(1) == 2 * xx + yy)
        def _():
            own_ref[0] = s

    if col_slabs:
        a_spec = lambda c: pl.BlockSpec((tr, cols), lambda i, q: (i, 2 * q + c))
    else:
        a_spec = lambda c: pl.BlockSpec((tr, cols), lambda i, q: ((2 * q + c) * (n // tr) + i, 0))
    by_chip = pl.BlockSpec((1, tr, cols), lambda i, q: (q, i, 0))
    g2 = g.reshape(-1, g.shape[-1])
    pair, own = pl.pallas_call(
        add_body, name=name + "_add", grid=(rows // tr, N_CHIP), in_specs=[a_spec(0), a_spec(1), by_chip],
        out_specs=[by_chip, pl.BlockSpec((1, tr, cols), lambda i, q: (0, i, 0))],
        out_shape=[jax.ShapeDtypeStruct((N_CHIP, rows, cols), bf16), jax.ShapeDtypeStruct((1, rows, cols), bf16)],
        compiler_params=_cparams(("parallel", "arbitrary")),
    )(g2, g2, got.reshape(N_CHIP, rows, cols))
    return own, pair


def _send_copies(p_refs, land_refs, send_sems, recv_sems):
    xx, yy, cc = lax.axis_index("x"), lax.axis_index("y"), lax.axis_index("c")
    copies = []
    for t, (p, land) in enumerate(zip(p_refs, land_refs)):
        for k in range(1, N_CHIP):
            px = 1 - xx if k & 2 else xx
            py = 1 - yy if k & 1 else yy
            s = (N_CHIP - 1) * t + k - 1
            copies.append(pltpu.make_async_remote_copy(
                src_ref=p.at[2 * px + py], dst_ref=land.at[k - 1], send_sem=send_sems.at[s], recv_sem=recv_sems.at[s],
                device_id=(px, py, cc), device_id_type=MESH))
    return copies


def _direct_copies(axes, g_refs, land_refs, send_sems, recv_sems):
    xx, yy, cc = lax.axis_index("x"), lax.axis_index("y"), lax.axis_index("c")
    copies = []
    for t, (g_ref, land, axis) in enumerate(zip(g_refs, land_refs, axes)):
        n = g_ref.shape[axis] // N_DEV
        for k in range(1, N_DEV):
            px = 1 - xx if k & 4 else xx
            py = 1 - yy if k & 2 else yy
            pc = 1 - cc if k & 1 else cc
            s = (N_DEV - 1) * t + k - 1
            copies.append(pltpu.make_async_remote_copy(
                src_ref=_slab(g_ref, axis, 4 * px + 2 * py + pc, n), dst_ref=land.at[k - 1],
                send_sem=send_sems.at[s], recv_sem=recv_sems.at[s], device_id=(px, py, pc), device_id_type=MESH))
    return copies


def own_slab(g, axis, *, name):
    n = g.shape[axis] // N_DEV
    slab_shape = g.shape[:axis] + (n,) + g.shape[axis + 1:]
    cols, rows = slab_shape[-1], math.prod(slab_shape[:-1])
    tr = min(256, rows)

    def body(g_ref, o_ref):
        o_ref[0] = g_ref[...]

    def where(i):
        me = 4 * lax.axis_index("x") + 2 * lax.axis_index("y") + lax.axis_index("c")
        return (i, me) if axis == g.ndim - 1 else (me * (n // tr) + i, 0)

    return pl.pallas_call(
        body, name=name, grid=(rows // tr,), in_specs=[pl.BlockSpec((tr, cols), where)],
        out_specs=pl.BlockSpec((1, tr, cols), lambda i: (0, i, 0)),
        out_shape=jax.ShapeDtypeStruct((1, rows, cols), g.dtype), compiler_params=_cparams(("parallel",)),
    )(g.reshape(-1, g.shape[-1]))


def send_pairs_start(pairs, after, *, name, direct_axes=None):
    nt = len(pairs)
    if direct_axes is None:
        build = _send_copies
        lands = [lax.empty((N_CHIP - 1,) + p.shape[1:], p.dtype) for p in pairs]
    else:
        build = functools.partial(_direct_copies, direct_axes)
        lands = [lax.empty((N_DEV - 1,) + p.shape[:ax] + (p.shape[ax] // N_DEV,) + p.shape[ax + 1:], p.dtype)
                 for p, ax in zip(pairs, direct_axes)]

    def body(*refs):
        p_refs, land_refs = refs[:nt], refs[nt:2 * nt]
        send_sems, recv_sems = refs[2 * nt + 1], refs[2 * nt + 2]
        token = refs[-1]
        for cp in build(p_refs, land_refs, send_sems, recv_sems):
            cp.start()
        token[...] = jnp.zeros_like(token)

    nsem = sum(l.shape[0] for l in lands)
    outs = pl.pallas_call(
        body, name=name,
        out_shape=(pltpu.SemaphoreType.DMA((nsem,)), pltpu.SemaphoreType.DMA((nsem,)))
        + tuple(pltpu.HBM(a.shape, a.dtype) for a in list(pairs) + lands) + (TOKEN,),
        in_specs=[HBM] * (2 * nt) + [ANY], out_specs=(SEM, SEM) + (HBM,) * (2 * nt) + (VM,),
        input_output_aliases={i: 2 + i for i in range(2 * nt)},
        compiler_params=pltpu.CompilerParams(has_side_effects=EFFECT),
    )(*[_in_hbm(a) for a in list(pairs) + lands], after)
    return outs[:-1], outs[-1]


def send_pairs_wait(handles, after, *, name, direct_axes=None):
    send_sems, recv_sems = handles[0], handles[1]
    bufs = handles[2:]
    nt = len(bufs) // 2
    build = _send_copies if direct_axes is None else functools.partial(_direct_copies, direct_axes)

    def body(*refs):
        p_refs, land_refs = refs[:nt], refs[nt:2 * nt]
        send_sems, recv_sems = refs[2 * nt], refs[2 * nt + 1]
        for cp in build(p_refs, land_refs, send_sems, recv_sems):
            cp.wait_send()
            cp.wait_recv()

    outs = pl.pallas_call(
        body, name=name, out_shape=tuple(pltpu.HBM(a.shape, a.dtype) for a in bufs),
        in_specs=[HBM] * (2 * nt) + [SEM, SEM, ANY], out_specs=(HBM,) * (2 * nt),
        input_output_aliases={i: i for i in range(2 * nt)},
        compiler_params=pltpu.CompilerParams(has_side_effects=EFFECT),
    )(*bufs, send_sems, recv_sems, after)
    return outs[nt:]


def _gather_copies(x_refs, land_refs, axes, send_sems, recv_sems):
    xx, yy, cc = lax.axis_index("x"), lax.axis_index("y"), lax.axis_index("c")
    me = 4 * xx + 2 * yy + cc
    copies = []
    for t, (x_ref, land, axis) in enumerate(zip(x_refs, land_refs, axes)):
        n = x_ref.shape[axis]
        for k in range(1, N_DEV):
            px = 1 - xx if k & 4 else xx
            py = 1 - yy if k & 2 else yy
            pc = 1 - cc if k & 1 else cc
            s = (N_DEV - 1) * t + k - 1
            copies.append(pltpu.make_async_remote_copy(
                src_ref=x_ref, dst_ref=_slab(land, axis, me, n), send_sem=send_sems.at[s], recv_sem=recv_sems.at[s],
                device_id=(px, py, pc), device_id_type=MESH))
    return copies


def _place_own(x, axis, *, name):
    full = x.shape[:axis] + (N_DEV * x.shape[axis],) + x.shape[axis + 1:]
    lead = x.shape[0]
    tile = lead if axis == 0 else (256 if x.ndim == 2 and lead % 256 == 0 else 1 if x.ndim == 3 else lead)
    rest = (0,) * (x.ndim - 1)

    def body(land_ref, x_ref, o_ref):
        o_ref[...] = x_ref[...]

    def where(i):
        me = 4 * lax.axis_index("x") + 2 * lax.axis_index("y") + lax.axis_index("c")
        idx = [i] + list(rest)
        idx[axis] = me
        return tuple(idx)

    return pl.pallas_call(
        body, name=name, grid=(lead // tile,),
        in_specs=[ANY, pl.BlockSpec((tile,) + x.shape[1:], lambda i: (i,) + rest)],
        out_specs=pl.BlockSpec((tile,) + x.shape[1:], where),
        out_shape=jax.ShapeDtypeStruct(full, x.dtype), input_output_aliases={0: 0},
        compiler_params=_cparams(("arbitrary",)),
    )(lax.empty(full, x.dtype), x)


def gather_start(xs, axes, after, *, name):
    nt = len(xs)
    lands = [_place_own(x, axis, name=name + "_own") for x, axis in zip(xs, axes)]

    def body(*refs):
        x_refs, land_refs = refs[:nt], refs[nt:2 * nt]
        send_sems, recv_sems = refs[2 * nt + 1], refs[2 * nt + 2]
        token = refs[-1]
        for cp in _gather_copies(x_refs, land_refs, axes, send_sems, recv_sems):
            cp.start()
        token[...] = jnp.zeros_like(token)

    nsem = (N_DEV - 1) * nt
    outs = pl.pallas_call(
        body, name=name,
        out_shape=(pltpu.SemaphoreType.DMA((nsem,)), pltpu.SemaphoreType.DMA((nsem,)))
        + tuple(pltpu.HBM(a.shape, a.dtype) for a in list(xs) + lands) + (TOKEN,),
        in_specs=[HBM] * (2 * nt) + [ANY], out_specs=(SEM, SEM) + (HBM,) * (2 * nt) + (VM,),
        input_output_aliases={i: 2 + i for i in range(2 * nt)},
        compiler_params=pltpu.CompilerParams(has_side_effects=EFFECT),
    )(*[_in_hbm(a) for a in list(xs) + lands], after)
    return outs[:-1], outs[-1]


def gather_wait(handles, axes, after, *, name):
    send_sems, recv_sems = handles[0], handles[1]
    bufs = handles[2:]
    nt = len(bufs) // 2

    def body(*refs):
        x_refs, land_refs = refs[:nt], refs[nt:2 * nt]
        send_sems, recv_sems = refs[2 * nt], refs[2 * nt + 1]
        for cp in _gather_copies(x_refs, land_refs, axes, send_sems, recv_sems):
            cp.wait_send()
            cp.wait_recv()

    outs = pl.pallas_call(
        body, name=name, out_shape=tuple(pltpu.HBM(a.shape, a.dtype) for a in bufs),
        in_specs=[HBM] * (2 * nt) + [SEM, SEM, ANY], out_specs=(HBM,) * (2 * nt),
        input_output_aliases={i: i for i in range(2 * nt)},
        compiler_params=pltpu.CompilerParams(has_side_effects=EFFECT),
    )(*bufs, send_sems, recv_sems, after)
    return outs[nt:]


def _blockdiag(b, nb):
    j, _, r, c = b.shape
    eye = jnp.eye(nb, dtype=bool)[None, :, None, :, None]
    return jnp.where(eye, b[:, :, :, None, :], jnp.zeros((), b.dtype)).reshape(j, nb * r, nb * c)


def _diagblocks(d, nb):
    j, rr, cc = d.shape
    return jnp.einsum('jarac->jarc', d.reshape(j, nb, rr // nb, nb, cc // nb))


def _s5_b_dense(bbar):
    return _blockdiag(bbar.transpose(0, 2, 1).reshape(NCH, 8, S5_GROUP, S5_STATE), 8)


def _s5_b_undense(d):
    return _diagblocks(d, 8).reshape(S5_GROUPS, S5_GROUP, S5_STATE).transpose(0, 2, 1)


def _s5_c_dense(c):
    return _blockdiag(c.transpose(0, 2, 1).reshape(NCH, 8, S5_STATE, S5_GROUP), 8)


def _s5_c_undense(d):
    return _diagblocks(d, 8).reshape(S5_GROUPS, S5_STATE, S5_GROUP).transpose(0, 2, 1)


def _rg_dense(w):
    return _blockdiag(w.reshape(NCH, 2, RG_BLOCK, RG_BLOCK), 2)


def _rg_undense(d):
    return _diagblocks(d, 2).reshape(RG_BLOCKS, RG_BLOCK, RG_BLOCK)


def _chunks(v):
    return v.reshape(NCH, 1, LANE)


def _tri(tm):
    r = jnp.arange(tm)
    m = (r[:, None] >= r[None, :]) & (r[:, None] // HG_SUB == r[None, :] // HG_SUB)
    m = m.astype(f32)
    return m[None], m.T[None]


SMALL = ['norm_w', 's5_lambda_re', 's5_lambda_im', 's5_log_step', 's5_b_re', 's5_b_im', 's5_c_re', 's5_c_im',
         's5_d', 's5_b_glu', 'rg_conv_w', 'rg_conv_b', 'rg_w_a', 'rg_b_a', 'rg_w_x', 'rg_b_x', 'rg_lambda',
         'hg_lower_bounds', 'hg_norm_w', 'final_norm_w']
WEIGHTS = ['norm_w', 'w_in', 's5_lambda_re', 's5_lambda_im', 's5_log_step', 's5_b_re', 's5_b_im', 's5_c_re',
           's5_c_im', 's5_d', 's5_w_glu', 's5_b_glu', 'rg_conv_w', 'rg_conv_b', 'rg_w_a', 'rg_b_a', 'rg_w_x',
           'rg_b_x', 'rg_lambda', 'hg_lower_bounds', 'hg_norm_w', 'w_branch', 'w_out', 'final_norm_w']
PACK_ROWS = 512


def _pack(arrs):
    flat = jnp.concatenate([a.reshape(-1) for a in arrs])
    pad = (-flat.shape[0]) % (PACK_ROWS * LANE)
    return jnp.pad(flat, (0, pad)).reshape(1, -1, LANE)


def _unpack(buf, shapes):
    flat = buf.reshape(-1)
    out, off = [], 0
    for s in shapes:
        n = math.prod(s)
        out.append(flat[off:off + n].reshape(s))
        off += n
    return out


def _step(x, tgt, w, m, v):
    t = x.shape[0]
    tri, tri_t = _tri(min(LANE, t))
    me = 4 * lax.axis_index("x") + 2 * lax.axis_index("y") + lax.axis_index("c")

    big = ('w_in', 's5_w_glu', 'w_branch', 'w_out')
    big_axis = (1, 0, 2, 0)
    shards = lambda l: [w[k][l].astype(bf16) for k in big]
    win, wglu, wbr, wout = ([None] * DEPTH for _ in range(4))
    win[0] = all_gather(shards(0)[0], big_axis[0], name="ag_w_in")
    rest0_axis = big_axis[1:] + (1,)
    rest0, rest0_token = gather_start(shards(0)[1:] + [w['rg_conv_w'].reshape(DEPTH * RG_CONV, LANE)], rest0_axis,
                                      win[0], name="ag_start_0")

    lb_rows = [w['hg_lower_bounds'][l][None] for l in range(DEPTH)]
    lbs = whole(lb_prep_fn, lb_rows, [(1, W_MIX)] * DEPTH, name="lb_prep")

    saved = []
    for l in range(DEPTH):
        s = {}
        nw = w['norm_w'][l].reshape(1, 1, D_MODEL)
        (h,) = rowwise(ln_fn, [(x, 0, D_MODEL)], [nw], [], [(D_MODEL, bf16)], name="ln_fwd", tm=512)
        token = None
        if l + 1 < DEPTH:
            handles, token = gather_start(shards(l + 1), big_axis, rest0_token if l == 0 else x, name=f"ag_start_{l + 1}")
        z = mm(h, win[l], after=token, name="mm_in", tn=2048)
        if l == 0:
            wglu[0], wbr[0], wout[0], conv_w = gather_wait(rest0, rest0_axis, z, name="ag_wait_0")
            conv_w = conv_w.reshape(DEPTH, RG_CONV, W_MIX)
        s5p = [w['s5_lambda_re'][l][..., None], w['s5_lambda_im'][l][..., None], w['s5_log_step'][l][:, None, None],
               w['s5_b_re'][l], w['s5_b_im'][l]]
        gp = (S5_GROUPS, S5_STATE)
        abar_re, abar_im, bbar_re, bbar_im = whole(
            s5_prep_fn, s5p, [gp + (1,), gp + (1,), gp + (S5_GROUP,), gp + (S5_GROUP,)], name="s5_prep")
        a_re, a_im = abar_re.reshape(NCH, 1, S5_SC), abar_im.reshape(NCH, 1, S5_SC)
        bd_re, bd_im = _s5_b_dense(bbar_re), _s5_b_dense(bbar_im)
        cd_re, cd_im = _s5_c_dense(w['s5_c_re'][l]), _s5_c_dense(w['s5_c_im'][l])
        yssm, xre, xim = s5_scan_fwd(z, bd_re, bd_im, cd_re, cd_im, a_re, a_im)
        s5post_p = [w['s5_d'][l].reshape(1, 1, W_MIX), wglu[l].astype(f32)[None], w['s5_b_glu'][l].reshape(1, 1, W_MIX)]
        s5post_rows = [(yssm, 0, W_MIX), (z, C_UA, W_MIX), (z, C_GA, W_MIX)]
        (ya,) = rowwise(s5_post_fn, s5post_rows, s5post_p, [], [(W_MIX, bf16)], name="s5_post_fwd", tm=512)
        cw, cb = conv_w[l].reshape(RG_CONV, NCH, LANE).transpose(1, 0, 2), _chunks(w['rg_conv_b'][l])
        rg_p = [_rg_dense(w['rg_w_a'][l]), _chunks(w['rg_b_a'][l]), _rg_dense(w['rg_w_x'][l]),
                _chunks(w['rg_b_x'][l]), _chunks(w['rg_lambda'][l])]
        xc, ra, hb = rg_fwd(z, cw, cb, rg_p)
        hg_rows = [(z, C_Q, W_MIX), (z, C_F, W_MIX)]
        hg_p = [_chunks(lbs[l].reshape(W_MIX))]
        qs, kk, gcum = rowwise(hg_pre_fn, hg_rows, hg_p, [tri, tri_t], [(W_MIX, f32)] * 3, name="hg_pre_fwd", ncol=NCH, tm=TM_CHUNK)
        oc, sall = hg_chunk_fwd(qs, kk, gcum, z)
        bp_rows = [(hb, 0, W_MIX), (z, C_GB, W_MIX), (oc, 0, W_MIX), (z, C_GC, W_MIX)]
        bp_p = [_chunks(w['hg_norm_w'][l])]
        yb, yc = rowwise(branch_prep_fn, bp_rows, bp_p, [], [(W_MIX, bf16)] * 2, name="branch_prep_fwd", ncol=NCH, tm=TM_CHUNK)
        ys = [ya, yb, yc]
        br = [mm(ys[n], wbr[l][n], name="mm_branch", out_dtype=bf16, tn=2048) for n in range(N_BRANCH)]
        mg_rows = [(br[n], 0, D_MODEL) for n in range(N_BRANCH)] + [(z, C_GATE + n * D_MODEL, D_MODEL) for n in range(N_BRANCH)]
        (merged,) = rowwise(merge_fn, mg_rows, [], [], [(D_MODEL, bf16)], name="merge_fwd", ncol=2, tm=512)
        x_new = mm(merged, wout[l], add=x, name="mm_out")
        s.update(x=x, h=h, z=z, s5p=s5p, s5=(bd_re, bd_im, cd_re, cd_im, a_re, a_im), xre=xre, xim=xim,
                 s5post_rows=s5post_rows, s5post_p=s5post_p, cw=cw, xc=xc, rg_p=rg_p, ra=ra, hb=hb,
                 hg_rows=hg_rows, hg_p=hg_p, qs=qs, kk=kk, gcum=gcum, sall=sall, bp_rows=bp_rows, bp_p=bp_p,
                 ys=ys, mg_rows=mg_rows, merged=merged, nw=nw)
        saved.append(s)
        x = x_new
        if l + 1 < DEPTH:
            win[l + 1], wglu[l + 1], wbr[l + 1], wout[l + 1] = gather_wait(handles, big_axis, x, name=f"ag_wait_{l + 1}")

    fnw = w['final_norm_w'].reshape(1, 1, D_MODEL)
    ones = jnp.ones((t, 1), f32)
    dx, d_fnw, loss_sum = rowwise_vjp(loss_fn, [(x, 0, D_MODEL), (tgt, 0, D_MODEL)], [fnw], [], [(ones, 0, 1)],
                                      [(0, f32)], name="loss_head", sum_primal=0)
    loss = lax.psum(loss_sum.reshape(()), ("x", "y", "c"))

    small_g = {k: [None] * DEPTH for k in SMALL if k != 'final_norm_w'}
    own_sums, in_flight = [None] * DEPTH, [None] * DEPTH
    d_lbs = [None] * DEPTH
    token = None
    for l in reversed(range(DEPTH)):
        s = saved[l]
        z = s['z']
        dxb = dx.astype(bf16)
        d_merged = mm(dxb, wout[l], bt=True, after=token, name="mm_out_dx", out_dtype=bf16, tn=2048)
        d_wout = mm(s['merged'], dxb, at=True, name="mm_out_dw", out_dtype=bf16)
        mg = rowwise_vjp(merge_fn, s['mg_rows'], [], [], [(d_merged, 0, D_MODEL)],
                         [(n, bf16) for n in range(2 * N_BRANCH)], name="merge_bwd", ncol=2)
        d_br, d_gl = mg[:N_BRANCH], mg[N_BRANCH:]
        d_ys = [mm(d_br[n], wbr[l][n], bt=True, name="mm_branch_dx", out_dtype=bf16) for n in range(N_BRANCH)]
        d_wbr = jnp.stack([mm(s['ys'][n], d_br[n], at=True, name="mm_branch_dw", out_dtype=bf16) for n in range(N_BRANCH)])
        d_hb, d_gb, d_oc, d_gc, d_hnw = rowwise_vjp(
            branch_prep_fn, s['bp_rows'], s['bp_p'], [], [(d_ys[1], 0, W_MIX), (d_ys[2], 0, W_MIX)],
            [(0, f32), (1, bf16), (2, f32), (3, bf16)], name="branch_prep_bwd", ncol=NCH, tm=TM_CHUNK)
        small_g['hg_norm_w'][l] = d_hnw.reshape(W_MIX)
        d_qs, d_kk, d_gcum, d_i = hg_chunk_bwd(d_oc, s['qs'], s['kk'], s['gcum'], z, s['sall'])
        d_q, d_f, d_lb = rowwise_vjp(
            hg_pre_fn, s['hg_rows'], s['hg_p'], [tri, tri_t], [(d_qs, 0, W_MIX), (d_kk, 0, W_MIX), (d_gcum, 0, W_MIX)],
            [(0, bf16), (1, bf16)], name="hg_pre_bwd", ncol=NCH, tm=TM_CHUNK)
        d_lbs[l] = d_lb.reshape(1, W_MIX)
        d_xb, d_cw, d_cb, d_wa, d_ba, d_wx, d_bx, d_lam = rg_bwd(d_hb, z, s['xc'], s['ra'], s['hb'], s['cw'], s['rg_p'])
        small_g['rg_w_a'][l], small_g['rg_w_x'][l] = _rg_undense(d_wa), _rg_undense(d_wx)
        small_g['rg_b_a'][l], small_g['rg_b_x'][l] = d_ba.reshape(W_MIX), d_bx.reshape(W_MIX)
        small_g['rg_lambda'][l] = d_lam.reshape(W_MIX)
        small_g['rg_conv_w'][l] = d_cw.transpose(1, 0, 2).reshape(RG_CONV, W_MIX)
        small_g['rg_conv_b'][l] = d_cb.reshape(W_MIX)
        d_yssm, d_u1, d_ga, d_d, d_wglu, d_bglu = rowwise_vjp(
            s5_post_fn, s['s5post_rows'], s['s5post_p'], [], [(d_ys[0], 0, W_MIX)],
            [(0, bf16), (1, bf16), (2, bf16)], name="s5_post_bwd", tm=512)
        small_g['s5_d'][l], small_g['s5_b_glu'][l] = d_d.reshape(W_MIX), d_bglu.reshape(W_MIX)
        d_ua, d_bdre, d_bdim, d_cdre, d_cdim, d_are, d_aim = s5_scan_bwd(d_yssm, d_u1, z, s['xre'], s['xim'], *s['s5'])
        small_g['s5_c_re'][l], small_g['s5_c_im'][l] = _s5_c_undense(d_cdre), _s5_c_undense(d_cdim)
        gp = (S5_GROUPS, S5_STATE, 1)
        s5g = whole_vjp(s5_prep_fn, s['s5p'],
                        [d_are.reshape(gp), d_aim.reshape(gp), _s5_b_undense(d_bdre), _s5_b_undense(d_bdim)],
                        name="s5_prep_bwd")
        small_g['s5_lambda_re'][l] = s5g[0].reshape(S5_GROUPS, S5_STATE)
        small_g['s5_lambda_im'][l] = s5g[1].reshape(S5_GROUPS, S5_STATE)
        small_g['s5_log_step'][l] = s5g[2].reshape(S5_GROUPS)
        small_g['s5_b_re'][l], small_g['s5_b_im'][l] = s5g[3], s5g[4]
        dz = jnp.concatenate([d_ua, d_ga, d_xb, d_gb, d_q, d_f, d_i, d_gc] + list(d_gl), axis=1)
        d_win = mm(s['h'], dz, at=True, name="mm_in_dw", out_dtype=bf16)
        grads = (d_win, d_wglu[0].astype(bf16), d_wbr, d_wout)
        if l == 0:
            sums = [pair_sums(g, ax, name="rs_" + k) for g, ax, k in zip(grads, big_axis, big)]
            own_sums[l] = [own for own, _ in sums]
            in_flight[l], token = send_pairs_start([pair for _, pair in sums], s5g[0], name=f"rs_start_{l}")
        else:
            own_sums[l] = [own_slab(g, ax, name="rs_own_" + k) for g, ax, k in zip(grads, big_axis, big)]
            in_flight[l], token = send_pairs_start(grads, s5g[0], name=f"rs_start_{l}", direct_axes=big_axis)
        d_h = mm(dz, win[l], bt=True, after=token, name="mm_in_dx", tk=N_IN // 4)
        dx, d_nw = rowwise_vjp(ln_res_fn, [(s['x'], 0, D_MODEL)], [s['nw']], [], [(d_h, 0, D_MODEL), (dx, 0, D_MODEL)],
                               [(0, f32)], name="ln_bwd", tm=512)
        small_g['norm_w'][l] = d_nw.reshape(D_MODEL)
    d_lb_raw = whole_vjp(lb_prep_fn, lb_rows, d_lbs, name="lb_prep_bwd")
    small_g['hg_lower_bounds'] = [r.reshape(W_MIX) for r in d_lb_raw]

    per_layer = [k for k in SMALL if k != 'final_norm_w']
    shapes = [(DEPTH,) + small_g[k][0].shape for k in per_layer] + [(D_MODEL,)]
    pieces = [small_g[k][l] for k in per_layer for l in range(DEPTH)] + [d_fnw]
    small_in_flight, small_token = gather_start([_pack(pieces)[0].astype(bf16)], (0,), dx, name="ag_small_start")
    res = {}

    arrived = [send_pairs_wait(in_flight[l], small_token, name=f"rs_wait_{l}", direct_axes=None if l == 0 else big_axis)
               for l in range(DEPTH)]
    for i, (k, tr) in enumerate((('w_in', 32), ('s5_w_glu', 32), ('w_branch', 128), ('w_out', 32))):
        shp = w[k].shape
        r3 = lambda a: a.reshape(DEPTH, -1, shp[-1])
        slots = [[own_sums[l][i], arrived[l][i].reshape(arrived[l][i].shape[0], -1, shp[-1])] for l in range(DEPTH)]
        outs = adamw(r3(w[k]), r3(m[k]), r3(v[k]), slots, name="adamw_" + k, tr=tr)
        for kind, buf in zip(('grad', 'delta', 'new_m', 'new_v'), outs):
            res[kind + '_' + k] = buf.reshape(shp)

    (g_all,) = gather_wait(small_in_flight, (0,), outs[0], name="ag_small_wait")
    g_all = g_all.reshape(N_DEV, -1, LANE)

    def local(d, k):
        return jnp.zeros(shapes[SMALL.index(k)], f32) if k == 'rg_conv_w' else d[k]
    packed = [_pack([local(d, k) for k in SMALL]) for d in (w, m, v)]
    outs = adamw(*packed, [[g_all]], name="adamw_small", tr=512)
    for kind, buf in zip(('grad', 'delta', 'new_m', 'new_v'), outs):
        for k, a in zip(SMALL, _unpack(buf, shapes)):
            res[kind + '_' + k] = a
    g_cw = lax.dynamic_slice_in_dim(res['grad_rg_conv_w'], me * LANE, LANE, axis=2)
    cw3 = lambda a: a.reshape(1, DEPTH * RG_CONV, LANE)
    outs = adamw(cw3(w['rg_conv_w']), cw3(m['rg_conv_w']), cw3(v['rg_conv_w']), [[cw3(g_cw)]], name="adamw_conv_w", tr=16)
    for kind, buf in zip(('grad', 'delta', 'new_m', 'new_v'), outs):
        res[kind + '_rg_conv_w'] = buf.reshape(DEPTH, RG_CONV, LANE)

    return (loss, dx[None]) + tuple(res[kind + '_' + k] for kind in ('grad', 'delta', 'new_m', 'new_v') for k in WEIGHTS)


def kernel(x, norm_w, w_in, s5_lambda_re, s5_lambda_im, s5_log_step, s5_b_re, s5_b_im, s5_c_re, s5_c_im, s5_d, s5_w_glu, s5_b_glu, rg_conv_w, rg_conv_b, rg_w_a, rg_b_a, rg_w_x, rg_b_x, rg_lambda, hg_lower_bounds, hg_norm_w, w_branch, w_out, final_norm_w, loss_target, m_norm_w, m_w_in, m_s5_lambda_re, m_s5_lambda_im, m_s5_log_step, m_s5_b_re, m_s5_b_im, m_s5_c_re, m_s5_c_im, m_s5_d, m_s5_w_glu, m_s5_b_glu, m_rg_conv_w, m_rg_conv_b, m_rg_w_a, m_rg_b_a, m_rg_w_x, m_rg_b_x, m_rg_lambda, m_hg_lower_bounds, m_hg_norm_w, m_w_branch, m_w_out, m_final_norm_w, v_norm_w, v_w_in, v_s5_lambda_re, v_s5_lambda_im, v_s5_log_step, v_s5_b_re, v_s5_b_im, v_s5_c_re, v_s5_c_im, v_s5_d, v_s5_w_glu, v_s5_b_glu, v_rg_conv_w, v_rg_conv_b, v_rg_w_a, v_rg_b_a, v_rg_w_x, v_rg_b_x, v_rg_lambda, v_hg_lower_bounds, v_hg_norm_w, v_w_branch, v_w_out, v_final_norm_w):
    w = dict(zip(WEIGHTS, (norm_w, w_in, s5_lambda_re, s5_lambda_im, s5_log_step, s5_b_re, s5_b_im, s5_c_re, s5_c_im, s5_d, s5_w_glu, s5_b_glu, rg_conv_w, rg_conv_b, rg_w_a, rg_b_a, rg_w_x, rg_b_x, rg_lambda, hg_lower_bounds, hg_norm_w, w_branch, w_out, final_norm_w)))
    m = dict(zip(WEIGHTS, (m_norm_w, m_w_in, m_s5_lambda_re, m_s5_lambda_im, m_s5_log_step, m_s5_b_re, m_s5_b_im, m_s5_c_re, m_s5_c_im, m_s5_d, m_s5_w_glu, m_s5_b_glu, m_rg_conv_w, m_rg_conv_b, m_rg_w_a, m_rg_b_a, m_rg_w_x, m_rg_b_x, m_rg_lambda, m_hg_lower_bounds, m_hg_norm_w, m_w_branch, m_w_out, m_final_norm_w)))
    v = dict(zip(WEIGHTS, (v_norm_w, v_w_in, v_s5_lambda_re, v_s5_lambda_im, v_s5_log_step, v_s5_b_re, v_s5_b_im, v_s5_c_re, v_s5_c_im, v_s5_d, v_s5_w_glu, v_s5_b_glu, v_rg_conv_w, v_rg_conv_b, v_rg_w_a, v_rg_b_a, v_rg_w_x, v_rg_b_x, v_rg_lambda, v_hg_lower_bounds, v_hg_norm_w, v_w_branch, v_w_out, v_final_norm_w)))
    return _step(x[0], loss_target[0], w, m, v)
```

```python
import functools
import math

import jax
import jax.numpy as jnp
from jax import lax
from jax.experimental import pallas as pl
from jax.experimental.pallas import tpu as pltpu

f32 = jnp.float32
bf16 = jnp.bfloat16

D_MODEL = 2048
W_MIX = 1024
DEPTH = 4
N_BRANCH = 3
N_IN = 8 * W_MIX + N_BRANCH * D_MODEL
S5_GROUPS, S5_STATE, S5_GROUP = 64, 64, 16
RG_BLOCKS, RG_BLOCK, RG_CONV, RG_C = 16, 64, 4, 8.0
HG_HEADS, HG_DK = 8, 128
HG_SUB = 16
EPS = 1e-6
ADAM_LR, ADAM_B1, ADAM_B2, ADAM_EPS, ADAM_WD, ADAM_STEP = 0.001, 0.9, 0.999, 1e-08, 0.01, 10

N_DEV = 8
LANE = 128
NCH = W_MIX // LANE
TM_CHUNK = 1024
VMEM_LIMIT = 56 * 1024 * 1024
MESH = pl.DeviceIdType.MESH
ANY = pl.BlockSpec(memory_space=pl.ANY)
HIGHEST = lax.Precision.HIGHEST

C_UA, C_GA, C_XB, C_GB, C_Q, C_F, C_I, C_GC, C_GATE = (W_MIX * k for k in range(9))


def _cparams(sem=None):
    return pltpu.CompilerParams(dimension_semantics=sem, vmem_limit_bytes=VMEM_LIMIT)


@jax.custom_vjp
def bdot(a, w):
    return jnp.dot(a.astype(bf16), w.astype(bf16), preferred_element_type=f32)


def _bdot_fwd(a, w):
    return bdot(a, w), (a, w)


def _bdot_bwd(res, g):
    a, w = res
    gb = g.astype(bf16)
    da = lax.dot_general(gb, w.astype(bf16), (((1,), (1,)), ((), ())), preferred_element_type=f32)
    dw = lax.dot_general(a.astype(bf16), gb, (((0,), (0,)), ((), ())), preferred_element_type=f32)
    return da, dw


bdot.defvjp(_bdot_fwd, _bdot_bwd)


def _blockmm(c, a):
    n = c.shape[0]
    return jnp.concatenate([jnp.dot(c, a[i:i + n], preferred_element_type=f32, precision=HIGHEST)
                            for i in range(0, a.shape[0], n)], axis=0)


@jax.custom_vjp
def cdot(c, ct, a):
    return _blockmm(c, a)


def _cdot_fwd(c, ct, a):
    return cdot(c, ct, a), (c, ct)


def _cdot_bwd(res, g):
    c, ct = res
    return jnp.zeros_like(c), jnp.zeros_like(ct), _blockmm(ct, g)


cdot.defvjp(_cdot_fwd, _cdot_bwd)


def mm(a, b, *, name, out_dtype=f32, add=None, after=None, at=False, bt=False, tm=1024, tn=1024, tk=4096):
    k, m = a.shape if at else a.shape[::-1]
    n = b.shape[0] if bt else b.shape[1]
    tm, tn, tk = min(tm, m), min(tn, n), min(tk, k)
    assert m % tm == 0 and n % tn == 0 and k % tk == 0
    nk = k // tk
    dims = (((0 if at else 1,), (1 if bt else 0,)), ((), ()))

    def body(*refs):
        a_ref, b_ref = refs[:2]
        r_ref = refs[2] if add is not None else None
        o_ref = refs[-1] if nk == 1 else refs[-2]
        part = lax.dot_general(a_ref[...], b_ref[...], dims, preferred_element_type=f32)
        if nk == 1:
            if add is not None:
                part = part + r_ref[...]
            o_ref[...] = part.astype(out_dtype)
            return
        acc_ref = refs[-1]
        kk = pl.program_id(2)

        @pl.when(kk == 0)
        def _():
            acc_ref[...] = part

        @pl.when(kk > 0)
        def _():
            acc_ref[...] = acc_ref[...] + part

        @pl.when(kk == nk - 1)
        def _():
            acc = acc_ref[...]
            if add is not None:
                acc = acc + r_ref[...]
            o_ref[...] = acc.astype(out_dtype)

    b_spec = pl.BlockSpec((tn, tk), lambda i, j, q: (j, q)) if bt else pl.BlockSpec((tk, tn), lambda i, j, q: (q, j))
    a_spec = pl.BlockSpec((tk, tm), lambda i, j, q: (q, i)) if at else pl.BlockSpec((tm, tk), lambda i, j, q: (i, q))
    in_specs = [a_spec, b_spec]
    args = [a, b]
    if add is not None:
        in_specs.append(pl.BlockSpec((tm, tn), lambda i, j, q: (i, j)))
        args.append(add)
    if after is not None:
        in_specs.append(pl.BlockSpec(after.shape, lambda i, j, q: (0, 0)))
        args.append(after)
    return pl.pallas_call(
        body, name=name, grid=(m // tm, n // tn, nk), in_specs=in_specs,
        out_specs=pl.BlockSpec((tm, tn), lambda i, j, q: (i, j)),
        out_shape=jax.ShapeDtypeStruct((m, n), out_dtype),
        scratch_shapes=[] if nk == 1 else [pltpu.VMEM((tm, tn), f32)],
        compiler_params=_cparams(("parallel", "parallel", "arbitrary")),
    )(*args)


def _row_spec(tm, wc, col_off):
    base = col_off // wc
    assert col_off % wc == 0
    return pl.BlockSpec((tm, wc), lambda j, i: (i, base + j))


def _slab_spec(arr):
    r, c = arr.shape[1:]
    if arr.shape[0] == 1:
        return pl.BlockSpec((1, r, c), lambda j, i: (0, 0, 0))
    return pl.BlockSpec((1, r, c), lambda j, i: (j, 0, 0))


def rowwise(fn, rows, params, consts, outs, *, name, tm=256, ncol=1, rowid=False):
    t = rows[0][0].shape[0]
    tm = min(tm, t)
    nr, npar, nc, no = len(rows), len(params), len(consts), len(outs)

    def body(*refs):
        r = [refs[k][...].astype(f32) for k in range(nr)]
        p = [refs[nr + k][0] for k in range(npar + nc)]
        extra = ()
        if rowid:
            extra = (pl.program_id(1) * tm + lax.broadcasted_iota(jnp.int32, (tm, 1), 0),)
        res = fn(*extra, *r, *p)
        for k in range(no):
            refs[nr + npar + nc + k][...] = res[k].astype(outs[k][1])

    in_specs = [_row_spec(tm, w // ncol, off) for (_, off, w) in rows]
    in_specs += [_slab_spec(a) for a in list(params) + list(consts)]
    out_specs = [pl.BlockSpec((tm, w // ncol), lambda j, i: (i, j)) for (w, _) in outs]
    out_shape = [jax.ShapeDtypeStruct((t, w), dt) for (w, dt) in outs]
    return pl.pallas_call(
        body, name=name, grid=(ncol, t // tm), in_specs=in_specs, out_specs=out_specs, out_shape=out_shape,
        compiler_params=_cparams(("parallel", "parallel")),
    )(*[r[0] for r in rows], *params, *consts)


def rowwise_vjp(fn, rows, params, consts, cts, d_rows, *, name, tm=256, ncol=1, rowid=False, sum_primal=None):
    t = rows[0][0].shape[0]
    tm = min(tm, t)
    nr, npar, nc, nct, ndr = len(rows), len(params), len(consts), len(cts), len(d_rows)

    def body(*refs):
        i = pl.program_id(1)
        r = [refs[k][...].astype(f32) for k in range(nr)]
        p = [refs[nr + k][0] for k in range(npar)]
        c = [refs[nr + npar + k][0] for k in range(nc)]
        g = [refs[nr + npar + nc + k][...].astype(f32) for k in range(nct)]
        orefs = refs[nr + npar + nc + nct:]
        extra = ()
        if rowid:
            extra = (i * tm + lax.broadcasted_iota(jnp.int32, (tm, 1), 0),)
        res, vjp = jax.vjp(lambda *v: fn(*extra, *v, *c), *r, *p)
        grads = vjp(tuple(g))
        for k, (idx, dt) in enumerate(d_rows):
            orefs[k][...] = grads[idx].astype(dt)
        acc = [grads[nr + k] for k in range(npar)]
        if sum_primal is not None:
            acc.append(jnp.sum(res[sum_primal], axis=0, keepdims=True))

        @pl.when(i == 0)
        def _():
            for k, a in enumerate(acc):
                orefs[ndr + k][0] = a

        @pl.when(i > 0)
        def _():
            for k, a in enumerate(acc):
                orefs[ndr + k][0] = orefs[ndr + k][0] + a

    in_specs = [_row_spec(tm, w // ncol, off) for (_, off, w) in rows]
    in_specs += [_slab_spec(a) for a in list(params) + list(consts)]
    in_specs += [_row_spec(tm, w // ncol, off) for (_, off, w) in cts]
    out_specs = [pl.BlockSpec((tm, rows[idx][2] // ncol), lambda j, i: (i, j)) for (idx, _) in d_rows]
    out_shape = [jax.ShapeDtypeStruct((t, rows[idx][2]), dt) for (idx, dt) in d_rows]
    for a in params:
        out_specs.append(pl.BlockSpec((1,) + a.shape[1:], lambda j, i: (j, 0, 0)))
        out_shape.append(jax.ShapeDtypeStruct(a.shape, f32))
    if sum_primal is not None:
        w = cts[sum_primal][2]
        out_specs.append(pl.BlockSpec((1, 1, w // ncol), lambda j, i: (j, 0, 0)))
        out_shape.append(jax.ShapeDtypeStruct((ncol, 1, w // ncol), f32))
    return pl.pallas_call(
        body, name=name, grid=(ncol, t // tm), in_specs=in_specs, out_specs=out_specs, out_shape=out_shape,
        compiler_params=_cparams(("parallel", "arbitrary")),
    )(*[r[0] for r in rows], *params, *consts, *[c[0] for c in cts])


VM = pl.BlockSpec(memory_space=pltpu.VMEM)


def whole(fn, ins, outs, *, name):
    def body(*refs):
        res = fn(*[r[...] for r in refs[:len(ins)]])
        for k, o in enumerate(refs[len(ins):]):
            o[...] = res[k]
    return pl.pallas_call(body, name=name, in_specs=[VM] * len(ins), out_specs=[VM] * len(outs),
                          out_shape=[jax.ShapeDtypeStruct(s, f32) for s in outs],
                          compiler_params=_cparams())(*ins)


def whole_vjp(fn, ins, cts, *, name):
    n = len(ins)

    def body(*refs):
        _, vjp = jax.vjp(fn, *[r[...] for r in refs[:n]])
        grads = vjp(tuple(r[...] for r in refs[n:n + len(cts)]))
        for k, o in enumerate(refs[n + len(cts):]):
            o[...] = grads[k]
    return pl.pallas_call(body, name=name, in_specs=[VM] * (n + len(cts)), out_specs=[VM] * n,
                          out_shape=[jax.ShapeDtypeStruct(a.shape, f32) for a in ins],
                          compiler_params=_cparams())(*ins, *cts)


def ln_fn(x, w):
    return (x * lax.rsqrt(jnp.mean(x * x, axis=-1, keepdims=True) + EPS) * w,)


def ln_res_fn(x, w):
    return ln_fn(x, w)[0], x


def loss_fn(x, tgt, w):
    y = ln_fn(x, w)[0]
    return (0.5 * jnp.mean(jnp.square(y - tgt), axis=-1, keepdims=True),)


def s5_prep_fn(lam_re, lam_im, log_step, b_re, b_im):
    step = jnp.exp(log_step)
    mag = jnp.exp(lam_re * step)
    ang = lam_im * step
    abar_re = mag * jnp.cos(ang)
    abar_im = mag * jnp.sin(ang)
    num_re = abar_re - 1.0
    num_im = abar_im
    den = lam_re * lam_re + lam_im * lam_im
    coef_re = (num_re * lam_re + num_im * lam_im) / den
    coef_im = (num_im * lam_re - num_re * lam_im) / den
    bbar_re = coef_re * b_re - coef_im * b_im
    bbar_im = coef_re * b_im + coef_im * b_re
    return abar_re, abar_im, bbar_re, bbar_im


def lb_prep_fn(r0, r1, r2, r3):
    m = jnp.maximum(jnp.maximum(r0, r1), jnp.maximum(r2, r3))
    e0, e1, e2, e3 = jnp.exp(r0 - m), jnp.exp(r1 - m), jnp.exp(r2 - m), jnp.exp(r3 - m)
    s = e0 + e1 + e2 + e3
    p0, p1, p2, p3 = e0 / s, e1 / s, e2 / s, e3 / s
    c1 = p0 + p1
    c2 = c1 + p2
    c3 = c2 + p3
    return p0 - p0, c1 - p0, c2 - p0, c3 - p0


def s5_post_fn(yssm, u, ga, d, wglu, bglu):
    y = jax.nn.gelu(yssm + d * u)
    y = y * jax.nn.sigmoid(bdot(y, wglu) + bglu)
    return (y * jax.nn.silu(ga),)


def rg_gate_fn(tglob, xc, wa, ba, wx, bx, lam):
    r = jax.nn.sigmoid(bdot(xc, wa) + ba)
    i = jax.nn.sigmoid(bdot(xc, wx) + bx)
    log_a = -RG_C * r * jax.nn.softplus(-lam)
    a = jnp.exp(log_a)
    mult = jnp.sqrt(-jnp.tanh(log_a) * (a * a + 1.0))
    mult = jnp.where(tglob == 0, 1.0, mult)
    return a, mult * (i * xc)


def hg_pre_fn(q, fl, lb, tri, tri_t):
    f = lb + (1.0 - lb) * jax.nn.sigmoid(fl)
    return jax.nn.silu(q), 1.0 - f, cdot(tri, tri_t, jnp.log(f))


def branch_prep_fn(hb, gb, oc, gc, nw):
    yb = hb * jax.nn.silu(gb)
    on = oc * lax.rsqrt(jnp.mean(oc * oc, axis=-1, keepdims=True) + EPS) * nw
    return yb, on * jax.nn.silu(gc)


def merge_fn(b0, b1, b2, g0, g1, g2):
    return (jax.nn.sigmoid(g0) * b0 + jax.nn.sigmoid(g1) * b1 + jax.nn.sigmoid(g2) * b2,)


S5_TB = 512
S5_SC = 512
S5_PARTS = (slice(0, 256), slice(256, 512))


SEG = 8


def _shift(v, k, pos, period, reverse, fill):
    if reverse:
        return jnp.where(pos < period - k, pltpu.roll(v, v.shape[0] - k, 0), fill)
    return jnp.where(pos >= k, pltpu.roll(v, k, 0), fill)


def _cmul(ar, ai, br, bi):
    return ar * br - ai * bi, ar * bi + ai * br


def _edge_rows(v, first):
    r0 = 0 if first else SEG - 1
    return jnp.concatenate([v[r:r + 1, :] for r in range(r0, v.shape[0], SEG)], axis=0)


def _spread(s):
    return jnp.concatenate([jnp.broadcast_to(s[g:g + 1, :], (SEG, s.shape[1])) for g in range(s.shape[0])], axis=0)


def lti_scan(xr, xi, ar, ai, cr, ci, reverse=False):
    n = xr.shape[0]
    g = n // SEG
    sub = lax.broadcasted_iota(jnp.int32, (n, 1), 0) & (SEG - 1)
    sub8 = lax.broadcasted_iota(jnp.int32, (SEG, 1), 0)
    grow = lax.broadcasted_iota(jnp.int32, (g, 1), 0)
    pr, pi_ = ar, ai
    wr, wi = jnp.broadcast_to(ar, (SEG, ar.shape[1])), jnp.broadcast_to(ai, (SEG, ai.shape[1]))
    k = 1
    while k < SEG:
        tr, ti = _cmul(pr, pi_, _shift(xr, k, sub, SEG, reverse, 0.0), _shift(xi, k, sub, SEG, reverse, 0.0))
        xr, xi = xr + tr, xi + ti
        pr, pi_ = _cmul(pr, pi_, pr, pi_)
        wr, wi = _cmul(wr, wi, _shift(wr, k, sub8, SEG, reverse, 1.0), _shift(wi, k, sub8, SEG, reverse, 0.0))
        k *= 2
    first, last = (g - 1, 0) if reverse else (0, g - 1)
    jr, ji = _cmul(pr, pi_, cr, ci)
    sr = _edge_rows(xr, reverse) + jnp.where(grow == first, jr, 0.0)
    si = _edge_rows(xi, reverse) + jnp.where(grow == first, ji, 0.0)
    k = 1
    while k < g:
        tr, ti = _cmul(pr, pi_, _shift(sr, k, grow, g, reverse, 0.0), _shift(si, k, grow, g, reverse, 0.0))
        sr, si = sr + tr, si + ti
        pr, pi_ = _cmul(pr, pi_, pr, pi_)
        k *= 2
    er, ei = _spread(_shift(sr, 1, grow, g, reverse, cr)), _spread(_shift(si, 1, grow, g, reverse, ci))
    tr, ti = _cmul(jnp.tile(wr, (g, 1)), jnp.tile(wi, (g, 1)), er, ei)
    return xr + tr, xi + ti, sr[last:last + 1, :], si[last:last + 1, :]


def tv_scan(aa, bb, carry, reverse=False):
    n = aa.shape[0]
    row = lax.broadcasted_iota(jnp.int32, (n, 1), 0)
    k = 1
    while k < n:
        bb = bb + aa * _shift(bb, k, row, n, reverse, 0.0)
        aa = aa * _shift(aa, k, row, n, reverse, 1.0)
        k *= 2
    h = bb + aa * carry
    last = 0 if reverse else n - 1
    return h, h[last:last + 1, :]


def s5_scan_fwd(z, bd_re, bd_im, cd_re, cd_im, a_re, a_im):
    t = z.shape[0]
    tb = min(S5_TB, t)

    def body(u_ref, bre, bim, cre, cim, are, aim, y_ref, xre_ref, xim_ref, car_re, car_im):
        @pl.when(pl.program_id(1) == 0)
        def _():
            car_re[...] = jnp.zeros_like(car_re)
            car_im[...] = jnp.zeros_like(car_im)

        u = u_ref[...].astype(bf16)
        y = None
        for sl in S5_PARTS:
            xr, xi, car_re[:, sl], car_im[:, sl] = lti_scan(
                jnp.dot(u, bre[0, :, sl], preferred_element_type=f32), jnp.dot(u, bim[0, :, sl], preferred_element_type=f32),
                are[0, :, sl], aim[0, :, sl], car_re[:, sl], car_im[:, sl])
            xre_ref[:, sl] = xr
            xim_ref[:, sl] = xi
            part = (jnp.dot(xr.astype(bf16), cre[0, sl, :], preferred_element_type=f32)
                    - jnp.dot(xi.astype(bf16), cim[0, sl, :], preferred_element_type=f32))
            y = part if y is None else y + part
        y_ref[...] = y

    chunk = lambda r, c: pl.BlockSpec((1, r, c), lambda j, i: (j, 0, 0))
    return pl.pallas_call(
        body, name="s5_scan_fwd", grid=(NCH, t // tb),
        in_specs=[pl.BlockSpec((tb, LANE), lambda j, i: (i, C_UA // LANE + j)),
                  chunk(LANE, S5_SC), chunk(LANE, S5_SC), chunk(S5_SC, LANE), chunk(S5_SC, LANE),
                  chunk(1, S5_SC), chunk(1, S5_SC)],
        out_specs=[pl.BlockSpec((tb, LANE), lambda j, i: (i, j)),
                   pl.BlockSpec((tb, S5_SC), lambda j, i: (i, j)),
                   pl.BlockSpec((tb, S5_SC), lambda j, i: (i, j))],
        out_shape=[jax.ShapeDtypeStruct((t, W_MIX), f32),
                   jax.ShapeDtypeStruct((t, NCH * S5_SC), f32),
                   jax.ShapeDtypeStruct((t, NCH * S5_SC), f32)],
        scratch_shapes=[pltpu.VMEM((1, S5_SC), f32)] * 2,
        compiler_params=_cparams(("parallel", "arbitrary")),
    )(z, bd_re.astype(bf16), bd_im.astype(bf16), cd_re.astype(bf16), cd_im.astype(bf16), a_re, a_im)


def s5_scan_bwd(dy, du1, z, xre, xim, bd_re, bd_im, cd_re, cd_im, a_re, a_im):
    t = z.shape[0]
    tb = min(S5_TB, t)
    nt = t // tb

    def body(dy_ref, du1_ref, u_ref, xre_ref, xim_ref, hre_ref, him_ref, bre, bim, cre, cim, are, aim,
             du_ref, dbre, dbim, dcre, dcim, dare, daim, car_re, car_im):
        step = pl.program_id(1)
        tt = nt - 1 - step

        @pl.when(step == 0)
        def _():
            car_re[...] = jnp.zeros_like(car_re)
            car_im[...] = jnp.zeros_like(car_im)

        nt_dims = (((1,), (1,)), ((), ()))
        tn_dims = (((0,), (0,)), ((), ()))
        dyb = dy_ref[...].astype(bf16)
        row = lax.broadcasted_iota(jnp.int32, (tb, 1), 0)
        ub = u_ref[...].astype(bf16)
        live = (tt > 0).astype(f32)
        du = du1_ref[...].astype(f32)
        for sl in S5_PARTS:
            xr, xi = xre_ref[:, sl], xim_ref[:, sl]
            ar, ai = are[0, :, sl], aim[0, :, sl]
            lr, li, car_re[:, sl], car_im[:, sl] = lti_scan(
                lax.dot_general(dyb, cre[0, sl, :], nt_dims, preferred_element_type=f32),
                -lax.dot_general(dyb, cim[0, sl, :], nt_dims, preferred_element_type=f32),
                ar, -ai, car_re[:, sl], car_im[:, sl], reverse=True)
            lrb, lib = lr.astype(bf16), li.astype(bf16)
            du = (du + lax.dot_general(lrb, bre[0, :, sl], nt_dims, preferred_element_type=f32)
                  + lax.dot_general(lib, bim[0, :, sl], nt_dims, preferred_element_type=f32))
            xpr = jnp.where(row == 0, hre_ref[7:8, sl] * live, pltpu.roll(xr, 1, 0))
            xpi = jnp.where(row == 0, him_ref[7:8, sl] * live, pltpu.roll(xi, 1, 0))
            acc = [
                (dbre, (0, slice(None), sl), lax.dot_general(ub, lrb, tn_dims, preferred_element_type=f32)),
                (dbim, (0, slice(None), sl), lax.dot_general(ub, lib, tn_dims, preferred_element_type=f32)),
                (dcre, (0, sl, slice(None)), lax.dot_general(xr.astype(bf16), dyb, tn_dims, preferred_element_type=f32)),
                (dcim, (0, sl, slice(None)), -lax.dot_general(xi.astype(bf16), dyb, tn_dims, preferred_element_type=f32)),
                (dare, (0, slice(None), sl), jnp.sum(lr * xpr + li * xpi, axis=0, keepdims=True)),
                (daim, (0, slice(None), sl), jnp.sum(li * xpr - lr * xpi, axis=0, keepdims=True)),
            ]

            @pl.when(step == 0)
            def _(acc=acc):
                for o, at, a in acc:
                    o[at] = a

            @pl.when(step > 0)
            def _(acc=acc):
                for o, at, a in acc:
                    o[at] = o[at] + a
        du_ref[...] = du.astype(du_ref.dtype)

    chunk = lambda r, c: pl.BlockSpec((1, r, c), lambda j, i: (j, 0, 0))
    rev = lambda w, base=0: pl.BlockSpec((tb, w), lambda j, i: (nt - 1 - i, base + j))
    halo = pl.BlockSpec((8, S5_SC), lambda j, i: (jnp.maximum((nt - 1 - i) * (tb // 8) - 1, 0), j))
    return pl.pallas_call(
        body, name="s5_scan_bwd", grid=(NCH, nt),
        in_specs=[rev(LANE), rev(LANE), rev(LANE, C_UA // LANE), rev(S5_SC), rev(S5_SC), halo, halo,
                  chunk(LANE, S5_SC), chunk(LANE, S5_SC), chunk(S5_SC, LANE), chunk(S5_SC, LANE),
                  chunk(1, S5_SC), chunk(1, S5_SC)],
        out_specs=[rev(LANE), chunk(LANE, S5_SC), chunk(LANE, S5_SC), chunk(S5_SC, LANE), chunk(S5_SC, LANE),
                   chunk(1, S5_SC), chunk(1, S5_SC)],
        out_shape=[jax.ShapeDtypeStruct((t, W_MIX), bf16),
                   jax.ShapeDtypeStruct((NCH, LANE, S5_SC), f32), jax.ShapeDtypeStruct((NCH, LANE, S5_SC), f32),
                   jax.ShapeDtypeStruct((NCH, S5_SC, LANE), f32), jax.ShapeDtypeStruct((NCH, S5_SC, LANE), f32),
                   jax.ShapeDtypeStruct((NCH, 1, S5_SC), f32), jax.ShapeDtypeStruct((NCH, 1, S5_SC), f32)],
        scratch_shapes=[pltpu.VMEM((1, S5_SC), f32)] * 2,
        compiler_params=_cparams(("parallel", "arbitrary")),
    )(dy, du1, z, xre, xim, xre, xim, bd_re.astype(bf16), bd_im.astype(bf16), cd_re.astype(bf16),
      cd_im.astype(bf16), a_re, a_im)


RG_TB = 512


def _rg_conv(ext, w, cb, tb):
    acc = cb + w[3:4, :] * ext[8:, :]
    for k in range(3):
        acc = acc + w[k:k + 1, :] * pltpu.roll(ext, 3 - k, 0)[8:, :]
    return acc


def rg_fwd(z, cw, cb, gate_p):
    t = z.shape[0]
    tb = min(RG_TB, t)

    def body(x_ref, halo_ref, cw_ref, cb_ref, wa, ba, wx, bx, lam, xc_ref, a_ref, h_ref, car):
        i = pl.program_id(1)

        @pl.when(i == 0)
        def _():
            car[...] = jnp.zeros_like(car)

        ext = jnp.concatenate([halo_ref[...] * (i > 0).astype(f32), x_ref[...]], axis=0)
        xc = _rg_conv(ext, cw_ref[0], cb_ref[0], tb)
        tglob = i * tb + lax.broadcasted_iota(jnp.int32, (tb, 1), 0)
        a, b = rg_gate_fn(tglob, xc, wa[0], ba[0], wx[0], bx[0], lam[0])
        xc_ref[...] = xc
        a_ref[...] = a
        h_ref[...], car[...] = tv_scan(a, b, car[...])

    base = C_XB // LANE
    out = pl.BlockSpec((tb, LANE), lambda j, i: (i, j))
    return pl.pallas_call(
        body, name="rg_fwd", grid=(NCH, t // tb),
        in_specs=[pl.BlockSpec((tb, LANE), lambda j, i: (i, base + j)),
                  pl.BlockSpec((8, LANE), lambda j, i: (jnp.maximum(i * (tb // 8) - 1, 0), base + j)),
                  _slab_spec(cw), _slab_spec(cb)] + [_slab_spec(p) for p in gate_p],
        out_specs=[out, out, out], out_shape=[jax.ShapeDtypeStruct((t, W_MIX), f32)] * 3,
        scratch_shapes=[pltpu.VMEM((1, LANE), f32)],
        compiler_params=_cparams(("parallel", "arbitrary")),
    )(z, z, cw, cb, *gate_p)


def rg_bwd(dh, z, xc, a, h, cw, gate_p):
    t = z.shape[0]
    tb = min(RG_TB, t)
    nt = t // tb

    def body(g_ref, a_ref, an_ref, h_ref, hp_ref, xc_ref, x_ref, xh_ref, cw_ref, wa, ba, wx, bx, lam,
             dx_ref, dcw_ref, dcb_ref, dwa, dba, dwx, dbx, dlam, car, later):
        step = pl.program_id(1)
        tt = nt - 1 - step

        @pl.when(step == 0)
        def _():
            car[...] = jnp.zeros_like(car)
            later[...] = jnp.zeros_like(later)

        row = lax.broadcasted_iota(jnp.int32, (tb, 1), 0)
        an = an_ref[0:1, :] * (tt < nt - 1).astype(f32)
        aa = jnp.where(row == tb - 1, an, pltpu.roll(a_ref[...], tb - 1, 0))
        lmb, car[...] = tv_scan(aa, g_ref[...], car[...], reverse=True)
        hp = jnp.where(row == 0, hp_ref[7:8, :] * (tt > 0).astype(f32), pltpu.roll(h_ref[...], 1, 0))
        tglob = tt * tb + row
        _, vjp = jax.vjp(lambda *v: rg_gate_fn(tglob, *v), xc_ref[...], wa[0], ba[0], wx[0], bx[0], lam[0])
        g, *dp = vjp((lmb * hp, lmb))
        gext = jnp.concatenate([g, later[...]], axis=0)
        later[...] = g[0:8, :]
        xext = jnp.concatenate([xh_ref[...] * (tt > 0).astype(f32), x_ref[...]], axis=0)
        w = cw_ref[0]
        dx = w[3:4, :] * g
        taps = [None] * RG_CONV
        taps[3] = jnp.sum(g * xext[8:, :], axis=0, keepdims=True)
        for k in range(3):
            s = 3 - k
            dx = dx + w[k:k + 1, :] * pltpu.roll(gext, tb + 8 - s, 0)[:tb, :]
            taps[k] = jnp.sum(g * pltpu.roll(xext, s, 0)[8:, :], axis=0, keepdims=True)
        dx_ref[...] = dx.astype(dx_ref.dtype)
        acc = [jnp.concatenate(taps, axis=0), jnp.sum(g, axis=0, keepdims=True)] + dp
        outs = [dcw_ref, dcb_ref, dwa, dba, dwx, dbx, dlam]

        @pl.when(step == 0)
        def _():
            for o, v in zip(outs, acc):
                o[0] = v

        @pl.when(step > 0)
        def _():
            for o, v in zip(outs, acc):
                o[0] = o[0] + v

    base = C_XB // LANE
    rev = lambda b=0: pl.BlockSpec((tb, LANE), lambda j, i: (nt - 1 - i, b + j))
    nxt = pl.BlockSpec((8, LANE), lambda j, i: (jnp.minimum((nt - i) * (tb // 8), t // 8 - 1), j))
    prv = lambda b=0: pl.BlockSpec((8, LANE), lambda j, i: (jnp.maximum((nt - 1 - i) * (tb // 8) - 1, 0), b + j))
    params = [cw] + list(gate_p)
    grads = [jax.ShapeDtypeStruct(s, f32) for s in [cw.shape, (NCH, 1, LANE)] + [p.shape for p in gate_p]]
    return pl.pallas_call(
        body, name="rg_bwd", grid=(NCH, nt),
        in_specs=[rev(), rev(), nxt, rev(), prv(), rev(), rev(base), prv(base)] + [_slab_spec(p) for p in params],
        out_specs=[rev()] + [_slab_spec(p) for p in grads], out_shape=[jax.ShapeDtypeStruct((t, W_MIX), bf16)] + grads,
        scratch_shapes=[pltpu.VMEM((1, LANE), f32), pltpu.VMEM((8, LANE), f32)],
        compiler_params=_cparams(("parallel", "arbitrary")),
    )(dh, a, a, h, h, xc, z, z, *params)


HG_TB = 256
HALF = HG_SUB // 2


def _heads(v):
    return jnp.stack([v[:, LANE * h:LANE * (h + 1)] for h in range(HG_HEADS)])


def _unheads(v):
    return jnp.concatenate([v[h] for h in range(HG_HEADS)], axis=-1)


def _bmm(eq, a, b):
    return jnp.einsum(eq, a.astype(bf16), b.astype(bf16), preferred_element_type=f32)


def hg_chunk_fwd(qs, kk, gcum, z):
    t = qs.shape[0]
    tb = min(HG_TB, t)
    nc = tb // HG_SUB

    def body(q_ref, k_ref, g_ref, v_ref, o_ref, sall_ref, st_ref):
        @pl.when(pl.program_id(0) == 0)
        def _():
            st_ref[...] = jnp.zeros_like(st_ref)

        ri = lax.broadcasted_iota(jnp.int32, (1, HALF, 1), 1)

        def chunk(c, carry):
            rows = pl.ds(pl.multiple_of(c * HG_SUB, HG_SUB), HG_SUB)
            q, k, g, v = _heads(q_ref[rows, :]), _heads(k_ref[rows, :]), _heads(g_ref[rows, :]), _heads(v_ref[rows, :])
            st = st_ref[...]
            sall_ref[c] = st
            o = _bmm('htk,hvk->htv', q * jnp.exp(g), st)
            halves = [[q[:, :HALF], g[:, :HALF], o[:, :HALF]], [q[:, HALF:], g[:, HALF:], o[:, HALF:]]]
            for s in range(HG_SUB):
                grow, krow, vrow = g[:, s:s + 1, :], k[:, s:s + 1, :], v[:, s:s + 1, :]
                for h in range(s // HALF, 2):
                    qh, gh, oh = halves[h]
                    p = jnp.exp(jnp.minimum(gh - grow, 0.0))
                    if s // HALF == h:
                        p = jnp.where(ri >= s - h * HALF, p, 0.0)
                    halves[h][2] = oh + jnp.sum(qh * krow * p, axis=-1, keepdims=True) * vrow
            o = jnp.concatenate([halves[0][2], halves[1][2]], axis=1)
            gl = g[:, HG_SUB - 1:HG_SUB, :]
            st_ref[...] = st * jnp.exp(gl) + _bmm('htv,htk->hvk', v, k * jnp.exp(gl - g))
            o_ref[rows, :] = _unheads(o)
            return carry

        lax.fori_loop(0, nc, chunk, 0)

    spec = lambda base=0: pl.BlockSpec((tb, W_MIX), lambda i: (i, base))
    return pl.pallas_call(
        body, name="hg_chunk_fwd", grid=(t // tb,),
        in_specs=[spec(), spec(), spec(), spec(C_I // W_MIX)],
        out_specs=[spec(), pl.BlockSpec((nc, HG_HEADS, HG_DK, HG_DK), lambda i: (i, 0, 0, 0))],
        out_shape=[jax.ShapeDtypeStruct((t, W_MIX), f32),
                   jax.ShapeDtypeStruct((t // HG_SUB, HG_HEADS, HG_DK, HG_DK), f32)],
        scratch_shapes=[pltpu.VMEM((HG_HEADS, HG_DK, HG_DK), f32)],
        compiler_params=_cparams(("arbitrary",)),
    )(qs, kk, gcum, z)


def hg_chunk_bwd(do, qs, kk, gcum, z, sall):
    t = qs.shape[0]
    tb = min(HG_TB, t)
    nc = tb // HG_SUB
    nt = t // tb

    def body(do_ref, q_ref, k_ref, g_ref, v_ref, sall_ref, dq_ref, dk_ref, dg_ref, dv_ref, dst_ref):
        @pl.when(pl.program_id(0) == 0)
        def _():
            dst_ref[...] = jnp.zeros_like(dst_ref)

        ri = lax.broadcasted_iota(jnp.int32, (1, HALF, 1), 1)
        ri_chunk = lax.broadcasted_iota(jnp.int32, (1, HG_SUB, 1), 1)

        def chunk(cc, carry):
            c = nc - 1 - cc
            rows = pl.ds(pl.multiple_of(c * HG_SUB, HG_SUB), HG_SUB)
            q, k, g, v = _heads(q_ref[rows, :]), _heads(k_ref[rows, :]), _heads(g_ref[rows, :]), _heads(v_ref[rows, :])
            d_o = _heads(do_ref[rows, :])
            st = sall_ref[c]
            dsn = dst_ref[...]
            eg = jnp.exp(g)
            qe = q * eg
            gl = g[:, HG_SUB - 1:HG_SUB, :]
            egl = jnp.exp(gl)
            dec = jnp.exp(gl - g)
            kd = k * dec
            dqe = _bmm('htv,hvk->htk', d_o, st)
            dst_ref[...] = _bmm('htv,htk->hvk', d_o, qe) + dsn * egl
            dgl_dec = jnp.sum(dsn * st, axis=1, keepdims=True) * egl
            dv = _bmm('htk,hvk->htv', kd, dsn)
            dkd = _bmm('htv,hvk->htk', v, dsn)
            tq, tg, tdo = [q[:, :HALF], q[:, HALF:]], [g[:, :HALF], g[:, HALF:]], [d_o[:, :HALF], d_o[:, HALF:]]
            a1 = [jnp.zeros_like(tq[0]), jnp.zeros_like(tq[1])]
            a2 = [jnp.zeros_like(tq[0]), jnp.zeros_like(tq[1])]
            dvh = [dv[:, :HALF], dv[:, HALF:]]
            for s in range(HG_SUB):
                grow, krow, vrow = g[:, s:s + 1, :], k[:, s:s + 1, :], v[:, s:s + 1, :]
                sh, sr = s // HALF, s % HALF
                dv_s, a2_s = 0.0, 0.0
                for h in range(sh, 2):
                    p = jnp.exp(jnp.minimum(tg[h] - grow, 0.0))
                    if sh == h:
                        p = jnp.where(ri >= sr, p, 0.0)
                    col = jnp.sum(tq[h] * krow * p, axis=-1, keepdims=True)
                    t1 = jnp.sum(tdo[h] * vrow, axis=-1, keepdims=True) * p
                    a1[h] = a1[h] + t1 * krow
                    dv_s = dv_s + jnp.sum(col * tdo[h], axis=1, keepdims=True)
                    a2_s = a2_s + jnp.sum(t1 * tq[h], axis=1, keepdims=True)
                dvh[sh] = jnp.where(ri == sr, dvh[sh] + dv_s, dvh[sh])
                a2[sh] = jnp.where(ri == sr, a2_s, a2[sh])
            a1, a2 = jnp.concatenate(a1, axis=1), jnp.concatenate(a2, axis=1)
            dv = jnp.concatenate(dvh, axis=1)
            dgl = jnp.sum(dkd * kd, axis=1, keepdims=True) + dgl_dec
            dg = dqe * qe + q * a1 - k * a2 - dkd * kd
            dg = jnp.where(ri_chunk == HG_SUB - 1, dg + dgl, dg)
            dq_ref[rows, :] = _unheads(dqe * eg + a1)
            dk_ref[rows, :] = _unheads(dkd * dec + a2)
            dg_ref[rows, :] = _unheads(dg)
            dv_ref[rows, :] = _unheads(dv).astype(dv_ref.dtype)
            return carry

        lax.fori_loop(0, nc, chunk, 0)

    spec = lambda base=0: pl.BlockSpec((tb, W_MIX), lambda i: (nt - 1 - i, base))
    return pl.pallas_call(
        body, name="hg_chunk_bwd", grid=(nt,),
        in_specs=[spec(), spec(), spec(), spec(), spec(C_I // W_MIX),
                  pl.BlockSpec((nc, HG_HEADS, HG_DK, HG_DK), lambda i: (nt - 1 - i, 0, 0, 0))],
        out_specs=[spec(), spec(), spec(), spec()],
        out_shape=[jax.ShapeDtypeStruct((t, W_MIX), f32)] * 3 + [jax.ShapeDtypeStruct((t, W_MIX), bf16)],
        scratch_shapes=[pltpu.VMEM((HG_HEADS, HG_DK, HG_DK), f32)],
        compiler_params=_cparams(("arbitrary",)),
    )(do, qs, kk, gcum, z, sall)


def adamw(w, m, v, slots, *, name, tr):
    nl, r, c = w.shape
    tr = min(tr, r)
    flat = [a for per_layer in slots for a in per_layer]
    c1 = 1.0 / (1.0 - ADAM_B1 ** ADAM_STEP)
    c2 = 1.0 / (1.0 - ADAM_B2 ** ADAM_STEP)

    def body(*refs):
        w_ref, m_ref, v_ref = refs[:3]
        s_refs = list(refs[3:3 + len(flat)])
        g_ref, d_ref, mo_ref, vo_ref = refs[3 + len(flat):]
        for l in range(nl):
            parts = [s_refs.pop(0) for _ in slots[l]]
            g = None
            for p in parts:
                for s in range(p.shape[0]):
                    term = p[s].astype(f32)
                    g = term if g is None else g + term
            mn = ADAM_B1 * m_ref[l] + (1.0 - ADAM_B1) * g
            vn = ADAM_B2 * v_ref[l] + (1.0 - ADAM_B2) * (g * g)
            g_ref[l] = g
            mo_ref[l] = mn
            vo_ref[l] = vn
            d_ref[l] = -ADAM_LR * ((mn * c1) / (jnp.sqrt(vn * c2) + ADAM_EPS) + ADAM_WD * w_ref[l])

    full = pl.BlockSpec((nl, tr, c), lambda i: (0, i, 0))
    slot = [pl.BlockSpec((a.shape[0], tr, c), lambda i: (0, i, 0)) for a in flat]
    return pl.pallas_call(
        body, name=name, grid=(r // tr,), in_specs=[full] * 3 + slot, out_specs=[full] * 4,
        out_shape=[jax.ShapeDtypeStruct(w.shape, f32)] * 4, compiler_params=_cparams(("parallel",)),
    )(w, m, v, *flat)


def _slab(ref, axis, idx, n):
    return ref.at[tuple([slice(None)] * axis + [pl.ds(idx * n, n)])]


def all_gather(x, axis, *, name):
    n = x.shape[axis]
    out_shape = x.shape[:axis] + (N_DEV * n,) + x.shape[axis + 1:]

    def body(x_ref, out_ref, send_sems, recv_sems, local_sem):
        xx, yy, cc = lax.axis_index("x"), lax.axis_index("y"), lax.axis_index("c")
        me, sibling = (xx, yy, cc), (xx, yy, 1 - cc)
        chips = [(1 - xx, yy), (xx, 1 - yy), (1 - xx, 1 - yy)]

        def slab(px, py, pc):
            return _slab(out_ref, axis, 4 * px + 2 * py + pc, n)

        def copy(k, block, to, src=None):
            return pltpu.make_async_remote_copy(
                src_ref=slab(*block) if src is None else src, dst_ref=slab(*block),
                send_sem=send_sems.at[k], recv_sem=recv_sems.at[k], device_id=to, device_id_type=MESH)

        mine = pltpu.make_async_copy(x_ref, slab(*me), local_sem)
        mine.start()
        first = [copy(0, me, sibling, src=x_ref)]
        first += [copy(1 + j, me, (*chip, cc), src=x_ref) for j, chip in enumerate(chips)]
        for cp in first:
            cp.start()
        passed = [copy(4 + j, (*chip, cc), sibling) for j, chip in enumerate(chips)]
        for j, chip in enumerate(chips):
            copy(1 + j, (*chip, cc), me).wait_recv()
            passed[j].start()
        copy(0, sibling, me).wait_recv()
        for j, chip in enumerate(chips):
            copy(4 + j, (*chip, 1 - cc), me).wait_recv()
        for cp in first + passed:
            cp.wait_send()
        mine.wait()

    return pl.pallas_call(
        body, name=name, out_shape=jax.ShapeDtypeStruct(out_shape, x.dtype), in_specs=[ANY], out_specs=ANY,
        scratch_shapes=[pltpu.SemaphoreType.DMA((7,)), pltpu.SemaphoreType.DMA((7,)), pltpu.SemaphoreType.DMA],
    )(x)


N_CHIP = 4


HBM = pl.BlockSpec(memory_space=pltpu.HBM)
SEM = pl.BlockSpec(memory_space=pltpu.SEMAPHORE)
EFFECT = pltpu.SideEffectType.DATAFLOW_SIDE_EFFECTING
TOKEN = jax.ShapeDtypeStruct((8, LANE), f32)


def _in_hbm(a):
    return pltpu.with_memory_space_constraint(a, pltpu.HBM)


def pair_sums(g, axis, *, name):
    n = g.shape[axis] // N_DEV
    slab_shape = g.shape[:axis] + (n,) + g.shape[axis + 1:]
    cols = slab_shape[-1]
    rows = math.prod(slab_shape[:-1])
    col_slabs = axis == g.ndim - 1
    assert col_slabs or (axis == 0 and g.ndim == 2)

    def swap_body(g_ref, got_ref, send_sems, recv_sems):
        xx, yy, cc = lax.axis_index("x"), lax.axis_index("y"), lax.axis_index("c")
        copies = [pltpu.make_async_remote_copy(
            src_ref=_slab(g_ref, axis, 2 * q + 1 - cc, n), dst_ref=got_ref.at[q],
            send_sem=send_sems.at[q], recv_sem=recv_sems.at[q], device_id=(xx, yy, 1 - cc), device_id_type=MESH)
            for q in range(N_CHIP)]
        for cp in copies:
            cp.start()
        for cp in copies:
            cp.wait()

    got = pl.pallas_call(
        swap_body, name=name + "_swap", out_shape=jax.ShapeDtypeStruct((N_CHIP,) + slab_shape, g.dtype),
        in_specs=[ANY], out_specs=ANY, scratch_shapes=[pltpu.SemaphoreType.DMA((N_CHIP,))] * 2,
    )(g)

    tr = min(256, rows)

    def add_body(a0_ref, a1_ref, b_ref, pair_ref, own_ref):
        xx, yy, cc = lax.axis_index("x"), lax.axis_index("y"), lax.axis_index("c")
        mine = jnp.where(cc == 0, a0_ref[...], a1_ref[...])
        s = (mine.astype(f32) + b_ref[0].astype(f32)).astype(bf16)
        pair_ref[0] = s

        @pl.when(pl.program_id(1) == 2 * xx + yy)
        def _():
            own_ref[0] = s

    if col_slabs:
        a_spec = lambda c: pl.BlockSpec((tr, cols), lambda i, q: (i, 2 * q + c))
    else:
        a_spec = lambda c: pl.BlockSpec((tr, cols), lambda i, q: ((2 * q + c) * (n // tr) + i, 0))
    by_chip = pl.BlockSpec((1, tr, cols), lambda i, q: (q, i, 0))
    g2 = g.reshape(-1, g.shape[-1])
    pair, own = pl.pallas_call(
        add_body, name=name + "_add", grid=(rows // tr, N_CHIP), in_specs=[a_spec(0), a_spec(1), by_chip],
        out_specs=[by_chip, pl.BlockSpec((1, tr, cols), lambda i, q: (0, i, 0))],
        out_shape=[jax.ShapeDtypeStruct((N_CHIP, rows, cols), bf16), jax.ShapeDtypeStruct((1, rows, cols), bf16)],
        compiler_params=_cparams(("parallel", "arbitrary")),
    )(g2, g2, got.reshape(N_CHIP, rows, cols))
    return own, pair


def _send_copies(p_refs, land_refs, send_sems, recv_sems):
    xx, yy, cc = lax.axis_index("x"), lax.axis_index("y"), lax.axis_index("c")
    copies = []
    for t, (p, land) in enumerate(zip(p_refs, land_refs)):
        for k in range(1, N_CHIP):
            px = 1 - xx if k & 2 else xx
            py = 1 - yy if k & 1 else yy
            s = (N_CHIP - 1) * t + k - 1
            copies.append(pltpu.make_async_remote_copy(
                src_ref=p.at[2 * px + py], dst_ref=land.at[k - 1], send_sem=send_sems.at[s], recv_sem=recv_sems.at[s],
                device_id=(px, py, cc), device_id_type=MESH))
    return copies


def _direct_copies(axes, g_refs, land_refs, send_sems, recv_sems):
    xx, yy, cc = lax.axis_index("x"), lax.axis_index("y"), lax.axis_index("c")
    copies = []
    for t, (g_ref, land, axis) in enumerate(zip(g_refs, land_refs, axes)):
        n = g_ref.shape[axis] // N_DEV
        for k in range(1, N_DEV):
            px = 1 - xx if k & 4 else xx
            py = 1 - yy if k & 2 else yy
            pc = 1 - cc if k & 1 else cc
            s = (N_DEV - 1) * t + k - 1
            copies.append(pltpu.make_async_remote_copy(
                src_ref=_slab(g_ref, axis, 4 * px + 2 * py + pc, n), dst_ref=land.at[k - 1],
                send_sem=send_sems.at[s], recv_sem=recv_sems.at[s], device_id=(px, py, pc), device_id_type=MESH))
    return copies


def own_slab(g, axis, *, name):
    n = g.shape[axis] // N_DEV
    slab_shape = g.shape[:axis] + (n,) + g.shape[axis + 1:]
    cols, rows = slab_shape[-1], math.prod(slab_shape[:-1])
    tr = min(256, rows)

    def body(g_ref, o_ref):
        o_ref[0] = g_ref[...]

    def where(i):
        me = 4 * lax.axis_index("x") + 2 * lax.axis_index("y") + lax.axis_index("c")
        return (i, me) if axis == g.ndim - 1 else (me * (n // tr) + i, 0)

    return pl.pallas_call(
        body, name=name, grid=(rows // tr,), in_specs=[pl.BlockSpec((tr, cols), where)],
        out_specs=pl.BlockSpec((1, tr, cols), lambda i: (0, i, 0)),
        out_shape=jax.ShapeDtypeStruct((1, rows, cols), g.dtype), compiler_params=_cparams(("parallel",)),
    )(g.reshape(-1, g.shape[-1]))


def send_pairs_start(pairs, after, *, name, direct_axes=None):
    nt = len(pairs)
    if direct_axes is None:
        build = _send_copies
        lands = [lax.empty((N_CHIP - 1,) + p.shape[1:], p.dtype) for p in pairs]
    else:
        build = functools.partial(_direct_copies, direct_axes)
        lands = [lax.empty((N_DEV - 1,) + p.shape[:ax] + (p.shape[ax] // N_DEV,) + p.shape[ax + 1:], p.dtype)
                 for p, ax in zip(pairs, direct_axes)]

    def body(*refs):
        p_refs, land_refs = refs[:nt], refs[nt:2 * nt]
        send_sems, recv_sems = refs[2 * nt + 1], refs[2 * nt + 2]
        token = refs[-1]
        for cp in build(p_refs, land_refs, send_sems, recv_sems):
            cp.start()
        token[...] = jnp.zeros_like(token)

    nsem = sum(l.shape[0] for l in lands)
    outs = pl.pallas_call(
        body, name=name,
        out_shape=(pltpu.SemaphoreType.DMA((nsem,)), pltpu.SemaphoreType.DMA((nsem,)))
        + tuple(pltpu.HBM(a.shape, a.dtype) for a in list(pairs) + lands) + (TOKEN,),
        in_specs=[HBM] * (2 * nt) + [ANY], out_specs=(SEM, SEM) + (HBM,) * (2 * nt) + (VM,),
        input_output_aliases={i: 2 + i for i in range(2 * nt)},
        compiler_params=pltpu.CompilerParams(has_side_effects=EFFECT),
    )(*[_in_hbm(a) for a in list(pairs) + lands], after)
    return outs[:-1], outs[-1]


def send_pairs_wait(handles, after, *, name, direct_axes=None):
    send_sems, recv_sems = handles[0], handles[1]
    bufs = handles[2:]
    nt = len(bufs) // 2
    build = _send_copies if direct_axes is None else functools.partial(_direct_copies, direct_axes)

    def body(*refs):
        p_refs, land_refs = refs[:nt], refs[nt:2 * nt]
        send_sems, recv_sems = refs[2 * nt], refs[2 * nt + 1]
        for cp in build(p_refs, land_refs, send_sems, recv_sems):
            cp.wait_send()
            cp.wait_recv()

    outs = pl.pallas_call(
        body, name=name, out_shape=tuple(pltpu.HBM(a.shape, a.dtype) for a in bufs),
        in_specs=[HBM] * (2 * nt) + [SEM, SEM, ANY], out_specs=(HBM,) * (2 * nt),
        input_output_aliases={i: i for i in range(2 * nt)},
        compiler_params=pltpu.CompilerParams(has_side_effects=EFFECT),
    )(*bufs, send_sems, recv_sems, after)
    return outs[nt:]


def _gather_copies(x_refs, land_refs, axes, send_sems, recv_sems):
    xx, yy, cc = lax.axis_index("x"), lax.axis_index("y"), lax.axis_index("c")
    me = 4 * xx + 2 * yy + cc
    copies = []
    for t, (x_ref, land, axis) in enumerate(zip(x_refs, land_refs, axes)):
        n = x_ref.shape[axis]
        for k in range(1, N_DEV):
            px = 1 - xx if k & 4 else xx
            py = 1 - yy if k & 2 else yy
            pc = 1 - cc if k & 1 else cc
            s = (N_DEV - 1) * t + k - 1
            copies.append(pltpu.make_async_remote_copy(
                src_ref=x_ref, dst_ref=_slab(land, axis, me, n), send_sem=send_sems.at[s], recv_sem=recv_sems.at[s],
                device_id=(px, py, pc), device_id_type=MESH))
    return copies


def _place_own(x, axis, *, name):
    full = x.shape[:axis] + (N_DEV * x.shape[axis],) + x.shape[axis + 1:]
    lead = x.shape[0]
    tile = lead if axis == 0 else (256 if x.ndim == 2 and lead % 256 == 0 else 1 if x.ndim == 3 else lead)
    rest = (0,) * (x.ndim - 1)

    def body(land_ref, x_ref, o_ref):
        o_ref[...] = x_ref[...]

    def where(i):
        me = 4 * lax.axis_index("x") + 2 * lax.axis_index("y") + lax.axis_index("c")
        idx = [i] + list(rest)
        idx[axis] = me
        return tuple(idx)

    return pl.pallas_call(
        body, name=name, grid=(lead // tile,),
        in_specs=[ANY, pl.BlockSpec((tile,) + x.shape[1:], lambda i: (i,) + rest)],
        out_specs=pl.BlockSpec((tile,) + x.shape[1:], where),
        out_shape=jax.ShapeDtypeStruct(full, x.dtype), input_output_aliases={0: 0},
        compiler_params=_cparams(("arbitrary",)),
    )(lax.empty(full, x.dtype), x)


def gather_start(xs, axes, after, *, name):
    nt = len(xs)
    lands = [_place_own(x, axis, name=name + "_own") for x, axis in zip(xs, axes)]

    def body(*refs):
        x_refs, land_refs = refs[:nt], refs[nt:2 * nt]
        send_sems, recv_sems = refs[2 * nt + 1], refs[2 * nt + 2]
        token = refs[-1]
        for cp in _gather_copies(x_refs, land_refs, axes, send_sems, recv_sems):
            cp.start()
        token[...] = jnp.zeros_like(token)

    nsem = (N_DEV - 1) * nt
    outs = pl.pallas_call(
        body, name=name,
        out_shape=(pltpu.SemaphoreType.DMA((nsem,)), pltpu.SemaphoreType.DMA((nsem,)))
        + tuple(pltpu.HBM(a.shape, a.dtype) for a in list(xs) + lands) + (TOKEN,),
        in_specs=[HBM] * (2 * nt) + [ANY], out_specs=(SEM, SEM) + (HBM,) * (2 * nt) + (VM,),
        input_output_aliases={i: 2 + i for i in range(2 * nt)},
        compiler_params=pltpu.CompilerParams(has_side_effects=EFFECT),
    )(*[_in_hbm(a) for a in list(xs) + lands], after)
    return outs[:-1], outs[-1]


def gather_wait(handles, axes, after, *, name):
    send_sems, recv_sems = handles[0], handles[1]
    bufs = handles[2:]
    nt = len(bufs) // 2

    def body(*refs):
        x_refs, land_refs = refs[:nt], refs[nt:2 * nt]
        send_sems, recv_sems = refs[2 * nt], refs[2 * nt + 1]
        for cp in _gather_copies(x_refs, land_refs, axes, send_sems, recv_sems):
            cp.wait_send()
            cp.wait_recv()

    outs = pl.pallas_call(
        body, name=name, out_shape=tuple(pltpu.HBM(a.shape, a.dtype) for a in bufs),
        in_specs=[HBM] * (2 * nt) + [SEM, SEM, ANY], out_specs=(HBM,) * (2 * nt),
        input_output_aliases={i: i for i in range(2 * nt)},
        compiler_params=pltpu.CompilerParams(has_side_effects=EFFECT),
    )(*bufs, send_sems, recv_sems, after)
    return outs[nt:]


def _blockdiag(b, nb):
    j, _, r, c = b.shape
    eye = jnp.eye(nb, dtype=bool)[None, :, None, :, None]
    return jnp.where(eye, b[:, :, :, None, :], jnp.zeros((), b.dtype)).reshape(j, nb * r, nb * c)


def _diagblocks(d, nb):
    j, rr, cc = d.shape
    return jnp.einsum('jarac->jarc', d.reshape(j, nb, rr // nb, nb, cc // nb))


def _s5_b_dense(bbar):
    return _blockdiag(bbar.transpose(0, 2, 1).reshape(NCH, 8, S5_GROUP, S5_STATE), 8)


def _s5_b_undense(d):
    return _diagblocks(d, 8).reshape(S5_GROUPS, S5_GROUP, S5_STATE).transpose(0, 2, 1)


def _s5_c_dense(c):
    return _blockdiag(c.transpose(0, 2, 1).reshape(NCH, 8, S5_STATE, S5_GROUP), 8)


def _s5_c_undense(d):
    return _diagblocks(d, 8).reshape(S5_GROUPS, S5_STATE, S5_GROUP).transpose(0, 2, 1)


def _rg_dense(w):
    return _blockdiag(w.reshape(NCH, 2, RG_BLOCK, RG_BLOCK), 2)


def _rg_undense(d):
    return _diagblocks(d, 2).reshape(RG_BLOCKS, RG_BLOCK, RG_BLOCK)


def _chunks(v):
    return v.reshape(NCH, 1, LANE)


def _tri(tm):
    r = jnp.arange(tm)
    m = (r[:, None] >= r[None, :]) & (r[:, None] // HG_SUB == r[None, :] // HG_SUB)
    m = m.astype(f32)
    return m[None], m.T[None]


SMALL = ['norm_w', 's5_lambda_re', 's5_lambda_im', 's5_log_step', 's5_b_re', 's5_b_im', 's5_c_re', 's5_c_im',
         's5_d', 's5_b_glu', 'rg_conv_w', 'rg_conv_b', 'rg_w_a', 'rg_b_a', 'rg_w_x', 'rg_b_x', 'rg_lambda',
         'hg_lower_bounds', 'hg_norm_w', 'final_norm_w']
WEIGHTS = ['norm_w', 'w_in', 's5_lambda_re', 's5_lambda_im', 's5_log_step', 's5_b_re', 's5_b_im', 's5_c_re',
           's5_c_im', 's5_d', 's5_w_glu', 's5_b_glu', 'rg_conv_w', 'rg_conv_b', 'rg_w_a', 'rg_b_a', 'rg_w_x',
           'rg_b_x', 'rg_lambda', 'hg_lower_bounds', 'hg_norm_w', 'w_branch', 'w_out', 'final_norm_w']
PACK_ROWS = 512


def _pack(arrs):
    flat = jnp.concatenate([a.reshape(-1) for a in arrs])
    pad = (-flat.shape[0]) % (PACK_ROWS * LANE)
    return jnp.pad(flat, (0, pad)).reshape(1, -1, LANE)


def _unpack(buf, shapes):
    flat = buf.reshape(-1)
    out, off = [], 0
    for s in shapes:
        n = math.prod(s)
        out.append(flat[off:off + n].reshape(s))
        off += n
    return out


def _step(x, tgt, w, m, v):
    t = x.shape[0]
    tri, tri_t = _tri(min(LANE, t))
    me = 4 * lax.axis_index("x") + 2 * lax.axis_index("y") + lax.axis_index("c")

    big = ('w_in', 's5_w_glu', 'w_branch', 'w_out')
    big_axis = (1, 0, 2, 0)
    shards = lambda l: [w[k][l].astype(bf16) for k in big]
    win, wglu, wbr, wout = ([None] * DEPTH for _ in range(4))
    win[0] = all_gather(shards(0)[0], big_axis[0], name="ag_w_in")
    rest0_axis = big_axis[1:] + (1,)
    rest0, rest0_token = gather_start(shards(0)[1:] + [w['rg_conv_w'].reshape(DEPTH * RG_CONV, LANE)], rest0_axis,
                                      win[0], name="ag_start_0")

    lb_rows = [w['hg_lower_bounds'][l][None] for l in range(DEPTH)]
    lbs = whole(lb_prep_fn, lb_rows, [(1, W_MIX)] * DEPTH, name="lb_prep")

    saved = []
    for l in range(DEPTH):
        s = {}
        nw = w['norm_w'][l].reshape(1, 1, D_MODEL)
        (h,) = rowwise(ln_fn, [(x, 0, D_MODEL)], [nw], [], [(D_MODEL, bf16)], name="ln_fwd", tm=512)
        token = None
        if l + 1 < DEPTH:
            handles, token = gather_start(shards(l + 1), big_axis, rest0_token if l == 0 else x, name=f"ag_start_{l + 1}")
        z = mm(h, win[l], after=token, name="mm_in", tn=2048)
        if l == 0:
            wglu[0], wbr[0], wout[0], conv_w = gather_wait(rest0, rest0_axis, z, name="ag_wait_0")
            conv_w = conv_w.reshape(DEPTH, RG_CONV, W_MIX)
        s5p = [w['s5_lambda_re'][l][..., None], w['s5_lambda_im'][l][..., None], w['s5_log_step'][l][:, None, None],
               w['s5_b_re'][l], w['s5_b_im'][l]]
        gp = (S5_GROUPS, S5_STATE)
        abar_re, abar_im, bbar_re, bbar_im = whole(
            s5_prep_fn, s5p, [gp + (1,), gp + (1,), gp + (S5_GROUP,), gp + (S5_GROUP,)], name="s5_prep")
        a_re, a_im = abar_re.reshape(NCH, 1, S5_SC), abar_im.reshape(NCH, 1, S5_SC)
        bd_re, bd_im = _s5_b_dense(bbar_re), _s5_b_dense(bbar_im)
        cd_re, cd_im = _s5_c_dense(w['s5_c_re'][l]), _s5_c_dense(w['s5_c_im'][l])
        yssm, xre, xim = s5_scan_fwd(z, bd_re, bd_im, cd_re, cd_im, a_re, a_im)
        s5post_p = [w['s5_d'][l].reshape(1, 1, W_MIX), wglu[l].astype(f32)[None], w['s5_b_glu'][l].reshape(1, 1, W_MIX)]
        s5post_rows = [(yssm, 0, W_MIX), (z, C_UA, W_MIX), (z, C_GA, W_MIX)]
        (ya,) = rowwise(s5_post_fn, s5post_rows, s5post_p, [], [(W_MIX, bf16)], name="s5_post_fwd", tm=512)
        cw, cb = conv_w[l].reshape(RG_CONV, NCH, LANE).transpose(1, 0, 2), _chunks(w['rg_conv_b'][l])
        rg_p = [_rg_dense(w['rg_w_a'][l]), _chunks(w['rg_b_a'][l]), _rg_dense(w['rg_w_x'][l]),
                _chunks(w['rg_b_x'][l]), _chunks(w['rg_lambda'][l])]
        xc, ra, hb = rg_fwd(z, cw, cb, rg_p)
        hg_rows = [(z, C_Q, W_MIX), (z, C_F, W_MIX)]
        hg_p = [_chunks(lbs[l].reshape(W_MIX))]
        qs, kk, gcum = rowwise(hg_pre_fn, hg_rows, hg_p, [tri, tri_t], [(W_MIX, f32)] * 3, name="hg_pre_fwd", ncol=NCH, tm=TM_CHUNK)
        oc, sall = hg_chunk_fwd(qs, kk, gcum, z)
        bp_rows = [(hb, 0, W_MIX), (z, C_GB, W_MIX), (oc, 0, W_MIX), (z, C_GC, W_MIX)]
        bp_p = [_chunks(w['hg_norm_w'][l])]
        yb, yc = rowwise(branch_prep_fn, bp_rows, bp_p, [], [(W_MIX, bf16)] * 2, name="branch_prep_fwd", ncol=NCH, tm=TM_CHUNK)
        ys = [ya, yb, yc]
        br = [mm(ys[n], wbr[l][n], name="mm_branch", out_dtype=bf16, tn=2048) for n in range(N_BRANCH)]
        mg_rows = [(br[n], 0, D_MODEL) for n in range(N_BRANCH)] + [(z, C_GATE + n * D_MODEL, D_MODEL) for n in range(N_BRANCH)]
        (merged,) = rowwise(merge_fn, mg_rows, [], [], [(D_MODEL, bf16)], name="merge_fwd", ncol=2, tm=512)
        x_new = mm(merged, wout[l], add=x, name="mm_out")
        s.update(x=x, h=h, z=z, s5p=s5p, s5=(bd_re, bd_im, cd_re, cd_im, a_re, a_im), xre=xre, xim=xim,
                 s5post_rows=s5post_rows, s5post_p=s5post_p, cw=cw, xc=xc, rg_p=rg_p, ra=ra, hb=hb,
                 hg_rows=hg_rows, hg_p=hg_p, qs=qs, kk=kk, gcum=gcum, sall=sall, bp_rows=bp_rows, bp_p=bp_p,
                 ys=ys, mg_rows=mg_rows, merged=merged, nw=nw)
        saved.append(s)
        x = x_new
        if l + 1 < DEPTH:
            win[l + 1], wglu[l + 1], wbr[l + 1], wout[l + 1] = gather_wait(handles, big_axis, x, name=f"ag_wait_{l + 1}")

    fnw = w['final_norm_w'].reshape(1, 1, D_MODEL)
    ones = jnp.ones((t, 1), f32)
    dx, d_fnw, loss_sum = rowwise_vjp(loss_fn, [(x, 0, D_MODEL), (tgt, 0, D_MODEL)], [fnw], [], [(ones, 0, 1)],
                                      [(0, f32)], name="loss_head", sum_primal=0)
    loss = lax.psum(loss_sum.reshape(()), ("x", "y", "c"))

    small_g = {k: [None] * DEPTH for k in SMALL if k != 'final_norm_w'}
    own_sums, in_flight = [None] * DEPTH, [None] * DEPTH
    d_lbs = [None] * DEPTH
    token = None
    for l in reversed(range(DEPTH)):
        s = saved[l]
        z = s['z']
        dxb = dx.astype(bf16)
        d_merged = mm(dxb, wout[l], bt=True, after=token, name="mm_out_dx", out_dtype=bf16, tn=2048)
        d_wout = mm(s['merged'], dxb, at=True, name="mm_out_dw", out_dtype=bf16)
        mg = rowwise_vjp(merge_fn, s['mg_rows'], [], [], [(d_merged, 0, D_MODEL)],
                         [(n, bf16) for n in range(2 * N_BRANCH)], name="merge_bwd", ncol=2)
        d_br, d_gl = mg[:N_BRANCH], mg[N_BRANCH:]
        d_ys = [mm(d_br[n], wbr[l][n], bt=True, name="mm_branch_dx", out_dtype=bf16) for n in range(N_BRANCH)]
        d_wbr = jnp.stack([mm(s['ys'][n], d_br[n], at=True, name="mm_branch_dw", out_dtype=bf16) for n in range(N_BRANCH)])
        d_hb, d_gb, d_oc, d_gc, d_hnw = rowwise_vjp(
            branch_prep_fn, s['bp_rows'], s['bp_p'], [], [(d_ys[1], 0, W_MIX), (d_ys[2], 0, W_MIX)],
            [(0, f32), (1, bf16), (2, f32), (3, bf16)], name="branch_prep_bwd", ncol=NCH, tm=TM_CHUNK)
        small_g['hg_norm_w'][l] = d_hnw.reshape(W_MIX)
        d_qs, d_kk, d_gcum, d_i = hg_chunk_bwd(d_oc, s['qs'], s['kk'], s['gcum'], z, s['sall'])
        d_q, d_f, d_lb = rowwise_vjp(
            hg_pre_fn, s['hg_rows'], s['hg_p'], [tri, tri_t], [(d_qs, 0, W_MIX), (d_kk, 0, W_MIX), (d_gcum, 0, W_MIX)],
            [(0, bf16), (1, bf16)], name="hg_pre_bwd", ncol=NCH, tm=TM_CHUNK)
        d_lbs[l] = d_lb.reshape(1, W_MIX)
        d_xb, d_cw, d_cb, d_wa, d_ba, d_wx, d_bx, d_lam = rg_bwd(d_hb, z, s['xc'], s['ra'], s['hb'], s['cw'], s['rg_p'])
        small_g['rg_w_a'][l], small_g['rg_w_x'][l] = _rg_undense(d_wa), _rg_undense(d_wx)
        small_g['rg_b_a'][l], small_g['rg_b_x'][l] = d_ba.reshape(W_MIX), d_bx.reshape(W_MIX)
        small_g['rg_lambda'][l] = d_lam.reshape(W_MIX)
        small_g['rg_conv_w'][l] = d_cw.transpose(1, 0, 2).reshape(RG_CONV, W_MIX)
        small_g['rg_conv_b'][l] = d_cb.reshape(W_MIX)
        d_yssm, d_u1, d_ga, d_d, d_wglu, d_bglu = rowwise_vjp(
            s5_post_fn, s['s5post_rows'], s['s5post_p'], [], [(d_ys[0], 0, W_MIX)],
            [(0, bf16), (1, bf16), (2, bf16)], name="s5_post_bwd", tm=512)
        small_g['s5_d'][l], small_g['s5_b_glu'][l] = d_d.reshape(W_MIX), d_bglu.reshape(W_MIX)
        d_ua, d_bdre, d_bdim, d_cdre, d_cdim, d_are, d_aim = s5_scan_bwd(d_yssm, d_u1, z, s['xre'], s['xim'], *s['s5'])
        small_g['s5_c_re'][l], small_g['s5_c_im'][l] = _s5_c_undense(d_cdre), _s5_c_undense(d_cdim)
        gp = (S5_GROUPS, S5_STATE, 1)
        s5g = whole_vjp(s5_prep_fn, s['s5p'],
                        [d_are.reshape(gp), d_aim.reshape(gp), _s5_b_undense(d_bdre), _s5_b_undense(d_bdim)],
                        name="s5_prep_bwd")
        small_g['s5_lambda_re'][l] = s5g[0].reshape(S5_GROUPS, S5_STATE)
        small_g['s5_lambda_im'][l] = s5g[1].reshape(S5_GROUPS, S5_STATE)
        small_g['s5_log_step'][l] = s5g[2].reshape(S5_GROUPS)
        small_g['s5_b_re'][l], small_g['s5_b_im'][l] = s5g[3], s5g[4]
        dz = jnp.concatenate([d_ua, d_ga, d_xb, d_gb, d_q, d_f, d_i, d_gc] + list(d_gl), axis=1)
        d_win = mm(s['h'], dz, at=True, name="mm_in_dw", out_dtype=bf16)
        grads = (d_win, d_wglu[0].astype(bf16), d_wbr, d_wout)
        if l == 0:
            sums = [pair_sums(g, ax, name="rs_" + k) for g, ax, k in zip(grads, big_axis, big)]
            own_sums[l] = [own for own, _ in sums]
            in_flight[l], token = send_pairs_start([pair for _, pair in sums], s5g[0], name=f"rs_start_{l}")
        else:
            own_sums[l] = [own_slab(g, ax, name="rs_own_" + k) for g, ax, k in zip(grads, big_axis, big)]
            in_flight[l], token = send_pairs_start(grads, s5g[0], name=f"rs_start_{l}", direct_axes=big_axis)
        d_h = mm(dz, win[l], bt=True, after=token, name="mm_in_dx", tk=N_IN // 4)
        dx, d_nw = rowwise_vjp(ln_res_fn, [(s['x'], 0, D_MODEL)], [s['nw']], [], [(d_h, 0, D_MODEL), (dx, 0, D_MODEL)],
                               [(0, f32)], name="ln_bwd", tm=512)
        small_g['norm_w'][l] = d_nw.reshape(D_MODEL)
    d_lb_raw = whole_vjp(lb_prep_fn, lb_rows, d_lbs, name="lb_prep_bwd")
    small_g['hg_lower_bounds'] = [r.reshape(W_MIX) for r in d_lb_raw]

    per_layer = [k for k in SMALL if k != 'final_norm_w']
    shapes = [(DEPTH,) + small_g[k][0].shape for k in per_layer] + [(D_MODEL,)]
    pieces = [small_g[k][l] for k in per_layer for l in range(DEPTH)] + [d_fnw]
    small_in_flight, small_token = gather_start([_pack(pieces)[0].astype(bf16)], (0,), dx, name="ag_small_start")
    res = {}

    arrived = [send_pairs_wait(in_flight[l], small_token, name=f"rs_wait_{l}", direct_axes=None if l == 0 else big_axis)
               for l in range(DEPTH)]
    for i, (k, tr) in enumerate((('w_in', 32), ('s5_w_glu', 32), ('w_branch', 128), ('w_out', 32))):
        shp = w[k].shape
        r3 = lambda a: a.reshape(DEPTH, -1, shp[-1])
        slots = [[own_sums[l][i], arrived[l][i].reshape(arrived[l][i].shape[0], -1, shp[-1])] for l in range(DEPTH)]
        outs = adamw(r3(w[k]), r3(m[k]), r3(v[k]), slots, name="adamw_" + k, tr=tr)
        for kind, buf in zip(('grad', 'delta', 'new_m', 'new_v'), outs):
            res[kind + '_' + k] = buf.reshape(shp)

    (g_all,) = gather_wait(small_in_flight, (0,), outs[0], name="ag_small_wait")
    g_all = g_all.reshape(N_DEV, -1, LANE)

    def local(d, k):
        return jnp.zeros(shapes[SMALL.index(k)], f32) if k == 'rg_conv_w' else d[k]
    packed = [_pack([local(d, k) for k in SMALL]) for d in (w, m, v)]
    outs = adamw(*packed, [[g_all]], name="adamw_small", tr=512)
    for kind, buf in zip(('grad', 'delta', 'new_m', 'new_v'), outs):
        for k, a in zip(SMALL, _unpack(buf, shapes)):
            res[kind + '_' + k] = a
    g_cw = lax.dynamic_slice_in_dim(res['grad_rg_conv_w'], me * LANE, LANE, axis=2)
    cw3 = lambda a: a.reshape(1, DEPTH * RG_CONV, LANE)
    outs = adamw(cw3(w['rg_conv_w']), cw3(m['rg_conv_w']), cw3(v['rg_conv_w']), [[cw3(g_cw)]], name="adamw_conv_w", tr=16)
    for kind, buf in zip(('grad', 'delta', 'new_m', 'new_v'), outs):
        res[kind + '_rg_conv_w'] = buf.reshape(DEPTH, RG_CONV, LANE)

    return (loss, dx[None]) + tuple(res[kind + '_' + k] for kind in ('grad', 'delta', 'new_m', 'new_v') for k in WEIGHTS)


def kernel(x, norm_w, w_in, s5_lambda_re, s5_lambda_im, s5_log_step, s5_b_re, s5_b_im, s5_c_re, s5_c_im, s5_d, s5_w_glu, s5_b_glu, rg_conv_w, rg_conv_b, rg_w_a, rg_b_a, rg_w_x, rg_b_x, rg_lambda, hg_lower_bounds, hg_norm_w, w_branch, w_out, final_norm_w, loss_target, m_norm_w, m_w_in, m_s5_lambda_re, m_s5_lambda_im, m_s5_log_step, m_s5_b_re, m_s5_b_im, m_s5_c_re, m_s5_c_im, m_s5_d, m_s5_w_glu, m_s5_b_glu, m_rg_conv_w, m_rg_conv_b, m_rg_w_a, m_rg_b_a, m_rg_w_x, m_rg_b_x, m_rg_lambda, m_hg_lower_bounds, m_hg_norm_w, m_w_branch, m_w_out, m_final_norm_w, v_norm_w, v_w_in, v_s5_lambda_re, v_s5_lambda_im, v_s5_log_step, v_s5_b_re, v_s5_b_im, v_s5_c_re, v_s5_c_im, v_s5_d, v_s5_w_glu, v_s5_b_glu, v_rg_conv_w, v_rg_conv_b, v_rg_w_a, v_rg_b_a, v_rg_w_x, v_rg_b_x, v_rg_lambda, v_hg_lower_bounds, v_hg_norm_w, v_w_branch, v_w_out, v_final_norm_w):
    w = dict(zip(WEIGHTS, (norm_w, w_in, s5_lambda_re, s5_lambda_im, s5_log_step, s5_b_re, s5_b_im, s5_c_re, s5_c_im, s5_d, s5_w_glu, s5_b_glu, rg_conv_w, rg_conv_b, rg_w_a, rg_b_a, rg_w_x, rg_b_x, rg_lambda, hg_lower_bounds, hg_norm_w, w_branch, w_out, final_norm_w)))
    m = dict(zip(WEIGHTS, (m_norm_w, m_w_in, m_s5_lambda_re, m_s5_lambda_im, m_s5_log_step, m_s5_b_re, m_s5_b_im, m_s5_c_re, m_s5_c_im, m_s5_d, m_s5_w_glu, m_s5_b_glu, m_rg_conv_w, m_rg_conv_b, m_rg_w_a, m_rg_b_a, m_rg_w_x, m_rg_b_x, m_rg_lambda, m_hg_lower_bounds, m_hg_norm_w, m_w_branch, m_w_out, m_final_norm_w)))
    v = dict(zip(WEIGHTS, (v_norm_w, v_w_in, v_s5_lambda_re, v_s5_lambda_im, v_s5_log_step, v_s5_b_re, v_s5_b_im, v_s5_c_re, v_s5_c_im, v_s5_d, v_s5_w_glu, v_s5_b_glu, v_rg_conv_w, v_rg_conv_b, v_rg_w_a, v_rg_b_a, v_rg_w_x, v_rg_b_x, v_rg_lambda, v_hg_lower_bounds, v_hg_norm_w, v_w_branch, v_w_out, v_final_norm_w)))
    return _step(x[0], loss_target[0], w, m, v)
```
